```python
import jax, jax.numpy as jnp
from jax import lax
import numpy as np

D_MODEL = 1024
BATCH = 8
SEQ = 8192
DEPTH = 2

CHUNK = 64
PLE_DIM = 256
EPS = 1e-6

D_SGU = 1024
SGU_BLOCK = 128
SGU_HEADS = 8
SGU_HEAD_DIM = D_SGU // SGU_HEADS

D_CONV = 1024
CONV_WIDTH = 31

D_POOL = 1024
POOL_WINDOWS = (2, 4, 8, 16)
POOL_GROUPS = len(POOL_WINDOWS)
POOL_GROUP_DIM = D_POOL // POOL_GROUPS

N_BRANCH = 3
D_IN = 2 * D_SGU + 2 * D_CONV + D_POOL + N_BRANCH * D_MODEL
SPLITS = (2 * D_SGU, 2 * D_SGU + 2 * D_CONV, 2 * D_SGU + 2 * D_CONV + D_POOL)

D_FF = -(-8 * D_MODEL // (3 * 256)) * 256

kernel_name = "hybrid_sgu_conformer_pool_block"


def rms_norm(x, g):
    xf = x.astype(jnp.float32)
    y = xf * lax.rsqrt(jnp.mean(xf * xf, axis=-1, keepdims=True) + EPS)
    return (y * g.astype(jnp.float32)).astype(x.dtype)


def layer_norm(x, g, b):
    xf = x.astype(jnp.float32)
    mu = jnp.mean(xf, axis=-1, keepdims=True)
    xc = xf - mu
    var = jnp.mean(xc * xc, axis=-1, keepdims=True)
    y = xc * lax.rsqrt(var + EPS) * g.astype(jnp.float32) + b.astype(jnp.float32)
    return y.astype(x.dtype)


def sgu_mask():
    c = jnp.arange(SGU_BLOCK) // CHUNK
    return c[None, :] <= c[:, None]


def spatial_gating(z, w_s, b_s, g_v, b_v):
    u, v = jnp.split(z, 2, axis=-1)
    v = layer_norm(v, g_v, b_v)
    bsz, s, _ = v.shape
    nb = s // SGU_BLOCK
    v = v.reshape(bsz, nb, SGU_BLOCK, SGU_HEADS, SGU_HEAD_DIM)
    w = jnp.where(sgu_mask()[None], w_s, jnp.zeros_like(w_s))
    mixed = jnp.einsum('hij,bnjhc->bnihc', w, v) + b_s.T[None, None, :, :, None]
    return u * mixed.reshape(bsz, s, D_SGU)


def conformer_conv(z, w_dw, b_dw, g_ln, b_ln):
    a, gate = jnp.split(z, 2, axis=-1)
    h = a * jax.nn.sigmoid(gate)
    h = lax.conv_general_dilated(
        h, w_dw, window_strides=(1,), padding=((CONV_WIDTH - 1, 0),),
        dimension_numbers=('NWC', 'WIO', 'NWC'),
        feature_group_count=D_CONV) + b_dw
    h = layer_norm(h, g_ln, b_ln)
    return jax.nn.silu(h)


def multiscale_pool(z, w_pool, s_pool):
    bsz, s, _ = z.shape
    zf = z.astype(jnp.float32)
    cs = jnp.cumsum(zf, axis=1)
    t = jnp.arange(1, s + 1, dtype=jnp.float32)
    outs = []
    for gi, w in enumerate(POOL_WINDOWS):
        sl = slice(gi * POOL_GROUP_DIM, (gi + 1) * POOL_GROUP_DIM)
        c = cs[..., sl]
        prev = jnp.pad(c[:, :s - w], ((0, 0), (w, 0), (0, 0)))
        cnt = jnp.minimum(t, float(w))[None, :, None]
        outs.append((c - prev) / cnt - zf[..., sl])
    pooled = jnp.stack(outs, axis=2).astype(z.dtype)
    mixed = jnp.einsum('bsgc,gcd->bsgd', pooled, w_pool)
    return mixed.reshape(bsz, s, D_POOL) * s_pool


def _fwd_setup_inputs(seed: int = 0) -> dict:
    key = jax.random.key(seed)
    ks = jax.random.split(key, 32)
    f32 = jnp.float32

    def nrm(k, shape, scale):
        return jax.random.normal(k, shape, f32) * scale

    def gain(k, shape):
        return 1.0 + 0.05 * jax.random.normal(k, shape, f32)

    L = DEPTH
    return {
        "x": nrm(ks[0], (BATCH, SEQ, D_MODEL), 1.0),
        "p": nrm(ks[1], (DEPTH, BATCH, SEQ, PLE_DIM), 1.0),
        "g_mix_pre": gain(ks[2], (L, D_MODEL)),
        "w_in": nrm(ks[3], (L, D_MODEL, D_IN), D_MODEL ** -0.5),
        "w_sgu_s": nrm(ks[4], (L, SGU_HEADS, SGU_BLOCK, SGU_BLOCK), SGU_BLOCK ** -0.5),
        "b_sgu_s": 1.0 + 0.1 * jax.random.normal(ks[5], (L, SGU_HEADS, SGU_BLOCK), f32),
        "g_sgu_v": gain(ks[6], (L, D_SGU)),
        "b_sgu_v": nrm(ks[7], (L, D_SGU), 0.02),
        "w_sgu_out": nrm(ks[8], (L, D_SGU, D_MODEL), D_SGU ** -0.5),
        "w_dw": nrm(ks[9], (L, CONV_WIDTH, 1, D_CONV), CONV_WIDTH ** -0.5),
        "b_dw": nrm(ks[10], (L, D_CONV), 0.02),
        "g_conv_ln": gain(ks[11], (L, D_CONV)),
        "b_conv_ln": nrm(ks[12], (L, D_CONV), 0.02),
        "w_conv_out": nrm(ks[13], (L, D_CONV, D_MODEL), D_CONV ** -0.5),
        "w_pool": nrm(ks[14], (L, POOL_GROUPS, POOL_GROUP_DIM, POOL_GROUP_DIM), POOL_GROUP_DIM ** -0.5),
        "s_pool": 1.0 + 0.1 * jax.random.normal(ks[15], (L, D_POOL), f32),
        "w_pool_out": nrm(ks[16], (L, D_POOL, D_MODEL), D_POOL ** -0.5),
        "w_out": nrm(ks[17], (L, D_MODEL, D_MODEL), D_MODEL ** -0.5),
        "g_mix_post": gain(ks[18], (L, D_MODEL)),
        "g_ffn_pre": gain(ks[19], (L, D_MODEL)),
        "w_ffn_in": nrm(ks[20], (L, D_MODEL, 2 * D_FF), D_MODEL ** -0.5),
        "w_ffn_out": nrm(ks[21], (L, D_FF, D_MODEL), D_FF ** -0.5),
        "g_ffn_post": gain(ks[22], (L, D_MODEL)),
        "w_ple": nrm(ks[23], (L, PLE_DIM, D_MODEL), PLE_DIM ** -0.5),
        "w_ple_gate": nrm(ks[24], (L, D_MODEL, D_MODEL), D_MODEL ** -0.5),
    }


def _fwd_reference(x, p, g_mix_pre, w_in, w_sgu_s, b_sgu_s, g_sgu_v, b_sgu_v, w_sgu_out,
              w_dw, b_dw, g_conv_ln, b_conv_ln, w_conv_out, w_pool, s_pool, w_pool_out,
              w_out, g_mix_post, g_ffn_pre, w_ffn_in, w_ffn_out, g_ffn_post,
              w_ple, w_ple_gate):
    h = x
    bsz, s, _ = x.shape
    for i in range(DEPTH):
        hn = rms_norm(h, g_mix_pre[i])
        proj = hn @ w_in[i]
        z_sgu, z_conv, z_pool, z_gate = jnp.split(proj, SPLITS, axis=-1)

        br_a = spatial_gating(jax.nn.gelu(z_sgu), w_sgu_s[i], b_sgu_s[i],
                              g_sgu_v[i], b_sgu_v[i]) @ w_sgu_out[i]
        br_b = conformer_conv(z_conv, w_dw[i], b_dw[i],
                              g_conv_ln[i], b_conv_ln[i]) @ w_conv_out[i]
        br_c = multiscale_pool(z_pool, w_pool[i], s_pool[i]) @ w_pool_out[i]

        gates = jax.nn.sigmoid(z_gate).reshape(bsz, s, N_BRANCH, D_MODEL)
        merged = gates[:, :, 0] * br_a + gates[:, :, 1] * br_b + gates[:, :, 2] * br_c
        h = h + rms_norm(merged @ w_out[i], g_mix_post[i])

        hn = rms_norm(h, g_ffn_pre[i])
        f_gate, f_up = jnp.split(hn @ w_ffn_in[i], 2, axis=-1)
        f = (jax.nn.silu(f_gate) * f_up) @ w_ffn_out[i]
        h = h + rms_norm(f, g_ffn_post[i])

        h = h + jax.nn.sigmoid(h @ w_ple_gate[i]) * (p[i] @ w_ple[i])
    return h


import jax as _jax
import jax.numpy as _jnp

TWIN_FORMAT = 'train_step'
FWD_PARAMS = ['x', 'p', 'g_mix_pre', 'w_in', 'w_sgu_s', 'b_sgu_s', 'g_sgu_v', 'b_sgu_v', 'w_sgu_out', 'w_dw', 'b_dw', 'g_conv_ln', 'b_conv_ln', 'w_conv_out', 'w_pool', 's_pool', 'w_pool_out', 'w_out', 'g_mix_post', 'g_ffn_pre', 'w_ffn_in', 'w_ffn_out', 'g_ffn_post', 'w_ple', 'w_ple_gate']
TWIN_WEIGHTS = ['g_mix_pre', 'w_in', 'w_sgu_s', 'b_sgu_s', 'g_sgu_v', 'b_sgu_v', 'w_sgu_out', 'w_dw', 'b_dw', 'g_conv_ln', 'b_conv_ln', 'w_conv_out', 'w_pool', 's_pool', 'w_pool_out', 'w_out', 'g_mix_post', 'g_ffn_pre', 'w_ffn_in', 'w_ffn_out', 'g_ffn_post', 'w_ple', 'w_ple_gate']
TWIN_DIFF_INPUT = 'x'
TWIN_INPUTS = ['x', 'p', 'g_mix_pre', 'w_in', 'w_sgu_s', 'b_sgu_s', 'g_sgu_v', 'b_sgu_v', 'w_sgu_out', 'w_dw', 'b_dw', 'g_conv_ln', 'b_conv_ln', 'w_conv_out', 'w_pool', 's_pool', 'w_pool_out', 'w_out', 'g_mix_post', 'g_ffn_pre', 'w_ffn_in', 'w_ffn_out', 'g_ffn_post', 'w_ple', 'w_ple_gate', 'loss_target', 'm_g_mix_pre', 'm_w_in', 'm_w_sgu_s', 'm_b_sgu_s', 'm_g_sgu_v', 'm_b_sgu_v', 'm_w_sgu_out', 'm_w_dw', 'm_b_dw', 'm_g_conv_ln', 'm_b_conv_ln', 'm_w_conv_out', 'm_w_pool', 'm_s_pool', 'm_w_pool_out', 'm_w_out', 'm_g_mix_post', 'm_g_ffn_pre', 'm_w_ffn_in', 'm_w_ffn_out', 'm_g_ffn_post', 'm_w_ple', 'm_w_ple_gate', 'v_g_mix_pre', 'v_w_in', 'v_w_sgu_s', 'v_b_sgu_s', 'v_g_sgu_v', 'v_b_sgu_v', 'v_w_sgu_out', 'v_w_dw', 'v_b_dw', 'v_g_conv_ln', 'v_b_conv_ln', 'v_w_conv_out', 'v_w_pool', 'v_s_pool', 'v_w_pool_out', 'v_w_out', 'v_g_mix_post', 'v_g_ffn_pre', 'v_w_ffn_in', 'v_w_ffn_out', 'v_g_ffn_post', 'v_w_ple', 'v_w_ple_gate']
TWIN_OUTPUTS = ['loss', 'grad_x', 'grad_g_mix_pre', 'grad_w_in', 'grad_w_sgu_s', 'grad_b_sgu_s', 'grad_g_sgu_v', 'grad_b_sgu_v', 'grad_w_sgu_out', 'grad_w_dw', 'grad_b_dw', 'grad_g_conv_ln', 'grad_b_conv_ln', 'grad_w_conv_out', 'grad_w_pool', 'grad_s_pool', 'grad_w_pool_out', 'grad_w_out', 'grad_g_mix_post', 'grad_g_ffn_pre', 'grad_w_ffn_in', 'grad_w_ffn_out', 'grad_g_ffn_post', 'grad_w_ple', 'grad_w_ple_gate', 'delta_g_mix_pre', 'delta_w_in', 'delta_w_sgu_s', 'delta_b_sgu_s', 'delta_g_sgu_v', 'delta_b_sgu_v', 'delta_w_sgu_out', 'delta_w_dw', 'delta_b_dw', 'delta_g_conv_ln', 'delta_b_conv_ln', 'delta_w_conv_out', 'delta_w_pool', 'delta_s_pool', 'delta_w_pool_out', 'delta_w_out', 'delta_g_mix_post', 'delta_g_ffn_pre', 'delta_w_ffn_in', 'delta_w_ffn_out', 'delta_g_ffn_post', 'delta_w_ple', 'delta_w_ple_gate', 'new_m_g_mix_pre', 'new_m_w_in', 'new_m_w_sgu_s', 'new_m_b_sgu_s', 'new_m_g_sgu_v', 'new_m_b_sgu_v', 'new_m_w_sgu_out', 'new_m_w_dw', 'new_m_b_dw', 'new_m_g_conv_ln', 'new_m_b_conv_ln', 'new_m_w_conv_out', 'new_m_w_pool', 'new_m_s_pool', 'new_m_w_pool_out', 'new_m_w_out', 'new_m_g_mix_post', 'new_m_g_ffn_pre', 'new_m_w_ffn_in', 'new_m_w_ffn_out', 'new_m_g_ffn_post', 'new_m_w_ple', 'new_m_w_ple_gate', 'new_v_g_mix_pre', 'new_v_w_in', 'new_v_w_sgu_s', 'new_v_b_sgu_s', 'new_v_g_sgu_v', 'new_v_b_sgu_v', 'new_v_w_sgu_out', 'new_v_w_dw', 'new_v_b_dw', 'new_v_g_conv_ln', 'new_v_b_conv_ln', 'new_v_w_conv_out', 'new_v_w_pool', 'new_v_s_pool', 'new_v_w_pool_out', 'new_v_w_out', 'new_v_g_mix_post', 'new_v_g_ffn_pre', 'new_v_w_ffn_in', 'new_v_w_ffn_out', 'new_v_g_ffn_post', 'new_v_w_ple', 'new_v_w_ple_gate']
TWIN_LEAF_KINDS = {'loss': 'loss', 'grad_x': 'grad_x', 'grad_g_mix_pre': 'grad_w', 'grad_w_in': 'grad_w', 'grad_w_sgu_s': 'grad_w', 'grad_b_sgu_s': 'grad_w', 'grad_g_sgu_v': 'grad_w', 'grad_b_sgu_v': 'grad_w', 'grad_w_sgu_out': 'grad_w', 'grad_w_dw': 'grad_w', 'grad_b_dw': 'grad_w', 'grad_g_conv_ln': 'grad_w', 'grad_b_conv_ln': 'grad_w', 'grad_w_conv_out': 'grad_w', 'grad_w_pool': 'grad_w', 'grad_s_pool': 'grad_w', 'grad_w_pool_out': 'grad_w', 'grad_w_out': 'grad_w', 'grad_g_mix_post': 'grad_w', 'grad_g_ffn_pre': 'grad_w', 'grad_w_ffn_in': 'grad_w', 'grad_w_ffn_out': 'grad_w', 'grad_g_ffn_post': 'grad_w', 'grad_w_ple': 'grad_w', 'grad_w_ple_gate': 'grad_w', 'delta_g_mix_pre': 'delta_w', 'delta_w_in': 'delta_w', 'delta_w_sgu_s': 'delta_w', 'delta_b_sgu_s': 'delta_w', 'delta_g_sgu_v': 'delta_w', 'delta_b_sgu_v': 'delta_w', 'delta_w_sgu_out': 'delta_w', 'delta_w_dw': 'delta_w', 'delta_b_dw': 'delta_w', 'delta_g_conv_ln': 'delta_w', 'delta_b_conv_ln': 'delta_w', 'delta_w_conv_out': 'delta_w', 'delta_w_pool': 'delta_w', 'delta_s_pool': 'delta_w', 'delta_w_pool_out': 'delta_w', 'delta_w_out': 'delta_w', 'delta_g_mix_post': 'delta_w', 'delta_g_ffn_pre': 'delta_w', 'delta_w_ffn_in': 'delta_w', 'delta_w_ffn_out': 'delta_w', 'delta_g_ffn_post': 'delta_w', 'delta_w_ple': 'delta_w', 'delta_w_ple_gate': 'delta_w', 'new_m_g_mix_pre': 'new_m', 'new_m_w_in': 'new_m', 'new_m_w_sgu_s': 'new_m', 'new_m_b_sgu_s': 'new_m', 'new_m_g_sgu_v': 'new_m', 'new_m_b_sgu_v': 'new_m', 'new_m_w_sgu_out': 'new_m', 'new_m_w_dw': 'new_m', 'new_m_b_dw': 'new_m', 'new_m_g_conv_ln': 'new_m', 'new_m_b_conv_ln': 'new_m', 'new_m_w_conv_out': 'new_m', 'new_m_w_pool': 'new_m', 'new_m_s_pool': 'new_m', 'new_m_w_pool_out': 'new_m', 'new_m_w_out': 'new_m', 'new_m_g_mix_post': 'new_m', 'new_m_g_ffn_pre': 'new_m', 'new_m_w_ffn_in': 'new_m', 'new_m_w_ffn_out': 'new_m', 'new_m_g_ffn_post': 'new_m', 'new_m_w_ple': 'new_m', 'new_m_w_ple_gate': 'new_m', 'new_v_g_mix_pre': 'new_v', 'new_v_w_in': 'new_v', 'new_v_w_sgu_s': 'new_v', 'new_v_b_sgu_s': 'new_v', 'new_v_g_sgu_v': 'new_v', 'new_v_b_sgu_v': 'new_v', 'new_v_w_sgu_out': 'new_v', 'new_v_w_dw': 'new_v', 'new_v_b_dw': 'new_v', 'new_v_g_conv_ln': 'new_v', 'new_v_b_conv_ln': 'new_v', 'new_v_w_conv_out': 'new_v', 'new_v_w_pool': 'new_v', 'new_v_s_pool': 'new_v', 'new_v_w_pool_out': 'new_v', 'new_v_w_out': 'new_v', 'new_v_g_mix_post': 'new_v', 'new_v_g_ffn_pre': 'new_v', 'new_v_w_ffn_in': 'new_v', 'new_v_w_ffn_out': 'new_v', 'new_v_g_ffn_post': 'new_v', 'new_v_w_ple': 'new_v', 'new_v_w_ple_gate': 'new_v'}


def _forward(args):
    return _fwd_reference(*[args[k] for k in FWD_PARAMS])


def _output_shape():
    def fwd():
        inp = _fwd_setup_inputs(0)
        return _fwd_reference(*[inp[k] for k in FWD_PARAMS])
    out = _jax.eval_shape(fwd)
    return out.shape, out.dtype

N_MICROBATCH = 1
ADAM_LR = 0.001
ADAM_B1 = 0.9
ADAM_B2 = 0.999
ADAM_EPS = 1e-08
ADAM_WD = 0.01
ADAM_STEP = 10
PER_EXAMPLE_BATCH_AXIS = {'x': 0, 'p': 1, 'loss_target': 0}
SHARED_INPUTS = []
_WEIGHT_DTYPES = {'g_mix_pre': _jnp.float32, 'w_in': _jnp.float32, 'w_sgu_s': _jnp.float32, 'b_sgu_s': _jnp.float32, 'g_sgu_v': _jnp.float32, 'b_sgu_v': _jnp.float32, 'w_sgu_out': _jnp.float32, 'w_dw': _jnp.float32, 'b_dw': _jnp.float32, 'g_conv_ln': _jnp.float32, 'b_conv_ln': _jnp.float32, 'w_conv_out': _jnp.float32, 'w_pool': _jnp.float32, 's_pool': _jnp.float32, 'w_pool_out': _jnp.float32, 'w_out': _jnp.float32, 'g_mix_post': _jnp.float32, 'g_ffn_pre': _jnp.float32, 'w_ffn_in': _jnp.float32, 'w_ffn_out': _jnp.float32, 'g_ffn_post': _jnp.float32, 'w_ple': _jnp.float32, 'w_ple_gate': _jnp.float32}
MOMENT_SCALE = {'g_mix_pre': 2.051804e+00, 'w_in': 6.864881e-01, 'w_sgu_s': 4.345771e-01, 'b_sgu_s': 5.763712e-01, 'g_sgu_v': 4.763255e-01, 'b_sgu_v': 4.432501e-01, 'w_sgu_out': 3.170456e+00, 'w_dw': 7.335910e-01, 'b_dw': 1.029238e+01, 'g_conv_ln': 3.923950e+00, 'b_conv_ln': 6.152151e+00, 'w_conv_out': 2.305171e+00, 'w_pool': 1.468418e+00, 's_pool': 1.509477e+00, 'w_pool_out': 1.499828e+00, 'w_out': 4.138171e+00, 'g_mix_post': 6.563359e+01, 'g_ffn_pre': 2.020739e+00, 'w_ffn_in': 8.552022e-01, 'w_ffn_out': 1.796326e+00, 'g_ffn_post': 6.511418e+01, 'w_ple': 1.047127e+00, 'w_ple_gate': 1.073669e+00}


def _to_microbatches(a, axis):
    t = _jnp.moveaxis(a, axis, 0)
    t = t.reshape((N_MICROBATCH, t.shape[0] // N_MICROBATCH) + t.shape[1:])
    return _jnp.moveaxis(t, 1, axis + 1)


def setup_inputs(seed: int = 0) -> dict:
    inp = _fwd_setup_inputs(seed)
    key = _jax.random.fold_in(_jax.random.key(seed), 7919)
    shape, _ = _output_shape()
    out = dict(inp)
    out["loss_target"] = _jax.random.normal(_jax.random.fold_in(key, 0), shape, _jnp.float32)
    for i, name in enumerate(TWIN_WEIGHTS):
        w = inp[name].astype(_jnp.float32)
        if MOMENT_SCALE is None:
            s = _jnp.sqrt(_jnp.mean(_jnp.square(w)) + 1e-30)
        else:
            s = MOMENT_SCALE[name]
        km, kv = _jax.random.split(_jax.random.fold_in(key, i + 1))
        out[name] = w
        out["m_" + name] = s * _jax.random.normal(km, w.shape, _jnp.float32)
        out["v_" + name] = (s * s) * _jax.random.uniform(kv, w.shape, _jnp.float32, 0.5, 1.5)
    if N_MICROBATCH > 1:
        for name, axis in PER_EXAMPLE_BATCH_AXIS.items():
            out[name] = _to_microbatches(out[name], axis)
    return {'x': out['x'], 'p': out['p'], 'g_mix_pre': out['g_mix_pre'], 'w_in': out['w_in'], 'w_sgu_s': out['w_sgu_s'], 'b_sgu_s': out['b_sgu_s'], 'g_sgu_v': out['g_sgu_v'], 'b_sgu_v': out['b_sgu_v'], 'w_sgu_out': out['w_sgu_out'], 'w_dw': out['w_dw'], 'b_dw': out['b_dw'], 'g_conv_ln': out['g_conv_ln'], 'b_conv_ln': out['b_conv_ln'], 'w_conv_out': out['w_conv_out'], 'w_pool': out['w_pool'], 's_pool': out['s_pool'], 'w_pool_out': out['w_pool_out'], 'w_out': out['w_out'], 'g_mix_post': out['g_mix_post'], 'g_ffn_pre': out['g_ffn_pre'], 'w_ffn_in': out['w_ffn_in'], 'w_ffn_out': out['w_ffn_out'], 'g_ffn_post': out['g_ffn_post'], 'w_ple': out['w_ple'], 'w_ple_gate': out['w_ple_gate'], 'loss_target': out['loss_target'], 'm_g_mix_pre': out['m_g_mix_pre'], 'm_w_in': out['m_w_in'], 'm_w_sgu_s': out['m_w_sgu_s'], 'm_b_sgu_s': out['m_b_sgu_s'], 'm_g_sgu_v': out['m_g_sgu_v'], 'm_b_sgu_v': out['m_b_sgu_v'], 'm_w_sgu_out': out['m_w_sgu_out'], 'm_w_dw': out['m_w_dw'], 'm_b_dw': out['m_b_dw'], 'm_g_conv_ln': out['m_g_conv_ln'], 'm_b_conv_ln': out['m_b_conv_ln'], 'm_w_conv_out': out['m_w_conv_out'], 'm_w_pool': out['m_w_pool'], 'm_s_pool': out['m_s_pool'], 'm_w_pool_out': out['m_w_pool_out'], 'm_w_out': out['m_w_out'], 'm_g_mix_post': out['m_g_mix_post'], 'm_g_ffn_pre': out['m_g_ffn_pre'], 'm_w_ffn_in': out['m_w_ffn_in'], 'm_w_ffn_out': out['m_w_ffn_out'], 'm_g_ffn_post': out['m_g_ffn_post'], 'm_w_ple': out['m_w_ple'], 'm_w_ple_gate': out['m_w_ple_gate'], 'v_g_mix_pre': out['v_g_mix_pre'], 'v_w_in': out['v_w_in'], 'v_w_sgu_s': out['v_w_sgu_s'], 'v_b_sgu_s': out['v_b_sgu_s'], 'v_g_sgu_v': out['v_g_sgu_v'], 'v_b_sgu_v': out['v_b_sgu_v'], 'v_w_sgu_out': out['v_w_sgu_out'], 'v_w_dw': out['v_w_dw'], 'v_b_dw': out['v_b_dw'], 'v_g_conv_ln': out['v_g_conv_ln'], 'v_b_conv_ln': out['v_b_conv_ln'], 'v_w_conv_out': out['v_w_conv_out'], 'v_w_pool': out['v_w_pool'], 'v_s_pool': out['v_s_pool'], 'v_w_pool_out': out['v_w_pool_out'], 'v_w_out': out['v_w_out'], 'v_g_mix_post': out['v_g_mix_post'], 'v_g_ffn_pre': out['v_g_ffn_pre'], 'v_w_ffn_in': out['v_w_ffn_in'], 'v_w_ffn_out': out['v_w_ffn_out'], 'v_g_ffn_post': out['v_g_ffn_post'], 'v_w_ple': out['v_w_ple'], 'v_w_ple_gate': out['v_w_ple_gate']}


def _loss(weights, diff, rest, loss_target):
    with _jax.named_scope("forward"):
        args = {**rest, TWIN_DIFF_INPUT: diff, **{k: w.astype(_WEIGHT_DTYPES[k]) for k, w in weights.items()}}
        y = _forward(args)
    with _jax.named_scope("loss_head"):
        err = _jnp.square(y.astype(_jnp.float32) - loss_target)
        return 0.5 * _jnp.sum(_jnp.mean(err, axis=-1)) if err.ndim else 0.5 * err


def _adamw(w, g, m, v):
    m = ADAM_B1 * m + (1.0 - ADAM_B1) * g
    v = ADAM_B2 * v + (1.0 - ADAM_B2) * _jnp.square(g)
    m_hat = m / (1.0 - ADAM_B1 ** ADAM_STEP)
    v_hat = v / (1.0 - ADAM_B2 ** ADAM_STEP)
    delta = -ADAM_LR * (m_hat / (_jnp.sqrt(v_hat) + ADAM_EPS) + ADAM_WD * w)
    return delta, m, v


def reference(x, p, g_mix_pre, w_in, w_sgu_s, b_sgu_s, g_sgu_v, b_sgu_v, w_sgu_out, w_dw, b_dw, g_conv_ln, b_conv_ln, w_conv_out, w_pool, s_pool, w_pool_out, w_out, g_mix_post, g_ffn_pre, w_ffn_in, w_ffn_out, g_ffn_post, w_ple, w_ple_gate, loss_target, m_g_mix_pre, m_w_in, m_w_sgu_s, m_b_sgu_s, m_g_sgu_v, m_b_sgu_v, m_w_sgu_out, m_w_dw, m_b_dw, m_g_conv_ln, m_b_conv_ln, m_w_conv_out, m_w_pool, m_s_pool, m_w_pool_out, m_w_out, m_g_mix_post, m_g_ffn_pre, m_w_ffn_in, m_w_ffn_out, m_g_ffn_post, m_w_ple, m_w_ple_gate, v_g_mix_pre, v_w_in, v_w_sgu_s, v_b_sgu_s, v_g_sgu_v, v_b_sgu_v, v_w_sgu_out, v_w_dw, v_b_dw, v_g_conv_ln, v_b_conv_ln, v_w_conv_out, v_w_pool, v_s_pool, v_w_pool_out, v_w_out, v_g_mix_post, v_g_ffn_pre, v_w_ffn_in, v_w_ffn_out, v_g_ffn_post, v_w_ple, v_w_ple_gate):
    given = dict(x=x, p=p, g_mix_pre=g_mix_pre, w_in=w_in, w_sgu_s=w_sgu_s, b_sgu_s=b_sgu_s, g_sgu_v=g_sgu_v, b_sgu_v=b_sgu_v, w_sgu_out=w_sgu_out, w_dw=w_dw, b_dw=b_dw, g_conv_ln=g_conv_ln, b_conv_ln=b_conv_ln, w_conv_out=w_conv_out, w_pool=w_pool, s_pool=s_pool, w_pool_out=w_pool_out, w_out=w_out, g_mix_post=g_mix_post, g_ffn_pre=g_ffn_pre, w_ffn_in=w_ffn_in, w_ffn_out=w_ffn_out, g_ffn_post=g_ffn_post, w_ple=w_ple, w_ple_gate=w_ple_gate, loss_target=loss_target, m_g_mix_pre=m_g_mix_pre, m_w_in=m_w_in, m_w_sgu_s=m_w_sgu_s, m_b_sgu_s=m_b_sgu_s, m_g_sgu_v=m_g_sgu_v, m_b_sgu_v=m_b_sgu_v, m_w_sgu_out=m_w_sgu_out, m_w_dw=m_w_dw, m_b_dw=m_b_dw, m_g_conv_ln=m_g_conv_ln, m_b_conv_ln=m_b_conv_ln, m_w_conv_out=m_w_conv_out, m_w_pool=m_w_pool, m_s_pool=m_s_pool, m_w_pool_out=m_w_pool_out, m_w_out=m_w_out, m_g_mix_post=m_g_mix_post, m_g_ffn_pre=m_g_ffn_pre, m_w_ffn_in=m_w_ffn_in, m_w_ffn_out=m_w_ffn_out, m_g_ffn_post=m_g_ffn_post, m_w_ple=m_w_ple, m_w_ple_gate=m_w_ple_gate, v_g_mix_pre=v_g_mix_pre, v_w_in=v_w_in, v_w_sgu_s=v_w_sgu_s, v_b_sgu_s=v_b_sgu_s, v_g_sgu_v=v_g_sgu_v, v_b_sgu_v=v_b_sgu_v, v_w_sgu_out=v_w_sgu_out, v_w_dw=v_w_dw, v_b_dw=v_b_dw, v_g_conv_ln=v_g_conv_ln, v_b_conv_ln=v_b_conv_ln, v_w_conv_out=v_w_conv_out, v_w_pool=v_w_pool, v_s_pool=v_s_pool, v_w_pool_out=v_w_pool_out, v_w_out=v_w_out, v_g_mix_post=v_g_mix_post, v_g_ffn_pre=v_g_ffn_pre, v_w_ffn_in=v_w_ffn_in, v_w_ffn_out=v_w_ffn_out, v_g_ffn_post=v_g_ffn_post, v_w_ple=v_w_ple, v_w_ple_gate=v_w_ple_gate)
    weights = {n: given[n] for n in TWIN_WEIGHTS}
    shared = {n: given[n] for n in SHARED_INPUTS}
    per_example = {n: given[n] for n in ['x', 'p']}
    grad_fn = _jax.value_and_grad(_loss, argnums=(0, 1))

    def one_microbatch(ex, loss_target):
        ex = dict(ex)
        diff = ex.pop(TWIN_DIFF_INPUT)
        return grad_fn(weights, diff, {**shared, **ex}, loss_target)

    if N_MICROBATCH == 1:
        loss, (grad_w, grad_x) = one_microbatch(per_example, given["loss_target"])
    else:
        def body(carry, xs):
            loss_sum, grad_sum = carry
            l_k, (gw_k, gx_k) = one_microbatch(xs[0], xs[1])
            with _jax.named_scope("update"):
                return (loss_sum + l_k, _jax.tree.map(_jnp.add, grad_sum, gw_k)), gx_k

        init = (_jnp.zeros((), _jnp.float32), _jax.tree.map(_jnp.zeros_like, weights))
        (loss, grad_w), grad_x = _jax.lax.scan(body, init, (per_example, given["loss_target"]))
    with _jax.named_scope("update"):
        delta_w, new_m, new_v = {}, {}, {}
        for n in TWIN_WEIGHTS:
            delta_w[n], new_m[n], new_v[n] = _adamw(weights[n], grad_w[n], given["m_" + n], given["v_" + n])
    return (loss, grad_x, *[grad_w[n] for n in TWIN_WEIGHTS], *[delta_w[n] for n in TWIN_WEIGHTS],
            *[new_m[n] for n in TWIN_WEIGHTS], *[new_v[n] for n in TWIN_WEIGHTS])
```

```python
import functools

import jax
import jax.numpy as jnp
from jax import lax
from jax.experimental import pallas as pl
from jax.experimental.pallas import tpu as pltpu

F32 = jnp.float32
BF16 = jnp.bfloat16
MESH = pl.DeviceIdType.MESH

EPS = 1e-6
D_MODEL = 1024
SGU_BLOCK = 128
SGU_HEADS = 8
CHUNK = 64
CONV_WIDTH = 31
CONV_HALO = 32
POOL_WINDOWS = (2, 4, 8, 16)
POOL_HALO = 16
POOL_GROUP = 256
D_FF = 2816
N_CHIPS = 4

ADAM_LR = 0.001
ADAM_B1 = 0.9
ADAM_B2 = 0.999
ADAM_EPS = 1e-08
ADAM_WD = 0.01
ADAM_STEP = 10

VMEM_LIMIT = 52 * 1024 * 1024
ROW_TILE = 512
ROW_TILE_HEAVY = 256
CONV_ROWS = 32
CONV_LANES = 256
EW_BLOCK_BYTES = 2 * 1024 * 1024


def _params(n_grid):
    return pltpu.CompilerParams(dimension_semantics=("arbitrary",) * n_grid, vmem_limit_bytes=VMEM_LIMIT)


def _dot(a, b):
    return jnp.dot(a.astype(BF16), b.astype(BF16), preferred_element_type=F32)


def _dot_nt(a, b):
    return lax.dot_general(a.astype(BF16), b.astype(BF16), (((1,), (1,)), ((), ())), preferred_element_type=F32)


def _dot_tn(a, b):
    return lax.dot_general(a.astype(BF16), b.astype(BF16), (((0,), (0,)), ((), ())), preferred_element_type=F32)


def _sigmoid(x):
    return 1.0 / (1.0 + jnp.exp(-x))


_GELU_C = 0.7978845608028654
_GELU_A = 0.044715


def _gelu(x):
    t = jnp.tanh(_GELU_C * (x + _GELU_A * x * x * x))
    return 0.5 * x * (1.0 + t)


def _gelu_and_grad(x):
    x2 = x * x
    t = jnp.tanh(_GELU_C * (x + _GELU_A * x2 * x))
    g = 0.5 * (1.0 + t) + 0.5 * x * (1.0 - t * t) * (_GELU_C * (1.0 + 3.0 * _GELU_A * x2))
    return 0.5 * x * (1.0 + t), g


def _rms_stats(x):
    r = lax.rsqrt(jnp.mean(x * x, axis=-1, keepdims=True) + EPS)
    return x * r, r


def _rms_bwd(xn, r, g, dy):
    gd = dy * g
    return r * (gd - xn * jnp.mean(gd * xn, axis=-1, keepdims=True)), dy * xn


def _ln_stats(x):
    mu = jnp.mean(x, axis=-1, keepdims=True)
    xc = x - mu
    rstd = lax.rsqrt(jnp.mean(xc * xc, axis=-1, keepdims=True) + EPS)
    return xc * rstd, rstd


def _ln_bwd(xhat, rstd, g, dy):
    dxh = dy * g
    return rstd * (dxh - jnp.mean(dxh, axis=-1, keepdims=True) - xhat * jnp.mean(dxh * xhat, axis=-1, keepdims=True))


def _rowsum(x):
    return jnp.sum(x, axis=0, keepdims=True)


def _tile(t, want):
    return min(t, want)


def _full(shape):
    n = len(shape)
    return pl.BlockSpec(shape, lambda *_: (0,) * n)


def _resident(shape):
    n = len(shape)
    return pl.BlockSpec(shape, lambda *_: (0,) * n, pipeline_mode=pl.Buffered(1))


def _norm_mm(h, g, w4, name):
    t, d = h.shape
    n = w4.shape[2]
    tm = _tile(t, ROW_TILE)

    def body(h_ref, g_ref, w_ref, o_ref, hn_ref):
        @pl.when(pl.program_id(1) == 0)
        def _():
            xn, _ = _rms_stats(h_ref[...])
            hn_ref[...] = (xn * g_ref[...]).astype(BF16)

        o_ref[...] = jnp.dot(hn_ref[...], w_ref[...], preferred_element_type=F32).astype(BF16)

    return pl.pallas_call(
        body, name=name, grid=(t // tm, N_CHIPS),
        in_specs=[pl.BlockSpec((tm, d), lambda i, j: (i, 0)), pl.BlockSpec((1, d), lambda i, j: (0, 0)),
                  pl.BlockSpec((None, d, n), lambda i, j: (j, 0, 0))],
        out_specs=[pl.BlockSpec((tm, n), lambda i, j: (i, j)), pl.BlockSpec((tm, d), lambda i, j: (i, 0))],
        out_shape=[jax.ShapeDtypeStruct((t, N_CHIPS * n), BF16), jax.ShapeDtypeStruct((t, d), BF16)],
        compiler_params=_params(2))(h, g, w4)


def _sgu_mask():
    ii = lax.broadcasted_iota(jnp.int32, (SGU_BLOCK, SGU_BLOCK), 0) // CHUNK
    jj = lax.broadcasted_iota(jnp.int32, (SGU_BLOCK, SGU_BLOCK), 1) // CHUNK
    return jj <= ii


def _sgu_fwd(proj, wm, bs3, gv, bv, name):
    t = proj.shape[0]
    d = D_MODEL
    tm = _tile(t, ROW_TILE)
    hd = d // SGU_HEADS

    def body(zu_ref, zv_ref, wm_ref, bs_ref, gv_ref, bv_ref, o_ref):
        mask = _sgu_mask()
        for blk in range(tm // SGU_BLOCK):
            rows = pl.ds(blk * SGU_BLOCK, SGU_BLOCK)
            u = _gelu(zu_ref[rows, :].astype(F32))
            xhat, _ = _ln_stats(_gelu(zv_ref[rows, :].astype(F32)))
            vn = (xhat * gv_ref[...] + bv_ref[...]).astype(BF16)
            for hh in range(SGU_HEADS):
                cols = slice(hh * hd, (hh + 1) * hd)
                wmh = jnp.where(mask, wm_ref[hh], 0.0).astype(BF16)
                mixed = jnp.dot(wmh, vn[:, cols], preferred_element_type=F32) + bs_ref[hh]
                o_ref[rows, cols] = (u[:, cols] * mixed).astype(BF16)

    return pl.pallas_call(
        body, name=name, grid=(t // tm,),
        in_specs=[pl.BlockSpec((tm, d), lambda i: (i, 0)), pl.BlockSpec((tm, d), lambda i: (i, 1)),
                  _full(wm.shape), _full(bs3.shape), _full(gv.shape), _full(bv.shape)],
        out_specs=pl.BlockSpec((tm, d), lambda i: (i, 0)),
        out_shape=jax.ShapeDtypeStruct((t, d), BF16),
        compiler_params=_params(1))(proj, proj, wm, bs3, gv, bv)


def _conv_taps(scr_ref, r0, c0, base, weight):
    n = CONV_ROWS + CONV_HALO
    win = scr_ref[pl.ds(r0, n), pl.ds(c0, CONV_LANES)]
    acc = None
    for r in range(8):
        rolled = win if r == 0 else pltpu.roll(win, n - r, 0)
        for q in range((CONV_HALO + 7) // 8 + 1):
            k = 8 * q + r - base
            if 0 <= k < CONV_WIDTH and 8 * q + CONV_ROWS <= n:
                term = weight(k) * rolled[8 * q:8 * q + CONV_ROWS]
                acc = term if acc is None else acc + term
    return acc


def _glu_rows(a_ref, g_ref):
    return a_ref[...].astype(F32) * _sigmoid(g_ref[...].astype(F32))


def _conv_into(scr_ref, cv_ref, w_ref, tm, base, flip):
    def chunk(ci, carry):
        r0 = pl.multiple_of(ci * CONV_ROWS, CONV_ROWS)
        for c0 in range(0, D_MODEL, CONV_LANES):
            def weight(k, c0=c0):
                kk = CONV_WIDTH - 1 - k if flip else k
                return w_ref[kk:kk + 1, c0:c0 + CONV_LANES]
            cv_ref[pl.ds(r0, CONV_ROWS), pl.ds(c0, CONV_LANES)] = _conv_taps(scr_ref, r0, c0, base, weight)
        return carry

    lax.fori_loop(0, tm // CONV_ROWS, chunk, 0)


def _conv_specs(t, tm, d):
    hb = tm // CONV_HALO
    main = [pl.BlockSpec((tm, d), lambda i: (i, 2)), pl.BlockSpec((tm, d), lambda i: (i, 3))]
    halo = [pl.BlockSpec((CONV_HALO, d), lambda i: (jnp.maximum(i * hb - 1, 0), 2)),
            pl.BlockSpec((CONV_HALO, d), lambda i: (jnp.maximum(i * hb - 1, 0), 3))]
    return main, halo


def _fill_glu_history(scr_ref, a_ref, g_ref, ah_ref, gh_ref, tm):
    hist = _glu_rows(ah_ref, gh_ref)
    scr_ref[0:CONV_HALO, :] = jnp.where(pl.program_id(0) > 0, hist, 0.0)
    scr_ref[CONV_HALO:CONV_HALO + tm, :] = _glu_rows(a_ref, g_ref)


_CONV_BASE = CONV_HALO - (CONV_WIDTH - 1)


def _conv_fwd(proj, wdw, bdw, gln, bln, name):
    t = proj.shape[0]
    d = D_MODEL
    tm = _tile(t, ROW_TILE)
    main, halo = _conv_specs(t, tm, d)

    def body(a_ref, g_ref, ah_ref, gh_ref, w_ref, b_ref, gl_ref, bl_ref, o_ref, scr_ref, cv_ref):
        _fill_glu_history(scr_ref, a_ref, g_ref, ah_ref, gh_ref, tm)
        _conv_into(scr_ref, cv_ref, w_ref, tm, _CONV_BASE, False)
        xhat, _ = _ln_stats(cv_ref[...] + b_ref[...])
        cn = xhat * gl_ref[...] + bl_ref[...]
        o_ref[...] = (cn * _sigmoid(cn)).astype(BF16)

    return pl.pallas_call(
        body, name=name, grid=(t // tm,),
        in_specs=main + halo + [_full(wdw.shape), _full(bdw.shape), _full(gln.shape), _full(bln.shape)],
        out_specs=pl.BlockSpec((tm, d), lambda i: (i, 0)),
        out_shape=jax.ShapeDtypeStruct((t, d), BF16),
        scratch_shapes=[pltpu.VMEM((tm + CONV_HALO, d), F32), pltpu.VMEM((tm, d), F32)],
        compiler_params=_params(1))(proj, proj, proj, proj, wdw, bdw, gln, bln)


def _pool_fill(scr_ref, z_ref, zh_ref, tm):
    scr_ref[0:POOL_HALO, :] = jnp.where(pl.program_id(0) > 0, zh_ref[...].astype(F32), 0.0)
    scr_ref[POOL_HALO:POOL_HALO + tm, :] = z_ref[...].astype(F32)


def _pool_count(t0, rows, w):
    pos = (t0 + lax.broadcasted_iota(jnp.int32, (rows, 1), 0) + 1).astype(F32)
    return jnp.minimum(pos, float(w))


def _pooled_group(scr_ref, gi, w, tm, t0):
    cols = pl.ds(gi * POOL_GROUP, POOL_GROUP)
    acc = scr_ref[pl.ds(POOL_HALO, tm), cols]
    z = acc
    for k in range(1, w):
        acc = acc + scr_ref[pl.ds(POOL_HALO - k, tm), cols]
    return acc / _pool_count(t0, tm, w) - z


def _pool_specs(tm, d):
    hb = tm // POOL_HALO
    return [pl.BlockSpec((tm, d), lambda i: (i, 4)),
            pl.BlockSpec((POOL_HALO, d), lambda i: (jnp.maximum(i * hb - 1, 0), 4))]


def _pool_fwd(proj, wpool, spool, name):
    t = proj.shape[0]
    d = D_MODEL
    tm = _tile(t, ROW_TILE)

    def body(z_ref, zh_ref, w_ref, s_ref, o_ref, scr_ref):
        _pool_fill(scr_ref, z_ref, zh_ref, tm)
        t0 = pl.program_id(0) * tm
        for gi, w in enumerate(POOL_WINDOWS):
            cols = slice(gi * POOL_GROUP, (gi + 1) * POOL_GROUP)
            pooled = _pooled_group(scr_ref, gi, w, tm, t0)
            o_ref[:, cols] = (_dot(pooled, w_ref[gi]) * s_ref[:, cols]).astype(BF16)

    return pl.pallas_call(
        body, name=name, grid=(t // tm,),
        in_specs=_pool_specs(tm, d) + [_full(wpool.shape), _full(spool.shape)],
        out_specs=pl.BlockSpec((tm, d), lambda i: (i, 0)),
        out_shape=jax.ShapeDtypeStruct((t, d), BF16),
        scratch_shapes=[pltpu.VMEM((tm + POOL_HALO, d), F32)],
        compiler_params=_params(1))(proj, proj, wpool, spool)


def _merge_fwd(proj, sg, cs, ps, wa, wb, wc, name):
    t = proj.shape[0]
    d = D_MODEL
    tm = _tile(t, ROW_TILE_HEAVY)

    def body(za_ref, zb_ref, zc_ref, sg_ref, cs_ref, ps_ref, wa_ref, wb_ref, wc_ref, ba_ref, bb_ref, bc_ref, m_ref):
        merged = None
        for z_ref, x_ref, w_ref, b_ref in ((za_ref, sg_ref, wa_ref, ba_ref), (zb_ref, cs_ref, wb_ref, bb_ref),
                                           (zc_ref, ps_ref, wc_ref, bc_ref)):
            br = jnp.dot(x_ref[...], w_ref[...], preferred_element_type=F32)
            b_ref[...] = br.astype(BF16)
            term = _sigmoid(z_ref[...].astype(F32)) * br
            merged = term if merged is None else merged + term
        m_ref[...] = merged.astype(BF16)

    row = pl.BlockSpec((tm, d), lambda i: (i, 0))
    wspec = _resident((d, d))
    return pl.pallas_call(
        body, name=name, grid=(t // tm,),
        in_specs=[pl.BlockSpec((tm, d), lambda i: (i, 5)), pl.BlockSpec((tm, d), lambda i: (i, 6)),
                  pl.BlockSpec((tm, d), lambda i: (i, 7)), row, row, row, wspec, wspec, wspec],
        out_specs=[row, row, row, row],
        out_shape=[jax.ShapeDtypeStruct((t, d), BF16)] * 4,
        compiler_params=_params(1))(proj, proj, proj, sg, cs, ps, wa, wb, wc)


def _mm_norm_res(a, w, g, hres, name):
    t, k = a.shape
    d = w.shape[1]
    tm = _tile(t, ROW_TILE)

    def body(a_ref, w_ref, g_ref, h_ref, y_ref, o_ref):
        y = jnp.dot(a_ref[...], w_ref[...], preferred_element_type=F32)
        y_ref[...] = y
        yn, _ = _rms_stats(y)
        o_ref[...] = h_ref[...] + yn * g_ref[...]

    row = pl.BlockSpec((tm, d), lambda i: (i, 0))
    return pl.pallas_call(
        body, name=name, grid=(t // tm,),
        in_specs=[pl.BlockSpec((tm, k), lambda i: (i, 0)), _resident(w.shape), _full(g.shape), row],
        out_specs=[row, row],
        out_shape=[jax.ShapeDtypeStruct((t, d), F32)] * 2,
        compiler_params=_params(1))(a, w, g, hres)


def _ffn_in(h, g, w4, name):
    t, d = h.shape
    n = w4.shape[2]
    tm = _tile(t, ROW_TILE)
    nj = D_FF // n

    def body(h_ref, g_ref, wg_ref, wu_ref, fg_ref, fu_ref, act_ref, hn_ref):
        @pl.when(pl.program_id(1) == 0)
        def _():
            xn, _ = _rms_stats(h_ref[...])
            hn_ref[...] = (xn * g_ref[...]).astype(BF16)

        fg = jnp.dot(hn_ref[...], wg_ref[...], preferred_element_type=F32)
        fu = jnp.dot(hn_ref[...], wu_ref[...], preferred_element_type=F32)
        fg_ref[...] = fg.astype(BF16)
        fu_ref[...] = fu.astype(BF16)
        act_ref[...] = (fg * _sigmoid(fg) * fu).astype(BF16)

    col = pl.BlockSpec((tm, n), lambda i, j: (i, j))
    return pl.pallas_call(
        body, name=name, grid=(t // tm, nj),
        in_specs=[pl.BlockSpec((tm, d), lambda i, j: (i, 0)), pl.BlockSpec((1, d), lambda i, j: (0, 0)),
                  pl.BlockSpec((None, d, n), lambda i, j: (j, 0, 0)),
                  pl.BlockSpec((None, d, n), lambda i, j: (j + nj, 0, 0))],
        out_specs=[col, col, col, pl.BlockSpec((tm, d), lambda i, j: (i, 0))],
        out_shape=[jax.ShapeDtypeStruct((t, D_FF), BF16)] * 3 + [jax.ShapeDtypeStruct((t, d), BF16)],
        compiler_params=_params(2))(h, g, w4, w4)


def _ple_fwd(h, p, wg, wp, name):
    t, d = h.shape
    tm = _tile(t, ROW_TILE)

    def body(h_ref, p_ref, wg_ref, wp_ref, o_ref, q_ref, e_ref):
        hh = h_ref[...]
        q = _dot(hh, wg_ref[...])
        e = _dot(p_ref[...], wp_ref[...])
        q_ref[...] = q.astype(BF16)
        e_ref[...] = e.astype(BF16)
        o_ref[...] = hh + _sigmoid(q) * e

    row = pl.BlockSpec((tm, d), lambda i: (i, 0))
    return pl.pallas_call(
        body, name=name, grid=(t // tm,),
        in_specs=[row, pl.BlockSpec((tm, p.shape[1]), lambda i: (i, 0)), _resident(wg.shape), _resident(wp.shape)],
        out_specs=[row, row, row],
        out_shape=[jax.ShapeDtypeStruct((t, d), F32), jax.ShapeDtypeStruct((t, d), BF16),
                   jax.ShapeDtypeStruct((t, d), BF16)],
        compiler_params=_params(1))(h, p, wg, wp)


def _loss_head(y, target, name):
    t, d = y.shape
    tm = _tile(t, ROW_TILE)

    def body(y_ref, t_ref, dy_ref, l_ref):
        @pl.when(pl.program_id(0) == 0)
        def _():
            l_ref[...] = jnp.zeros_like(l_ref)

        err = y_ref[...] - t_ref[...]
        dy_ref[...] = err * (1.0 / d)
        l_ref[...] += jnp.sum(err * err, keepdims=True)[:, :1] * jnp.ones((1, 128), F32)

    row = pl.BlockSpec((tm, d), lambda i: (i, 0))
    return pl.pallas_call(
        body, name=name, grid=(t // tm,),
        in_specs=[row, row], out_specs=[row, _full((1, 128))],
        out_shape=[jax.ShapeDtypeStruct((t, d), F32), jax.ShapeDtypeStruct((1, 128), F32)],
        compiler_params=_params(1))(y, target)


def _ple_bwd(dh, q, e, wg, name):
    t, d = dh.shape
    tm = _tile(t, ROW_TILE)

    def body(dh_ref, q_ref, e_ref, wg_ref, dq_ref, de_ref, o_ref):
        dh_ = dh_ref[...]
        s = _sigmoid(q_ref[...].astype(F32))
        dq = (dh_ * e_ref[...].astype(F32) * s * (1.0 - s)).astype(BF16)
        dq_ref[...] = dq
        de_ref[...] = (dh_ * s).astype(BF16)
        o_ref[...] = dh_ + _dot_nt(dq, wg_ref[...])

    row = pl.BlockSpec((tm, d), lambda i: (i, 0))
    return pl.pallas_call(
        body, name=name, grid=(t // tm,),
        in_specs=[row, row, row, _resident(wg.shape)], out_specs=[row, row, row],
        out_shape=[jax.ShapeDtypeStruct((t, d), BF16), jax.ShapeDtypeStruct((t, d), BF16),
                   jax.ShapeDtypeStruct((t, d), F32)],
        compiler_params=_params(1))(dh, q, e, wg)


def _ffn_out_bwd(dh, f, g, fg, fu, w, name):
    t, d = dh.shape
    tm = _tile(t, ROW_TILE_HEAVY)

    def body(dh_ref, f_ref, g_ref, fg_ref, fu_ref, w_ref, df_ref, dff_ref, dg_ref):
        @pl.when(pl.program_id(0) == 0)
        def _():
            dg_ref[...] = jnp.zeros_like(dg_ref)

        fn, r = _rms_stats(f_ref[...])
        df, dgt = _rms_bwd(fn, r, g_ref[...], dh_ref[...])
        dg_ref[...] += _rowsum(dgt)
        df = df.astype(BF16)
        df_ref[...] = df
        dact = _dot_nt(df, w_ref[...])
        fg_ = fg_ref[...].astype(F32)
        s = _sigmoid(fg_)
        dff_ref[:, 0:D_FF] = (dact * fu_ref[...].astype(F32) * (s * (1.0 + fg_ * (1.0 - s)))).astype(BF16)
        dff_ref[:, D_FF:2 * D_FF] = (dact * (fg_ * s)).astype(BF16)

    row = pl.BlockSpec((tm, d), lambda i: (i, 0))
    wide = pl.BlockSpec((tm, D_FF), lambda i: (i, 0))
    return pl.pallas_call(
        body, name=name, grid=(t // tm,),
        in_specs=[row, row, _full(g.shape), wide, wide, _resident(w.shape)],
        out_specs=[row, pl.BlockSpec((tm, 2 * D_FF), lambda i: (i, 0)), _full((1, d))],
        out_shape=[jax.ShapeDtypeStruct((t, d), BF16), jax.ShapeDtypeStruct((t, 2 * D_FF), BF16),
                   jax.ShapeDtypeStruct((1, d), F32)],
        compiler_params=_params(1))(dh, f, g, fg, fu, w)


def _in_bwd(pieces, w4, unit, h, g, dres, name):
    t, d = h.shape
    tm = _tile(t, ROW_TILE)
    per_chunk = w4.shape[2] // unit
    offs = []
    total = 0
    for _, nu in pieces:
        offs.append(total)
        total += nu
    n_p = len(pieces)

    def body(*refs):
        p_refs = refs[:n_p]
        w_ref, h_ref, g_ref, r_ref, o_ref, dg_ref, acc_ref = refs[n_p:]
        u = pl.program_id(1)

        @pl.when(u == 0)
        def _():
            acc_ref[...] = jnp.zeros_like(acc_ref)

        @pl.when((pl.program_id(0) == 0) & (u == 0))
        def _():
            dg_ref[...] = jnp.zeros_like(dg_ref)

        for p_ref, off, (_, nu) in zip(p_refs, offs, pieces):
            @pl.when((u >= off) & (u < off + nu))
            def _(p_ref=p_ref):
                acc_ref[...] += _dot_nt(p_ref[...], w_ref[...])

        @pl.when(u == total - 1)
        def _():
            xn, r = _rms_stats(h_ref[...])
            dx, dgt = _rms_bwd(xn, r, g_ref[...], acc_ref[...])
            dg_ref[...] += _rowsum(dgt)
            o_ref[...] = r_ref[...] + dx

    def piece_spec(off, nu):
        return pl.BlockSpec((tm, unit), lambda i, u: (i, jnp.clip(u - off, 0, nu - 1)))

    row = pl.BlockSpec((tm, d), lambda i, u: (i, 0))
    return pl.pallas_call(
        body, name=name, grid=(t // tm, total),
        in_specs=[piece_spec(off, nu) for off, (_, nu) in zip(offs, pieces)]
        + [pl.BlockSpec((None, d, unit), lambda i, u: (u // per_chunk, 0, u % per_chunk)), row,
           pl.BlockSpec((1, d), lambda i, u: (0, 0)), row],
        out_specs=[row, pl.BlockSpec((1, d), lambda i, u: (0, 0))],
        out_shape=[jax.ShapeDtypeStruct((t, d), F32), jax.ShapeDtypeStruct((1, d), F32)],
        scratch_shapes=[pltpu.VMEM((tm, d), F32)],
        compiler_params=_params(2))(*[a for a, _ in pieces], w4, h, g, dres)


def _lane_block(n, cap):
    return max(b for b in range(128, min(n, cap) + 1, 128) if n % b == 0)


def _mm_tn(x, dy, name, bn=None):
    t, m = x.shape
    n = dy.shape[1]
    bm = _lane_block(m, 1408)
    bn = bn or _lane_block(n, 1408)
    tk = _tile(t, 1024)

    def body(x_ref, dy_ref, o_ref):
        @pl.when(pl.program_id(2) == 0)
        def _():
            o_ref[...] = jnp.zeros_like(o_ref)

        o_ref[...] += _dot_tn(x_ref[...], dy_ref[...])

    return pl.pallas_call(
        body, name=name, grid=(m // bm, n // bn, t // tk),
        in_specs=[pl.BlockSpec((tk, bm), lambda a, b, k: (k, a)), pl.BlockSpec((tk, bn), lambda a, b, k: (k, b))],
        out_specs=pl.BlockSpec((bm, bn), lambda a, b, k: (a, b)),
        out_shape=jax.ShapeDtypeStruct((m, n), F32),
        compiler_params=_params(3))(x, dy)


def _merge_bwd(dh, mo, g, proj, bra, brb, brc, w_out, wa, wb, wc, name):
    t, d = dh.shape
    tm = _tile(t, ROW_TILE_HEAVY)

    def body(dh_ref, mo_ref, g_ref, za_ref, zb_ref, zc_ref, ba_ref, bb_ref, bc_ref, wo_ref, wa_ref, wb_ref, wc_ref,
             dmo_ref, dba_ref, dbb_ref, dbc_ref, dz_ref, dsg_ref, dcs_ref, dps_ref, dg_ref):
        @pl.when(pl.program_id(0) == 0)
        def _():
            dg_ref[...] = jnp.zeros_like(dg_ref)

        mon, r = _rms_stats(mo_ref[...])
        dmo, dgt = _rms_bwd(mon, r, g_ref[...], dh_ref[...])
        dg_ref[...] += _rowsum(dgt)
        dmo = dmo.astype(BF16)
        dmo_ref[...] = dmo
        dmerged = _dot_nt(dmo, wo_ref[...])
        branches = ((za_ref, ba_ref, wa_ref, dba_ref, dsg_ref), (zb_ref, bb_ref, wb_ref, dbb_ref, dcs_ref),
                    (zc_ref, bc_ref, wc_ref, dbc_ref, dps_ref))
        for j, (z_ref, b_ref, w_ref, db_ref, dx_ref) in enumerate(branches):
            gate = _sigmoid(z_ref[...].astype(F32))
            dbr = (dmerged * gate).astype(BF16)
            db_ref[...] = dbr
            dz_ref[:, j * d:(j + 1) * d] = (dmerged * b_ref[...].astype(F32) * gate * (1.0 - gate)).astype(BF16)
            dx_ref[...] = _dot_nt(dbr, w_ref[...]).astype(BF16)

    row = pl.BlockSpec((tm, d), lambda i: (i, 0))
    wspec = _resident((d, d))
    bf = jax.ShapeDtypeStruct((t, d), BF16)
    return pl.pallas_call(
        body, name=name, grid=(t // tm,),
        in_specs=[row, row, _full(g.shape), pl.BlockSpec((tm, d), lambda i: (i, 5)),
                  pl.BlockSpec((tm, d), lambda i: (i, 6)), pl.BlockSpec((tm, d), lambda i: (i, 7)),
                  row, row, row, wspec, wspec, wspec, wspec],
        out_specs=[row, row, row, row, pl.BlockSpec((tm, 3 * d), lambda i: (i, 0)), row, row, row, _full((1, d))],
        out_shape=[bf, bf, bf, bf, jax.ShapeDtypeStruct((t, 3 * d), BF16), bf, bf, bf,
                   jax.ShapeDtypeStruct((1, d), F32)],
        compiler_params=_params(1))(dh, mo, g, proj, proj, proj, bra, brb, brc, w_out, wa, wb, wc)


def _sgu_bwd(proj, dsg, wm, bs3, gv, bv, name):
    t = proj.shape[0]
    d = D_MODEL
    tm = _tile(t, ROW_TILE_HEAVY)
    hd = d // SGU_HEADS

    def body(zu_ref, zv_ref, d_ref, wm_ref, bs_ref, gv_ref, bv_ref, dz_ref, dwm_ref, dbs_ref, dgv_ref, dbv_ref,
             dvn_ref):
        @pl.when(pl.program_id(0) == 0)
        def _():
            dwm_ref[...] = jnp.zeros_like(dwm_ref)
            dbs_ref[...] = jnp.zeros_like(dbs_ref)
            dgv_ref[...] = jnp.zeros_like(dgv_ref)
            dbv_ref[...] = jnp.zeros_like(dbv_ref)

        mask = _sgu_mask()
        for blk in range(tm // SGU_BLOCK):
            rows = pl.ds(blk * SGU_BLOCK, SGU_BLOCK)
            u, du_dz = _gelu_and_grad(zu_ref[rows, :].astype(F32))
            v0, dv_dz = _gelu_and_grad(zv_ref[rows, :].astype(F32))
            xhat, rstd = _ln_stats(v0)
            vn = (xhat * gv_ref[...] + bv_ref[...]).astype(BF16)
            dsg = d_ref[rows, :].astype(F32)
            dmix = (dsg * u).astype(BF16)
            for hh in range(SGU_HEADS):
                cols = slice(hh * hd, (hh + 1) * hd)
                wmh = jnp.where(mask, wm_ref[hh], 0.0).astype(BF16)
                vb = vn[:, cols]
                mixed = jnp.dot(wmh, vb, preferred_element_type=F32) + bs_ref[hh]
                dz_ref[rows, cols] = (dsg[:, cols] * mixed * du_dz[:, cols]).astype(BF16)
                dmh = dmix[:, cols]
                dwm_ref[hh] += jnp.where(mask, _dot_nt(dmh, vb), 0.0)
                dbs_ref[hh] += jnp.sum(dmh.astype(F32), axis=1, keepdims=True)
                dvn_ref[:, cols] = _dot_tn(wmh, dmh)
            dvn = dvn_ref[...]
            dgv_ref[...] += _rowsum(dvn * xhat)
            dbv_ref[...] += _rowsum(dvn)
            dz_ref[rows, d:2 * d] = (_ln_bwd(xhat, rstd, gv_ref[...], dvn) * dv_dz).astype(BF16)

    return pl.pallas_call(
        body, name=name, grid=(t // tm,),
        in_specs=[pl.BlockSpec((tm, d), lambda i: (i, 0)), pl.BlockSpec((tm, d), lambda i: (i, 1)),
                  pl.BlockSpec((tm, d), lambda i: (i, 0)), _full(wm.shape), _full(bs3.shape), _full(gv.shape),
                  _full(bv.shape)],
        out_specs=[pl.BlockSpec((tm, 2 * d), lambda i: (i, 0)), _full(wm.shape), _full(bs3.shape), _full((1, d)),
                   _full((1, d))],
        out_shape=[jax.ShapeDtypeStruct((t, 2 * d), BF16), jax.ShapeDtypeStruct(wm.shape, F32),
                   jax.ShapeDtypeStruct(bs3.shape, F32), jax.ShapeDtypeStruct((1, d), F32),
                   jax.ShapeDtypeStruct((1, d), F32)],
        scratch_shapes=[pltpu.VMEM((SGU_BLOCK, d), F32)],
        compiler_params=_params(1))(proj, proj, dsg, wm, bs3, gv, bv)


def _conv_bwd_norm(proj, dcs, wdw, bdw, gln, bln, name):
    t = proj.shape[0]
    d = D_MODEL
    tm = _tile(t, ROW_TILE)
    main, halo = _conv_specs(t, tm, d)
    n_win = CONV_ROWS + CONV_HALO

    def body(a_ref, g_ref, ah_ref, gh_ref, dcs_ref, w_ref, b_ref, gl_ref, bl_ref,
             dcv_ref, dw_ref, db_ref, dgl_ref, dbl_ref, scr_ref, cv_ref, dwacc_ref):
        @pl.when(pl.program_id(0) == 0)
        def _():
            dwacc_ref[...] = jnp.zeros_like(dwacc_ref)
            db_ref[...] = jnp.zeros_like(db_ref)
            dgl_ref[...] = jnp.zeros_like(dgl_ref)
            dbl_ref[...] = jnp.zeros_like(dbl_ref)

        _fill_glu_history(scr_ref, a_ref, g_ref, ah_ref, gh_ref, tm)
        _conv_into(scr_ref, cv_ref, w_ref, tm, _CONV_BASE, False)
        xhat, rstd = _ln_stats(cv_ref[...] + b_ref[...])
        cn = xhat * gl_ref[...] + bl_ref[...]
        s = _sigmoid(cn)
        dcn = dcs_ref[...].astype(F32) * (s * (1.0 + cn * (1.0 - s)))
        dgl_ref[...] += _rowsum(dcn * xhat)
        dbl_ref[...] += _rowsum(dcn)
        dcv = _ln_bwd(xhat, rstd, gl_ref[...], dcn)
        db_ref[...] += _rowsum(dcv)
        dcv_ref[...] = dcv

        def chunk(ci, carry):
            r0 = pl.multiple_of(ci * CONV_ROWS, CONV_ROWS)
            for c0 in range(0, d, CONV_LANES):
                lanes = pl.ds(c0, CONV_LANES)
                win = scr_ref[pl.ds(r0, n_win), lanes]
                dchunk = dcv_ref[pl.ds(r0, CONV_ROWS), lanes]
                for r in range(8):
                    rolled = win if r == 0 else pltpu.roll(win, n_win - r, 0)
                    for q in range(n_win // 8):
                        k = 8 * q + r - _CONV_BASE
                        if 0 <= k < CONV_WIDTH and 8 * q + CONV_ROWS <= n_win:
                            prod = dchunk * rolled[8 * q:8 * q + CONV_ROWS]
                            part = prod[0:8]
                            for s8 in range(8, CONV_ROWS, 8):
                                part = part + prod[s8:s8 + 8]
                            dwacc_ref[pl.ds(8 * k, 8), lanes] += part
            return carry

        lax.fori_loop(0, tm // CONV_ROWS, chunk, 0)

        @pl.when(pl.program_id(0) == pl.num_programs(0) - 1)
        def _():
            dw_ref[...] = jnp.sum(dwacc_ref[...].reshape(CONV_HALO, 8, d), axis=1)

    row = pl.BlockSpec((tm, d), lambda i: (i, 0))
    vec = _full((1, d))
    return pl.pallas_call(
        body, name=name, grid=(t // tm,),
        in_specs=main + halo + [row, _full(wdw.shape), vec, vec, vec],
        out_specs=[row, _full((CONV_HALO, d)), vec, vec, vec],
        out_shape=[jax.ShapeDtypeStruct((t, d), F32), jax.ShapeDtypeStruct((CONV_HALO, d), F32)]
        + [jax.ShapeDtypeStruct((1, d), F32)] * 3,
        scratch_shapes=[pltpu.VMEM((tm + CONV_HALO, d), F32), pltpu.VMEM((tm, d), F32),
                        pltpu.VMEM((8 * CONV_HALO, d), F32)],
        compiler_params=_params(1))(proj, proj, proj, proj, dcs, wdw, bdw, gln, bln)


def _conv_bwd_taps(proj, dcv, wdw, name):
    t = proj.shape[0]
    d = D_MODEL
    tm = _tile(t, ROW_TILE)
    hb = tm // CONV_HALO
    last_halo = t // CONV_HALO - 1

    def body(a_ref, g_ref, dcv_ref, dnext_ref, w_ref, dz_ref, scr_ref, dh_ref):
        scr_ref[0:tm, :] = dcv_ref[...]
        is_last = pl.program_id(0) == pl.num_programs(0) - 1
        scr_ref[tm:tm + CONV_HALO, :] = jnp.where(is_last, 0.0, dnext_ref[...])
        _conv_into(scr_ref, dh_ref, w_ref, tm, 0, True)
        dglu = dh_ref[...]
        a = a_ref[...].astype(F32)
        s = _sigmoid(g_ref[...].astype(F32))
        dz_ref[:, 0:d] = (dglu * s).astype(BF16)
        dz_ref[:, d:2 * d] = (dglu * a * s * (1.0 - s)).astype(BF16)

    return pl.pallas_call(
        body, name=name, grid=(t // tm,),
        in_specs=[pl.BlockSpec((tm, d), lambda i: (i, 2)), pl.BlockSpec((tm, d), lambda i: (i, 3)),
                  pl.BlockSpec((tm, d), lambda i: (i, 0)),
                  pl.BlockSpec((CONV_HALO, d), lambda i: (jnp.minimum((i + 1) * hb, last_halo), 0)),
                  _full(wdw.shape)],
        out_specs=pl.BlockSpec((tm, 2 * d), lambda i: (i, 0)),
        out_shape=jax.ShapeDtypeStruct((t, 2 * d), BF16),
        scratch_shapes=[pltpu.VMEM((tm + CONV_HALO, d), F32), pltpu.VMEM((tm, d), F32)],
        compiler_params=_params(1))(proj, proj, dcv, dcv, wdw)


def _pool_bwd(proj, dps, wpool, spool, name):
    t = proj.shape[0]
    d = D_MODEL
    tm = _tile(t, ROW_TILE)
    hb = tm // POOL_HALO
    last_halo = t // POOL_HALO - 1
    ext = tm + POOL_HALO

    def body(z_ref, zh_ref, d_ref, dnext_ref, w_ref, s_ref, dz_ref, dw_ref, ds_ref, scr_ref, dext_ref, dq_ref):
        @pl.when(pl.program_id(0) == 0)
        def _():
            dw_ref[...] = jnp.zeros_like(dw_ref)
            ds_ref[...] = jnp.zeros_like(ds_ref)

        _pool_fill(scr_ref, z_ref, zh_ref, tm)
        t0 = pl.program_id(0) * tm
        is_last = pl.program_id(0) == pl.num_programs(0) - 1
        dext_ref[0:tm, :] = d_ref[...].astype(F32)
        dext_ref[tm:ext, :] = jnp.where(is_last, 0.0, dnext_ref[...].astype(F32))
        for gi, w in enumerate(POOL_WINDOWS):
            cols = slice(gi * POOL_GROUP, (gi + 1) * POOL_GROUP)
            dps_ext = dext_ref[:, cols]
            dpm_ext = (dps_ext * s_ref[:, cols]).astype(BF16)
            dpooled_ext = _dot_nt(dpm_ext, w_ref[gi])
            dq_ref[...] = dpooled_ext / _pool_count(t0, ext, w)
            acc = dq_ref[pl.ds(0, tm), :]
            for k in range(1, w):
                acc = acc + dq_ref[pl.ds(k, tm), :]
            dz_ref[:, cols] = (acc - dpooled_ext[0:tm]).astype(BF16)
            pooled = _pooled_group(scr_ref, gi, w, tm, t0).astype(BF16)
            pm = jnp.dot(pooled, w_ref[gi], preferred_element_type=F32)
            ds_ref[:, cols] += _rowsum(dps_ext[0:tm] * pm)
            dw_ref[gi] += _dot_tn(pooled, dpm_ext[0:tm])

    return pl.pallas_call(
        body, name=name, grid=(t // tm,),
        in_specs=_pool_specs(tm, d) + [pl.BlockSpec((tm, d), lambda i: (i, 0)),
                                       pl.BlockSpec((POOL_HALO, d), lambda i: (jnp.minimum((i + 1) * hb, last_halo), 0)),
                                       _full(wpool.shape), _full(spool.shape)],
        out_specs=[pl.BlockSpec((tm, d), lambda i: (i, 0)), _full(wpool.shape), _full((1, d))],
        out_shape=[jax.ShapeDtypeStruct((t, d), BF16), jax.ShapeDtypeStruct(wpool.shape, F32),
                   jax.ShapeDtypeStruct((1, d), F32)],
        scratch_shapes=[pltpu.VMEM((tm + POOL_HALO, d), F32), pltpu.VMEM((ext, d), F32),
                        pltpu.VMEM((ext, POOL_GROUP), F32)],
        compiler_params=_params(1))(proj, proj, dps, dps, wpool, spool)


ANY = pl.BlockSpec(memory_space=pl.ANY)


def _mesh_pos():
    x, y, c = lax.axis_index("x"), lax.axis_index("y"), lax.axis_index("c")
    chips = [(1 - x, y), (x, 1 - y), (1 - x, 1 - y)]
    return x, y, c, chips


def _rows_half(ref, c, rh):
    lead = (slice(None),) * (len(ref.shape) - 2)
    return ref.at[lead + (pl.ds(c * rh, rh), slice(None))]


def _all_gather(shards, name):
    n = len(shards)

    def body(*refs):
        src, dst = refs[:n], refs[n:2 * n]
        local_sem, send_sem, recv_sem = refs[2 * n:]
        x, y, c, chips = _mesh_pos()
        me = 2 * x + y
        sibling = (x, y, 1 - c)
        started = []
        for g in range(n):
            cp = pltpu.make_async_copy(src[g], dst[g].at[me], local_sem.at[g])
            cp.start()
            started.append(cp)
        rh = [s.shape[1] // 2 for s in shards]

        def ici(g, j):
            return pltpu.make_async_remote_copy(
                src_ref=_rows_half(src[g], c, rh[g]), dst_ref=_rows_half(dst[g].at[me], c, rh[g]),
                send_sem=send_sem.at[6 * g + j], recv_sem=recv_sem.at[6 * g + j],
                device_id=(*chips[j], c), device_id_type=MESH)

        def landed(g, j, half):
            kj = 2 * chips[j][0] + chips[j][1]
            return _rows_half(dst[g].at[kj], half, rh[g])

        def d2d(g, j):
            return pltpu.make_async_remote_copy(
                src_ref=landed(g, j, c), dst_ref=landed(g, j, c),
                send_sem=send_sem.at[6 * g + 3 + j], recv_sem=recv_sem.at[6 * g + 3 + j],
                device_id=sibling, device_id_type=MESH)

        sends = []
        for g in range(n):
            for j in range(3):
                cp = ici(g, j)
                cp.start()
                sends.append(cp)
        for j in range(3):
            for g in range(n):
                pltpu.make_async_remote_copy(
                    src_ref=landed(g, j, c), dst_ref=landed(g, j, c), send_sem=send_sem.at[6 * g + j],
                    recv_sem=recv_sem.at[6 * g + j], device_id=(*chips[j], c), device_id_type=MESH).wait_recv()
                cp = d2d(g, j)
                cp.start()
                sends.append(cp)
        for j in range(3):
            for g in range(n):
                pltpu.make_async_remote_copy(
                    src_ref=landed(g, j, 1 - c), dst_ref=landed(g, j, 1 - c), send_sem=send_sem.at[6 * g + 3 + j],
                    recv_sem=recv_sem.at[6 * g + 3 + j], device_id=sibling, device_id_type=MESH).wait_recv()
        for cp in sends:
            cp.wait_send()
        for cp in started:
            cp.wait()

    return pl.pallas_call(
        body, name=name, in_specs=[ANY] * n, out_specs=[ANY] * n,
        out_shape=[jax.ShapeDtypeStruct((N_CHIPS,) + s.shape, s.dtype) for s in shards],
        scratch_shapes=[pltpu.SemaphoreType.DMA((n,)), pltpu.SemaphoreType.DMA((6 * n,)),
                        pltpu.SemaphoreType.DMA((6 * n,))],
        compiler_params=pltpu.CompilerParams(has_side_effects=True))(*shards)


def _pair_exchange(grads, name):
    n = len(grads)

    def body(*refs):
        src, dst = refs[:n], refs[n:2 * n]
        send_sem, recv_sem = refs[2 * n:]
        x, y, c, _ = _mesh_pos()
        copies = []
        for g in range(n):
            rh = grads[g].shape[2] // 2
            cp = pltpu.make_async_remote_copy(
                src_ref=_rows_half(src[g], 1 - c, rh), dst_ref=dst[g], send_sem=send_sem.at[g],
                recv_sem=recv_sem.at[g], device_id=(x, y, 1 - c), device_id_type=MESH)
            cp.start()
            copies.append(cp)
        for cp in copies:
            cp.wait()

    return pl.pallas_call(
        body, name=name, in_specs=[ANY] * n, out_specs=[ANY] * n,
        out_shape=[jax.ShapeDtypeStruct(g.shape[:2] + (g.shape[2] // 2, g.shape[3]), g.dtype) for g in grads],
        scratch_shapes=[pltpu.SemaphoreType.DMA((n,)), pltpu.SemaphoreType.DMA((n,))],
        compiler_params=pltpu.CompilerParams(has_side_effects=True))(*grads)


def _chip_exchange(parts, name):
    n = len(parts)

    def body(*refs):
        src, dst = refs[:n], refs[n:2 * n]
        local_sem, send_sem, recv_sem = refs[2 * n:]
        x, y, c, chips = _mesh_pos()
        me = 2 * x + y
        local, sends = [], []
        for g in range(n):
            cp = pltpu.make_async_copy(src[g].at[me], dst[g].at[me], local_sem.at[g])
            cp.start()
            local.append(cp)
            for j in range(3):
                kj = 2 * chips[j][0] + chips[j][1]
                cp = pltpu.make_async_remote_copy(
                    src_ref=src[g].at[kj], dst_ref=dst[g].at[me], send_sem=send_sem.at[3 * g + j],
                    recv_sem=recv_sem.at[3 * g + j], device_id=(*chips[j], c), device_id_type=MESH)
                cp.start()
                sends.append(cp)
        for g in range(n):
            for j in range(3):
                kj = 2 * chips[j][0] + chips[j][1]
                pltpu.make_async_remote_copy(
                    src_ref=src[g].at[kj], dst_ref=dst[g].at[kj], send_sem=send_sem.at[3 * g + j],
                    recv_sem=recv_sem.at[3 * g + j], device_id=(*chips[j], c), device_id_type=MESH).wait_recv()
        for cp in sends:
            cp.wait_send()
        for cp in local:
            cp.wait()

    return pl.pallas_call(
        body, name=name, in_specs=[ANY] * n, out_specs=[ANY] * n,
        out_shape=[jax.ShapeDtypeStruct(p.shape, p.dtype) for p in parts],
        scratch_shapes=[pltpu.SemaphoreType.DMA((n,)), pltpu.SemaphoreType.DMA((3 * n,)),
                        pltpu.SemaphoreType.DMA((3 * n,))],
        compiler_params=pltpu.CompilerParams(has_side_effects=True))(*parts)


def _pair_share(halves, name):
    n = len(halves)

    def body(*refs):
        src, dst = refs[:n], refs[n:2 * n]
        local_sem, send_sem, recv_sem = refs[2 * n:]
        x, y, c, _ = _mesh_pos()
        copies = []
        for g in range(n):
            rh = halves[g].shape[1]
            cp = pltpu.make_async_copy(src[g], _rows_half(dst[g], c, rh), local_sem.at[g])
            cp.start()
            copies.append(cp)
            cp = pltpu.make_async_remote_copy(
                src_ref=src[g], dst_ref=_rows_half(dst[g], c, rh), send_sem=send_sem.at[g], recv_sem=recv_sem.at[g],
                device_id=(x, y, 1 - c), device_id_type=MESH)
            cp.start()
            copies.append(cp)
        for cp in copies:
            cp.wait()

    return pl.pallas_call(
        body, name=name, in_specs=[ANY] * n, out_specs=[ANY] * n,
        out_shape=[jax.ShapeDtypeStruct((h.shape[0], 2 * h.shape[1], h.shape[2]), h.dtype) for h in halves],
        scratch_shapes=[pltpu.SemaphoreType.DMA((n,)), pltpu.SemaphoreType.DMA((n,)), pltpu.SemaphoreType.DMA((n,))],
        compiler_params=pltpu.CompilerParams(has_side_effects=True))(*halves)


def _all_reduce_small(vec, name):
    r = vec.shape[0]

    def body(v_ref, o_ref, gath_ref, send_sem, recv_sem):
        x, y, c, _ = _mesh_pos()
        me = 4 * x + 2 * y + c
        gath_ref[me] = v_ref[...]
        copies = []
        for k in range(1, 8):
            peer = (x ^ (k >> 2), y ^ ((k >> 1) & 1), c ^ (k & 1))
            cp = pltpu.make_async_remote_copy(
                src_ref=v_ref, dst_ref=gath_ref.at[me], send_sem=send_sem.at[k - 1], recv_sem=recv_sem.at[k - 1],
                device_id=peer, device_id_type=MESH)
            cp.start()
            copies.append(cp)
        for k in range(1, 8):
            src_id = me ^ k
            pltpu.make_async_remote_copy(
                src_ref=v_ref, dst_ref=gath_ref.at[src_id], send_sem=send_sem.at[k - 1], recv_sem=recv_sem.at[k - 1],
                device_id=(x, y, c), device_id_type=MESH).wait_recv()
        for cp in copies:
            cp.wait_send()
        acc = gath_ref[0]
        for k in range(1, 8):
            acc = acc + gath_ref[k]
        o_ref[...] = acc

    return pl.pallas_call(
        body, name=name,
        in_specs=[pl.BlockSpec(memory_space=pltpu.VMEM)], out_specs=pl.BlockSpec(memory_space=pltpu.VMEM),
        out_shape=jax.ShapeDtypeStruct(vec.shape, F32),
        scratch_shapes=[pltpu.VMEM((8, r, 128), F32), pltpu.SemaphoreType.DMA((7,)), pltpu.SemaphoreType.DMA((7,))],
        compiler_params=pltpu.CompilerParams(has_side_effects=True, vmem_limit_bytes=VMEM_LIMIT))(vec)


def _row_block(rows, cols, mult=16):
    best = None
    for cand in range(mult, rows + 1, mult):
        if rows % cand == 0 and cand * cols * 4 <= EW_BLOCK_BYTES:
            best = cand
    return best or rows


def _pair_sum(grad, recv, c_arr, out_dtype, name):
    _, s, rh, cols = recv.shape
    tr = _row_block(rh, cols)
    nb = rh // tr

    def body(c_ref, g_ref, r_ref, o_ref):
        o_ref[...] = (g_ref[...] + r_ref[...]).astype(out_dtype)

    blk = (None, None, tr, cols)
    return pl.pallas_call(
        body, name=name,
        grid_spec=pltpu.PrefetchScalarGridSpec(
            num_scalar_prefetch=1, grid=(N_CHIPS, s, nb),
            in_specs=[pl.BlockSpec(blk, lambda a, b, i, c_ref: (a, b, c_ref[0] * nb + i, 0)),
                      pl.BlockSpec(blk, lambda a, b, i, c_ref: (a, b, i, 0))],
            out_specs=pl.BlockSpec(blk, lambda a, b, i, c_ref: (a, b, i, 0))),
        out_shape=jax.ShapeDtypeStruct(recv.shape, out_dtype),
        compiler_params=_params(3))(c_arr, grad, recv)


def _chip_sum(parts, name):
    _, s, rh, cols = parts.shape
    tr = _row_block(rh, cols)

    def body(p_ref, o_ref):
        acc = p_ref[0].astype(F32)
        for k in range(1, N_CHIPS):
            acc = acc + p_ref[k].astype(F32)
        o_ref[...] = acc

    return pl.pallas_call(
        body, name=name, grid=(s, rh // tr),
        in_specs=[pl.BlockSpec((N_CHIPS, None, tr, cols), lambda b, i: (0, b, i, 0))],
        out_specs=pl.BlockSpec((None, tr, cols), lambda b, i: (b, i, 0)),
        out_shape=jax.ShapeDtypeStruct((s, rh, cols), F32),
        compiler_params=_params(2))(parts)


def _adamw_math(w, g, m, v):
    m = ADAM_B1 * m + (1.0 - ADAM_B1) * g
    v = ADAM_B2 * v + (1.0 - ADAM_B2) * (g * g)
    m_hat = m / (1.0 - ADAM_B1 ** ADAM_STEP)
    v_hat = v / (1.0 - ADAM_B2 ** ADAM_STEP)
    delta = -ADAM_LR * (m_hat / (jnp.sqrt(v_hat) + ADAM_EPS) + ADAM_WD * w)
    return delta, m, v


def _adamw(w, g, m, v, name):
    l, rows, cols = w.shape
    tr = _row_block(rows, cols, 8)

    def body(w_ref, g_ref, m_ref, v_ref, go_ref, d_ref, mo_ref, vo_ref):
        g_ = g_ref[...]
        delta, m_, v_ = _adamw_math(w_ref[...], g_, m_ref[...], v_ref[...])
        go_ref[...] = g_
        d_ref[...] = delta
        mo_ref[...] = m_
        vo_ref[...] = v_

    blk = pl.BlockSpec((None, tr, cols), lambda a, i: (a, i, 0))
    return pl.pallas_call(
        body, name=name, grid=(l, rows // tr), in_specs=[blk] * 4, out_specs=[blk] * 4,
        out_shape=[jax.ShapeDtypeStruct(w.shape, F32)] * 4,
        compiler_params=_params(2))(w, g, m, v)


SQ = ("w_sgu_out", "w_conv_out", "w_pool_out", "w_out", "w_ple_gate")
SMALL = ("g_mix_pre", "w_sgu_s", "b_sgu_s", "g_sgu_v", "b_sgu_v", "b_dw", "g_conv_ln", "b_conv_ln", "s_pool",
         "g_mix_post", "g_ffn_pre", "g_ffn_post")


def _layer_weights(gath, small, li):
    w = {}
    w["w_in"] = gath["in"].reshape(N_CHIPS, D_MODEL, -1)
    for slot, nm in enumerate(SQ):
        w[nm] = gath["sq"][:, slot].reshape(D_MODEL, D_MODEL)
    w["w_ffn_in"] = gath["ffn_in"].reshape(N_CHIPS, D_MODEL, -1)
    w["w_ffn_out"] = gath["ffn_out"].reshape(D_FF, D_MODEL)
    mix = gath["mix"]
    w["w_ple"] = mix[:, 0].transpose(1, 0, 2).reshape(256, D_MODEL)
    w["w_pool"] = mix[:, 1].reshape(N_CHIPS, 4, 64, 256).transpose(1, 0, 2, 3).reshape(4, 256, 256)
    w["w_dw"] = gath["dw"].reshape(N_CHIPS, CONV_HALO, -1).transpose(1, 0, 2).reshape(CONV_HALO, D_MODEL)
    for nm in SMALL:
        w[nm] = small[nm][li]
    return w


def _vec(a):
    return a.reshape(1, -1)


def _layer_fwd(h, p, w, li):
    s = {}
    tag = "_l%d" % li
    s["h0"] = h
    proj, hn = _norm_mm(h, _vec(w["g_mix_pre"]), w["w_in"], "mix_in" + tag)
    s["proj"], s["hn"] = proj, hn
    bs3 = w["b_sgu_s"].reshape(SGU_HEADS, SGU_BLOCK, 1)
    s["sg"] = _sgu_fwd(proj, w["w_sgu_s"], bs3, _vec(w["g_sgu_v"]), _vec(w["b_sgu_v"]), "sgu_fwd" + tag)
    s["cs"] = _conv_fwd(proj, w["w_dw"], _vec(w["b_dw"]), _vec(w["g_conv_ln"]), _vec(w["b_conv_ln"]),
                        "conv_fwd" + tag)
    s["ps"] = _pool_fwd(proj, w["w_pool"], _vec(w["s_pool"]), "pool_fwd" + tag)
    s["bra"], s["brb"], s["brc"], s["merged"] = _merge_fwd(
        proj, s["sg"], s["cs"], s["ps"], w["w_sgu_out"], w["w_conv_out"], w["w_pool_out"], "merge_fwd" + tag)
    s["mo"], h1 = _mm_norm_res(s["merged"], w["w_out"], _vec(w["g_mix_post"]), h, "mix_out" + tag)
    s["h1"] = h1
    s["fg"], s["fu"], s["act"], s["hn2"] = _ffn_in(h1, _vec(w["g_ffn_pre"]), w["w_ffn_in"], "ffn_in" + tag)
    s["f"], h2 = _mm_norm_res(s["act"], w["w_ffn_out"], _vec(w["g_ffn_post"]), h1, "ffn_out" + tag)
    s["h2"] = h2
    h3, s["q"], s["e"] = _ple_fwd(h2, p, w["w_ple_gate"], w["w_ple"], "ple_fwd" + tag)
    return h3, s


def _layer_bwd(dh3, p, w, s, li):
    tag = "_l%d" % li
    d = D_MODEL
    gs = {}
    dq, de, dh2 = _ple_bwd(dh3, s["q"], s["e"], w["w_ple_gate"], "ple_bwd" + tag)
    dw_ple = _mm_tn(p, de, "dw_ple" + tag)
    dw_ple_gate = _mm_tn(s["h2"], dq, "dw_ple_gate" + tag)
    df, dff, gs["g_ffn_post"] = _ffn_out_bwd(dh2, s["f"], _vec(w["g_ffn_post"]), s["fg"], s["fu"], w["w_ffn_out"],
                                             "ffn_out_bwd" + tag)
    dw_ffn_out = _mm_tn(s["act"], df, "dw_ffn_out" + tag)
    n_ff = w["w_ffn_in"].shape[2]
    dh1, gs["g_ffn_pre"] = _in_bwd([(dff, 2 * D_FF // n_ff)], w["w_ffn_in"], n_ff, s["h1"], _vec(w["g_ffn_pre"]),
                                   dh2, "ffn_in_bwd" + tag)
    dw_ffn_in = _mm_tn(s["hn2"], dff, "dw_ffn_in" + tag, bn=n_ff)
    (dmo, dbra, dbrb, dbrc, dzg, dsg, dcs, dps, gs["g_mix_post"]) = _merge_bwd(
        dh1, s["mo"], _vec(w["g_mix_post"]), s["proj"], s["bra"], s["brb"], s["brc"], w["w_out"], w["w_sgu_out"],
        w["w_conv_out"], w["w_pool_out"], "merge_bwd" + tag)
    dw_out = _mm_tn(s["merged"], dmo, "dw_out" + tag)
    dw_sgu_out = _mm_tn(s["sg"], dbra, "dw_sgu_out" + tag)
    dw_conv_out = _mm_tn(s["cs"], dbrb, "dw_conv_out" + tag)
    dw_pool_out = _mm_tn(s["ps"], dbrc, "dw_pool_out" + tag)
    bs3 = w["b_sgu_s"].reshape(SGU_HEADS, SGU_BLOCK, 1)
    dz_sgu, gs["w_sgu_s"], dbs3, gs["g_sgu_v"], gs["b_sgu_v"] = _sgu_bwd(
        s["proj"], dsg, w["w_sgu_s"], bs3, _vec(w["g_sgu_v"]), _vec(w["b_sgu_v"]), "sgu_bwd" + tag)
    gs["b_sgu_s"] = dbs3
    dcv, dwdw, gs["b_dw"], gs["g_conv_ln"], gs["b_conv_ln"] = _conv_bwd_norm(
        s["proj"], dcs, w["w_dw"], _vec(w["b_dw"]), _vec(w["g_conv_ln"]), _vec(w["b_conv_ln"]), "conv_bwd_norm" + tag)
    dz_conv = _conv_bwd_taps(s["proj"], dcv, w["w_dw"], "conv_bwd_taps" + tag)
    dz_pool, dwpool, gs["s_pool"] = _pool_bwd(s["proj"], dps, w["w_pool"], _vec(w["s_pool"]), "pool_bwd" + tag)
    pieces = [(dz_sgu, 2), (dz_conv, 2), (dz_pool, 1), (dzg, 3)]
    dh0, gs["g_mix_pre"] = _in_bwd(pieces, w["w_in"], d, s["h0"], _vec(w["g_mix_pre"]), dh1, "mix_in_bwd" + tag)
    dw_in = jnp.concatenate([_mm_tn(s["hn"], dz, "dw_in%d" % k + tag) for k, (dz, _) in enumerate(pieces)], axis=1)

    big = {}
    big["in"] = dw_in.reshape(d, N_CHIPS, 2 * d).transpose(1, 0, 2)[:, None]
    sq = {"w_sgu_out": dw_sgu_out, "w_conv_out": dw_conv_out, "w_pool_out": dw_pool_out, "w_out": dw_out,
          "w_ple_gate": dw_ple_gate}
    big["sq"] = jnp.stack([sq[nm].reshape(N_CHIPS, d // N_CHIPS, d) for nm in SQ], axis=1)
    big["ffn_in"] = dw_ffn_in.reshape(d, N_CHIPS, n_ff).transpose(1, 0, 2)[:, None]
    big["ffn_out"] = dw_ffn_out.reshape(N_CHIPS, 1, D_FF // N_CHIPS, d)
    gple = dw_ple.reshape(256, N_CHIPS, 256).transpose(1, 0, 2)
    gpool = dwpool.reshape(4, N_CHIPS, 64, 256).transpose(1, 0, 2, 3).reshape(N_CHIPS, 256, 256)
    big["mix"] = jnp.stack([gple, gpool], axis=1)
    big["dw"] = dwdw.reshape(CONV_HALO, N_CHIPS, 256).transpose(1, 0, 2)[:, None]
    return dh0, big, gs


GROUPS = ("in", "sq", "ffn_in", "ffn_out", "mix", "dw")
WIRE_DTYPE = {"in": BF16, "sq": BF16, "ffn_in": BF16, "ffn_out": BF16, "mix": BF16, "dw": F32}


def _local_shards(wts, li):
    sh = {}
    sh["in"] = wts["w_in"][li][None].astype(BF16)
    sh["sq"] = jnp.stack([wts[nm][li] for nm in SQ], axis=0).astype(BF16)
    sh["ffn_in"] = wts["w_ffn_in"][li][None].astype(BF16)
    sh["ffn_out"] = wts["w_ffn_out"][li][None].astype(BF16)
    sh["mix"] = jnp.stack([wts["w_ple"][li], wts["w_pool"][li].reshape(256, 256)], axis=0).astype(BF16)
    dw = wts["w_dw"][li].reshape(CONV_WIDTH, 256)
    sh["dw"] = jnp.pad(dw, ((0, CONV_HALO - CONV_WIDTH), (0, 0)))[None]
    return sh


def _pack_small(tree):
    flat = jnp.concatenate([tree[nm].reshape(-1).astype(F32) for nm in SMALL])
    return flat.reshape(-1, 128)


def _unpack_small(packed, like):
    out, off = {}, 0
    flat = packed.reshape(-1)
    for nm in SMALL:
        n = like[nm].size
        out[nm] = flat[off:off + n].reshape(like[nm].shape)
        off += n
    return out


WEIGHTS = ("g_mix_pre", "w_in", "w_sgu_s", "b_sgu_s", "g_sgu_v", "b_sgu_v", "w_sgu_out", "w_dw", "b_dw", "g_conv_ln",
           "b_conv_ln", "w_conv_out", "w_pool", "s_pool", "w_pool_out", "w_out", "g_mix_post", "g_ffn_pre",
           "w_ffn_in", "w_ffn_out", "g_ffn_post", "w_ple", "w_ple_gate")


def kernel(x, p, g_mix_pre, w_in, w_sgu_s, b_sgu_s, g_sgu_v, b_sgu_v, w_sgu_out, w_dw, b_dw, g_conv_ln, b_conv_ln, w_conv_out, w_pool, s_pool, w_pool_out, w_out, g_mix_post, g_ffn_pre, w_ffn_in, w_ffn_out, g_ffn_post, w_ple, w_ple_gate, loss_target, m_g_mix_pre, m_w_in, m_w_sgu_s, m_b_sgu_s, m_g_sgu_v, m_b_sgu_v, m_w_sgu_out, m_w_dw, m_b_dw, m_g_conv_ln, m_b_conv_ln, m_w_conv_out, m_w_pool, m_s_pool, m_w_pool_out, m_w_out, m_g_mix_post, m_g_ffn_pre, m_w_ffn_in, m_w_ffn_out, m_g_ffn_post, m_w_ple, m_w_ple_gate, v_g_mix_pre, v_w_in, v_w_sgu_s, v_b_sgu_s, v_g_sgu_v, v_b_sgu_v, v_w_sgu_out, v_w_dw, v_b_dw, v_g_conv_ln, v_b_conv_ln, v_w_conv_out, v_w_pool, v_s_pool, v_w_pool_out, v_w_out, v_g_mix_post, v_g_ffn_pre, v_w_ffn_in, v_w_ffn_out, v_g_ffn_post, v_w_ple, v_w_ple_gate):
    args = dict(locals())
    wts = {nm: args[nm] for nm in WEIGHTS}
    mom = {nm: args["m_" + nm] for nm in WEIGHTS}
    var = {nm: args["v_" + nm] for nm in WEIGHTS}
    n_layers = w_in.shape[0]
    h = x.reshape(x.shape[1:])
    target = loss_target.reshape(loss_target.shape[1:])
    c_arr = lax.axis_index("c").astype(jnp.int32).reshape(1)

    gathered = []
    for li in range(n_layers):
        sh = _local_shards(wts, li)
        out = _all_gather([sh[g] for g in GROUPS], "all_gather_l%d" % li)
        gathered.append(dict(zip(GROUPS, out)))

    saved, layer_w = [], []
    for li in range(n_layers):
        w = _layer_weights(gathered[li], wts, li)
        layer_w.append(w)
        h, s = _layer_fwd(h, p[li, 0], w, li)
        saved.append(s)
    dh, sq_err = _loss_head(h, target, "loss_head")
    loss = lax.psum(sq_err[0, 0] * (0.5 / D_MODEL), ("x", "y", "c"))

    small_grads = [None] * n_layers
    halves = [None] * n_layers
    for li in reversed(range(n_layers)):
        dh, big, small_grads[li] = _layer_bwd(dh, p[li, 0], layer_w[li], saved[li], li)
        tag = "_l%d" % li
        grads = [big[g] for g in GROUPS]
        recv = _pair_exchange(grads, "pair_exchange" + tag)
        parts = [_pair_sum(gr, rv, c_arr, WIRE_DTYPE[g], "pair_sum_%s" % g + tag)
                 for g, gr, rv in zip(GROUPS, grads, recv)]
        landed = _chip_exchange(parts, "chip_exchange" + tag)
        halves[li] = [_chip_sum(ld, "chip_sum_%s" % g + tag) for g, ld in zip(GROUPS, landed)]
    grad_x = dh[None]

    shared = _pair_share([hv for li in range(n_layers) for hv in halves[li]], "pair_share")
    ng = len(GROUPS)
    red = {g: jnp.stack([shared[li * ng + k] for li in range(n_layers)], axis=0) for k, g in enumerate(GROUPS)}

    gbig = {"w_in": red["in"][:, 0], "w_ffn_in": red["ffn_in"][:, 0], "w_ffn_out": red["ffn_out"][:, 0],
            "w_ple": red["mix"][:, 0], "w_pool": red["mix"][:, 1], "w_dw": red["dw"][:, 0, :CONV_WIDTH]}
    for slot, nm in enumerate(SQ):
        gbig[nm] = red["sq"][:, slot]

    outs = {}
    for nm, g in gbig.items():
        shape = wts[nm].shape
        g3 = g.reshape(n_layers, g.shape[1], g.shape[2])
        to3 = lambda a: a.reshape(g3.shape)
        res = _adamw(to3(wts[nm]), g3, to3(mom[nm]), to3(var[nm]), "adamw_" + nm)
        outs[nm] = [r.reshape(shape) for r in res]

    small_tree = {nm: jnp.stack([small_grads[li][nm].reshape(wts[nm].shape[1:]) for li in range(n_layers)], axis=0)
                  for nm in SMALL}
    gsmall = _all_reduce_small(_pack_small(small_tree), "all_reduce_small")
    pk = lambda tree: _pack_small({nm: tree[nm] for nm in SMALL})[None]
    res = _adamw(pk(wts), gsmall[None], pk(mom), pk(var), "adamw_small")
    unpacked = [_unpack_small(r[0], wts) for r in res]
    for nm in SMALL:
        outs[nm] = [u[nm] for u in unpacked]

    result = [loss, grad_x]
    for k in range(4):
        result += [outs[nm][k] for nm in WEIGHTS]
    return tuple(result)
```

```python
import functools

import jax
import jax.numpy as jnp
from jax import lax
from jax.experimental import pallas as pl
from jax.experimental.pallas import tpu as pltpu

F32 = jnp.float32
BF16 = jnp.bfloat16
MESH = pl.DeviceIdType.MESH

EPS = 1e-6
D_MODEL = 1024
SGU_BLOCK = 128
SGU_HEADS = 8
CHUNK = 64
CONV_WIDTH = 31
CONV_HALO = 32
POOL_WINDOWS = (2, 4, 8, 16)
POOL_HALO = 16
POOL_GROUP = 256
D_FF = 2816
N_CHIPS = 4

ADAM_LR = 0.001
ADAM_B1 = 0.9
ADAM_B2 = 0.999
ADAM_EPS = 1e-08
ADAM_WD = 0.01
ADAM_STEP = 10

VMEM_LIMIT = 52 * 1024 * 1024
ROW_TILE = 512
ROW_TILE_HEAVY = 256
CONV_ROWS = 32
CONV_LANES = 256
EW_BLOCK_BYTES = 2 * 1024 * 1024


def _params(n_grid):
    return pltpu.CompilerParams(dimension_semantics=("arbitrary",) * n_grid, vmem_limit_bytes=VMEM_LIMIT)


def _dot(a, b):
    return jnp.dot(a.astype(BF16), b.astype(BF16), preferred_element_type=F32)


def _dot_nt(a, b):
    return lax.dot_general(a.astype(BF16), b.astype(BF16), (((1,), (1,)), ((), ())), preferred_element_type=F32)


def _dot_tn(a, b):
    return lax.dot_general(a.astype(BF16), b.astype(BF16), (((0,), (0,)), ((), ())), preferred_element_type=F32)


def _sigmoid(x):
    return 1.0 / (1.0 + jnp.exp(-x))


_GELU_C = 0.7978845608028654
_GELU_A = 0.044715


def _gelu(x):
    t = jnp.tanh(_GELU_C * (x + _GELU_A * x * x * x))
    return 0.5 * x * (1.0 + t)


def _gelu_and_grad(x):
    x2 = x * x
    t = jnp.tanh(_GELU_C * (x + _GELU_A * x2 * x))
    g = 0.5 * (1.0 + t) + 0.5 * x * (1.0 - t * t) * (_GELU_C * (1.0 + 3.0 * _GELU_A * x2))
    return 0.5 * x * (1.0 + t), g


def _rms_stats(x):
    r = lax.rsqrt(jnp.mean(x * x, axis=-1, keepdims=True) + EPS)
    return x * r, r


def _rms_bwd(xn, r, g, dy):
    gd = dy * g
    return r * (gd - xn * jnp.mean(gd * xn, axis=-1, keepdims=True)), dy * xn


def _ln_stats(x):
    mu = jnp.mean(x, axis=-1, keepdims=True)
    xc = x - mu
    rstd = lax.rsqrt(jnp.mean(xc * xc, axis=-1, keepdims=True) + EPS)
    return xc * rstd, rstd


def _ln_bwd(xhat, rstd, g, dy):
    dxh = dy * g
    return rstd * (dxh - jnp.mean(dxh, axis=-1, keepdims=True) - xhat * jnp.mean(dxh * xhat, axis=-1, keepdims=True))


def _rowsum(x):
    return jnp.sum(x, axis=0, keepdims=True)


def _tile(t, want):
    return min(t, want)


def _full(shape):
    n = len(shape)
    return pl.BlockSpec(shape, lambda *_: (0,) * n)


def _resident(shape):
    n = len(shape)
    return pl.BlockSpec(shape, lambda *_: (0,) * n, pipeline_mode=pl.Buffered(1))


def _norm_mm(h, g, w4, name):
    t, d = h.shape
    n = w4.shape[2]
    tm = _tile(t, ROW_TILE)

    def body(h_ref, g_ref, w_ref, o_ref, hn_ref):
        @pl.when(pl.program_id(1) == 0)
        def _():
            xn, _ = _rms_stats(h_ref[...])
            hn_ref[...] = (xn * g_ref[...]).astype(BF16)

        o_ref[...] = jnp.dot(hn_ref[...], w_ref[...], preferred_element_type=F32).astype(BF16)

    return pl.pallas_call(
        body, name=name, grid=(t // tm, N_CHIPS),
        in_specs=[pl.BlockSpec((tm, d), lambda i, j: (i, 0)), pl.BlockSpec((1, d), lambda i, j: (0, 0)),
                  pl.BlockSpec((None, d, n), lambda i, j: (j, 0, 0))],
        out_specs=[pl.BlockSpec((tm, n), lambda i, j: (i, j)), pl.BlockSpec((tm, d), lambda i, j: (i, 0))],
        out_shape=[jax.ShapeDtypeStruct((t, N_CHIPS * n), BF16), jax.ShapeDtypeStruct((t, d), BF16)],
        compiler_params=_params(2))(h, g, w4)


def _sgu_mask():
    ii = lax.broadcasted_iota(jnp.int32, (SGU_BLOCK, SGU_BLOCK), 0) // CHUNK
    jj = lax.broadcasted_iota(jnp.int32, (SGU_BLOCK, SGU_BLOCK), 1) // CHUNK
    return jj <= ii


def _sgu_fwd(proj, wm, bs3, gv, bv, name):
    t = proj.shape[0]
    d = D_MODEL
    tm = _tile(t, ROW_TILE)
    hd = d // SGU_HEADS

    def body(zu_ref, zv_ref, wm_ref, bs_ref, gv_ref, bv_ref, o_ref):
        mask = _sgu_mask()
        for blk in range(tm // SGU_BLOCK):
            rows = pl.ds(blk * SGU_BLOCK, SGU_BLOCK)
            u = _gelu(zu_ref[rows, :].astype(F32))
            xhat, _ = _ln_stats(_gelu(zv_ref[rows, :].astype(F32)))
            vn = (xhat * gv_ref[...] + bv_ref[...]).astype(BF16)
            for hh in range(SGU_HEADS):
                cols = slice(hh * hd, (hh + 1) * hd)
                wmh = jnp.where(mask, wm_ref[hh], 0.0).astype(BF16)
                mixed = jnp.dot(wmh, vn[:, cols], preferred_element_type=F32) + bs_ref[hh]
                o_ref[rows, cols] = (u[:, cols] * mixed).astype(BF16)

    return pl.pallas_call(
        body, name=name, grid=(t // tm,),
        in_specs=[pl.BlockSpec((tm, d), lambda i: (i, 0)), pl.BlockSpec((tm, d), lambda i: (i, 1)),
                  _full(wm.shape), _full(bs3.shape), _full(gv.shape), _full(bv.shape)],
        out_specs=pl.BlockSpec((tm, d), lambda i: (i, 0)),
        out_shape=jax.ShapeDtypeStruct((t, d), BF16),
        compiler_params=_params(1))(proj, proj, wm, bs3, gv, bv)


def _conv_taps(scr_ref, r0, c0, base, weight):
    n = CONV_ROWS + CONV_HALO
    win = scr_ref[pl.ds(r0, n), pl.ds(c0, CONV_LANES)]
    acc = None
    for r in range(8):
        rolled = win if r == 0 else pltpu.roll(win, n - r, 0)
        for q in range((CONV_HALO + 7) // 8 + 1):
            k = 8 * q + r - base
            if 0 <= k < CONV_WIDTH and 8 * q + CONV_ROWS <= n:
                term = weight(k) * rolled[8 * q:8 * q + CONV_ROWS]
                acc = term if acc is None else acc + term
    return acc


def _glu_rows(a_ref, g_ref):
    return a_ref[...].astype(F32) * _sigmoid(g_ref[...].astype(F32))


def _conv_into(scr_ref, cv_ref, w_ref, tm, base, flip):
    def chunk(ci, carry):
        r0 = pl.multiple_of(ci * CONV_ROWS, CONV_ROWS)
        for c0 in range(0, D_MODEL, CONV_LANES):
            def weight(k, c0=c0):
                kk = CONV_WIDTH - 1 - k if flip else k
                return w_ref[kk:kk + 1, c0:c0 + CONV_LANES]
            cv_ref[pl.ds(r0, CONV_ROWS), pl.ds(c0, CONV_LANES)] = _conv_taps(scr_ref, r0, c0, base, weight)
        return carry

    lax.fori_loop(0, tm // CONV_ROWS, chunk, 0)


def _conv_specs(t, tm, d):
    hb = tm // CONV_HALO
    main = [pl.BlockSpec((tm, d), lambda i: (i, 2)), pl.BlockSpec((tm, d), lambda i: (i, 3))]
    halo = [pl.BlockSpec((CONV_HALO, d), lambda i: (jnp.maximum(i * hb - 1, 0), 2)),
            pl.BlockSpec((CONV_HALO, d), lambda i: (jnp.maximum(i * hb - 1, 0), 3))]
    return main, halo


def _fill_glu_history(scr_ref, a_ref, g_ref, ah_ref, gh_ref, tm):
    hist = _glu_rows(ah_ref, gh_ref)
    scr_ref[0:CONV_HALO, :] = jnp.where(pl.program_id(0) > 0, hist, 0.0)
    scr_ref[CONV_HALO:CONV_HALO + tm, :] = _glu_rows(a_ref, g_ref)


_CONV_BASE = CONV_HALO - (CONV_WIDTH - 1)


def _conv_fwd(proj, wdw, bdw, gln, bln, name):
    t = proj.shape[0]
    d = D_MODEL
    tm = _tile(t, ROW_TILE)
    main, halo = _conv_specs(t, tm, d)

    def body(a_ref, g_ref, ah_ref, gh_ref, w_ref, b_ref, gl_ref, bl_ref, o_ref, scr_ref, cv_ref):
        _fill_glu_history(scr_ref, a_ref, g_ref, ah_ref, gh_ref, tm)
        _conv_into(scr_ref, cv_ref, w_ref, tm, _CONV_BASE, False)
        xhat, _ = _ln_stats(cv_ref[...] + b_ref[...])
        cn = xhat * gl_ref[...] + bl_ref[...]
        o_ref[...] = (cn * _sigmoid(cn)).astype(BF16)

    return pl.pallas_call(
        body, name=name, grid=(t // tm,),
        in_specs=main + halo + [_full(wdw.shape), _full(bdw.shape), _full(gln.shape), _full(bln.shape)],
        out_specs=pl.BlockSpec((tm, d), lambda i: (i, 0)),
        out_shape=jax.ShapeDtypeStruct((t, d), BF16),
        scratch_shapes=[pltpu.VMEM((tm + CONV_HALO, d), F32), pltpu.VMEM((tm, d), F32)],
        compiler_params=_params(1))(proj, proj, proj, proj, wdw, bdw, gln, bln)


def _pool_fill(scr_ref, z_ref, zh_ref, tm):
    scr_ref[0:POOL_HALO, :] = jnp.where(pl.program_id(0) > 0, zh_ref[...].astype(F32), 0.0)
    scr_ref[POOL_HALO:POOL_HALO + tm, :] = z_ref[...].astype(F32)


def _pool_count(t0, rows, w):
    pos = (t0 + lax.broadcasted_iota(jnp.int32, (rows, 1), 0) + 1).astype(F32)
    return jnp.minimum(pos, float(w))


def _pooled_group(scr_ref, gi, w, tm, t0):
    cols = pl.ds(gi * POOL_GROUP, POOL_GROUP)
    acc = scr_ref[pl.ds(POOL_HALO, tm), cols]
    z = acc
    for k in range(1, w):
        acc = acc + scr_ref[pl.ds(POOL_HALO - k, tm), cols]
    return acc / _pool_count(t0, tm, w) - z


def _pool_specs(tm, d):
    hb = tm // POOL_HALO
    return [pl.BlockSpec((tm, d), lambda i: (i, 4)),
            pl.BlockSpec((POOL_HALO, d), lambda i: (jnp.maximum(i * hb - 1, 0), 4))]


def _pool_fwd(proj, wpool, spool, name):
    t = proj.shape[0]
    d = D_MODEL
    tm = _tile(t, ROW_TILE)

    def body(z_ref, zh_ref, w_ref, s_ref, o_ref, scr_ref):
        _pool_fill(scr_ref, z_ref, zh_ref, tm)
        t0 = pl.program_id(0) * tm
        for gi, w in enumerate(POOL_WINDOWS):
            cols = slice(gi * POOL_GROUP, (gi + 1) * POOL_GROUP)
            pooled = _pooled_group(scr_ref, gi, w, tm, t0)
            o_ref[:, cols] = (_dot(pooled, w_ref[gi]) * s_ref[:, cols]).astype(BF16)

    return pl.pallas_call(
        body, name=name, grid=(t // tm,),
        in_specs=_pool_specs(tm, d) + [_full(wpool.shape), _full(spool.shape)],
        out_specs=pl.BlockSpec((tm, d), lambda i: (i, 0)),
        out_shape=jax.ShapeDtypeStruct((t, d), BF16),
        scratch_shapes=[pltpu.VMEM((tm + POOL_HALO, d), F32)],
        compiler_params=_params(1))(proj, proj, wpool, spool)


def _merge_fwd(proj, sg, cs, ps, wa, wb, wc, name):
    t = proj.shape[0]
    d = D_MODEL
    tm = _tile(t, ROW_TILE_HEAVY)

    def body(za_ref, zb_ref, zc_ref, sg_ref, cs_ref, ps_ref, wa_ref, wb_ref, wc_ref, ba_ref, bb_ref, bc_ref, m_ref):
        merged = None
        for z_ref, x_ref, w_ref, b_ref in ((za_ref, sg_ref, wa_ref, ba_ref), (zb_ref, cs_ref, wb_ref, bb_ref),
                                           (zc_ref, ps_ref, wc_ref, bc_ref)):
            br = jnp.dot(x_ref[...], w_ref[...], preferred_element_type=F32)
            b_ref[...] = br.astype(BF16)
            term = _sigmoid(z_ref[...].astype(F32)) * br
            merged = term if merged is None else merged + term
        m_ref[...] = merged.astype(BF16)

    row = pl.BlockSpec((tm, d), lambda i: (i, 0))
    wspec = _resident((d, d))
    return pl.pallas_call(
        body, name=name, grid=(t // tm,),
        in_specs=[pl.BlockSpec((tm, d), lambda i: (i, 5)), pl.BlockSpec((tm, d), lambda i: (i, 6)),
                  pl.BlockSpec((tm, d), lambda i: (i, 7)), row, row, row, wspec, wspec, wspec],
        out_specs=[row, row, row, row],
        out_shape=[jax.ShapeDtypeStruct((t, d), BF16)] * 4,
        compiler_params=_params(1))(proj, proj, proj, sg, cs, ps, wa, wb, wc)


def _mm_norm_res(a, w, g, hres, name):
    t, k = a.shape
    d = w.shape[1]
    tm = _tile(t, ROW_TILE)

    def body(a_ref, w_ref, g_ref, h_ref, y_ref, o_ref):
        y = jnp.dot(a_ref[...], w_ref[...], preferred_element_type=F32)
        y_ref[...] = y
        yn, _ = _rms_stats(y)
        o_ref[...] = h_ref[...] + yn * g_ref[...]

    row = pl.BlockSpec((tm, d), lambda i: (i, 0))
    return pl.pallas_call(
        body, name=name, grid=(t // tm,),
        in_specs=[pl.BlockSpec((tm, k), lambda i: (i, 0)), _resident(w.shape), _full(g.shape), row],
        out_specs=[row, row],
        out_shape=[jax.ShapeDtypeStruct((t, d), F32)] * 2,
        compiler_params=_params(1))(a, w, g, hres)


def _ffn_in(h, g, w4, name):
    t, d = h.shape
    n = w4.shape[2]
    tm = _tile(t, ROW_TILE)
    nj = D_FF // n

    def body(h_ref, g_ref, wg_ref, wu_ref, fg_ref, fu_ref, act_ref, hn_ref):
        @pl.when(pl.program_id(1) == 0)
        def _():
            xn, _ = _rms_stats(h_ref[...])
            hn_ref[...] = (xn * g_ref[...]).astype(BF16)

        fg = jnp.dot(hn_ref[...], wg_ref[...], preferred_element_type=F32)
        fu = jnp.dot(hn_ref[...], wu_ref[...], preferred_element_type=F32)
        fg_ref[...] = fg.astype(BF16)
        fu_ref[...] = fu.astype(BF16)
        act_ref[...] = (fg * _sigmoid(fg) * fu).astype(BF16)

    col = pl.BlockSpec((tm, n), lambda i, j: (i, j))
    return pl.pallas_call(
        body, name=name, grid=(t // tm, nj),
        in_specs=[pl.BlockSpec((tm, d), lambda i, j: (i, 0)), pl.BlockSpec((1, d), lambda i, j: (0, 0)),
                  pl.BlockSpec((None, d, n), lambda i, j: (j, 0, 0)),
                  pl.BlockSpec((None, d, n), lambda i, j: (j + nj, 0, 0))],
        out_specs=[col, col, col, pl.BlockSpec((tm, d), lambda i, j: (i, 0))],
        out_shape=[jax.ShapeDtypeStruct((t, D_FF), BF16)] * 3 + [jax.ShapeDtypeStruct((t, d), BF16)],
        compiler_params=_params(2))(h, g, w4, w4)


def _ple_fwd(h, p, wg, wp, name):
    t, d = h.shape
    tm = _tile(t, ROW_TILE)

    def body(h_ref, p_ref, wg_ref, wp_ref, o_ref, q_ref, e_ref):
        hh = h_ref[...]
        q = _dot(hh, wg_ref[...])
        e = _dot(p_ref[...], wp_ref[...])
        q_ref[...] = q.astype(BF16)
        e_ref[...] = e.astype(BF16)
        o_ref[...] = hh + _sigmoid(q) * e

    row = pl.BlockSpec((tm, d), lambda i: (i, 0))
    return pl.pallas_call(
        body, name=name, grid=(t // tm,),
        in_specs=[row, pl.BlockSpec((tm, p.shape[1]), lambda i: (i, 0)), _resident(wg.shape), _resident(wp.shape)],
        out_specs=[row, row, row],
        out_shape=[jax.ShapeDtypeStruct((t, d), F32), jax.ShapeDtypeStruct((t, d), BF16),
                   jax.ShapeDtypeStruct((t, d), BF16)],
        compiler_params=_params(1))(h, p, wg, wp)


def _loss_head(y, target, name):
    t, d = y.shape
    tm = _tile(t, ROW_TILE)

    def body(y_ref, t_ref, dy_ref, l_ref):
        @pl.when(pl.program_id(0) == 0)
        def _():
            l_ref[...] = jnp.zeros_like(l_ref)

        err = y_ref[...] - t_ref[...]
        dy_ref[...] = err * (1.0 / d)
        l_ref[...] += jnp.sum(err * err, keepdims=True)[:, :1] * jnp.ones((1, 128), F32)

    row = pl.BlockSpec((tm, d), lambda i: (i, 0))
    return pl.pallas_call(
        body, name=name, grid=(t // tm,),
        in_specs=[row, row], out_specs=[row, _full((1, 128))],
        out_shape=[jax.ShapeDtypeStruct((t, d), F32), jax.ShapeDtypeStruct((1, 128), F32)],
        compiler_params=_params(1))(y, target)


def _ple_bwd(dh, q, e, wg, name):
    t, d = dh.shape
    tm = _tile(t, ROW_TILE)

    def body(dh_ref, q_ref, e_ref, wg_ref, dq_ref, de_ref, o_ref):
        dh_ = dh_ref[...]
        s = _sigmoid(q_ref[...].astype(F32))
        dq = (dh_ * e_ref[...].astype(F32) * s * (1.0 - s)).astype(BF16)
        dq_ref[...] = dq
        de_ref[...] = (dh_ * s).astype(BF16)
        o_ref[...] = dh_ + _dot_nt(dq, wg_ref[...])

    row = pl.BlockSpec((tm, d), lambda i: (i, 0))
    return pl.pallas_call(
        body, name=name, grid=(t // tm,),
        in_specs=[row, row, row, _resident(wg.shape)], out_specs=[row, row, row],
        out_shape=[jax.ShapeDtypeStruct((t, d), BF16), jax.ShapeDtypeStruct((t, d), BF16),
                   jax.ShapeDtypeStruct((t, d), F32)],
        compiler_params=_params(1))(dh, q, e, wg)


def _ffn_out_bwd(dh, f, g, fg, fu, w, name):
    t, d = dh.shape
    tm = _tile(t, ROW_TILE_HEAVY)

    def body(dh_ref, f_ref, g_ref, fg_ref, fu_ref, w_ref, df_ref, dff_ref, dg_ref):
        @pl.when(pl.program_id(0) == 0)
        def _():
            dg_ref[...] = jnp.zeros_like(dg_ref)

        fn, r = _rms_stats(f_ref[...])
        df, dgt = _rms_bwd(fn, r, g_ref[...], dh_ref[...])
        dg_ref[...] += _rowsum(dgt)
        df = df.astype(BF16)
        df_ref[...] = df
        dact = _dot_nt(df, w_ref[...])
        fg_ = fg_ref[...].astype(F32)
        s = _sigmoid(fg_)
        dff_ref[:, 0:D_FF] = (dact * fu_ref[...].astype(F32) * (s * (1.0 + fg_ * (1.0 - s)))).astype(BF16)
        dff_ref[:, D_FF:2 * D_FF] = (dact * (fg_ * s)).astype(BF16)

    row = pl.BlockSpec((tm, d), lambda i: (i, 0))
    wide = pl.BlockSpec((tm, D_FF), lambda i: (i, 0))
    return pl.pallas_call(
        body, name=name, grid=(t // tm,),
        in_specs=[row, row, _full(g.shape), wide, wide, _resident(w.shape)],
        out_specs=[row, pl.BlockSpec((tm, 2 * D_FF), lambda i: (i, 0)), _full((1, d))],
        out_shape=[jax.ShapeDtypeStruct((t, d), BF16), jax.ShapeDtypeStruct((t, 2 * D_FF), BF16),
                   jax.ShapeDtypeStruct((1, d), F32)],
        compiler_params=_params(1))(dh, f, g, fg, fu, w)


def _in_bwd(pieces, w4, unit, h, g, dres, name):
    t, d = h.shape
    tm = _tile(t, ROW_TILE)
    per_chunk = w4.shape[2] // unit
    offs = []
    total = 0
    for _, nu in pieces:
        offs.append(total)
        total += nu
    n_p = len(pieces)

    def body(*refs):
        p_refs = refs[:n_p]
        w_ref, h_ref, g_ref, r_ref, o_ref, dg_ref, acc_ref = refs[n_p:]
        u = pl.program_id(1)

        @pl.when(u == 0)
        def _():
            acc_ref[...] = jnp.zeros_like(acc_ref)

        @pl.when((pl.program_id(0) == 0) & (u == 0))
        def _():
            dg_ref[...] = jnp.zeros_like(dg_ref)

        for p_ref, off, (_, nu) in zip(p_refs, offs, pieces):
            @pl.when((u >= off) & (u < off + nu))
            def _(p_ref=p_ref):
                acc_ref[...] += _dot_nt(p_ref[...], w_ref[...])

        @pl.when(u == total - 1)
        def _():
            xn, r = _rms_stats(h_ref[...])
            dx, dgt = _rms_bwd(xn, r, g_ref[...], acc_ref[...])
            dg_ref[...] += _rowsum(dgt)
            o_ref[...] = r_ref[...] + dx

    def piece_spec(off, nu):
        return pl.BlockSpec((tm, unit), lambda i, u: (i, jnp.clip(u - off, 0, nu - 1)))

    row = pl.BlockSpec((tm, d), lambda i, u: (i, 0))
    return pl.pallas_call(
        body, name=name, grid=(t // tm, total),
        in_specs=[piece_spec(off, nu) for off, (_, nu) in zip(offs, pieces)]
        + [pl.BlockSpec((None, d, unit), lambda i, u: (u // per_chunk, 0, u % per_chunk)), row,
           pl.BlockSpec((1, d), lambda i, u: (0, 0)), row],
        out_specs=[row, pl.BlockSpec((1, d), lambda i, u: (0, 0))],
        out_shape=[jax.ShapeDtypeStruct((t, d), F32), jax.ShapeDtypeStruct((1, d), F32)],
        scratch_shapes=[pltpu.VMEM((tm, d), F32)],
        compiler_params=_params(2))(*[a for a, _ in pieces], w4, h, g, dres)


def _lane_block(n, cap):
    return max(b for b in range(128, min(n, cap) + 1, 128) if n % b == 0)


def _mm_tn(x, dy, name, bn=None):
    t, m = x.shape
    n = dy.shape[1]
    bm = _lane_block(m, 1408)
    bn = bn or _lane_block(n, 1408)
    tk = _tile(t, 1024)

    def body(x_ref, dy_ref, o_ref):
        @pl.when(pl.program_id(2) == 0)
        def _():
            o_ref[...] = jnp.zeros_like(o_ref)

        o_ref[...] += _dot_tn(x_ref[...], dy_ref[...])

    return pl.pallas_call(
        body, name=name, grid=(m // bm, n // bn, t // tk),
        in_specs=[pl.BlockSpec((tk, bm), lambda a, b, k: (k, a)), pl.BlockSpec((tk, bn), lambda a, b, k: (k, b))],
        out_specs=pl.BlockSpec((bm, bn), lambda a, b, k: (a, b)),
        out_shape=jax.ShapeDtypeStruct((m, n), F32),
        compiler_params=_params(3))(x, dy)


def _merge_bwd(dh, mo, g, proj, bra, brb, brc, w_out, wa, wb, wc, name):
    t, d = dh.shape
    tm = _tile(t, ROW_TILE_HEAVY)

    def body(dh_ref, mo_ref, g_ref, za_ref, zb_ref, zc_ref, ba_ref, bb_ref, bc_ref, wo_ref, wa_ref, wb_ref, wc_ref,
             dmo_ref, dba_ref, dbb_ref, dbc_ref, dz_ref, dsg_ref, dcs_ref, dps_ref, dg_ref):
        @pl.when(pl.program_id(0) == 0)
        def _():
            dg_ref[...] = jnp.zeros_like(dg_ref)

        mon, r = _rms_stats(mo_ref[...])
        dmo, dgt = _rms_bwd(mon, r, g_ref[...], dh_ref[...])
        dg_ref[...] += _rowsum(dgt)
        dmo = dmo.astype(BF16)
        dmo_ref[...] = dmo
        dmerged = _dot_nt(dmo, wo_ref[...])
        branches = ((za_ref, ba_ref, wa_ref, dba_ref, dsg_ref), (zb_ref, bb_ref, wb_ref, dbb_ref, dcs_ref),
                    (zc_ref, bc_ref, wc_ref, dbc_ref, dps_ref))
        for j, (z_ref, b_ref, w_ref, db_ref, dx_ref) in enumerate(branches):
            gate = _sigmoid(z_ref[...].astype(F32))
            dbr = (dmerged * gate).astype(BF16)
            db_ref[...] = dbr
            dz_ref[:, j * d:(j + 1) * d] = (dmerged * b_ref[...].astype(F32) * gate * (1.0 - gate)).astype(BF16)
            dx_ref[...] = _dot_nt(dbr, w_ref[...]).astype(BF16)

    row = pl.BlockSpec((tm, d), lambda i: (i, 0))
    wspec = _resident((d, d))
    bf = jax.ShapeDtypeStruct((t, d), BF16)
    return pl.pallas_call(
        body, name=name, grid=(t // tm,),
        in_specs=[row, row, _full(g.shape), pl.BlockSpec((tm, d), lambda i: (i, 5)),
                  pl.BlockSpec((tm, d), lambda i: (i, 6)), pl.BlockSpec((tm, d), lambda i: (i, 7)),
                  row, row, row, wspec, wspec, wspec, wspec],
        out_specs=[row, row, row, row, pl.BlockSpec((tm, 3 * d), lambda i: (i, 0)), row, row, row, _full((1, d))],
        out_shape=[bf, bf, bf, bf, jax.ShapeDtypeStruct((t, 3 * d), BF16), bf, bf, bf,
                   jax.ShapeDtypeStruct((1, d), F32)],
        compiler_params=_params(1))(dh, mo, g, proj, proj, proj, bra, brb, brc, w_out, wa, wb, wc)


def _sgu_bwd(proj, dsg, wm, bs3, gv, bv, name):
    t = proj.shape[0]
    d = D_MODEL
    tm = _tile(t, ROW_TILE_HEAVY)
    hd = d // SGU_HEADS

    def body(zu_ref, zv_ref, d_ref, wm_ref, bs_ref, gv_ref, bv_ref, dz_ref, dwm_ref, dbs_ref, dgv_ref, dbv_ref,
             dvn_ref):
        @pl.when(pl.program_id(0) == 0)
        def _():
            dwm_ref[...] = jnp.zeros_like(dwm_ref)
            dbs_ref[...] = jnp.zeros_like(dbs_ref)
            dgv_ref[...] = jnp.zeros_like(dgv_ref)
            dbv_ref[...] = jnp.zeros_like(dbv_ref)

        mask = _sgu_mask()
        for blk in range(tm // SGU_BLOCK):
            rows = pl.ds(blk * SGU_BLOCK, SGU_BLOCK)
            u, du_dz = _gelu_and_grad(zu_ref[rows, :].astype(F32))
            v0, dv_dz = _gelu_and_grad(zv_ref[rows, :].astype(F32))
            xhat, rstd = _ln_stats(v0)
            vn = (xhat * gv_ref[...] + bv_ref[...]).astype(BF16)
            dsg = d_ref[rows, :].astype(F32)
            dmix = (dsg * u).astype(BF16)
            for hh in range(SGU_HEADS):
                cols = slice(hh * hd, (hh + 1) * hd)
                wmh = jnp.where(mask, wm_ref[hh], 0.0).astype(BF16)
                vb = vn[:, cols]
                mixed = jnp.dot(wmh, vb, preferred_element_type=F32) + bs_ref[hh]
                dz_ref[rows, cols] = (dsg[:, cols] * mixed * du_dz[:, cols]).astype(BF16)
                dmh = dmix[:, cols]
                dwm_ref[hh] += jnp.where(mask, _dot_nt(dmh, vb), 0.0)
                dbs_ref[hh] += jnp.sum(dmh.astype(F32), axis=1, keepdims=True)
                dvn_ref[:, cols] = _dot_tn(wmh, dmh)
            dvn = dvn_ref[...]
            dgv_ref[...] += _rowsum(dvn * xhat)
            dbv_ref[...] += _rowsum(dvn)
            dz_ref[rows, d:2 * d] = (_ln_bwd(xhat, rstd, gv_ref[...], dvn) * dv_dz).astype(BF16)

    return pl.pallas_call(
        body, name=name, grid=(t // tm,),
        in_specs=[pl.BlockSpec((tm, d), lambda i: (i, 0)), pl.BlockSpec((tm, d), lambda i: (i, 1)),
                  pl.BlockSpec((tm, d), lambda i: (i, 0)), _full(wm.shape), _full(bs3.shape), _full(gv.shape),
                  _full(bv.shape)],
        out_specs=[pl.BlockSpec((tm, 2 * d), lambda i: (i, 0)), _full(wm.shape), _full(bs3.shape), _full((1, d)),
                   _full((1, d))],
        out_shape=[jax.ShapeDtypeStruct((t, 2 * d), BF16), jax.ShapeDtypeStruct(wm.shape, F32),
                   jax.ShapeDtypeStruct(bs3.shape, F32), jax.ShapeDtypeStruct((1, d), F32),
                   jax.ShapeDtypeStruct((1, d), F32)],
        scratch_shapes=[pltpu.VMEM((SGU_BLOCK, d), F32)],
        compiler_params=_params(1))(proj, proj, dsg, wm, bs3, gv, bv)


def _conv_bwd_norm(proj, dcs, wdw, bdw, gln, bln, name):
    t = proj.shape[0]
    d = D_MODEL
    tm = _tile(t, ROW_TILE)
    main, halo = _conv_specs(t, tm, d)
    n_win = CONV_ROWS + CONV_HALO

    def body(a_ref, g_ref, ah_ref, gh_ref, dcs_ref, w_ref, b_ref, gl_ref, bl_ref,
             dcv_ref, dw_ref, db_ref, dgl_ref, dbl_ref, scr_ref, cv_ref, dwacc_ref):
        @pl.when(pl.program_id(0) == 0)
        def _():
            dwacc_ref[...] = jnp.zeros_like(dwacc_ref)
            db_ref[...] = jnp.zeros_like(db_ref)
            dgl_ref[...] = jnp.zeros_like(dgl_ref)
            dbl_ref[...] = jnp.zeros_like(dbl_ref)

        _fill_glu_history(scr_ref, a_ref, g_ref, ah_ref, gh_ref, tm)
        _conv_into(scr_ref, cv_ref, w_ref, tm, _CONV_BASE, False)
        xhat, rstd = _ln_stats(cv_ref[...] + b_ref[...])
        cn = xhat * gl_ref[...] + bl_ref[...]
        s = _sigmoid(cn)
        dcn = dcs_ref[...].astype(F32) * (s * (1.0 + cn * (1.0 - s)))
        dgl_ref[...] += _rowsum(dcn * xhat)
        dbl_ref[...] += _rowsum(dcn)
        dcv = _ln_bwd(xhat, rstd, gl_ref[...], dcn)
        db_ref[...] += _rowsum(dcv)
        dcv_ref[...] = dcv

        def chunk(ci, carry):
            r0 = pl.multiple_of(ci * CONV_ROWS, CONV_ROWS)
            for c0 in range(0, d, CONV_LANES):
                lanes = pl.ds(c0, CONV_LANES)
                win = scr_ref[pl.ds(r0, n_win), lanes]
                dchunk = dcv_ref[pl.ds(r0, CONV_ROWS), lanes]
                for r in range(8):
                    rolled = win if r == 0 else pltpu.roll(win, n_win - r, 0)
                    for q in range(n_win // 8):
                        k = 8 * q + r - _CONV_BASE
                        if 0 <= k < CONV_WIDTH and 8 * q + CONV_ROWS <= n_win:
                            prod = dchunk * rolled[8 * q:8 * q + CONV_ROWS]
                            part = prod[0:8]
                            for s8 in range(8, CONV_ROWS, 8):
                                part = part + prod[s8:s8 + 8]
                            dwacc_ref[pl.ds(8 * k, 8), lanes] += part
            return carry

        lax.fori_loop(0, tm // CONV_ROWS, chunk, 0)

        @pl.when(pl.program_id(0) == pl.num_programs(0) - 1)
        def _():
            dw_ref[...] = jnp.sum(dwacc_ref[...].reshape(CONV_HALO, 8, d), axis=1)

    row = pl.BlockSpec((tm, d), lambda i: (i, 0))
    vec = _full((1, d))
    return pl.pallas_call(
        body, name=name, grid=(t // tm,),
        in_specs=main + halo + [row, _full(wdw.shape), vec, vec, vec],
        out_specs=[row, _full((CONV_HALO, d)), vec, vec, vec],
        out_shape=[jax.ShapeDtypeStruct((t, d), F32), jax.ShapeDtypeStruct((CONV_HALO, d), F32)]
        + [jax.ShapeDtypeStruct((1, d), F32)] * 3,
        scratch_shapes=[pltpu.VMEM((tm + CONV_HALO, d), F32), pltpu.VMEM((tm, d), F32),
                        pltpu.VMEM((8 * CONV_HALO, d), F32)],
        compiler_params=_params(1))(proj, proj, proj, proj, dcs, wdw, bdw, gln, bln)


def _conv_bwd_taps(proj, dcv, wdw, name):
    t = proj.shape[0]
    d = D_MODEL
    tm = _tile(t, ROW_TILE)
    hb = tm // CONV_HALO
    last_halo = t // CONV_HALO - 1

    def body(a_ref, g_ref, dcv_ref, dnext_ref, w_ref, dz_ref, scr_ref, dh_ref):
        scr_ref[0:tm, :] = dcv_ref[...]
        is_last = pl.program_id(0) == pl.num_programs(0) - 1
        scr_ref[tm:tm + CONV_HALO, :] = jnp.where(is_last, 0.0, dnext_ref[...])
        _conv_into(scr_ref, dh_ref, w_ref, tm, 0, True)
        dglu = dh_ref[...]
        a = a_ref[...].astype(F32)
        s = _sigmoid(g_ref[...].astype(F32))
        dz_ref[:, 0:d] = (dglu * s).astype(BF16)
        dz_ref[:, d:2 * d] = (dglu * a * s * (1.0 - s)).astype(BF16)

    return pl.pallas_call(
        body, name=name, grid=(t // tm,),
        in_specs=[pl.BlockSpec((tm, d), lambda i: (i, 2)), pl.BlockSpec((tm, d), lambda i: (i, 3)),
                  pl.BlockSpec((tm, d), lambda i: (i, 0)),
                  pl.BlockSpec((CONV_HALO, d), lambda i: (jnp.minimum((i + 1) * hb, last_halo), 0)),
                  _full(wdw.shape)],
        out_specs=pl.BlockSpec((tm, 2 * d), lambda i: (i, 0)),
        out_shape=jax.ShapeDtypeStruct((t, 2 * d), BF16),
        scratch_shapes=[pltpu.VMEM((tm + CONV_HALO, d), F32), pltpu.VMEM((tm, d), F32)],
        compiler_params=_params(1))(proj, proj, dcv, dcv, wdw)


def _pool_bwd(proj, dps, wpool, spool, name):
    t = proj.shape[0]
    d = D_MODEL
    tm = _tile(t, ROW_TILE)
    hb = tm // POOL_HALO
    last_halo = t // POOL_HALO - 1
    ext = tm + POOL_HALO

    def body(z_ref, zh_ref, d_ref, dnext_ref, w_ref, s_ref, dz_ref, dw_ref, ds_ref, scr_ref, dext_ref, dq_ref):
        @pl.when(pl.program_id(0) == 0)
        def _():
            dw_ref[...] = jnp.zeros_like(dw_ref)
            ds_ref[...] = jnp.zeros_like(ds_ref)

        _pool_fill(scr_ref, z_ref, zh_ref, tm)
        t0 = pl.program_id(0) * tm
        is_last = pl.program_id(0) == pl.num_programs(0) - 1
        dext_ref[0:tm, :] = d_ref[...].astype(F32)
        dext_ref[tm:ext, :] = jnp.where(is_last, 0.0, dnext_ref[...].astype(F32))
        for gi, w in enumerate(POOL_WINDOWS):
            cols = slice(gi * POOL_GROUP, (gi + 1) * POOL_GROUP)
            dps_ext = dext_ref[:, cols]
            dpm_ext = (dps_ext * s_ref[:, cols]).astype(BF16)
            dpooled_ext = _dot_nt(dpm_ext, w_ref[gi])
            dq_ref[...] = dpooled_ext / _pool_count(t0, ext, w)
            acc = dq_ref[pl.ds(0, tm), :]
            for k in range(1, w):
                acc = acc + dq_ref[pl.ds(k, tm), :]
            dz_ref[:, cols] = (acc - dpooled_ext[0:tm]).astype(BF16)
            pooled = _pooled_group(scr_ref, gi, w, tm, t0).astype(BF16)
            pm = jnp.dot(pooled, w_ref[gi], preferred_element_type=F32)
            ds_ref[:, cols] += _rowsum(dps_ext[0:tm] * pm)
            dw_ref[gi] += _dot_tn(pooled, dpm_ext[0:tm])

    return pl.pallas_call(
        body, name=name, grid=(t // tm,),
        in_specs=_pool_specs(tm, d) + [pl.BlockSpec((tm, d), lambda i: (i, 0)),
                                       pl.BlockSpec((POOL_HALO, d), lambda i: (jnp.minimum((i + 1) * hb, last_halo), 0)),
                                       _full(wpool.shape), _full(spool.shape)],
        out_specs=[pl.BlockSpec((tm, d), lambda i: (i, 0)), _full(wpool.shape), _full((1, d))],
        out_shape=[jax.ShapeDtypeStruct((t, d), BF16), jax.ShapeDtypeStruct(wpool.shape, F32),
                   jax.ShapeDtypeStruct((1, d), F32)],
        scratch_shapes=[pltpu.VMEM((tm + POOL_HALO, d), F32), pltpu.VMEM((ext, d), F32),
                        pltpu.VMEM((ext, POOL_GROUP), F32)],
        compiler_params=_params(1))(proj, proj, dps, dps, wpool, spool)


ANY = pl.BlockSpec(memory_space=pl.ANY)


def _mesh_pos():
    x, y, c = lax.axis_index("x"), lax.axis_index("y"), lax.axis_index("c")
    chips = [(1 - x, y), (x, 1 - y), (1 - x, 1 - y)]
    return x, y, c, chips


def _chip_of(xy):
    return 2 * xy[0] + xy[1]


def _half_view(a):
    return a.reshape(a.shape[:-2] + (2, a.shape[-2] // 2, a.shape[-1]))


def _all_gather(bufs, name):
    n = len(bufs)

    def body(*refs):
        buf = refs[n:2 * n]
        send_sem, recv_sem = refs[2 * n:]
        x, y, c, chips = _mesh_pos()
        me = 2 * x + y
        sibling = (x, y, 1 - c)

        def slab(g, chip, half):
            return buf[g].at[chip, :, half]

        def ici(g, j):
            return pltpu.make_async_remote_copy(
                src_ref=slab(g, me, c), dst_ref=slab(g, me, c),
                send_sem=send_sem.at[6 * g + j], recv_sem=recv_sem.at[6 * g + j],
                device_id=(*chips[j], c), device_id_type=MESH)

        def d2d(g, j, half):
            return pltpu.make_async_remote_copy(
                src_ref=slab(g, _chip_of(chips[j]), half), dst_ref=slab(g, _chip_of(chips[j]), half),
                send_sem=send_sem.at[6 * g + 3 + j], recv_sem=recv_sem.at[6 * g + 3 + j],
                device_id=sibling, device_id_type=MESH)

        sends = []
        for g in range(n):
            for j in range(3):
                cp = ici(g, j)
                cp.start()
                sends.append(cp)
        for j in range(3):
            for g in range(n):
                pltpu.make_async_remote_copy(
                    src_ref=slab(g, _chip_of(chips[j]), c), dst_ref=slab(g, _chip_of(chips[j]), c),
                    send_sem=send_sem.at[6 * g + j], recv_sem=recv_sem.at[6 * g + j],
                    device_id=(*chips[j], c), device_id_type=MESH).wait_recv()
                cp = d2d(g, j, c)
                cp.start()
                sends.append(cp)
        for j in range(3):
            for g in range(n):
                d2d(g, j, 1 - c).wait_recv()
        for cp in sends:
            cp.wait_send()

    return pl.pallas_call(
        body, name=name, in_specs=[ANY] * n, out_specs=[ANY] * n,
        out_shape=[jax.ShapeDtypeStruct(b.shape, b.dtype) for b in bufs],
        input_output_aliases={g: g for g in range(n)},
        scratch_shapes=[pltpu.SemaphoreType.DMA((6 * n,)), pltpu.SemaphoreType.DMA((6 * n,))],
        compiler_params=pltpu.CompilerParams(has_side_effects=True))(*bufs)


def _pair_exchange(grads, name):
    n = len(grads)

    def body(*refs):
        src, dst = refs[:n], refs[n:2 * n]
        send_sem, recv_sem = refs[2 * n:]
        x, y, c, _ = _mesh_pos()
        copies = []
        for g in range(n):
            cp = pltpu.make_async_remote_copy(
                src_ref=src[g].at[:, :, 1 - c], dst_ref=dst[g], send_sem=send_sem.at[g],
                recv_sem=recv_sem.at[g], device_id=(x, y, 1 - c), device_id_type=MESH)
            cp.start()
            copies.append(cp)
        for cp in copies:
            cp.wait()

    return pl.pallas_call(
        body, name=name, in_specs=[ANY] * n, out_specs=[ANY] * n,
        out_shape=[jax.ShapeDtypeStruct(g.shape[:2] + g.shape[3:], g.dtype) for g in grads],
        scratch_shapes=[pltpu.SemaphoreType.DMA((n,)), pltpu.SemaphoreType.DMA((n,))],
        compiler_params=pltpu.CompilerParams(has_side_effects=True))(*grads)


def _chip_exchange(parts, name):
    n = len(parts)

    def body(*refs):
        src, dst = refs[:n], refs[n:2 * n]
        send_sem, recv_sem = refs[2 * n:]
        x, y, c, chips = _mesh_pos()
        me = 2 * x + y
        sends = []
        for g in range(n):
            for j in range(3):
                cp = pltpu.make_async_remote_copy(
                    src_ref=src[g].at[_chip_of(chips[j])], dst_ref=dst[g].at[me], send_sem=send_sem.at[3 * g + j],
                    recv_sem=recv_sem.at[3 * g + j], device_id=(*chips[j], c), device_id_type=MESH)
                cp.start()
                sends.append(cp)
        for g in range(n):
            for j in range(3):
                kj = _chip_of(chips[j])
                pltpu.make_async_remote_copy(
                    src_ref=src[g].at[kj], dst_ref=dst[g].at[kj], send_sem=send_sem.at[3 * g + j],
                    recv_sem=recv_sem.at[3 * g + j], device_id=(*chips[j], c), device_id_type=MESH).wait_recv()
        for cp in sends:
            cp.wait_send()

    return pl.pallas_call(
        body, name=name, in_specs=[ANY] * n, out_specs=[ANY] * n,
        out_shape=[jax.ShapeDtypeStruct(p.shape, p.dtype) for p in parts],
        scratch_shapes=[pltpu.SemaphoreType.DMA((3 * n,)), pltpu.SemaphoreType.DMA((3 * n,))],
        compiler_params=pltpu.CompilerParams(has_side_effects=True))(*parts)


def _pair_share(bufs, name):
    n = len(bufs)

    def body(*refs):
        buf = refs[n:2 * n]
        send_sem, recv_sem = refs[2 * n:]
        x, y, c, _ = _mesh_pos()
        copies = []
        for g in range(n):
            cp = pltpu.make_async_remote_copy(
                src_ref=buf[g].at[:, :, c], dst_ref=buf[g].at[:, :, c], send_sem=send_sem.at[g],
                recv_sem=recv_sem.at[g], device_id=(x, y, 1 - c), device_id_type=MESH)
            cp.start()
            copies.append(cp)
        for g, cp in enumerate(copies):
            cp.wait_send()
            pltpu.make_async_remote_copy(
                src_ref=buf[g].at[:, :, 1 - c], dst_ref=buf[g].at[:, :, 1 - c], send_sem=send_sem.at[g],
                recv_sem=recv_sem.at[g], device_id=(x, y, 1 - c), device_id_type=MESH).wait_recv()

    return pl.pallas_call(
        body, name=name, in_specs=[ANY] * n, out_specs=[ANY] * n,
        out_shape=[jax.ShapeDtypeStruct(b.shape, b.dtype) for b in bufs],
        input_output_aliases={g: g for g in range(n)},
        scratch_shapes=[pltpu.SemaphoreType.DMA((n,)), pltpu.SemaphoreType.DMA((n,))],
        compiler_params=pltpu.CompilerParams(has_side_effects=True))(*bufs)


def _all_reduce_small(vec, name):
    r = vec.shape[0]

    def body(v_ref, o_ref, gath_ref, send_sem, recv_sem):
        x, y, c, _ = _mesh_pos()
        me = 4 * x + 2 * y + c
        gath_ref[me] = v_ref[...]
        copies = []
        for k in range(1, 8):
            peer = (x ^ (k >> 2), y ^ ((k >> 1) & 1), c ^ (k & 1))
            cp = pltpu.make_async_remote_copy(
                src_ref=v_ref, dst_ref=gath_ref.at[me], send_sem=send_sem.at[k - 1], recv_sem=recv_sem.at[k - 1],
                device_id=peer, device_id_type=MESH)
            cp.start()
            copies.append(cp)
        for k in range(1, 8):
            src_id = me ^ k
            pltpu.make_async_remote_copy(
                src_ref=v_ref, dst_ref=gath_ref.at[src_id], send_sem=send_sem.at[k - 1], recv_sem=recv_sem.at[k - 1],
                device_id=(x, y, c), device_id_type=MESH).wait_recv()
        for cp in copies:
            cp.wait_send()
        acc = gath_ref[0]
        for k in range(1, 8):
            acc = acc + gath_ref[k]
        o_ref[...] = acc

    return pl.pallas_call(
        body, name=name,
        in_specs=[pl.BlockSpec(memory_space=pltpu.VMEM)], out_specs=pl.BlockSpec(memory_space=pltpu.VMEM),
        out_shape=jax.ShapeDtypeStruct(vec.shape, F32),
        scratch_shapes=[pltpu.VMEM((8, r, 128), F32), pltpu.SemaphoreType.DMA((7,)), pltpu.SemaphoreType.DMA((7,))],
        compiler_params=pltpu.CompilerParams(has_side_effects=True, vmem_limit_bytes=VMEM_LIMIT))(vec)


def _row_block(rows, cols, mult=16):
    best = None
    for cand in range(mult, rows + 1, mult):
        if rows % cand == 0 and cand * cols * 4 <= EW_BLOCK_BYTES:
            best = cand
    return best or rows


POS_ME, POS_CORE = 0, 4


def _place(arrs, li, pos, dtype, name):
    s = len(arrs)
    _, rows, cols = arrs[0].shape
    rh = rows // 2
    tr = _row_block(rh, cols)
    nb = rh // tr

    def body(pos_ref, *refs):
        o_ref = refs[s]
        for j in range(s):
            @pl.when(pl.program_id(0) == j)
            def _(j=j):
                o_ref[...] = refs[j][...].astype(dtype)

    def in_spec(j):
        return pl.BlockSpec((None, tr, cols), lambda b, hf, i, pos_ref: (li, jnp.where(b == j, hf * nb + i, 0), 0))

    return pl.pallas_call(
        body, name=name,
        grid_spec=pltpu.PrefetchScalarGridSpec(
            num_scalar_prefetch=1, grid=(s, 2, nb), in_specs=[in_spec(j) for j in range(s)],
            out_specs=pl.BlockSpec((None, None, None, tr, cols),
                                   lambda b, hf, i, pos_ref: (pos_ref[POS_ME], b, hf, i, 0))),
        out_shape=jax.ShapeDtypeStruct((N_CHIPS, s, 2, rh, cols), dtype),
        compiler_params=_params(3))(pos, *arrs)


def _pair_sum(grad, recv, pos, out_dtype, name):
    _, s, rh, cols = recv.shape
    tr = _row_block(rh, cols)

    def body(pos_ref, g_ref, r_ref, o_ref):
        o_ref[...] = (g_ref[...] + r_ref[...]).astype(out_dtype)

    blk = (None, None, tr, cols)
    return pl.pallas_call(
        body, name=name,
        grid_spec=pltpu.PrefetchScalarGridSpec(
            num_scalar_prefetch=1, grid=(N_CHIPS, s, rh // tr),
            in_specs=[pl.BlockSpec((None, None, None, tr, cols),
                                   lambda a, b, i, pos_ref: (a, b, pos_ref[POS_CORE], i, 0)),
                      pl.BlockSpec(blk, lambda a, b, i, pos_ref: (a, b, i, 0))],
            out_specs=pl.BlockSpec(blk, lambda a, b, i, pos_ref: (a, b, i, 0))),
        out_shape=jax.ShapeDtypeStruct(recv.shape, out_dtype),
        compiler_params=_params(3))(pos, grad, recv)


def _chip_sum(part, landed, gbuf, li, n_layers, pos, name):
    _, s, rh, cols = part.shape
    tr = _row_block(rh, cols)

    def body(pos_ref, p_ref, a_ref, b_ref, c_ref, *rest):
        o_ref = rest[-1]
        o_ref[...] = ((p_ref[...].astype(F32) + a_ref[...].astype(F32)) + b_ref[...].astype(F32)) \
            + c_ref[...].astype(F32)

    def slot(k):
        return pl.BlockSpec((None, None, tr, cols), lambda b, i, pos_ref: (pos_ref[k], b, i, 0))

    in_specs = [slot(0), slot(1), slot(2), slot(3)]
    operands = [pos, part, landed, landed, landed]
    aliases = {}
    if gbuf is not None:
        in_specs.append(ANY)
        operands.append(gbuf)
        aliases = {len(operands) - 1: 0}
    return pl.pallas_call(
        body, name=name,
        grid_spec=pltpu.PrefetchScalarGridSpec(
            num_scalar_prefetch=1, grid=(s, rh // tr), in_specs=in_specs,
            out_specs=pl.BlockSpec((None, None, None, tr, cols),
                                   lambda b, i, pos_ref: (li, b, pos_ref[POS_CORE], i, 0))),
        out_shape=jax.ShapeDtypeStruct((n_layers, s, 2, rh, cols), F32),
        input_output_aliases=aliases,
        compiler_params=_params(2))(*operands)


def _adamw_math(w, g, m, v):
    m = ADAM_B1 * m + (1.0 - ADAM_B1) * g
    v = ADAM_B2 * v + (1.0 - ADAM_B2) * (g * g)
    m_hat = m / (1.0 - ADAM_B1 ** ADAM_STEP)
    v_hat = v / (1.0 - ADAM_B2 ** ADAM_STEP)
    delta = -ADAM_LR * (m_hat / (jnp.sqrt(v_hat) + ADAM_EPS) + ADAM_WD * w)
    return delta, m, v


def _adamw(w, g, slot, m, v, name):
    l, rows, cols = w.shape
    tr = _row_block(rows, cols, 8)

    def body(w_ref, g_ref, m_ref, v_ref, go_ref, d_ref, mo_ref, vo_ref):
        g_ = g_ref[...]
        delta, m_, v_ = _adamw_math(w_ref[...], g_, m_ref[...], v_ref[...])
        go_ref[...] = g_
        d_ref[...] = delta
        mo_ref[...] = m_
        vo_ref[...] = v_

    blk = pl.BlockSpec((None, tr, cols), lambda a, i: (a, i, 0))
    gblk = pl.BlockSpec((None, None, tr, cols), lambda a, i: (a, slot, i, 0))
    return pl.pallas_call(
        body, name=name, grid=(l, rows // tr), in_specs=[blk, gblk, blk, blk], out_specs=[blk] * 4,
        out_shape=[jax.ShapeDtypeStruct(w.shape, F32)] * 4,
        compiler_params=_params(2))(w, g, m, v)


SQ = ("w_sgu_out", "w_conv_out", "w_pool_out", "w_out", "w_ple_gate")
SMALL = ("g_mix_pre", "w_sgu_s", "b_sgu_s", "g_sgu_v", "b_sgu_v", "b_dw", "g_conv_ln", "b_conv_ln", "s_pool",
         "g_mix_post", "g_ffn_pre", "g_ffn_post")


def _layer_weights(gath, small, li):
    w = {}
    gath = {k: v.reshape(v.shape[:2] + (-1, v.shape[-1])) for k, v in gath.items()}
    w["w_in"] = gath["in"].reshape(N_CHIPS, D_MODEL, -1)
    for slot, nm in enumerate(SQ):
        w[nm] = gath["sq"][:, slot].reshape(D_MODEL, D_MODEL)
    w["w_ffn_in"] = gath["ffn_in"].reshape(N_CHIPS, D_MODEL, -1)
    w["w_ffn_out"] = gath["ffn_out"].reshape(D_FF, D_MODEL)
    mix = gath["mix"]
    w["w_ple"] = mix[:, 0].transpose(1, 0, 2).reshape(256, D_MODEL)
    w["w_pool"] = mix[:, 1].reshape(N_CHIPS, 4, 64, 256).transpose(1, 0, 2, 3).reshape(4, 256, 256)
    w["w_dw"] = gath["dw"].reshape(N_CHIPS, CONV_HALO, -1).transpose(1, 0, 2).reshape(CONV_HALO, D_MODEL)
    for nm in SMALL:
        w[nm] = small[nm][li]
    return w


def _vec(a):
    return a.reshape(1, -1)


def _layer_fwd(h, p, w, li):
    s = {}
    tag = "_l%d" % li
    s["h0"] = h
    proj, hn = _norm_mm(h, _vec(w["g_mix_pre"]), w["w_in"], "mix_in" + tag)
    s["proj"], s["hn"] = proj, hn
    bs3 = w["b_sgu_s"].reshape(SGU_HEADS, SGU_BLOCK, 1)
    s["sg"] = _sgu_fwd(proj, w["w_sgu_s"], bs3, _vec(w["g_sgu_v"]), _vec(w["b_sgu_v"]), "sgu_fwd" + tag)
    s["cs"] = _conv_fwd(proj, w["w_dw"], _vec(w["b_dw"]), _vec(w["g_conv_ln"]), _vec(w["b_conv_ln"]),
                        "conv_fwd" + tag)
    s["ps"] = _pool_fwd(proj, w["w_pool"], _vec(w["s_pool"]), "pool_fwd" + tag)
    s["bra"], s["brb"], s["brc"], s["merged"] = _merge_fwd(
        proj, s["sg"], s["cs"], s["ps"], w["w_sgu_out"], w["w_conv_out"], w["w_pool_out"], "merge_fwd" + tag)
    s["mo"], h1 = _mm_norm_res(s["merged"], w["w_out"], _vec(w["g_mix_post"]), h, "mix_out" + tag)
    s["h1"] = h1
    s["fg"], s["fu"], s["act"], s["hn2"] = _ffn_in(h1, _vec(w["g_ffn_pre"]), w["w_ffn_in"], "ffn_in" + tag)
    s["f"], h2 = _mm_norm_res(s["act"], w["w_ffn_out"], _vec(w["g_ffn_post"]), h1, "ffn_out" + tag)
    s["h2"] = h2
    h3, s["q"], s["e"] = _ple_fwd(h2, p, w["w_ple_gate"], w["w_ple"], "ple_fwd" + tag)
    return h3, s


def _layer_bwd(dh3, p, w, s, li):
    tag = "_l%d" % li
    d = D_MODEL
    gs = {}
    dq, de, dh2 = _ple_bwd(dh3, s["q"], s["e"], w["w_ple_gate"], "ple_bwd" + tag)
    dw_ple = _mm_tn(p, de, "dw_ple" + tag)
    dw_ple_gate = _mm_tn(s["h2"], dq, "dw_ple_gate" + tag)
    df, dff, gs["g_ffn_post"] = _ffn_out_bwd(dh2, s["f"], _vec(w["g_ffn_post"]), s["fg"], s["fu"], w["w_ffn_out"],
                                             "ffn_out_bwd" + tag)
    dw_ffn_out = _mm_tn(s["act"], df, "dw_ffn_out" + tag)
    n_ff = w["w_ffn_in"].shape[2]
    dh1, gs["g_ffn_pre"] = _in_bwd([(dff, 2 * D_FF // n_ff)], w["w_ffn_in"], n_ff, s["h1"], _vec(w["g_ffn_pre"]),
                                   dh2, "ffn_in_bwd" + tag)
    dw_ffn_in = _mm_tn(s["hn2"], dff, "dw_ffn_in" + tag, bn=n_ff)
    (dmo, dbra, dbrb, dbrc, dzg, dsg, dcs, dps, gs["g_mix_post"]) = _merge_bwd(
        dh1, s["mo"], _vec(w["g_mix_post"]), s["proj"], s["bra"], s["brb"], s["brc"], w["w_out"], w["w_sgu_out"],
        w["w_conv_out"], w["w_pool_out"], "merge_bwd" + tag)
    dw_out = _mm_tn(s["merged"], dmo, "dw_out" + tag)
    dw_sgu_out = _mm_tn(s["sg"], dbra, "dw_sgu_out" + tag)
    dw_conv_out = _mm_tn(s["cs"], dbrb, "dw_conv_out" + tag)
    dw_pool_out = _mm_tn(s["ps"], dbrc, "dw_pool_out" + tag)
    bs3 = w["b_sgu_s"].reshape(SGU_HEADS, SGU_BLOCK, 1)
    dz_sgu, gs["w_sgu_s"], dbs3, gs["g_sgu_v"], gs["b_sgu_v"] = _sgu_bwd(
        s["proj"], dsg, w["w_sgu_s"], bs3, _vec(w["g_sgu_v"]), _vec(w["b_sgu_v"]), "sgu_bwd" + tag)
    gs["b_sgu_s"] = dbs3
    dcv, dwdw, gs["b_dw"], gs["g_conv_ln"], gs["b_conv_ln"] = _conv_bwd_norm(
        s["proj"], dcs, w["w_dw"], _vec(w["b_dw"]), _vec(w["g_conv_ln"]), _vec(w["b_conv_ln"]), "conv_bwd_norm" + tag)
    dz_conv = _conv_bwd_taps(s["proj"], dcv, w["w_dw"], "conv_bwd_taps" + tag)
    dz_pool, dwpool, gs["s_pool"] = _pool_bwd(s["proj"], dps, w["w_pool"], _vec(w["s_pool"]), "pool_bwd" + tag)
    pieces = [(dz_sgu, 2), (dz_conv, 2), (dz_pool, 1), (dzg, 3)]
    dh0, gs["g_mix_pre"] = _in_bwd(pieces, w["w_in"], d, s["h0"], _vec(w["g_mix_pre"]), dh1, "mix_in_bwd" + tag)
    dw_in = jnp.concatenate([_mm_tn(s["hn"], dz, "dw_in%d" % k + tag) for k, (dz, _) in enumerate(pieces)], axis=1)

    big = {}
    big["in"] = dw_in.reshape(d, N_CHIPS, 2 * d).transpose(1, 0, 2)[:, None]
    sq = {"w_sgu_out": dw_sgu_out, "w_conv_out": dw_conv_out, "w_pool_out": dw_pool_out, "w_out": dw_out,
          "w_ple_gate": dw_ple_gate}
    big["sq"] = jnp.stack([sq[nm].reshape(N_CHIPS, d // N_CHIPS, d) for nm in SQ], axis=1)
    big["ffn_in"] = dw_ffn_in.reshape(d, N_CHIPS, n_ff).transpose(1, 0, 2)[:, None]
    big["ffn_out"] = dw_ffn_out.reshape(N_CHIPS, 1, D_FF // N_CHIPS, d)
    gple = dw_ple.reshape(256, N_CHIPS, 256).transpose(1, 0, 2)
    gpool = dwpool.reshape(4, N_CHIPS, 64, 256).transpose(1, 0, 2, 3).reshape(N_CHIPS, 256, 256)
    big["mix"] = jnp.stack([gple, gpool], axis=1)
    big["dw"] = dwdw.reshape(CONV_HALO, N_CHIPS, 256).transpose(1, 0, 2)[:, None]
    return dh0, big, gs


GROUPS = ("in", "sq", "ffn_in", "ffn_out", "mix", "dw")
WIRE_DTYPE = {"in": BF16, "sq": BF16, "ffn_in": BF16, "ffn_out": BF16, "mix": BF16, "dw": F32}


def _group_members(wts):
    n_layers = wts["w_in"].shape[0]
    dw = wts["w_dw"].reshape(n_layers, CONV_WIDTH, -1)
    return {"in": [wts["w_in"]], "sq": [wts[nm] for nm in SQ], "ffn_in": [wts["w_ffn_in"]],
            "ffn_out": [wts["w_ffn_out"]],
            "mix": [wts["w_ple"], wts["w_pool"].reshape(n_layers, POOL_GROUP, POOL_GROUP)],
            "dw": [jnp.pad(dw, ((0, 0), (0, CONV_HALO - CONV_WIDTH), (0, 0)))]}


def _pack_small(tree):
    flat = jnp.concatenate([tree[nm].reshape(-1).astype(F32) for nm in SMALL])
    return flat.reshape(-1, 128)


def _unpack_small(packed, like):
    out, off = {}, 0
    flat = packed.reshape(-1)
    for nm in SMALL:
        n = like[nm].size
        out[nm] = flat[off:off + n].reshape(like[nm].shape)
        off += n
    return out


WEIGHTS = ("g_mix_pre", "w_in", "w_sgu_s", "b_sgu_s", "g_sgu_v", "b_sgu_v", "w_sgu_out", "w_dw", "b_dw", "g_conv_ln",
           "b_conv_ln", "w_conv_out", "w_pool", "s_pool", "w_pool_out", "w_out", "g_mix_post", "g_ffn_pre",
           "w_ffn_in", "w_ffn_out", "g_ffn_post", "w_ple", "w_ple_gate")


def kernel(x, p, g_mix_pre, w_in, w_sgu_s, b_sgu_s, g_sgu_v, b_sgu_v, w_sgu_out, w_dw, b_dw, g_conv_ln, b_conv_ln, w_conv_out, w_pool, s_pool, w_pool_out, w_out, g_mix_post, g_ffn_pre, w_ffn_in, w_ffn_out, g_ffn_post, w_ple, w_ple_gate, loss_target, m_g_mix_pre, m_w_in, m_w_sgu_s, m_b_sgu_s, m_g_sgu_v, m_b_sgu_v, m_w_sgu_out, m_w_dw, m_b_dw, m_g_conv_ln, m_b_conv_ln, m_w_conv_out, m_w_pool, m_s_pool, m_w_pool_out, m_w_out, m_g_mix_post, m_g_ffn_pre, m_w_ffn_in, m_w_ffn_out, m_g_ffn_post, m_w_ple, m_w_ple_gate, v_g_mix_pre, v_w_in, v_w_sgu_s, v_b_sgu_s, v_g_sgu_v, v_b_sgu_v, v_w_sgu_out, v_w_dw, v_b_dw, v_g_conv_ln, v_b_conv_ln, v_w_conv_out, v_w_pool, v_s_pool, v_w_pool_out, v_w_out, v_g_mix_post, v_g_ffn_pre, v_w_ffn_in, v_w_ffn_out, v_g_ffn_post, v_w_ple, v_w_ple_gate):
    args = dict(locals())
    wts = {nm: args[nm] for nm in WEIGHTS}
    mom = {nm: args["m_" + nm] for nm in WEIGHTS}
    var = {nm: args["v_" + nm] for nm in WEIGHTS}
    n_layers = w_in.shape[0]
    h = x.reshape(x.shape[1:])
    target = loss_target.reshape(loss_target.shape[1:])
    cx, cy, core = lax.axis_index("x"), lax.axis_index("y"), lax.axis_index("c")
    pos = jnp.stack([2 * cx + cy, 2 * (1 - cx) + cy, 2 * cx + (1 - cy), 2 * (1 - cx) + (1 - cy), core])
    pos = pos.astype(jnp.int32)

    members = _group_members(wts)
    gathered = []
    for li in range(n_layers):
        bufs = [_place(members[g], li, pos, WIRE_DTYPE[g], "place_%s_l%d" % (g, li)) for g in GROUPS]
        out = _all_gather(bufs, "all_gather_l%d" % li)
        gathered.append(dict(zip(GROUPS, out)))

    saved, layer_w = [], []
    for li in range(n_layers):
        w = _layer_weights(gathered[li], wts, li)
        layer_w.append(w)
        h, s = _layer_fwd(h, p[li, 0], w, li)
        saved.append(s)
    dh, sq_err = _loss_head(h, target, "loss_head")
    loss = lax.psum(sq_err[0, 0] * (0.5 / D_MODEL), ("x", "y", "c"))

    small_grads = [None] * n_layers
    reduced = [None] * len(GROUPS)
    for li in reversed(range(n_layers)):
        dh, big, small_grads[li] = _layer_bwd(dh, p[li, 0], layer_w[li], saved[li], li)
        tag = "_l%d" % li
        grads = [_half_view(big[g]) for g in GROUPS]
        recv = _pair_exchange(grads, "pair_exchange" + tag)
        parts = [_pair_sum(gr, rv, pos, WIRE_DTYPE[g], "pair_sum_%s" % g + tag)
                 for g, gr, rv in zip(GROUPS, grads, recv)]
        landed = _chip_exchange(parts, "chip_exchange" + tag)
        reduced = [_chip_sum(pt, ld, gb, li, n_layers, pos, "chip_sum_%s" % g + tag)
                   for g, pt, ld, gb in zip(GROUPS, parts, landed, reduced)]
    grad_x = dh[None]

    shared = _pair_share(reduced, "pair_share")
    red = {g: b.reshape(b.shape[:2] + (-1, b.shape[-1])) for g, b in zip(GROUPS, shared)}

    where = {"w_in": ("in", 0), "w_ffn_in": ("ffn_in", 0), "w_ffn_out": ("ffn_out", 0), "w_ple": ("mix", 0),
             "w_pool": ("mix", 1)}
    for slot, nm in enumerate(SQ):
        where[nm] = ("sq", slot)
    outs = {}
    for nm, (g, slot) in where.items():
        shape = wts[nm].shape
        to3 = lambda a: a.reshape((n_layers,) + red[g].shape[2:])
        res = _adamw(to3(wts[nm]), red[g], slot, to3(mom[nm]), to3(var[nm]), "adamw_" + nm)
        outs[nm] = [r.reshape(shape) for r in res]
    gdw = red["dw"][:, :, :CONV_WIDTH]
    to3 = lambda a: a.reshape(n_layers, CONV_WIDTH, -1)
    res = _adamw(to3(wts["w_dw"]), gdw, 0, to3(mom["w_dw"]), to3(var["w_dw"]), "adamw_w_dw")
    outs["w_dw"] = [r.reshape(wts["w_dw"].shape) for r in res]

    small_tree = {nm: jnp.stack([small_grads[li][nm].reshape(wts[nm].shape[1:]) for li in range(n_layers)], axis=0)
                  for nm in SMALL}
    gsmall = _all_reduce_small(_pack_small(small_tree), "all_reduce_small")
    pk = lambda tree: _pack_small({nm: tree[nm] for nm in SMALL})[None]
    res = _adamw(pk(wts), gsmall[None, None], 0, pk(mom), pk(var), "adamw_small")
    unpacked = [_unpack_small(r[0], wts) for r in res]
    for nm in SMALL:
        outs[nm] = [u[nm] for u in unpacked]

    result = [loss, grad_x]
    for k in range(4):
        result += [outs[nm][k] for nm in WEIGHTS]
    return tuple(result)
```

```python
import functools

import jax
import jax.numpy as jnp
from jax import lax
from jax.experimental import pallas as pl
from jax.experimental.pallas import tpu as pltpu

F32 = jnp.float32
BF16 = jnp.bfloat16
MESH = pl.DeviceIdType.MESH

EPS = 1e-6
D_MODEL = 1024
SGU_BLOCK = 128
SGU_HEADS = 8
CHUNK = 64
CONV_WIDTH = 31
CONV_HALO = 32
POOL_WINDOWS = (2, 4, 8, 16)
POOL_HALO = 16
POOL_GROUP = 256
D_FF = 2816
N_CHIPS = 4

ADAM_LR = 0.001
ADAM_B1 = 0.9
ADAM_B2 = 0.999
ADAM_EPS = 1e-08
ADAM_WD = 0.01
ADAM_STEP = 10

VMEM_LIMIT = 52 * 1024 * 1024
ROW_TILE = 512
ROW_TILE_HEAVY = 256
CONV_ROWS = 32
CONV_LANES = 256
EW_BLOCK_BYTES = 2 * 1024 * 1024


def _params(n_grid):
    return pltpu.CompilerParams(dimension_semantics=("arbitrary",) * n_grid, vmem_limit_bytes=VMEM_LIMIT)


def _dot(a, b):
    return jnp.dot(a.astype(BF16), b.astype(BF16), preferred_element_type=F32)


def _dot_nt(a, b):
    return lax.dot_general(a.astype(BF16), b.astype(BF16), (((1,), (1,)), ((), ())), preferred_element_type=F32)


def _dot_tn(a, b):
    return lax.dot_general(a.astype(BF16), b.astype(BF16), (((0,), (0,)), ((), ())), preferred_element_type=F32)


def _sigmoid(x):
    return 1.0 / (1.0 + jnp.exp(-x))


_GELU_C = 0.7978845608028654
_GELU_A = 0.044715


def _gelu(x):
    t = jnp.tanh(_GELU_C * (x + _GELU_A * x * x * x))
    return 0.5 * x * (1.0 + t)


def _gelu_and_grad(x):
    x2 = x * x
    t = jnp.tanh(_GELU_C * (x + _GELU_A * x2 * x))
    g = 0.5 * (1.0 + t) + 0.5 * x * (1.0 - t * t) * (_GELU_C * (1.0 + 3.0 * _GELU_A * x2))
    return 0.5 * x * (1.0 + t), g


def _rms_stats(x):
    r = lax.rsqrt(jnp.mean(x * x, axis=-1, keepdims=True) + EPS)
    return x * r, r


def _rms_bwd(xn, r, g, dy):
    gd = dy * g
    return r * (gd - xn * jnp.mean(gd * xn, axis=-1, keepdims=True)), dy * xn


def _ln_stats(x):
    mu = jnp.mean(x, axis=-1, keepdims=True)
    xc = x - mu
    rstd = lax.rsqrt(jnp.mean(xc * xc, axis=-1, keepdims=True) + EPS)
    return xc * rstd, rstd


def _ln_bwd(xhat, rstd, g, dy):
    dxh = dy * g
    return rstd * (dxh - jnp.mean(dxh, axis=-1, keepdims=True) - xhat * jnp.mean(dxh * xhat, axis=-1, keepdims=True))


def _rowsum(x):
    return jnp.sum(x, axis=0, keepdims=True)


def _tile(t, want):
    return min(t, want)


def _full(shape):
    n = len(shape)
    return pl.BlockSpec(shape, lambda *_: (0,) * n)


def _resident(shape):
    n = len(shape)
    return pl.BlockSpec(shape, lambda *_: (0,) * n, pipeline_mode=pl.Buffered(1))


class _Payload:
    def __init__(self, operands, out_shape, aliases, scratch, start, finish):
        self.operands, self.out_shape, self.aliases, self.scratch = list(operands), list(out_shape), aliases, scratch
        self.start, self.finish = start, finish
        self.results = None


def _pcall(body, *, name, grid, in_specs, out_specs, out_shape, operands, scratch_shapes=(), comm=None):
    single = not isinstance(out_shape, (list, tuple))
    out_specs = [out_specs] if single else list(out_specs)
    out_shape = [out_shape] if single else list(out_shape)
    if comm is None:
        res = pl.pallas_call(
            body, name=name, grid=grid, in_specs=list(in_specs), out_specs=out_specs, out_shape=out_shape,
            scratch_shapes=list(scratch_shapes), compiler_params=_params(len(grid)))(*operands)
        return res[0] if single else res
    n_in, n_out, n_scr = len(in_specs), len(out_shape), len(scratch_shapes)
    ci, co = len(comm.operands), len(comm.out_shape)

    def hosted(*refs):
        bounds = [0, n_in, n_in + ci, n_in + ci + n_out, n_in + ci + n_out + co, n_in + ci + n_out + co + n_scr]
        a, b, c_, d_, s_ = [refs[lo:hi] for lo, hi in zip(bounds[:-1], bounds[1:])]
        t_ = refs[bounds[-1]:]
        ids = [pl.program_id(q) for q in range(len(grid))]
        first = functools.reduce(jnp.logical_and, [i == 0 for i in ids])
        last = functools.reduce(jnp.logical_and, [i == pl.num_programs(q) - 1 for q, i in enumerate(ids)])

        @pl.when(first)
        def _():
            comm.start(b, d_, t_)

        body(*a, *c_, *s_)

        @pl.when(last)
        def _():
            comm.finish(b, d_, t_)

    res = pl.pallas_call(
        hosted, name=name, grid=grid, in_specs=list(in_specs) + [ANY] * ci, out_specs=out_specs + [ANY] * co,
        out_shape=out_shape + comm.out_shape, scratch_shapes=list(scratch_shapes) + list(comm.scratch),
        input_output_aliases={n_in + i: n_out + o for i, o in comm.aliases.items()},
        compiler_params=pltpu.CompilerParams(dimension_semantics=("arbitrary",) * len(grid),
                                             vmem_limit_bytes=VMEM_LIMIT, has_side_effects=True),
    )(*operands, *comm.operands)
    comm.results = list(res[n_out:])
    res = res[:n_out]
    return res[0] if single else res


def _run_payload(comm, name):
    ci, co = len(comm.operands), len(comm.out_shape)

    def body(*refs):
        b, d_, t_ = refs[:ci], refs[ci:ci + co], refs[ci + co:]
        comm.start(b, d_, t_)
        comm.finish(b, d_, t_)

    res = pl.pallas_call(
        body, name=name, in_specs=[ANY] * ci, out_specs=[ANY] * co, out_shape=comm.out_shape,
        input_output_aliases=dict(comm.aliases), scratch_shapes=list(comm.scratch),
        compiler_params=pltpu.CompilerParams(has_side_effects=True))(*comm.operands)
    comm.results = list(res)
    return comm.results


def _take(hosts, key):
    return hosts[key]() if hosts and key in hosts else None


def _norm_mm(h, g, w4, name, comm=None):
    t, d = h.shape
    n = w4.shape[2]
    tm = _tile(t, ROW_TILE)

    def body(h_ref, g_ref, w_ref, o_ref, hn_ref):
        @pl.when(pl.program_id(1) == 0)
        def _():
            xn, _ = _rms_stats(h_ref[...])
            hn_ref[...] = (xn * g_ref[...]).astype(BF16)

        o_ref[...] = jnp.dot(hn_ref[...], w_ref[...], preferred_element_type=F32).astype(BF16)

    return _pcall(
        body, name=name, grid=(t // tm, N_CHIPS),
        in_specs=[pl.BlockSpec((tm, d), lambda i, j: (i, 0)), pl.BlockSpec((1, d), lambda i, j: (0, 0)),
                  pl.BlockSpec((None, d, n), lambda i, j: (j, 0, 0))],
        out_specs=[pl.BlockSpec((tm, n), lambda i, j: (i, j)), pl.BlockSpec((tm, d), lambda i, j: (i, 0))],
        out_shape=[jax.ShapeDtypeStruct((t, N_CHIPS * n), BF16), jax.ShapeDtypeStruct((t, d), BF16)],
        operands=(h, g, w4), comm=comm)


def _sgu_mask():
    ii = lax.broadcasted_iota(jnp.int32, (SGU_BLOCK, SGU_BLOCK), 0) // CHUNK
    jj = lax.broadcasted_iota(jnp.int32, (SGU_BLOCK, SGU_BLOCK), 1) // CHUNK
    return jj <= ii


def _sgu_fwd(proj, wm, bs3, gv, bv, name):
    t = proj.shape[0]
    d = D_MODEL
    tm = _tile(t, ROW_TILE)
    hd = d // SGU_HEADS

    def body(zu_ref, zv_ref, wm_ref, bs_ref, gv_ref, bv_ref, o_ref):
        mask = _sgu_mask()
        for blk in range(tm // SGU_BLOCK):
            rows = pl.ds(blk * SGU_BLOCK, SGU_BLOCK)
            u = _gelu(zu_ref[rows, :].astype(F32))
            xhat, _ = _ln_stats(_gelu(zv_ref[rows, :].astype(F32)))
            vn = (xhat * gv_ref[...] + bv_ref[...]).astype(BF16)
            for hh in range(SGU_HEADS):
                cols = slice(hh * hd, (hh + 1) * hd)
                wmh = jnp.where(mask, wm_ref[hh], 0.0).astype(BF16)
                mixed = jnp.dot(wmh, vn[:, cols], preferred_element_type=F32) + bs_ref[hh]
                o_ref[rows, cols] = (u[:, cols] * mixed).astype(BF16)

    return pl.pallas_call(
        body, name=name, grid=(t // tm,),
        in_specs=[pl.BlockSpec((tm, d), lambda i: (i, 0)), pl.BlockSpec((tm, d), lambda i: (i, 1)),
                  _full(wm.shape), _full(bs3.shape), _full(gv.shape), _full(bv.shape)],
        out_specs=pl.BlockSpec((tm, d), lambda i: (i, 0)),
        out_shape=jax.ShapeDtypeStruct((t, d), BF16),
        compiler_params=_params(1))(proj, proj, wm, bs3, gv, bv)


def _conv_taps(scr_ref, r0, c0, base, weight):
    n = CONV_ROWS + CONV_HALO
    win = scr_ref[pl.ds(r0, n), pl.ds(c0, CONV_LANES)]
    acc = None
    for r in range(8):
        rolled = win if r == 0 else pltpu.roll(win, n - r, 0)
        for q in range((CONV_HALO + 7) // 8 + 1):
            k = 8 * q + r - base
            if 0 <= k < CONV_WIDTH and 8 * q + CONV_ROWS <= n:
                term = weight(k) * rolled[8 * q:8 * q + CONV_ROWS]
                acc = term if acc is None else acc + term
    return acc


def _glu_rows(a_ref, g_ref):
    return a_ref[...].astype(F32) * _sigmoid(g_ref[...].astype(F32))


def _conv_into(scr_ref, cv_ref, w_ref, tm, base, flip):
    def chunk(ci, carry):
        r0 = pl.multiple_of(ci * CONV_ROWS, CONV_ROWS)
        for c0 in range(0, D_MODEL, CONV_LANES):
            def weight(k, c0=c0):
                kk = CONV_WIDTH - 1 - k if flip else k
                return w_ref[kk:kk + 1, c0:c0 + CONV_LANES]
            cv_ref[pl.ds(r0, CONV_ROWS), pl.ds(c0, CONV_LANES)] = _conv_taps(scr_ref, r0, c0, base, weight)
        return carry

    lax.fori_loop(0, tm // CONV_ROWS, chunk, 0)


def _conv_specs(t, tm, d):
    hb = tm // CONV_HALO
    main = [pl.BlockSpec((tm, d), lambda i: (i, 2)), pl.BlockSpec((tm, d), lambda i: (i, 3))]
    halo = [pl.BlockSpec((CONV_HALO, d), lambda i: (jnp.maximum(i * hb - 1, 0), 2)),
            pl.BlockSpec((CONV_HALO, d), lambda i: (jnp.maximum(i * hb - 1, 0), 3))]
    return main, halo


def _fill_glu_history(scr_ref, a_ref, g_ref, ah_ref, gh_ref, tm):
    hist = _glu_rows(ah_ref, gh_ref)
    scr_ref[0:CONV_HALO, :] = jnp.where(pl.program_id(0) > 0, hist, 0.0)
    scr_ref[CONV_HALO:CONV_HALO + tm, :] = _glu_rows(a_ref, g_ref)


_CONV_BASE = CONV_HALO - (CONV_WIDTH - 1)


def _conv_fwd(proj, wdw, bdw, gln, bln, name, comm=None):
    t = proj.shape[0]
    d = D_MODEL
    tm = _tile(t, ROW_TILE)
    main, halo = _conv_specs(t, tm, d)

    def body(a_ref, g_ref, ah_ref, gh_ref, w_ref, b_ref, gl_ref, bl_ref, o_ref, cv_ref, scr_ref):
        _fill_glu_history(scr_ref, a_ref, g_ref, ah_ref, gh_ref, tm)
        _conv_into(scr_ref, cv_ref, w_ref, tm, _CONV_BASE, False)
        xhat, _ = _ln_stats(cv_ref[...] + b_ref[...])
        cn = xhat * gl_ref[...] + bl_ref[...]
        o_ref[...] = (cn * _sigmoid(cn)).astype(BF16)

    row = pl.BlockSpec((tm, d), lambda i: (i, 0))
    return _pcall(
        body, name=name, grid=(t // tm,),
        in_specs=main + halo + [_full(wdw.shape), _full(bdw.shape), _full(gln.shape), _full(bln.shape)],
        out_specs=[row, row],
        out_shape=[jax.ShapeDtypeStruct((t, d), BF16), jax.ShapeDtypeStruct((t, d), F32)],
        scratch_shapes=[pltpu.VMEM((tm + CONV_HALO, d), F32)],
        operands=(proj, proj, proj, proj, wdw, bdw, gln, bln), comm=comm)


def _pool_fill(scr_ref, z_ref, zh_ref, tm):
    scr_ref[0:POOL_HALO, :] = jnp.where(pl.program_id(0) > 0, zh_ref[...].astype(F32), 0.0)
    scr_ref[POOL_HALO:POOL_HALO + tm, :] = z_ref[...].astype(F32)


def _pool_count(t0, rows, w):
    pos = (t0 + lax.broadcasted_iota(jnp.int32, (rows, 1), 0) + 1).astype(F32)
    return jnp.minimum(pos, float(w))


def _pooled_group(scr_ref, gi, w, tm, t0):
    cols = pl.ds(gi * POOL_GROUP, POOL_GROUP)
    acc = scr_ref[pl.ds(POOL_HALO, tm), cols]
    z = acc
    for k in range(1, w):
        acc = acc + scr_ref[pl.ds(POOL_HALO - k, tm), cols]
    return acc / _pool_count(t0, tm, w) - z


def _pool_specs(tm, d):
    hb = tm // POOL_HALO
    return [pl.BlockSpec((tm, d), lambda i: (i, 4)),
            pl.BlockSpec((POOL_HALO, d), lambda i: (jnp.maximum(i * hb - 1, 0), 4))]


def _pool_fwd(proj, wpool, spool, name):
    t = proj.shape[0]
    d = D_MODEL
    tm = _tile(t, ROW_TILE)

    def body(z_ref, zh_ref, w_ref, s_ref, o_ref, scr_ref):
        _pool_fill(scr_ref, z_ref, zh_ref, tm)
        t0 = pl.program_id(0) * tm
        for gi, w in enumerate(POOL_WINDOWS):
            cols = slice(gi * POOL_GROUP, (gi + 1) * POOL_GROUP)
            pooled = _pooled_group(scr_ref, gi, w, tm, t0)
            o_ref[:, cols] = (_dot(pooled, w_ref[gi]) * s_ref[:, cols]).astype(BF16)

    return pl.pallas_call(
        body, name=name, grid=(t // tm,),
        in_specs=_pool_specs(tm, d) + [_full(wpool.shape), _full(spool.shape)],
        out_specs=pl.BlockSpec((tm, d), lambda i: (i, 0)),
        out_shape=jax.ShapeDtypeStruct((t, d), BF16),
        scratch_shapes=[pltpu.VMEM((tm + POOL_HALO, d), F32)],
        compiler_params=_params(1))(proj, proj, wpool, spool)


def _merge_fwd(proj, sg, cs, ps, wa, wb, wc, name):
    t = proj.shape[0]
    d = D_MODEL
    tm = _tile(t, ROW_TILE_HEAVY)

    def body(za_ref, zb_ref, zc_ref, sg_ref, cs_ref, ps_ref, wa_ref, wb_ref, wc_ref, ba_ref, bb_ref, bc_ref, m_ref):
        merged = None
        for z_ref, x_ref, w_ref, b_ref in ((za_ref, sg_ref, wa_ref, ba_ref), (zb_ref, cs_ref, wb_ref, bb_ref),
                                           (zc_ref, ps_ref, wc_ref, bc_ref)):
            br = jnp.dot(x_ref[...], w_ref[...], preferred_element_type=F32)
            b_ref[...] = br.astype(BF16)
            term = _sigmoid(z_ref[...].astype(F32)) * br
            merged = term if merged is None else merged + term
        m_ref[...] = merged.astype(BF16)

    row = pl.BlockSpec((tm, d), lambda i: (i, 0))
    wspec = _resident((d, d))
    return pl.pallas_call(
        body, name=name, grid=(t // tm,),
        in_specs=[pl.BlockSpec((tm, d), lambda i: (i, 5)), pl.BlockSpec((tm, d), lambda i: (i, 6)),
                  pl.BlockSpec((tm, d), lambda i: (i, 7)), row, row, row, wspec, wspec, wspec],
        out_specs=[row, row, row, row],
        out_shape=[jax.ShapeDtypeStruct((t, d), BF16)] * 4,
        compiler_params=_params(1))(proj, proj, proj, sg, cs, ps, wa, wb, wc)


def _mm_norm_res(a, w, g, hres, name, comm=None):
    t, k = a.shape
    d = w.shape[1]
    tm = _tile(t, ROW_TILE)

    def body(a_ref, w_ref, g_ref, h_ref, y_ref, o_ref):
        y = jnp.dot(a_ref[...], w_ref[...], preferred_element_type=F32)
        y_ref[...] = y
        yn, _ = _rms_stats(y)
        o_ref[...] = h_ref[...] + yn * g_ref[...]

    row = pl.BlockSpec((tm, d), lambda i: (i, 0))
    return _pcall(
        body, name=name, grid=(t // tm,),
        in_specs=[pl.BlockSpec((tm, k), lambda i: (i, 0)), _resident(w.shape), _full(g.shape), row],
        out_specs=[row, row],
        out_shape=[jax.ShapeDtypeStruct((t, d), F32)] * 2,
        operands=(a, w, g, hres), comm=comm)


def _ffn_in(h, g, w4, name, comm=None):
    t, d = h.shape
    n = w4.shape[2]
    tm = _tile(t, ROW_TILE)
    nj = D_FF // n

    def body(h_ref, g_ref, wg_ref, wu_ref, fg_ref, fu_ref, act_ref, hn_ref):
        @pl.when(pl.program_id(1) == 0)
        def _():
            xn, _ = _rms_stats(h_ref[...])
            hn_ref[...] = (xn * g_ref[...]).astype(BF16)

        fg = jnp.dot(hn_ref[...], wg_ref[...], preferred_element_type=F32)
        fu = jnp.dot(hn_ref[...], wu_ref[...], preferred_element_type=F32)
        fg_ref[...] = fg.astype(BF16)
        fu_ref[...] = fu.astype(BF16)
        act_ref[...] = (fg * _sigmoid(fg) * fu).astype(BF16)

    col = pl.BlockSpec((tm, n), lambda i, j: (i, j))
    return _pcall(
        body, name=name, grid=(t // tm, nj),
        in_specs=[pl.BlockSpec((tm, d), lambda i, j: (i, 0)), pl.BlockSpec((1, d), lambda i, j: (0, 0)),
                  pl.BlockSpec((None, d, n), lambda i, j: (j, 0, 0)),
                  pl.BlockSpec((None, d, n), lambda i, j: (j + nj, 0, 0))],
        out_specs=[col, col, col, pl.BlockSpec((tm, d), lambda i, j: (i, 0))],
        out_shape=[jax.ShapeDtypeStruct((t, D_FF), BF16)] * 3 + [jax.ShapeDtypeStruct((t, d), BF16)],
        operands=(h, g, w4, w4), comm=comm)


def _ple_fwd(h, p, wg, wp, name, comm=None):
    t, d = h.shape
    tm = _tile(t, ROW_TILE)

    def body(h_ref, p_ref, wg_ref, wp_ref, o_ref, q_ref, e_ref):
        hh = h_ref[...]
        q = _dot(hh, wg_ref[...])
        e = _dot(p_ref[...], wp_ref[...])
        q_ref[...] = q.astype(BF16)
        e_ref[...] = e.astype(BF16)
        o_ref[...] = hh + _sigmoid(q) * e

    row = pl.BlockSpec((tm, d), lambda i: (i, 0))
    return _pcall(
        body, name=name, grid=(t // tm,),
        in_specs=[row, pl.BlockSpec((tm, p.shape[1]), lambda i: (i, 0)), _resident(wg.shape), _resident(wp.shape)],
        out_specs=[row, row, row],
        out_shape=[jax.ShapeDtypeStruct((t, d), F32), jax.ShapeDtypeStruct((t, d), BF16),
                   jax.ShapeDtypeStruct((t, d), BF16)],
        operands=(h, p, wg, wp), comm=comm)


def _loss_head(y, target, name):
    t, d = y.shape
    tm = _tile(t, ROW_TILE)

    def body(y_ref, t_ref, dy_ref, l_ref):
        @pl.when(pl.program_id(0) == 0)
        def _():
            l_ref[...] = jnp.zeros_like(l_ref)

        err = y_ref[...] - t_ref[...]
        dy_ref[...] = err * (1.0 / d)
        l_ref[...] += jnp.sum(err * err, keepdims=True)[:, :1] * jnp.ones((1, 128), F32)

    row = pl.BlockSpec((tm, d), lambda i: (i, 0))
    return pl.pallas_call(
        body, name=name, grid=(t // tm,),
        in_specs=[row, row], out_specs=[row, _full((1, 128))],
        out_shape=[jax.ShapeDtypeStruct((t, d), F32), jax.ShapeDtypeStruct((1, 128), F32)],
        compiler_params=_params(1))(y, target)


def _ple_bwd(dh, q, e, wg, name):
    t, d = dh.shape
    tm = _tile(t, ROW_TILE)

    def body(dh_ref, q_ref, e_ref, wg_ref, dq_ref, de_ref, o_ref):
        dh_ = dh_ref[...]
        s = _sigmoid(q_ref[...].astype(F32))
        dq = (dh_ * e_ref[...].astype(F32) * s * (1.0 - s)).astype(BF16)
        dq_ref[...] = dq
        de_ref[...] = (dh_ * s).astype(BF16)
        o_ref[...] = dh_ + _dot_nt(dq, wg_ref[...])

    row = pl.BlockSpec((tm, d), lambda i: (i, 0))
    return pl.pallas_call(
        body, name=name, grid=(t // tm,),
        in_specs=[row, row, row, _resident(wg.shape)], out_specs=[row, row, row],
        out_shape=[jax.ShapeDtypeStruct((t, d), BF16), jax.ShapeDtypeStruct((t, d), BF16),
                   jax.ShapeDtypeStruct((t, d), F32)],
        compiler_params=_params(1))(dh, q, e, wg)


def _ffn_out_bwd(dh, f, g, fg, fu, w, name, comm=None):
    t, d = dh.shape
    tm = _tile(t, ROW_TILE_HEAVY)

    def body(dh_ref, f_ref, g_ref, fg_ref, fu_ref, w_ref, df_ref, dff_ref, dg_ref):
        @pl.when(pl.program_id(0) == 0)
        def _():
            dg_ref[...] = jnp.zeros_like(dg_ref)

        fn, r = _rms_stats(f_ref[...])
        df, dgt = _rms_bwd(fn, r, g_ref[...], dh_ref[...])
        dg_ref[...] += _rowsum(dgt)
        df = df.astype(BF16)
        df_ref[...] = df
        dact = _dot_nt(df, w_ref[...])
        fg_ = fg_ref[...].astype(F32)
        s = _sigmoid(fg_)
        dff_ref[:, 0:D_FF] = (dact * fu_ref[...].astype(F32) * (s * (1.0 + fg_ * (1.0 - s)))).astype(BF16)
        dff_ref[:, D_FF:2 * D_FF] = (dact * (fg_ * s)).astype(BF16)

    row = pl.BlockSpec((tm, d), lambda i: (i, 0))
    wide = pl.BlockSpec((tm, D_FF), lambda i: (i, 0))
    return _pcall(
        body, name=name, grid=(t // tm,),
        in_specs=[row, row, _full(g.shape), wide, wide, _resident(w.shape)],
        out_specs=[row, pl.BlockSpec((tm, 2 * D_FF), lambda i: (i, 0)), _full((1, d))],
        out_shape=[jax.ShapeDtypeStruct((t, d), BF16), jax.ShapeDtypeStruct((t, 2 * D_FF), BF16),
                   jax.ShapeDtypeStruct((1, d), F32)],
        operands=(dh, f, g, fg, fu, w), comm=comm)


def _in_bwd(pieces, w4, unit, h, g, dres, name):
    t, d = h.shape
    tm = _tile(t, ROW_TILE)
    per_chunk = w4.shape[2] // unit
    offs = []
    total = 0
    for _, nu in pieces:
        offs.append(total)
        total += nu
    n_p = len(pieces)

    def body(*refs):
        p_refs = refs[:n_p]
        w_ref, h_ref, g_ref, r_ref, o_ref, dg_ref, acc_ref = refs[n_p:]
        u = pl.program_id(1)

        @pl.when(u == 0)
        def _():
            acc_ref[...] = jnp.zeros_like(acc_ref)

        @pl.when((pl.program_id(0) == 0) & (u == 0))
        def _():
            dg_ref[...] = jnp.zeros_like(dg_ref)

        for p_ref, off, (_, nu) in zip(p_refs, offs, pieces):
            @pl.when((u >= off) & (u < off + nu))
            def _(p_ref=p_ref):
                acc_ref[...] += _dot_nt(p_ref[...], w_ref[...])

        @pl.when(u == total - 1)
        def _():
            xn, r = _rms_stats(h_ref[...])
            dx, dgt = _rms_bwd(xn, r, g_ref[...], acc_ref[...])
            dg_ref[...] += _rowsum(dgt)
            o_ref[...] = r_ref[...] + dx

    def piece_spec(off, nu):
        return pl.BlockSpec((tm, unit), lambda i, u: (i, jnp.clip(u - off, 0, nu - 1)))

    row = pl.BlockSpec((tm, d), lambda i, u: (i, 0))
    return pl.pallas_call(
        body, name=name, grid=(t // tm, total),
        in_specs=[piece_spec(off, nu) for off, (_, nu) in zip(offs, pieces)]
        + [pl.BlockSpec((None, d, unit), lambda i, u: (u // per_chunk, 0, u % per_chunk)), row,
           pl.BlockSpec((1, d), lambda i, u: (0, 0)), row],
        out_specs=[row, pl.BlockSpec((1, d), lambda i, u: (0, 0))],
        out_shape=[jax.ShapeDtypeStruct((t, d), F32), jax.ShapeDtypeStruct((1, d), F32)],
        scratch_shapes=[pltpu.VMEM((tm, d), F32)],
        compiler_params=_params(2))(*[a for a, _ in pieces], w4, h, g, dres)


def _lane_block(n, cap):
    return max(b for b in range(128, min(n, cap) + 1, 128) if n % b == 0)


def _mm_tn(x, dy, name, bn=None):
    t, m = x.shape
    n = dy.shape[1]
    bm = _lane_block(m, 1408)
    bn = bn or _lane_block(n, 1408)
    tk = _tile(t, 1024)

    def body(x_ref, dy_ref, o_ref):
        @pl.when(pl.program_id(2) == 0)
        def _():
            o_ref[...] = jnp.zeros_like(o_ref)

        o_ref[...] += _dot_tn(x_ref[...], dy_ref[...])

    return pl.pallas_call(
        body, name=name, grid=(m // bm, n // bn, t // tk),
        in_specs=[pl.BlockSpec((tk, bm), lambda a, b, k: (k, a)), pl.BlockSpec((tk, bn), lambda a, b, k: (k, b))],
        out_specs=pl.BlockSpec((bm, bn), lambda a, b, k: (a, b)),
        out_shape=jax.ShapeDtypeStruct((m, n), F32),
        compiler_params=_params(3))(x, dy)


def _merge_bwd(dh, mo, g, proj, bra, brb, brc, w_out, wa, wb, wc, name, comm=None):
    t, d = dh.shape
    tm = _tile(t, ROW_TILE_HEAVY)

    def body(dh_ref, mo_ref, g_ref, za_ref, zb_ref, zc_ref, ba_ref, bb_ref, bc_ref, wo_ref, wa_ref, wb_ref, wc_ref,
             dmo_ref, dba_ref, dbb_ref, dbc_ref, dz_ref, dsg_ref, dcs_ref, dps_ref, dg_ref):
        @pl.when(pl.program_id(0) == 0)
        def _():
            dg_ref[...] = jnp.zeros_like(dg_ref)

        mon, r = _rms_stats(mo_ref[...])
        dmo, dgt = _rms_bwd(mon, r, g_ref[...], dh_ref[...])
        dg_ref[...] += _rowsum(dgt)
        dmo = dmo.astype(BF16)
        dmo_ref[...] = dmo
        dmerged = _dot_nt(dmo, wo_ref[...])
        branches = ((za_ref, ba_ref, wa_ref, dba_ref, dsg_ref), (zb_ref, bb_ref, wb_ref, dbb_ref, dcs_ref),
                    (zc_ref, bc_ref, wc_ref, dbc_ref, dps_ref))
        for j, (z_ref, b_ref, w_ref, db_ref, dx_ref) in enumerate(branches):
            gate = _sigmoid(z_ref[...].astype(F32))
            dbr = (dmerged * gate).astype(BF16)
            db_ref[...] = dbr
            dz_ref[:, j * d:(j + 1) * d] = (dmerged * b_ref[...].astype(F32) * gate * (1.0 - gate)).astype(BF16)
            dx_ref[...] = _dot_nt(dbr, w_ref[...]).astype(BF16)

    row = pl.BlockSpec((tm, d), lambda i: (i, 0))
    wspec = _resident((d, d))
    bf = jax.ShapeDtypeStruct((t, d), BF16)
    return _pcall(
        body, name=name, grid=(t // tm,),
        in_specs=[row, row, _full(g.shape), pl.BlockSpec((tm, d), lambda i: (i, 5)),
                  pl.BlockSpec((tm, d), lambda i: (i, 6)), pl.BlockSpec((tm, d), lambda i: (i, 7)),
                  row, row, row, wspec, wspec, wspec, wspec],
        out_specs=[row, row, row, row, pl.BlockSpec((tm, 3 * d), lambda i: (i, 0)), row, row, row, _full((1, d))],
        out_shape=[bf, bf, bf, bf, jax.ShapeDtypeStruct((t, 3 * d), BF16), bf, bf, bf,
                   jax.ShapeDtypeStruct((1, d), F32)],
        operands=(dh, mo, g, proj, proj, proj, bra, brb, brc, w_out, wa, wb, wc), comm=comm)


def _sgu_bwd(proj, dsg, wm, bs3, gv, bv, name):
    t = proj.shape[0]
    d = D_MODEL
    tm = _tile(t, ROW_TILE_HEAVY)
    hd = d // SGU_HEADS

    def body(zu_ref, zv_ref, d_ref, wm_ref, bs_ref, gv_ref, bv_ref, dz_ref, dwm_ref, dbs_ref, dgv_ref, dbv_ref,
             dvn_ref):
        @pl.when(pl.program_id(0) == 0)
        def _():
            dwm_ref[...] = jnp.zeros_like(dwm_ref)
            dbs_ref[...] = jnp.zeros_like(dbs_ref)
            dgv_ref[...] = jnp.zeros_like(dgv_ref)
            dbv_ref[...] = jnp.zeros_like(dbv_ref)

        mask = _sgu_mask()
        for blk in range(tm // SGU_BLOCK):
            rows = pl.ds(blk * SGU_BLOCK, SGU_BLOCK)
            u, du_dz = _gelu_and_grad(zu_ref[rows, :].astype(F32))
            v0, dv_dz = _gelu_and_grad(zv_ref[rows, :].astype(F32))
            xhat, rstd = _ln_stats(v0)
            vn = (xhat * gv_ref[...] + bv_ref[...]).astype(BF16)
            dsg = d_ref[rows, :].astype(F32)
            dmix = (dsg * u).astype(BF16)
            for hh in range(SGU_HEADS):
                cols = slice(hh * hd, (hh + 1) * hd)
                wmh = jnp.where(mask, wm_ref[hh], 0.0).astype(BF16)
                vb = vn[:, cols]
                mixed = jnp.dot(wmh, vb, preferred_element_type=F32) + bs_ref[hh]
                dz_ref[rows, cols] = (dsg[:, cols] * mixed * du_dz[:, cols]).astype(BF16)
                dmh = dmix[:, cols]
                dwm_ref[hh] += jnp.where(mask, _dot_nt(dmh, vb), 0.0)
                dbs_ref[hh] += jnp.sum(dmh.astype(F32), axis=1, keepdims=True)
                dvn_ref[:, cols] = _dot_tn(wmh, dmh)
            dvn = dvn_ref[...]
            dgv_ref[...] += _rowsum(dvn * xhat)
            dbv_ref[...] += _rowsum(dvn)
            dz_ref[rows, d:2 * d] = (_ln_bwd(xhat, rstd, gv_ref[...], dvn) * dv_dz).astype(BF16)

    return pl.pallas_call(
        body, name=name, grid=(t // tm,),
        in_specs=[pl.BlockSpec((tm, d), lambda i: (i, 0)), pl.BlockSpec((tm, d), lambda i: (i, 1)),
                  pl.BlockSpec((tm, d), lambda i: (i, 0)), _full(wm.shape), _full(bs3.shape), _full(gv.shape),
                  _full(bv.shape)],
        out_specs=[pl.BlockSpec((tm, 2 * d), lambda i: (i, 0)), _full(wm.shape), _full(bs3.shape), _full((1, d)),
                   _full((1, d))],
        out_shape=[jax.ShapeDtypeStruct((t, 2 * d), BF16), jax.ShapeDtypeStruct(wm.shape, F32),
                   jax.ShapeDtypeStruct(bs3.shape, F32), jax.ShapeDtypeStruct((1, d), F32),
                   jax.ShapeDtypeStruct((1, d), F32)],
        scratch_shapes=[pltpu.VMEM((SGU_BLOCK, d), F32)],
        compiler_params=_params(1))(proj, proj, dsg, wm, bs3, gv, bv)


def _conv_bwd_norm(proj, dcs, cv, bdw, gln, bln, name, comm=None):
    t = proj.shape[0]
    d = D_MODEL
    tm = _tile(t, ROW_TILE)
    main, halo = _conv_specs(t, tm, d)
    n_win = CONV_ROWS + CONV_HALO

    def body(a_ref, g_ref, ah_ref, gh_ref, dcs_ref, cv_ref, b_ref, gl_ref, bl_ref,
             dcv_ref, dw_ref, db_ref, dgl_ref, dbl_ref, scr_ref, dwacc_ref):
        @pl.when(pl.program_id(0) == 0)
        def _():
            dwacc_ref[...] = jnp.zeros_like(dwacc_ref)
            db_ref[...] = jnp.zeros_like(db_ref)
            dgl_ref[...] = jnp.zeros_like(dgl_ref)
            dbl_ref[...] = jnp.zeros_like(dbl_ref)

        _fill_glu_history(scr_ref, a_ref, g_ref, ah_ref, gh_ref, tm)
        xhat, rstd = _ln_stats(cv_ref[...] + b_ref[...])
        cn = xhat * gl_ref[...] + bl_ref[...]
        s = _sigmoid(cn)
        dcn = dcs_ref[...].astype(F32) * (s * (1.0 + cn * (1.0 - s)))
        dgl_ref[...] += _rowsum(dcn * xhat)
        dbl_ref[...] += _rowsum(dcn)
        dcv = _ln_bwd(xhat, rstd, gl_ref[...], dcn)
        db_ref[...] += _rowsum(dcv)
        dcv_ref[...] = dcv

        def chunk(ci, carry):
            r0 = pl.multiple_of(ci * CONV_ROWS, CONV_ROWS)
            for c0 in range(0, d, CONV_LANES):
                lanes = pl.ds(c0, CONV_LANES)
                win = scr_ref[pl.ds(r0, n_win), lanes]
                dchunk = dcv_ref[pl.ds(r0, CONV_ROWS), lanes]
                for r in range(8):
                    rolled = win if r == 0 else pltpu.roll(win, n_win - r, 0)
                    for q in range(n_win // 8):
                        k = 8 * q + r - _CONV_BASE
                        if 0 <= k < CONV_WIDTH and 8 * q + CONV_ROWS <= n_win:
                            prod = dchunk * rolled[8 * q:8 * q + CONV_ROWS]
                            part = prod[0:8]
                            for s8 in range(8, CONV_ROWS, 8):
                                part = part + prod[s8:s8 + 8]
                            dwacc_ref[pl.ds(8 * k, 8), lanes] += part
            return carry

        lax.fori_loop(0, tm // CONV_ROWS, chunk, 0)

        @pl.when(pl.program_id(0) == pl.num_programs(0) - 1)
        def _():
            dw_ref[...] = jnp.sum(dwacc_ref[...].reshape(CONV_HALO, 8, d), axis=1)

    row = pl.BlockSpec((tm, d), lambda i: (i, 0))
    vec = _full((1, d))
    return _pcall(
        body, name=name, grid=(t // tm,),
        in_specs=main + halo + [row, row, vec, vec, vec],
        out_specs=[row, _full((CONV_HALO, d)), vec, vec, vec],
        out_shape=[jax.ShapeDtypeStruct((t, d), F32), jax.ShapeDtypeStruct((CONV_HALO, d), F32)]
        + [jax.ShapeDtypeStruct((1, d), F32)] * 3,
        scratch_shapes=[pltpu.VMEM((tm + CONV_HALO, d), F32), pltpu.VMEM((8 * CONV_HALO, d), F32)],
        operands=(proj, proj, proj, proj, dcs, cv, bdw, gln, bln), comm=comm)


def _conv_bwd_taps(proj, dcv, wdw, name):
    t = proj.shape[0]
    d = D_MODEL
    tm = _tile(t, ROW_TILE)
    hb = tm // CONV_HALO
    last_halo = t // CONV_HALO - 1

    def body(a_ref, g_ref, dcv_ref, dnext_ref, w_ref, dz_ref, scr_ref, dh_ref):
        scr_ref[0:tm, :] = dcv_ref[...]
        is_last = pl.program_id(0) == pl.num_programs(0) - 1
        scr_ref[tm:tm + CONV_HALO, :] = jnp.where(is_last, 0.0, dnext_ref[...])
        _conv_into(scr_ref, dh_ref, w_ref, tm, 0, True)
        dglu = dh_ref[...]
        a = a_ref[...].astype(F32)
        s = _sigmoid(g_ref[...].astype(F32))
        dz_ref[:, 0:d] = (dglu * s).astype(BF16)
        dz_ref[:, d:2 * d] = (dglu * a * s * (1.0 - s)).astype(BF16)

    return pl.pallas_call(
        body, name=name, grid=(t // tm,),
        in_specs=[pl.BlockSpec((tm, d), lambda i: (i, 2)), pl.BlockSpec((tm, d), lambda i: (i, 3)),
                  pl.BlockSpec((tm, d), lambda i: (i, 0)),
                  pl.BlockSpec((CONV_HALO, d), lambda i: (jnp.minimum((i + 1) * hb, last_halo), 0)),
                  _full(wdw.shape)],
        out_specs=pl.BlockSpec((tm, 2 * d), lambda i: (i, 0)),
        out_shape=jax.ShapeDtypeStruct((t, 2 * d), BF16),
        scratch_shapes=[pltpu.VMEM((tm + CONV_HALO, d), F32), pltpu.VMEM((tm, d), F32)],
        compiler_params=_params(1))(proj, proj, dcv, dcv, wdw)


def _pool_bwd(proj, dps, wpool, spool, name):
    t = proj.shape[0]
    d = D_MODEL
    tm = _tile(t, ROW_TILE)
    hb = tm // POOL_HALO
    last_halo = t // POOL_HALO - 1
    ext = tm + POOL_HALO

    def body(z_ref, zh_ref, d_ref, dnext_ref, w_ref, s_ref, dz_ref, dw_ref, ds_ref, scr_ref, dext_ref, dq_ref):
        @pl.when(pl.program_id(0) == 0)
        def _():
            dw_ref[...] = jnp.zeros_like(dw_ref)
            ds_ref[...] = jnp.zeros_like(ds_ref)

        _pool_fill(scr_ref, z_ref, zh_ref, tm)
        t0 = pl.program_id(0) * tm
        is_last = pl.program_id(0) == pl.num_programs(0) - 1
        dext_ref[0:tm, :] = d_ref[...].astype(F32)
        dext_ref[tm:ext, :] = jnp.where(is_last, 0.0, dnext_ref[...].astype(F32))
        for gi, w in enumerate(POOL_WINDOWS):
            cols = slice(gi * POOL_GROUP, (gi + 1) * POOL_GROUP)
            dps_ext = dext_ref[:, cols]
            dpm_ext = (dps_ext * s_ref[:, cols]).astype(BF16)
            dpooled_ext = _dot_nt(dpm_ext, w_ref[gi])
            dq_ref[...] = dpooled_ext / _pool_count(t0, ext, w)
            acc = dq_ref[pl.ds(0, tm), :]
            for k in range(1, w):
                acc = acc + dq_ref[pl.ds(k, tm), :]
            dz_ref[:, cols] = (acc - dpooled_ext[0:tm]).astype(BF16)
            pooled = _pooled_group(scr_ref, gi, w, tm, t0).astype(BF16)
            pm = jnp.dot(pooled, w_ref[gi], preferred_element_type=F32)
            ds_ref[:, cols] += _rowsum(dps_ext[0:tm] * pm)
            dw_ref[gi] += _dot_tn(pooled, dpm_ext[0:tm])

    return pl.pallas_call(
        body, name=name, grid=(t // tm,),
        in_specs=_pool_specs(tm, d) + [pl.BlockSpec((tm, d), lambda i: (i, 0)),
                                       pl.BlockSpec((POOL_HALO, d), lambda i: (jnp.minimum((i + 1) * hb, last_halo), 0)),
                                       _full(wpool.shape), _full(spool.shape)],
        out_specs=[pl.BlockSpec((tm, d), lambda i: (i, 0)), _full(wpool.shape), _full((1, d))],
        out_shape=[jax.ShapeDtypeStruct((t, d), BF16), jax.ShapeDtypeStruct(wpool.shape, F32),
                   jax.ShapeDtypeStruct((1, d), F32)],
        scratch_shapes=[pltpu.VMEM((tm + POOL_HALO, d), F32), pltpu.VMEM((ext, d), F32),
                        pltpu.VMEM((ext, POOL_GROUP), F32)],
        compiler_params=_params(1))(proj, proj, dps, dps, wpool, spool)


ANY = pl.BlockSpec(memory_space=pl.ANY)


def _mesh_pos():
    x, y, c = lax.axis_index("x"), lax.axis_index("y"), lax.axis_index("c")
    chips = [(1 - x, y), (x, 1 - y), (1 - x, 1 - y)]
    return x, y, c, chips


def _chip_of(xy):
    return 2 * xy[0] + xy[1]


def _half_view(a):
    return a.reshape(a.shape[:-2] + (2, a.shape[-2] // 2, a.shape[-1]))


def _same(arrs):
    return [jax.ShapeDtypeStruct(a.shape, a.dtype) for a in arrs]


def _in_place(n):
    return {g: g for g in range(n)}


def _sems(count):
    return [pltpu.SemaphoreType.DMA((count,)), pltpu.SemaphoreType.DMA((count,))]


def _gather_ici(bufs):
    n = len(bufs)

    def copy(buf, sems, g, j, chip):
        x, y, c, chips = _mesh_pos()
        slab = buf[g].at[chip, :, c]
        return pltpu.make_async_remote_copy(
            src_ref=slab, dst_ref=slab, send_sem=sems[0].at[3 * g + j], recv_sem=sems[1].at[3 * g + j],
            device_id=(*chips[j], c), device_id_type=MESH)

    def start(ins, buf, sems):
        x, y, c, chips = _mesh_pos()
        for g in range(n):
            for j in range(3):
                copy(buf, sems, g, j, 2 * x + y).start()

    def finish(ins, buf, sems):
        x, y, c, chips = _mesh_pos()
        for g in range(n):
            for j in range(3):
                copy(buf, sems, g, j, _chip_of(chips[j])).wait_recv()
        for g in range(n):
            for j in range(3):
                copy(buf, sems, g, j, 2 * x + y).wait_send()

    return _Payload(bufs, _same(bufs), _in_place(n), _sems(3 * n), start, finish)


def _gather_d2d(bufs):
    n = len(bufs)

    def copy(buf, sems, g, j, half):
        x, y, c, chips = _mesh_pos()
        slab = buf[g].at[_chip_of(chips[j]), :, half]
        return pltpu.make_async_remote_copy(
            src_ref=slab, dst_ref=slab, send_sem=sems[0].at[3 * g + j], recv_sem=sems[1].at[3 * g + j],
            device_id=(x, y, 1 - c), device_id_type=MESH)

    def start(ins, buf, sems):
        c = lax.axis_index("c")
        for g in range(n):
            for j in range(3):
                copy(buf, sems, g, j, c).start()

    def finish(ins, buf, sems):
        c = lax.axis_index("c")
        for g in range(n):
            for j in range(3):
                copy(buf, sems, g, j, 1 - c).wait_recv()
        for g in range(n):
            for j in range(3):
                copy(buf, sems, g, j, c).wait_send()

    return _Payload(bufs, _same(bufs), _in_place(n), _sems(3 * n), start, finish)


def _pair_exchange(grads):
    n = len(grads)

    def copy(src, dst, sems, g):
        x, y, c, _ = _mesh_pos()
        return pltpu.make_async_remote_copy(
            src_ref=src[g].at[:, :, 1 - c], dst_ref=dst[g], send_sem=sems[0].at[g], recv_sem=sems[1].at[g],
            device_id=(x, y, 1 - c), device_id_type=MESH)

    def start(src, dst, sems):
        for g in range(n):
            copy(src, dst, sems, g).start()

    def finish(src, dst, sems):
        for g in range(n):
            copy(src, dst, sems, g).wait()

    out_shape = [jax.ShapeDtypeStruct(g.shape[:2] + g.shape[3:], g.dtype) for g in grads]
    return _Payload(grads, out_shape, {}, _sems(n), start, finish)


def _chip_exchange(parts):
    n = len(parts)

    def copy(src, dst, sems, g, j, slot):
        x, y, c, chips = _mesh_pos()
        return pltpu.make_async_remote_copy(
            src_ref=src[g].at[_chip_of(chips[j])], dst_ref=dst[g].at[slot], send_sem=sems[0].at[3 * g + j],
            recv_sem=sems[1].at[3 * g + j], device_id=(*chips[j], c), device_id_type=MESH)

    def start(src, dst, sems):
        x, y, c, chips = _mesh_pos()
        for g in range(n):
            for j in range(3):
                copy(src, dst, sems, g, j, 2 * x + y).start()

    def finish(src, dst, sems):
        x, y, c, chips = _mesh_pos()
        for g in range(n):
            for j in range(3):
                copy(src, dst, sems, g, j, _chip_of(chips[j])).wait_recv()
        for g in range(n):
            for j in range(3):
                copy(src, dst, sems, g, j, 2 * x + y).wait_send()

    return _Payload(parts, _same(parts), {}, _sems(3 * n), start, finish)


def _pair_share(bufs):
    n = len(bufs)

    def copy(buf, sems, g, half):
        x, y, c, _ = _mesh_pos()
        slab = buf[g].at[:, :, half]
        return pltpu.make_async_remote_copy(
            src_ref=slab, dst_ref=slab, send_sem=sems[0].at[g], recv_sem=sems[1].at[g],
            device_id=(x, y, 1 - c), device_id_type=MESH)

    def start(ins, buf, sems):
        c = lax.axis_index("c")
        for g in range(n):
            copy(buf, sems, g, c).start()

    def finish(ins, buf, sems):
        c = lax.axis_index("c")
        for g in range(n):
            copy(buf, sems, g, 1 - c).wait_recv()
        for g in range(n):
            copy(buf, sems, g, c).wait_send()

    return _Payload(bufs, _same(bufs), _in_place(n), _sems(n), start, finish)


def _all_reduce_small(vec, name):
    r = vec.shape[0]

    def body(v_ref, o_ref, gath_ref, send_sem, recv_sem):
        x, y, c, _ = _mesh_pos()
        me = 4 * x + 2 * y + c
        gath_ref[me] = v_ref[...]
        copies = []
        for k in range(1, 8):
            peer = (x ^ (k >> 2), y ^ ((k >> 1) & 1), c ^ (k & 1))
            cp = pltpu.make_async_remote_copy(
                src_ref=v_ref, dst_ref=gath_ref.at[me], send_sem=send_sem.at[k - 1], recv_sem=recv_sem.at[k - 1],
                device_id=peer, device_id_type=MESH)
            cp.start()
            copies.append(cp)
        for k in range(1, 8):
            src_id = me ^ k
            pltpu.make_async_remote_copy(
                src_ref=v_ref, dst_ref=gath_ref.at[src_id], send_sem=send_sem.at[k - 1], recv_sem=recv_sem.at[k - 1],
                device_id=(x, y, c), device_id_type=MESH).wait_recv()
        for cp in copies:
            cp.wait_send()
        acc = gath_ref[0]
        for k in range(1, 8):
            acc = acc + gath_ref[k]
        o_ref[...] = acc

    return pl.pallas_call(
        body, name=name,
        in_specs=[pl.BlockSpec(memory_space=pltpu.VMEM)], out_specs=pl.BlockSpec(memory_space=pltpu.VMEM),
        out_shape=jax.ShapeDtypeStruct(vec.shape, F32),
        scratch_shapes=[pltpu.VMEM((8, r, 128), F32), pltpu.SemaphoreType.DMA((7,)), pltpu.SemaphoreType.DMA((7,))],
        compiler_params=pltpu.CompilerParams(has_side_effects=True, vmem_limit_bytes=VMEM_LIMIT))(vec)


def _row_block(rows, cols, mult=16):
    best = None
    for cand in range(mult, rows + 1, mult):
        if rows % cand == 0 and cand * cols * 4 <= EW_BLOCK_BYTES:
            best = cand
    return best or rows


POS_ME, POS_CORE = 0, 4


def _place(arrs, li, pos, dtype, name):
    s = len(arrs)
    _, rows, cols = arrs[0].shape
    rh = rows // 2
    tr = _row_block(rh, cols)
    nb = rh // tr

    def body(pos_ref, *refs):
        o_ref = refs[s]
        for j in range(s):
            @pl.when(pl.program_id(0) == j)
            def _(j=j):
                o_ref[...] = refs[j][...].astype(dtype)

    def in_spec(j):
        return pl.BlockSpec((None, tr, cols), lambda b, hf, i, pos_ref: (li, jnp.where(b == j, hf * nb + i, 0), 0))

    return pl.pallas_call(
        body, name=name,
        grid_spec=pltpu.PrefetchScalarGridSpec(
            num_scalar_prefetch=1, grid=(s, 2, nb), in_specs=[in_spec(j) for j in range(s)],
            out_specs=pl.BlockSpec((None, None, None, tr, cols),
                                   lambda b, hf, i, pos_ref: (pos_ref[POS_ME], b, hf, i, 0))),
        out_shape=jax.ShapeDtypeStruct((N_CHIPS, s, 2, rh, cols), dtype),
        compiler_params=_params(3))(pos, *arrs)


def _pair_sum(grad, recv, pos, out_dtype, name):
    _, s, rh, cols = recv.shape
    tr = _row_block(rh, cols)

    def body(pos_ref, g_ref, r_ref, o_ref):
        o_ref[...] = (g_ref[...] + r_ref[...]).astype(out_dtype)

    blk = (None, None, tr, cols)
    return pl.pallas_call(
        body, name=name,
        grid_spec=pltpu.PrefetchScalarGridSpec(
            num_scalar_prefetch=1, grid=(N_CHIPS, s, rh // tr),
            in_specs=[pl.BlockSpec((None, None, None, tr, cols),
                                   lambda a, b, i, pos_ref: (a, b, pos_ref[POS_CORE], i, 0)),
                      pl.BlockSpec(blk, lambda a, b, i, pos_ref: (a, b, i, 0))],
            out_specs=pl.BlockSpec(blk, lambda a, b, i, pos_ref: (a, b, i, 0))),
        out_shape=jax.ShapeDtypeStruct(recv.shape, out_dtype),
        compiler_params=_params(3))(pos, grad, recv)


def _chip_sum(part, landed, gbuf, li, n_layers, pos, name):
    _, s, rh, cols = part.shape
    tr = _row_block(rh, cols)

    def body(pos_ref, p_ref, a_ref, b_ref, c_ref, *rest):
        o_ref = rest[-1]
        o_ref[...] = ((p_ref[...].astype(F32) + a_ref[...].astype(F32)) + b_ref[...].astype(F32)) \
            + c_ref[...].astype(F32)

    def slot(k):
        return pl.BlockSpec((None, None, tr, cols), lambda b, i, pos_ref: (pos_ref[k], b, i, 0))

    in_specs = [slot(0), slot(1), slot(2), slot(3)]
    operands = [pos, part, landed, landed, landed]
    aliases = {}
    if gbuf is not None:
        in_specs.append(ANY)
        operands.append(gbuf)
        aliases = {len(operands) - 1: 0}
    return pl.pallas_call(
        body, name=name,
        grid_spec=pltpu.PrefetchScalarGridSpec(
            num_scalar_prefetch=1, grid=(s, rh // tr), in_specs=in_specs,
            out_specs=pl.BlockSpec((None, None, None, tr, cols),
                                   lambda b, i, pos_ref: (li, b, pos_ref[POS_CORE], i, 0))),
        out_shape=jax.ShapeDtypeStruct((n_layers, s, 2, rh, cols), F32),
        input_output_aliases=aliases,
        compiler_params=_params(2))(*operands)


def _adamw_math(w, g, m, v):
    m = ADAM_B1 * m + (1.0 - ADAM_B1) * g
    v = ADAM_B2 * v + (1.0 - ADAM_B2) * (g * g)
    m_hat = m / (1.0 - ADAM_B1 ** ADAM_STEP)
    v_hat = v / (1.0 - ADAM_B2 ** ADAM_STEP)
    delta = -ADAM_LR * (m_hat / (jnp.sqrt(v_hat) + ADAM_EPS) + ADAM_WD * w)
    return delta, m, v


def _adamw(w, g, slot, m, v, name):
    l, rows, cols = w.shape
    tr = _row_block(rows, cols, 8)

    def body(w_ref, g_ref, m_ref, v_ref, go_ref, d_ref, mo_ref, vo_ref):
        g_ = g_ref[...]
        delta, m_, v_ = _adamw_math(w_ref[...], g_, m_ref[...], v_ref[...])
        go_ref[...] = g_
        d_ref[...] = delta
        mo_ref[...] = m_
        vo_ref[...] = v_

    blk = pl.BlockSpec((None, tr, cols), lambda a, i: (a, i, 0))
    gblk = pl.BlockSpec((None, None, tr, cols), lambda a, i: (a, slot, i, 0))
    return pl.pallas_call(
        body, name=name, grid=(l, rows // tr), in_specs=[blk, gblk, blk, blk], out_specs=[blk] * 4,
        out_shape=[jax.ShapeDtypeStruct(w.shape, F32)] * 4,
        compiler_params=_params(2))(w, g, m, v)


SQ = ("w_sgu_out", "w_conv_out", "w_pool_out", "w_out", "w_ple_gate")
SMALL = ("g_mix_pre", "w_sgu_s", "b_sgu_s", "g_sgu_v", "b_sgu_v", "b_dw", "g_conv_ln", "b_conv_ln", "s_pool",
         "g_mix_post", "g_ffn_pre", "g_ffn_post")


def _layer_weights(gath, small, li):
    w = {}
    gath = {k: v.reshape(v.shape[:2] + (-1, v.shape[-1])) for k, v in gath.items()}
    w["w_in"] = gath["in"].reshape(N_CHIPS, D_MODEL, -1)
    for slot, nm in enumerate(SQ):
        w[nm] = gath["sq"][:, slot].reshape(D_MODEL, D_MODEL)
    w["w_ffn_in"] = gath["ffn_in"].reshape(N_CHIPS, D_MODEL, -1)
    w["w_ffn_out"] = gath["ffn_out"].reshape(D_FF, D_MODEL)
    mix = gath["mix"]
    w["w_ple"] = mix[:, 0].transpose(1, 0, 2).reshape(256, D_MODEL)
    w["w_pool"] = mix[:, 1].reshape(N_CHIPS, 4, 64, 256).transpose(1, 0, 2, 3).reshape(4, 256, 256)
    w["w_dw"] = gath["dw"].reshape(N_CHIPS, CONV_HALO, -1).transpose(1, 0, 2).reshape(CONV_HALO, D_MODEL)
    for nm in SMALL:
        w[nm] = small[nm][li]
    return w


def _vec(a):
    return a.reshape(1, -1)


def _layer_fwd(h, p, w, li, hosts=None):
    s = {}
    tag = "_l%d" % li
    s["h0"] = h
    proj, hn = _norm_mm(h, _vec(w["g_mix_pre"]), w["w_in"], "mix_in" + tag, _take(hosts, "mix_in"))
    s["proj"], s["hn"] = proj, hn
    bs3 = w["b_sgu_s"].reshape(SGU_HEADS, SGU_BLOCK, 1)
    s["sg"] = _sgu_fwd(proj, w["w_sgu_s"], bs3, _vec(w["g_sgu_v"]), _vec(w["b_sgu_v"]), "sgu_fwd" + tag)
    s["cs"], s["cv"] = _conv_fwd(proj, w["w_dw"], _vec(w["b_dw"]), _vec(w["g_conv_ln"]), _vec(w["b_conv_ln"]),
                                 "conv_fwd" + tag)
    s["ps"] = _pool_fwd(proj, w["w_pool"], _vec(w["s_pool"]), "pool_fwd" + tag)
    s["bra"], s["brb"], s["brc"], s["merged"] = _merge_fwd(
        proj, s["sg"], s["cs"], s["ps"], w["w_sgu_out"], w["w_conv_out"], w["w_pool_out"], "merge_fwd" + tag)
    s["mo"], h1 = _mm_norm_res(s["merged"], w["w_out"], _vec(w["g_mix_post"]), h, "mix_out" + tag)
    s["h1"] = h1
    s["fg"], s["fu"], s["act"], s["hn2"] = _ffn_in(h1, _vec(w["g_ffn_pre"]), w["w_ffn_in"], "ffn_in" + tag,
                                                   _take(hosts, "ffn_in"))
    s["f"], h2 = _mm_norm_res(s["act"], w["w_ffn_out"], _vec(w["g_ffn_post"]), h1, "ffn_out" + tag,
                              _take(hosts, "ffn_out"))
    s["h2"] = h2
    h3, s["q"], s["e"] = _ple_fwd(h2, p, w["w_ple_gate"], w["w_ple"], "ple_fwd" + tag, _take(hosts, "ple_fwd"))
    return h3, s


def _layer_bwd(dh3, p, w, s, li, hosts=None):
    tag = "_l%d" % li
    d = D_MODEL
    gs = {}
    dq, de, dh2 = _ple_bwd(dh3, s["q"], s["e"], w["w_ple_gate"], "ple_bwd" + tag)
    dw_ple = _mm_tn(p, de, "dw_ple" + tag)
    dw_ple_gate = _mm_tn(s["h2"], dq, "dw_ple_gate" + tag)
    df, dff, gs["g_ffn_post"] = _ffn_out_bwd(dh2, s["f"], _vec(w["g_ffn_post"]), s["fg"], s["fu"], w["w_ffn_out"],
                                             "ffn_out_bwd" + tag, _take(hosts, "ffn_out_bwd"))
    dw_ffn_out = _mm_tn(s["act"], df, "dw_ffn_out" + tag)
    n_ff = w["w_ffn_in"].shape[2]
    dh1, gs["g_ffn_pre"] = _in_bwd([(dff, 2 * D_FF // n_ff)], w["w_ffn_in"], n_ff, s["h1"], _vec(w["g_ffn_pre"]),
                                   dh2, "ffn_in_bwd" + tag)
    dw_ffn_in = _mm_tn(s["hn2"], dff, "dw_ffn_in" + tag, bn=n_ff)
    (dmo, dbra, dbrb, dbrc, dzg, dsg, dcs, dps, gs["g_mix_post"]) = _merge_bwd(
        dh1, s["mo"], _vec(w["g_mix_post"]), s["proj"], s["bra"], s["brb"], s["brc"], w["w_out"], w["w_sgu_out"],
        w["w_conv_out"], w["w_pool_out"], "merge_bwd" + tag, _take(hosts, "merge_bwd"))
    dw_out = _mm_tn(s["merged"], dmo, "dw_out" + tag)
    dw_sgu_out = _mm_tn(s["sg"], dbra, "dw_sgu_out" + tag)
    dw_conv_out = _mm_tn(s["cs"], dbrb, "dw_conv_out" + tag)
    dw_pool_out = _mm_tn(s["ps"], dbrc, "dw_pool_out" + tag)
    bs3 = w["b_sgu_s"].reshape(SGU_HEADS, SGU_BLOCK, 1)
    dz_sgu, gs["w_sgu_s"], dbs3, gs["g_sgu_v"], gs["b_sgu_v"] = _sgu_bwd(
        s["proj"], dsg, w["w_sgu_s"], bs3, _vec(w["g_sgu_v"]), _vec(w["b_sgu_v"]), "sgu_bwd" + tag)
    gs["b_sgu_s"] = dbs3
    dcv, dwdw, gs["b_dw"], gs["g_conv_ln"], gs["b_conv_ln"] = _conv_bwd_norm(
        s["proj"], dcs, s["cv"], _vec(w["b_dw"]), _vec(w["g_conv_ln"]), _vec(w["b_conv_ln"]), "conv_bwd_norm" + tag,
        _take(hosts, "conv_bwd_norm"))
    dz_conv = _conv_bwd_taps(s["proj"], dcv, w["w_dw"], "conv_bwd_taps" + tag)
    dz_pool, dwpool, gs["s_pool"] = _pool_bwd(s["proj"], dps, w["w_pool"], _vec(w["s_pool"]), "pool_bwd" + tag)
    pieces = [(dz_sgu, 2), (dz_conv, 2), (dz_pool, 1), (dzg, 3)]
    dh0, gs["g_mix_pre"] = _in_bwd(pieces, w["w_in"], d, s["h0"], _vec(w["g_mix_pre"]), dh1, "mix_in_bwd" + tag)
    dw_in = jnp.concatenate([_mm_tn(s["hn"], dz, "dw_in%d" % k + tag) for k, (dz, _) in enumerate(pieces)], axis=1)

    big = {}
    big["in"] = dw_in.reshape(d, N_CHIPS, 2 * d).transpose(1, 0, 2)[:, None]
    sq = {"w_sgu_out": dw_sgu_out, "w_conv_out": dw_conv_out, "w_pool_out": dw_pool_out, "w_out": dw_out,
          "w_ple_gate": dw_ple_gate}
    big["sq"] = jnp.stack([sq[nm].reshape(N_CHIPS, d // N_CHIPS, d) for nm in SQ], axis=1)
    big["ffn_in"] = dw_ffn_in.reshape(d, N_CHIPS, n_ff).transpose(1, 0, 2)[:, None]
    big["ffn_out"] = dw_ffn_out.reshape(N_CHIPS, 1, D_FF // N_CHIPS, d)
    gple = dw_ple.reshape(256, N_CHIPS, 256).transpose(1, 0, 2)
    gpool = dwpool.reshape(4, N_CHIPS, 64, 256).transpose(1, 0, 2, 3).reshape(N_CHIPS, 256, 256)
    big["mix"] = jnp.stack([gple, gpool], axis=1)
    big["dw"] = dwdw.reshape(CONV_HALO, N_CHIPS, 256).transpose(1, 0, 2)[:, None]
    return dh0, big, gs


GROUPS = ("in", "sq", "ffn_in", "ffn_out", "mix", "dw")
WIRE_DTYPE = {"in": BF16, "sq": BF16, "ffn_in": BF16, "ffn_out": BF16, "mix": BF16, "dw": F32}
GATHER_EARLY, GATHER_LATE = ("in", "sq", "mix", "dw"), ("ffn_in", "ffn_out")
REDUCE_EARLY, REDUCE_LATE = ("in", "ffn_out"), ("sq", "ffn_in", "mix", "dw")


def _group_members(wts):
    n_layers = wts["w_in"].shape[0]
    dw = wts["w_dw"].reshape(n_layers, CONV_WIDTH, -1)
    return {"in": [wts["w_in"]], "sq": [wts[nm] for nm in SQ], "ffn_in": [wts["w_ffn_in"]],
            "ffn_out": [wts["w_ffn_out"]],
            "mix": [wts["w_ple"], wts["w_pool"].reshape(n_layers, POOL_GROUP, POOL_GROUP)],
            "dw": [jnp.pad(dw, ((0, 0), (0, CONV_HALO - CONV_WIDTH), (0, 0)))]}


def _pack_small(tree):
    flat = jnp.concatenate([tree[nm].reshape(-1).astype(F32) for nm in SMALL])
    return flat.reshape(-1, 128)


def _unpack_small(packed, like):
    out, off = {}, 0
    flat = packed.reshape(-1)
    for nm in SMALL:
        n = like[nm].size
        out[nm] = flat[off:off + n].reshape(like[nm].shape)
        off += n
    return out


WEIGHTS = ("g_mix_pre", "w_in", "w_sgu_s", "b_sgu_s", "g_sgu_v", "b_sgu_v", "w_sgu_out", "w_dw", "b_dw", "g_conv_ln",
           "b_conv_ln", "w_conv_out", "w_pool", "s_pool", "w_pool_out", "w_out", "g_mix_post", "g_ffn_pre",
           "w_ffn_in", "w_ffn_out", "g_ffn_post", "w_ple", "w_ple_gate")


def kernel(x, p, g_mix_pre, w_in, w_sgu_s, b_sgu_s, g_sgu_v, b_sgu_v, w_sgu_out, w_dw, b_dw, g_conv_ln, b_conv_ln, w_conv_out, w_pool, s_pool, w_pool_out, w_out, g_mix_post, g_ffn_pre, w_ffn_in, w_ffn_out, g_ffn_post, w_ple, w_ple_gate, loss_target, m_g_mix_pre, m_w_in, m_w_sgu_s, m_b_sgu_s, m_g_sgu_v, m_b_sgu_v, m_w_sgu_out, m_w_dw, m_b_dw, m_g_conv_ln, m_b_conv_ln, m_w_conv_out, m_w_pool, m_s_pool, m_w_pool_out, m_w_out, m_g_mix_post, m_g_ffn_pre, m_w_ffn_in, m_w_ffn_out, m_g_ffn_post, m_w_ple, m_w_ple_gate, v_g_mix_pre, v_w_in, v_w_sgu_s, v_b_sgu_s, v_g_sgu_v, v_b_sgu_v, v_w_sgu_out, v_w_dw, v_b_dw, v_g_conv_ln, v_b_conv_ln, v_w_conv_out, v_w_pool, v_s_pool, v_w_pool_out, v_w_out, v_g_mix_post, v_g_ffn_pre, v_w_ffn_in, v_w_ffn_out, v_g_ffn_post, v_w_ple, v_w_ple_gate):
    args = dict(locals())
    wts = {nm: args[nm] for nm in WEIGHTS}
    mom = {nm: args["m_" + nm] for nm in WEIGHTS}
    var = {nm: args["v_" + nm] for nm in WEIGHTS}
    n_layers = w_in.shape[0]
    h = x.reshape(x.shape[1:])
    target = loss_target.reshape(loss_target.shape[1:])
    cx, cy, core = lax.axis_index("x"), lax.axis_index("y"), lax.axis_index("c")
    pos = jnp.stack([2 * cx + cy, 2 * (1 - cx) + cy, 2 * cx + (1 - cy), 2 * (1 - cx) + (1 - cy), core])
    pos = pos.astype(jnp.int32)

    members = _group_members(wts)

    def placed(li):
        return {g: _place(members[g], li, pos, WIRE_DTYPE[g], "place_%s_l%d" % (g, li)) for g in GROUPS}

    def gather_hosts(bufs, done):
        stage = {}

        def ici(keys):
            def make():
                stage[keys] = _gather_ici([bufs[g] for g in keys])
                return stage[keys]
            return make

        def d2d(keys):
            def make():
                pay = _gather_d2d(stage[keys].results)
                done.append((keys, pay))
                return pay
            return make

        return {"mix_in": ici(GATHER_EARLY), "ffn_in": ici(GATHER_LATE), "ffn_out": d2d(GATHER_EARLY),
                "ple_fwd": d2d(GATHER_LATE)}

    bufs = placed(0)
    first = _gather_ici([bufs[g] for g in GROUPS])
    _run_payload(first, "gather_ici_l0")
    gathered = dict(zip(GROUPS, _run_payload(_gather_d2d(first.results), "gather_d2d_l0")))

    saved, layer_w = [], []
    for li in range(n_layers):
        w = _layer_weights(gathered, wts, li)
        layer_w.append(w)
        done = []
        hosts = gather_hosts(placed(li + 1), done) if li + 1 < n_layers else None
        h, s = _layer_fwd(h, p[li, 0], w, li, hosts)
        saved.append(s)
        gathered = {g: r for keys, pay in done for g, r in zip(keys, pay.results)}
    dh, sq_err = _loss_head(h, target, "loss_head")
    loss = lax.psum(sq_err[0, 0] * (0.5 / D_MODEL), ("x", "y", "c"))

    def reduce_hosts(grads, tag, out):
        stage = {}

        def exchange():
            stage["recv"] = _pair_exchange([grads[g] for g in GROUPS])
            return stage["recv"]

        def chips(keys):
            def make():
                recv = dict(zip(GROUPS, stage["recv"].results))
                parts = [_pair_sum(grads[g], recv[g], pos, WIRE_DTYPE[g], "pair_sum_%s" % g + tag) for g in keys]
                pay = _chip_exchange(parts)
                out.append((keys, parts, pay))
                return pay
            return make

        return {"ffn_out_bwd": exchange, "merge_bwd": chips(REDUCE_EARLY), "conv_bwd_norm": chips(REDUCE_LATE)}

    def chip_sums(out, li, reduced):
        tag = "_l%d" % li
        for keys, parts, pay in out:
            for g, pt, ld in zip(keys, parts, pay.results):
                reduced[g] = _chip_sum(pt, ld, reduced.get(g), li, n_layers, pos, "chip_sum_%s" % g + tag)

    small_grads = [None] * n_layers
    reduced = {}
    hosts, out = None, []
    for li in reversed(range(n_layers)):
        dh, big, small_grads[li] = _layer_bwd(dh, p[li, 0], layer_w[li], saved[li], li, hosts)
        if hosts is not None:
            chip_sums(out, li + 1, reduced)
        grads = {g: _half_view(big[g]) for g in GROUPS}
        out = []
        hosts = reduce_hosts(grads, "_l%d" % li, out)
    grad_x = dh[None]
    _run_payload(hosts["ffn_out_bwd"](), "pair_exchange_l0")
    for key in ("merge_bwd", "conv_bwd_norm"):
        _run_payload(hosts[key](), "chip_exchange_%s_l0" % key)
    chip_sums(out, 0, reduced)

    shared = _run_payload(_pair_share([reduced[g] for g in GROUPS]), "pair_share")
    red = {g: b.reshape(b.shape[:2] + (-1, b.shape[-1])) for g, b in zip(GROUPS, shared)}

    where = {"w_in": ("in", 0), "w_ffn_in": ("ffn_in", 0), "w_ffn_out": ("ffn_out", 0), "w_ple": ("mix", 0),
             "w_pool": ("mix", 1)}
    for slot, nm in enumerate(SQ):
        where[nm] = ("sq", slot)
    outs = {}
    for nm, (g, slot) in where.items():
        shape = wts[nm].shape
        to3 = lambda a: a.reshape((n_layers,) + red[g].shape[2:])
        res = _adamw(to3(wts[nm]), red[g], slot, to3(mom[nm]), to3(var[nm]), "adamw_" + nm)
        outs[nm] = [r.reshape(shape) for r in res]
    gdw = red["dw"][:, :, :CONV_WIDTH]
    to3 = lambda a: a.reshape(n_layers, CONV_WIDTH, -1)
    res = _adamw(to3(wts["w_dw"]), gdw, 0, to3(mom["w_dw"]), to3(var["w_dw"]), "adamw_w_dw")
    outs["w_dw"] = [r.reshape(wts["w_dw"].shape) for r in res]

    small_tree = {nm: jnp.stack([small_grads[li][nm].reshape(wts[nm].shape[1:]) for li in range(n_layers)], axis=0)
                  for nm in SMALL}
    gsmall = _all_reduce_small(_pack_small(small_tree), "all_reduce_small")
    pk = lambda tree: _pack_small({nm: tree[nm] for nm in SMALL})[None]
    res = _adamw(pk(wts), gsmall[None, None], 0, pk(mom), pk(var), "adamw_small")
    unpacked = [_unpack_small(r[0], wts) for r in res]
    for nm in SMALL:
        outs[nm] = [u[nm] for u in unpacked]

    result = [loss, grad_x]
    for k in range(4):
        result += [outs[nm][k] for nm in WEIGHTS]
    return tuple(result)
```

```python
import functools

import jax
import jax.numpy as jnp
from jax import lax
from jax.experimental import pallas as pl
from jax.experimental.pallas import tpu as pltpu

F32 = jnp.float32
BF16 = jnp.bfloat16
MESH = pl.DeviceIdType.MESH

EPS = 1e-6
D_MODEL = 1024
SGU_BLOCK = 128
SGU_HEADS = 8
CHUNK = 64
CONV_WIDTH = 31
CONV_HALO = 32
POOL_WINDOWS = (2, 4, 8, 16)
POOL_HALO = 16
POOL_GROUP = 256
D_FF = 2816
N_CHIPS = 4

ADAM_LR = 0.001
ADAM_B1 = 0.9
ADAM_B2 = 0.999
ADAM_EPS = 1e-08
ADAM_WD = 0.01
ADAM_STEP = 10

VMEM_LIMIT = 52 * 1024 * 1024
ROW_TILE = 512
ROW_TILE_HEAVY = 256
CONV_ROWS = 32
CONV_LANES = 256
EW_BLOCK_BYTES = 2 * 1024 * 1024


def _params(n_grid):
    return pltpu.CompilerParams(dimension_semantics=("arbitrary",) * n_grid, vmem_limit_bytes=VMEM_LIMIT)


def _dot(a, b):
    return jnp.dot(a.astype(BF16), b.astype(BF16), preferred_element_type=F32)


def _dot_nt(a, b):
    return lax.dot_general(a.astype(BF16), b.astype(BF16), (((1,), (1,)), ((), ())), preferred_element_type=F32)


def _dot_tn(a, b):
    return lax.dot_general(a.astype(BF16), b.astype(BF16), (((0,), (0,)), ((), ())), preferred_element_type=F32)


def _sigmoid(x):
    return 1.0 / (1.0 + jnp.exp(-x))


_GELU_C = 0.7978845608028654
_GELU_A = 0.044715


def _gelu(x):
    t = jnp.tanh(_GELU_C * (x + _GELU_A * x * x * x))
    return 0.5 * x * (1.0 + t)


def _gelu_and_grad(x):
    x2 = x * x
    t = jnp.tanh(_GELU_C * (x + _GELU_A * x2 * x))
    g = 0.5 * (1.0 + t) + 0.5 * x * (1.0 - t * t) * (_GELU_C * (1.0 + 3.0 * _GELU_A * x2))
    return 0.5 * x * (1.0 + t), g


def _rms_stats(x):
    r = lax.rsqrt(jnp.mean(x * x, axis=-1, keepdims=True) + EPS)
    return x * r, r


def _rms_bwd(xn, r, g, dy):
    gd = dy * g
    return r * (gd - xn * jnp.mean(gd * xn, axis=-1, keepdims=True)), dy * xn


def _ln_stats(x):
    mu = jnp.mean(x, axis=-1, keepdims=True)
    xc = x - mu
    rstd = lax.rsqrt(jnp.mean(xc * xc, axis=-1, keepdims=True) + EPS)
    return xc * rstd, rstd


def _ln_bwd(xhat, rstd, g, dy):
    dxh = dy * g
    return rstd * (dxh - jnp.mean(dxh, axis=-1, keepdims=True) - xhat * jnp.mean(dxh * xhat, axis=-1, keepdims=True))


def _rowsum(x):
    return jnp.sum(x, axis=0, keepdims=True)


def _tile(t, want):
    return min(t, want)


def _full(shape):
    n = len(shape)
    return pl.BlockSpec(shape, lambda *_: (0,) * n)


def _resident(shape):
    n = len(shape)
    return pl.BlockSpec(shape, lambda *_: (0,) * n, pipeline_mode=pl.Buffered(1))


class _Payload:
    def __init__(self, operands, out_shape, aliases, scratch, start, finish):
        self.operands, self.out_shape, self.aliases, self.scratch = list(operands), list(out_shape), aliases, scratch
        self.start, self.finish = start, finish
        self.results = None


def _pcall(body, *, name, grid, in_specs, out_specs, out_shape, operands, scratch_shapes=(), comm=None):
    single = not isinstance(out_shape, (list, tuple))
    out_specs = [out_specs] if single else list(out_specs)
    out_shape = [out_shape] if single else list(out_shape)
    if comm is None:
        res = pl.pallas_call(
            body, name=name, grid=grid, in_specs=list(in_specs), out_specs=out_specs, out_shape=out_shape,
            scratch_shapes=list(scratch_shapes), compiler_params=_params(len(grid)))(*operands)
        return res[0] if single else res
    n_in, n_out, n_scr = len(in_specs), len(out_shape), len(scratch_shapes)
    ci, co = len(comm.operands), len(comm.out_shape)

    def hosted(*refs):
        bounds = [0, n_in, n_in + ci, n_in + ci + n_out, n_in + ci + n_out + co, n_in + ci + n_out + co + n_scr]
        a, b, c_, d_, s_ = [refs[lo:hi] for lo, hi in zip(bounds[:-1], bounds[1:])]
        t_ = refs[bounds[-1]:]
        ids = [pl.program_id(q) for q in range(len(grid))]
        first = functools.reduce(jnp.logical_and, [i == 0 for i in ids])
        last = functools.reduce(jnp.logical_and, [i == pl.num_programs(q) - 1 for q, i in enumerate(ids)])

        @pl.when(first)
        def _():
            comm.start(b, d_, t_)

        body(*a, *c_, *s_)

        @pl.when(last)
        def _():
            comm.finish(b, d_, t_)

    res = pl.pallas_call(
        hosted, name=name, grid=grid, in_specs=list(in_specs) + [ANY] * ci, out_specs=out_specs + [ANY] * co,
        out_shape=out_shape + comm.out_shape, scratch_shapes=list(scratch_shapes) + list(comm.scratch),
        input_output_aliases={n_in + i: n_out + o for i, o in comm.aliases.items()},
        compiler_params=pltpu.CompilerParams(dimension_semantics=("arbitrary",) * len(grid),
                                             vmem_limit_bytes=VMEM_LIMIT, has_side_effects=True),
    )(*operands, *comm.operands)
    comm.results = list(res[n_out:])
    res = res[:n_out]
    return res[0] if single else res


def _run_payload(comm, name):
    ci, co = len(comm.operands), len(comm.out_shape)

    def body(*refs):
        b, d_, t_ = refs[:ci], refs[ci:ci + co], refs[ci + co:]
        comm.start(b, d_, t_)
        comm.finish(b, d_, t_)

    res = pl.pallas_call(
        body, name=name, in_specs=[ANY] * ci, out_specs=[ANY] * co, out_shape=comm.out_shape,
        input_output_aliases=dict(comm.aliases), scratch_shapes=list(comm.scratch),
        compiler_params=pltpu.CompilerParams(has_side_effects=True))(*comm.operands)
    comm.results = list(res)
    return comm.results


def _take(hosts, key):
    return hosts[key]() if hosts and key in hosts else None


def _norm_mm(h, g, w4, name, comm=None):
    t, d = h.shape
    n = w4.shape[2]
    tm = _tile(t, ROW_TILE)

    step = _lane_block(n, 1024)

    def body(h_ref, g_ref, w_ref, o_ref, hn_ref):
        xn, _ = _rms_stats(h_ref[...])
        hn = (xn * g_ref[...]).astype(BF16)
        hn_ref[...] = hn
        for j in range(N_CHIPS):
            for c0 in range(0, n, step):
                o_ref[:, j * n + c0:j * n + c0 + step] = jnp.dot(
                    hn, w_ref[j, :, c0:c0 + step], preferred_element_type=F32).astype(BF16)

    return _pcall(
        body, name=name, grid=(t // tm,),
        in_specs=[pl.BlockSpec((tm, d), lambda i: (i, 0)), _full((1, d)), _resident(w4.shape)],
        out_specs=[pl.BlockSpec((tm, N_CHIPS * n), lambda i: (i, 0)), pl.BlockSpec((tm, d), lambda i: (i, 0))],
        out_shape=[jax.ShapeDtypeStruct((t, N_CHIPS * n), BF16), jax.ShapeDtypeStruct((t, d), BF16)],
        operands=(h, g, w4), comm=comm)


def _sgu_mask():
    ii = lax.broadcasted_iota(jnp.int32, (SGU_BLOCK, SGU_BLOCK), 0) // CHUNK
    jj = lax.broadcasted_iota(jnp.int32, (SGU_BLOCK, SGU_BLOCK), 1) // CHUNK
    return jj <= ii


def _sgu_fwd(proj, wm, bs3, gv, bv, name, comm=None):
    t = proj.shape[0]
    d = D_MODEL
    tm = _tile(t, ROW_TILE)
    hd = d // SGU_HEADS

    def body(zu_ref, zv_ref, wm_ref, bs_ref, gv_ref, bv_ref, o_ref):
        mask = _sgu_mask()
        for blk in range(tm // SGU_BLOCK):
            rows = pl.ds(blk * SGU_BLOCK, SGU_BLOCK)
            u = _gelu(zu_ref[rows, :].astype(F32))
            xhat, _ = _ln_stats(_gelu(zv_ref[rows, :].astype(F32)))
            vn = (xhat * gv_ref[...] + bv_ref[...]).astype(BF16)
            for hh in range(SGU_HEADS):
                cols = slice(hh * hd, (hh + 1) * hd)
                wmh = jnp.where(mask, wm_ref[hh], 0.0).astype(BF16)
                mixed = jnp.dot(wmh, vn[:, cols], preferred_element_type=F32) + bs_ref[hh]
                o_ref[rows, cols] = (u[:, cols] * mixed).astype(BF16)

    return _pcall(
        body, name=name, grid=(t // tm,),
        in_specs=[pl.BlockSpec((tm, d), lambda i: (i, 0)), pl.BlockSpec((tm, d), lambda i: (i, 1)),
                  _full(wm.shape), _full(bs3.shape), _full(gv.shape), _full(bv.shape)],
        out_specs=pl.BlockSpec((tm, d), lambda i: (i, 0)),
        out_shape=jax.ShapeDtypeStruct((t, d), BF16),
        operands=(proj, proj, wm, bs3, gv, bv), comm=comm)


def _conv_taps(scr_ref, r0, c0, base, weight):
    n = CONV_ROWS + CONV_HALO
    win = scr_ref[pl.ds(r0, n), pl.ds(c0, CONV_LANES)]
    acc = None
    for r in range(8):
        rolled = win if r == 0 else pltpu.roll(win, n - r, 0)
        for q in range((CONV_HALO + 7) // 8 + 1):
            k = 8 * q + r - base
            if 0 <= k < CONV_WIDTH and 8 * q + CONV_ROWS <= n:
                term = weight(k) * rolled[8 * q:8 * q + CONV_ROWS]
                acc = term if acc is None else acc + term
    return acc


def _glu_rows(a_ref, g_ref):
    return a_ref[...].astype(F32) * _sigmoid(g_ref[...].astype(F32))


def _conv_into(scr_ref, cv_ref, w_ref, tm, base, flip):
    def chunk(ci, carry):
        r0 = pl.multiple_of(ci * CONV_ROWS, CONV_ROWS)
        for c0 in range(0, D_MODEL, CONV_LANES):
            def weight(k, c0=c0):
                kk = CONV_WIDTH - 1 - k if flip else k
                return w_ref[kk:kk + 1, c0:c0 + CONV_LANES]
            cv_ref[pl.ds(r0, CONV_ROWS), pl.ds(c0, CONV_LANES)] = _conv_taps(scr_ref, r0, c0, base, weight)
        return carry

    lax.fori_loop(0, tm // CONV_ROWS, chunk, 0)


def _conv_specs(t, tm, d):
    hb = tm // CONV_HALO
    main = [pl.BlockSpec((tm, d), lambda i: (i, 2)), pl.BlockSpec((tm, d), lambda i: (i, 3))]
    halo = [pl.BlockSpec((CONV_HALO, d), lambda i: (jnp.maximum(i * hb - 1, 0), 2)),
            pl.BlockSpec((CONV_HALO, d), lambda i: (jnp.maximum(i * hb - 1, 0), 3))]
    return main, halo


def _fill_glu_history(scr_ref, a_ref, g_ref, ah_ref, gh_ref, tm):
    hist = _glu_rows(ah_ref, gh_ref)
    scr_ref[0:CONV_HALO, :] = jnp.where(pl.program_id(0) > 0, hist, 0.0)
    scr_ref[CONV_HALO:CONV_HALO + tm, :] = _glu_rows(a_ref, g_ref)


_CONV_BASE = CONV_HALO - (CONV_WIDTH - 1)


def _conv_fwd(proj, wdw, bdw, gln, bln, name, comm=None):
    t = proj.shape[0]
    d = D_MODEL
    tm = _tile(t, ROW_TILE)
    main, halo = _conv_specs(t, tm, d)

    def body(a_ref, g_ref, ah_ref, gh_ref, w_ref, b_ref, gl_ref, bl_ref, o_ref, cv_ref, scr_ref):
        _fill_glu_history(scr_ref, a_ref, g_ref, ah_ref, gh_ref, tm)
        _conv_into(scr_ref, cv_ref, w_ref, tm, _CONV_BASE, False)
        xhat, _ = _ln_stats(cv_ref[...] + b_ref[...])
        cn = xhat * gl_ref[...] + bl_ref[...]
        o_ref[...] = (cn * _sigmoid(cn)).astype(BF16)

    row = pl.BlockSpec((tm, d), lambda i: (i, 0))
    return _pcall(
        body, name=name, grid=(t // tm,),
        in_specs=main + halo + [_full(wdw.shape), _full(bdw.shape), _full(gln.shape), _full(bln.shape)],
        out_specs=[row, row],
        out_shape=[jax.ShapeDtypeStruct((t, d), BF16), jax.ShapeDtypeStruct((t, d), F32)],
        scratch_shapes=[pltpu.VMEM((tm + CONV_HALO, d), F32)],
        operands=(proj, proj, proj, proj, wdw, bdw, gln, bln), comm=comm)


def _pool_fill(scr_ref, z_ref, zh_ref, tm):
    scr_ref[0:POOL_HALO, :] = jnp.where(pl.program_id(0) > 0, zh_ref[...].astype(F32), 0.0)
    scr_ref[POOL_HALO:POOL_HALO + tm, :] = z_ref[...].astype(F32)


def _pool_count(t0, rows, w):
    pos = (t0 + lax.broadcasted_iota(jnp.int32, (rows, 1), 0) + 1).astype(F32)
    return jnp.minimum(pos, float(w))


def _pooled_group(scr_ref, gi, w, tm, t0):
    cols = pl.ds(gi * POOL_GROUP, POOL_GROUP)
    acc = scr_ref[pl.ds(POOL_HALO, tm), cols]
    z = acc
    for k in range(1, w):
        acc = acc + scr_ref[pl.ds(POOL_HALO - k, tm), cols]
    return acc / _pool_count(t0, tm, w) - z


def _pool_specs(tm, d):
    hb = tm // POOL_HALO
    return [pl.BlockSpec((tm, d), lambda i: (i, 4)),
            pl.BlockSpec((POOL_HALO, d), lambda i: (jnp.maximum(i * hb - 1, 0), 4))]


def _pool_fwd(proj, wpool, spool, name, comm=None):
    t = proj.shape[0]
    d = D_MODEL
    tm = _tile(t, ROW_TILE)

    def body(z_ref, zh_ref, w_ref, s_ref, o_ref, scr_ref):
        _pool_fill(scr_ref, z_ref, zh_ref, tm)
        t0 = pl.program_id(0) * tm
        for gi, w in enumerate(POOL_WINDOWS):
            cols = slice(gi * POOL_GROUP, (gi + 1) * POOL_GROUP)
            pooled = _pooled_group(scr_ref, gi, w, tm, t0)
            o_ref[:, cols] = (_dot(pooled, w_ref[gi]) * s_ref[:, cols]).astype(BF16)

    return _pcall(
        body, name=name, grid=(t // tm,),
        in_specs=_pool_specs(tm, d) + [_full(wpool.shape), _full(spool.shape)],
        out_specs=pl.BlockSpec((tm, d), lambda i: (i, 0)),
        out_shape=jax.ShapeDtypeStruct((t, d), BF16),
        scratch_shapes=[pltpu.VMEM((tm + POOL_HALO, d), F32)],
        operands=(proj, proj, wpool, spool), comm=comm)


def _merge_fwd(proj, sg, cs, ps, wa, wb, wc, name, comm=None):
    t = proj.shape[0]
    d = D_MODEL
    tm = _tile(t, ROW_TILE_HEAVY)

    def body(za_ref, zb_ref, zc_ref, sg_ref, cs_ref, ps_ref, wa_ref, wb_ref, wc_ref, ba_ref, bb_ref, bc_ref, m_ref):
        merged = None
        for z_ref, x_ref, w_ref, b_ref in ((za_ref, sg_ref, wa_ref, ba_ref), (zb_ref, cs_ref, wb_ref, bb_ref),
                                           (zc_ref, ps_ref, wc_ref, bc_ref)):
            br = jnp.dot(x_ref[...], w_ref[...], preferred_element_type=F32)
            b_ref[...] = br.astype(BF16)
            term = _sigmoid(z_ref[...].astype(F32)) * br
            merged = term if merged is None else merged + term
        m_ref[...] = merged.astype(BF16)

    row = pl.BlockSpec((tm, d), lambda i: (i, 0))
    wspec = _resident((d, d))
    return _pcall(
        body, name=name, grid=(t // tm,),
        in_specs=[pl.BlockSpec((tm, d), lambda i: (i, 5)), pl.BlockSpec((tm, d), lambda i: (i, 6)),
                  pl.BlockSpec((tm, d), lambda i: (i, 7)), row, row, row, wspec, wspec, wspec],
        out_specs=[row, row, row, row],
        out_shape=[jax.ShapeDtypeStruct((t, d), BF16)] * 4,
        operands=(proj, proj, proj, sg, cs, ps, wa, wb, wc), comm=comm)


def _mm_norm_res(a, w, g, hres, name, comm=None):
    t, k = a.shape
    d = w.shape[1]
    tm = _tile(t, ROW_TILE)

    def body(a_ref, w_ref, g_ref, h_ref, y_ref, o_ref):
        y = jnp.dot(a_ref[...], w_ref[...], preferred_element_type=F32)
        y_ref[...] = y
        yn, _ = _rms_stats(y)
        o_ref[...] = h_ref[...] + yn * g_ref[...]

    row = pl.BlockSpec((tm, d), lambda i: (i, 0))
    return _pcall(
        body, name=name, grid=(t // tm,),
        in_specs=[pl.BlockSpec((tm, k), lambda i: (i, 0)), _resident(w.shape), _full(g.shape), row],
        out_specs=[row, row],
        out_shape=[jax.ShapeDtypeStruct((t, d), F32)] * 2,
        operands=(a, w, g, hres), comm=comm)


def _ffn_in(h, g, w4, name, comm=None):
    t, d = h.shape
    n = w4.shape[2]
    tm = _tile(t, ROW_TILE)
    nj = D_FF // n

    def body(h_ref, g_ref, w_ref, fg_ref, fu_ref, act_ref, hn_ref):
        xn, _ = _rms_stats(h_ref[...])
        hn = (xn * g_ref[...]).astype(BF16)
        hn_ref[...] = hn
        for j in range(nj):
            cols = slice(j * n, (j + 1) * n)
            fg = jnp.dot(hn, w_ref[j], preferred_element_type=F32)
            fu = jnp.dot(hn, w_ref[j + nj], preferred_element_type=F32)
            fg_ref[:, cols] = fg.astype(BF16)
            fu_ref[:, cols] = fu.astype(BF16)
            act_ref[:, cols] = (fg * _sigmoid(fg) * fu).astype(BF16)

    wide = pl.BlockSpec((tm, D_FF), lambda i: (i, 0))
    return _pcall(
        body, name=name, grid=(t // tm,),
        in_specs=[pl.BlockSpec((tm, d), lambda i: (i, 0)), _full((1, d)), _resident(w4.shape)],
        out_specs=[wide, wide, wide, pl.BlockSpec((tm, d), lambda i: (i, 0))],
        out_shape=[jax.ShapeDtypeStruct((t, D_FF), BF16)] * 3 + [jax.ShapeDtypeStruct((t, d), BF16)],
        operands=(h, g, w4), comm=comm)


def _ple_fwd(h, p, wg, wp, name, comm=None):
    t, d = h.shape
    tm = _tile(t, ROW_TILE)

    def body(h_ref, p_ref, wg_ref, wp_ref, o_ref, q_ref, e_ref):
        hh = h_ref[...]
        q = _dot(hh, wg_ref[...])
        e = _dot(p_ref[...], wp_ref[...])
        q_ref[...] = q.astype(BF16)
        e_ref[...] = e.astype(BF16)
        o_ref[...] = hh + _sigmoid(q) * e

    row = pl.BlockSpec((tm, d), lambda i: (i, 0))
    return _pcall(
        body, name=name, grid=(t // tm,),
        in_specs=[row, pl.BlockSpec((tm, p.shape[1]), lambda i: (i, 0)), _resident(wg.shape), _resident(wp.shape)],
        out_specs=[row, row, row],
        out_shape=[jax.ShapeDtypeStruct((t, d), F32), jax.ShapeDtypeStruct((t, d), BF16),
                   jax.ShapeDtypeStruct((t, d), BF16)],
        operands=(h, p, wg, wp), comm=comm)


def _loss_head(y, target, name):
    t, d = y.shape
    tm = _tile(t, ROW_TILE)

    def body(y_ref, t_ref, dy_ref, l_ref):
        @pl.when(pl.program_id(0) == 0)
        def _():
            l_ref[...] = jnp.zeros_like(l_ref)

        err = y_ref[...] - t_ref[...]
        dy_ref[...] = err * (1.0 / d)
        l_ref[...] += jnp.sum(err * err, keepdims=True)[:, :1] * jnp.ones((1, 128), F32)

    row = pl.BlockSpec((tm, d), lambda i: (i, 0))
    return pl.pallas_call(
        body, name=name, grid=(t // tm,),
        in_specs=[row, row], out_specs=[row, _full((1, 128))],
        out_shape=[jax.ShapeDtypeStruct((t, d), F32), jax.ShapeDtypeStruct((1, 128), F32)],
        compiler_params=_params(1))(y, target)


def _ple_bwd(dh, q, e, wg, name):
    t, d = dh.shape
    tm = _tile(t, ROW_TILE)

    def body(dh_ref, q_ref, e_ref, wg_ref, dq_ref, de_ref, o_ref):
        dh_ = dh_ref[...]
        s = _sigmoid(q_ref[...].astype(F32))
        dq = (dh_ * e_ref[...].astype(F32) * s * (1.0 - s)).astype(BF16)
        dq_ref[...] = dq
        de_ref[...] = (dh_ * s).astype(BF16)
        o_ref[...] = dh_ + _dot_nt(dq, wg_ref[...])

    row = pl.BlockSpec((tm, d), lambda i: (i, 0))
    return pl.pallas_call(
        body, name=name, grid=(t // tm,),
        in_specs=[row, row, row, _resident(wg.shape)], out_specs=[row, row, row],
        out_shape=[jax.ShapeDtypeStruct((t, d), BF16), jax.ShapeDtypeStruct((t, d), BF16),
                   jax.ShapeDtypeStruct((t, d), F32)],
        compiler_params=_params(1))(dh, q, e, wg)


def _ffn_out_bwd(dh, f, g, fg, fu, w, name, comm=None):
    t, d = dh.shape
    tm = _tile(t, ROW_TILE_HEAVY)

    def body(dh_ref, f_ref, g_ref, fg_ref, fu_ref, w_ref, df_ref, dff_ref, dg_ref):
        @pl.when(pl.program_id(0) == 0)
        def _():
            dg_ref[...] = jnp.zeros_like(dg_ref)

        fn, r = _rms_stats(f_ref[...])
        df, dgt = _rms_bwd(fn, r, g_ref[...], dh_ref[...])
        dg_ref[...] += _rowsum(dgt)
        df = df.astype(BF16)
        df_ref[...] = df
        dact = _dot_nt(df, w_ref[...])
        fg_ = fg_ref[...].astype(F32)
        s = _sigmoid(fg_)
        dff_ref[:, 0:D_FF] = (dact * fu_ref[...].astype(F32) * (s * (1.0 + fg_ * (1.0 - s)))).astype(BF16)
        dff_ref[:, D_FF:2 * D_FF] = (dact * (fg_ * s)).astype(BF16)

    row = pl.BlockSpec((tm, d), lambda i: (i, 0))
    wide = pl.BlockSpec((tm, D_FF), lambda i: (i, 0))
    return _pcall(
        body, name=name, grid=(t // tm,),
        in_specs=[row, row, _full(g.shape), wide, wide, _resident(w.shape)],
        out_specs=[row, pl.BlockSpec((tm, 2 * D_FF), lambda i: (i, 0)), _full((1, d))],
        out_shape=[jax.ShapeDtypeStruct((t, d), BF16), jax.ShapeDtypeStruct((t, 2 * D_FF), BF16),
                   jax.ShapeDtypeStruct((1, d), F32)],
        operands=(dh, f, g, fg, fu, w), comm=comm)


def _in_bwd(pieces, w4, unit, h, g, dres, tm, name, comm=None):
    t, d = h.shape
    tm = _tile(t, tm)
    per_chunk = w4.shape[2] // unit
    n_p = len(pieces)

    def body(*refs):
        p_refs = refs[:n_p]
        w_ref, h_ref, g_ref, r_ref, o_ref, dg_ref = refs[n_p:]

        @pl.when(pl.program_id(0) == 0)
        def _():
            dg_ref[...] = jnp.zeros_like(dg_ref)

        acc = None
        u = 0
        for p_ref, (_, nu) in zip(p_refs, pieces):
            for k in range(nu):
                lanes = slice((u % per_chunk) * unit, (u % per_chunk + 1) * unit)
                term = _dot_nt(p_ref[:, k * unit:(k + 1) * unit], w_ref[u // per_chunk, :, lanes])
                acc = term if acc is None else acc + term
                u += 1
        xn, r = _rms_stats(h_ref[...])
        dx, dgt = _rms_bwd(xn, r, g_ref[...], acc)
        dg_ref[...] += _rowsum(dgt)
        o_ref[...] = r_ref[...] + dx

    row = pl.BlockSpec((tm, d), lambda i: (i, 0))
    return _pcall(
        body, name=name, grid=(t // tm,),
        in_specs=[pl.BlockSpec((tm, a.shape[1]), lambda i: (i, 0)) for a, _ in pieces]
        + [_resident(w4.shape), row, _full((1, d)), row],
        out_specs=[row, _full((1, d))],
        out_shape=[jax.ShapeDtypeStruct((t, d), F32), jax.ShapeDtypeStruct((1, d), F32)],
        operands=(*[a for a, _ in pieces], w4, h, g, dres), comm=comm)


def _lane_block(n, cap):
    return max(b for b in range(128, min(n, cap) + 1, 128) if n % b == 0)


def _mm_tn(x, dy, name, bn=None):
    t, m = x.shape
    n = dy.shape[1]
    bm = _lane_block(m, 1408)
    bn = bn or _lane_block(n, 1408)
    tk = _tile(t, 1024)

    def body(x_ref, dy_ref, o_ref):
        @pl.when(pl.program_id(2) == 0)
        def _():
            o_ref[...] = jnp.zeros_like(o_ref)

        o_ref[...] += _dot_tn(x_ref[...], dy_ref[...])

    return pl.pallas_call(
        body, name=name, grid=(m // bm, n // bn, t // tk),
        in_specs=[pl.BlockSpec((tk, bm), lambda a, b, k: (k, a)), pl.BlockSpec((tk, bn), lambda a, b, k: (k, b))],
        out_specs=pl.BlockSpec((bm, bn), lambda a, b, k: (a, b)),
        out_shape=jax.ShapeDtypeStruct((m, n), F32),
        compiler_params=_params(3))(x, dy)


def _merge_bwd(dh, mo, g, proj, bra, brb, brc, w_out, wa, wb, wc, name, comm=None):
    t, d = dh.shape
    tm = _tile(t, ROW_TILE_HEAVY)

    def body(dh_ref, mo_ref, g_ref, za_ref, zb_ref, zc_ref, ba_ref, bb_ref, bc_ref, wo_ref, wa_ref, wb_ref, wc_ref,
             dmo_ref, dba_ref, dbb_ref, dbc_ref, dz_ref, dsg_ref, dcs_ref, dps_ref, dg_ref):
        @pl.when(pl.program_id(0) == 0)
        def _():
            dg_ref[...] = jnp.zeros_like(dg_ref)

        mon, r = _rms_stats(mo_ref[...])
        dmo, dgt = _rms_bwd(mon, r, g_ref[...], dh_ref[...])
        dg_ref[...] += _rowsum(dgt)
        dmo = dmo.astype(BF16)
        dmo_ref[...] = dmo
        dmerged = _dot_nt(dmo, wo_ref[...])
        branches = ((za_ref, ba_ref, wa_ref, dba_ref, dsg_ref), (zb_ref, bb_ref, wb_ref, dbb_ref, dcs_ref),
                    (zc_ref, bc_ref, wc_ref, dbc_ref, dps_ref))
        for j, (z_ref, b_ref, w_ref, db_ref, dx_ref) in enumerate(branches):
            gate = _sigmoid(z_ref[...].astype(F32))
            dbr = (dmerged * gate).astype(BF16)
            db_ref[...] = dbr
            dz_ref[:, j * d:(j + 1) * d] = (dmerged * b_ref[...].astype(F32) * gate * (1.0 - gate)).astype(BF16)
            dx_ref[...] = _dot_nt(dbr, w_ref[...]).astype(BF16)

    row = pl.BlockSpec((tm, d), lambda i: (i, 0))
    wspec = _resident((d, d))
    bf = jax.ShapeDtypeStruct((t, d), BF16)
    return _pcall(
        body, name=name, grid=(t // tm,),
        in_specs=[row, row, _full(g.shape), pl.BlockSpec((tm, d), lambda i: (i, 5)),
                  pl.BlockSpec((tm, d), lambda i: (i, 6)), pl.BlockSpec((tm, d), lambda i: (i, 7)),
                  row, row, row, wspec, wspec, wspec, wspec],
        out_specs=[row, row, row, row, pl.BlockSpec((tm, 3 * d), lambda i: (i, 0)), row, row, row, _full((1, d))],
        out_shape=[bf, bf, bf, bf, jax.ShapeDtypeStruct((t, 3 * d), BF16), bf, bf, bf,
                   jax.ShapeDtypeStruct((1, d), F32)],
        operands=(dh, mo, g, proj, proj, proj, bra, brb, brc, w_out, wa, wb, wc), comm=comm)


def _sgu_bwd(proj, dsg, wm, bs3, gv, bv, name, comm=None):
    t = proj.shape[0]
    d = D_MODEL
    tm = _tile(t, ROW_TILE_HEAVY)
    hd = d // SGU_HEADS

    def body(zu_ref, zv_ref, d_ref, wm_ref, bs_ref, gv_ref, bv_ref, dz_ref, dwm_ref, dbs_ref, dgv_ref, dbv_ref,
             dvn_ref):
        @pl.when(pl.program_id(0) == 0)
        def _():
            dwm_ref[...] = jnp.zeros_like(dwm_ref)
            dbs_ref[...] = jnp.zeros_like(dbs_ref)
            dgv_ref[...] = jnp.zeros_like(dgv_ref)
            dbv_ref[...] = jnp.zeros_like(dbv_ref)

        mask = _sgu_mask()
        for blk in range(tm // SGU_BLOCK):
            rows = pl.ds(blk * SGU_BLOCK, SGU_BLOCK)
            u, du_dz = _gelu_and_grad(zu_ref[rows, :].astype(F32))
            v0, dv_dz = _gelu_and_grad(zv_ref[rows, :].astype(F32))
            xhat, rstd = _ln_stats(v0)
            vn = (xhat * gv_ref[...] + bv_ref[...]).astype(BF16)
            dsg = d_ref[rows, :].astype(F32)
            dmix = (dsg * u).astype(BF16)
            for hh in range(SGU_HEADS):
                cols = slice(hh * hd, (hh + 1) * hd)
                wmh = jnp.where(mask, wm_ref[hh], 0.0).astype(BF16)
                vb = vn[:, cols]
                mixed = jnp.dot(wmh, vb, preferred_element_type=F32) + bs_ref[hh]
                dz_ref[rows, cols] = (dsg[:, cols] * mixed * du_dz[:, cols]).astype(BF16)
                dmh = dmix[:, cols]
                dwm_ref[hh] += jnp.where(mask, _dot_nt(dmh, vb), 0.0)
                dbs_ref[hh] += jnp.sum(dmh.astype(F32), axis=1, keepdims=True)
                dvn_ref[:, cols] = _dot_tn(wmh, dmh)
            dvn = dvn_ref[...]
            dgv_ref[...] += _rowsum(dvn * xhat)
            dbv_ref[...] += _rowsum(dvn)
            dz_ref[rows, d:2 * d] = (_ln_bwd(xhat, rstd, gv_ref[...], dvn) * dv_dz).astype(BF16)

    return _pcall(
        body, name=name, grid=(t // tm,),
        in_specs=[pl.BlockSpec((tm, d), lambda i: (i, 0)), pl.BlockSpec((tm, d), lambda i: (i, 1)),
                  pl.BlockSpec((tm, d), lambda i: (i, 0)), _full(wm.shape), _full(bs3.shape), _full(gv.shape),
                  _full(bv.shape)],
        out_specs=[pl.BlockSpec((tm, 2 * d), lambda i: (i, 0)), _full(wm.shape), _full(bs3.shape), _full((1, d)),
                   _full((1, d))],
        out_shape=[jax.ShapeDtypeStruct((t, 2 * d), BF16), jax.ShapeDtypeStruct(wm.shape, F32),
                   jax.ShapeDtypeStruct(bs3.shape, F32), jax.ShapeDtypeStruct((1, d), F32),
                   jax.ShapeDtypeStruct((1, d), F32)],
        scratch_shapes=[pltpu.VMEM((SGU_BLOCK, d), F32)],
        operands=(proj, proj, dsg, wm, bs3, gv, bv), comm=comm)


def _conv_bwd_norm(proj, dcs, cv, bdw, gln, bln, name, comm=None):
    t = proj.shape[0]
    d = D_MODEL
    tm = _tile(t, ROW_TILE)
    main, halo = _conv_specs(t, tm, d)
    n_win = CONV_ROWS + CONV_HALO

    def body(a_ref, g_ref, ah_ref, gh_ref, dcs_ref, cv_ref, b_ref, gl_ref, bl_ref,
             dcv_ref, dw_ref, db_ref, dgl_ref, dbl_ref, scr_ref, dwacc_ref):
        @pl.when(pl.program_id(0) == 0)
        def _():
            dwacc_ref[...] = jnp.zeros_like(dwacc_ref)
            db_ref[...] = jnp.zeros_like(db_ref)
            dgl_ref[...] = jnp.zeros_like(dgl_ref)
            dbl_ref[...] = jnp.zeros_like(dbl_ref)

        _fill_glu_history(scr_ref, a_ref, g_ref, ah_ref, gh_ref, tm)
        xhat, rstd = _ln_stats(cv_ref[...] + b_ref[...])
        cn = xhat * gl_ref[...] + bl_ref[...]
        s = _sigmoid(cn)
        dcn = dcs_ref[...].astype(F32) * (s * (1.0 + cn * (1.0 - s)))
        dgl_ref[...] += _rowsum(dcn * xhat)
        dbl_ref[...] += _rowsum(dcn)
        dcv = _ln_bwd(xhat, rstd, gl_ref[...], dcn)
        db_ref[...] += _rowsum(dcv)
        dcv_ref[...] = dcv

        def chunk(ci, carry):
            r0 = pl.multiple_of(ci * CONV_ROWS, CONV_ROWS)
            for c0 in range(0, d, CONV_LANES):
                lanes = pl.ds(c0, CONV_LANES)
                win = scr_ref[pl.ds(r0, n_win), lanes]
                dchunk = dcv_ref[pl.ds(r0, CONV_ROWS), lanes]
                for r in range(8):
                    rolled = win if r == 0 else pltpu.roll(win, n_win - r, 0)
                    for q in range(n_win // 8):
                        k = 8 * q + r - _CONV_BASE
                        if 0 <= k < CONV_WIDTH and 8 * q + CONV_ROWS <= n_win:
                            prod = dchunk * rolled[8 * q:8 * q + CONV_ROWS]
                            part = prod[0:8]
                            for s8 in range(8, CONV_ROWS, 8):
                                part = part + prod[s8:s8 + 8]
                            dwacc_ref[pl.ds(8 * k, 8), lanes] += part
            return carry

        lax.fori_loop(0, tm // CONV_ROWS, chunk, 0)

        @pl.when(pl.program_id(0) == pl.num_programs(0) - 1)
        def _():
            dw_ref[...] = jnp.sum(dwacc_ref[...].reshape(CONV_HALO, 8, d), axis=1)

    row = pl.BlockSpec((tm, d), lambda i: (i, 0))
    vec = _full((1, d))
    return _pcall(
        body, name=name, grid=(t // tm,),
        in_specs=main + halo + [row, row, vec, vec, vec],
        out_specs=[row, _full((CONV_HALO, d)), vec, vec, vec],
        out_shape=[jax.ShapeDtypeStruct((t, d), F32), jax.ShapeDtypeStruct((CONV_HALO, d), F32)]
        + [jax.ShapeDtypeStruct((1, d), F32)] * 3,
        scratch_shapes=[pltpu.VMEM((tm + CONV_HALO, d), F32), pltpu.VMEM((8 * CONV_HALO, d), F32)],
        operands=(proj, proj, proj, proj, dcs, cv, bdw, gln, bln), comm=comm)


def _conv_bwd_taps(proj, dcv, wdw, name, comm=None):
    t = proj.shape[0]
    d = D_MODEL
    tm = _tile(t, ROW_TILE)
    hb = tm // CONV_HALO
    last_halo = t // CONV_HALO - 1

    def body(a_ref, g_ref, dcv_ref, dnext_ref, w_ref, dz_ref, scr_ref, dh_ref):
        scr_ref[0:tm, :] = dcv_ref[...]
        is_last = pl.program_id(0) == pl.num_programs(0) - 1
        scr_ref[tm:tm + CONV_HALO, :] = jnp.where(is_last, 0.0, dnext_ref[...])
        _conv_into(scr_ref, dh_ref, w_ref, tm, 0, True)
        dglu = dh_ref[...]
        a = a_ref[...].astype(F32)
        s = _sigmoid(g_ref[...].astype(F32))
        dz_ref[:, 0:d] = (dglu * s).astype(BF16)
        dz_ref[:, d:2 * d] = (dglu * a * s * (1.0 - s)).astype(BF16)

    return _pcall(
        body, name=name, grid=(t // tm,),
        in_specs=[pl.BlockSpec((tm, d), lambda i: (i, 2)), pl.BlockSpec((tm, d), lambda i: (i, 3)),
                  pl.BlockSpec((tm, d), lambda i: (i, 0)),
                  pl.BlockSpec((CONV_HALO, d), lambda i: (jnp.minimum((i + 1) * hb, last_halo), 0)),
                  _full(wdw.shape)],
        out_specs=pl.BlockSpec((tm, 2 * d), lambda i: (i, 0)),
        out_shape=jax.ShapeDtypeStruct((t, 2 * d), BF16),
        scratch_shapes=[pltpu.VMEM((tm + CONV_HALO, d), F32), pltpu.VMEM((tm, d), F32)],
        operands=(proj, proj, dcv, dcv, wdw), comm=comm)


def _pool_bwd(proj, dps, wpool, spool, name):
    t = proj.shape[0]
    d = D_MODEL
    tm = _tile(t, ROW_TILE)
    hb = tm // POOL_HALO
    last_halo = t // POOL_HALO - 1
    ext = tm + POOL_HALO

    def body(z_ref, zh_ref, d_ref, dnext_ref, w_ref, s_ref, dz_ref, dw_ref, ds_ref, scr_ref, dext_ref, dq_ref):
        @pl.when(pl.program_id(0) == 0)
        def _():
            dw_ref[...] = jnp.zeros_like(dw_ref)
            ds_ref[...] = jnp.zeros_like(ds_ref)

        _pool_fill(scr_ref, z_ref, zh_ref, tm)
        t0 = pl.program_id(0) * tm
        is_last = pl.program_id(0) == pl.num_programs(0) - 1
        dext_ref[0:tm, :] = d_ref[...].astype(F32)
        dext_ref[tm:ext, :] = jnp.where(is_last, 0.0, dnext_ref[...].astype(F32))
        for gi, w in enumerate(POOL_WINDOWS):
            cols = slice(gi * POOL_GROUP, (gi + 1) * POOL_GROUP)
            dps_ext = dext_ref[:, cols]
            dpm_ext = (dps_ext * s_ref[:, cols]).astype(BF16)
            dpooled_ext = _dot_nt(dpm_ext, w_ref[gi])
            dq_ref[...] = dpooled_ext / _pool_count(t0, ext, w)
            acc = dq_ref[pl.ds(0, tm), :]
            for k in range(1, w):
                acc = acc + dq_ref[pl.ds(k, tm), :]
            dz_ref[:, cols] = (acc - dpooled_ext[0:tm]).astype(BF16)
            pooled = _pooled_group(scr_ref, gi, w, tm, t0).astype(BF16)
            pm = jnp.dot(pooled, w_ref[gi], preferred_element_type=F32)
            ds_ref[:, cols] += _rowsum(dps_ext[0:tm] * pm)
            dw_ref[gi] += _dot_tn(pooled, dpm_ext[0:tm])

    return pl.pallas_call(
        body, name=name, grid=(t // tm,),
        in_specs=_pool_specs(tm, d) + [pl.BlockSpec((tm, d), lambda i: (i, 0)),
                                       pl.BlockSpec((POOL_HALO, d), lambda i: (jnp.minimum((i + 1) * hb, last_halo), 0)),
                                       _full(wpool.shape), _full(spool.shape)],
        out_specs=[pl.BlockSpec((tm, d), lambda i: (i, 0)), _full(wpool.shape), _full((1, d))],
        out_shape=[jax.ShapeDtypeStruct((t, d), BF16), jax.ShapeDtypeStruct(wpool.shape, F32),
                   jax.ShapeDtypeStruct((1, d), F32)],
        scratch_shapes=[pltpu.VMEM((tm + POOL_HALO, d), F32), pltpu.VMEM((ext, d), F32),
                        pltpu.VMEM((ext, POOL_GROUP), F32)],
        compiler_params=_params(1))(proj, proj, dps, dps, wpool, spool)


ANY = pl.BlockSpec(memory_space=pl.ANY)


def _mesh_pos():
    x, y, c = lax.axis_index("x"), lax.axis_index("y"), lax.axis_index("c")
    chips = [(1 - x, y), (x, 1 - y), (1 - x, 1 - y)]
    return x, y, c, chips


def _chip_of(xy):
    return 2 * xy[0] + xy[1]


def _half_view(a):
    return a.reshape(a.shape[:-2] + (2, a.shape[-2] // 2, a.shape[-1]))


def _same(arrs):
    return [jax.ShapeDtypeStruct(a.shape, a.dtype) for a in arrs]


def _in_place(n):
    return {g: g for g in range(n)}


def _sems(count):
    return [pltpu.SemaphoreType.DMA((count,)), pltpu.SemaphoreType.DMA((count,))]


def _gather_ici(bufs):
    n = len(bufs)

    def copy(buf, sems, g, j, chip):
        x, y, c, chips = _mesh_pos()
        slab = buf[g].at[chip, :, c]
        return pltpu.make_async_remote_copy(
            src_ref=slab, dst_ref=slab, send_sem=sems[0].at[3 * g + j], recv_sem=sems[1].at[3 * g + j],
            device_id=(*chips[j], c), device_id_type=MESH)

    def start(ins, buf, sems):
        x, y, c, chips = _mesh_pos()
        for g in range(n):
            for j in range(3):
                copy(buf, sems, g, j, 2 * x + y).start()

    def finish(ins, buf, sems):
        x, y, c, chips = _mesh_pos()
        for g in range(n):
            for j in range(3):
                copy(buf, sems, g, j, _chip_of(chips[j])).wait_recv()
        for g in range(n):
            for j in range(3):
                copy(buf, sems, g, j, 2 * x + y).wait_send()

    return _Payload(bufs, _same(bufs), _in_place(n), _sems(3 * n), start, finish)


def _gather_d2d(bufs):
    n = len(bufs)

    def copy(buf, sems, g, j, half):
        x, y, c, chips = _mesh_pos()
        slab = buf[g].at[_chip_of(chips[j]), :, half]
        return pltpu.make_async_remote_copy(
            src_ref=slab, dst_ref=slab, send_sem=sems[0].at[3 * g + j], recv_sem=sems[1].at[3 * g + j],
            device_id=(x, y, 1 - c), device_id_type=MESH)

    def start(ins, buf, sems):
        c = lax.axis_index("c")
        for g in range(n):
            for j in range(3):
                copy(buf, sems, g, j, c).start()

    def finish(ins, buf, sems):
        c = lax.axis_index("c")
        for g in range(n):
            for j in range(3):
                copy(buf, sems, g, j, 1 - c).wait_recv()
        for g in range(n):
            for j in range(3):
                copy(buf, sems, g, j, c).wait_send()

    return _Payload(bufs, _same(bufs), _in_place(n), _sems(3 * n), start, finish)


def _pair_exchange(grads):
    n = len(grads)

    def copy(src, dst, sems, g):
        x, y, c, _ = _mesh_pos()
        return pltpu.make_async_remote_copy(
            src_ref=src[g].at[:, :, 1 - c], dst_ref=dst[g], send_sem=sems[0].at[g], recv_sem=sems[1].at[g],
            device_id=(x, y, 1 - c), device_id_type=MESH)

    def start(src, dst, sems):
        for g in range(n):
            copy(src, dst, sems, g).start()

    def finish(src, dst, sems):
        for g in range(n):
            copy(src, dst, sems, g).wait()

    out_shape = [jax.ShapeDtypeStruct(g.shape[:2] + g.shape[3:], g.dtype) for g in grads]
    return _Payload(grads, out_shape, {}, _sems(n), start, finish)


def _chip_exchange(parts):
    n = len(parts)

    def copy(src, dst, sems, g, j, slot):
        x, y, c, chips = _mesh_pos()
        return pltpu.make_async_remote_copy(
            src_ref=src[g].at[_chip_of(chips[j])], dst_ref=dst[g].at[slot], send_sem=sems[0].at[3 * g + j],
            recv_sem=sems[1].at[3 * g + j], device_id=(*chips[j], c), device_id_type=MESH)

    def start(src, dst, sems):
        x, y, c, chips = _mesh_pos()
        for g in range(n):
            for j in range(3):
                copy(src, dst, sems, g, j, 2 * x + y).start()

    def finish(src, dst, sems):
        x, y, c, chips = _mesh_pos()
        for g in range(n):
            for j in range(3):
                copy(src, dst, sems, g, j, _chip_of(chips[j])).wait_recv()
        for g in range(n):
            for j in range(3):
                copy(src, dst, sems, g, j, 2 * x + y).wait_send()

    return _Payload(parts, _same(parts), {}, _sems(3 * n), start, finish)


def _pair_share(bufs):
    n = len(bufs)

    def copy(buf, sems, g, half):
        x, y, c, _ = _mesh_pos()
        slab = buf[g].at[:, :, half]
        return pltpu.make_async_remote_copy(
            src_ref=slab, dst_ref=slab, send_sem=sems[0].at[g], recv_sem=sems[1].at[g],
            device_id=(x, y, 1 - c), device_id_type=MESH)

    def start(ins, buf, sems):
        c = lax.axis_index("c")
        for g in range(n):
            copy(buf, sems, g, c).start()

    def finish(ins, buf, sems):
        c = lax.axis_index("c")
        for g in range(n):
            copy(buf, sems, g, 1 - c).wait_recv()
        for g in range(n):
            copy(buf, sems, g, c).wait_send()

    return _Payload(bufs, _same(bufs), _in_place(n), _sems(n), start, finish)


def _all_reduce_small(vec, name):
    r = vec.shape[0]

    def body(v_ref, o_ref, gath_ref, send_sem, recv_sem):
        x, y, c, _ = _mesh_pos()
        me = 4 * x + 2 * y + c
        gath_ref[me] = v_ref[...]
        copies = []
        for k in range(1, 8):
            peer = (x ^ (k >> 2), y ^ ((k >> 1) & 1), c ^ (k & 1))
            cp = pltpu.make_async_remote_copy(
                src_ref=v_ref, dst_ref=gath_ref.at[me], send_sem=send_sem.at[k - 1], recv_sem=recv_sem.at[k - 1],
                device_id=peer, device_id_type=MESH)
            cp.start()
            copies.append(cp)
        for k in range(1, 8):
            src_id = me ^ k
            pltpu.make_async_remote_copy(
                src_ref=v_ref, dst_ref=gath_ref.at[src_id], send_sem=send_sem.at[k - 1], recv_sem=recv_sem.at[k - 1],
                device_id=(x, y, c), device_id_type=MESH).wait_recv()
        for cp in copies:
            cp.wait_send()
        acc = gath_ref[0]
        for k in range(1, 8):
            acc = acc + gath_ref[k]
        o_ref[...] = acc

    return pl.pallas_call(
        body, name=name,
        in_specs=[pl.BlockSpec(memory_space=pltpu.VMEM)], out_specs=pl.BlockSpec(memory_space=pltpu.VMEM),
        out_shape=jax.ShapeDtypeStruct(vec.shape, F32),
        scratch_shapes=[pltpu.VMEM((8, r, 128), F32), pltpu.SemaphoreType.DMA((7,)), pltpu.SemaphoreType.DMA((7,))],
        compiler_params=pltpu.CompilerParams(has_side_effects=True, vmem_limit_bytes=VMEM_LIMIT))(vec)


def _row_block(rows, cols, mult=16):
    best = None
    for cand in range(mult, rows + 1, mult):
        if rows % cand == 0 and cand * cols * 4 <= EW_BLOCK_BYTES:
            best = cand
    return best or rows


POS_ME, POS_CORE = 0, 4


def _place(arrs, li, pos, dtype, name):
    s = len(arrs)
    _, rows, cols = arrs[0].shape
    rh = rows // 2
    tr = _row_block(rh, cols)
    nb = rh // tr

    def body(pos_ref, *refs):
        o_ref = refs[s]
        for j in range(s):
            @pl.when(pl.program_id(0) == j)
            def _(j=j):
                o_ref[...] = refs[j][...].astype(dtype)

    def in_spec(j):
        return pl.BlockSpec((None, tr, cols), lambda b, hf, i, pos_ref: (li, jnp.where(b == j, hf * nb + i, 0), 0))

    return pl.pallas_call(
        body, name=name,
        grid_spec=pltpu.PrefetchScalarGridSpec(
            num_scalar_prefetch=1, grid=(s, 2, nb), in_specs=[in_spec(j) for j in range(s)],
            out_specs=pl.BlockSpec((None, None, None, tr, cols),
                                   lambda b, hf, i, pos_ref: (pos_ref[POS_ME], b, hf, i, 0))),
        out_shape=jax.ShapeDtypeStruct((N_CHIPS, s, 2, rh, cols), dtype),
        compiler_params=_params(3))(pos, *arrs)


def _pair_sum(grad, recv, pos, out_dtype, name):
    _, s, rh, cols = recv.shape
    tr = _row_block(rh, cols)

    def body(pos_ref, g_ref, r_ref, o_ref):
        o_ref[...] = (g_ref[...] + r_ref[...]).astype(out_dtype)

    blk = (None, None, tr, cols)
    return pl.pallas_call(
        body, name=name,
        grid_spec=pltpu.PrefetchScalarGridSpec(
            num_scalar_prefetch=1, grid=(N_CHIPS, s, rh // tr),
            in_specs=[pl.BlockSpec((None, None, None, tr, cols),
                                   lambda a, b, i, pos_ref: (a, b, pos_ref[POS_CORE], i, 0)),
                      pl.BlockSpec(blk, lambda a, b, i, pos_ref: (a, b, i, 0))],
            out_specs=pl.BlockSpec(blk, lambda a, b, i, pos_ref: (a, b, i, 0))),
        out_shape=jax.ShapeDtypeStruct(recv.shape, out_dtype),
        compiler_params=_params(3))(pos, grad, recv)


def _chip_sum(part, landed, gbuf, li, n_layers, pos, name):
    _, s, rh, cols = part.shape
    tr = _row_block(rh, cols)

    def body(pos_ref, p_ref, a_ref, b_ref, c_ref, *rest):
        o_ref = rest[-1]
        o_ref[...] = ((p_ref[...].astype(F32) + a_ref[...].astype(F32)) + b_ref[...].astype(F32)) \
            + c_ref[...].astype(F32)

    def slot(k):
        return pl.BlockSpec((None, None, tr, cols), lambda b, i, pos_ref: (pos_ref[k], b, i, 0))

    in_specs = [slot(0), slot(1), slot(2), slot(3)]
    operands = [pos, part, landed, landed, landed]
    aliases = {}
    if gbuf is not None:
        in_specs.append(ANY)
        operands.append(gbuf)
        aliases = {len(operands) - 1: 0}
    return pl.pallas_call(
        body, name=name,
        grid_spec=pltpu.PrefetchScalarGridSpec(
            num_scalar_prefetch=1, grid=(s, rh // tr), in_specs=in_specs,
            out_specs=pl.BlockSpec((None, None, None, tr, cols),
                                   lambda b, i, pos_ref: (li, b, pos_ref[POS_CORE], i, 0))),
        out_shape=jax.ShapeDtypeStruct((n_layers, s, 2, rh, cols), F32),
        input_output_aliases=aliases,
        compiler_params=_params(2))(*operands)


def _adamw_math(w, g, m, v):
    m = ADAM_B1 * m + (1.0 - ADAM_B1) * g
    v = ADAM_B2 * v + (1.0 - ADAM_B2) * (g * g)
    m_hat = m / (1.0 - ADAM_B1 ** ADAM_STEP)
    v_hat = v / (1.0 - ADAM_B2 ** ADAM_STEP)
    delta = -ADAM_LR * (m_hat / (jnp.sqrt(v_hat) + ADAM_EPS) + ADAM_WD * w)
    return delta, m, v


def _adamw(w, g, slot, m, v, name):
    l, rows, cols = w.shape
    tr = _row_block(rows, cols, 8)

    def body(w_ref, g_ref, m_ref, v_ref, go_ref, d_ref, mo_ref, vo_ref):
        g_ = g_ref[...]
        delta, m_, v_ = _adamw_math(w_ref[...], g_, m_ref[...], v_ref[...])
        go_ref[...] = g_
        d_ref[...] = delta
        mo_ref[...] = m_
        vo_ref[...] = v_

    blk = pl.BlockSpec((None, tr, cols), lambda a, i: (a, i, 0))
    gblk = pl.BlockSpec((None, None, tr, cols), lambda a, i: (a, slot, i, 0))
    return pl.pallas_call(
        body, name=name, grid=(l, rows // tr), in_specs=[blk, gblk, blk, blk], out_specs=[blk] * 4,
        out_shape=[jax.ShapeDtypeStruct(w.shape, F32)] * 4,
        compiler_params=_params(2))(w, g, m, v)


SQ = ("w_sgu_out", "w_conv_out", "w_pool_out", "w_out", "w_ple_gate")
SMALL = ("g_mix_pre", "w_sgu_s", "b_sgu_s", "g_sgu_v", "b_sgu_v", "b_dw", "g_conv_ln", "b_conv_ln", "s_pool",
         "g_mix_post", "g_ffn_pre", "g_ffn_post")


WHERE = {"w_in": ("in", 0), "w_ffn_in": ("ffn_in", 0), "w_ffn_out": ("ffn_out", 0), "w_ple": ("mix", 0),
         "w_pool": ("mix", 1), "w_dw": ("dw", 0)}
WHERE.update({nm: ("sq", slot) for slot, nm in enumerate(SQ)})


class _LayerWeights:
    def __init__(self, fetch, small, li):
        self.fetch, self.small, self.li, self.cache = fetch, small, li, {}

    def __getitem__(self, nm):
        if nm not in self.cache:
            self.cache[nm] = self._big(nm) if nm in WHERE else self.small[nm][self.li]
        return self.cache[nm]

    def _big(self, nm):
        group, slot = WHERE[nm]
        g = self.fetch(group)
        g = g.reshape(g.shape[:2] + (-1, g.shape[-1]))
        if nm in ("w_in", "w_ffn_in"):
            return g.reshape(N_CHIPS, D_MODEL, -1)
        if nm == "w_ffn_out":
            return g.reshape(D_FF, D_MODEL)
        if nm in SQ:
            return g[:, slot].reshape(D_MODEL, D_MODEL)
        if nm == "w_ple":
            return g[:, slot].transpose(1, 0, 2).reshape(256, D_MODEL)
        if nm == "w_pool":
            return g[:, slot].reshape(N_CHIPS, 4, 64, 256).transpose(1, 0, 2, 3).reshape(4, 256, 256)
        return g.reshape(N_CHIPS, CONV_HALO, -1).transpose(1, 0, 2).reshape(CONV_HALO, D_MODEL)


def _vec(a):
    return a.reshape(1, -1)


def _layer_fwd(h, p, w, li, hosts=None):
    s = {}
    tag = "_l%d" % li
    s["h0"] = h
    proj, hn = _norm_mm(h, _vec(w["g_mix_pre"]), w["w_in"], "mix_in" + tag, _take(hosts, "mix_in"))
    s["proj"], s["hn"] = proj, hn
    bs3 = w["b_sgu_s"].reshape(SGU_HEADS, SGU_BLOCK, 1)
    s["sg"] = _sgu_fwd(proj, w["w_sgu_s"], bs3, _vec(w["g_sgu_v"]), _vec(w["b_sgu_v"]), "sgu_fwd" + tag,
                       _take(hosts, "sgu_fwd"))
    s["cs"], s["cv"] = _conv_fwd(proj, w["w_dw"], _vec(w["b_dw"]), _vec(w["g_conv_ln"]), _vec(w["b_conv_ln"]),
                                 "conv_fwd" + tag, _take(hosts, "conv_fwd"))
    s["ps"] = _pool_fwd(proj, w["w_pool"], _vec(w["s_pool"]), "pool_fwd" + tag, _take(hosts, "pool_fwd"))
    s["bra"], s["brb"], s["brc"], s["merged"] = _merge_fwd(
        proj, s["sg"], s["cs"], s["ps"], w["w_sgu_out"], w["w_conv_out"], w["w_pool_out"], "merge_fwd" + tag,
        _take(hosts, "merge_fwd"))
    s["mo"], h1 = _mm_norm_res(s["merged"], w["w_out"], _vec(w["g_mix_post"]), h, "mix_out" + tag,
                               _take(hosts, "mix_out"))
    s["h1"] = h1
    s["fg"], s["fu"], s["act"], s["hn2"] = _ffn_in(h1, _vec(w["g_ffn_pre"]), w["w_ffn_in"], "ffn_in" + tag,
                                                   _take(hosts, "ffn_in"))
    s["f"], h2 = _mm_norm_res(s["act"], w["w_ffn_out"], _vec(w["g_ffn_post"]), h1, "ffn_out" + tag,
                              _take(hosts, "ffn_out"))
    s["h2"] = h2
    h3, s["q"], s["e"] = _ple_fwd(h2, p, w["w_ple_gate"], w["w_ple"], "ple_fwd" + tag, _take(hosts, "ple_fwd"))
    return h3, s


def _layer_bwd(dh3, p, w, s, li, hosts=None, big=None):
    tag = "_l%d" % li
    d = D_MODEL
    gs = {}
    big = {} if big is None else big
    dq, de, dh2 = _ple_bwd(dh3, s["q"], s["e"], w["w_ple_gate"], "ple_bwd" + tag)
    dw_ple = _mm_tn(p, de, "dw_ple" + tag)
    dw_ple_gate = _mm_tn(s["h2"], dq, "dw_ple_gate" + tag)
    df, dff, gs["g_ffn_post"] = _ffn_out_bwd(dh2, s["f"], _vec(w["g_ffn_post"]), s["fg"], s["fu"], w["w_ffn_out"],
                                             "ffn_out_bwd" + tag, _take(hosts, "ffn_out_bwd"))
    dw_ffn_out = _mm_tn(s["act"], df, "dw_ffn_out" + tag)
    big["ffn_out"] = dw_ffn_out.reshape(N_CHIPS, 1, D_FF // N_CHIPS, d)
    n_ff = w["w_ffn_in"].shape[2]
    dh1, gs["g_ffn_pre"] = _in_bwd([(dff, 2 * D_FF // n_ff)], w["w_ffn_in"], n_ff, s["h1"], _vec(w["g_ffn_pre"]),
                                   dh2, ROW_TILE, "ffn_in_bwd" + tag, _take(hosts, "ffn_in_bwd"))
    dw_ffn_in = _mm_tn(s["hn2"], dff, "dw_ffn_in" + tag, bn=n_ff)
    big["ffn_in"] = dw_ffn_in.reshape(d, N_CHIPS, n_ff).transpose(1, 0, 2)[:, None]
    (dmo, dbra, dbrb, dbrc, dzg, dsg, dcs, dps, gs["g_mix_post"]) = _merge_bwd(
        dh1, s["mo"], _vec(w["g_mix_post"]), s["proj"], s["bra"], s["brb"], s["brc"], w["w_out"], w["w_sgu_out"],
        w["w_conv_out"], w["w_pool_out"], "merge_bwd" + tag, _take(hosts, "merge_bwd"))
    sq = {"w_ple_gate": dw_ple_gate, "w_out": _mm_tn(s["merged"], dmo, "dw_out" + tag),
          "w_sgu_out": _mm_tn(s["sg"], dbra, "dw_sgu_out" + tag),
          "w_conv_out": _mm_tn(s["cs"], dbrb, "dw_conv_out" + tag),
          "w_pool_out": _mm_tn(s["ps"], dbrc, "dw_pool_out" + tag)}
    big["sq"] = jnp.stack([sq[nm].reshape(N_CHIPS, d // N_CHIPS, d) for nm in SQ], axis=1)
    bs3 = w["b_sgu_s"].reshape(SGU_HEADS, SGU_BLOCK, 1)
    dz_sgu, gs["w_sgu_s"], dbs3, gs["g_sgu_v"], gs["b_sgu_v"] = _sgu_bwd(
        s["proj"], dsg, w["w_sgu_s"], bs3, _vec(w["g_sgu_v"]), _vec(w["b_sgu_v"]), "sgu_bwd" + tag,
        _take(hosts, "sgu_bwd"))
    gs["b_sgu_s"] = dbs3
    dcv, dwdw, gs["b_dw"], gs["g_conv_ln"], gs["b_conv_ln"] = _conv_bwd_norm(
        s["proj"], dcs, s["cv"], _vec(w["b_dw"]), _vec(w["g_conv_ln"]), _vec(w["b_conv_ln"]), "conv_bwd_norm" + tag,
        _take(hosts, "conv_bwd_norm"))
    dz_conv = _conv_bwd_taps(s["proj"], dcv, w["w_dw"], "conv_bwd_taps" + tag, _take(hosts, "conv_bwd_taps"))
    dz_pool, dwpool, gs["s_pool"] = _pool_bwd(s["proj"], dps, w["w_pool"], _vec(w["s_pool"]), "pool_bwd" + tag)
    pieces = [(dz_sgu, 2), (dz_conv, 2), (dz_pool, 1), (dzg, 3)]
    dh0, gs["g_mix_pre"] = _in_bwd(pieces, w["w_in"], d, s["h0"], _vec(w["g_mix_pre"]), dh1, ROW_TILE_HEAVY,
                                   "mix_in_bwd" + tag, _take(hosts, "mix_in_bwd"))
    dw_in = jnp.concatenate([_mm_tn(s["hn"], dz, "dw_in%d" % k + tag) for k, (dz, _) in enumerate(pieces)], axis=1)
    big["in"] = dw_in.reshape(d, N_CHIPS, 2 * d).transpose(1, 0, 2)[:, None]
    gple = dw_ple.reshape(256, N_CHIPS, 256).transpose(1, 0, 2)
    gpool = dwpool.reshape(4, N_CHIPS, 64, 256).transpose(1, 0, 2, 3).reshape(N_CHIPS, 256, 256)
    big["mix"] = jnp.stack([gple, gpool], axis=1)
    big["dw"] = dwdw.reshape(CONV_HALO, N_CHIPS, 256).transpose(1, 0, 2)[:, None]
    return dh0, big, gs


GROUPS = ("in", "sq", "ffn_in", "ffn_out", "mix", "dw")
WIRE_DTYPE = {"in": BF16, "sq": BF16, "ffn_in": BF16, "ffn_out": BF16, "mix": BF16, "dw": F32}
GATHER_FIRST = ("in", "mix", "dw")
GATHER_RIDES = (("mix_in", "sgu_fwd", ("sq", "ffn_in"), ()),
                ("conv_fwd", "pool_fwd", ("ffn_out",), ("in",)),
                ("merge_fwd", "mix_out", (), ("sq",)),
                ("ffn_in", "ffn_out", (), ("ffn_in", "ffn_out", "mix", "dw")))
REDUCE_UPPER = ("ffn_out_bwd", (("ffn_in_bwd", ("in", "ffn_out")), ("merge_bwd", ("sq", "ffn_in", "mix", "dw"))))
REDUCE_OWN = ("sgu_bwd", (("conv_bwd_norm", ("ffn_in", "ffn_out")), ("conv_bwd_taps", ("sq",))))
REDUCE_LAST = ("in", "mix", "dw")


def _group_members(wts):
    n_layers = wts["w_in"].shape[0]
    dw = wts["w_dw"].reshape(n_layers, CONV_WIDTH, -1)
    return {"in": [wts["w_in"]], "sq": [wts[nm] for nm in SQ], "ffn_in": [wts["w_ffn_in"]],
            "ffn_out": [wts["w_ffn_out"]],
            "mix": [wts["w_ple"], wts["w_pool"].reshape(n_layers, POOL_GROUP, POOL_GROUP)],
            "dw": [jnp.pad(dw, ((0, 0), (0, CONV_HALO - CONV_WIDTH), (0, 0)))]}


class _Gather:
    PLACED, OVER_ICI, FULL = 0, 1, 2

    def __init__(self):
        self.buf, self.stage, self.pending = {}, {}, []

    def put(self, key, buf):
        self.buf[key], self.stage[key] = buf, self.PLACED

    def _flush(self):
        for keys, pay, stage in self.pending:
            if pay.results is not None:
                for key, res in zip(keys, pay.results):
                    self.buf[key], self.stage[key] = res, stage
        self.pending = [entry for entry in self.pending if entry[1].results is None]

    def _factory(self, make, keys, before, after):
        def factory():
            if not keys:
                return None
            self._flush()
            assert all(self.stage[k] == before for k in keys), (keys, self.stage)
            pay = make([self.buf[k] for k in keys])
            self.pending.append((keys, pay, after))
            return pay
        return factory

    def ici(self, keys):
        return self._factory(_gather_ici, keys, self.PLACED, self.OVER_ICI)

    def d2d(self, keys):
        return self._factory(_gather_d2d, keys, self.OVER_ICI, self.FULL)

    def get(self, li, group):
        self._flush()
        assert self.stage[(li, group)] == self.FULL, (li, group)
        return self.buf[(li, group)]


class _Reduce:
    def __init__(self, pos, n_layers):
        self.pos, self.n_layers, self.exchanged, self.stages = pos, n_layers, [], []

    def exchange(self, li, groups, grads):
        def factory():
            pay = _pair_exchange([_half_view(grads[g]) for g in groups])
            self.exchanged.append((li, list(groups), pay))
            return pay
        return factory

    def _received(self, li, group):
        for lj, groups, pay in self.exchanged:
            if lj == li and group in groups:
                return pay.results[groups.index(group)]
        raise KeyError((li, group))

    def chips(self, li, groups, grads):
        def factory():
            parts = [_pair_sum(_half_view(grads[g]), self._received(li, g), self.pos, WIRE_DTYPE[g],
                               "pair_sum_%s_l%d" % (g, li)) for g in groups]
            pay = _chip_exchange(parts)
            self.stages.append((li, groups, parts, pay))
            return pay
        return factory

    def finish(self):
        reduced = {}
        for li, groups, parts, pay in self.stages:
            for g, part, landed in zip(groups, parts, pay.results):
                reduced[g] = _chip_sum(part, landed, reduced.get(g), li, self.n_layers, self.pos,
                                       "chip_sum_%s_l%d" % (g, li))
        return reduced


def _pack_small(tree):
    flat = jnp.concatenate([tree[nm].reshape(-1).astype(F32) for nm in SMALL])
    return flat.reshape(-1, 128)


def _unpack_small(packed, like):
    out, off = {}, 0
    flat = packed.reshape(-1)
    for nm in SMALL:
        n = like[nm].size
        out[nm] = flat[off:off + n].reshape(like[nm].shape)
        off += n
    return out


WEIGHTS = ("g_mix_pre", "w_in", "w_sgu_s", "b_sgu_s", "g_sgu_v", "b_sgu_v", "w_sgu_out", "w_dw", "b_dw", "g_conv_ln",
           "b_conv_ln", "w_conv_out", "w_pool", "s_pool", "w_pool_out", "w_out", "g_mix_post", "g_ffn_pre",
           "w_ffn_in", "w_ffn_out", "g_ffn_post", "w_ple", "w_ple_gate")


def kernel(x, p, g_mix_pre, w_in, w_sgu_s, b_sgu_s, g_sgu_v, b_sgu_v, w_sgu_out, w_dw, b_dw, g_conv_ln, b_conv_ln, w_conv_out, w_pool, s_pool, w_pool_out, w_out, g_mix_post, g_ffn_pre, w_ffn_in, w_ffn_out, g_ffn_post, w_ple, w_ple_gate, loss_target, m_g_mix_pre, m_w_in, m_w_sgu_s, m_b_sgu_s, m_g_sgu_v, m_b_sgu_v, m_w_sgu_out, m_w_dw, m_b_dw, m_g_conv_ln, m_b_conv_ln, m_w_conv_out, m_w_pool, m_s_pool, m_w_pool_out, m_w_out, m_g_mix_post, m_g_ffn_pre, m_w_ffn_in, m_w_ffn_out, m_g_ffn_post, m_w_ple, m_w_ple_gate, v_g_mix_pre, v_w_in, v_w_sgu_s, v_b_sgu_s, v_g_sgu_v, v_b_sgu_v, v_w_sgu_out, v_w_dw, v_b_dw, v_g_conv_ln, v_b_conv_ln, v_w_conv_out, v_w_pool, v_s_pool, v_w_pool_out, v_w_out, v_g_mix_post, v_g_ffn_pre, v_w_ffn_in, v_w_ffn_out, v_g_ffn_post, v_w_ple, v_w_ple_gate):
    args = dict(locals())
    wts = {nm: args[nm] for nm in WEIGHTS}
    mom = {nm: args["m_" + nm] for nm in WEIGHTS}
    var = {nm: args["v_" + nm] for nm in WEIGHTS}
    n_layers = w_in.shape[0]
    h = x.reshape(x.shape[1:])
    target = loss_target.reshape(loss_target.shape[1:])
    cx, cy, core = lax.axis_index("x"), lax.axis_index("y"), lax.axis_index("c")
    pos = jnp.stack([2 * cx + cy, 2 * (1 - cx) + cy, 2 * cx + (1 - cy), 2 * (1 - cx) + (1 - cy), core])
    pos = pos.astype(jnp.int32)

    members = _group_members(wts)
    gather = _Gather()
    for li in range(n_layers):
        for g in GROUPS:
            gather.put((li, g), _place(members[g], li, pos, WIRE_DTYPE[g], "place_%s_l%d" % (g, li)))
    first = [(0, g) for g in GATHER_FIRST]
    _run_payload(gather.ici(first)(), "gather_ici_first")
    _run_payload(gather.d2d(first)(), "gather_d2d_first")

    saved, layer_w = [], []
    for li in range(n_layers):
        hosts = {}
        for ici_host, d2d_host, own, nxt in GATHER_RIDES:
            keys = [(li, g) for g in own if li == 0] + [(li + 1, g) for g in nxt if li + 1 < n_layers]
            hosts[ici_host], hosts[d2d_host] = gather.ici(keys), gather.d2d(keys)
        w = _LayerWeights(functools.partial(gather.get, li), wts, li)
        layer_w.append(w)
        h, s = _layer_fwd(h, p[li, 0], w, li, hosts)
        saved.append(s)
    dh, sq_err = _loss_head(h, target, "loss_head")
    loss = lax.psum(sq_err[0, 0] * (0.5 / D_MODEL), ("x", "y", "c"))

    reduce = _Reduce(pos, n_layers)
    small_grads = [None] * n_layers
    upper = None
    for li in reversed(range(n_layers)):
        own = {}
        hosts = {}
        plans = [(REDUCE_UPPER, li + 1, upper)] if upper is not None else []
        if li == 0:
            plans.append((REDUCE_OWN, 0, own))
        for (pair_host, chip_hosts), lj, grads in plans:
            groups = [g for _, gs_ in chip_hosts for g in gs_]
            hosts[pair_host] = reduce.exchange(lj, groups, grads)
            for chip_host, gs_ in chip_hosts:
                hosts[chip_host] = reduce.chips(lj, gs_, grads)
        dh, upper, small_grads[li] = _layer_bwd(dh, p[li, 0], layer_w[li], saved[li], li, hosts, own)
    grad_x = dh[None]
    last = list(REDUCE_LAST) if n_layers > 0 else []
    _run_payload(reduce.exchange(0, last, upper)(), "pair_exchange_last")
    _run_payload(reduce.chips(0, last, upper)(), "chip_exchange_last")
    reduced = reduce.finish()

    shared = _run_payload(_pair_share([reduced[g] for g in GROUPS]), "pair_share")
    red = {g: b.reshape(b.shape[:2] + (-1, b.shape[-1])) for g, b in zip(GROUPS, shared)}

    where = {"w_in": ("in", 0), "w_ffn_in": ("ffn_in", 0), "w_ffn_out": ("ffn_out", 0), "w_ple": ("mix", 0),
             "w_pool": ("mix", 1)}
    for slot, nm in enumerate(SQ):
        where[nm] = ("sq", slot)
    outs = {}
    for nm, (g, slot) in where.items():
        shape = wts[nm].shape
        to3 = lambda a: a.reshape((n_layers,) + red[g].shape[2:])
        res = _adamw(to3(wts[nm]), red[g], slot, to3(mom[nm]), to3(var[nm]), "adamw_" + nm)
        outs[nm] = [r.reshape(shape) for r in res]
    gdw = red["dw"][:, :, :CONV_WIDTH]
    to3 = lambda a: a.reshape(n_layers, CONV_WIDTH, -1)
    res = _adamw(to3(wts["w_dw"]), gdw, 0, to3(mom["w_dw"]), to3(var["w_dw"]), "adamw_w_dw")
    outs["w_dw"] = [r.reshape(wts["w_dw"].shape) for r in res]

    small_tree = {nm: jnp.stack([small_grads[li][nm].reshape(wts[nm].shape[1:]) for li in range(n_layers)], axis=0)
                  for nm in SMALL}
    gsmall = _all_reduce_small(_pack_small(small_tree), "all_reduce_small")
    pk = lambda tree: _pack_small({nm: tree[nm] for nm in SMALL})[None]
    res = _adamw(pk(wts), gsmall[None, None], 0, pk(mom), pk(var), "adamw_small")
    unpacked = [_unpack_small(r[0], wts) for r in res]
    for nm in SMALL:
        outs[nm] = [u[nm] for u in unpacked]

    result = [loss, grad_x]
    for k in range(4):
        result += [outs[nm][k] for nm in WEIGHTS]
    return tuple(result)
```

```python
import functools

import jax
import jax.numpy as jnp
from jax import lax
from jax.experimental import pallas as pl
from jax.experimental.pallas import tpu as pltpu

F32 = jnp.float32
BF16 = jnp.bfloat16
MESH = pl.DeviceIdType.MESH

EPS = 1e-6
D_MODEL = 1024
SGU_BLOCK = 128
SGU_HEADS = 8
CHUNK = 64
CONV_WIDTH = 31
CONV_HALO = 32
POOL_WINDOWS = (2, 4, 8, 16)
POOL_HALO = 16
POOL_GROUP = 256
D_FF = 2816
N_CHIPS = 4

ADAM_LR = 0.001
ADAM_B1 = 0.9
ADAM_B2 = 0.999
ADAM_EPS = 1e-08
ADAM_WD = 0.01
ADAM_STEP = 10

VMEM_LIMIT = 52 * 1024 * 1024
ROW_TILE = 512
ROW_TILE_HEAVY = 256
CONV_ROWS = 64
CONV_LANES = 128
EW_BLOCK_BYTES = 2 * 1024 * 1024


def _params(n_grid):
    return pltpu.CompilerParams(dimension_semantics=("arbitrary",) * n_grid, vmem_limit_bytes=VMEM_LIMIT)


def _dot(a, b):
    return jnp.dot(a.astype(BF16), b.astype(BF16), preferred_element_type=F32)


def _dot_nt(a, b):
    return lax.dot_general(a.astype(BF16), b.astype(BF16), (((1,), (1,)), ((), ())), preferred_element_type=F32)


def _dot_tn(a, b):
    return lax.dot_general(a.astype(BF16), b.astype(BF16), (((0,), (0,)), ((), ())), preferred_element_type=F32)


def _sigmoid(x):
    return 0.5 * jnp.tanh(0.5 * x) + 0.5


_GELU_C = 0.7978845608028654
_GELU_A = 0.044715


def _gelu(x):
    t = jnp.tanh(_GELU_C * (x + _GELU_A * x * x * x))
    return 0.5 * x * (1.0 + t)


def _gelu_and_grad(x):
    x2 = x * x
    t = jnp.tanh(_GELU_C * (x + _GELU_A * x2 * x))
    g = 0.5 * (1.0 + t) + 0.5 * x * (1.0 - t * t) * (_GELU_C * (1.0 + 3.0 * _GELU_A * x2))
    return 0.5 * x * (1.0 + t), g


def _rms_stats(x):
    r = lax.rsqrt(jnp.mean(x * x, axis=-1, keepdims=True) + EPS)
    return x * r, r


def _rms_bwd(xn, r, g, dy):
    gd = dy * g
    return r * (gd - xn * jnp.mean(gd * xn, axis=-1, keepdims=True)), dy * xn


def _ln_stats(x):
    mu = jnp.mean(x, axis=-1, keepdims=True)
    xc = x - mu
    rstd = lax.rsqrt(jnp.mean(xc * xc, axis=-1, keepdims=True) + EPS)
    return xc * rstd, rstd


def _ln_bwd(xhat, rstd, g, dy):
    dxh = dy * g
    return rstd * (dxh - jnp.mean(dxh, axis=-1, keepdims=True) - xhat * jnp.mean(dxh * xhat, axis=-1, keepdims=True))


def _rowsum(x):
    return jnp.sum(x, axis=0, keepdims=True)


def _tile(t, want):
    return min(t, want)


def _full(shape):
    n = len(shape)
    return pl.BlockSpec(shape, lambda *_: (0,) * n)


def _resident(shape):
    n = len(shape)
    return pl.BlockSpec(shape, lambda *_: (0,) * n, pipeline_mode=pl.Buffered(1))


class _Payload:
    def __init__(self, operands, out_shape, aliases, scratch, start, finish):
        self.operands, self.out_shape, self.aliases, self.scratch = list(operands), list(out_shape), aliases, scratch
        self.start, self.finish = start, finish
        self.results = None


def _pcall(body, *, name, grid, in_specs, out_specs, out_shape, operands, scratch_shapes=(), comm=None):
    single = not isinstance(out_shape, (list, tuple))
    out_specs = [out_specs] if single else list(out_specs)
    out_shape = [out_shape] if single else list(out_shape)
    if comm is None:
        res = pl.pallas_call(
            body, name=name, grid=grid, in_specs=list(in_specs), out_specs=out_specs, out_shape=out_shape,
            scratch_shapes=list(scratch_shapes), compiler_params=_params(len(grid)))(*operands)
        return res[0] if single else res
    n_in, n_out, n_scr = len(in_specs), len(out_shape), len(scratch_shapes)
    ci, co = len(comm.operands), len(comm.out_shape)

    def hosted(*refs):
        bounds = [0, n_in, n_in + ci, n_in + ci + n_out, n_in + ci + n_out + co, n_in + ci + n_out + co + n_scr]
        a, b, c_, d_, s_ = [refs[lo:hi] for lo, hi in zip(bounds[:-1], bounds[1:])]
        t_ = refs[bounds[-1]:]
        ids = [pl.program_id(q) for q in range(len(grid))]
        first = functools.reduce(jnp.logical_and, [i == 0 for i in ids])
        last = functools.reduce(jnp.logical_and, [i == pl.num_programs(q) - 1 for q, i in enumerate(ids)])

        @pl.when(first)
        def _():
            comm.start(b, d_, t_)

        body(*a, *c_, *s_)

        @pl.when(last)
        def _():
            comm.finish(b, d_, t_)

    res = pl.pallas_call(
        hosted, name=name, grid=grid, in_specs=list(in_specs) + [ANY] * ci, out_specs=out_specs + [ANY] * co,
        out_shape=out_shape + comm.out_shape, scratch_shapes=list(scratch_shapes) + list(comm.scratch),
        input_output_aliases={n_in + i: n_out + o for i, o in comm.aliases.items()},
        compiler_params=pltpu.CompilerParams(dimension_semantics=("arbitrary",) * len(grid),
                                             vmem_limit_bytes=VMEM_LIMIT, has_side_effects=True),
    )(*operands, *comm.operands)
    comm.results = list(res[n_out:])
    res = res[:n_out]
    return res[0] if single else res


def _run_payload(comm, name):
    ci, co = len(comm.operands), len(comm.out_shape)

    def body(*refs):
        b, d_, t_ = refs[:ci], refs[ci:ci + co], refs[ci + co:]
        comm.start(b, d_, t_)
        comm.finish(b, d_, t_)

    res = pl.pallas_call(
        body, name=name, in_specs=[ANY] * ci, out_specs=[ANY] * co, out_shape=comm.out_shape,
        input_output_aliases=dict(comm.aliases), scratch_shapes=list(comm.scratch),
        compiler_params=pltpu.CompilerParams(has_side_effects=True))(*comm.operands)
    comm.results = list(res)
    return comm.results


def _take(hosts, key):
    return hosts[key]() if hosts and key in hosts else None


def _norm_mm(h, g, w4, name, comm=None):
    t, d = h.shape
    n = w4.shape[2]
    tm = _tile(t, ROW_TILE)

    step = _lane_block(n, 1024)

    def body(h_ref, g_ref, w_ref, o_ref, hn_ref):
        xn, _ = _rms_stats(h_ref[...])
        hn = (xn * g_ref[...]).astype(BF16)
        hn_ref[...] = hn
        for j in range(N_CHIPS):
            for c0 in range(0, n, step):
                o_ref[:, j * n + c0:j * n + c0 + step] = jnp.dot(
                    hn, w_ref[j, :, c0:c0 + step], preferred_element_type=F32).astype(BF16)

    return _pcall(
        body, name=name, grid=(t // tm,),
        in_specs=[pl.BlockSpec((tm, d), lambda i: (i, 0)), _full((1, d)), _resident(w4.shape)],
        out_specs=[pl.BlockSpec((tm, N_CHIPS * n), lambda i: (i, 0)), pl.BlockSpec((tm, d), lambda i: (i, 0))],
        out_shape=[jax.ShapeDtypeStruct((t, N_CHIPS * n), BF16), jax.ShapeDtypeStruct((t, d), BF16)],
        operands=(h, g, w4), comm=comm)


def _sgu_mask():
    ii = lax.broadcasted_iota(jnp.int32, (SGU_BLOCK, SGU_BLOCK), 0) // CHUNK
    jj = lax.broadcasted_iota(jnp.int32, (SGU_BLOCK, SGU_BLOCK), 1) // CHUNK
    return jj <= ii


def _sgu_fwd(proj, wm, bs3, gv, bv, name, comm=None):
    t = proj.shape[0]
    d = D_MODEL
    tm = _tile(t, ROW_TILE)
    hd = d // SGU_HEADS

    def body(zu_ref, zv_ref, wm_ref, bs_ref, gv_ref, bv_ref, o_ref):
        mask = _sgu_mask()
        for blk in range(tm // SGU_BLOCK):
            rows = pl.ds(blk * SGU_BLOCK, SGU_BLOCK)
            u = _gelu(zu_ref[rows, :].astype(F32))
            xhat, _ = _ln_stats(_gelu(zv_ref[rows, :].astype(F32)))
            vn = (xhat * gv_ref[...] + bv_ref[...]).astype(BF16)
            for hh in range(SGU_HEADS):
                cols = slice(hh * hd, (hh + 1) * hd)
                wmh = jnp.where(mask, wm_ref[hh], 0.0).astype(BF16)
                mixed = jnp.dot(wmh, vn[:, cols], preferred_element_type=F32) + bs_ref[hh]
                o_ref[rows, cols] = (u[:, cols] * mixed).astype(BF16)

    return _pcall(
        body, name=name, grid=(t // tm,),
        in_specs=[pl.BlockSpec((tm, d), lambda i: (i, 0)), pl.BlockSpec((tm, d), lambda i: (i, 1)),
                  _full(wm.shape), _full(bs3.shape), _full(gv.shape), _full(bv.shape)],
        out_specs=pl.BlockSpec((tm, d), lambda i: (i, 0)),
        out_shape=jax.ShapeDtypeStruct((t, d), BF16),
        operands=(proj, proj, wm, bs3, gv, bv), comm=comm)


def _conv_taps(scr_ref, r0, c0, base, weight):
    n = CONV_ROWS + CONV_HALO
    win = scr_ref[pl.ds(r0, n), pl.ds(c0, CONV_LANES)]
    acc = None
    for r in range(8):
        rolled = win if r == 0 else pltpu.roll(win, n - r, 0)
        for q in range((CONV_HALO + 7) // 8 + 1):
            k = 8 * q + r - base
            if 0 <= k < CONV_WIDTH and 8 * q + CONV_ROWS <= n:
                term = weight(k) * rolled[8 * q:8 * q + CONV_ROWS]
                acc = term if acc is None else acc + term
    return acc


def _glu_rows(a_ref, g_ref):
    return a_ref[...].astype(F32) * _sigmoid(g_ref[...].astype(F32))


def _conv_into(scr_ref, cv_ref, w_ref, tm, base, flip):
    def chunk(ci, carry):
        r0 = pl.multiple_of(ci * CONV_ROWS, CONV_ROWS)
        for c0 in range(0, D_MODEL, CONV_LANES):
            def weight(k, c0=c0):
                kk = CONV_WIDTH - 1 - k if flip else k
                return w_ref[kk:kk + 1, c0:c0 + CONV_LANES]
            cv_ref[pl.ds(r0, CONV_ROWS), pl.ds(c0, CONV_LANES)] = _conv_taps(scr_ref, r0, c0, base, weight)
        return carry

    lax.fori_loop(0, tm // CONV_ROWS, chunk, 0)


def _conv_specs(t, tm, d):
    hb = tm // CONV_HALO
    main = [pl.BlockSpec((tm, d), lambda i: (i, 2)), pl.BlockSpec((tm, d), lambda i: (i, 3))]
    halo = [pl.BlockSpec((CONV_HALO, d), lambda i: (jnp.maximum(i * hb - 1, 0), 2)),
            pl.BlockSpec((CONV_HALO, d), lambda i: (jnp.maximum(i * hb - 1, 0), 3))]
    return main, halo


def _fill_glu_history(scr_ref, a_ref, g_ref, ah_ref, gh_ref, tm):
    hist = _glu_rows(ah_ref, gh_ref)
    scr_ref[0:CONV_HALO, :] = jnp.where(pl.program_id(0) > 0, hist, 0.0)
    scr_ref[CONV_HALO:CONV_HALO + tm, :] = _glu_rows(a_ref, g_ref)


_CONV_BASE = CONV_HALO - (CONV_WIDTH - 1)


def _conv_fwd(proj, wdw, bdw, gln, bln, name, comm=None):
    t = proj.shape[0]
    d = D_MODEL
    tm = _tile(t, ROW_TILE)
    main, halo = _conv_specs(t, tm, d)

    def body(a_ref, g_ref, ah_ref, gh_ref, w_ref, b_ref, gl_ref, bl_ref, o_ref, cv_ref, scr_ref):
        _fill_glu_history(scr_ref, a_ref, g_ref, ah_ref, gh_ref, tm)
        _conv_into(scr_ref, cv_ref, w_ref, tm, _CONV_BASE, False)
        xhat, _ = _ln_stats(cv_ref[...] + b_ref[...])
        cn = xhat * gl_ref[...] + bl_ref[...]
        o_ref[...] = (cn * _sigmoid(cn)).astype(BF16)

    row = pl.BlockSpec((tm, d), lambda i: (i, 0))
    return _pcall(
        body, name=name, grid=(t // tm,),
        in_specs=main + halo + [_full(wdw.shape), _full(bdw.shape), _full(gln.shape), _full(bln.shape)],
        out_specs=[row, row],
        out_shape=[jax.ShapeDtypeStruct((t, d), BF16), jax.ShapeDtypeStruct((t, d), F32)],
        scratch_shapes=[pltpu.VMEM((tm + CONV_HALO, d), F32)],
        operands=(proj, proj, proj, proj, wdw, bdw, gln, bln), comm=comm)


def _pool_fill(scr_ref, z_ref, zh_ref, tm):
    scr_ref[0:POOL_HALO, :] = jnp.where(pl.program_id(0) > 0, zh_ref[...].astype(F32), 0.0)
    scr_ref[POOL_HALO:POOL_HALO + tm, :] = z_ref[...].astype(F32)


def _pool_count(t0, rows, w):
    pos = (t0 + lax.broadcasted_iota(jnp.int32, (rows, 1), 0) + 1).astype(F32)
    return jnp.minimum(pos, float(w))


def _pooled_group(scr_ref, gi, w, tm, t0):
    cols = pl.ds(gi * POOL_GROUP, POOL_GROUP)
    acc = scr_ref[pl.ds(POOL_HALO, tm), cols]
    z = acc
    for k in range(1, w):
        acc = acc + scr_ref[pl.ds(POOL_HALO - k, tm), cols]
    return acc / _pool_count(t0, tm, w) - z


def _pool_specs(tm, d):
    hb = tm // POOL_HALO
    return [pl.BlockSpec((tm, d), lambda i: (i, 4)),
            pl.BlockSpec((POOL_HALO, d), lambda i: (jnp.maximum(i * hb - 1, 0), 4))]


def _pool_fwd(proj, wpool, spool, name, comm=None):
    t = proj.shape[0]
    d = D_MODEL
    tm = _tile(t, ROW_TILE)

    def body(z_ref, zh_ref, w_ref, s_ref, o_ref, scr_ref):
        _pool_fill(scr_ref, z_ref, zh_ref, tm)
        t0 = pl.program_id(0) * tm
        for gi, w in enumerate(POOL_WINDOWS):
            cols = slice(gi * POOL_GROUP, (gi + 1) * POOL_GROUP)
            pooled = _pooled_group(scr_ref, gi, w, tm, t0)
            o_ref[:, cols] = (_dot(pooled, w_ref[gi]) * s_ref[:, cols]).astype(BF16)

    return _pcall(
        body, name=name, grid=(t // tm,),
        in_specs=_pool_specs(tm, d) + [_full(wpool.shape), _full(spool.shape)],
        out_specs=pl.BlockSpec((tm, d), lambda i: (i, 0)),
        out_shape=jax.ShapeDtypeStruct((t, d), BF16),
        scratch_shapes=[pltpu.VMEM((tm + POOL_HALO, d), F32)],
        operands=(proj, proj, wpool, spool), comm=comm)


def _merge_fwd(proj, sg, cs, ps, wa, wb, wc, name, comm=None):
    t = proj.shape[0]
    d = D_MODEL
    tm = _tile(t, ROW_TILE_HEAVY)

    def body(za_ref, zb_ref, zc_ref, sg_ref, cs_ref, ps_ref, wa_ref, wb_ref, wc_ref, ba_ref, bb_ref, bc_ref, m_ref):
        merged = None
        for z_ref, x_ref, w_ref, b_ref in ((za_ref, sg_ref, wa_ref, ba_ref), (zb_ref, cs_ref, wb_ref, bb_ref),
                                           (zc_ref, ps_ref, wc_ref, bc_ref)):
            br = jnp.dot(x_ref[...], w_ref[...], preferred_element_type=F32)
            b_ref[...] = br.astype(BF16)
            term = _sigmoid(z_ref[...].astype(F32)) * br
            merged = term if merged is None else merged + term
        m_ref[...] = merged.astype(BF16)

    row = pl.BlockSpec((tm, d), lambda i: (i, 0))
    wspec = _resident((d, d))
    return _pcall(
        body, name=name, grid=(t // tm,),
        in_specs=[pl.BlockSpec((tm, d), lambda i: (i, 5)), pl.BlockSpec((tm, d), lambda i: (i, 6)),
                  pl.BlockSpec((tm, d), lambda i: (i, 7)), row, row, row, wspec, wspec, wspec],
        out_specs=[row, row, row, row],
        out_shape=[jax.ShapeDtypeStruct((t, d), BF16)] * 4,
        operands=(proj, proj, proj, sg, cs, ps, wa, wb, wc), comm=comm)


def _mm_norm_res(a, w, g, hres, name, comm=None):
    t, k = a.shape
    d = w.shape[1]
    tm = _tile(t, ROW_TILE)

    def body(a_ref, w_ref, g_ref, h_ref, y_ref, o_ref):
        y = jnp.dot(a_ref[...], w_ref[...], preferred_element_type=F32)
        y_ref[...] = y
        yn, _ = _rms_stats(y)
        o_ref[...] = h_ref[...] + yn * g_ref[...]

    row = pl.BlockSpec((tm, d), lambda i: (i, 0))
    return _pcall(
        body, name=name, grid=(t // tm,),
        in_specs=[pl.BlockSpec((tm, k), lambda i: (i, 0)), _resident(w.shape), _full(g.shape), row],
        out_specs=[row, row],
        out_shape=[jax.ShapeDtypeStruct((t, d), F32)] * 2,
        operands=(a, w, g, hres), comm=comm)


def _ffn_in(h, g, w4, name, comm=None):
    t, d = h.shape
    n = w4.shape[2]
    tm = _tile(t, ROW_TILE)
    nj = D_FF // n

    def body(h_ref, g_ref, w_ref, fg_ref, fu_ref, act_ref, hn_ref):
        xn, _ = _rms_stats(h_ref[...])
        hn = (xn * g_ref[...]).astype(BF16)
        hn_ref[...] = hn
        for j in range(nj):
            cols = slice(j * n, (j + 1) * n)
            fg = jnp.dot(hn, w_ref[j], preferred_element_type=F32)
            fu = jnp.dot(hn, w_ref[j + nj], preferred_element_type=F32)
            fg_ref[:, cols] = fg.astype(BF16)
            fu_ref[:, cols] = fu.astype(BF16)
            act_ref[:, cols] = (fg * _sigmoid(fg) * fu).astype(BF16)

    wide = pl.BlockSpec((tm, D_FF), lambda i: (i, 0))
    return _pcall(
        body, name=name, grid=(t // tm,),
        in_specs=[pl.BlockSpec((tm, d), lambda i: (i, 0)), _full((1, d)), _resident(w4.shape)],
        out_specs=[wide, wide, wide, pl.BlockSpec((tm, d), lambda i: (i, 0))],
        out_shape=[jax.ShapeDtypeStruct((t, D_FF), BF16)] * 3 + [jax.ShapeDtypeStruct((t, d), BF16)],
        operands=(h, g, w4), comm=comm)


def _ple_fwd(h, p, wg, wp, name, comm=None):
    t, d = h.shape
    tm = _tile(t, ROW_TILE)

    def body(h_ref, p_ref, wg_ref, wp_ref, o_ref, q_ref, e_ref):
        hh = h_ref[...]
        q = _dot(hh, wg_ref[...])
        e = _dot(p_ref[...], wp_ref[...])
        q_ref[...] = q.astype(BF16)
        e_ref[...] = e.astype(BF16)
        o_ref[...] = hh + _sigmoid(q) * e

    row = pl.BlockSpec((tm, d), lambda i: (i, 0))
    return _pcall(
        body, name=name, grid=(t // tm,),
        in_specs=[row, pl.BlockSpec((tm, p.shape[1]), lambda i: (i, 0)), _resident(wg.shape), _resident(wp.shape)],
        out_specs=[row, row, row],
        out_shape=[jax.ShapeDtypeStruct((t, d), F32), jax.ShapeDtypeStruct((t, d), BF16),
                   jax.ShapeDtypeStruct((t, d), BF16)],
        operands=(h, p, wg, wp), comm=comm)


def _loss_head(y, target, name):
    t, d = y.shape
    tm = _tile(t, ROW_TILE)

    def body(y_ref, t_ref, dy_ref, l_ref):
        @pl.when(pl.program_id(0) == 0)
        def _():
            l_ref[...] = jnp.zeros_like(l_ref)

        err = y_ref[...] - t_ref[...]
        dy_ref[...] = err * (1.0 / d)
        l_ref[...] += jnp.sum(err * err, keepdims=True)[:, :1] * jnp.ones((1, 128), F32)

    row = pl.BlockSpec((tm, d), lambda i: (i, 0))
    return pl.pallas_call(
        body, name=name, grid=(t // tm,),
        in_specs=[row, row], out_specs=[row, _full((1, 128))],
        out_shape=[jax.ShapeDtypeStruct((t, d), F32), jax.ShapeDtypeStruct((1, 128), F32)],
        compiler_params=_params(1))(y, target)


def _ple_bwd(dh, q, e, wg, name):
    t, d = dh.shape
    tm = _tile(t, ROW_TILE)

    def body(dh_ref, q_ref, e_ref, wg_ref, dq_ref, de_ref, o_ref):
        dh_ = dh_ref[...]
        s = _sigmoid(q_ref[...].astype(F32))
        dq = (dh_ * e_ref[...].astype(F32) * s * (1.0 - s)).astype(BF16)
        dq_ref[...] = dq
        de_ref[...] = (dh_ * s).astype(BF16)
        o_ref[...] = dh_ + _dot_nt(dq, wg_ref[...])

    row = pl.BlockSpec((tm, d), lambda i: (i, 0))
    return pl.pallas_call(
        body, name=name, grid=(t // tm,),
        in_specs=[row, row, row, _resident(wg.shape)], out_specs=[row, row, row],
        out_shape=[jax.ShapeDtypeStruct((t, d), BF16), jax.ShapeDtypeStruct((t, d), BF16),
                   jax.ShapeDtypeStruct((t, d), F32)],
        compiler_params=_params(1))(dh, q, e, wg)


def _ffn_out_bwd(dh, f, g, fg, fu, w, name, comm=None):
    t, d = dh.shape
    tm = _tile(t, ROW_TILE_HEAVY)

    def body(dh_ref, f_ref, g_ref, fg_ref, fu_ref, w_ref, df_ref, dff_ref, dg_ref):
        @pl.when(pl.program_id(0) == 0)
        def _():
            dg_ref[...] = jnp.zeros_like(dg_ref)

        fn, r = _rms_stats(f_ref[...])
        df, dgt = _rms_bwd(fn, r, g_ref[...], dh_ref[...])
        dg_ref[...] += _rowsum(dgt)
        df = df.astype(BF16)
        df_ref[...] = df
        dact = _dot_nt(df, w_ref[...])
        fg_ = fg_ref[...].astype(F32)
        s = _sigmoid(fg_)
        dff_ref[:, 0:D_FF] = (dact * fu_ref[...].astype(F32) * (s * (1.0 + fg_ * (1.0 - s)))).astype(BF16)
        dff_ref[:, D_FF:2 * D_FF] = (dact * (fg_ * s)).astype(BF16)

    row = pl.BlockSpec((tm, d), lambda i: (i, 0))
    wide = pl.BlockSpec((tm, D_FF), lambda i: (i, 0))
    return _pcall(
        body, name=name, grid=(t // tm,),
        in_specs=[row, row, _full(g.shape), wide, wide, _resident(w.shape)],
        out_specs=[row, pl.BlockSpec((tm, 2 * D_FF), lambda i: (i, 0)), _full((1, d))],
        out_shape=[jax.ShapeDtypeStruct((t, d), BF16), jax.ShapeDtypeStruct((t, 2 * D_FF), BF16),
                   jax.ShapeDtypeStruct((1, d), F32)],
        operands=(dh, f, g, fg, fu, w), comm=comm)


def _in_bwd(pieces, w4, unit, h, g, dres, tm, name, comm=None):
    t, d = h.shape
    tm = _tile(t, tm)
    per_chunk = w4.shape[2] // unit
    n_p = len(pieces)

    def body(*refs):
        p_refs = refs[:n_p]
        w_ref, h_ref, g_ref, r_ref, o_ref, dg_ref = refs[n_p:]

        @pl.when(pl.program_id(0) == 0)
        def _():
            dg_ref[...] = jnp.zeros_like(dg_ref)

        acc = None
        u = 0
        for p_ref, (_, nu) in zip(p_refs, pieces):
            for k in range(nu):
                lanes = slice((u % per_chunk) * unit, (u % per_chunk + 1) * unit)
                term = _dot_nt(p_ref[:, k * unit:(k + 1) * unit], w_ref[u // per_chunk, :, lanes])
                acc = term if acc is None else acc + term
                u += 1
        xn, r = _rms_stats(h_ref[...])
        dx, dgt = _rms_bwd(xn, r, g_ref[...], acc)
        dg_ref[...] += _rowsum(dgt)
        o_ref[...] = r_ref[...] + dx

    row = pl.BlockSpec((tm, d), lambda i: (i, 0))
    return _pcall(
        body, name=name, grid=(t // tm,),
        in_specs=[pl.BlockSpec((tm, a.shape[1]), lambda i: (i, 0)) for a, _ in pieces]
        + [_resident(w4.shape), row, _full((1, d)), row],
        out_specs=[row, _full((1, d))],
        out_shape=[jax.ShapeDtypeStruct((t, d), F32), jax.ShapeDtypeStruct((1, d), F32)],
        operands=(*[a for a, _ in pieces], w4, h, g, dres), comm=comm)


def _lane_block(n, cap):
    return max(b for b in range(128, min(n, cap) + 1, 128) if n % b == 0)


def _mm_tn(x, dy, name, bn=None):
    t, m = x.shape
    n = dy.shape[1]
    bm = _lane_block(m, 1408)
    bn = bn or _lane_block(n, 1408)
    tk = _tile(t, 1024)

    def body(x_ref, dy_ref, o_ref):
        @pl.when(pl.program_id(2) == 0)
        def _():
            o_ref[...] = jnp.zeros_like(o_ref)

        o_ref[...] += _dot_tn(x_ref[...], dy_ref[...])

    return pl.pallas_call(
        body, name=name, grid=(m // bm, n // bn, t // tk),
        in_specs=[pl.BlockSpec((tk, bm), lambda a, b, k: (k, a)), pl.BlockSpec((tk, bn), lambda a, b, k: (k, b))],
        out_specs=pl.BlockSpec((bm, bn), lambda a, b, k: (a, b)),
        out_shape=jax.ShapeDtypeStruct((m, n), F32),
        compiler_params=_params(3))(x, dy)


def _dw_cols(x, pieces, unit, per_chunk, name):
    t, m = x.shape
    tk = _tile(t, 1024)
    offs, total = [], 0
    for _, nu in pieces:
        offs.append(total)
        total += nu

    def body(x_ref, *refs):
        o_ref = refs[-1]
        u = pl.program_id(0)

        @pl.when(pl.program_id(1) == 0)
        def _():
            o_ref[...] = jnp.zeros_like(o_ref)

        for p_ref, off, (_, nu) in zip(refs[:-1], offs, pieces):
            @pl.when((u >= off) & (u < off + nu))
            def _(p_ref=p_ref):
                o_ref[...] += _dot_tn(x_ref[...], p_ref[...])

    def piece_spec(off, nu):
        def index(u, k):
            mine = (u >= off) & (u < off + nu)
            return jnp.where(mine, k, 0), jnp.clip(u - off, 0, nu - 1)
        return pl.BlockSpec((tk, unit), index)

    return pl.pallas_call(
        body, name=name, grid=(total, t // tk),
        in_specs=[pl.BlockSpec((tk, m), lambda u, k: (k, 0))] + [piece_spec(o, nu) for o, (_, nu) in zip(offs, pieces)],
        out_specs=pl.BlockSpec((None, None, m, unit), lambda u, k: (u // per_chunk, 0, 0, u % per_chunk)),
        out_shape=jax.ShapeDtypeStruct((N_CHIPS, 1, m, per_chunk * unit), F32),
        compiler_params=_params(2))(x, *[a for a, _ in pieces])


def _dw_slot(x, dy, buf, slot, n_slots, name):
    t, m = x.shape
    n = dy.shape[1]
    bn = _lane_block(n, 1024)
    tk = _tile(t, 1024)
    rows = m // N_CHIPS

    def body(x_ref, dy_ref, *rest):
        o_ref = rest[-1]

        @pl.when(pl.program_id(1) == 0)
        def _():
            o_ref[...] = jnp.zeros_like(o_ref)

        o_ref[...] += _dot_tn(x_ref[...], dy_ref[...]).reshape(N_CHIPS, rows, bn)

    in_specs = [pl.BlockSpec((tk, m), lambda b, k: (k, 0)), pl.BlockSpec((tk, bn), lambda b, k: (k, b))]
    operands = [x, dy]
    aliases = {}
    if buf is not None:
        in_specs.append(pl.BlockSpec(memory_space=pl.ANY))
        operands.append(buf)
        aliases = {2: 0}
    return pl.pallas_call(
        body, name=name, grid=(n // bn, t // tk), in_specs=in_specs,
        out_specs=pl.BlockSpec((N_CHIPS, None, rows, bn), lambda b, k: (0, slot, 0, b)),
        out_shape=jax.ShapeDtypeStruct((N_CHIPS, n_slots, rows, n), F32),
        input_output_aliases=aliases, compiler_params=_params(2))(*operands)


def _merge_bwd(dh, mo, g, proj, bra, brb, brc, w_out, wa, wb, wc, name, comm=None):
    t, d = dh.shape
    tm = _tile(t, ROW_TILE_HEAVY)

    def body(dh_ref, mo_ref, g_ref, za_ref, zb_ref, zc_ref, ba_ref, bb_ref, bc_ref, wo_ref, wa_ref, wb_ref, wc_ref,
             dmo_ref, dba_ref, dbb_ref, dbc_ref, dz_ref, dsg_ref, dcs_ref, dps_ref, dg_ref):
        @pl.when(pl.program_id(0) == 0)
        def _():
            dg_ref[...] = jnp.zeros_like(dg_ref)

        mon, r = _rms_stats(mo_ref[...])
        dmo, dgt = _rms_bwd(mon, r, g_ref[...], dh_ref[...])
        dg_ref[...] += _rowsum(dgt)
        dmo = dmo.astype(BF16)
        dmo_ref[...] = dmo
        dmerged = _dot_nt(dmo, wo_ref[...])
        branches = ((za_ref, ba_ref, wa_ref, dba_ref, dsg_ref), (zb_ref, bb_ref, wb_ref, dbb_ref, dcs_ref),
                    (zc_ref, bc_ref, wc_ref, dbc_ref, dps_ref))
        for j, (z_ref, b_ref, w_ref, db_ref, dx_ref) in enumerate(branches):
            gate = _sigmoid(z_ref[...].astype(F32))
            dbr = (dmerged * gate).astype(BF16)
            db_ref[...] = dbr
            dz_ref[:, j * d:(j + 1) * d] = (dmerged * b_ref[...].astype(F32) * gate * (1.0 - gate)).astype(BF16)
            dx_ref[...] = _dot_nt(dbr, w_ref[...]).astype(BF16)

    row = pl.BlockSpec((tm, d), lambda i: (i, 0))
    wspec = _resident((d, d))
    bf = jax.ShapeDtypeStruct((t, d), BF16)
    return _pcall(
        body, name=name, grid=(t // tm,),
        in_specs=[row, row, _full(g.shape), pl.BlockSpec((tm, d), lambda i: (i, 5)),
                  pl.BlockSpec((tm, d), lambda i: (i, 6)), pl.BlockSpec((tm, d), lambda i: (i, 7)),
                  row, row, row, wspec, wspec, wspec, wspec],
        out_specs=[row, row, row, row, pl.BlockSpec((tm, 3 * d), lambda i: (i, 0)), row, row, row, _full((1, d))],
        out_shape=[bf, bf, bf, bf, jax.ShapeDtypeStruct((t, 3 * d), BF16), bf, bf, bf,
                   jax.ShapeDtypeStruct((1, d), F32)],
        operands=(dh, mo, g, proj, proj, proj, bra, brb, brc, w_out, wa, wb, wc), comm=comm)


def _sgu_bwd(proj, dsg, wm, bs3, gv, bv, name, comm=None):
    t = proj.shape[0]
    d = D_MODEL
    tm = _tile(t, ROW_TILE_HEAVY)
    hd = d // SGU_HEADS

    def body(zu_ref, zv_ref, d_ref, wm_ref, bs_ref, gv_ref, bv_ref, dz_ref, dwm_ref, dbs_ref, dgv_ref, dbv_ref,
             dvn_ref):
        @pl.when(pl.program_id(0) == 0)
        def _():
            dwm_ref[...] = jnp.zeros_like(dwm_ref)
            dbs_ref[...] = jnp.zeros_like(dbs_ref)
            dgv_ref[...] = jnp.zeros_like(dgv_ref)
            dbv_ref[...] = jnp.zeros_like(dbv_ref)

        mask = _sgu_mask()
        for blk in range(tm // SGU_BLOCK):
            rows = pl.ds(blk * SGU_BLOCK, SGU_BLOCK)
            u, du_dz = _gelu_and_grad(zu_ref[rows, :].astype(F32))
            v0, dv_dz = _gelu_and_grad(zv_ref[rows, :].astype(F32))
            xhat, rstd = _ln_stats(v0)
            vn = (xhat * gv_ref[...] + bv_ref[...]).astype(BF16)
            dsg = d_ref[rows, :].astype(F32)
            dmix = (dsg * u).astype(BF16)
            for hh in range(SGU_HEADS):
                cols = slice(hh * hd, (hh + 1) * hd)
                wmh = jnp.where(mask, wm_ref[hh], 0.0).astype(BF16)
                vb = vn[:, cols]
                mixed = jnp.dot(wmh, vb, preferred_element_type=F32) + bs_ref[hh]
                dz_ref[rows, cols] = (dsg[:, cols] * mixed * du_dz[:, cols]).astype(BF16)
                dmh = dmix[:, cols]
                dwm_ref[hh] += jnp.where(mask, _dot_nt(dmh, vb), 0.0)
                dbs_ref[hh] += jnp.sum(dmh.astype(F32), axis=1, keepdims=True)
                dvn_ref[:, cols] = _dot_tn(wmh, dmh)
            dvn = dvn_ref[...]
            dgv_ref[...] += _rowsum(dvn * xhat)
            dbv_ref[...] += _rowsum(dvn)
            dz_ref[rows, d:2 * d] = (_ln_bwd(xhat, rstd, gv_ref[...], dvn) * dv_dz).astype(BF16)

    return _pcall(
        body, name=name, grid=(t // tm,),
        in_specs=[pl.BlockSpec((tm, d), lambda i: (i, 0)), pl.BlockSpec((tm, d), lambda i: (i, 1)),
                  pl.BlockSpec((tm, d), lambda i: (i, 0)), _full(wm.shape), _full(bs3.shape), _full(gv.shape),
                  _full(bv.shape)],
        out_specs=[pl.BlockSpec((tm, 2 * d), lambda i: (i, 0)), _full(wm.shape), _full(bs3.shape), _full((1, d)),
                   _full((1, d))],
        out_shape=[jax.ShapeDtypeStruct((t, 2 * d), BF16), jax.ShapeDtypeStruct(wm.shape, F32),
                   jax.ShapeDtypeStruct(bs3.shape, F32), jax.ShapeDtypeStruct((1, d), F32),
                   jax.ShapeDtypeStruct((1, d), F32)],
        scratch_shapes=[pltpu.VMEM((SGU_BLOCK, d), F32)],
        operands=(proj, proj, dsg, wm, bs3, gv, bv), comm=comm)


def _conv_bwd_norm(proj, dcs, cv, bdw, gln, bln, name, comm=None):
    t = proj.shape[0]
    d = D_MODEL
    tm = _tile(t, ROW_TILE)
    main, halo = _conv_specs(t, tm, d)
    n_win = CONV_ROWS + CONV_HALO

    def body(a_ref, g_ref, ah_ref, gh_ref, dcs_ref, cv_ref, b_ref, gl_ref, bl_ref,
             dcv_ref, dw_ref, db_ref, dgl_ref, dbl_ref, scr_ref, dwacc_ref):
        @pl.when(pl.program_id(0) == 0)
        def _():
            dwacc_ref[...] = jnp.zeros_like(dwacc_ref)
            db_ref[...] = jnp.zeros_like(db_ref)
            dgl_ref[...] = jnp.zeros_like(dgl_ref)
            dbl_ref[...] = jnp.zeros_like(dbl_ref)

        _fill_glu_history(scr_ref, a_ref, g_ref, ah_ref, gh_ref, tm)
        xhat, rstd = _ln_stats(cv_ref[...] + b_ref[...])
        cn = xhat * gl_ref[...] + bl_ref[...]
        s = _sigmoid(cn)
        dcn = dcs_ref[...].astype(F32) * (s * (1.0 + cn * (1.0 - s)))
        dgl_ref[...] += _rowsum(dcn * xhat)
        dbl_ref[...] += _rowsum(dcn)
        dcv = _ln_bwd(xhat, rstd, gl_ref[...], dcn)
        db_ref[...] += _rowsum(dcv)
        dcv_ref[...] = dcv

        def chunk(ci, carry):
            r0 = pl.multiple_of(ci * CONV_ROWS, CONV_ROWS)
            for c0 in range(0, d, CONV_LANES):
                lanes = pl.ds(c0, CONV_LANES)
                win = scr_ref[pl.ds(r0, n_win), lanes]
                dchunk = dcv_ref[pl.ds(r0, CONV_ROWS), lanes]
                for r in range(8):
                    rolled = win if r == 0 else pltpu.roll(win, n_win - r, 0)
                    for q in range(n_win // 8):
                        k = 8 * q + r - _CONV_BASE
                        if 0 <= k < CONV_WIDTH and 8 * q + CONV_ROWS <= n_win:
                            prod = dchunk * rolled[8 * q:8 * q + CONV_ROWS]
                            part = prod[0:8]
                            for s8 in range(8, CONV_ROWS, 8):
                                part = part + prod[s8:s8 + 8]
                            dwacc_ref[pl.ds(8 * k, 8), lanes] += part
            return carry

        lax.fori_loop(0, tm // CONV_ROWS, chunk, 0)

        @pl.when(pl.program_id(0) == pl.num_programs(0) - 1)
        def _():
            dw_ref[...] = jnp.sum(dwacc_ref[...].reshape(CONV_HALO, 8, d), axis=1)

    row = pl.BlockSpec((tm, d), lambda i: (i, 0))
    vec = _full((1, d))
    return _pcall(
        body, name=name, grid=(t // tm,),
        in_specs=main + halo + [row, row, vec, vec, vec],
        out_specs=[row, _full((CONV_HALO, d)), vec, vec, vec],
        out_shape=[jax.ShapeDtypeStruct((t, d), F32), jax.ShapeDtypeStruct((CONV_HALO, d), F32)]
        + [jax.ShapeDtypeStruct((1, d), F32)] * 3,
        scratch_shapes=[pltpu.VMEM((tm + CONV_HALO, d), F32), pltpu.VMEM((8 * CONV_HALO, d), F32)],
        operands=(proj, proj, proj, proj, dcs, cv, bdw, gln, bln), comm=comm)


def _conv_bwd_taps(proj, dcv, wdw, name, comm=None):
    t = proj.shape[0]
    d = D_MODEL
    tm = _tile(t, ROW_TILE)
    hb = tm // CONV_HALO
    last_halo = t // CONV_HALO - 1

    def body(a_ref, g_ref, dcv_ref, dnext_ref, w_ref, dz_ref, scr_ref, dh_ref):
        scr_ref[0:tm, :] = dcv_ref[...]
        is_last = pl.program_id(0) == pl.num_programs(0) - 1
        scr_ref[tm:tm + CONV_HALO, :] = jnp.where(is_last, 0.0, dnext_ref[...])
        _conv_into(scr_ref, dh_ref, w_ref, tm, 0, True)
        dglu = dh_ref[...]
        a = a_ref[...].astype(F32)
        s = _sigmoid(g_ref[...].astype(F32))
        dz_ref[:, 0:d] = (dglu * s).astype(BF16)
        dz_ref[:, d:2 * d] = (dglu * a * s * (1.0 - s)).astype(BF16)

    return _pcall(
        body, name=name, grid=(t // tm,),
        in_specs=[pl.BlockSpec((tm, d), lambda i: (i, 2)), pl.BlockSpec((tm, d), lambda i: (i, 3)),
                  pl.BlockSpec((tm, d), lambda i: (i, 0)),
                  pl.BlockSpec((CONV_HALO, d), lambda i: (jnp.minimum((i + 1) * hb, last_halo), 0)),
                  _full(wdw.shape)],
        out_specs=pl.BlockSpec((tm, 2 * d), lambda i: (i, 0)),
        out_shape=jax.ShapeDtypeStruct((t, 2 * d), BF16),
        scratch_shapes=[pltpu.VMEM((tm + CONV_HALO, d), F32), pltpu.VMEM((tm, d), F32)],
        operands=(proj, proj, dcv, dcv, wdw), comm=comm)


def _pool_bwd(proj, dps, wpool, spool, name):
    t = proj.shape[0]
    d = D_MODEL
    tm = _tile(t, ROW_TILE)
    hb = tm // POOL_HALO
    last_halo = t // POOL_HALO - 1
    ext = tm + POOL_HALO

    def body(z_ref, zh_ref, d_ref, dnext_ref, w_ref, s_ref, dz_ref, dw_ref, ds_ref, scr_ref, dext_ref, dq_ref):
        @pl.when(pl.program_id(0) == 0)
        def _():
            dw_ref[...] = jnp.zeros_like(dw_ref)
            ds_ref[...] = jnp.zeros_like(ds_ref)

        _pool_fill(scr_ref, z_ref, zh_ref, tm)
        t0 = pl.program_id(0) * tm
        is_last = pl.program_id(0) == pl.num_programs(0) - 1
        dext_ref[0:tm, :] = d_ref[...].astype(F32)
        dext_ref[tm:ext, :] = jnp.where(is_last, 0.0, dnext_ref[...].astype(F32))
        for gi, w in enumerate(POOL_WINDOWS):
            cols = slice(gi * POOL_GROUP, (gi + 1) * POOL_GROUP)
            dps_ext = dext_ref[:, cols]
            dpm_ext = (dps_ext * s_ref[:, cols]).astype(BF16)
            dpooled_ext = _dot_nt(dpm_ext, w_ref[gi])
            dq_ref[...] = dpooled_ext / _pool_count(t0, ext, w)
            acc = dq_ref[pl.ds(0, tm), :]
            for k in range(1, w):
                acc = acc + dq_ref[pl.ds(k, tm), :]
            dz_ref[:, cols] = (acc - dpooled_ext[0:tm]).astype(BF16)
            pooled = _pooled_group(scr_ref, gi, w, tm, t0).astype(BF16)
            pm = jnp.dot(pooled, w_ref[gi], preferred_element_type=F32)
            ds_ref[:, cols] += _rowsum(dps_ext[0:tm] * pm)
            dw_ref[gi] += _dot_tn(pooled, dpm_ext[0:tm])

    return pl.pallas_call(
        body, name=name, grid=(t // tm,),
        in_specs=_pool_specs(tm, d) + [pl.BlockSpec((tm, d), lambda i: (i, 0)),
                                       pl.BlockSpec((POOL_HALO, d), lambda i: (jnp.minimum((i + 1) * hb, last_halo), 0)),
                                       _full(wpool.shape), _full(spool.shape)],
        out_specs=[pl.BlockSpec((tm, d), lambda i: (i, 0)), _full(wpool.shape), _full((1, d))],
        out_shape=[jax.ShapeDtypeStruct((t, d), BF16), jax.ShapeDtypeStruct(wpool.shape, F32),
                   jax.ShapeDtypeStruct((1, d), F32)],
        scratch_shapes=[pltpu.VMEM((tm + POOL_HALO, d), F32), pltpu.VMEM((ext, d), F32),
                        pltpu.VMEM((ext, POOL_GROUP), F32)],
        compiler_params=_params(1))(proj, proj, dps, dps, wpool, spool)


ANY = pl.BlockSpec(memory_space=pl.ANY)


def _mesh_pos():
    x, y, c = lax.axis_index("x"), lax.axis_index("y"), lax.axis_index("c")
    chips = [(1 - x, y), (x, 1 - y), (1 - x, 1 - y)]
    return x, y, c, chips


def _chip_of(xy):
    return 2 * xy[0] + xy[1]


def _half_view(a):
    return a.reshape(a.shape[:-2] + (2, a.shape[-2] // 2, a.shape[-1]))


def _same(arrs):
    return [jax.ShapeDtypeStruct(a.shape, a.dtype) for a in arrs]


def _in_place(n):
    return {g: g for g in range(n)}


def _sems(count):
    return [pltpu.SemaphoreType.DMA((count,)), pltpu.SemaphoreType.DMA((count,))]


def _gather_ici(bufs):
    n = len(bufs)

    def copy(buf, sems, g, j, chip):
        x, y, c, chips = _mesh_pos()
        slab = buf[g].at[chip, :, c]
        return pltpu.make_async_remote_copy(
            src_ref=slab, dst_ref=slab, send_sem=sems[0].at[3 * g + j], recv_sem=sems[1].at[3 * g + j],
            device_id=(*chips[j], c), device_id_type=MESH)

    def start(ins, buf, sems):
        x, y, c, chips = _mesh_pos()
        for g in range(n):
            for j in range(3):
                copy(buf, sems, g, j, 2 * x + y).start()

    def finish(ins, buf, sems):
        x, y, c, chips = _mesh_pos()
        for g in range(n):
            for j in range(3):
                copy(buf, sems, g, j, _chip_of(chips[j])).wait_recv()
        for g in range(n):
            for j in range(3):
                copy(buf, sems, g, j, 2 * x + y).wait_send()

    return _Payload(bufs, _same(bufs), _in_place(n), _sems(3 * n), start, finish)


def _gather_d2d(bufs):
    n = len(bufs)

    def copy(buf, sems, g, j, half):
        x, y, c, chips = _mesh_pos()
        slab = buf[g].at[_chip_of(chips[j]), :, half]
        return pltpu.make_async_remote_copy(
            src_ref=slab, dst_ref=slab, send_sem=sems[0].at[3 * g + j], recv_sem=sems[1].at[3 * g + j],
            device_id=(x, y, 1 - c), device_id_type=MESH)

    def start(ins, buf, sems):
        c = lax.axis_index("c")
        for g in range(n):
            for j in range(3):
                copy(buf, sems, g, j, c).start()

    def finish(ins, buf, sems):
        c = lax.axis_index("c")
        for g in range(n):
            for j in range(3):
                copy(buf, sems, g, j, 1 - c).wait_recv()
        for g in range(n):
            for j in range(3):
                copy(buf, sems, g, j, c).wait_send()

    return _Payload(bufs, _same(bufs), _in_place(n), _sems(3 * n), start, finish)


def _pair_exchange(grads):
    n = len(grads)

    def copy(src, dst, sems, g):
        x, y, c, _ = _mesh_pos()
        return pltpu.make_async_remote_copy(
            src_ref=src[g].at[:, :, 1 - c], dst_ref=dst[g], send_sem=sems[0].at[g], recv_sem=sems[1].at[g],
            device_id=(x, y, 1 - c), device_id_type=MESH)

    def start(src, dst, sems):
        for g in range(n):
            copy(src, dst, sems, g).start()

    def finish(src, dst, sems):
        for g in range(n):
            copy(src, dst, sems, g).wait()

    out_shape = [jax.ShapeDtypeStruct(g.shape[:2] + g.shape[3:], g.dtype) for g in grads]
    return _Payload(grads, out_shape, {}, _sems(n), start, finish)


def _chip_exchange(parts):
    n = len(parts)

    def copy(src, dst, sems, g, j, slot):
        x, y, c, chips = _mesh_pos()
        return pltpu.make_async_remote_copy(
            src_ref=src[g].at[_chip_of(chips[j])], dst_ref=dst[g].at[slot], send_sem=sems[0].at[3 * g + j],
            recv_sem=sems[1].at[3 * g + j], device_id=(*chips[j], c), device_id_type=MESH)

    def start(src, dst, sems):
        x, y, c, chips = _mesh_pos()
        for g in range(n):
            for j in range(3):
                copy(src, dst, sems, g, j, 2 * x + y).start()

    def finish(src, dst, sems):
        x, y, c, chips = _mesh_pos()
        for g in range(n):
            for j in range(3):
                copy(src, dst, sems, g, j, _chip_of(chips[j])).wait_recv()
        for g in range(n):
            for j in range(3):
                copy(src, dst, sems, g, j, 2 * x + y).wait_send()

    return _Payload(parts, _same(parts), {}, _sems(3 * n), start, finish)


def _pair_share(bufs):
    n = len(bufs)

    def copy(buf, sems, g, half):
        x, y, c, _ = _mesh_pos()
        slab = buf[g].at[:, :, half]
        return pltpu.make_async_remote_copy(
            src_ref=slab, dst_ref=slab, send_sem=sems[0].at[g], recv_sem=sems[1].at[g],
            device_id=(x, y, 1 - c), device_id_type=MESH)

    def start(ins, buf, sems):
        c = lax.axis_index("c")
        for g in range(n):
            copy(buf, sems, g, c).start()

    def finish(ins, buf, sems):
        c = lax.axis_index("c")
        for g in range(n):
            copy(buf, sems, g, 1 - c).wait_recv()
        for g in range(n):
            copy(buf, sems, g, c).wait_send()

    return _Payload(bufs, _same(bufs), _in_place(n), _sems(n), start, finish)


def _all_reduce_small(vec, name):
    r = vec.shape[0]

    def body(v_ref, o_ref, gath_ref, send_sem, recv_sem):
        x, y, c, _ = _mesh_pos()
        me = 4 * x + 2 * y + c
        gath_ref[me] = v_ref[...]
        copies = []
        for k in range(1, 8):
            peer = (x ^ (k >> 2), y ^ ((k >> 1) & 1), c ^ (k & 1))
            cp = pltpu.make_async_remote_copy(
                src_ref=v_ref, dst_ref=gath_ref.at[me], send_sem=send_sem.at[k - 1], recv_sem=recv_sem.at[k - 1],
                device_id=peer, device_id_type=MESH)
            cp.start()
            copies.append(cp)
        for k in range(1, 8):
            src_id = me ^ k
            pltpu.make_async_remote_copy(
                src_ref=v_ref, dst_ref=gath_ref.at[src_id], send_sem=send_sem.at[k - 1], recv_sem=recv_sem.at[k - 1],
                device_id=(x, y, c), device_id_type=MESH).wait_recv()
        for cp in copies:
            cp.wait_send()
        acc = gath_ref[0]
        for k in range(1, 8):
            acc = acc + gath_ref[k]
        o_ref[...] = acc

    return pl.pallas_call(
        body, name=name,
        in_specs=[pl.BlockSpec(memory_space=pltpu.VMEM)], out_specs=pl.BlockSpec(memory_space=pltpu.VMEM),
        out_shape=jax.ShapeDtypeStruct(vec.shape, F32),
        scratch_shapes=[pltpu.VMEM((8, r, 128), F32), pltpu.SemaphoreType.DMA((7,)), pltpu.SemaphoreType.DMA((7,))],
        compiler_params=pltpu.CompilerParams(has_side_effects=True, vmem_limit_bytes=VMEM_LIMIT))(vec)


def _row_block(rows, cols, mult=16):
    best = None
    for cand in range(mult, rows + 1, mult):
        if rows % cand == 0 and cand * cols * 4 <= EW_BLOCK_BYTES:
            best = cand
    return best or rows


POS_ME, POS_CORE = 0, 4


def _place(arrs, li, pos, dtype, name):
    s = len(arrs)
    _, rows, cols = arrs[0].shape
    rh = rows // 2
    tr = _row_block(rh, cols)
    nb = rh // tr

    def body(pos_ref, *refs):
        o_ref = refs[s]
        for j in range(s):
            @pl.when(pl.program_id(0) == j)
            def _(j=j):
                o_ref[...] = refs[j][...].astype(dtype)

    def in_spec(j):
        return pl.BlockSpec((None, tr, cols), lambda b, hf, i, pos_ref: (li, jnp.where(b == j, hf * nb + i, 0), 0))

    return pl.pallas_call(
        body, name=name,
        grid_spec=pltpu.PrefetchScalarGridSpec(
            num_scalar_prefetch=1, grid=(s, 2, nb), in_specs=[in_spec(j) for j in range(s)],
            out_specs=pl.BlockSpec((None, None, None, tr, cols),
                                   lambda b, hf, i, pos_ref: (pos_ref[POS_ME], b, hf, i, 0))),
        out_shape=jax.ShapeDtypeStruct((N_CHIPS, s, 2, rh, cols), dtype),
        compiler_params=_params(3))(pos, *arrs)


def _pair_sum(grad, recv, pos, out_dtype, name):
    _, s, rh, cols = recv.shape
    tr = _row_block(rh, cols)

    def body(pos_ref, g_ref, r_ref, o_ref):
        o_ref[...] = (g_ref[...] + r_ref[...]).astype(out_dtype)

    blk = (None, None, tr, cols)
    return pl.pallas_call(
        body, name=name,
        grid_spec=pltpu.PrefetchScalarGridSpec(
            num_scalar_prefetch=1, grid=(N_CHIPS, s, rh // tr),
            in_specs=[pl.BlockSpec((None, None, None, tr, cols),
                                   lambda a, b, i, pos_ref: (a, b, pos_ref[POS_CORE], i, 0)),
                      pl.BlockSpec(blk, lambda a, b, i, pos_ref: (a, b, i, 0))],
            out_specs=pl.BlockSpec(blk, lambda a, b, i, pos_ref: (a, b, i, 0))),
        out_shape=jax.ShapeDtypeStruct(recv.shape, out_dtype),
        compiler_params=_params(3))(pos, grad, recv)


def _chip_sum(part, landed, gbuf, li, n_layers, pos, name):
    _, s, rh, cols = part.shape
    tr = _row_block(rh, cols)

    def body(pos_ref, p_ref, a_ref, b_ref, c_ref, *rest):
        o_ref = rest[-1]
        o_ref[...] = ((p_ref[...].astype(F32) + a_ref[...].astype(F32)) + b_ref[...].astype(F32)) \
            + c_ref[...].astype(F32)

    def slot(k):
        return pl.BlockSpec((None, None, tr, cols), lambda b, i, pos_ref: (pos_ref[k], b, i, 0))

    in_specs = [slot(0), slot(1), slot(2), slot(3)]
    operands = [pos, part, landed, landed, landed]
    aliases = {}
    if gbuf is not None:
        in_specs.append(ANY)
        operands.append(gbuf)
        aliases = {len(operands) - 1: 0}
    return pl.pallas_call(
        body, name=name,
        grid_spec=pltpu.PrefetchScalarGridSpec(
            num_scalar_prefetch=1, grid=(s, rh // tr), in_specs=in_specs,
            out_specs=pl.BlockSpec((None, None, None, tr, cols),
                                   lambda b, i, pos_ref: (li, b, pos_ref[POS_CORE], i, 0))),
        out_shape=jax.ShapeDtypeStruct((n_layers, s, 2, rh, cols), F32),
        input_output_aliases=aliases,
        compiler_params=_params(2))(*operands)


def _adamw_math(w, g, m, v):
    m = ADAM_B1 * m + (1.0 - ADAM_B1) * g
    v = ADAM_B2 * v + (1.0 - ADAM_B2) * (g * g)
    m_hat = m / (1.0 - ADAM_B1 ** ADAM_STEP)
    v_hat = v / (1.0 - ADAM_B2 ** ADAM_STEP)
    delta = -ADAM_LR * (m_hat / (jnp.sqrt(v_hat) + ADAM_EPS) + ADAM_WD * w)
    return delta, m, v


def _adamw(w, g, slot, m, v, name):
    l, rows, cols = w.shape
    tr = _row_block(rows, cols, 8)

    def body(w_ref, g_ref, m_ref, v_ref, go_ref, d_ref, mo_ref, vo_ref):
        g_ = g_ref[...]
        delta, m_, v_ = _adamw_math(w_ref[...], g_, m_ref[...], v_ref[...])
        go_ref[...] = g_
        d_ref[...] = delta
        mo_ref[...] = m_
        vo_ref[...] = v_

    blk = pl.BlockSpec((None, tr, cols), lambda a, i: (a, i, 0))
    gblk = pl.BlockSpec((None, None, tr, cols), lambda a, i: (a, slot, i, 0))
    return pl.pallas_call(
        body, name=name, grid=(l, rows // tr), in_specs=[blk, gblk, blk, blk], out_specs=[blk] * 4,
        out_shape=[jax.ShapeDtypeStruct(w.shape, F32)] * 4,
        compiler_params=_params(2))(w, g, m, v)


SQ = ("w_sgu_out", "w_conv_out", "w_pool_out", "w_out", "w_ple_gate")
SMALL = ("g_mix_pre", "w_sgu_s", "b_sgu_s", "g_sgu_v", "b_sgu_v", "b_dw", "g_conv_ln", "b_conv_ln", "s_pool",
         "g_mix_post", "g_ffn_pre", "g_ffn_post")


WHERE = {"w_in": ("in", 0), "w_ffn_in": ("ffn_in", 0), "w_ffn_out": ("ffn_out", 0), "w_ple": ("mix", 0),
         "w_pool": ("mix", 1), "w_dw": ("dw", 0)}
WHERE.update({nm: ("sq", slot) for slot, nm in enumerate(SQ)})


class _LayerWeights:
    def __init__(self, fetch, small, li):
        self.fetch, self.small, self.li, self.cache = fetch, small, li, {}

    def __getitem__(self, nm):
        if nm not in self.cache:
            self.cache[nm] = self._big(nm) if nm in WHERE else self.small[nm][self.li]
        return self.cache[nm]

    def _big(self, nm):
        group, slot = WHERE[nm]
        g = self.fetch(group)
        g = g.reshape(g.shape[:2] + (-1, g.shape[-1]))
        if nm in ("w_in", "w_ffn_in"):
            return g.reshape(N_CHIPS, D_MODEL, -1)
        if nm == "w_ffn_out":
            return g.reshape(D_FF, D_MODEL)
        if nm in SQ:
            return g[:, slot].reshape(D_MODEL, D_MODEL)
        if nm == "w_ple":
            return g[:, slot].transpose(1, 0, 2).reshape(256, D_MODEL)
        if nm == "w_pool":
            return g[:, slot].reshape(N_CHIPS, 4, 64, 256).transpose(1, 0, 2, 3).reshape(4, 256, 256)
        return g.reshape(N_CHIPS, CONV_HALO, -1).transpose(1, 0, 2).reshape(CONV_HALO, D_MODEL)


def _vec(a):
    return a.reshape(1, -1)


def _layer_fwd(h, p, w, li, hosts=None):
    s = {}
    tag = "_l%d" % li
    s["h0"] = h
    proj, hn = _norm_mm(h, _vec(w["g_mix_pre"]), w["w_in"], "mix_in" + tag, _take(hosts, "mix_in"))
    s["proj"], s["hn"] = proj, hn
    bs3 = w["b_sgu_s"].reshape(SGU_HEADS, SGU_BLOCK, 1)
    s["sg"] = _sgu_fwd(proj, w["w_sgu_s"], bs3, _vec(w["g_sgu_v"]), _vec(w["b_sgu_v"]), "sgu_fwd" + tag,
                       _take(hosts, "sgu_fwd"))
    s["cs"], s["cv"] = _conv_fwd(proj, w["w_dw"], _vec(w["b_dw"]), _vec(w["g_conv_ln"]), _vec(w["b_conv_ln"]),
                                 "conv_fwd" + tag, _take(hosts, "conv_fwd"))
    s["ps"] = _pool_fwd(proj, w["w_pool"], _vec(w["s_pool"]), "pool_fwd" + tag, _take(hosts, "pool_fwd"))
    s["bra"], s["brb"], s["brc"], s["merged"] = _merge_fwd(
        proj, s["sg"], s["cs"], s["ps"], w["w_sgu_out"], w["w_conv_out"], w["w_pool_out"], "merge_fwd" + tag,
        _take(hosts, "merge_fwd"))
    s["mo"], h1 = _mm_norm_res(s["merged"], w["w_out"], _vec(w["g_mix_post"]), h, "mix_out" + tag,
                               _take(hosts, "mix_out"))
    s["h1"] = h1
    s["fg"], s["fu"], s["act"], s["hn2"] = _ffn_in(h1, _vec(w["g_ffn_pre"]), w["w_ffn_in"], "ffn_in" + tag,
                                                   _take(hosts, "ffn_in"))
    s["f"], h2 = _mm_norm_res(s["act"], w["w_ffn_out"], _vec(w["g_ffn_post"]), h1, "ffn_out" + tag,
                              _take(hosts, "ffn_out"))
    s["h2"] = h2
    h3, s["q"], s["e"] = _ple_fwd(h2, p, w["w_ple_gate"], w["w_ple"], "ple_fwd" + tag, _take(hosts, "ple_fwd"))
    return h3, s


def _layer_bwd(dh3, p, w, s, li, hosts=None, big=None):
    tag = "_l%d" % li
    d = D_MODEL
    gs = {}
    big = {} if big is None else big
    dq, de, dh2 = _ple_bwd(dh3, s["q"], s["e"], w["w_ple_gate"], "ple_bwd" + tag)
    dw_ple = _mm_tn(p, de, "dw_ple" + tag)
    sq = _dw_slot(s["h2"], dq, None, SQ.index("w_ple_gate"), len(SQ), "dw_ple_gate" + tag)
    df, dff, gs["g_ffn_post"] = _ffn_out_bwd(dh2, s["f"], _vec(w["g_ffn_post"]), s["fg"], s["fu"], w["w_ffn_out"],
                                             "ffn_out_bwd" + tag, _take(hosts, "ffn_out_bwd"))
    dw_ffn_out = _mm_tn(s["act"], df, "dw_ffn_out" + tag)
    big["ffn_out"] = dw_ffn_out.reshape(N_CHIPS, 1, D_FF // N_CHIPS, d)
    n_ff = w["w_ffn_in"].shape[2]
    dh1, gs["g_ffn_pre"] = _in_bwd([(dff, 2 * D_FF // n_ff)], w["w_ffn_in"], n_ff, s["h1"], _vec(w["g_ffn_pre"]),
                                   dh2, ROW_TILE, "ffn_in_bwd" + tag, _take(hosts, "ffn_in_bwd"))
    big["ffn_in"] = _dw_cols(s["hn2"], [(dff, 2 * D_FF // n_ff)], n_ff, 1, "dw_ffn_in" + tag)
    (dmo, dbra, dbrb, dbrc, dzg, dsg, dcs, dps, gs["g_mix_post"]) = _merge_bwd(
        dh1, s["mo"], _vec(w["g_mix_post"]), s["proj"], s["bra"], s["brb"], s["brc"], w["w_out"], w["w_sgu_out"],
        w["w_conv_out"], w["w_pool_out"], "merge_bwd" + tag, _take(hosts, "merge_bwd"))
    for nm, x_, dy_ in (("w_out", s["merged"], dmo), ("w_sgu_out", s["sg"], dbra), ("w_conv_out", s["cs"], dbrb),
                        ("w_pool_out", s["ps"], dbrc)):
        sq = _dw_slot(x_, dy_, sq, SQ.index(nm), len(SQ), "d" + nm + tag)
    big["sq"] = sq
    bs3 = w["b_sgu_s"].reshape(SGU_HEADS, SGU_BLOCK, 1)
    dz_sgu, gs["w_sgu_s"], dbs3, gs["g_sgu_v"], gs["b_sgu_v"] = _sgu_bwd(
        s["proj"], dsg, w["w_sgu_s"], bs3, _vec(w["g_sgu_v"]), _vec(w["b_sgu_v"]), "sgu_bwd" + tag,
        _take(hosts, "sgu_bwd"))
    gs["b_sgu_s"] = dbs3
    dcv, dwdw, gs["b_dw"], gs["g_conv_ln"], gs["b_conv_ln"] = _conv_bwd_norm(
        s["proj"], dcs, s["cv"], _vec(w["b_dw"]), _vec(w["g_conv_ln"]), _vec(w["b_conv_ln"]), "conv_bwd_norm" + tag,
        _take(hosts, "conv_bwd_norm"))
    dz_conv = _conv_bwd_taps(s["proj"], dcv, w["w_dw"], "conv_bwd_taps" + tag, _take(hosts, "conv_bwd_taps"))
    dz_pool, dwpool, gs["s_pool"] = _pool_bwd(s["proj"], dps, w["w_pool"], _vec(w["s_pool"]), "pool_bwd" + tag)
    pieces = [(dz_sgu, 2), (dz_conv, 2), (dz_pool, 1), (dzg, 3)]
    big["in"] = _dw_cols(s["hn"], pieces, d, 2, "dw_in" + tag)
    gple = dw_ple.reshape(256, N_CHIPS, 256).transpose(1, 0, 2)
    gpool = dwpool.reshape(4, N_CHIPS, 64, 256).transpose(1, 0, 2, 3).reshape(N_CHIPS, 256, 256)
    big["mix"] = jnp.stack([gple, gpool], axis=1)
    big["dw"] = dwdw.reshape(CONV_HALO, N_CHIPS, 256).transpose(1, 0, 2)[:, None]
    dh0, gs["g_mix_pre"] = _in_bwd(pieces, w["w_in"], d, s["h0"], _vec(w["g_mix_pre"]), dh1, ROW_TILE_HEAVY,
                                   "mix_in_bwd" + tag, _take(hosts, "mix_in_bwd"))
    return dh0, big, gs


GROUPS = ("in", "sq", "ffn_in", "ffn_out", "mix", "dw")
WIRE_DTYPE = {"in": BF16, "sq": BF16, "ffn_in": BF16, "ffn_out": BF16, "mix": BF16, "dw": F32}
GATHER_FIRST = ("in", "mix", "dw")
GATHER_RIDES = (("mix_in", "sgu_fwd", ("sq", "ffn_in"), ()),
                ("conv_fwd", "pool_fwd", ("ffn_out",), ("in",)),
                ("merge_fwd", "mix_out", (), ("sq",)),
                ("ffn_in", "ffn_out", (), ("ffn_in", "ffn_out", "mix", "dw")))
REDUCE_UPPER = ("ffn_out_bwd", (("ffn_in_bwd", ("in", "ffn_out")), ("merge_bwd", ("sq", "ffn_in", "mix", "dw"))))
REDUCE_OWN = ("sgu_bwd", (("conv_bwd_norm", ("ffn_in", "ffn_out")), ("conv_bwd_taps", ("sq",))))
REDUCE_LAST = ("in", "mix", "dw")


def _group_members(wts):
    n_layers = wts["w_in"].shape[0]
    dw = wts["w_dw"].reshape(n_layers, CONV_WIDTH, -1)
    return {"in": [wts["w_in"]], "sq": [wts[nm] for nm in SQ], "ffn_in": [wts["w_ffn_in"]],
            "ffn_out": [wts["w_ffn_out"]],
            "mix": [wts["w_ple"], wts["w_pool"].reshape(n_layers, POOL_GROUP, POOL_GROUP)],
            "dw": [jnp.pad(dw, ((0, 0), (0, CONV_HALO - CONV_WIDTH), (0, 0)))]}


class _Gather:
    PLACED, OVER_ICI, FULL = 0, 1, 2

    def __init__(self):
        self.buf, self.stage, self.pending = {}, {}, []

    def put(self, key, buf):
        self.buf[key], self.stage[key] = buf, self.PLACED

    def _flush(self):
        for keys, pay, stage in self.pending:
            if pay.results is not None:
                for key, res in zip(keys, pay.results):
                    self.buf[key], self.stage[key] = res, stage
        self.pending = [entry for entry in self.pending if entry[1].results is None]

    def _factory(self, make, keys, before, after):
        def factory():
            if not keys:
                return None
            self._flush()
            assert all(self.stage[k] == before for k in keys), (keys, self.stage)
            pay = make([self.buf[k] for k in keys])
            self.pending.append((keys, pay, after))
            return pay
        return factory

    def ici(self, keys):
        return self._factory(_gather_ici, keys, self.PLACED, self.OVER_ICI)

    def d2d(self, keys):
        return self._factory(_gather_d2d, keys, self.OVER_ICI, self.FULL)

    def get(self, li, group):
        self._flush()
        assert self.stage[(li, group)] == self.FULL, (li, group)
        return self.buf[(li, group)]


class _Reduce:
    def __init__(self, pos, n_layers):
        self.pos, self.n_layers, self.exchanged, self.stages = pos, n_layers, [], []

    def exchange(self, li, groups, grads):
        def factory():
            pay = _pair_exchange([_half_view(grads[g]) for g in groups])
            self.exchanged.append((li, list(groups), pay))
            return pay
        return factory

    def _received(self, li, group):
        for lj, groups, pay in self.exchanged:
            if lj == li and group in groups:
                return pay.results[groups.index(group)]
        raise KeyError((li, group))

    def chips(self, li, groups, grads):
        def factory():
            parts = [_pair_sum(_half_view(grads[g]), self._received(li, g), self.pos, WIRE_DTYPE[g],
                               "pair_sum_%s_l%d" % (g, li)) for g in groups]
            pay = _chip_exchange(parts)
            self.stages.append((li, groups, parts, pay))
            return pay
        return factory

    def finish(self):
        reduced = {}
        for li, groups, parts, pay in self.stages:
            for g, part, landed in zip(groups, parts, pay.results):
                reduced[g] = _chip_sum(part, landed, reduced.get(g), li, self.n_layers, self.pos,
                                       "chip_sum_%s_l%d" % (g, li))
        return reduced


def _pack_small(tree):
    flat = jnp.concatenate([tree[nm].reshape(-1).astype(F32) for nm in SMALL])
    return flat.reshape(-1, 128)


def _unpack_small(packed, like):
    out, off = {}, 0
    flat = packed.reshape(-1)
    for nm in SMALL:
        n = like[nm].size
        out[nm] = flat[off:off + n].reshape(like[nm].shape)
        off += n
    return out


WEIGHTS = ("g_mix_pre", "w_in", "w_sgu_s", "b_sgu_s", "g_sgu_v", "b_sgu_v", "w_sgu_out", "w_dw", "b_dw", "g_conv_ln",
           "b_conv_ln", "w_conv_out", "w_pool", "s_pool", "w_pool_out", "w_out", "g_mix_post", "g_ffn_pre",
           "w_ffn_in", "w_ffn_out", "g_ffn_post", "w_ple", "w_ple_gate")


def kernel(x, p, g_mix_pre, w_in, w_sgu_s, b_sgu_s, g_sgu_v, b_sgu_v, w_sgu_out, w_dw, b_dw, g_conv_ln, b_conv_ln, w_conv_out, w_pool, s_pool, w_pool_out, w_out, g_mix_post, g_ffn_pre, w_ffn_in, w_ffn_out, g_ffn_post, w_ple, w_ple_gate, loss_target, m_g_mix_pre, m_w_in, m_w_sgu_s, m_b_sgu_s, m_g_sgu_v, m_b_sgu_v, m_w_sgu_out, m_w_dw, m_b_dw, m_g_conv_ln, m_b_conv_ln, m_w_conv_out, m_w_pool, m_s_pool, m_w_pool_out, m_w_out, m_g_mix_post, m_g_ffn_pre, m_w_ffn_in, m_w_ffn_out, m_g_ffn_post, m_w_ple, m_w_ple_gate, v_g_mix_pre, v_w_in, v_w_sgu_s, v_b_sgu_s, v_g_sgu_v, v_b_sgu_v, v_w_sgu_out, v_w_dw, v_b_dw, v_g_conv_ln, v_b_conv_ln, v_w_conv_out, v_w_pool, v_s_pool, v_w_pool_out, v_w_out, v_g_mix_post, v_g_ffn_pre, v_w_ffn_in, v_w_ffn_out, v_g_ffn_post, v_w_ple, v_w_ple_gate):
    args = dict(locals())
    wts = {nm: args[nm] for nm in WEIGHTS}
    mom = {nm: args["m_" + nm] for nm in WEIGHTS}
    var = {nm: args["v_" + nm] for nm in WEIGHTS}
    n_layers = w_in.shape[0]
    h = x.reshape(x.shape[1:])
    target = loss_target.reshape(loss_target.shape[1:])
    cx, cy, core = lax.axis_index("x"), lax.axis_index("y"), lax.axis_index("c")
    pos = jnp.stack([2 * cx + cy, 2 * (1 - cx) + cy, 2 * cx + (1 - cy), 2 * (1 - cx) + (1 - cy), core])
    pos = pos.astype(jnp.int32)

    members = _group_members(wts)
    gather = _Gather()
    for li in range(n_layers):
        for g in GROUPS:
            gather.put((li, g), _place(members[g], li, pos, WIRE_DTYPE[g], "place_%s_l%d" % (g, li)))
    first = [(0, g) for g in GATHER_FIRST]
    _run_payload(gather.ici(first)(), "gather_ici_first")
    _run_payload(gather.d2d(first)(), "gather_d2d_first")

    saved, layer_w = [], []
    for li in range(n_layers):
        hosts = {}
        for ici_host, d2d_host, own, nxt in GATHER_RIDES:
            keys = [(li, g) for g in own if li == 0] + [(li + 1, g) for g in nxt if li + 1 < n_layers]
            hosts[ici_host], hosts[d2d_host] = gather.ici(keys), gather.d2d(keys)
        w = _LayerWeights(functools.partial(gather.get, li), wts, li)
        layer_w.append(w)
        h, s = _layer_fwd(h, p[li, 0], w, li, hosts)
        saved.append(s)
    dh, sq_err = _loss_head(h, target, "loss_head")
    loss = lax.psum(sq_err[0, 0] * (0.5 / D_MODEL), ("x", "y", "c"))

    reduce = _Reduce(pos, n_layers)
    small_grads = [None] * n_layers
    upper = None
    for li in reversed(range(n_layers)):
        own = {}
        hosts = {}
        plans = [(REDUCE_UPPER, li + 1, upper)] if upper is not None else []
        if li == 0:
            plans.append((REDUCE_OWN, 0, own))

            def last_groups(own=own):
                _run_payload(reduce.exchange(0, REDUCE_LAST, own)(), "pair_exchange_last")
                return reduce.chips(0, REDUCE_LAST, own)()
            hosts["mix_in_bwd"] = last_groups
        for (pair_host, chip_hosts), lj, grads in plans:
            groups = [g for _, gs_ in chip_hosts for g in gs_]
            hosts[pair_host] = reduce.exchange(lj, groups, grads)
            for chip_host, gs_ in chip_hosts:
                hosts[chip_host] = reduce.chips(lj, gs_, grads)
        dh, upper, small_grads[li] = _layer_bwd(dh, p[li, 0], layer_w[li], saved[li], li, hosts, own)
    grad_x = dh[None]
    reduced = reduce.finish()

    shared = _run_payload(_pair_share([reduced[g] for g in GROUPS]), "pair_share")
    red = {g: b.reshape(b.shape[:2] + (-1, b.shape[-1])) for g, b in zip(GROUPS, shared)}

    where = {"w_in": ("in", 0), "w_ffn_in": ("ffn_in", 0), "w_ffn_out": ("ffn_out", 0), "w_ple": ("mix", 0),
             "w_pool": ("mix", 1)}
    for slot, nm in enumerate(SQ):
        where[nm] = ("sq", slot)
    outs = {}
    for nm, (g, slot) in where.items():
        shape = wts[nm].shape
        to3 = lambda a: a.reshape((n_layers,) + red[g].shape[2:])
        res = _adamw(to3(wts[nm]), red[g], slot, to3(mom[nm]), to3(var[nm]), "adamw_" + nm)
        outs[nm] = [r.reshape(shape) for r in res]
    gdw = red["dw"][:, :, :CONV_WIDTH]
    to3 = lambda a: a.reshape(n_layers, CONV_WIDTH, -1)
    res = _adamw(to3(wts["w_dw"]), gdw, 0, to3(mom["w_dw"]), to3(var["w_dw"]), "adamw_w_dw")
    outs["w_dw"] = [r.reshape(wts["w_dw"].shape) for r in res]

    small_tree = {nm: jnp.stack([small_grads[li][nm].reshape(wts[nm].shape[1:]) for li in range(n_layers)], axis=0)
                  for nm in SMALL}
    gsmall = _all_reduce_small(_pack_small(small_tree), "all_reduce_small")
    pk = lambda tree: _pack_small({nm: tree[nm] for nm in SMALL})[None]
    res = _adamw(pk(wts), gsmall[None, None], 0, pk(mom), pk(var), "adamw_small")
    unpacked = [_unpack_small(r[0], wts) for r in res]
    for nm in SMALL:
        outs[nm] = [u[nm] for u in unpacked]

    result = [loss, grad_x]
    for k in range(4):
        result += [outs[nm][k] for nm in WEIGHTS]
    return tuple(result)
```

```python
import functools

import jax
import jax.numpy as jnp
from jax import lax
from jax.experimental import pallas as pl
from jax.experimental.pallas import tpu as pltpu

F32 = jnp.float32
BF16 = jnp.bfloat16
MESH = pl.DeviceIdType.MESH

EPS = 1e-6
D_MODEL = 1024
SGU_BLOCK = 128
SGU_HEADS = 8
CHUNK = 64
CONV_WIDTH = 31
CONV_HALO = 32
POOL_WINDOWS = (2, 4, 8, 16)
POOL_HALO = 16
POOL_GROUP = 256
D_FF = 2816
N_CHIPS = 4

ADAM_LR = 0.001
ADAM_B1 = 0.9
ADAM_B2 = 0.999
ADAM_EPS = 1e-08
ADAM_WD = 0.01
ADAM_STEP = 10

VMEM_LIMIT = 52 * 1024 * 1024
ROW_TILE = 512
ROW_TILE_HEAVY = 256
CONV_ROWS = 64
CONV_LANES = 128
EW_BLOCK_BYTES = 2 * 1024 * 1024
TOKEN_TILE = 2048
FF_CHUNK = 256


def _params(n_grid):
    return pltpu.CompilerParams(dimension_semantics=("arbitrary",) * n_grid, vmem_limit_bytes=VMEM_LIMIT)


def _dot(a, b):
    return jnp.dot(a.astype(BF16), b.astype(BF16), preferred_element_type=F32)


def _dot_nt(a, b):
    return lax.dot_general(a.astype(BF16), b.astype(BF16), (((1,), (1,)), ((), ())), preferred_element_type=F32)


def _dot_tn(a, b):
    return lax.dot_general(a.astype(BF16), b.astype(BF16), (((0,), (0,)), ((), ())), preferred_element_type=F32)


def _sigmoid(x):
    return 0.5 * jnp.tanh(0.5 * x) + 0.5


_GELU_C = 0.7978845608028654
_GELU_A = 0.044715


def _gelu(x):
    t = jnp.tanh(_GELU_C * (x + _GELU_A * x * x * x))
    return 0.5 * x * (1.0 + t)


def _gelu_and_grad(x):
    x2 = x * x
    t = jnp.tanh(_GELU_C * (x + _GELU_A * x2 * x))
    g = 0.5 * (1.0 + t) + 0.5 * x * (1.0 - t * t) * (_GELU_C * (1.0 + 3.0 * _GELU_A * x2))
    return 0.5 * x * (1.0 + t), g


def _rms_stats(x):
    r = lax.rsqrt(jnp.mean(x * x, axis=-1, keepdims=True) + EPS)
    return x * r, r


def _rms_bwd(xn, r, g, dy):
    gd = dy * g
    return r * (gd - xn * jnp.mean(gd * xn, axis=-1, keepdims=True)), dy * xn


def _ln_stats(x):
    mu = jnp.mean(x, axis=-1, keepdims=True)
    xc = x - mu
    rstd = lax.rsqrt(jnp.mean(xc * xc, axis=-1, keepdims=True) + EPS)
    return xc * rstd, rstd


def _ln_bwd(xhat, rstd, g, dy):
    dxh = dy * g
    return rstd * (dxh - jnp.mean(dxh, axis=-1, keepdims=True) - xhat * jnp.mean(dxh * xhat, axis=-1, keepdims=True))


def _rowsum(x):
    return jnp.sum(x, axis=0, keepdims=True)


def _tile(t, want):
    return min(t, want)


def _full(shape):
    n = len(shape)
    return pl.BlockSpec(shape, lambda *_: (0,) * n)


def _resident(shape):
    n = len(shape)
    return pl.BlockSpec(shape, lambda *_: (0,) * n, pipeline_mode=pl.Buffered(1))


class _Payload:
    def __init__(self, operands, out_shape, aliases, scratch, start, finish):
        self.operands, self.out_shape, self.aliases, self.scratch = list(operands), list(out_shape), aliases, scratch
        self.start, self.finish = start, finish
        self.results = None
        self.parts = None

    def deliver(self, results):
        self.results = list(results)
        if self.parts:
            a, b, ma = self.parts
            a.deliver(self.results[:ma])
            b.deliver(self.results[ma:])


def _pcall(body, *, name, grid, in_specs, out_specs, out_shape, operands, scratch_shapes=(), comm=None):
    single = not isinstance(out_shape, (list, tuple))
    out_specs = [out_specs] if single else list(out_specs)
    out_shape = [out_shape] if single else list(out_shape)
    if comm is None:
        res = pl.pallas_call(
            body, name=name, grid=grid, in_specs=list(in_specs), out_specs=out_specs, out_shape=out_shape,
            scratch_shapes=list(scratch_shapes), compiler_params=_params(len(grid)))(*operands)
        return res[0] if single else res
    n_in, n_out, n_scr = len(in_specs), len(out_shape), len(scratch_shapes)
    ci, co = len(comm.operands), len(comm.out_shape)

    def hosted(*refs):
        bounds = [0, n_in, n_in + ci, n_in + ci + n_out, n_in + ci + n_out + co, n_in + ci + n_out + co + n_scr]
        a, b, c_, d_, s_ = [refs[lo:hi] for lo, hi in zip(bounds[:-1], bounds[1:])]
        t_ = refs[bounds[-1]:]
        ids = [pl.program_id(q) for q in range(len(grid))]
        first = functools.reduce(jnp.logical_and, [i == 0 for i in ids])
        last = functools.reduce(jnp.logical_and, [i == pl.num_programs(q) - 1 for q, i in enumerate(ids)])

        @pl.when(first)
        def _():
            comm.start(b, d_, t_)

        body(*a, *c_, *s_)

        @pl.when(last)
        def _():
            comm.finish(b, d_, t_)

    res = pl.pallas_call(
        hosted, name=name, grid=grid, in_specs=list(in_specs) + [ANY] * ci, out_specs=out_specs + [ANY] * co,
        out_shape=out_shape + comm.out_shape, scratch_shapes=list(scratch_shapes) + list(comm.scratch),
        input_output_aliases={n_in + i: n_out + o for i, o in comm.aliases.items()},
        compiler_params=pltpu.CompilerParams(dimension_semantics=("arbitrary",) * len(grid),
                                             vmem_limit_bytes=VMEM_LIMIT, has_side_effects=True),
    )(*operands, *comm.operands)
    comm.deliver(res[n_out:])
    res = res[:n_out]
    return res[0] if single else res


def _run_payload(comm, name):
    ci, co = len(comm.operands), len(comm.out_shape)

    def body(*refs):
        b, d_, t_ = refs[:ci], refs[ci:ci + co], refs[ci + co:]
        comm.start(b, d_, t_)
        comm.finish(b, d_, t_)

    res = pl.pallas_call(
        body, name=name, in_specs=[ANY] * ci, out_specs=[ANY] * co, out_shape=comm.out_shape,
        input_output_aliases=dict(comm.aliases), scratch_shapes=list(comm.scratch),
        compiler_params=pltpu.CompilerParams(has_side_effects=True))(*comm.operands)
    comm.deliver(res)
    return comm.results


def _take(hosts, key):
    return hosts[key]() if hosts and key in hosts else None


def _norm_mm(h, g, w4, name, comm=None):
    t, d = h.shape
    n = w4.shape[2]
    tm = _tile(t, ROW_TILE)

    step = _lane_block(n, 1024)

    def body(h_ref, g_ref, w_ref, o_ref, hn_ref):
        xn, _ = _rms_stats(h_ref[...])
        hn = (xn * g_ref[...]).astype(BF16)
        hn_ref[...] = hn
        for j in range(N_CHIPS):
            for c0 in range(0, n, step):
                o_ref[:, j * n + c0:j * n + c0 + step] = jnp.dot(
                    hn, w_ref[j, :, c0:c0 + step], preferred_element_type=F32).astype(BF16)

    return _pcall(
        body, name=name, grid=(t // tm,),
        in_specs=[pl.BlockSpec((tm, d), lambda i: (i, 0)), _full((1, d)), _resident(w4.shape)],
        out_specs=[pl.BlockSpec((tm, N_CHIPS * n), lambda i: (i, 0)), pl.BlockSpec((tm, d), lambda i: (i, 0))],
        out_shape=[jax.ShapeDtypeStruct((t, N_CHIPS * n), BF16), jax.ShapeDtypeStruct((t, d), BF16)],
        operands=(h, g, w4), comm=comm)


def _sgu_mask():
    ii = lax.broadcasted_iota(jnp.int32, (SGU_BLOCK, SGU_BLOCK), 0) // CHUNK
    jj = lax.broadcasted_iota(jnp.int32, (SGU_BLOCK, SGU_BLOCK), 1) // CHUNK
    return jj <= ii


def _sgu_fwd(proj, wm, bs3, gv, bv, name, comm=None):
    t = proj.shape[0]
    d = D_MODEL
    tm = _tile(t, ROW_TILE)
    hd = d // SGU_HEADS

    def body(zu_ref, zv_ref, wm_ref, bs_ref, gv_ref, bv_ref, o_ref):
        mask = _sgu_mask()
        for blk in range(tm // SGU_BLOCK):
            rows = pl.ds(blk * SGU_BLOCK, SGU_BLOCK)
            u = _gelu(zu_ref[rows, :].astype(F32))
            xhat, _ = _ln_stats(_gelu(zv_ref[rows, :].astype(F32)))
            vn = (xhat * gv_ref[...] + bv_ref[...]).astype(BF16)
            for hh in range(SGU_HEADS):
                cols = slice(hh * hd, (hh + 1) * hd)
                wmh = jnp.where(mask, wm_ref[hh], 0.0).astype(BF16)
                mixed = jnp.dot(wmh, vn[:, cols], preferred_element_type=F32) + bs_ref[hh]
                o_ref[rows, cols] = (u[:, cols] * mixed).astype(BF16)

    return _pcall(
        body, name=name, grid=(t // tm,),
        in_specs=[pl.BlockSpec((tm, d), lambda i: (i, 0)), pl.BlockSpec((tm, d), lambda i: (i, 1)),
                  _full(wm.shape), _full(bs3.shape), _full(gv.shape), _full(bv.shape)],
        out_specs=pl.BlockSpec((tm, d), lambda i: (i, 0)),
        out_shape=jax.ShapeDtypeStruct((t, d), BF16),
        operands=(proj, proj, wm, bs3, gv, bv), comm=comm)


def _conv_taps(scr_ref, r0, c0, base, weight):
    n = CONV_ROWS + CONV_HALO
    win = scr_ref[pl.ds(r0, n), pl.ds(c0, CONV_LANES)]
    acc = None
    for r in range(8):
        rolled = win if r == 0 else pltpu.roll(win, n - r, 0)
        for q in range((CONV_HALO + 7) // 8 + 1):
            k = 8 * q + r - base
            if 0 <= k < CONV_WIDTH and 8 * q + CONV_ROWS <= n:
                term = weight(k) * rolled[8 * q:8 * q + CONV_ROWS]
                acc = term if acc is None else acc + term
    return acc


def _glu_rows(a_ref, g_ref):
    return a_ref[...].astype(F32) * _sigmoid(g_ref[...].astype(F32))


def _conv_into(scr_ref, cv_ref, w_ref, tm, base, flip):
    def chunk(ci, carry):
        r0 = pl.multiple_of(ci * CONV_ROWS, CONV_ROWS)
        for c0 in range(0, D_MODEL, CONV_LANES):
            def weight(k, c0=c0):
                kk = CONV_WIDTH - 1 - k if flip else k
                return w_ref[kk:kk + 1, c0:c0 + CONV_LANES]
            cv_ref[pl.ds(r0, CONV_ROWS), pl.ds(c0, CONV_LANES)] = _conv_taps(scr_ref, r0, c0, base, weight)
        return carry

    lax.fori_loop(0, tm // CONV_ROWS, chunk, 0)


def _conv_specs(t, tm, d):
    hb = tm // CONV_HALO
    main = [pl.BlockSpec((tm, d), lambda i: (i, 2)), pl.BlockSpec((tm, d), lambda i: (i, 3))]
    halo = [pl.BlockSpec((CONV_HALO, d), lambda i: (jnp.maximum(i * hb - 1, 0), 2)),
            pl.BlockSpec((CONV_HALO, d), lambda i: (jnp.maximum(i * hb - 1, 0), 3))]
    return main, halo


def _fill_glu_history(scr_ref, a_ref, g_ref, ah_ref, gh_ref, tm):
    hist = _glu_rows(ah_ref, gh_ref)
    scr_ref[0:CONV_HALO, :] = jnp.where(pl.program_id(0) > 0, hist, 0.0)
    scr_ref[CONV_HALO:CONV_HALO + tm, :] = _glu_rows(a_ref, g_ref)


_CONV_BASE = CONV_HALO - (CONV_WIDTH - 1)


def _conv_fwd(proj, wdw, bdw, gln, bln, name, comm=None):
    t = proj.shape[0]
    d = D_MODEL
    tm = _tile(t, ROW_TILE)
    main, halo = _conv_specs(t, tm, d)

    def body(a_ref, g_ref, ah_ref, gh_ref, w_ref, b_ref, gl_ref, bl_ref, o_ref, cv_ref, scr_ref):
        _fill_glu_history(scr_ref, a_ref, g_ref, ah_ref, gh_ref, tm)
        _conv_into(scr_ref, cv_ref, w_ref, tm, _CONV_BASE, False)
        xhat, _ = _ln_stats(cv_ref[...] + b_ref[...])
        cn = xhat * gl_ref[...] + bl_ref[...]
        o_ref[...] = (cn * _sigmoid(cn)).astype(BF16)

    row = pl.BlockSpec((tm, d), lambda i: (i, 0))
    return _pcall(
        body, name=name, grid=(t // tm,),
        in_specs=main + halo + [_full(wdw.shape), _full(bdw.shape), _full(gln.shape), _full(bln.shape)],
        out_specs=[row, row],
        out_shape=[jax.ShapeDtypeStruct((t, d), BF16), jax.ShapeDtypeStruct((t, d), F32)],
        scratch_shapes=[pltpu.VMEM((tm + CONV_HALO, d), F32)],
        operands=(proj, proj, proj, proj, wdw, bdw, gln, bln), comm=comm)


def _pool_fill(scr_ref, z_ref, zh_ref, tm):
    scr_ref[0:POOL_HALO, :] = jnp.where(pl.program_id(0) > 0, zh_ref[...].astype(F32), 0.0)
    scr_ref[POOL_HALO:POOL_HALO + tm, :] = z_ref[...].astype(F32)


def _pool_count(t0, rows, w):
    pos = (t0 + lax.broadcasted_iota(jnp.int32, (rows, 1), 0) + 1).astype(F32)
    return jnp.minimum(pos, float(w))


def _pooled_group(scr_ref, gi, w, tm, t0):
    cols = pl.ds(gi * POOL_GROUP, POOL_GROUP)
    acc = scr_ref[pl.ds(POOL_HALO, tm), cols]
    z = acc
    for k in range(1, w):
        acc = acc + scr_ref[pl.ds(POOL_HALO - k, tm), cols]
    return acc / _pool_count(t0, tm, w) - z


def _pool_specs(tm, d):
    hb = tm // POOL_HALO
    return [pl.BlockSpec((tm, d), lambda i: (i, 4)),
            pl.BlockSpec((POOL_HALO, d), lambda i: (jnp.maximum(i * hb - 1, 0), 4))]


def _pool_fwd(proj, wpool, spool, name, comm=None):
    t = proj.shape[0]
    d = D_MODEL
    tm = _tile(t, ROW_TILE)

    def body(z_ref, zh_ref, w_ref, s_ref, o_ref, scr_ref):
        _pool_fill(scr_ref, z_ref, zh_ref, tm)
        t0 = pl.program_id(0) * tm
        for gi, w in enumerate(POOL_WINDOWS):
            cols = slice(gi * POOL_GROUP, (gi + 1) * POOL_GROUP)
            pooled = _pooled_group(scr_ref, gi, w, tm, t0)
            o_ref[:, cols] = (_dot(pooled, w_ref[gi]) * s_ref[:, cols]).astype(BF16)

    return _pcall(
        body, name=name, grid=(t // tm,),
        in_specs=_pool_specs(tm, d) + [_full(wpool.shape), _full(spool.shape)],
        out_specs=pl.BlockSpec((tm, d), lambda i: (i, 0)),
        out_shape=jax.ShapeDtypeStruct((t, d), BF16),
        scratch_shapes=[pltpu.VMEM((tm + POOL_HALO, d), F32)],
        operands=(proj, proj, wpool, spool), comm=comm)


def _merge_fwd(proj, sg, cs, ps, wa, wb, wc, name, comm=None):
    t = proj.shape[0]
    d = D_MODEL
    tm = _tile(t, ROW_TILE_HEAVY)

    def body(za_ref, zb_ref, zc_ref, sg_ref, cs_ref, ps_ref, wa_ref, wb_ref, wc_ref, ba_ref, bb_ref, bc_ref, m_ref):
        merged = None
        for z_ref, x_ref, w_ref, b_ref in ((za_ref, sg_ref, wa_ref, ba_ref), (zb_ref, cs_ref, wb_ref, bb_ref),
                                           (zc_ref, ps_ref, wc_ref, bc_ref)):
            br = jnp.dot(x_ref[...], w_ref[...], preferred_element_type=F32)
            b_ref[...] = br.astype(BF16)
            term = _sigmoid(z_ref[...].astype(F32)) * br
            merged = term if merged is None else merged + term
        m_ref[...] = merged.astype(BF16)

    row = pl.BlockSpec((tm, d), lambda i: (i, 0))
    wspec = _resident((d, d))
    return _pcall(
        body, name=name, grid=(t // tm,),
        in_specs=[pl.BlockSpec((tm, d), lambda i: (i, 5)), pl.BlockSpec((tm, d), lambda i: (i, 6)),
                  pl.BlockSpec((tm, d), lambda i: (i, 7)), row, row, row, wspec, wspec, wspec],
        out_specs=[row, row, row, row],
        out_shape=[jax.ShapeDtypeStruct((t, d), BF16)] * 4,
        operands=(proj, proj, proj, sg, cs, ps, wa, wb, wc), comm=comm)


def _mm_norm_res(a, w, g, hres, name, comm=None):
    t, k = a.shape
    d = w.shape[1]
    tm = _tile(t, ROW_TILE)

    def body(a_ref, w_ref, g_ref, h_ref, y_ref, o_ref):
        y = jnp.dot(a_ref[...], w_ref[...], preferred_element_type=F32)
        y_ref[...] = y
        yn, _ = _rms_stats(y)
        o_ref[...] = h_ref[...] + yn * g_ref[...]

    row = pl.BlockSpec((tm, d), lambda i: (i, 0))
    return _pcall(
        body, name=name, grid=(t // tm,),
        in_specs=[pl.BlockSpec((tm, k), lambda i: (i, 0)), _resident(w.shape), _full(g.shape), row],
        out_specs=[row, row],
        out_shape=[jax.ShapeDtypeStruct((t, d), F32)] * 2,
        operands=(a, w, g, hres), comm=comm)


def _ffn_in(h, g, w4, name, comm=None):
    t, d = h.shape
    n = w4.shape[2]
    tm = _tile(t, ROW_TILE)
    nj = D_FF // n

    def body(h_ref, g_ref, w_ref, fg_ref, fu_ref, act_ref, hn_ref):
        xn, _ = _rms_stats(h_ref[...])
        hn = (xn * g_ref[...]).astype(BF16)
        hn_ref[...] = hn
        for j in range(nj):
            cols = slice(j * n, (j + 1) * n)
            fg = jnp.dot(hn, w_ref[j], preferred_element_type=F32)
            fu = jnp.dot(hn, w_ref[j + nj], preferred_element_type=F32)
            fg_ref[:, cols] = fg.astype(BF16)
            fu_ref[:, cols] = fu.astype(BF16)
            act_ref[:, cols] = (fg * _sigmoid(fg) * fu).astype(BF16)

    wide = pl.BlockSpec((tm, D_FF), lambda i: (i, 0))
    return _pcall(
        body, name=name, grid=(t // tm,),
        in_specs=[pl.BlockSpec((tm, d), lambda i: (i, 0)), _full((1, d)), _resident(w4.shape)],
        out_specs=[wide, wide, wide, pl.BlockSpec((tm, d), lambda i: (i, 0))],
        out_shape=[jax.ShapeDtypeStruct((t, D_FF), BF16)] * 3 + [jax.ShapeDtypeStruct((t, d), BF16)],
        operands=(h, g, w4), comm=comm)


def _ple_fwd(h, p, wg, wp, name, comm=None):
    t, d = h.shape
    tm = _tile(t, ROW_TILE)

    def body(h_ref, p_ref, wg_ref, wp_ref, o_ref, q_ref, e_ref):
        hh = h_ref[...]
        q = _dot(hh, wg_ref[...])
        e = _dot(p_ref[...], wp_ref[...])
        q_ref[...] = q.astype(BF16)
        e_ref[...] = e.astype(BF16)
        o_ref[...] = hh + _sigmoid(q) * e

    row = pl.BlockSpec((tm, d), lambda i: (i, 0))
    return _pcall(
        body, name=name, grid=(t // tm,),
        in_specs=[row, pl.BlockSpec((tm, p.shape[1]), lambda i: (i, 0)), _resident(wg.shape), _resident(wp.shape)],
        out_specs=[row, row, row],
        out_shape=[jax.ShapeDtypeStruct((t, d), F32), jax.ShapeDtypeStruct((t, d), BF16),
                   jax.ShapeDtypeStruct((t, d), BF16)],
        operands=(h, p, wg, wp), comm=comm)


def _loss_head(y, target, name):
    t, d = y.shape
    tm = _tile(t, ROW_TILE)

    def body(y_ref, t_ref, dy_ref, l_ref):
        @pl.when(pl.program_id(0) == 0)
        def _():
            l_ref[...] = jnp.zeros_like(l_ref)

        err = y_ref[...] - t_ref[...]
        dy_ref[...] = err * (1.0 / d)
        l_ref[...] += jnp.sum(err * err, keepdims=True)[:, :1] * jnp.ones((1, 128), F32)

    row = pl.BlockSpec((tm, d), lambda i: (i, 0))
    return pl.pallas_call(
        body, name=name, grid=(t // tm,),
        in_specs=[row, row], out_specs=[row, _full((1, 128))],
        out_shape=[jax.ShapeDtypeStruct((t, d), F32), jax.ShapeDtypeStruct((1, 128), F32)],
        compiler_params=_params(1))(y, target)


def _ple_bwd(dh, q, e, wg, name):
    t, d = dh.shape
    tm = _tile(t, ROW_TILE)

    def body(dh_ref, q_ref, e_ref, wg_ref, dq_ref, de_ref, o_ref):
        dh_ = dh_ref[...]
        s = _sigmoid(q_ref[...].astype(F32))
        dq = (dh_ * e_ref[...].astype(F32) * s * (1.0 - s)).astype(BF16)
        dq_ref[...] = dq
        de_ref[...] = (dh_ * s).astype(BF16)
        o_ref[...] = dh_ + _dot_nt(dq, wg_ref[...])

    row = pl.BlockSpec((tm, d), lambda i: (i, 0))
    return pl.pallas_call(
        body, name=name, grid=(t // tm,),
        in_specs=[row, row, row, _resident(wg.shape)], out_specs=[row, row, row],
        out_shape=[jax.ShapeDtypeStruct((t, d), BF16), jax.ShapeDtypeStruct((t, d), BF16),
                   jax.ShapeDtypeStruct((t, d), F32)],
        compiler_params=_params(1))(dh, q, e, wg)


def _ffn_out_bwd(dh, f, g, fg, fu, w, name, comm=None):
    t, d = dh.shape
    tm = _tile(t, ROW_TILE_HEAVY)

    def body(dh_ref, f_ref, g_ref, fg_ref, fu_ref, w_ref, df_ref, dff_ref, dg_ref):
        @pl.when(pl.program_id(0) == 0)
        def _():
            dg_ref[...] = jnp.zeros_like(dg_ref)

        fn, r = _rms_stats(f_ref[...])
        df, dgt = _rms_bwd(fn, r, g_ref[...], dh_ref[...])
        dg_ref[...] += _rowsum(dgt)
        df = df.astype(BF16)
        df_ref[...] = df
        for c0 in range(0, D_FF, FF_CHUNK):
            cols = slice(c0, c0 + FF_CHUNK)
            dact = _dot_nt(df, w_ref[cols, :])
            fg_ = fg_ref[:, cols].astype(F32)
            s = _sigmoid(fg_)
            gs = fg_ * s
            dff_ref[:, cols] = (dact * fu_ref[:, cols].astype(F32) * (s + gs - gs * s)).astype(BF16)
            dff_ref[:, D_FF + c0:D_FF + c0 + FF_CHUNK] = (dact * gs).astype(BF16)

    row = pl.BlockSpec((tm, d), lambda i: (i, 0))
    wide = pl.BlockSpec((tm, D_FF), lambda i: (i, 0))
    return _pcall(
        body, name=name, grid=(t // tm,),
        in_specs=[row, row, _full(g.shape), wide, wide, _resident(w.shape)],
        out_specs=[row, pl.BlockSpec((tm, 2 * D_FF), lambda i: (i, 0)), _full((1, d))],
        out_shape=[jax.ShapeDtypeStruct((t, d), BF16), jax.ShapeDtypeStruct((t, 2 * D_FF), BF16),
                   jax.ShapeDtypeStruct((1, d), F32)],
        operands=(dh, f, g, fg, fu, w), comm=comm)


def _in_bwd(pieces, w4, unit, h, g, dres, tm, name, comm=None):
    t, d = h.shape
    tm = _tile(t, tm)
    per_chunk = w4.shape[2] // unit
    n_p = len(pieces)

    def body(*refs):
        p_refs = refs[:n_p]
        w_ref, h_ref, g_ref, r_ref, o_ref, dg_ref = refs[n_p:]

        @pl.when(pl.program_id(0) == 0)
        def _():
            dg_ref[...] = jnp.zeros_like(dg_ref)

        acc = None
        u = 0
        for p_ref, (_, nu) in zip(p_refs, pieces):
            for k in range(nu):
                lanes = slice((u % per_chunk) * unit, (u % per_chunk + 1) * unit)
                term = _dot_nt(p_ref[:, k * unit:(k + 1) * unit], w_ref[u // per_chunk, :, lanes])
                acc = term if acc is None else acc + term
                u += 1
        xn, r = _rms_stats(h_ref[...])
        dx, dgt = _rms_bwd(xn, r, g_ref[...], acc)
        dg_ref[...] += _rowsum(dgt)
        o_ref[...] = r_ref[...] + dx

    row = pl.BlockSpec((tm, d), lambda i: (i, 0))
    return _pcall(
        body, name=name, grid=(t // tm,),
        in_specs=[pl.BlockSpec((tm, a.shape[1]), lambda i: (i, 0)) for a, _ in pieces]
        + [_resident(w4.shape), row, _full((1, d)), row],
        out_specs=[row, _full((1, d))],
        out_shape=[jax.ShapeDtypeStruct((t, d), F32), jax.ShapeDtypeStruct((1, d), F32)],
        operands=(*[a for a, _ in pieces], w4, h, g, dres), comm=comm)


def _lane_block(n, cap):
    return max(b for b in range(128, min(n, cap) + 1, 128) if n % b == 0)


def _mm_tn(x, dy, name, bn=None):
    t, m = x.shape
    n = dy.shape[1]
    bm = _lane_block(m, 1408)
    bn = bn or _lane_block(n, 1408)
    tk = _tile(t, TOKEN_TILE)

    def body(x_ref, dy_ref, o_ref):
        @pl.when(pl.program_id(2) == 0)
        def _():
            o_ref[...] = jnp.zeros_like(o_ref)

        o_ref[...] += _dot_tn(x_ref[...], dy_ref[...])

    return pl.pallas_call(
        body, name=name, grid=(m // bm, n // bn, t // tk),
        in_specs=[pl.BlockSpec((tk, bm), lambda a, b, k: (k, a)), pl.BlockSpec((tk, bn), lambda a, b, k: (k, b))],
        out_specs=pl.BlockSpec((bm, bn), lambda a, b, k: (a, b)),
        out_shape=jax.ShapeDtypeStruct((m, n), F32),
        compiler_params=_params(3))(x, dy)


def _dw_cols(x, pieces, unit, per_chunk, name):
    t, m = x.shape
    tk = _tile(t, TOKEN_TILE)
    offs, total = [], 0
    for _, nu in pieces:
        offs.append(total)
        total += nu

    def body(x_ref, *refs):
        o_ref = refs[-1]
        u = pl.program_id(0)

        @pl.when(pl.program_id(1) == 0)
        def _():
            o_ref[...] = jnp.zeros_like(o_ref)

        for p_ref, off, (_, nu) in zip(refs[:-1], offs, pieces):
            @pl.when((u >= off) & (u < off + nu))
            def _(p_ref=p_ref):
                o_ref[...] += _dot_tn(x_ref[...], p_ref[...])

    def piece_spec(off, nu):
        def index(u, k):
            mine = (u >= off) & (u < off + nu)
            return jnp.where(mine, k, 0), jnp.clip(u - off, 0, nu - 1)
        return pl.BlockSpec((tk, unit), index)

    return pl.pallas_call(
        body, name=name, grid=(total, t // tk),
        in_specs=[pl.BlockSpec((tk, m), lambda u, k: (k, 0))] + [piece_spec(o, nu) for o, (_, nu) in zip(offs, pieces)],
        out_specs=pl.BlockSpec((None, None, m, unit), lambda u, k: (u // per_chunk, 0, 0, u % per_chunk)),
        out_shape=jax.ShapeDtypeStruct((N_CHIPS, 1, m, per_chunk * unit), F32),
        compiler_params=_params(2))(x, *[a for a, _ in pieces])


def _dw_slot(x, dy, buf, slot, n_slots, name):
    t, m = x.shape
    n = dy.shape[1]
    bn = _lane_block(n, 1024)
    tk = _tile(t, TOKEN_TILE)
    rows = m // N_CHIPS

    def body(x_ref, dy_ref, *rest):
        o_ref = rest[-1]

        @pl.when(pl.program_id(1) == 0)
        def _():
            o_ref[...] = jnp.zeros_like(o_ref)

        o_ref[...] += _dot_tn(x_ref[...], dy_ref[...]).reshape(N_CHIPS, rows, bn)

    in_specs = [pl.BlockSpec((tk, m), lambda b, k: (k, 0)), pl.BlockSpec((tk, bn), lambda b, k: (k, b))]
    operands = [x, dy]
    aliases = {}
    if buf is not None:
        in_specs.append(pl.BlockSpec(memory_space=pl.ANY))
        operands.append(buf)
        aliases = {2: 0}
    return pl.pallas_call(
        body, name=name, grid=(n // bn, t // tk), in_specs=in_specs,
        out_specs=pl.BlockSpec((N_CHIPS, None, rows, bn), lambda b, k: (0, slot, 0, b)),
        out_shape=jax.ShapeDtypeStruct((N_CHIPS, n_slots, rows, n), F32),
        input_output_aliases=aliases, compiler_params=_params(2))(*operands)


def _merge_bwd(dh, mo, g, proj, bra, brb, brc, w_out, wa, wb, wc, name, comm=None):
    t, d = dh.shape
    tm = _tile(t, ROW_TILE_HEAVY)

    def body(dh_ref, mo_ref, g_ref, za_ref, zb_ref, zc_ref, ba_ref, bb_ref, bc_ref, wo_ref, wa_ref, wb_ref, wc_ref,
             dmo_ref, dba_ref, dbb_ref, dbc_ref, dz_ref, dsg_ref, dcs_ref, dps_ref, dg_ref):
        @pl.when(pl.program_id(0) == 0)
        def _():
            dg_ref[...] = jnp.zeros_like(dg_ref)

        mon, r = _rms_stats(mo_ref[...])
        dmo, dgt = _rms_bwd(mon, r, g_ref[...], dh_ref[...])
        dg_ref[...] += _rowsum(dgt)
        dmo = dmo.astype(BF16)
        dmo_ref[...] = dmo
        dmerged = _dot_nt(dmo, wo_ref[...])
        branches = ((za_ref, ba_ref, wa_ref, dba_ref, dsg_ref), (zb_ref, bb_ref, wb_ref, dbb_ref, dcs_ref),
                    (zc_ref, bc_ref, wc_ref, dbc_ref, dps_ref))
        for j, (z_ref, b_ref, w_ref, db_ref, dx_ref) in enumerate(branches):
            gate = _sigmoid(z_ref[...].astype(F32))
            dbr = (dmerged * gate).astype(BF16)
            db_ref[...] = dbr
            dz_ref[:, j * d:(j + 1) * d] = (dmerged * b_ref[...].astype(F32) * gate * (1.0 - gate)).astype(BF16)
            dx_ref[...] = _dot_nt(dbr, w_ref[...]).astype(BF16)

    row = pl.BlockSpec((tm, d), lambda i: (i, 0))
    wspec = _resident((d, d))
    bf = jax.ShapeDtypeStruct((t, d), BF16)
    return _pcall(
        body, name=name, grid=(t // tm,),
        in_specs=[row, row, _full(g.shape), pl.BlockSpec((tm, d), lambda i: (i, 5)),
                  pl.BlockSpec((tm, d), lambda i: (i, 6)), pl.BlockSpec((tm, d), lambda i: (i, 7)),
                  row, row, row, wspec, wspec, wspec, wspec],
        out_specs=[row, row, row, row, pl.BlockSpec((tm, 3 * d), lambda i: (i, 0)), row, row, row, _full((1, d))],
        out_shape=[bf, bf, bf, bf, jax.ShapeDtypeStruct((t, 3 * d), BF16), bf, bf, bf,
                   jax.ShapeDtypeStruct((1, d), F32)],
        operands=(dh, mo, g, proj, proj, proj, bra, brb, brc, w_out, wa, wb, wc), comm=comm)


def _sgu_bwd(proj, dsg, wm, bs3, gv, bv, name, comm=None):
    t = proj.shape[0]
    d = D_MODEL
    tm = _tile(t, ROW_TILE_HEAVY)
    hd = d // SGU_HEADS

    def body(zu_ref, zv_ref, d_ref, wm_ref, bs_ref, gv_ref, bv_ref, dz_ref, dwm_ref, dbs_ref, dgv_ref, dbv_ref,
             dvn_ref):
        @pl.when(pl.program_id(0) == 0)
        def _():
            dwm_ref[...] = jnp.zeros_like(dwm_ref)
            dbs_ref[...] = jnp.zeros_like(dbs_ref)
            dgv_ref[...] = jnp.zeros_like(dgv_ref)
            dbv_ref[...] = jnp.zeros_like(dbv_ref)

        mask = _sgu_mask()
        for blk in range(tm // SGU_BLOCK):
            rows = pl.ds(blk * SGU_BLOCK, SGU_BLOCK)
            u, du_dz = _gelu_and_grad(zu_ref[rows, :].astype(F32))
            v0, dv_dz = _gelu_and_grad(zv_ref[rows, :].astype(F32))
            xhat, rstd = _ln_stats(v0)
            vn = (xhat * gv_ref[...] + bv_ref[...]).astype(BF16)
            dsg = d_ref[rows, :].astype(F32)
            dmix = (dsg * u).astype(BF16)
            for hh in range(SGU_HEADS):
                cols = slice(hh * hd, (hh + 1) * hd)
                wmh = jnp.where(mask, wm_ref[hh], 0.0).astype(BF16)
                vb = vn[:, cols]
                mixed = jnp.dot(wmh, vb, preferred_element_type=F32) + bs_ref[hh]
                dz_ref[rows, cols] = (dsg[:, cols] * mixed * du_dz[:, cols]).astype(BF16)
                dmh = dmix[:, cols]
                dwm_ref[hh] += jnp.where(mask, _dot_nt(dmh, vb), 0.0)
                dbs_ref[hh] += jnp.sum(dmh.astype(F32), axis=1, keepdims=True)
                dvn_ref[:, cols] = _dot_tn(wmh, dmh)
            dvn = dvn_ref[...]
            dgv_ref[...] += _rowsum(dvn * xhat)
            dbv_ref[...] += _rowsum(dvn)
            dz_ref[rows, d:2 * d] = (_ln_bwd(xhat, rstd, gv_ref[...], dvn) * dv_dz).astype(BF16)

    return _pcall(
        body, name=name, grid=(t // tm,),
        in_specs=[pl.BlockSpec((tm, d), lambda i: (i, 0)), pl.BlockSpec((tm, d), lambda i: (i, 1)),
                  pl.BlockSpec((tm, d), lambda i: (i, 0)), _full(wm.shape), _full(bs3.shape), _full(gv.shape),
                  _full(bv.shape)],
        out_specs=[pl.BlockSpec((tm, 2 * d), lambda i: (i, 0)), _full(wm.shape), _full(bs3.shape), _full((1, d)),
                   _full((1, d))],
        out_shape=[jax.ShapeDtypeStruct((t, 2 * d), BF16), jax.ShapeDtypeStruct(wm.shape, F32),
                   jax.ShapeDtypeStruct(bs3.shape, F32), jax.ShapeDtypeStruct((1, d), F32),
                   jax.ShapeDtypeStruct((1, d), F32)],
        scratch_shapes=[pltpu.VMEM((SGU_BLOCK, d), F32)],
        operands=(proj, proj, dsg, wm, bs3, gv, bv), comm=comm)


def _conv_bwd_norm(proj, dcs, cv, bdw, gln, bln, name, comm=None):
    t = proj.shape[0]
    d = D_MODEL
    tm = _tile(t, ROW_TILE)
    main, halo = _conv_specs(t, tm, d)
    n_win = CONV_ROWS + CONV_HALO

    def body(a_ref, g_ref, ah_ref, gh_ref, dcs_ref, cv_ref, b_ref, gl_ref, bl_ref,
             dcv_ref, dw_ref, db_ref, dgl_ref, dbl_ref, scr_ref, dwacc_ref):
        @pl.when(pl.program_id(0) == 0)
        def _():
            dwacc_ref[...] = jnp.zeros_like(dwacc_ref)
            db_ref[...] = jnp.zeros_like(db_ref)
            dgl_ref[...] = jnp.zeros_like(dgl_ref)
            dbl_ref[...] = jnp.zeros_like(dbl_ref)

        _fill_glu_history(scr_ref, a_ref, g_ref, ah_ref, gh_ref, tm)
        xhat, rstd = _ln_stats(cv_ref[...] + b_ref[...])
        cn = xhat * gl_ref[...] + bl_ref[...]
        s = _sigmoid(cn)
        dcn = dcs_ref[...].astype(F32) * (s * (1.0 + cn * (1.0 - s)))
        dgl_ref[...] += _rowsum(dcn * xhat)
        dbl_ref[...] += _rowsum(dcn)
        dcv = _ln_bwd(xhat, rstd, gl_ref[...], dcn)
        db_ref[...] += _rowsum(dcv)
        dcv_ref[...] = dcv

        def chunk(ci, carry):
            r0 = pl.multiple_of(ci * CONV_ROWS, CONV_ROWS)
            for c0 in range(0, d, CONV_LANES):
                lanes = pl.ds(c0, CONV_LANES)
                win = scr_ref[pl.ds(r0, n_win), lanes]
                dchunk = dcv_ref[pl.ds(r0, CONV_ROWS), lanes]
                for r in range(8):
                    rolled = win if r == 0 else pltpu.roll(win, n_win - r, 0)
                    for q in range(n_win // 8):
                        k = 8 * q + r - _CONV_BASE
                        if 0 <= k < CONV_WIDTH and 8 * q + CONV_ROWS <= n_win:
                            prod = dchunk * rolled[8 * q:8 * q + CONV_ROWS]
                            part = prod[0:8]
                            for s8 in range(8, CONV_ROWS, 8):
                                part = part + prod[s8:s8 + 8]
                            dwacc_ref[pl.ds(8 * k, 8), lanes] += part
            return carry

        lax.fori_loop(0, tm // CONV_ROWS, chunk, 0)

        @pl.when(pl.program_id(0) == pl.num_programs(0) - 1)
        def _():
            dw_ref[...] = jnp.sum(dwacc_ref[...].reshape(CONV_HALO, 8, d), axis=1)

    row = pl.BlockSpec((tm, d), lambda i: (i, 0))
    vec = _full((1, d))
    return _pcall(
        body, name=name, grid=(t // tm,),
        in_specs=main + halo + [row, row, vec, vec, vec],
        out_specs=[row, _full((CONV_HALO, d)), vec, vec, vec],
        out_shape=[jax.ShapeDtypeStruct((t, d), F32), jax.ShapeDtypeStruct((CONV_HALO, d), F32)]
        + [jax.ShapeDtypeStruct((1, d), F32)] * 3,
        scratch_shapes=[pltpu.VMEM((tm + CONV_HALO, d), F32), pltpu.VMEM((8 * CONV_HALO, d), F32)],
        operands=(proj, proj, proj, proj, dcs, cv, bdw, gln, bln), comm=comm)


def _conv_bwd_taps(proj, dcv, wdw, name, comm=None):
    t = proj.shape[0]
    d = D_MODEL
    tm = _tile(t, ROW_TILE)
    hb = tm // CONV_HALO
    last_halo = t // CONV_HALO - 1

    def body(a_ref, g_ref, dcv_ref, dnext_ref, w_ref, dz_ref, scr_ref, dh_ref):
        scr_ref[0:tm, :] = dcv_ref[...]
        is_last = pl.program_id(0) == pl.num_programs(0) - 1
        scr_ref[tm:tm + CONV_HALO, :] = jnp.where(is_last, 0.0, dnext_ref[...])
        _conv_into(scr_ref, dh_ref, w_ref, tm, 0, True)
        dglu = dh_ref[...]
        a = a_ref[...].astype(F32)
        s = _sigmoid(g_ref[...].astype(F32))
        dz_ref[:, 0:d] = (dglu * s).astype(BF16)
        dz_ref[:, d:2 * d] = (dglu * a * s * (1.0 - s)).astype(BF16)

    return _pcall(
        body, name=name, grid=(t // tm,),
        in_specs=[pl.BlockSpec((tm, d), lambda i: (i, 2)), pl.BlockSpec((tm, d), lambda i: (i, 3)),
                  pl.BlockSpec((tm, d), lambda i: (i, 0)),
                  pl.BlockSpec((CONV_HALO, d), lambda i: (jnp.minimum((i + 1) * hb, last_halo), 0)),
                  _full(wdw.shape)],
        out_specs=pl.BlockSpec((tm, 2 * d), lambda i: (i, 0)),
        out_shape=jax.ShapeDtypeStruct((t, 2 * d), BF16),
        scratch_shapes=[pltpu.VMEM((tm + CONV_HALO, d), F32), pltpu.VMEM((tm, d), F32)],
        operands=(proj, proj, dcv, dcv, wdw), comm=comm)


def _pool_bwd(proj, dps, wpool, spool, name):
    t = proj.shape[0]
    d = D_MODEL
    tm = _tile(t, ROW_TILE)
    hb = tm // POOL_HALO
    last_halo = t // POOL_HALO - 1
    ext = tm + POOL_HALO

    def body(z_ref, zh_ref, d_ref, dnext_ref, w_ref, s_ref, dz_ref, dw_ref, ds_ref, scr_ref, dext_ref, dq_ref):
        @pl.when(pl.program_id(0) == 0)
        def _():
            dw_ref[...] = jnp.zeros_like(dw_ref)
            ds_ref[...] = jnp.zeros_like(ds_ref)

        _pool_fill(scr_ref, z_ref, zh_ref, tm)
        t0 = pl.program_id(0) * tm
        is_last = pl.program_id(0) == pl.num_programs(0) - 1
        dext_ref[0:tm, :] = d_ref[...].astype(F32)
        dext_ref[tm:ext, :] = jnp.where(is_last, 0.0, dnext_ref[...].astype(F32))
        for gi, w in enumerate(POOL_WINDOWS):
            cols = slice(gi * POOL_GROUP, (gi + 1) * POOL_GROUP)
            dps_ext = dext_ref[:, cols]
            dpm_ext = (dps_ext * s_ref[:, cols]).astype(BF16)
            dpooled_ext = _dot_nt(dpm_ext, w_ref[gi])
            dq_ref[...] = dpooled_ext / _pool_count(t0, ext, w)
            acc = dq_ref[pl.ds(0, tm), :]
            for k in range(1, w):
                acc = acc + dq_ref[pl.ds(k, tm), :]
            dz_ref[:, cols] = (acc - dpooled_ext[0:tm]).astype(BF16)
            pooled = _pooled_group(scr_ref, gi, w, tm, t0).astype(BF16)
            pm = jnp.dot(pooled, w_ref[gi], preferred_element_type=F32)
            ds_ref[:, cols] += _rowsum(dps_ext[0:tm] * pm)
            dw_ref[gi] += _dot_tn(pooled, dpm_ext[0:tm])

    return pl.pallas_call(
        body, name=name, grid=(t // tm,),
        in_specs=_pool_specs(tm, d) + [pl.BlockSpec((tm, d), lambda i: (i, 0)),
                                       pl.BlockSpec((POOL_HALO, d), lambda i: (jnp.minimum((i + 1) * hb, last_halo), 0)),
                                       _full(wpool.shape), _full(spool.shape)],
        out_specs=[pl.BlockSpec((tm, d), lambda i: (i, 0)), _full(wpool.shape), _full((1, d))],
        out_shape=[jax.ShapeDtypeStruct((t, d), BF16), jax.ShapeDtypeStruct(wpool.shape, F32),
                   jax.ShapeDtypeStruct((1, d), F32)],
        scratch_shapes=[pltpu.VMEM((tm + POOL_HALO, d), F32), pltpu.VMEM((ext, d), F32),
                        pltpu.VMEM((ext, POOL_GROUP), F32)],
        compiler_params=_params(1))(proj, proj, dps, dps, wpool, spool)


ANY = pl.BlockSpec(memory_space=pl.ANY)


def _mesh_pos():
    x, y, c = lax.axis_index("x"), lax.axis_index("y"), lax.axis_index("c")
    chips = [(1 - x, y), (x, 1 - y), (1 - x, 1 - y)]
    return x, y, c, chips


def _chip_of(xy):
    return 2 * xy[0] + xy[1]


def _half_view(a):
    return a.reshape(a.shape[:-2] + (2, a.shape[-2] // 2, a.shape[-1]))


def _same(arrs):
    return [jax.ShapeDtypeStruct(a.shape, a.dtype) for a in arrs]


def _in_place(n):
    return {g: g for g in range(n)}


def _sems(count):
    return [pltpu.SemaphoreType.DMA((count,)), pltpu.SemaphoreType.DMA((count,))]


def _gather_ici(bufs):
    n = len(bufs)

    def copy(buf, sems, g, j, chip):
        x, y, c, chips = _mesh_pos()
        slab = buf[g].at[chip, :, c]
        return pltpu.make_async_remote_copy(
            src_ref=slab, dst_ref=slab, send_sem=sems[0].at[3 * g + j], recv_sem=sems[1].at[3 * g + j],
            device_id=(*chips[j], c), device_id_type=MESH)

    def start(ins, buf, sems):
        x, y, c, chips = _mesh_pos()
        for g in range(n):
            for j in range(3):
                copy(buf, sems, g, j, 2 * x + y).start()

    def finish(ins, buf, sems):
        x, y, c, chips = _mesh_pos()
        for g in range(n):
            for j in range(3):
                copy(buf, sems, g, j, _chip_of(chips[j])).wait_recv()
        for g in range(n):
            for j in range(3):
                copy(buf, sems, g, j, 2 * x + y).wait_send()

    return _Payload(bufs, _same(bufs), _in_place(n), _sems(3 * n), start, finish)


def _gather_d2d(bufs):
    n = len(bufs)

    def copy(buf, sems, g, j, half):
        x, y, c, chips = _mesh_pos()
        slab = buf[g].at[_chip_of(chips[j]), :, half]
        return pltpu.make_async_remote_copy(
            src_ref=slab, dst_ref=slab, send_sem=sems[0].at[3 * g + j], recv_sem=sems[1].at[3 * g + j],
            device_id=(x, y, 1 - c), device_id_type=MESH)

    def start(ins, buf, sems):
        c = lax.axis_index("c")
        for g in range(n):
            for j in range(3):
                copy(buf, sems, g, j, c).start()

    def finish(ins, buf, sems):
        c = lax.axis_index("c")
        for g in range(n):
            for j in range(3):
                copy(buf, sems, g, j, 1 - c).wait_recv()
        for g in range(n):
            for j in range(3):
                copy(buf, sems, g, j, c).wait_send()

    return _Payload(bufs, _same(bufs), _in_place(n), _sems(3 * n), start, finish)


def _pair_exchange(grads):
    n = len(grads)

    def copy(src, dst, sems, g):
        x, y, c, _ = _mesh_pos()
        return pltpu.make_async_remote_copy(
            src_ref=src[g].at[:, :, 1 - c], dst_ref=dst[g], send_sem=sems[0].at[g], recv_sem=sems[1].at[g],
            device_id=(x, y, 1 - c), device_id_type=MESH)

    def start(src, dst, sems):
        for g in range(n):
            copy(src, dst, sems, g).start()

    def finish(src, dst, sems):
        for g in range(n):
            copy(src, dst, sems, g).wait()

    out_shape = [jax.ShapeDtypeStruct(g.shape[:2] + g.shape[3:], g.dtype) for g in grads]
    return _Payload(grads, out_shape, {}, _sems(n), start, finish)


def _chip_exchange(parts):
    n = len(parts)

    def copy(src, dst, sems, g, j, slot):
        x, y, c, chips = _mesh_pos()
        return pltpu.make_async_remote_copy(
            src_ref=src[g].at[_chip_of(chips[j])], dst_ref=dst[g].at[slot], send_sem=sems[0].at[3 * g + j],
            recv_sem=sems[1].at[3 * g + j], device_id=(*chips[j], c), device_id_type=MESH)

    def start(src, dst, sems):
        x, y, c, chips = _mesh_pos()
        for g in range(n):
            for j in range(3):
                copy(src, dst, sems, g, j, 2 * x + y).start()

    def finish(src, dst, sems):
        x, y, c, chips = _mesh_pos()
        for g in range(n):
            for j in range(3):
                copy(src, dst, sems, g, j, _chip_of(chips[j])).wait_recv()
        for g in range(n):
            for j in range(3):
                copy(src, dst, sems, g, j, 2 * x + y).wait_send()

    return _Payload(parts, _same(parts), {}, _sems(3 * n), start, finish)


def _pair_share(bufs):
    n = len(bufs)

    def copy(buf, sems, g, half):
        x, y, c, _ = _mesh_pos()
        slab = buf[g].at[:, :, half]
        return pltpu.make_async_remote_copy(
            src_ref=slab, dst_ref=slab, send_sem=sems[0].at[g], recv_sem=sems[1].at[g],
            device_id=(x, y, 1 - c), device_id_type=MESH)

    def start(ins, buf, sems):
        c = lax.axis_index("c")
        for g in range(n):
            copy(buf, sems, g, c).start()

    def finish(ins, buf, sems):
        c = lax.axis_index("c")
        for g in range(n):
            copy(buf, sems, g, 1 - c).wait_recv()
        for g in range(n):
            copy(buf, sems, g, c).wait_send()

    return _Payload(bufs, _same(bufs), _in_place(n), _sems(n), start, finish)


def _join(a, b):
    if a is None or b is None:
        return a or b
    na, ma = len(a.operands), len(a.out_shape)
    aliases = dict(a.aliases)
    aliases.update({na + i: ma + o for i, o in b.aliases.items()})
    ka = len(a.scratch)

    def start(ins, outs, sems):
        a.start(ins[:na], outs[:ma], sems[:ka])
        b.start(ins[na:], outs[ma:], sems[ka:])

    def finish(ins, outs, sems):
        a.finish(ins[:na], outs[:ma], sems[:ka])
        b.finish(ins[na:], outs[ma:], sems[ka:])

    joined = _Payload(a.operands + b.operands, a.out_shape + b.out_shape, aliases, list(a.scratch) + list(b.scratch),
                      start, finish)
    joined.parts = (a, b, ma)
    return joined


def _small_exchange(vec):
    def copy(src, dst, sems, k, slot):
        x, y, c, _ = _mesh_pos()
        peer = (x ^ (k >> 2), y ^ ((k >> 1) & 1), c ^ (k & 1))
        return pltpu.make_async_remote_copy(
            src_ref=src[0], dst_ref=dst[0].at[slot], send_sem=sems[0].at[k - 1], recv_sem=sems[1].at[k - 1],
            device_id=peer, device_id_type=MESH)

    def me():
        x, y, c, _ = _mesh_pos()
        return 4 * x + 2 * y + c

    def start(src, dst, sems):
        for k in range(1, 8):
            copy(src, dst, sems, k, me()).start()

    def finish(src, dst, sems):
        for k in range(1, 8):
            copy(src, dst, sems, k, me() ^ k).wait_recv()
        for k in range(1, 8):
            copy(src, dst, sems, k, me()).wait_send()

    return _Payload([vec], [jax.ShapeDtypeStruct((8,) + vec.shape, vec.dtype)], {}, _sems(7), start, finish)


def _small_sum(vec, landed, pos, name):
    r = vec.shape[0]

    def body(pos_ref, v_ref, l_ref, o_ref):
        k = pl.program_id(0)

        @pl.when(k == 0)
        def _():
            o_ref[...] = jnp.zeros_like(o_ref)

        @pl.when(k == pos_ref[POS_DEVICE])
        def _():
            o_ref[...] += v_ref[...]

        @pl.when(k != pos_ref[POS_DEVICE])
        def _():
            o_ref[...] += l_ref[...]

    def landed_index(k, pos_ref):
        me = pos_ref[POS_DEVICE]
        return jnp.where(k == me, (me + 1) % 8, k), 0, 0

    return pl.pallas_call(
        body, name=name,
        grid_spec=pltpu.PrefetchScalarGridSpec(
            num_scalar_prefetch=1, grid=(8,),
            in_specs=[pl.BlockSpec((r, 128), lambda k, pos_ref: (0, 0)), pl.BlockSpec((None, r, 128), landed_index)],
            out_specs=pl.BlockSpec((r, 128), lambda k, pos_ref: (0, 0))),
        out_shape=jax.ShapeDtypeStruct(vec.shape, F32),
        compiler_params=_params(1))(pos, vec, landed)


def _all_reduce_small(vec, name):
    r = vec.shape[0]

    def body(v_ref, o_ref, gath_ref, send_sem, recv_sem):
        x, y, c, _ = _mesh_pos()
        me = 4 * x + 2 * y + c
        gath_ref[me] = v_ref[...]
        copies = []
        for k in range(1, 8):
            peer = (x ^ (k >> 2), y ^ ((k >> 1) & 1), c ^ (k & 1))
            cp = pltpu.make_async_remote_copy(
                src_ref=v_ref, dst_ref=gath_ref.at[me], send_sem=send_sem.at[k - 1], recv_sem=recv_sem.at[k - 1],
                device_id=peer, device_id_type=MESH)
            cp.start()
            copies.append(cp)
        for k in range(1, 8):
            src_id = me ^ k
            pltpu.make_async_remote_copy(
                src_ref=v_ref, dst_ref=gath_ref.at[src_id], send_sem=send_sem.at[k - 1], recv_sem=recv_sem.at[k - 1],
                device_id=(x, y, c), device_id_type=MESH).wait_recv()
        for cp in copies:
            cp.wait_send()
        acc = gath_ref[0]
        for k in range(1, 8):
            acc = acc + gath_ref[k]
        o_ref[...] = acc

    return pl.pallas_call(
        body, name=name,
        in_specs=[pl.BlockSpec(memory_space=pltpu.VMEM)], out_specs=pl.BlockSpec(memory_space=pltpu.VMEM),
        out_shape=jax.ShapeDtypeStruct(vec.shape, F32),
        scratch_shapes=[pltpu.VMEM((8, r, 128), F32), pltpu.SemaphoreType.DMA((7,)), pltpu.SemaphoreType.DMA((7,))],
        compiler_params=pltpu.CompilerParams(has_side_effects=True, vmem_limit_bytes=VMEM_LIMIT))(vec)


def _row_block(rows, cols, mult=16):
    best = None
    for cand in range(mult, rows + 1, mult):
        if rows % cand == 0 and cand * cols * 4 <= EW_BLOCK_BYTES:
            best = cand
    return best or rows


POS_ME, POS_CORE, POS_DEVICE = 0, 4, 5


def _place(arrs, li, pos, dtype, name):
    s = len(arrs)
    _, rows, cols = arrs[0].shape
    rh = rows // 2
    tr = _row_block(rh, cols)
    nb = rh // tr

    def body(pos_ref, *refs):
        o_ref = refs[s]
        for j in range(s):
            @pl.when(pl.program_id(0) == j)
            def _(j=j):
                o_ref[...] = refs[j][...].astype(dtype)

    def in_spec(j):
        return pl.BlockSpec((None, tr, cols), lambda b, hf, i, pos_ref: (li, jnp.where(b == j, hf * nb + i, 0), 0))

    return pl.pallas_call(
        body, name=name,
        grid_spec=pltpu.PrefetchScalarGridSpec(
            num_scalar_prefetch=1, grid=(s, 2, nb), in_specs=[in_spec(j) for j in range(s)],
            out_specs=pl.BlockSpec((None, None, None, tr, cols),
                                   lambda b, hf, i, pos_ref: (pos_ref[POS_ME], b, hf, i, 0))),
        out_shape=jax.ShapeDtypeStruct((N_CHIPS, s, 2, rh, cols), dtype),
        compiler_params=_params(3))(pos, *arrs)


def _pair_sum(grad, recv, pos, out_dtype, name):
    _, s, rh, cols = recv.shape
    tr = _row_block(rh, cols)

    def body(pos_ref, g_ref, r_ref, o_ref):
        o_ref[...] = (g_ref[...] + r_ref[...]).astype(out_dtype)

    blk = (None, None, tr, cols)
    return pl.pallas_call(
        body, name=name,
        grid_spec=pltpu.PrefetchScalarGridSpec(
            num_scalar_prefetch=1, grid=(N_CHIPS, s, rh // tr),
            in_specs=[pl.BlockSpec((None, None, None, tr, cols),
                                   lambda a, b, i, pos_ref: (a, b, pos_ref[POS_CORE], i, 0)),
                      pl.BlockSpec(blk, lambda a, b, i, pos_ref: (a, b, i, 0))],
            out_specs=pl.BlockSpec(blk, lambda a, b, i, pos_ref: (a, b, i, 0))),
        out_shape=jax.ShapeDtypeStruct(recv.shape, out_dtype),
        compiler_params=_params(3))(pos, grad, recv)


def _chip_sum(part, landed, gbuf, li, n_layers, pos, name):
    _, s, rh, cols = part.shape
    tr = _row_block(rh, cols)

    def body(pos_ref, p_ref, a_ref, b_ref, c_ref, *rest):
        o_ref = rest[-1]
        o_ref[...] = ((p_ref[...].astype(F32) + a_ref[...].astype(F32)) + b_ref[...].astype(F32)) \
            + c_ref[...].astype(F32)

    def slot(k):
        return pl.BlockSpec((None, None, tr, cols), lambda b, i, pos_ref: (pos_ref[k], b, i, 0))

    in_specs = [slot(0), slot(1), slot(2), slot(3)]
    operands = [pos, part, landed, landed, landed]
    aliases = {}
    if gbuf is not None:
        in_specs.append(ANY)
        operands.append(gbuf)
        aliases = {len(operands) - 1: 0}
    return pl.pallas_call(
        body, name=name,
        grid_spec=pltpu.PrefetchScalarGridSpec(
            num_scalar_prefetch=1, grid=(s, rh // tr), in_specs=in_specs,
            out_specs=pl.BlockSpec((None, None, None, tr, cols),
                                   lambda b, i, pos_ref: (li, b, pos_ref[POS_CORE], i, 0))),
        out_shape=jax.ShapeDtypeStruct((n_layers, s, 2, rh, cols), F32),
        input_output_aliases=aliases,
        compiler_params=_params(2))(*operands)


def _adamw_math(w, g, m, v):
    m = ADAM_B1 * m + (1.0 - ADAM_B1) * g
    v = ADAM_B2 * v + (1.0 - ADAM_B2) * (g * g)
    m_hat = m / (1.0 - ADAM_B1 ** ADAM_STEP)
    v_hat = v / (1.0 - ADAM_B2 ** ADAM_STEP)
    delta = -ADAM_LR * (m_hat / (jnp.sqrt(v_hat) + ADAM_EPS) + ADAM_WD * w)
    return delta, m, v


def _adamw(w, g, slot, m, v, name):
    l, rows, cols = w.shape
    tr = _row_block(rows, cols, 8)

    def body(w_ref, g_ref, m_ref, v_ref, go_ref, d_ref, mo_ref, vo_ref):
        g_ = g_ref[...]
        delta, m_, v_ = _adamw_math(w_ref[...], g_, m_ref[...], v_ref[...])
        go_ref[...] = g_
        d_ref[...] = delta
        mo_ref[...] = m_
        vo_ref[...] = v_

    blk = pl.BlockSpec((None, tr, cols), lambda a, i: (a, i, 0))
    gblk = pl.BlockSpec((None, None, tr, cols), lambda a, i: (a, slot, i, 0))
    return pl.pallas_call(
        body, name=name, grid=(l, rows // tr), in_specs=[blk, gblk, blk, blk], out_specs=[blk] * 4,
        out_shape=[jax.ShapeDtypeStruct(w.shape, F32)] * 4,
        compiler_params=_params(2))(w, g, m, v)


SQ = ("w_sgu_out", "w_conv_out", "w_pool_out", "w_out", "w_ple_gate")
SMALL = ("g_mix_pre", "w_sgu_s", "b_sgu_s", "g_sgu_v", "b_sgu_v", "b_dw", "g_conv_ln", "b_conv_ln", "s_pool",
         "g_mix_post", "g_ffn_pre", "g_ffn_post")


WHERE = {"w_in": ("in", 0), "w_ffn_in": ("ffn_in", 0), "w_ffn_out": ("ffn_out", 0), "w_ple": ("mix", 0),
         "w_pool": ("mix", 1), "w_dw": ("dw", 0)}
WHERE.update({nm: ("sq", slot) for slot, nm in enumerate(SQ)})


class _LayerWeights:
    def __init__(self, fetch, small, li):
        self.fetch, self.small, self.li, self.cache = fetch, small, li, {}

    def __getitem__(self, nm):
        if nm not in self.cache:
            self.cache[nm] = self._big(nm) if nm in WHERE else self.small[nm][self.li]
        return self.cache[nm]

    def _big(self, nm):
        group, slot = WHERE[nm]
        g = self.fetch(group)
        g = g.reshape(g.shape[:2] + (-1, g.shape[-1]))
        if nm in ("w_in", "w_ffn_in"):
            return g.reshape(N_CHIPS, D_MODEL, -1)
        if nm == "w_ffn_out":
            return g.reshape(D_FF, D_MODEL)
        if nm in SQ:
            return g[:, slot].reshape(D_MODEL, D_MODEL)
        if nm == "w_ple":
            return g[:, slot].transpose(1, 0, 2).reshape(256, D_MODEL)
        if nm == "w_pool":
            return g[:, slot].reshape(N_CHIPS, 4, 64, 256).transpose(1, 0, 2, 3).reshape(4, 256, 256)
        return g.reshape(N_CHIPS, CONV_HALO, -1).transpose(1, 0, 2).reshape(CONV_HALO, D_MODEL)


def _vec(a):
    return a.reshape(1, -1)


def _layer_fwd(h, p, w, li, hosts=None):
    s = {}
    tag = "_l%d" % li
    s["h0"] = h
    proj, hn = _norm_mm(h, _vec(w["g_mix_pre"]), w["w_in"], "mix_in" + tag, _take(hosts, "mix_in"))
    s["proj"], s["hn"] = proj, hn
    bs3 = w["b_sgu_s"].reshape(SGU_HEADS, SGU_BLOCK, 1)
    s["sg"] = _sgu_fwd(proj, w["w_sgu_s"], bs3, _vec(w["g_sgu_v"]), _vec(w["b_sgu_v"]), "sgu_fwd" + tag,
                       _take(hosts, "sgu_fwd"))
    s["cs"], s["cv"] = _conv_fwd(proj, w["w_dw"], _vec(w["b_dw"]), _vec(w["g_conv_ln"]), _vec(w["b_conv_ln"]),
                                 "conv_fwd" + tag, _take(hosts, "conv_fwd"))
    s["ps"] = _pool_fwd(proj, w["w_pool"], _vec(w["s_pool"]), "pool_fwd" + tag, _take(hosts, "pool_fwd"))
    s["bra"], s["brb"], s["brc"], s["merged"] = _merge_fwd(
        proj, s["sg"], s["cs"], s["ps"], w["w_sgu_out"], w["w_conv_out"], w["w_pool_out"], "merge_fwd" + tag,
        _take(hosts, "merge_fwd"))
    s["mo"], h1 = _mm_norm_res(s["merged"], w["w_out"], _vec(w["g_mix_post"]), h, "mix_out" + tag,
                               _take(hosts, "mix_out"))
    s["h1"] = h1
    s["fg"], s["fu"], s["act"], s["hn2"] = _ffn_in(h1, _vec(w["g_ffn_pre"]), w["w_ffn_in"], "ffn_in" + tag,
                                                   _take(hosts, "ffn_in"))
    s["f"], h2 = _mm_norm_res(s["act"], w["w_ffn_out"], _vec(w["g_ffn_post"]), h1, "ffn_out" + tag,
                              _take(hosts, "ffn_out"))
    s["h2"] = h2
    h3, s["q"], s["e"] = _ple_fwd(h2, p, w["w_ple_gate"], w["w_ple"], "ple_fwd" + tag, _take(hosts, "ple_fwd"))
    return h3, s


def _layer_bwd(dh3, p, w, s, li, hosts=None, big=None, gs=None):
    tag = "_l%d" % li
    d = D_MODEL
    gs = {} if gs is None else gs
    big = {} if big is None else big
    dq, de, dh2 = _ple_bwd(dh3, s["q"], s["e"], w["w_ple_gate"], "ple_bwd" + tag)
    dw_ple = _mm_tn(p, de, "dw_ple" + tag)
    sq = _dw_slot(s["h2"], dq, None, SQ.index("w_ple_gate"), len(SQ), "dw_ple_gate" + tag)
    df, dff, gs["g_ffn_post"] = _ffn_out_bwd(dh2, s["f"], _vec(w["g_ffn_post"]), s["fg"], s["fu"], w["w_ffn_out"],
                                             "ffn_out_bwd" + tag, _take(hosts, "ffn_out_bwd"))
    dw_ffn_out = _mm_tn(s["act"], df, "dw_ffn_out" + tag)
    big["ffn_out"] = dw_ffn_out.reshape(N_CHIPS, 1, D_FF // N_CHIPS, d)
    n_ff = w["w_ffn_in"].shape[2]
    dh1, gs["g_ffn_pre"] = _in_bwd([(dff, 2 * D_FF // n_ff)], w["w_ffn_in"], n_ff, s["h1"], _vec(w["g_ffn_pre"]),
                                   dh2, ROW_TILE, "ffn_in_bwd" + tag, _take(hosts, "ffn_in_bwd"))
    big["ffn_in"] = _dw_cols(s["hn2"], [(dff, 2 * D_FF // n_ff)], n_ff, 1, "dw_ffn_in" + tag)
    (dmo, dbra, dbrb, dbrc, dzg, dsg, dcs, dps, gs["g_mix_post"]) = _merge_bwd(
        dh1, s["mo"], _vec(w["g_mix_post"]), s["proj"], s["bra"], s["brb"], s["brc"], w["w_out"], w["w_sgu_out"],
        w["w_conv_out"], w["w_pool_out"], "merge_bwd" + tag, _take(hosts, "merge_bwd"))
    for nm, x_, dy_ in (("w_out", s["merged"], dmo), ("w_sgu_out", s["sg"], dbra), ("w_conv_out", s["cs"], dbrb),
                        ("w_pool_out", s["ps"], dbrc)):
        sq = _dw_slot(x_, dy_, sq, SQ.index(nm), len(SQ), "d" + nm + tag)
    big["sq"] = sq
    bs3 = w["b_sgu_s"].reshape(SGU_HEADS, SGU_BLOCK, 1)
    dz_sgu, gs["w_sgu_s"], dbs3, gs["g_sgu_v"], gs["b_sgu_v"] = _sgu_bwd(
        s["proj"], dsg, w["w_sgu_s"], bs3, _vec(w["g_sgu_v"]), _vec(w["b_sgu_v"]), "sgu_bwd" + tag,
        _take(hosts, "sgu_bwd"))
    gs["b_sgu_s"] = dbs3
    dcv, dwdw, gs["b_dw"], gs["g_conv_ln"], gs["b_conv_ln"] = _conv_bwd_norm(
        s["proj"], dcs, s["cv"], _vec(w["b_dw"]), _vec(w["g_conv_ln"]), _vec(w["b_conv_ln"]), "conv_bwd_norm" + tag,
        _take(hosts, "conv_bwd_norm"))
    dz_conv = _conv_bwd_taps(s["proj"], dcv, w["w_dw"], "conv_bwd_taps" + tag, _take(hosts, "conv_bwd_taps"))
    dz_pool, dwpool, gs["s_pool"] = _pool_bwd(s["proj"], dps, w["w_pool"], _vec(w["s_pool"]), "pool_bwd" + tag)
    pieces = [(dz_sgu, 2), (dz_conv, 2), (dz_pool, 1), (dzg, 3)]
    big["in"] = _dw_cols(s["hn"], pieces, d, 2, "dw_in" + tag)
    gple = dw_ple.reshape(256, N_CHIPS, 256).transpose(1, 0, 2)
    gpool = dwpool.reshape(4, N_CHIPS, 64, 256).transpose(1, 0, 2, 3).reshape(N_CHIPS, 256, 256)
    big["mix"] = jnp.stack([gple, gpool], axis=1)
    big["dw"] = dwdw.reshape(CONV_HALO, N_CHIPS, 256).transpose(1, 0, 2)[:, None]
    dh0, gs["g_mix_pre"] = _in_bwd(pieces, w["w_in"], d, s["h0"], _vec(w["g_mix_pre"]), dh1, ROW_TILE_HEAVY,
                                   "mix_in_bwd" + tag, _take(hosts, "mix_in_bwd"))
    return dh0, big, gs


GROUPS = ("in", "sq", "ffn_in", "ffn_out", "mix", "dw")
WIRE_DTYPE = {"in": BF16, "sq": BF16, "ffn_in": BF16, "ffn_out": BF16, "mix": BF16, "dw": F32}
GATHER_FIRST = ("in", "mix", "dw")
GATHER_RIDES = (("mix_in", "sgu_fwd", ("sq", "ffn_in"), ()),
                ("conv_fwd", "pool_fwd", ("ffn_out",), ("in",)),
                ("merge_fwd", "mix_out", (), ("sq",)),
                ("ffn_in", "ffn_out", (), ("ffn_in", "ffn_out", "mix", "dw")))
REDUCE_UPPER = ("ffn_out_bwd", (("ffn_in_bwd", ("in", "ffn_out")), ("merge_bwd", ("sq", "ffn_in", "mix", "dw"))))
REDUCE_OWN = ("sgu_bwd", (("conv_bwd_norm", ("ffn_in", "ffn_out")), ("conv_bwd_taps", ("sq",))))
REDUCE_LAST = ("in", "mix", "dw")


def _group_members(wts):
    n_layers = wts["w_in"].shape[0]
    dw = wts["w_dw"].reshape(n_layers, CONV_WIDTH, -1)
    return {"in": [wts["w_in"]], "sq": [wts[nm] for nm in SQ], "ffn_in": [wts["w_ffn_in"]],
            "ffn_out": [wts["w_ffn_out"]],
            "mix": [wts["w_ple"], wts["w_pool"].reshape(n_layers, POOL_GROUP, POOL_GROUP)],
            "dw": [jnp.pad(dw, ((0, 0), (0, CONV_HALO - CONV_WIDTH), (0, 0)))]}


class _Gather:
    PLACED, OVER_ICI, FULL = 0, 1, 2

    def __init__(self):
        self.buf, self.stage, self.pending = {}, {}, []

    def put(self, key, buf):
        self.buf[key], self.stage[key] = buf, self.PLACED

    def _flush(self):
        for keys, pay, stage in self.pending:
            if pay.results is not None:
                for key, res in zip(keys, pay.results):
                    self.buf[key], self.stage[key] = res, stage
        self.pending = [entry for entry in self.pending if entry[1].results is None]

    def _factory(self, make, keys, before, after):
        def factory():
            if not keys:
                return None
            self._flush()
            assert all(self.stage[k] == before for k in keys), (keys, self.stage)
            pay = make([self.buf[k] for k in keys])
            self.pending.append((keys, pay, after))
            return pay
        return factory

    def ici(self, keys):
        return self._factory(_gather_ici, keys, self.PLACED, self.OVER_ICI)

    def d2d(self, keys):
        return self._factory(_gather_d2d, keys, self.OVER_ICI, self.FULL)

    def get(self, li, group):
        self._flush()
        assert self.stage[(li, group)] == self.FULL, (li, group)
        return self.buf[(li, group)]


class _Reduce:
    def __init__(self, pos, n_layers):
        self.pos, self.n_layers, self.exchanged, self.stages = pos, n_layers, [], []

    def exchange(self, li, groups, grads):
        def factory():
            pay = _pair_exchange([_half_view(grads[g]) for g in groups])
            self.exchanged.append((li, list(groups), pay))
            return pay
        return factory

    def _received(self, li, group):
        for lj, groups, pay in self.exchanged:
            if lj == li and group in groups:
                return pay.results[groups.index(group)]
        raise KeyError((li, group))

    def chips(self, li, groups, grads):
        def factory():
            parts = [_pair_sum(_half_view(grads[g]), self._received(li, g), self.pos, WIRE_DTYPE[g],
                               "pair_sum_%s_l%d" % (g, li)) for g in groups]
            pay = _chip_exchange(parts)
            self.stages.append((li, groups, parts, pay))
            return pay
        return factory

    def finish(self):
        reduced = {}
        for li, groups, parts, pay in self.stages:
            for g, part, landed in zip(groups, parts, pay.results):
                reduced[g] = _chip_sum(part, landed, reduced.get(g), li, self.n_layers, self.pos,
                                       "chip_sum_%s_l%d" % (g, li))
        return reduced


def _pack_small(tree):
    flat = jnp.concatenate([tree[nm].reshape(-1).astype(F32) for nm in SMALL])
    return flat.reshape(-1, 128)


def _unpack_small(packed, like):
    out, off = {}, 0
    flat = packed.reshape(-1)
    for nm in SMALL:
        n = like[nm].size
        out[nm] = flat[off:off + n].reshape(like[nm].shape)
        off += n
    return out


WEIGHTS = ("g_mix_pre", "w_in", "w_sgu_s", "b_sgu_s", "g_sgu_v", "b_sgu_v", "w_sgu_out", "w_dw", "b_dw", "g_conv_ln",
           "b_conv_ln", "w_conv_out", "w_pool", "s_pool", "w_pool_out", "w_out", "g_mix_post", "g_ffn_pre",
           "w_ffn_in", "w_ffn_out", "g_ffn_post", "w_ple", "w_ple_gate")


def kernel(x, p, g_mix_pre, w_in, w_sgu_s, b_sgu_s, g_sgu_v, b_sgu_v, w_sgu_out, w_dw, b_dw, g_conv_ln, b_conv_ln, w_conv_out, w_pool, s_pool, w_pool_out, w_out, g_mix_post, g_ffn_pre, w_ffn_in, w_ffn_out, g_ffn_post, w_ple, w_ple_gate, loss_target, m_g_mix_pre, m_w_in, m_w_sgu_s, m_b_sgu_s, m_g_sgu_v, m_b_sgu_v, m_w_sgu_out, m_w_dw, m_b_dw, m_g_conv_ln, m_b_conv_ln, m_w_conv_out, m_w_pool, m_s_pool, m_w_pool_out, m_w_out, m_g_mix_post, m_g_ffn_pre, m_w_ffn_in, m_w_ffn_out, m_g_ffn_post, m_w_ple, m_w_ple_gate, v_g_mix_pre, v_w_in, v_w_sgu_s, v_b_sgu_s, v_g_sgu_v, v_b_sgu_v, v_w_sgu_out, v_w_dw, v_b_dw, v_g_conv_ln, v_b_conv_ln, v_w_conv_out, v_w_pool, v_s_pool, v_w_pool_out, v_w_out, v_g_mix_post, v_g_ffn_pre, v_w_ffn_in, v_w_ffn_out, v_g_ffn_post, v_w_ple, v_w_ple_gate):
    args = dict(locals())
    wts = {nm: args[nm] for nm in WEIGHTS}
    mom = {nm: args["m_" + nm] for nm in WEIGHTS}
    var = {nm: args["v_" + nm] for nm in WEIGHTS}
    n_layers = w_in.shape[0]
    h = x.reshape(x.shape[1:])
    target = loss_target.reshape(loss_target.shape[1:])
    cx, cy, core = lax.axis_index("x"), lax.axis_index("y"), lax.axis_index("c")
    pos = jnp.stack([2 * cx + cy, 2 * (1 - cx) + cy, 2 * cx + (1 - cy), 2 * (1 - cx) + (1 - cy), core,
                     4 * cx + 2 * cy + core])
    pos = pos.astype(jnp.int32)

    members = _group_members(wts)
    gather = _Gather()
    for li in range(n_layers):
        for g in GROUPS:
            gather.put((li, g), _place(members[g], li, pos, WIRE_DTYPE[g], "place_%s_l%d" % (g, li)))
    first = [(0, g) for g in GATHER_FIRST]
    _run_payload(gather.ici(first)(), "gather_ici_first")
    _run_payload(gather.d2d(first)(), "gather_d2d_first")

    saved, layer_w = [], []
    for li in range(n_layers):
        hosts = {}
        for ici_host, d2d_host, own, nxt in GATHER_RIDES:
            keys = [(li, g) for g in own if li == 0] + [(li + 1, g) for g in nxt if li + 1 < n_layers]
            hosts[ici_host], hosts[d2d_host] = gather.ici(keys), gather.d2d(keys)
        w = _LayerWeights(functools.partial(gather.get, li), wts, li)
        layer_w.append(w)
        h, s = _layer_fwd(h, p[li, 0], w, li, hosts)
        saved.append(s)
    dh, sq_err = _loss_head(h, target, "loss_head")
    loss = lax.psum(sq_err[0, 0] * (0.5 / D_MODEL), ("x", "y", "c"))

    reduce = _Reduce(pos, n_layers)
    small_grads = [{} for _ in range(n_layers)]
    late = (0, SMALL[0])
    small = {}

    def small_vec():
        def leaf(li, nm):
            shape = wts[nm].shape[1:]
            return jnp.zeros(shape, F32) if (li, nm) == late else small_grads[li][nm].reshape(shape)
        return _pack_small({nm: jnp.stack([leaf(li, nm) for li in range(n_layers)], axis=0) for nm in SMALL})

    upper = None
    for li in reversed(range(n_layers)):
        own = {}
        hosts = {}
        plans = [(REDUCE_UPPER, li + 1, upper)] if upper is not None else []
        if li == 0:
            plans.append((REDUCE_OWN, 0, own))

            def last_rides(own=own):
                _run_payload(reduce.exchange(0, REDUCE_LAST, own)(), "pair_exchange_last")
                small["vec"] = small_vec()
                small["exchange"] = _small_exchange(small["vec"])
                return _join(reduce.chips(0, REDUCE_LAST, own)(), small["exchange"])
            hosts["mix_in_bwd"] = last_rides
        for (pair_host, chip_hosts), lj, grads in plans:
            groups = [g for _, gs_ in chip_hosts for g in gs_]
            hosts[pair_host] = reduce.exchange(lj, groups, grads)
            for chip_host, gs_ in chip_hosts:
                hosts[chip_host] = reduce.chips(lj, gs_, grads)
        dh, upper, _ = _layer_bwd(dh, p[li, 0], layer_w[li], saved[li], li, hosts, own, small_grads[li])
    grad_x = dh[None]
    reduced = reduce.finish()

    shared = _run_payload(_pair_share([reduced[g] for g in GROUPS]), "pair_share")
    red = {g: b.reshape(b.shape[:2] + (-1, b.shape[-1])) for g, b in zip(GROUPS, shared)}

    where = {"w_in": ("in", 0), "w_ffn_in": ("ffn_in", 0), "w_ffn_out": ("ffn_out", 0), "w_ple": ("mix", 0),
             "w_pool": ("mix", 1)}
    for slot, nm in enumerate(SQ):
        where[nm] = ("sq", slot)
    outs = {}
    for nm, (g, slot) in where.items():
        shape = wts[nm].shape
        to3 = lambda a: a.reshape((n_layers,) + red[g].shape[2:])
        res = _adamw(to3(wts[nm]), red[g], slot, to3(mom[nm]), to3(var[nm]), "adamw_" + nm)
        outs[nm] = [r.reshape(shape) for r in res]
    gdw = red["dw"][:, :, :CONV_WIDTH]
    to3 = lambda a: a.reshape(n_layers, CONV_WIDTH, -1)
    res = _adamw(to3(wts["w_dw"]), gdw, 0, to3(mom["w_dw"]), to3(var["w_dw"]), "adamw_w_dw")
    outs["w_dw"] = [r.reshape(wts["w_dw"].shape) for r in res]

    gmain = _small_sum(small["vec"], small["exchange"].results[0], pos, "small_sum")
    glate = _all_reduce_small(small_grads[late[0]][late[1]].reshape(-1, 128), "all_reduce_late")
    gsmall = jnp.concatenate([glate, gmain[glate.shape[0]:]], axis=0)
    pk = lambda tree: _pack_small({nm: tree[nm] for nm in SMALL})[None]
    res = _adamw(pk(wts), gsmall[None, None], 0, pk(mom), pk(var), "adamw_small")
    unpacked = [_unpack_small(r[0], wts) for r in res]
    for nm in SMALL:
        outs[nm] = [u[nm] for u in unpacked]

    result = [loss, grad_x]
    for k in range(4):
        result += [outs[nm][k] for nm in WEIGHTS]
    return tuple(result)
```

```python
import functools

import jax
import jax.numpy as jnp
from jax import lax
from jax.experimental import pallas as pl
from jax.experimental.pallas import tpu as pltpu

F32 = jnp.float32
BF16 = jnp.bfloat16
MESH = pl.DeviceIdType.MESH

EPS = 1e-6
D_MODEL = 1024
SGU_BLOCK = 128
SGU_HEADS = 8
CHUNK = 64
CONV_WIDTH = 31
CONV_HALO = 32
POOL_WINDOWS = (2, 4, 8, 16)
POOL_HALO = 16
POOL_GROUP = 256
D_FF = 2816
N_CHIPS = 4

ADAM_LR = 0.001
ADAM_B1 = 0.9
ADAM_B2 = 0.999
ADAM_EPS = 1e-08
ADAM_WD = 0.01
ADAM_STEP = 10

VMEM_LIMIT = 52 * 1024 * 1024
ROW_TILE = 512
ROW_TILE_HEAVY = 256
CONV_ROWS = 64
CONV_LANES = 128
EW_BLOCK_BYTES = 2 * 1024 * 1024
TOKEN_TILE = 2048
FF_CHUNK = 256


def _params(n_grid):
    return pltpu.CompilerParams(dimension_semantics=("arbitrary",) * n_grid, vmem_limit_bytes=VMEM_LIMIT)


def _dot(a, b):
    return jnp.dot(a.astype(BF16), b.astype(BF16), preferred_element_type=F32)


def _dot_nt(a, b):
    return lax.dot_general(a.astype(BF16), b.astype(BF16), (((1,), (1,)), ((), ())), preferred_element_type=F32)


def _dot_tn(a, b):
    return lax.dot_general(a.astype(BF16), b.astype(BF16), (((0,), (0,)), ((), ())), preferred_element_type=F32)


def _sigmoid(x):
    return 0.5 * jnp.tanh(0.5 * x) + 0.5


_GELU_C = 0.7978845608028654
_GELU_A = 0.044715


def _gelu(x):
    t = jnp.tanh(_GELU_C * (x + _GELU_A * x * x * x))
    return 0.5 * x * (1.0 + t)


def _gelu_and_grad(x):
    x2 = x * x
    t = jnp.tanh(_GELU_C * (x + _GELU_A * x2 * x))
    g = 0.5 * (1.0 + t) + 0.5 * x * (1.0 - t * t) * (_GELU_C * (1.0 + 3.0 * _GELU_A * x2))
    return 0.5 * x * (1.0 + t), g


def _rms_stats(x):
    r = lax.rsqrt(jnp.mean(x * x, axis=-1, keepdims=True) + EPS)
    return x * r, r


def _rms_bwd(xn, r, g, dy):
    gd = dy * g
    return r * (gd - xn * jnp.mean(gd * xn, axis=-1, keepdims=True)), dy * xn


def _ln_stats(x):
    mu = jnp.mean(x, axis=-1, keepdims=True)
    xc = x - mu
    rstd = lax.rsqrt(jnp.mean(xc * xc, axis=-1, keepdims=True) + EPS)
    return xc * rstd, rstd


def _ln_bwd(xhat, rstd, g, dy):
    dxh = dy * g
    return rstd * (dxh - jnp.mean(dxh, axis=-1, keepdims=True) - xhat * jnp.mean(dxh * xhat, axis=-1, keepdims=True))


def _rowsum(x):
    return jnp.sum(x, axis=0, keepdims=True)


def _tile(t, want):
    return min(t, want)


def _full(shape):
    n = len(shape)
    return pl.BlockSpec(shape, lambda *_: (0,) * n)


def _resident(shape):
    n = len(shape)
    return pl.BlockSpec(shape, lambda *_: (0,) * n, pipeline_mode=pl.Buffered(1))


class _Payload:
    def __init__(self, operands, out_shape, aliases, scratch, start, finish):
        self.operands, self.out_shape, self.aliases, self.scratch = list(operands), list(out_shape), aliases, scratch
        self.start, self.finish = start, finish
        self.results = None
        self.parts = None

    def deliver(self, results):
        self.results = list(results)
        if self.parts:
            a, b, ma = self.parts
            a.deliver(self.results[:ma])
            b.deliver(self.results[ma:])


def _pcall(body, *, name, grid, in_specs, out_specs, out_shape, operands, scratch_shapes=(), comm=None, aliases=None):
    single = not isinstance(out_shape, (list, tuple))
    out_specs = [out_specs] if single else list(out_specs)
    out_shape = [out_shape] if single else list(out_shape)
    aliases = dict(aliases or {})
    if comm is None:
        res = pl.pallas_call(
            body, name=name, grid=grid, in_specs=list(in_specs), out_specs=out_specs, out_shape=out_shape,
            scratch_shapes=list(scratch_shapes), input_output_aliases=aliases,
            compiler_params=_params(len(grid)))(*operands)
        return res[0] if single else res
    n_in, n_out, n_scr = len(in_specs), len(out_shape), len(scratch_shapes)
    ci, co = len(comm.operands), len(comm.out_shape)

    def hosted(*refs):
        bounds = [0, n_in, n_in + ci, n_in + ci + n_out, n_in + ci + n_out + co, n_in + ci + n_out + co + n_scr]
        a, b, c_, d_, s_ = [refs[lo:hi] for lo, hi in zip(bounds[:-1], bounds[1:])]
        t_ = refs[bounds[-1]:]
        ids = [pl.program_id(q) for q in range(len(grid))]
        first = functools.reduce(jnp.logical_and, [i == 0 for i in ids])
        last = functools.reduce(jnp.logical_and, [i == pl.num_programs(q) - 1 for q, i in enumerate(ids)])

        @pl.when(first)
        def _():
            comm.start(b, d_, t_)

        body(*a, *c_, *s_)

        @pl.when(last)
        def _():
            comm.finish(b, d_, t_)

    res = pl.pallas_call(
        hosted, name=name, grid=grid, in_specs=list(in_specs) + [ANY] * ci, out_specs=out_specs + [ANY] * co,
        out_shape=out_shape + comm.out_shape, scratch_shapes=list(scratch_shapes) + list(comm.scratch),
        input_output_aliases={**aliases, **{n_in + i: n_out + o for i, o in comm.aliases.items()}},
        compiler_params=pltpu.CompilerParams(dimension_semantics=("arbitrary",) * len(grid),
                                             vmem_limit_bytes=VMEM_LIMIT, has_side_effects=True),
    )(*operands, *comm.operands)
    comm.deliver(res[n_out:])
    res = res[:n_out]
    return res[0] if single else res


def _run_payload(comm, name):
    ci, co = len(comm.operands), len(comm.out_shape)

    def body(*refs):
        b, d_, t_ = refs[:ci], refs[ci:ci + co], refs[ci + co:]
        comm.start(b, d_, t_)
        comm.finish(b, d_, t_)

    res = pl.pallas_call(
        body, name=name, in_specs=[ANY] * ci, out_specs=[ANY] * co, out_shape=comm.out_shape,
        input_output_aliases=dict(comm.aliases), scratch_shapes=list(comm.scratch),
        compiler_params=pltpu.CompilerParams(has_side_effects=True))(*comm.operands)
    comm.deliver(res)
    return comm.results


def _take(hosts, key):
    return hosts[key]() if hosts and key in hosts else None


def _norm_mm(h, g, w4, name, comm=None):
    t, d = h.shape
    n = w4.shape[2]
    tm = _tile(t, ROW_TILE)

    step = _lane_block(n, 1024)

    def body(h_ref, g_ref, w_ref, o_ref, hn_ref):
        xn, _ = _rms_stats(h_ref[...])
        hn = (xn * g_ref[...]).astype(BF16)
        hn_ref[...] = hn
        for j in range(N_CHIPS):
            for c0 in range(0, n, step):
                o_ref[:, j * n + c0:j * n + c0 + step] = jnp.dot(
                    hn, w_ref[j, :, c0:c0 + step], preferred_element_type=F32).astype(BF16)

    return _pcall(
        body, name=name, grid=(t // tm,),
        in_specs=[pl.BlockSpec((tm, d), lambda i: (i, 0)), _full((1, d)), _resident(w4.shape)],
        out_specs=[pl.BlockSpec((tm, N_CHIPS * n), lambda i: (i, 0)), pl.BlockSpec((tm, d), lambda i: (i, 0))],
        out_shape=[jax.ShapeDtypeStruct((t, N_CHIPS * n), BF16), jax.ShapeDtypeStruct((t, d), BF16)],
        operands=(h, g, w4), comm=comm)


def _sgu_mask():
    ii = lax.broadcasted_iota(jnp.int32, (SGU_BLOCK, SGU_BLOCK), 0) // CHUNK
    jj = lax.broadcasted_iota(jnp.int32, (SGU_BLOCK, SGU_BLOCK), 1) // CHUNK
    return jj <= ii


def _sgu_fwd(proj, wm, bs3, gv, bv, name, comm=None):
    t = proj.shape[0]
    d = D_MODEL
    tm = _tile(t, ROW_TILE)
    hd = d // SGU_HEADS

    def body(zu_ref, zv_ref, wm_ref, bs_ref, gv_ref, bv_ref, o_ref):
        mask = _sgu_mask()
        for blk in range(tm // SGU_BLOCK):
            rows = pl.ds(blk * SGU_BLOCK, SGU_BLOCK)
            u = _gelu(zu_ref[rows, :].astype(F32))
            xhat, _ = _ln_stats(_gelu(zv_ref[rows, :].astype(F32)))
            vn = (xhat * gv_ref[...] + bv_ref[...]).astype(BF16)
            for hh in range(SGU_HEADS):
                cols = slice(hh * hd, (hh + 1) * hd)
                wmh = jnp.where(mask, wm_ref[hh], 0.0).astype(BF16)
                mixed = jnp.dot(wmh, vn[:, cols], preferred_element_type=F32) + bs_ref[hh]
                o_ref[rows, cols] = (u[:, cols] * mixed).astype(BF16)

    return _pcall(
        body, name=name, grid=(t // tm,),
        in_specs=[pl.BlockSpec((tm, d), lambda i: (i, 0)), pl.BlockSpec((tm, d), lambda i: (i, 1)),
                  _full(wm.shape), _full(bs3.shape), _full(gv.shape), _full(bv.shape)],
        out_specs=pl.BlockSpec((tm, d), lambda i: (i, 0)),
        out_shape=jax.ShapeDtypeStruct((t, d), BF16),
        operands=(proj, proj, wm, bs3, gv, bv), comm=comm)


def _conv_taps(scr_ref, r0, c0, base, weight):
    n = CONV_ROWS + CONV_HALO
    win = scr_ref[pl.ds(r0, n), pl.ds(c0, CONV_LANES)]
    acc = None
    for r in range(8):
        rolled = win if r == 0 else pltpu.roll(win, n - r, 0)
        for q in range((CONV_HALO + 7) // 8 + 1):
            k = 8 * q + r - base
            if 0 <= k < CONV_WIDTH and 8 * q + CONV_ROWS <= n:
                term = weight(k) * rolled[8 * q:8 * q + CONV_ROWS]
                acc = term if acc is None else acc + term
    return acc


def _glu_rows(a_ref, g_ref):
    return a_ref[...].astype(F32) * _sigmoid(g_ref[...].astype(F32))


def _conv_into(scr_ref, cv_ref, w_ref, tm, base, flip):
    def chunk(ci, carry):
        r0 = pl.multiple_of(ci * CONV_ROWS, CONV_ROWS)
        for c0 in range(0, D_MODEL, CONV_LANES):
            def weight(k, c0=c0):
                kk = CONV_WIDTH - 1 - k if flip else k
                return w_ref[kk:kk + 1, c0:c0 + CONV_LANES]
            cv_ref[pl.ds(r0, CONV_ROWS), pl.ds(c0, CONV_LANES)] = _conv_taps(scr_ref, r0, c0, base, weight)
        return carry

    lax.fori_loop(0, tm // CONV_ROWS, chunk, 0)


def _conv_specs(t, tm, d):
    hb = tm // CONV_HALO
    main = [pl.BlockSpec((tm, d), lambda i: (i, 2)), pl.BlockSpec((tm, d), lambda i: (i, 3))]
    halo = [pl.BlockSpec((CONV_HALO, d), lambda i: (jnp.maximum(i * hb - 1, 0), 2)),
            pl.BlockSpec((CONV_HALO, d), lambda i: (jnp.maximum(i * hb - 1, 0), 3))]
    return main, halo


def _fill_glu_history(scr_ref, a_ref, g_ref, ah_ref, gh_ref, tm):
    hist = _glu_rows(ah_ref, gh_ref)
    scr_ref[0:CONV_HALO, :] = jnp.where(pl.program_id(0) > 0, hist, 0.0)
    scr_ref[CONV_HALO:CONV_HALO + tm, :] = _glu_rows(a_ref, g_ref)


_CONV_BASE = CONV_HALO - (CONV_WIDTH - 1)


def _conv_fwd(proj, wdw, bdw, gln, bln, name, comm=None):
    t = proj.shape[0]
    d = D_MODEL
    tm = _tile(t, ROW_TILE)
    main, halo = _conv_specs(t, tm, d)

    def body(a_ref, g_ref, ah_ref, gh_ref, w_ref, b_ref, gl_ref, bl_ref, o_ref, cv_ref, scr_ref):
        _fill_glu_history(scr_ref, a_ref, g_ref, ah_ref, gh_ref, tm)
        _conv_into(scr_ref, cv_ref, w_ref, tm, _CONV_BASE, False)
        xhat, _ = _ln_stats(cv_ref[...] + b_ref[...])
        cn = xhat * gl_ref[...] + bl_ref[...]
        o_ref[...] = (cn * _sigmoid(cn)).astype(BF16)

    row = pl.BlockSpec((tm, d), lambda i: (i, 0))
    return _pcall(
        body, name=name, grid=(t // tm,),
        in_specs=main + halo + [_full(wdw.shape), _full(bdw.shape), _full(gln.shape), _full(bln.shape)],
        out_specs=[row, row],
        out_shape=[jax.ShapeDtypeStruct((t, d), BF16), jax.ShapeDtypeStruct((t, d), F32)],
        scratch_shapes=[pltpu.VMEM((tm + CONV_HALO, d), F32)],
        operands=(proj, proj, proj, proj, wdw, bdw, gln, bln), comm=comm)


def _pool_fill(scr_ref, z_ref, zh_ref, tm):
    scr_ref[0:POOL_HALO, :] = jnp.where(pl.program_id(0) > 0, zh_ref[...].astype(F32), 0.0)
    scr_ref[POOL_HALO:POOL_HALO + tm, :] = z_ref[...].astype(F32)


def _pool_count(t0, rows, w):
    pos = (t0 + lax.broadcasted_iota(jnp.int32, (rows, 1), 0) + 1).astype(F32)
    return jnp.minimum(pos, float(w))


def _pooled_group(scr_ref, gi, w, tm, t0):
    cols = pl.ds(gi * POOL_GROUP, POOL_GROUP)
    acc = scr_ref[pl.ds(POOL_HALO, tm), cols]
    z = acc
    for k in range(1, w):
        acc = acc + scr_ref[pl.ds(POOL_HALO - k, tm), cols]
    return acc / _pool_count(t0, tm, w) - z


def _pool_specs(tm, d):
    hb = tm // POOL_HALO
    return [pl.BlockSpec((tm, d), lambda i: (i, 4)),
            pl.BlockSpec((POOL_HALO, d), lambda i: (jnp.maximum(i * hb - 1, 0), 4))]


def _pool_fwd(proj, wpool, spool, name, comm=None):
    t = proj.shape[0]
    d = D_MODEL
    tm = _tile(t, ROW_TILE)

    def body(z_ref, zh_ref, w_ref, s_ref, o_ref, scr_ref):
        _pool_fill(scr_ref, z_ref, zh_ref, tm)
        t0 = pl.program_id(0) * tm
        for gi, w in enumerate(POOL_WINDOWS):
            cols = slice(gi * POOL_GROUP, (gi + 1) * POOL_GROUP)
            pooled = _pooled_group(scr_ref, gi, w, tm, t0)
            o_ref[:, cols] = (_dot(pooled, w_ref[gi]) * s_ref[:, cols]).astype(BF16)

    return _pcall(
        body, name=name, grid=(t // tm,),
        in_specs=_pool_specs(tm, d) + [_full(wpool.shape), _full(spool.shape)],
        out_specs=pl.BlockSpec((tm, d), lambda i: (i, 0)),
        out_shape=jax.ShapeDtypeStruct((t, d), BF16),
        scratch_shapes=[pltpu.VMEM((tm + POOL_HALO, d), F32)],
        operands=(proj, proj, wpool, spool), comm=comm)


def _merge_fwd(proj, sg, cs, ps, wa, wb, wc, name, comm=None):
    t = proj.shape[0]
    d = D_MODEL
    tm = _tile(t, ROW_TILE_HEAVY)

    def body(za_ref, zb_ref, zc_ref, sg_ref, cs_ref, ps_ref, wa_ref, wb_ref, wc_ref, ba_ref, bb_ref, bc_ref, m_ref):
        merged = None
        for z_ref, x_ref, w_ref, b_ref in ((za_ref, sg_ref, wa_ref, ba_ref), (zb_ref, cs_ref, wb_ref, bb_ref),
                                           (zc_ref, ps_ref, wc_ref, bc_ref)):
            br = jnp.dot(x_ref[...], w_ref[...], preferred_element_type=F32)
            b_ref[...] = br.astype(BF16)
            term = _sigmoid(z_ref[...].astype(F32)) * br
            merged = term if merged is None else merged + term
        m_ref[...] = merged.astype(BF16)

    row = pl.BlockSpec((tm, d), lambda i: (i, 0))
    wspec = _resident((d, d))
    return _pcall(
        body, name=name, grid=(t // tm,),
        in_specs=[pl.BlockSpec((tm, d), lambda i: (i, 5)), pl.BlockSpec((tm, d), lambda i: (i, 6)),
                  pl.BlockSpec((tm, d), lambda i: (i, 7)), row, row, row, wspec, wspec, wspec],
        out_specs=[row, row, row, row],
        out_shape=[jax.ShapeDtypeStruct((t, d), BF16)] * 4,
        operands=(proj, proj, proj, sg, cs, ps, wa, wb, wc), comm=comm)


def _mm_norm_res(a, w, g, hres, name, comm=None):
    t, k = a.shape
    d = w.shape[1]
    tm = _tile(t, ROW_TILE)

    def body(a_ref, w_ref, g_ref, h_ref, y_ref, o_ref):
        y = jnp.dot(a_ref[...], w_ref[...], preferred_element_type=F32)
        y_ref[...] = y
        yn, _ = _rms_stats(y)
        o_ref[...] = h_ref[...] + yn * g_ref[...]

    row = pl.BlockSpec((tm, d), lambda i: (i, 0))
    return _pcall(
        body, name=name, grid=(t // tm,),
        in_specs=[pl.BlockSpec((tm, k), lambda i: (i, 0)), _resident(w.shape), _full(g.shape), row],
        out_specs=[row, row],
        out_shape=[jax.ShapeDtypeStruct((t, d), F32)] * 2,
        operands=(a, w, g, hres), comm=comm)


def _ffn_in(h, g, w4, name, comm=None):
    t, d = h.shape
    n = w4.shape[2]
    tm = _tile(t, ROW_TILE)
    nj = D_FF // n

    def body(h_ref, g_ref, w_ref, fg_ref, fu_ref, act_ref, hn_ref):
        xn, _ = _rms_stats(h_ref[...])
        hn = (xn * g_ref[...]).astype(BF16)
        hn_ref[...] = hn
        for j in range(nj):
            cols = slice(j * n, (j + 1) * n)
            fg = jnp.dot(hn, w_ref[j], preferred_element_type=F32)
            fu = jnp.dot(hn, w_ref[j + nj], preferred_element_type=F32)
            fg_ref[:, cols] = fg.astype(BF16)
            fu_ref[:, cols] = fu.astype(BF16)
            act_ref[:, cols] = (fg * _sigmoid(fg) * fu).astype(BF16)

    wide = pl.BlockSpec((tm, D_FF), lambda i: (i, 0))
    return _pcall(
        body, name=name, grid=(t // tm,),
        in_specs=[pl.BlockSpec((tm, d), lambda i: (i, 0)), _full((1, d)), _resident(w4.shape)],
        out_specs=[wide, wide, wide, pl.BlockSpec((tm, d), lambda i: (i, 0))],
        out_shape=[jax.ShapeDtypeStruct((t, D_FF), BF16)] * 3 + [jax.ShapeDtypeStruct((t, d), BF16)],
        operands=(h, g, w4), comm=comm)


def _ple_fwd(h, p, wg, wp, name, comm=None):
    t, d = h.shape
    tm = _tile(t, ROW_TILE)

    def body(h_ref, p_ref, wg_ref, wp_ref, o_ref, q_ref, e_ref):
        hh = h_ref[...]
        q = _dot(hh, wg_ref[...])
        e = _dot(p_ref[...], wp_ref[...])
        q_ref[...] = q.astype(BF16)
        e_ref[...] = e.astype(BF16)
        o_ref[...] = hh + _sigmoid(q) * e

    row = pl.BlockSpec((tm, d), lambda i: (i, 0))
    return _pcall(
        body, name=name, grid=(t // tm,),
        in_specs=[row, pl.BlockSpec((tm, p.shape[1]), lambda i: (i, 0)), _resident(wg.shape), _resident(wp.shape)],
        out_specs=[row, row, row],
        out_shape=[jax.ShapeDtypeStruct((t, d), F32), jax.ShapeDtypeStruct((t, d), BF16),
                   jax.ShapeDtypeStruct((t, d), BF16)],
        operands=(h, p, wg, wp), comm=comm)


def _loss_head(y, target, name):
    t, d = y.shape
    tm = _tile(t, ROW_TILE)

    def body(y_ref, t_ref, dy_ref, l_ref):
        @pl.when(pl.program_id(0) == 0)
        def _():
            l_ref[...] = jnp.zeros_like(l_ref)

        err = y_ref[...] - t_ref[...]
        dy_ref[...] = err * (1.0 / d)
        l_ref[...] += jnp.sum(err * err, keepdims=True)[:, :1] * jnp.ones((1, 128), F32)

    row = pl.BlockSpec((tm, d), lambda i: (i, 0))
    return pl.pallas_call(
        body, name=name, grid=(t // tm,),
        in_specs=[row, row], out_specs=[row, _full((1, 128))],
        out_shape=[jax.ShapeDtypeStruct((t, d), F32), jax.ShapeDtypeStruct((1, 128), F32)],
        compiler_params=_params(1))(y, target)


def _flush_slots(acc_ref, out_ref, slots, sem_ref):
    @pl.when(pl.program_id(0) == pl.num_programs(0) - 1)
    def _():
        copies = [pltpu.make_async_copy(acc_ref.at[k], out_ref.at[:, slot], sem_ref.at[k])
                  for k, slot in enumerate(slots)]
        for cp in copies:
            cp.start()
        for cp in copies:
            cp.wait()


def _ple_bwd(dh, q, e, wg, h_in, p, slot, n_slots, name):
    t, d = dh.shape
    tm = _tile(t, ROW_TILE)
    rows = d // N_CHIPS

    def body(dh_ref, q_ref, e_ref, wg_ref, h_ref, p_ref, o_ref, sq_ref, dwp_ref, acc_ref, sem_ref):
        @pl.when(pl.program_id(0) == 0)
        def _():
            acc_ref[...] = jnp.zeros_like(acc_ref)
            dwp_ref[...] = jnp.zeros_like(dwp_ref)

        dh_ = dh_ref[...]
        s = _sigmoid(q_ref[...].astype(F32))
        dq = (dh_ * e_ref[...].astype(F32) * s * (1.0 - s)).astype(BF16)
        o_ref[...] = dh_ + _dot_nt(dq, wg_ref[...])
        acc_ref[0] += _dot_tn(h_ref[...], dq).reshape(N_CHIPS, rows, d)
        dwp_ref[...] += _dot_tn(p_ref[...], dh_ * s)
        _flush_slots(acc_ref, sq_ref, (slot,), sem_ref)

    row = pl.BlockSpec((tm, d), lambda i: (i, 0))
    return _pcall(
        body, name=name, grid=(t // tm,),
        in_specs=[row, row, row, _resident(wg.shape), row, pl.BlockSpec((tm, p.shape[1]), lambda i: (i, 0))],
        out_specs=[row, ANY, _full((p.shape[1], d))],
        out_shape=[jax.ShapeDtypeStruct((t, d), F32), jax.ShapeDtypeStruct((N_CHIPS, n_slots, rows, d), F32),
                   jax.ShapeDtypeStruct((p.shape[1], d), F32)],
        scratch_shapes=[pltpu.VMEM((1, N_CHIPS, rows, d), F32), pltpu.SemaphoreType.DMA((1,))],
        operands=(dh, q, e, wg, h_in, p))


def _ffn_out_bwd(dh, f, g, fg, fu, act, w, name, comm=None):
    t, d = dh.shape
    tm = _tile(t, ROW_TILE_HEAVY)

    def body(dh_ref, f_ref, g_ref, fg_ref, fu_ref, act_ref, w_ref, dff_ref, dg_ref, dw_ref, acc_ref, sem_ref):
        @pl.when(pl.program_id(0) == 0)
        def _():
            dg_ref[...] = jnp.zeros_like(dg_ref)
            acc_ref[...] = jnp.zeros_like(acc_ref)

        fn, r = _rms_stats(f_ref[...])
        df, dgt = _rms_bwd(fn, r, g_ref[...], dh_ref[...])
        dg_ref[...] += _rowsum(dgt)
        df = df.astype(BF16)
        acc_ref[...] += _dot_tn(act_ref[...], df)
        for c0 in range(0, D_FF, FF_CHUNK):
            cols = slice(c0, c0 + FF_CHUNK)
            dact = _dot_nt(df, w_ref[cols, :])
            fg_ = fg_ref[:, cols].astype(F32)
            s = _sigmoid(fg_)
            gs = fg_ * s
            dff_ref[:, cols] = (dact * fu_ref[:, cols].astype(F32) * (s + gs - gs * s)).astype(BF16)
            dff_ref[:, D_FF + c0:D_FF + c0 + FF_CHUNK] = (dact * gs).astype(BF16)

        @pl.when(pl.program_id(0) == pl.num_programs(0) - 1)
        def _():
            cp = pltpu.make_async_copy(acc_ref, dw_ref, sem_ref.at[0])
            cp.start()
            cp.wait()

    row = pl.BlockSpec((tm, d), lambda i: (i, 0))
    wide = pl.BlockSpec((tm, D_FF), lambda i: (i, 0))
    return _pcall(
        body, name=name, grid=(t // tm,),
        in_specs=[row, row, _full(g.shape), wide, wide, wide, _resident(w.shape)],
        out_specs=[pl.BlockSpec((tm, 2 * D_FF), lambda i: (i, 0)), _full((1, d)), ANY],
        out_shape=[jax.ShapeDtypeStruct((t, 2 * D_FF), BF16), jax.ShapeDtypeStruct((1, d), F32),
                   jax.ShapeDtypeStruct((D_FF, d), F32)],
        scratch_shapes=[pltpu.VMEM((D_FF, d), F32), pltpu.SemaphoreType.DMA((1,))],
        operands=(dh, f, g, fg, fu, act, w), comm=comm)


def _in_bwd(pieces, w4, unit, h, g, dres, tm, name, comm=None):
    t, d = h.shape
    tm = _tile(t, tm)
    per_chunk = w4.shape[2] // unit
    n_p = len(pieces)

    def body(*refs):
        p_refs = refs[:n_p]
        w_ref, h_ref, g_ref, r_ref, o_ref, dg_ref = refs[n_p:]

        @pl.when(pl.program_id(0) == 0)
        def _():
            dg_ref[...] = jnp.zeros_like(dg_ref)

        acc = None
        u = 0
        for p_ref, (_, nu) in zip(p_refs, pieces):
            for k in range(nu):
                lanes = slice((u % per_chunk) * unit, (u % per_chunk + 1) * unit)
                term = _dot_nt(p_ref[:, k * unit:(k + 1) * unit], w_ref[u // per_chunk, :, lanes])
                acc = term if acc is None else acc + term
                u += 1
        xn, r = _rms_stats(h_ref[...])
        dx, dgt = _rms_bwd(xn, r, g_ref[...], acc)
        dg_ref[...] += _rowsum(dgt)
        o_ref[...] = r_ref[...] + dx

    row = pl.BlockSpec((tm, d), lambda i: (i, 0))
    return _pcall(
        body, name=name, grid=(t // tm,),
        in_specs=[pl.BlockSpec((tm, a.shape[1]), lambda i: (i, 0)) for a, _ in pieces]
        + [_resident(w4.shape), row, _full((1, d)), row],
        out_specs=[row, _full((1, d))],
        out_shape=[jax.ShapeDtypeStruct((t, d), F32), jax.ShapeDtypeStruct((1, d), F32)],
        operands=(*[a for a, _ in pieces], w4, h, g, dres), comm=comm)


def _lane_block(n, cap):
    return max(b for b in range(128, min(n, cap) + 1, 128) if n % b == 0)


def _dw_cols(x, pieces, unit, per_chunk, name):
    t, m = x.shape
    tk = _tile(t, TOKEN_TILE)
    offs, total = [], 0
    for _, nu in pieces:
        offs.append(total)
        total += nu

    def body(x_ref, *refs):
        o_ref = refs[-1]
        u = pl.program_id(0)

        @pl.when(pl.program_id(1) == 0)
        def _():
            o_ref[...] = jnp.zeros_like(o_ref)

        for p_ref, off, (_, nu) in zip(refs[:-1], offs, pieces):
            @pl.when((u >= off) & (u < off + nu))
            def _(p_ref=p_ref):
                o_ref[...] += _dot_tn(x_ref[...], p_ref[...])

    def piece_spec(off, nu):
        def index(u, k):
            mine = (u >= off) & (u < off + nu)
            return jnp.where(mine, k, 0), jnp.clip(u - off, 0, nu - 1)
        return pl.BlockSpec((tk, unit), index)

    return pl.pallas_call(
        body, name=name, grid=(total, t // tk),
        in_specs=[pl.BlockSpec((tk, m), lambda u, k: (k, 0))] + [piece_spec(o, nu) for o, (_, nu) in zip(offs, pieces)],
        out_specs=pl.BlockSpec((None, None, m, unit), lambda u, k: (u // per_chunk, 0, 0, u % per_chunk)),
        out_shape=jax.ShapeDtypeStruct((N_CHIPS, 1, m, per_chunk * unit), F32),
        compiler_params=_params(2))(x, *[a for a, _ in pieces])


def _merge_bwd(dh, mo, g, proj, br, xs, merged, w_out, ws, sq, slots, name, comm=None):
    t, d = dh.shape
    tm = _tile(t, ROW_TILE_HEAVY)
    rows = d // N_CHIPS

    def body(dh_ref, mo_ref, g_ref, za_ref, zb_ref, zc_ref, ba_ref, bb_ref, bc_ref, xa_ref, xb_ref, xc_ref, m_ref,
             wo_ref, wa_ref, wb_ref, wc_ref, sq_in_ref, dz_ref, dsg_ref, dcs_ref, dps_ref, dg_ref, sq_ref,
             acc_ref, sem_ref):
        @pl.when(pl.program_id(0) == 0)
        def _():
            dg_ref[...] = jnp.zeros_like(dg_ref)
            acc_ref[...] = jnp.zeros_like(acc_ref)

        mon, r = _rms_stats(mo_ref[...])
        dmo, dgt = _rms_bwd(mon, r, g_ref[...], dh_ref[...])
        dg_ref[...] += _rowsum(dgt)
        dmo = dmo.astype(BF16)
        acc_ref[0] += _dot_tn(m_ref[...], dmo).reshape(N_CHIPS, rows, d)
        dmerged = _dot_nt(dmo, wo_ref[...])
        branches = ((za_ref, ba_ref, xa_ref, wa_ref, dsg_ref), (zb_ref, bb_ref, xb_ref, wb_ref, dcs_ref),
                    (zc_ref, bc_ref, xc_ref, wc_ref, dps_ref))
        for j, (z_ref, b_ref, x_ref, w_ref, dx_ref) in enumerate(branches):
            gate = _sigmoid(z_ref[...].astype(F32))
            dbr = (dmerged * gate).astype(BF16)
            dz_ref[:, j * d:(j + 1) * d] = (dmerged * b_ref[...].astype(F32) * gate * (1.0 - gate)).astype(BF16)
            dx_ref[...] = _dot_nt(dbr, w_ref[...]).astype(BF16)
            acc_ref[1 + j] += _dot_tn(x_ref[...], dbr).reshape(N_CHIPS, rows, d)
        _flush_slots(acc_ref, sq_ref, slots, sem_ref)

    row = pl.BlockSpec((tm, d), lambda i: (i, 0))
    wspec = _resident((d, d))
    bf = jax.ShapeDtypeStruct((t, d), BF16)
    n_in = 18
    return _pcall(
        body, name=name, grid=(t // tm,),
        in_specs=[row, row, _full(g.shape), pl.BlockSpec((tm, d), lambda i: (i, 5)),
                  pl.BlockSpec((tm, d), lambda i: (i, 6)), pl.BlockSpec((tm, d), lambda i: (i, 7)),
                  row, row, row, row, row, row, row, wspec, wspec, wspec, wspec, ANY],
        out_specs=[pl.BlockSpec((tm, 3 * d), lambda i: (i, 0)), row, row, row, _full((1, d)), ANY],
        out_shape=[jax.ShapeDtypeStruct((t, 3 * d), BF16), bf, bf, bf, jax.ShapeDtypeStruct((1, d), F32),
                   jax.ShapeDtypeStruct(sq.shape, sq.dtype)],
        scratch_shapes=[pltpu.VMEM((4, N_CHIPS, rows, d), F32), pltpu.SemaphoreType.DMA((4,))],
        operands=(dh, mo, g, proj, proj, proj, *br, *xs, merged, w_out, *ws, sq), comm=comm,
        aliases={n_in - 1: 5})


def _sgu_bwd(proj, dsg, wm, bs3, gv, bv, name, comm=None):
    t = proj.shape[0]
    d = D_MODEL
    tm = _tile(t, ROW_TILE_HEAVY)
    hd = d // SGU_HEADS

    def body(zu_ref, zv_ref, d_ref, wm_ref, bs_ref, gv_ref, bv_ref, dz_ref, dwm_ref, dbs_ref, dgv_ref, dbv_ref,
             dvn_ref):
        @pl.when(pl.program_id(0) == 0)
        def _():
            dwm_ref[...] = jnp.zeros_like(dwm_ref)
            dbs_ref[...] = jnp.zeros_like(dbs_ref)
            dgv_ref[...] = jnp.zeros_like(dgv_ref)
            dbv_ref[...] = jnp.zeros_like(dbv_ref)

        mask = _sgu_mask()
        for blk in range(tm // SGU_BLOCK):
            rows = pl.ds(blk * SGU_BLOCK, SGU_BLOCK)
            u, du_dz = _gelu_and_grad(zu_ref[rows, :].astype(F32))
            v0, dv_dz = _gelu_and_grad(zv_ref[rows, :].astype(F32))
            xhat, rstd = _ln_stats(v0)
            vn = (xhat * gv_ref[...] + bv_ref[...]).astype(BF16)
            dsg = d_ref[rows, :].astype(F32)
            dmix = (dsg * u).astype(BF16)
            for hh in range(SGU_HEADS):
                cols = slice(hh * hd, (hh + 1) * hd)
                wmh = jnp.where(mask, wm_ref[hh], 0.0).astype(BF16)
                vb = vn[:, cols]
                mixed = jnp.dot(wmh, vb, preferred_element_type=F32) + bs_ref[hh]
                dz_ref[rows, cols] = (dsg[:, cols] * mixed * du_dz[:, cols]).astype(BF16)
                dmh = dmix[:, cols]
                dwm_ref[hh] += jnp.where(mask, _dot_nt(dmh, vb), 0.0)
                dbs_ref[hh] += jnp.sum(dmh.astype(F32), axis=1, keepdims=True)
                dvn_ref[:, cols] = _dot_tn(wmh, dmh)
            dvn = dvn_ref[...]
            dgv_ref[...] += _rowsum(dvn * xhat)
            dbv_ref[...] += _rowsum(dvn)
            dz_ref[rows, d:2 * d] = (_ln_bwd(xhat, rstd, gv_ref[...], dvn) * dv_dz).astype(BF16)

    return _pcall(
        body, name=name, grid=(t // tm,),
        in_specs=[pl.BlockSpec((tm, d), lambda i: (i, 0)), pl.BlockSpec((tm, d), lambda i: (i, 1)),
                  pl.BlockSpec((tm, d), lambda i: (i, 0)), _full(wm.shape), _full(bs3.shape), _full(gv.shape),
                  _full(bv.shape)],
        out_specs=[pl.BlockSpec((tm, 2 * d), lambda i: (i, 0)), _full(wm.shape), _full(bs3.shape), _full((1, d)),
                   _full((1, d))],
        out_shape=[jax.ShapeDtypeStruct((t, 2 * d), BF16), jax.ShapeDtypeStruct(wm.shape, F32),
                   jax.ShapeDtypeStruct(bs3.shape, F32), jax.ShapeDtypeStruct((1, d), F32),
                   jax.ShapeDtypeStruct((1, d), F32)],
        scratch_shapes=[pltpu.VMEM((SGU_BLOCK, d), F32)],
        operands=(proj, proj, dsg, wm, bs3, gv, bv), comm=comm)


def _conv_bwd_norm(proj, dcs, cv, bdw, gln, bln, name, comm=None):
    t = proj.shape[0]
    d = D_MODEL
    tm = _tile(t, ROW_TILE)
    main, halo = _conv_specs(t, tm, d)
    n_win = CONV_ROWS + CONV_HALO

    def body(a_ref, g_ref, ah_ref, gh_ref, dcs_ref, cv_ref, b_ref, gl_ref, bl_ref,
             dcv_ref, dw_ref, db_ref, dgl_ref, dbl_ref, scr_ref, dwacc_ref):
        @pl.when(pl.program_id(0) == 0)
        def _():
            dwacc_ref[...] = jnp.zeros_like(dwacc_ref)
            db_ref[...] = jnp.zeros_like(db_ref)
            dgl_ref[...] = jnp.zeros_like(dgl_ref)
            dbl_ref[...] = jnp.zeros_like(dbl_ref)

        _fill_glu_history(scr_ref, a_ref, g_ref, ah_ref, gh_ref, tm)
        xhat, rstd = _ln_stats(cv_ref[...] + b_ref[...])
        cn = xhat * gl_ref[...] + bl_ref[...]
        s = _sigmoid(cn)
        dcn = dcs_ref[...].astype(F32) * (s * (1.0 + cn * (1.0 - s)))
        dgl_ref[...] += _rowsum(dcn * xhat)
        dbl_ref[...] += _rowsum(dcn)
        dcv = _ln_bwd(xhat, rstd, gl_ref[...], dcn)
        db_ref[...] += _rowsum(dcv)
        dcv_ref[...] = dcv

        def chunk(ci, carry):
            r0 = pl.multiple_of(ci * CONV_ROWS, CONV_ROWS)
            for c0 in range(0, d, CONV_LANES):
                lanes = pl.ds(c0, CONV_LANES)
                win = scr_ref[pl.ds(r0, n_win), lanes]
                dchunk = dcv_ref[pl.ds(r0, CONV_ROWS), lanes]
                for r in range(8):
                    rolled = win if r == 0 else pltpu.roll(win, n_win - r, 0)
                    for q in range(n_win // 8):
                        k = 8 * q + r - _CONV_BASE
                        if 0 <= k < CONV_WIDTH and 8 * q + CONV_ROWS <= n_win:
                            prod = dchunk * rolled[8 * q:8 * q + CONV_ROWS]
                            part = prod[0:8]
                            for s8 in range(8, CONV_ROWS, 8):
                                part = part + prod[s8:s8 + 8]
                            dwacc_ref[pl.ds(8 * k, 8), lanes] += part
            return carry

        lax.fori_loop(0, tm // CONV_ROWS, chunk, 0)

        @pl.when(pl.program_id(0) == pl.num_programs(0) - 1)
        def _():
            dw_ref[...] = jnp.sum(dwacc_ref[...].reshape(CONV_HALO, 8, d), axis=1)

    row = pl.BlockSpec((tm, d), lambda i: (i, 0))
    vec = _full((1, d))
    return _pcall(
        body, name=name, grid=(t // tm,),
        in_specs=main + halo + [row, row, vec, vec, vec],
        out_specs=[row, _full((CONV_HALO, d)), vec, vec, vec],
        out_shape=[jax.ShapeDtypeStruct((t, d), F32), jax.ShapeDtypeStruct((CONV_HALO, d), F32)]
        + [jax.ShapeDtypeStruct((1, d), F32)] * 3,
        scratch_shapes=[pltpu.VMEM((tm + CONV_HALO, d), F32), pltpu.VMEM((8 * CONV_HALO, d), F32)],
        operands=(proj, proj, proj, proj, dcs, cv, bdw, gln, bln), comm=comm)


def _conv_bwd_taps(proj, dcv, wdw, name, comm=None):
    t = proj.shape[0]
    d = D_MODEL
    tm = _tile(t, ROW_TILE)
    hb = tm // CONV_HALO
    last_halo = t // CONV_HALO - 1

    def body(a_ref, g_ref, dcv_ref, dnext_ref, w_ref, dz_ref, scr_ref, dh_ref):
        scr_ref[0:tm, :] = dcv_ref[...]
        is_last = pl.program_id(0) == pl.num_programs(0) - 1
        scr_ref[tm:tm + CONV_HALO, :] = jnp.where(is_last, 0.0, dnext_ref[...])
        _conv_into(scr_ref, dh_ref, w_ref, tm, 0, True)
        dglu = dh_ref[...]
        a = a_ref[...].astype(F32)
        s = _sigmoid(g_ref[...].astype(F32))
        dz_ref[:, 0:d] = (dglu * s).astype(BF16)
        dz_ref[:, d:2 * d] = (dglu * a * s * (1.0 - s)).astype(BF16)

    return _pcall(
        body, name=name, grid=(t // tm,),
        in_specs=[pl.BlockSpec((tm, d), lambda i: (i, 2)), pl.BlockSpec((tm, d), lambda i: (i, 3)),
                  pl.BlockSpec((tm, d), lambda i: (i, 0)),
                  pl.BlockSpec((CONV_HALO, d), lambda i: (jnp.minimum((i + 1) * hb, last_halo), 0)),
                  _full(wdw.shape)],
        out_specs=pl.BlockSpec((tm, 2 * d), lambda i: (i, 0)),
        out_shape=jax.ShapeDtypeStruct((t, 2 * d), BF16),
        scratch_shapes=[pltpu.VMEM((tm + CONV_HALO, d), F32), pltpu.VMEM((tm, d), F32)],
        operands=(proj, proj, dcv, dcv, wdw), comm=comm)


def _pool_bwd(proj, dps, wpool, spool, name):
    t = proj.shape[0]
    d = D_MODEL
    tm = _tile(t, ROW_TILE)
    hb = tm // POOL_HALO
    last_halo = t // POOL_HALO - 1
    ext = tm + POOL_HALO

    def body(z_ref, zh_ref, d_ref, dnext_ref, w_ref, s_ref, dz_ref, dw_ref, ds_ref, scr_ref, dext_ref, dq_ref):
        @pl.when(pl.program_id(0) == 0)
        def _():
            dw_ref[...] = jnp.zeros_like(dw_ref)
            ds_ref[...] = jnp.zeros_like(ds_ref)

        _pool_fill(scr_ref, z_ref, zh_ref, tm)
        t0 = pl.program_id(0) * tm
        is_last = pl.program_id(0) == pl.num_programs(0) - 1
        dext_ref[0:tm, :] = d_ref[...].astype(F32)
        dext_ref[tm:ext, :] = jnp.where(is_last, 0.0, dnext_ref[...].astype(F32))
        for gi, w in enumerate(POOL_WINDOWS):
            cols = slice(gi * POOL_GROUP, (gi + 1) * POOL_GROUP)
            dps_ext = dext_ref[:, cols]
            dpm_ext = (dps_ext * s_ref[:, cols]).astype(BF16)
            dpooled_ext = _dot_nt(dpm_ext, w_ref[gi])
            dq_ref[...] = dpooled_ext / _pool_count(t0, ext, w)
            acc = dq_ref[pl.ds(0, tm), :]
            for k in range(1, w):
                acc = acc + dq_ref[pl.ds(k, tm), :]
            dz_ref[:, cols] = (acc - dpooled_ext[0:tm]).astype(BF16)
            pooled = _pooled_group(scr_ref, gi, w, tm, t0).astype(BF16)
            pm = jnp.dot(pooled, w_ref[gi], preferred_element_type=F32)
            ds_ref[:, cols] += _rowsum(dps_ext[0:tm] * pm)
            dw_ref[gi] += _dot_tn(pooled, dpm_ext[0:tm])

    return pl.pallas_call(
        body, name=name, grid=(t // tm,),
        in_specs=_pool_specs(tm, d) + [pl.BlockSpec((tm, d), lambda i: (i, 0)),
                                       pl.BlockSpec((POOL_HALO, d), lambda i: (jnp.minimum((i + 1) * hb, last_halo), 0)),
                                       _full(wpool.shape), _full(spool.shape)],
        out_specs=[pl.BlockSpec((tm, d), lambda i: (i, 0)), _full(wpool.shape), _full((1, d))],
        out_shape=[jax.ShapeDtypeStruct((t, d), BF16), jax.ShapeDtypeStruct(wpool.shape, F32),
                   jax.ShapeDtypeStruct((1, d), F32)],
        scratch_shapes=[pltpu.VMEM((tm + POOL_HALO, d), F32), pltpu.VMEM((ext, d), F32),
                        pltpu.VMEM((ext, POOL_GROUP), F32)],
        compiler_params=_params(1))(proj, proj, dps, dps, wpool, spool)


ANY = pl.BlockSpec(memory_space=pl.ANY)


def _mesh_pos():
    x, y, c = lax.axis_index("x"), lax.axis_index("y"), lax.axis_index("c")
    chips = [(1 - x, y), (x, 1 - y), (1 - x, 1 - y)]
    return x, y, c, chips


def _chip_of(xy):
    return 2 * xy[0] + xy[1]


def _half_view(a):
    return a.reshape(a.shape[:-2] + (2, a.shape[-2] // 2, a.shape[-1]))


def _same(arrs):
    return [jax.ShapeDtypeStruct(a.shape, a.dtype) for a in arrs]


def _in_place(n):
    return {g: g for g in range(n)}


def _sems(count):
    return [pltpu.SemaphoreType.DMA((count,)), pltpu.SemaphoreType.DMA((count,))]


def _gather_ici(bufs):
    n = len(bufs)

    def copy(buf, sems, g, j, chip):
        x, y, c, chips = _mesh_pos()
        slab = buf[g].at[chip, :, c]
        return pltpu.make_async_remote_copy(
            src_ref=slab, dst_ref=slab, send_sem=sems[0].at[3 * g + j], recv_sem=sems[1].at[3 * g + j],
            device_id=(*chips[j], c), device_id_type=MESH)

    def start(ins, buf, sems):
        x, y, c, chips = _mesh_pos()
        for g in range(n):
            for j in range(3):
                copy(buf, sems, g, j, 2 * x + y).start()

    def finish(ins, buf, sems):
        x, y, c, chips = _mesh_pos()
        for g in range(n):
            for j in range(3):
                copy(buf, sems, g, j, _chip_of(chips[j])).wait_recv()
        for g in range(n):
            for j in range(3):
                copy(buf, sems, g, j, 2 * x + y).wait_send()

    return _Payload(bufs, _same(bufs), _in_place(n), _sems(3 * n), start, finish)


def _gather_d2d(bufs):
    n = len(bufs)

    def copy(buf, sems, g, j, half):
        x, y, c, chips = _mesh_pos()
        slab = buf[g].at[_chip_of(chips[j]), :, half]
        return pltpu.make_async_remote_copy(
            src_ref=slab, dst_ref=slab, send_sem=sems[0].at[3 * g + j], recv_sem=sems[1].at[3 * g + j],
            device_id=(x, y, 1 - c), device_id_type=MESH)

    def start(ins, buf, sems):
        c = lax.axis_index("c")
        for g in range(n):
            for j in range(3):
                copy(buf, sems, g, j, c).start()

    def finish(ins, buf, sems):
        c = lax.axis_index("c")
        for g in range(n):
            for j in range(3):
                copy(buf, sems, g, j, 1 - c).wait_recv()
        for g in range(n):
            for j in range(3):
                copy(buf, sems, g, j, c).wait_send()

    return _Payload(bufs, _same(bufs), _in_place(n), _sems(3 * n), start, finish)


def _pair_exchange(grads):
    n = len(grads)

    def copy(src, dst, sems, g):
        x, y, c, _ = _mesh_pos()
        return pltpu.make_async_remote_copy(
            src_ref=src[g].at[:, :, 1 - c], dst_ref=dst[g], send_sem=sems[0].at[g], recv_sem=sems[1].at[g],
            device_id=(x, y, 1 - c), device_id_type=MESH)

    def start(src, dst, sems):
        for g in range(n):
            copy(src, dst, sems, g).start()

    def finish(src, dst, sems):
        for g in range(n):
            copy(src, dst, sems, g).wait()

    out_shape = [jax.ShapeDtypeStruct(g.shape[:2] + g.shape[3:], g.dtype) for g in grads]
    return _Payload(grads, out_shape, {}, _sems(n), start, finish)


def _chip_exchange(parts):
    n = len(parts)

    def copy(src, dst, sems, g, j, slot):
        x, y, c, chips = _mesh_pos()
        return pltpu.make_async_remote_copy(
            src_ref=src[g].at[_chip_of(chips[j])], dst_ref=dst[g].at[slot], send_sem=sems[0].at[3 * g + j],
            recv_sem=sems[1].at[3 * g + j], device_id=(*chips[j], c), device_id_type=MESH)

    def start(src, dst, sems):
        x, y, c, chips = _mesh_pos()
        for g in range(n):
            for j in range(3):
                copy(src, dst, sems, g, j, 2 * x + y).start()

    def finish(src, dst, sems):
        x, y, c, chips = _mesh_pos()
        for g in range(n):
            for j in range(3):
                copy(src, dst, sems, g, j, _chip_of(chips[j])).wait_recv()
        for g in range(n):
            for j in range(3):
                copy(src, dst, sems, g, j, 2 * x + y).wait_send()

    return _Payload(parts, _same(parts), {}, _sems(3 * n), start, finish)


def _pair_share(bufs):
    n = len(bufs)

    def copy(buf, sems, g, half):
        x, y, c, _ = _mesh_pos()
        slab = buf[g].at[:, :, half]
        return pltpu.make_async_remote_copy(
            src_ref=slab, dst_ref=slab, send_sem=sems[0].at[g], recv_sem=sems[1].at[g],
            device_id=(x, y, 1 - c), device_id_type=MESH)

    def start(ins, buf, sems):
        c = lax.axis_index("c")
        for g in range(n):
            copy(buf, sems, g, c).start()

    def finish(ins, buf, sems):
        c = lax.axis_index("c")
        for g in range(n):
            copy(buf, sems, g, 1 - c).wait_recv()
        for g in range(n):
            copy(buf, sems, g, c).wait_send()

    return _Payload(bufs, _same(bufs), _in_place(n), _sems(n), start, finish)


def _join(a, b):
    if a is None or b is None:
        return a or b
    na, ma = len(a.operands), len(a.out_shape)
    aliases = dict(a.aliases)
    aliases.update({na + i: ma + o for i, o in b.aliases.items()})
    ka = len(a.scratch)

    def start(ins, outs, sems):
        a.start(ins[:na], outs[:ma], sems[:ka])
        b.start(ins[na:], outs[ma:], sems[ka:])

    def finish(ins, outs, sems):
        a.finish(ins[:na], outs[:ma], sems[:ka])
        b.finish(ins[na:], outs[ma:], sems[ka:])

    joined = _Payload(a.operands + b.operands, a.out_shape + b.out_shape, aliases, list(a.scratch) + list(b.scratch),
                      start, finish)
    joined.parts = (a, b, ma)
    return joined


def _small_exchange(vec):
    def copy(src, dst, sems, k, slot):
        x, y, c, _ = _mesh_pos()
        peer = (x ^ (k >> 2), y ^ ((k >> 1) & 1), c ^ (k & 1))
        return pltpu.make_async_remote_copy(
            src_ref=src[0], dst_ref=dst[0].at[slot], send_sem=sems[0].at[k - 1], recv_sem=sems[1].at[k - 1],
            device_id=peer, device_id_type=MESH)

    def me():
        x, y, c, _ = _mesh_pos()
        return 4 * x + 2 * y + c

    def start(src, dst, sems):
        for k in range(1, 8):
            copy(src, dst, sems, k, me()).start()

    def finish(src, dst, sems):
        for k in range(1, 8):
            copy(src, dst, sems, k, me() ^ k).wait_recv()
        for k in range(1, 8):
            copy(src, dst, sems, k, me()).wait_send()

    return _Payload([vec], [jax.ShapeDtypeStruct((8,) + vec.shape, vec.dtype)], {}, _sems(7), start, finish)


def _small_sum(vec, landed, pos, name):
    r = vec.shape[0]

    def body(pos_ref, v_ref, l_ref, o_ref):
        k = pl.program_id(0)

        @pl.when(k == 0)
        def _():
            o_ref[...] = jnp.zeros_like(o_ref)

        @pl.when(k == pos_ref[POS_DEVICE])
        def _():
            o_ref[...] += v_ref[...]

        @pl.when(k != pos_ref[POS_DEVICE])
        def _():
            o_ref[...] += l_ref[...]

    def landed_index(k, pos_ref):
        me = pos_ref[POS_DEVICE]
        return jnp.where(k == me, (me + 1) % 8, k), 0, 0

    return pl.pallas_call(
        body, name=name,
        grid_spec=pltpu.PrefetchScalarGridSpec(
            num_scalar_prefetch=1, grid=(8,),
            in_specs=[pl.BlockSpec((r, 128), lambda k, pos_ref: (0, 0)), pl.BlockSpec((None, r, 128), landed_index)],
            out_specs=pl.BlockSpec((r, 128), lambda k, pos_ref: (0, 0))),
        out_shape=jax.ShapeDtypeStruct(vec.shape, F32),
        compiler_params=_params(1))(pos, vec, landed)


def _all_reduce_small(vec, name):
    r = vec.shape[0]

    def body(v_ref, o_ref, gath_ref, send_sem, recv_sem):
        x, y, c, _ = _mesh_pos()
        me = 4 * x + 2 * y + c
        gath_ref[me] = v_ref[...]
        copies = []
        for k in range(1, 8):
            peer = (x ^ (k >> 2), y ^ ((k >> 1) & 1), c ^ (k & 1))
            cp = pltpu.make_async_remote_copy(
                src_ref=v_ref, dst_ref=gath_ref.at[me], send_sem=send_sem.at[k - 1], recv_sem=recv_sem.at[k - 1],
                device_id=peer, device_id_type=MESH)
            cp.start()
            copies.append(cp)
        for k in range(1, 8):
            src_id = me ^ k
            pltpu.make_async_remote_copy(
                src_ref=v_ref, dst_ref=gath_ref.at[src_id], send_sem=send_sem.at[k - 1], recv_sem=recv_sem.at[k - 1],
                device_id=(x, y, c), device_id_type=MESH).wait_recv()
        for cp in copies:
            cp.wait_send()
        acc = gath_ref[0]
        for k in range(1, 8):
            acc = acc + gath_ref[k]
        o_ref[...] = acc

    return pl.pallas_call(
        body, name=name,
        in_specs=[pl.BlockSpec(memory_space=pltpu.VMEM)], out_specs=pl.BlockSpec(memory_space=pltpu.VMEM),
        out_shape=jax.ShapeDtypeStruct(vec.shape, F32),
        scratch_shapes=[pltpu.VMEM((8, r, 128), F32), pltpu.SemaphoreType.DMA((7,)), pltpu.SemaphoreType.DMA((7,))],
        compiler_params=pltpu.CompilerParams(has_side_effects=True, vmem_limit_bytes=VMEM_LIMIT))(vec)


def _row_block(rows, cols, mult=16):
    best = None
    for cand in range(mult, rows + 1, mult):
        if rows % cand == 0 and cand * cols * 4 <= EW_BLOCK_BYTES:
            best = cand
    return best or rows


POS_ME, POS_CORE, POS_DEVICE = 0, 4, 5


def _place(arrs, li, pos, dtype, name):
    s = len(arrs)
    _, rows, cols = arrs[0].shape
    rh = rows // 2
    tr = _row_block(rh, cols)
    nb = rh // tr

    def body(pos_ref, *refs):
        o_ref = refs[s]
        for j in range(s):
            @pl.when(pl.program_id(0) == j)
            def _(j=j):
                o_ref[...] = refs[j][...].astype(dtype)

    def in_spec(j):
        return pl.BlockSpec((None, tr, cols), lambda b, hf, i, pos_ref: (li, jnp.where(b == j, hf * nb + i, 0), 0))

    return pl.pallas_call(
        body, name=name,
        grid_spec=pltpu.PrefetchScalarGridSpec(
            num_scalar_prefetch=1, grid=(s, 2, nb), in_specs=[in_spec(j) for j in range(s)],
            out_specs=pl.BlockSpec((None, None, None, tr, cols),
                                   lambda b, hf, i, pos_ref: (pos_ref[POS_ME], b, hf, i, 0))),
        out_shape=jax.ShapeDtypeStruct((N_CHIPS, s, 2, rh, cols), dtype),
        compiler_params=_params(3))(pos, *arrs)


def _pair_sum(grad, recv, pos, out_dtype, name):
    _, s, rh, cols = recv.shape
    tr = _row_block(rh, cols)

    def body(pos_ref, g_ref, r_ref, o_ref):
        o_ref[...] = (g_ref[...] + r_ref[...]).astype(out_dtype)

    blk = (None, None, tr, cols)
    return pl.pallas_call(
        body, name=name,
        grid_spec=pltpu.PrefetchScalarGridSpec(
            num_scalar_prefetch=1, grid=(N_CHIPS, s, rh // tr),
            in_specs=[pl.BlockSpec((None, None, None, tr, cols),
                                   lambda a, b, i, pos_ref: (a, b, pos_ref[POS_CORE], i, 0)),
                      pl.BlockSpec(blk, lambda a, b, i, pos_ref: (a, b, i, 0))],
            out_specs=pl.BlockSpec(blk, lambda a, b, i, pos_ref: (a, b, i, 0))),
        out_shape=jax.ShapeDtypeStruct(recv.shape, out_dtype),
        compiler_params=_params(3))(pos, grad, recv)


def _chip_sum(part, landed, gbuf, li, n_layers, pos, name):
    _, s, rh, cols = part.shape
    tr = _row_block(rh, cols)

    def body(pos_ref, p_ref, a_ref, b_ref, c_ref, *rest):
        o_ref = rest[-1]
        o_ref[...] = ((p_ref[...].astype(F32) + a_ref[...].astype(F32)) + b_ref[...].astype(F32)) \
            + c_ref[...].astype(F32)

    def slot(k):
        return pl.BlockSpec((None, None, tr, cols), lambda b, i, pos_ref: (pos_ref[k], b, i, 0))

    in_specs = [slot(0), slot(1), slot(2), slot(3)]
    operands = [pos, part, landed, landed, landed]
    aliases = {}
    if gbuf is not None:
        in_specs.append(ANY)
        operands.append(gbuf)
        aliases = {len(operands) - 1: 0}
    return pl.pallas_call(
        body, name=name,
        grid_spec=pltpu.PrefetchScalarGridSpec(
            num_scalar_prefetch=1, grid=(s, rh // tr), in_specs=in_specs,
            out_specs=pl.BlockSpec((None, None, None, tr, cols),
                                   lambda b, i, pos_ref: (li, b, pos_ref[POS_CORE], i, 0))),
        out_shape=jax.ShapeDtypeStruct((n_layers, s, 2, rh, cols), F32),
        input_output_aliases=aliases,
        compiler_params=_params(2))(*operands)


def _adamw_math(w, g, m, v):
    m = ADAM_B1 * m + (1.0 - ADAM_B1) * g
    v = ADAM_B2 * v + (1.0 - ADAM_B2) * (g * g)
    m_hat = m / (1.0 - ADAM_B1 ** ADAM_STEP)
    v_hat = v / (1.0 - ADAM_B2 ** ADAM_STEP)
    delta = -ADAM_LR * (m_hat / (jnp.sqrt(v_hat) + ADAM_EPS) + ADAM_WD * w)
    return delta, m, v


def _adamw(w, g, slot, m, v, name):
    l, rows, cols = w.shape
    tr = _row_block(rows, cols, 8)

    def body(w_ref, g_ref, m_ref, v_ref, go_ref, d_ref, mo_ref, vo_ref):
        g_ = g_ref[...]
        delta, m_, v_ = _adamw_math(w_ref[...], g_, m_ref[...], v_ref[...])
        go_ref[...] = g_
        d_ref[...] = delta
        mo_ref[...] = m_
        vo_ref[...] = v_

    blk = pl.BlockSpec((None, tr, cols), lambda a, i: (a, i, 0))
    gblk = pl.BlockSpec((None, None, tr, cols), lambda a, i: (a, slot, i, 0))
    return pl.pallas_call(
        body, name=name, grid=(l, rows // tr), in_specs=[blk, gblk, blk, blk], out_specs=[blk] * 4,
        out_shape=[jax.ShapeDtypeStruct(w.shape, F32)] * 4,
        compiler_params=_params(2))(w, g, m, v)


SQ = ("w_sgu_out", "w_conv_out", "w_pool_out", "w_out", "w_ple_gate")
SMALL = ("g_mix_pre", "w_sgu_s", "b_sgu_s", "g_sgu_v", "b_sgu_v", "b_dw", "g_conv_ln", "b_conv_ln", "s_pool",
         "g_mix_post", "g_ffn_pre", "g_ffn_post")


WHERE = {"w_in": ("in", 0), "w_ffn_in": ("ffn_in", 0), "w_ffn_out": ("ffn_out", 0), "w_ple": ("mix", 0),
         "w_pool": ("mix", 1), "w_dw": ("dw", 0)}
WHERE.update({nm: ("sq", slot) for slot, nm in enumerate(SQ)})


class _LayerWeights:
    def __init__(self, fetch, small, li):
        self.fetch, self.small, self.li, self.cache = fetch, small, li, {}

    def __getitem__(self, nm):
        if nm not in self.cache:
            self.cache[nm] = self._big(nm) if nm in WHERE else self.small[nm][self.li]
        return self.cache[nm]

    def _big(self, nm):
        group, slot = WHERE[nm]
        g = self.fetch(group)
        g = g.reshape(g.shape[:2] + (-1, g.shape[-1]))
        if nm in ("w_in", "w_ffn_in"):
            return g.reshape(N_CHIPS, D_MODEL, -1)
        if nm == "w_ffn_out":
            return g.reshape(D_FF, D_MODEL)
        if nm in SQ:
            return g[:, slot].reshape(D_MODEL, D_MODEL)
        if nm == "w_ple":
            return g[:, slot].transpose(1, 0, 2).reshape(256, D_MODEL)
        if nm == "w_pool":
            return g[:, slot].reshape(N_CHIPS, 4, 64, 256).transpose(1, 0, 2, 3).reshape(4, 256, 256)
        return g.reshape(N_CHIPS, CONV_HALO, -1).transpose(1, 0, 2).reshape(CONV_HALO, D_MODEL)


def _vec(a):
    return a.reshape(1, -1)


def _layer_fwd(h, p, w, li, hosts=None):
    s = {}
    tag = "_l%d" % li
    s["h0"] = h
    proj, hn = _norm_mm(h, _vec(w["g_mix_pre"]), w["w_in"], "mix_in" + tag, _take(hosts, "mix_in"))
    s["proj"], s["hn"] = proj, hn
    bs3 = w["b_sgu_s"].reshape(SGU_HEADS, SGU_BLOCK, 1)
    s["sg"] = _sgu_fwd(proj, w["w_sgu_s"], bs3, _vec(w["g_sgu_v"]), _vec(w["b_sgu_v"]), "sgu_fwd" + tag,
                       _take(hosts, "sgu_fwd"))
    s["cs"], s["cv"] = _conv_fwd(proj, w["w_dw"], _vec(w["b_dw"]), _vec(w["g_conv_ln"]), _vec(w["b_conv_ln"]),
                                 "conv_fwd" + tag, _take(hosts, "conv_fwd"))
    s["ps"] = _pool_fwd(proj, w["w_pool"], _vec(w["s_pool"]), "pool_fwd" + tag, _take(hosts, "pool_fwd"))
    s["bra"], s["brb"], s["brc"], s["merged"] = _merge_fwd(
        proj, s["sg"], s["cs"], s["ps"], w["w_sgu_out"], w["w_conv_out"], w["w_pool_out"], "merge_fwd" + tag,
        _take(hosts, "merge_fwd"))
    s["mo"], h1 = _mm_norm_res(s["merged"], w["w_out"], _vec(w["g_mix_post"]), h, "mix_out" + tag,
                               _take(hosts, "mix_out"))
    s["h1"] = h1
    s["fg"], s["fu"], s["act"], s["hn2"] = _ffn_in(h1, _vec(w["g_ffn_pre"]), w["w_ffn_in"], "ffn_in" + tag,
                                                   _take(hosts, "ffn_in"))
    s["f"], h2 = _mm_norm_res(s["act"], w["w_ffn_out"], _vec(w["g_ffn_post"]), h1, "ffn_out" + tag,
                              _take(hosts, "ffn_out"))
    s["h2"] = h2
    h3, s["q"], s["e"] = _ple_fwd(h2, p, w["w_ple_gate"], w["w_ple"], "ple_fwd" + tag, _take(hosts, "ple_fwd"))
    return h3, s


def _layer_bwd(dh3, p, w, s, li, hosts=None, big=None, gs=None):
    tag = "_l%d" % li
    d = D_MODEL
    gs = {} if gs is None else gs
    big = {} if big is None else big
    dh2, sq, dw_ple = _ple_bwd(dh3, s["q"], s["e"], w["w_ple_gate"], s["h2"], p, SQ.index("w_ple_gate"), len(SQ),
                               "ple_bwd" + tag)
    dff, gs["g_ffn_post"], dw_ffn_out = _ffn_out_bwd(
        dh2, s["f"], _vec(w["g_ffn_post"]), s["fg"], s["fu"], s["act"], w["w_ffn_out"], "ffn_out_bwd" + tag,
        _take(hosts, "ffn_out_bwd"))
    big["ffn_out"] = dw_ffn_out.reshape(N_CHIPS, 1, D_FF // N_CHIPS, d)
    n_ff = w["w_ffn_in"].shape[2]
    dh1, gs["g_ffn_pre"] = _in_bwd([(dff, 2 * D_FF // n_ff)], w["w_ffn_in"], n_ff, s["h1"], _vec(w["g_ffn_pre"]),
                                   dh2, ROW_TILE, "ffn_in_bwd" + tag, _take(hosts, "ffn_in_bwd"))
    big["ffn_in"] = _dw_cols(s["hn2"], [(dff, 2 * D_FF // n_ff)], n_ff, 1, "dw_ffn_in" + tag)
    branch_w = ("w_sgu_out", "w_conv_out", "w_pool_out")
    dzg, dsg, dcs, dps, gs["g_mix_post"], big["sq"] = _merge_bwd(
        dh1, s["mo"], _vec(w["g_mix_post"]), s["proj"], (s["bra"], s["brb"], s["brc"]), (s["sg"], s["cs"], s["ps"]),
        s["merged"], w["w_out"], [w[nm] for nm in branch_w], sq, [SQ.index(nm) for nm in ("w_out",) + branch_w],
        "merge_bwd" + tag, _take(hosts, "merge_bwd"))
    bs3 = w["b_sgu_s"].reshape(SGU_HEADS, SGU_BLOCK, 1)
    dz_sgu, gs["w_sgu_s"], dbs3, gs["g_sgu_v"], gs["b_sgu_v"] = _sgu_bwd(
        s["proj"], dsg, w["w_sgu_s"], bs3, _vec(w["g_sgu_v"]), _vec(w["b_sgu_v"]), "sgu_bwd" + tag,
        _take(hosts, "sgu_bwd"))
    gs["b_sgu_s"] = dbs3
    dcv, dwdw, gs["b_dw"], gs["g_conv_ln"], gs["b_conv_ln"] = _conv_bwd_norm(
        s["proj"], dcs, s["cv"], _vec(w["b_dw"]), _vec(w["g_conv_ln"]), _vec(w["b_conv_ln"]), "conv_bwd_norm" + tag,
        _take(hosts, "conv_bwd_norm"))
    dz_conv = _conv_bwd_taps(s["proj"], dcv, w["w_dw"], "conv_bwd_taps" + tag, _take(hosts, "conv_bwd_taps"))
    dz_pool, dwpool, gs["s_pool"] = _pool_bwd(s["proj"], dps, w["w_pool"], _vec(w["s_pool"]), "pool_bwd" + tag)
    pieces = [(dz_sgu, 2), (dz_conv, 2), (dz_pool, 1), (dzg, 3)]
    big["in"] = _dw_cols(s["hn"], pieces, d, 2, "dw_in" + tag)
    gple = dw_ple.reshape(256, N_CHIPS, 256).transpose(1, 0, 2)
    gpool = dwpool.reshape(4, N_CHIPS, 64, 256).transpose(1, 0, 2, 3).reshape(N_CHIPS, 256, 256)
    big["mix"] = jnp.stack([gple, gpool], axis=1)
    big["dw"] = dwdw.reshape(CONV_HALO, N_CHIPS, 256).transpose(1, 0, 2)[:, None]
    dh0, gs["g_mix_pre"] = _in_bwd(pieces, w["w_in"], d, s["h0"], _vec(w["g_mix_pre"]), dh1, ROW_TILE_HEAVY,
                                   "mix_in_bwd" + tag, _take(hosts, "mix_in_bwd"))
    return dh0, big, gs


GROUPS = ("in", "sq", "ffn_in", "ffn_out", "mix", "dw")
WIRE_DTYPE = {"in": BF16, "sq": BF16, "ffn_in": BF16, "ffn_out": BF16, "mix": BF16, "dw": F32}
GATHER_FIRST = ("in", "mix", "dw")
GATHER_RIDES = (("mix_in", "sgu_fwd", ("sq", "ffn_in"), ()),
                ("conv_fwd", "pool_fwd", ("ffn_out",), ("in",)),
                ("merge_fwd", "mix_out", (), ("sq",)),
                ("ffn_in", "ffn_out", (), ("ffn_in", "ffn_out", "mix", "dw")))
REDUCE_UPPER = ("ffn_out_bwd", (("ffn_in_bwd", ("in", "ffn_out")), ("merge_bwd", ("sq", "ffn_in", "mix", "dw"))))
REDUCE_OWN = ("sgu_bwd", (("conv_bwd_norm", ("ffn_in", "ffn_out")), ("conv_bwd_taps", ("sq",))))
REDUCE_LAST = ("in", "mix", "dw")


def _group_members(wts):
    n_layers = wts["w_in"].shape[0]
    dw = wts["w_dw"].reshape(n_layers, CONV_WIDTH, -1)
    return {"in": [wts["w_in"]], "sq": [wts[nm] for nm in SQ], "ffn_in": [wts["w_ffn_in"]],
            "ffn_out": [wts["w_ffn_out"]],
            "mix": [wts["w_ple"], wts["w_pool"].reshape(n_layers, POOL_GROUP, POOL_GROUP)],
            "dw": [jnp.pad(dw, ((0, 0), (0, CONV_HALO - CONV_WIDTH), (0, 0)))]}


class _Gather:
    PLACED, OVER_ICI, FULL = 0, 1, 2

    def __init__(self):
        self.buf, self.stage, self.pending = {}, {}, []

    def put(self, key, buf):
        self.buf[key], self.stage[key] = buf, self.PLACED

    def _flush(self):
        for keys, pay, stage in self.pending:
            if pay.results is not None:
                for key, res in zip(keys, pay.results):
                    self.buf[key], self.stage[key] = res, stage
        self.pending = [entry for entry in self.pending if entry[1].results is None]

    def _factory(self, make, keys, before, after):
        def factory():
            if not keys:
                return None
            self._flush()
            assert all(self.stage[k] == before for k in keys), (keys, self.stage)
            pay = make([self.buf[k] for k in keys])
            self.pending.append((keys, pay, after))
            return pay
        return factory

    def ici(self, keys):
        return self._factory(_gather_ici, keys, self.PLACED, self.OVER_ICI)

    def d2d(self, keys):
        return self._factory(_gather_d2d, keys, self.OVER_ICI, self.FULL)

    def get(self, li, group):
        self._flush()
        assert self.stage[(li, group)] == self.FULL, (li, group)
        return self.buf[(li, group)]


class _Reduce:
    def __init__(self, pos, n_layers):
        self.pos, self.n_layers, self.exchanged, self.stages = pos, n_layers, [], []

    def exchange(self, li, groups, grads):
        def factory():
            pay = _pair_exchange([_half_view(grads[g]) for g in groups])
            self.exchanged.append((li, list(groups), pay))
            return pay
        return factory

    def _received(self, li, group):
        for lj, groups, pay in self.exchanged:
            if lj == li and group in groups:
                return pay.results[groups.index(group)]
        raise KeyError((li, group))

    def chips(self, li, groups, grads):
        def factory():
            parts = [_pair_sum(_half_view(grads[g]), self._received(li, g), self.pos, WIRE_DTYPE[g],
                               "pair_sum_%s_l%d" % (g, li)) for g in groups]
            pay = _chip_exchange(parts)
            self.stages.append((li, groups, parts, pay))
            return pay
        return factory

    def finish(self):
        reduced = {}
        for li, groups, parts, pay in self.stages:
            for g, part, landed in zip(groups, parts, pay.results):
                reduced[g] = _chip_sum(part, landed, reduced.get(g), li, self.n_layers, self.pos,
                                       "chip_sum_%s_l%d" % (g, li))
        return reduced


def _pack_small(tree):
    flat = jnp.concatenate([tree[nm].reshape(-1).astype(F32) for nm in SMALL])
    return flat.reshape(-1, 128)


def _unpack_small(packed, like):
    out, off = {}, 0
    flat = packed.reshape(-1)
    for nm in SMALL:
        n = like[nm].size
        out[nm] = flat[off:off + n].reshape(like[nm].shape)
        off += n
    return out


WEIGHTS = ("g_mix_pre", "w_in", "w_sgu_s", "b_sgu_s", "g_sgu_v", "b_sgu_v", "w_sgu_out", "w_dw", "b_dw", "g_conv_ln",
           "b_conv_ln", "w_conv_out", "w_pool", "s_pool", "w_pool_out", "w_out", "g_mix_post", "g_ffn_pre",
           "w_ffn_in", "w_ffn_out", "g_ffn_post", "w_ple", "w_ple_gate")


def kernel(x, p, g_mix_pre, w_in, w_sgu_s, b_sgu_s, g_sgu_v, b_sgu_v, w_sgu_out, w_dw, b_dw, g_conv_ln, b_conv_ln, w_conv_out, w_pool, s_pool, w_pool_out, w_out, g_mix_post, g_ffn_pre, w_ffn_in, w_ffn_out, g_ffn_post, w_ple, w_ple_gate, loss_target, m_g_mix_pre, m_w_in, m_w_sgu_s, m_b_sgu_s, m_g_sgu_v, m_b_sgu_v, m_w_sgu_out, m_w_dw, m_b_dw, m_g_conv_ln, m_b_conv_ln, m_w_conv_out, m_w_pool, m_s_pool, m_w_pool_out, m_w_out, m_g_mix_post, m_g_ffn_pre, m_w_ffn_in, m_w_ffn_out, m_g_ffn_post, m_w_ple, m_w_ple_gate, v_g_mix_pre, v_w_in, v_w_sgu_s, v_b_sgu_s, v_g_sgu_v, v_b_sgu_v, v_w_sgu_out, v_w_dw, v_b_dw, v_g_conv_ln, v_b_conv_ln, v_w_conv_out, v_w_pool, v_s_pool, v_w_pool_out, v_w_out, v_g_mix_post, v_g_ffn_pre, v_w_ffn_in, v_w_ffn_out, v_g_ffn_post, v_w_ple, v_w_ple_gate):
    args = dict(locals())
    wts = {nm: args[nm] for nm in WEIGHTS}
    mom = {nm: args["m_" + nm] for nm in WEIGHTS}
    var = {nm: args["v_" + nm] for nm in WEIGHTS}
    n_layers = w_in.shape[0]
    h = x.reshape(x.shape[1:])
    target = loss_target.reshape(loss_target.shape[1:])
    cx, cy, core = lax.axis_index("x"), lax.axis_index("y"), lax.axis_index("c")
    pos = jnp.stack([2 * cx + cy, 2 * (1 - cx) + cy, 2 * cx + (1 - cy), 2 * (1 - cx) + (1 - cy), core,
                     4 * cx + 2 * cy + core])
    pos = pos.astype(jnp.int32)

    members = _group_members(wts)
    gather = _Gather()
    for li in range(n_layers):
        for g in GROUPS:
            gather.put((li, g), _place(members[g], li, pos, WIRE_DTYPE[g], "place_%s_l%d" % (g, li)))
    first = [(0, g) for g in GATHER_FIRST]
    _run_payload(gather.ici(first)(), "gather_ici_first")
    _run_payload(gather.d2d(first)(), "gather_d2d_first")

    saved, layer_w = [], []
    for li in range(n_layers):
        hosts = {}
        for ici_host, d2d_host, own, nxt in GATHER_RIDES:
            keys = [(li, g) for g in own if li == 0] + [(li + 1, g) for g in nxt if li + 1 < n_layers]
            hosts[ici_host], hosts[d2d_host] = gather.ici(keys), gather.d2d(keys)
        w = _LayerWeights(functools.partial(gather.get, li), wts, li)
        layer_w.append(w)
        h, s = _layer_fwd(h, p[li, 0], w, li, hosts)
        saved.append(s)
    dh, sq_err = _loss_head(h, target, "loss_head")
    loss = lax.psum(sq_err[0, 0] * (0.5 / D_MODEL), ("x", "y", "c"))

    reduce = _Reduce(pos, n_layers)
    small_grads = [{} for _ in range(n_layers)]
    late = (0, SMALL[0])
    small = {}

    def small_vec():
        def leaf(li, nm):
            shape = wts[nm].shape[1:]
            return jnp.zeros(shape, F32) if (li, nm) == late else small_grads[li][nm].reshape(shape)
        return _pack_small({nm: jnp.stack([leaf(li, nm) for li in range(n_layers)], axis=0) for nm in SMALL})

    upper = None
    for li in reversed(range(n_layers)):
        own = {}
        hosts = {}
        plans = [(REDUCE_UPPER, li + 1, upper)] if upper is not None else []
        if li == 0:
            plans.append((REDUCE_OWN, 0, own))

            def last_rides(own=own):
                _run_payload(reduce.exchange(0, REDUCE_LAST, own)(), "pair_exchange_last")
                small["vec"] = small_vec()
                small["exchange"] = _small_exchange(small["vec"])
                return _join(reduce.chips(0, REDUCE_LAST, own)(), small["exchange"])
            hosts["mix_in_bwd"] = last_rides
        for (pair_host, chip_hosts), lj, grads in plans:
            groups = [g for _, gs_ in chip_hosts for g in gs_]
            hosts[pair_host] = reduce.exchange(lj, groups, grads)
            for chip_host, gs_ in chip_hosts:
                hosts[chip_host] = reduce.chips(lj, gs_, grads)
        dh, upper, _ = _layer_bwd(dh, p[li, 0], layer_w[li], saved[li], li, hosts, own, small_grads[li])
    grad_x = dh[None]
    reduced = reduce.finish()

    shared = _run_payload(_pair_share([reduced[g] for g in GROUPS]), "pair_share")
    red = {g: b.reshape(b.shape[:2] + (-1, b.shape[-1])) for g, b in zip(GROUPS, shared)}

    where = {"w_in": ("in", 0), "w_ffn_in": ("ffn_in", 0), "w_ffn_out": ("ffn_out", 0), "w_ple": ("mix", 0),
             "w_pool": ("mix", 1)}
    for slot, nm in enumerate(SQ):
        where[nm] = ("sq", slot)
    outs = {}
    for nm, (g, slot) in where.items():
        shape = wts[nm].shape
        to3 = lambda a: a.reshape((n_layers,) + red[g].shape[2:])
        res = _adamw(to3(wts[nm]), red[g], slot, to3(mom[nm]), to3(var[nm]), "adamw_" + nm)
        outs[nm] = [r.reshape(shape) for r in res]
    gdw = red["dw"][:, :, :CONV_WIDTH]
    to3 = lambda a: a.reshape(n_layers, CONV_WIDTH, -1)
    res = _adamw(to3(wts["w_dw"]), gdw, 0, to3(mom["w_dw"]), to3(var["w_dw"]), "adamw_w_dw")
    outs["w_dw"] = [r.reshape(wts["w_dw"].shape) for r in res]

    gmain = _small_sum(small["vec"], small["exchange"].results[0], pos, "small_sum")
    glate = _all_reduce_small(small_grads[late[0]][late[1]].reshape(-1, 128), "all_reduce_late")
    gsmall = jnp.concatenate([glate, gmain[glate.shape[0]:]], axis=0)
    pk = lambda tree: _pack_small({nm: tree[nm] for nm in SMALL})[None]
    res = _adamw(pk(wts), gsmall[None, None], 0, pk(mom), pk(var), "adamw_small")
    unpacked = [_unpack_small(r[0], wts) for r in res]
    for nm in SMALL:
        outs[nm] = [u[nm] for u in unpacked]

    result = [loss, grad_x]
    for k in range(4):
        result += [outs[nm][k] for nm in WEIGHTS]
    return tuple(result)
```

```python
import functools

import jax
import jax.numpy as jnp
from jax import lax
from jax.experimental import pallas as pl
from jax.experimental.pallas import tpu as pltpu

F32 = jnp.float32
BF16 = jnp.bfloat16
MESH = pl.DeviceIdType.MESH

EPS = 1e-6
D_MODEL = 1024
SGU_BLOCK = 128
SGU_HEADS = 8
CHUNK = 64
CONV_WIDTH = 31
CONV_HALO = 32
POOL_WINDOWS = (2, 4, 8, 16)
POOL_BLOCK = 128
POOL_GROUP = 256
D_FF = 2816
N_CHIPS = 4

ADAM_LR = 0.001
ADAM_B1 = 0.9
ADAM_B2 = 0.999
ADAM_EPS = 1e-08
ADAM_WD = 0.01
ADAM_STEP = 10

VMEM_LIMIT = 52 * 1024 * 1024
ROW_TILE = 512
ROW_TILE_HEAVY = 256
CONV_ROWS = 64
CONV_LANES = 128
EW_BLOCK_BYTES = 2 * 1024 * 1024
TOKEN_TILE = 2048
FF_CHUNK = 256


def _params(n_grid):
    return pltpu.CompilerParams(dimension_semantics=("arbitrary",) * n_grid, vmem_limit_bytes=VMEM_LIMIT)


def _dot(a, b):
    return jnp.dot(a.astype(BF16), b.astype(BF16), preferred_element_type=F32)


def _dot_nt(a, b):
    return lax.dot_general(a.astype(BF16), b.astype(BF16), (((1,), (1,)), ((), ())), preferred_element_type=F32)


def _dot_tn(a, b):
    return lax.dot_general(a.astype(BF16), b.astype(BF16), (((0,), (0,)), ((), ())), preferred_element_type=F32)


def _sigmoid(x):
    return 0.5 * jnp.tanh(0.5 * x) + 0.5


_GELU_C = 0.7978845608028654
_GELU_A = 0.044715


def _gelu(x):
    t = jnp.tanh(_GELU_C * (x + _GELU_A * x * x * x))
    return 0.5 * x * (1.0 + t)


def _gelu_and_grad(x):
    x2 = x * x
    t = jnp.tanh(_GELU_C * (x + _GELU_A * x2 * x))
    g = 0.5 * (1.0 + t) + 0.5 * x * (1.0 - t * t) * (_GELU_C * (1.0 + 3.0 * _GELU_A * x2))
    return 0.5 * x * (1.0 + t), g


def _rms_stats(x):
    r = lax.rsqrt(jnp.mean(x * x, axis=-1, keepdims=True) + EPS)
    return x * r, r


def _rms_bwd(xn, r, g, dy):
    gd = dy * g
    return r * (gd - xn * jnp.mean(gd * xn, axis=-1, keepdims=True)), dy * xn


def _ln_stats(x):
    mu = jnp.mean(x, axis=-1, keepdims=True)
    xc = x - mu
    rstd = lax.rsqrt(jnp.mean(xc * xc, axis=-1, keepdims=True) + EPS)
    return xc * rstd, rstd


def _ln_bwd(xhat, rstd, g, dy):
    dxh = dy * g
    return rstd * (dxh - jnp.mean(dxh, axis=-1, keepdims=True) - xhat * jnp.mean(dxh * xhat, axis=-1, keepdims=True))


def _rowsum(x):
    return jnp.sum(x, axis=0, keepdims=True)


def _tile(t, want):
    return min(t, want)


def _full(shape):
    n = len(shape)
    return pl.BlockSpec(shape, lambda *_: (0,) * n)


def _resident(shape):
    n = len(shape)
    return pl.BlockSpec(shape, lambda *_: (0,) * n, pipeline_mode=pl.Buffered(1))


class _Payload:
    def __init__(self, operands, out_shape, aliases, scratch, start, finish):
        self.operands, self.out_shape, self.aliases, self.scratch = list(operands), list(out_shape), aliases, scratch
        self.start, self.finish = start, finish
        self.results = None
        self.parts = None

    def deliver(self, results):
        self.results = list(results)
        if self.parts:
            a, b, ma = self.parts
            a.deliver(self.results[:ma])
            b.deliver(self.results[ma:])


def _pcall(body, *, name, grid, in_specs, out_specs, out_shape, operands, scratch_shapes=(), comm=None, aliases=None):
    single = not isinstance(out_shape, (list, tuple))
    out_specs = [out_specs] if single else list(out_specs)
    out_shape = [out_shape] if single else list(out_shape)
    aliases = dict(aliases or {})
    if comm is None:
        res = pl.pallas_call(
            body, name=name, grid=grid, in_specs=list(in_specs), out_specs=out_specs, out_shape=out_shape,
            scratch_shapes=list(scratch_shapes), input_output_aliases=aliases,
            compiler_params=_params(len(grid)))(*operands)
        return res[0] if single else res
    n_in, n_out, n_scr = len(in_specs), len(out_shape), len(scratch_shapes)
    ci, co = len(comm.operands), len(comm.out_shape)

    def hosted(*refs):
        bounds = [0, n_in, n_in + ci, n_in + ci + n_out, n_in + ci + n_out + co, n_in + ci + n_out + co + n_scr]
        a, b, c_, d_, s_ = [refs[lo:hi] for lo, hi in zip(bounds[:-1], bounds[1:])]
        t_ = refs[bounds[-1]:]
        ids = [pl.program_id(q) for q in range(len(grid))]
        first = functools.reduce(jnp.logical_and, [i == 0 for i in ids])
        last = functools.reduce(jnp.logical_and, [i == pl.num_programs(q) - 1 for q, i in enumerate(ids)])

        @pl.when(first)
        def _():
            comm.start(b, d_, t_)

        body(*a, *c_, *s_)

        @pl.when(last)
        def _():
            comm.finish(b, d_, t_)

    res = pl.pallas_call(
        hosted, name=name, grid=grid, in_specs=list(in_specs) + [ANY] * ci, out_specs=out_specs + [ANY] * co,
        out_shape=out_shape + comm.out_shape, scratch_shapes=list(scratch_shapes) + list(comm.scratch),
        input_output_aliases={**aliases, **{n_in + i: n_out + o for i, o in comm.aliases.items()}},
        compiler_params=pltpu.CompilerParams(dimension_semantics=("arbitrary",) * len(grid),
                                             vmem_limit_bytes=VMEM_LIMIT, has_side_effects=True),
    )(*operands, *comm.operands)
    comm.deliver(res[n_out:])
    res = res[:n_out]
    return res[0] if single else res


def _run_payload(comm, name):
    ci, co = len(comm.operands), len(comm.out_shape)

    def body(*refs):
        b, d_, t_ = refs[:ci], refs[ci:ci + co], refs[ci + co:]
        comm.start(b, d_, t_)
        comm.finish(b, d_, t_)

    res = pl.pallas_call(
        body, name=name, in_specs=[ANY] * ci, out_specs=[ANY] * co, out_shape=comm.out_shape,
        input_output_aliases=dict(comm.aliases), scratch_shapes=list(comm.scratch),
        compiler_params=pltpu.CompilerParams(has_side_effects=True))(*comm.operands)
    comm.deliver(res)
    return comm.results


def _take(hosts, key):
    return hosts[key]() if hosts and key in hosts else None


def _norm_mm(h, g, w4, name, comm=None):
    t, d = h.shape
    n = w4.shape[2]
    tm = _tile(t, ROW_TILE)

    step = _lane_block(n, 1024)

    def body(h_ref, g_ref, w_ref, o_ref, hn_ref):
        xn, _ = _rms_stats(h_ref[...])
        hn = (xn * g_ref[...]).astype(BF16)
        hn_ref[...] = hn
        for j in range(N_CHIPS):
            for c0 in range(0, n, step):
                o_ref[:, j * n + c0:j * n + c0 + step] = jnp.dot(
                    hn, w_ref[j, :, c0:c0 + step], preferred_element_type=F32).astype(BF16)

    return _pcall(
        body, name=name, grid=(t // tm,),
        in_specs=[pl.BlockSpec((tm, d), lambda i: (i, 0)), _full((1, d)), _resident(w4.shape)],
        out_specs=[pl.BlockSpec((tm, N_CHIPS * n), lambda i: (i, 0)), pl.BlockSpec((tm, d), lambda i: (i, 0))],
        out_shape=[jax.ShapeDtypeStruct((t, N_CHIPS * n), BF16), jax.ShapeDtypeStruct((t, d), BF16)],
        operands=(h, g, w4), comm=comm)


def _sgu_mask():
    ii = lax.broadcasted_iota(jnp.int32, (SGU_BLOCK, SGU_BLOCK), 0) // CHUNK
    jj = lax.broadcasted_iota(jnp.int32, (SGU_BLOCK, SGU_BLOCK), 1) // CHUNK
    return jj <= ii


def _sgu_fwd(proj, wm, bs3, gv, bv, name, comm=None):
    t = proj.shape[0]
    d = D_MODEL
    tm = _tile(t, ROW_TILE)
    hd = d // SGU_HEADS

    def body(zu_ref, zv_ref, wm_ref, bs_ref, gv_ref, bv_ref, o_ref):
        mask = _sgu_mask()
        for blk in range(tm // SGU_BLOCK):
            rows = pl.ds(blk * SGU_BLOCK, SGU_BLOCK)
            u = _gelu(zu_ref[rows, :].astype(F32))
            xhat, _ = _ln_stats(_gelu(zv_ref[rows, :].astype(F32)))
            vn = (xhat * gv_ref[...] + bv_ref[...]).astype(BF16)
            for hh in range(SGU_HEADS):
                cols = slice(hh * hd, (hh + 1) * hd)
                wmh = jnp.where(mask, wm_ref[hh], 0.0).astype(BF16)
                mixed = jnp.dot(wmh, vn[:, cols], preferred_element_type=F32) + bs_ref[hh]
                o_ref[rows, cols] = (u[:, cols] * mixed).astype(BF16)

    return _pcall(
        body, name=name, grid=(t // tm,),
        in_specs=[pl.BlockSpec((tm, d), lambda i: (i, 0)), pl.BlockSpec((tm, d), lambda i: (i, 1)),
                  _full(wm.shape), _full(bs3.shape), _full(gv.shape), _full(bv.shape)],
        out_specs=pl.BlockSpec((tm, d), lambda i: (i, 0)),
        out_shape=jax.ShapeDtypeStruct((t, d), BF16),
        operands=(proj, proj, wm, bs3, gv, bv), comm=comm)


def _conv_taps(scr_ref, r0, c0, base, weight):
    n = CONV_ROWS + CONV_HALO
    win = scr_ref[pl.ds(r0, n), pl.ds(c0, CONV_LANES)]
    acc = None
    for r in range(8):
        rolled = win if r == 0 else pltpu.roll(win, n - r, 0)
        for q in range((CONV_HALO + 7) // 8 + 1):
            k = 8 * q + r - base
            if 0 <= k < CONV_WIDTH and 8 * q + CONV_ROWS <= n:
                term = weight(k) * rolled[8 * q:8 * q + CONV_ROWS]
                acc = term if acc is None else acc + term
    return acc


def _glu_rows(a_ref, g_ref):
    return a_ref[...].astype(F32) * _sigmoid(g_ref[...].astype(F32))


def _conv_into(scr_ref, cv_ref, w_ref, tm, base, flip):
    def chunk(ci, carry):
        r0 = pl.multiple_of(ci * CONV_ROWS, CONV_ROWS)
        for c0 in range(0, D_MODEL, CONV_LANES):
            def weight(k, c0=c0):
                kk = CONV_WIDTH - 1 - k if flip else k
                return w_ref[kk:kk + 1, c0:c0 + CONV_LANES]
            cv_ref[pl.ds(r0, CONV_ROWS), pl.ds(c0, CONV_LANES)] = _conv_taps(scr_ref, r0, c0, base, weight)
        return carry

    lax.fori_loop(0, tm // CONV_ROWS, chunk, 0)


def _conv_specs(t, tm, d):
    hb = tm // CONV_HALO
    main = [pl.BlockSpec((tm, d), lambda i: (i, 2)), pl.BlockSpec((tm, d), lambda i: (i, 3))]
    halo = [pl.BlockSpec((CONV_HALO, d), lambda i: (jnp.maximum(i * hb - 1, 0), 2)),
            pl.BlockSpec((CONV_HALO, d), lambda i: (jnp.maximum(i * hb - 1, 0), 3))]
    return main, halo


def _fill_glu_history(scr_ref, a_ref, g_ref, ah_ref, gh_ref, tm):
    hist = _glu_rows(ah_ref, gh_ref)
    scr_ref[0:CONV_HALO, :] = jnp.where(pl.program_id(0) > 0, hist, 0.0)
    scr_ref[CONV_HALO:CONV_HALO + tm, :] = _glu_rows(a_ref, g_ref)


_CONV_BASE = CONV_HALO - (CONV_WIDTH - 1)


def _conv_fwd(proj, wdw, bdw, gln, bln, name, comm=None):
    t = proj.shape[0]
    d = D_MODEL
    tm = _tile(t, ROW_TILE)
    main, halo = _conv_specs(t, tm, d)

    def body(a_ref, g_ref, ah_ref, gh_ref, w_ref, b_ref, gl_ref, bl_ref, o_ref, cv_ref, scr_ref):
        _fill_glu_history(scr_ref, a_ref, g_ref, ah_ref, gh_ref, tm)
        _conv_into(scr_ref, cv_ref, w_ref, tm, _CONV_BASE, False)
        xhat, _ = _ln_stats(cv_ref[...] + b_ref[...])
        cn = xhat * gl_ref[...] + bl_ref[...]
        o_ref[...] = (cn * _sigmoid(cn)).astype(BF16)

    row = pl.BlockSpec((tm, d), lambda i: (i, 0))
    return _pcall(
        body, name=name, grid=(t // tm,),
        in_specs=main + halo + [_full(wdw.shape), _full(bdw.shape), _full(gln.shape), _full(bln.shape)],
        out_specs=[row, row],
        out_shape=[jax.ShapeDtypeStruct((t, d), BF16), jax.ShapeDtypeStruct((t, d), F32)],
        scratch_shapes=[pltpu.VMEM((tm + CONV_HALO, d), F32)],
        operands=(proj, proj, proj, proj, wdw, bdw, gln, bln), comm=comm)


def _pool_fill(scr_ref, z_ref, zh_ref, tm):
    scr_ref[0:POOL_BLOCK, :] = jnp.where(pl.program_id(0) > 0, zh_ref[...], jnp.zeros_like(zh_ref))
    scr_ref[POOL_BLOCK:POOL_BLOCK + tm, :] = z_ref[...]


def _pool_count(t0, rows, w):
    pos = (t0 + lax.broadcasted_iota(jnp.int32, (rows, 1), 0) + 1).astype(F32)
    return jnp.minimum(pos, float(w))


def _band(w, leading):
    i = lax.broadcasted_iota(jnp.int32, (POOL_BLOCK, 2 * POOL_BLOCK), 0)
    j = lax.broadcasted_iota(jnp.int32, (POOL_BLOCK, 2 * POOL_BLOCK), 1)
    off = j - i if leading else POOL_BLOCK + i - j
    return jnp.where((off >= 0) & (off < w), 1.0, 0.0).astype(BF16)


def _window_sums(refs, band, tm, cols):
    blocks = []
    for b in range(tm // POOL_BLOCK):
        rows = pl.ds(b * POOL_BLOCK, 2 * POOL_BLOCK)
        acc = None
        for ref in refs:
            term = jnp.dot(band, ref[rows, cols], preferred_element_type=F32)
            acc = term if acc is None else acc + term
        blocks.append(acc)
    return blocks[0] if len(blocks) == 1 else jnp.concatenate(blocks, axis=0)


def _pooled_group(scr_ref, gi, w, tm, t0):
    cols = pl.ds(gi * POOL_GROUP, POOL_GROUP)
    sums = _window_sums([scr_ref], _band(w, False), tm, cols)
    return sums / _pool_count(t0, tm, w) - scr_ref[pl.ds(POOL_BLOCK, tm), cols].astype(F32)


def _pool_specs(tm, d):
    hb = tm // POOL_BLOCK
    return [pl.BlockSpec((tm, d), lambda i: (i, 4)),
            pl.BlockSpec((POOL_BLOCK, d), lambda i: (jnp.maximum(i * hb - 1, 0), 4))]


def _pool_fwd(proj, wpool, spool, name, comm=None):
    t = proj.shape[0]
    d = D_MODEL
    tm = _tile(t, ROW_TILE)

    def body(z_ref, zh_ref, w_ref, s_ref, o_ref, scr_ref):
        _pool_fill(scr_ref, z_ref, zh_ref, tm)
        t0 = pl.program_id(0) * tm
        for gi, w in enumerate(POOL_WINDOWS):
            cols = slice(gi * POOL_GROUP, (gi + 1) * POOL_GROUP)
            pooled = _pooled_group(scr_ref, gi, w, tm, t0)
            o_ref[:, cols] = (_dot(pooled, w_ref[gi]) * s_ref[:, cols]).astype(BF16)

    return _pcall(
        body, name=name, grid=(t // tm,),
        in_specs=_pool_specs(tm, d) + [_full(wpool.shape), _full(spool.shape)],
        out_specs=pl.BlockSpec((tm, d), lambda i: (i, 0)),
        out_shape=jax.ShapeDtypeStruct((t, d), BF16),
        scratch_shapes=[pltpu.VMEM((tm + POOL_BLOCK, d), BF16)],
        operands=(proj, proj, wpool, spool), comm=comm)


def _merge_fwd(proj, sg, cs, ps, wa, wb, wc, name, comm=None):
    t = proj.shape[0]
    d = D_MODEL
    tm = _tile(t, ROW_TILE_HEAVY)

    def body(za_ref, zb_ref, zc_ref, sg_ref, cs_ref, ps_ref, wa_ref, wb_ref, wc_ref, ba_ref, bb_ref, bc_ref, m_ref):
        merged = None
        for z_ref, x_ref, w_ref, b_ref in ((za_ref, sg_ref, wa_ref, ba_ref), (zb_ref, cs_ref, wb_ref, bb_ref),
                                           (zc_ref, ps_ref, wc_ref, bc_ref)):
            br = jnp.dot(x_ref[...], w_ref[...], preferred_element_type=F32)
            b_ref[...] = br.astype(BF16)
            term = _sigmoid(z_ref[...].astype(F32)) * br
            merged = term if merged is None else merged + term
        m_ref[...] = merged.astype(BF16)

    row = pl.BlockSpec((tm, d), lambda i: (i, 0))
    wspec = _resident((d, d))
    return _pcall(
        body, name=name, grid=(t // tm,),
        in_specs=[pl.BlockSpec((tm, d), lambda i: (i, 5)), pl.BlockSpec((tm, d), lambda i: (i, 6)),
                  pl.BlockSpec((tm, d), lambda i: (i, 7)), row, row, row, wspec, wspec, wspec],
        out_specs=[row, row, row, row],
        out_shape=[jax.ShapeDtypeStruct((t, d), BF16)] * 4,
        operands=(proj, proj, proj, sg, cs, ps, wa, wb, wc), comm=comm)


def _mm_norm_res(a, w, g, hres, name, comm=None):
    t, k = a.shape
    d = w.shape[1]
    tm = _tile(t, ROW_TILE)

    def body(a_ref, w_ref, g_ref, h_ref, y_ref, o_ref):
        y = jnp.dot(a_ref[...], w_ref[...], preferred_element_type=F32)
        y_ref[...] = y
        yn, _ = _rms_stats(y)
        o_ref[...] = h_ref[...] + yn * g_ref[...]

    row = pl.BlockSpec((tm, d), lambda i: (i, 0))
    return _pcall(
        body, name=name, grid=(t // tm,),
        in_specs=[pl.BlockSpec((tm, k), lambda i: (i, 0)), _resident(w.shape), _full(g.shape), row],
        out_specs=[row, row],
        out_shape=[jax.ShapeDtypeStruct((t, d), F32)] * 2,
        operands=(a, w, g, hres), comm=comm)


def _ffn_in(h, g, w4, name, comm=None):
    t, d = h.shape
    n = w4.shape[2]
    tm = _tile(t, ROW_TILE)
    nj = D_FF // n

    def body(h_ref, g_ref, w_ref, fg_ref, fu_ref, act_ref, hn_ref):
        xn, _ = _rms_stats(h_ref[...])
        hn = (xn * g_ref[...]).astype(BF16)
        hn_ref[...] = hn
        for j in range(nj):
            cols = slice(j * n, (j + 1) * n)
            fg = jnp.dot(hn, w_ref[j], preferred_element_type=F32)
            fu = jnp.dot(hn, w_ref[j + nj], preferred_element_type=F32)
            fg_ref[:, cols] = fg.astype(BF16)
            fu_ref[:, cols] = fu.astype(BF16)
            act_ref[:, cols] = (fg * _sigmoid(fg) * fu).astype(BF16)

    wide = pl.BlockSpec((tm, D_FF), lambda i: (i, 0))
    return _pcall(
        body, name=name, grid=(t // tm,),
        in_specs=[pl.BlockSpec((tm, d), lambda i: (i, 0)), _full((1, d)), _resident(w4.shape)],
        out_specs=[wide, wide, wide, pl.BlockSpec((tm, d), lambda i: (i, 0))],
        out_shape=[jax.ShapeDtypeStruct((t, D_FF), BF16)] * 3 + [jax.ShapeDtypeStruct((t, d), BF16)],
        operands=(h, g, w4), comm=comm)


def _ple_fwd(h, p, wg, wp, name, comm=None):
    t, d = h.shape
    tm = _tile(t, ROW_TILE)

    def body(h_ref, p_ref, wg_ref, wp_ref, o_ref, q_ref, e_ref):
        hh = h_ref[...]
        q = _dot(hh, wg_ref[...])
        e = _dot(p_ref[...], wp_ref[...])
        q_ref[...] = q.astype(BF16)
        e_ref[...] = e.astype(BF16)
        o_ref[...] = hh + _sigmoid(q) * e

    row = pl.BlockSpec((tm, d), lambda i: (i, 0))
    return _pcall(
        body, name=name, grid=(t // tm,),
        in_specs=[row, pl.BlockSpec((tm, p.shape[1]), lambda i: (i, 0)), _resident(wg.shape), _resident(wp.shape)],
        out_specs=[row, row, row],
        out_shape=[jax.ShapeDtypeStruct((t, d), F32), jax.ShapeDtypeStruct((t, d), BF16),
                   jax.ShapeDtypeStruct((t, d), BF16)],
        operands=(h, p, wg, wp), comm=comm)


def _loss_head(y, target, name):
    t, d = y.shape
    tm = _tile(t, ROW_TILE)

    def body(y_ref, t_ref, dy_ref, l_ref):
        @pl.when(pl.program_id(0) == 0)
        def _():
            l_ref[...] = jnp.zeros_like(l_ref)

        err = y_ref[...] - t_ref[...]
        dy_ref[...] = err * (1.0 / d)
        l_ref[...] += jnp.sum(err * err, keepdims=True)[:, :1] * jnp.ones((1, 128), F32)

    row = pl.BlockSpec((tm, d), lambda i: (i, 0))
    return pl.pallas_call(
        body, name=name, grid=(t // tm,),
        in_specs=[row, row], out_specs=[row, _full((1, 128))],
        out_shape=[jax.ShapeDtypeStruct((t, d), F32), jax.ShapeDtypeStruct((1, 128), F32)],
        compiler_params=_params(1))(y, target)


def _flush_slots(acc_ref, out_ref, slots, sem_ref):
    rows = out_ref.shape[2]

    @pl.when(pl.program_id(0) == pl.num_programs(0) - 1)
    def _():
        copies = [pltpu.make_async_copy(acc_ref.at[k, pl.ds(j * rows, rows)], out_ref.at[j, slot],
                                        sem_ref.at[N_CHIPS * k + j])
                  for k, slot in enumerate(slots) for j in range(N_CHIPS)]
        for cp in copies:
            cp.start()
        for cp in copies:
            cp.wait()


def _ple_bwd(dh, q, e, wg, h_in, p, slot, n_slots, name):
    t, d = dh.shape
    tm = _tile(t, ROW_TILE)
    rows = d // N_CHIPS

    def body(dh_ref, q_ref, e_ref, wg_ref, h_ref, p_ref, o_ref, sq_ref, dwp_ref, acc_ref, sem_ref):
        @pl.when(pl.program_id(0) == 0)
        def _():
            acc_ref[...] = jnp.zeros_like(acc_ref)
            dwp_ref[...] = jnp.zeros_like(dwp_ref)

        dh_ = dh_ref[...]
        s = _sigmoid(q_ref[...].astype(F32))
        dq = (dh_ * e_ref[...].astype(F32) * s * (1.0 - s)).astype(BF16)
        o_ref[...] = dh_ + _dot_nt(dq, wg_ref[...])
        acc_ref[0] += _dot_tn(h_ref[...], dq)
        dwp_ref[...] += _dot_tn(p_ref[...], dh_ * s)
        _flush_slots(acc_ref, sq_ref, (slot,), sem_ref)

    row = pl.BlockSpec((tm, d), lambda i: (i, 0))
    return _pcall(
        body, name=name, grid=(t // tm,),
        in_specs=[row, row, row, _resident(wg.shape), row, pl.BlockSpec((tm, p.shape[1]), lambda i: (i, 0))],
        out_specs=[row, ANY, _full((p.shape[1], d))],
        out_shape=[jax.ShapeDtypeStruct((t, d), F32), jax.ShapeDtypeStruct((N_CHIPS, n_slots, rows, d), F32),
                   jax.ShapeDtypeStruct((p.shape[1], d), F32)],
        scratch_shapes=[pltpu.VMEM((1, d, d), F32), pltpu.SemaphoreType.DMA((N_CHIPS,))],
        operands=(dh, q, e, wg, h_in, p))


def _ffn_out_bwd(dh, f, g, fg, fu, act, w, name, comm=None):
    t, d = dh.shape
    tm = _tile(t, ROW_TILE_HEAVY)

    def body(dh_ref, f_ref, g_ref, fg_ref, fu_ref, act_ref, w_ref, dff_ref, dg_ref, dw_ref, acc_ref, sem_ref):
        @pl.when(pl.program_id(0) == 0)
        def _():
            dg_ref[...] = jnp.zeros_like(dg_ref)
            acc_ref[...] = jnp.zeros_like(acc_ref)

        fn, r = _rms_stats(f_ref[...])
        df, dgt = _rms_bwd(fn, r, g_ref[...], dh_ref[...])
        dg_ref[...] += _rowsum(dgt)
        df = df.astype(BF16)
        acc_ref[...] += _dot_tn(act_ref[...], df)
        for c0 in range(0, D_FF, FF_CHUNK):
            cols = slice(c0, c0 + FF_CHUNK)
            dact = _dot_nt(df, w_ref[cols, :])
            fg_ = fg_ref[:, cols].astype(F32)
            s = _sigmoid(fg_)
            gs = fg_ * s
            dff_ref[:, cols] = (dact * fu_ref[:, cols].astype(F32) * (s + gs - gs * s)).astype(BF16)
            dff_ref[:, D_FF + c0:D_FF + c0 + FF_CHUNK] = (dact * gs).astype(BF16)

        @pl.when(pl.program_id(0) == pl.num_programs(0) - 1)
        def _():
            cp = pltpu.make_async_copy(acc_ref, dw_ref, sem_ref.at[0])
            cp.start()
            cp.wait()

    row = pl.BlockSpec((tm, d), lambda i: (i, 0))
    wide = pl.BlockSpec((tm, D_FF), lambda i: (i, 0))
    return _pcall(
        body, name=name, grid=(t // tm,),
        in_specs=[row, row, _full(g.shape), wide, wide, wide, _resident(w.shape)],
        out_specs=[pl.BlockSpec((tm, 2 * D_FF), lambda i: (i, 0)), _full((1, d)), ANY],
        out_shape=[jax.ShapeDtypeStruct((t, 2 * D_FF), BF16), jax.ShapeDtypeStruct((1, d), F32),
                   jax.ShapeDtypeStruct((D_FF, d), F32)],
        scratch_shapes=[pltpu.VMEM((D_FF, d), F32), pltpu.SemaphoreType.DMA((1,))],
        operands=(dh, f, g, fg, fu, act, w), comm=comm)


def _in_bwd(pieces, w4, unit, h, g, dres, tm, name, comm=None):
    t, d = h.shape
    tm = _tile(t, tm)
    per_chunk = w4.shape[2] // unit
    n_p = len(pieces)

    def body(*refs):
        p_refs = refs[:n_p]
        w_ref, h_ref, g_ref, r_ref, o_ref, dg_ref = refs[n_p:]

        @pl.when(pl.program_id(0) == 0)
        def _():
            dg_ref[...] = jnp.zeros_like(dg_ref)

        acc = None
        u = 0
        for p_ref, (_, nu) in zip(p_refs, pieces):
            for k in range(nu):
                lanes = slice((u % per_chunk) * unit, (u % per_chunk + 1) * unit)
                term = _dot_nt(p_ref[:, k * unit:(k + 1) * unit], w_ref[u // per_chunk, :, lanes])
                acc = term if acc is None else acc + term
                u += 1
        xn, r = _rms_stats(h_ref[...])
        dx, dgt = _rms_bwd(xn, r, g_ref[...], acc)
        dg_ref[...] += _rowsum(dgt)
        o_ref[...] = r_ref[...] + dx

    row = pl.BlockSpec((tm, d), lambda i: (i, 0))
    return _pcall(
        body, name=name, grid=(t // tm,),
        in_specs=[pl.BlockSpec((tm, a.shape[1]), lambda i: (i, 0)) for a, _ in pieces]
        + [_resident(w4.shape), row, _full((1, d)), row],
        out_specs=[row, _full((1, d))],
        out_shape=[jax.ShapeDtypeStruct((t, d), F32), jax.ShapeDtypeStruct((1, d), F32)],
        operands=(*[a for a, _ in pieces], w4, h, g, dres), comm=comm)


def _lane_block(n, cap):
    return max(b for b in range(128, min(n, cap) + 1, 128) if n % b == 0)


def _dw_cols(x, pieces, unit, per_chunk, name):
    t, m = x.shape
    tk = _tile(t, TOKEN_TILE)
    offs, total = [], 0
    for _, nu in pieces:
        offs.append(total)
        total += nu

    def body(x_ref, *refs):
        o_ref = refs[-1]
        u = pl.program_id(0)

        @pl.when(pl.program_id(1) == 0)
        def _():
            o_ref[...] = jnp.zeros_like(o_ref)

        for p_ref, off, (_, nu) in zip(refs[:-1], offs, pieces):
            @pl.when((u >= off) & (u < off + nu))
            def _(p_ref=p_ref):
                o_ref[...] += _dot_tn(x_ref[...], p_ref[...])

    def piece_spec(off, nu):
        def index(u, k):
            mine = (u >= off) & (u < off + nu)
            return jnp.where(mine, k, 0), jnp.clip(u - off, 0, nu - 1)
        return pl.BlockSpec((tk, unit), index)

    return pl.pallas_call(
        body, name=name, grid=(total, t // tk),
        in_specs=[pl.BlockSpec((tk, m), lambda u, k: (k, 0))] + [piece_spec(o, nu) for o, (_, nu) in zip(offs, pieces)],
        out_specs=pl.BlockSpec((None, None, m, unit), lambda u, k: (u // per_chunk, 0, 0, u % per_chunk)),
        out_shape=jax.ShapeDtypeStruct((N_CHIPS, 1, m, per_chunk * unit), F32),
        compiler_params=_params(2))(x, *[a for a, _ in pieces])


def _merge_bwd(dh, mo, g, proj, br, xs, merged, w_out, ws, sq, slots, name, comm=None):
    t, d = dh.shape
    tm = _tile(t, ROW_TILE_HEAVY)
    rows = d // N_CHIPS

    def body(dh_ref, mo_ref, g_ref, za_ref, zb_ref, zc_ref, ba_ref, bb_ref, bc_ref, xa_ref, xb_ref, xc_ref, m_ref,
             wo_ref, wa_ref, wb_ref, wc_ref, sq_in_ref, dz_ref, dsg_ref, dcs_ref, dps_ref, dg_ref, sq_ref,
             acc_ref, sem_ref):
        @pl.when(pl.program_id(0) == 0)
        def _():
            dg_ref[...] = jnp.zeros_like(dg_ref)
            acc_ref[...] = jnp.zeros_like(acc_ref)

        mon, r = _rms_stats(mo_ref[...])
        dmo, dgt = _rms_bwd(mon, r, g_ref[...], dh_ref[...])
        dg_ref[...] += _rowsum(dgt)
        dmo = dmo.astype(BF16)
        acc_ref[0] += _dot_tn(m_ref[...], dmo)
        dmerged = _dot_nt(dmo, wo_ref[...])
        branches = ((za_ref, ba_ref, xa_ref, wa_ref, dsg_ref), (zb_ref, bb_ref, xb_ref, wb_ref, dcs_ref),
                    (zc_ref, bc_ref, xc_ref, wc_ref, dps_ref))
        for j, (z_ref, b_ref, x_ref, w_ref, dx_ref) in enumerate(branches):
            gate = _sigmoid(z_ref[...].astype(F32))
            dbr = (dmerged * gate).astype(BF16)
            dz_ref[:, j * d:(j + 1) * d] = (dmerged * b_ref[...].astype(F32) * gate * (1.0 - gate)).astype(BF16)
            dx_ref[...] = _dot_nt(dbr, w_ref[...]).astype(BF16)
            acc_ref[1 + j] += _dot_tn(x_ref[...], dbr)
        _flush_slots(acc_ref, sq_ref, slots, sem_ref)

    row = pl.BlockSpec((tm, d), lambda i: (i, 0))
    wspec = _resident((d, d))
    bf = jax.ShapeDtypeStruct((t, d), BF16)
    n_in = 18
    return _pcall(
        body, name=name, grid=(t // tm,),
        in_specs=[row, row, _full(g.shape), pl.BlockSpec((tm, d), lambda i: (i, 5)),
                  pl.BlockSpec((tm, d), lambda i: (i, 6)), pl.BlockSpec((tm, d), lambda i: (i, 7)),
                  row, row, row, row, row, row, row, wspec, wspec, wspec, wspec, ANY],
        out_specs=[pl.BlockSpec((tm, 3 * d), lambda i: (i, 0)), row, row, row, _full((1, d)), ANY],
        out_shape=[jax.ShapeDtypeStruct((t, 3 * d), BF16), bf, bf, bf, jax.ShapeDtypeStruct((1, d), F32),
                   jax.ShapeDtypeStruct(sq.shape, sq.dtype)],
        scratch_shapes=[pltpu.VMEM((4, d, d), F32), pltpu.SemaphoreType.DMA((4 * N_CHIPS,))],
        operands=(dh, mo, g, proj, proj, proj, *br, *xs, merged, w_out, *ws, sq), comm=comm,
        aliases={n_in - 1: 5})


def _sgu_bwd(proj, dsg, wm, bs3, gv, bv, name, comm=None):
    t = proj.shape[0]
    d = D_MODEL
    tm = _tile(t, ROW_TILE_HEAVY)
    hd = d // SGU_HEADS

    def body(zu_ref, zv_ref, d_ref, wm_ref, bs_ref, gv_ref, bv_ref, dz_ref, dwm_ref, dbs_ref, dgv_ref, dbv_ref,
             dvn_ref):
        @pl.when(pl.program_id(0) == 0)
        def _():
            dwm_ref[...] = jnp.zeros_like(dwm_ref)
            dbs_ref[...] = jnp.zeros_like(dbs_ref)
            dgv_ref[...] = jnp.zeros_like(dgv_ref)
            dbv_ref[...] = jnp.zeros_like(dbv_ref)

        mask = _sgu_mask()
        for blk in range(tm // SGU_BLOCK):
            rows = pl.ds(blk * SGU_BLOCK, SGU_BLOCK)
            u, du_dz = _gelu_and_grad(zu_ref[rows, :].astype(F32))
            v0, dv_dz = _gelu_and_grad(zv_ref[rows, :].astype(F32))
            xhat, rstd = _ln_stats(v0)
            vn = (xhat * gv_ref[...] + bv_ref[...]).astype(BF16)
            dsg = d_ref[rows, :].astype(F32)
            dmix = (dsg * u).astype(BF16)
            for hh in range(SGU_HEADS):
                cols = slice(hh * hd, (hh + 1) * hd)
                wmh = jnp.where(mask, wm_ref[hh], 0.0).astype(BF16)
                vb = vn[:, cols]
                mixed = jnp.dot(wmh, vb, preferred_element_type=F32) + bs_ref[hh]
                dz_ref[rows, cols] = (dsg[:, cols] * mixed * du_dz[:, cols]).astype(BF16)
                dmh = dmix[:, cols]
                dwm_ref[hh] += jnp.where(mask, _dot_nt(dmh, vb), 0.0)
                dbs_ref[hh] += jnp.sum(dmh.astype(F32), axis=1, keepdims=True)
                dvn_ref[:, cols] = _dot_tn(wmh, dmh)
            dvn = dvn_ref[...]
            dgv_ref[...] += _rowsum(dvn * xhat)
            dbv_ref[...] += _rowsum(dvn)
            dz_ref[rows, d:2 * d] = (_ln_bwd(xhat, rstd, gv_ref[...], dvn) * dv_dz).astype(BF16)

    return _pcall(
        body, name=name, grid=(t // tm,),
        in_specs=[pl.BlockSpec((tm, d), lambda i: (i, 0)), pl.BlockSpec((tm, d), lambda i: (i, 1)),
                  pl.BlockSpec((tm, d), lambda i: (i, 0)), _full(wm.shape), _full(bs3.shape), _full(gv.shape),
                  _full(bv.shape)],
        out_specs=[pl.BlockSpec((tm, 2 * d), lambda i: (i, 0)), _full(wm.shape), _full(bs3.shape), _full((1, d)),
                   _full((1, d))],
        out_shape=[jax.ShapeDtypeStruct((t, 2 * d), BF16), jax.ShapeDtypeStruct(wm.shape, F32),
                   jax.ShapeDtypeStruct(bs3.shape, F32), jax.ShapeDtypeStruct((1, d), F32),
                   jax.ShapeDtypeStruct((1, d), F32)],
        scratch_shapes=[pltpu.VMEM((SGU_BLOCK, d), F32)],
        operands=(proj, proj, dsg, wm, bs3, gv, bv), comm=comm)


def _conv_bwd_norm(proj, dcs, cv, bdw, gln, bln, name, comm=None):
    t = proj.shape[0]
    d = D_MODEL
    tm = _tile(t, ROW_TILE)
    main, halo = _conv_specs(t, tm, d)
    n_win = CONV_ROWS + CONV_HALO

    def body(a_ref, g_ref, ah_ref, gh_ref, dcs_ref, cv_ref, b_ref, gl_ref, bl_ref,
             dcv_ref, dw_ref, db_ref, dgl_ref, dbl_ref, scr_ref, dwacc_ref):
        @pl.when(pl.program_id(0) == 0)
        def _():
            dwacc_ref[...] = jnp.zeros_like(dwacc_ref)
            db_ref[...] = jnp.zeros_like(db_ref)
            dgl_ref[...] = jnp.zeros_like(dgl_ref)
            dbl_ref[...] = jnp.zeros_like(dbl_ref)

        _fill_glu_history(scr_ref, a_ref, g_ref, ah_ref, gh_ref, tm)
        xhat, rstd = _ln_stats(cv_ref[...] + b_ref[...])
        cn = xhat * gl_ref[...] + bl_ref[...]
        s = _sigmoid(cn)
        dcn = dcs_ref[...].astype(F32) * (s * (1.0 + cn * (1.0 - s)))
        dgl_ref[...] += _rowsum(dcn * xhat)
        dbl_ref[...] += _rowsum(dcn)
        dcv = _ln_bwd(xhat, rstd, gl_ref[...], dcn)
        db_ref[...] += _rowsum(dcv)
        dcv_ref[...] = dcv

        def chunk(ci, carry):
            r0 = pl.multiple_of(ci * CONV_ROWS, CONV_ROWS)
            for c0 in range(0, d, CONV_LANES):
                lanes = pl.ds(c0, CONV_LANES)
                win = scr_ref[pl.ds(r0, n_win), lanes]
                dchunk = dcv_ref[pl.ds(r0, CONV_ROWS), lanes]
                for r in range(8):
                    rolled = win if r == 0 else pltpu.roll(win, n_win - r, 0)
                    for q in range(n_win // 8):
                        k = 8 * q + r - _CONV_BASE
                        if 0 <= k < CONV_WIDTH and 8 * q + CONV_ROWS <= n_win:
                            prod = dchunk * rolled[8 * q:8 * q + CONV_ROWS]
                            part = prod[0:8]
                            for s8 in range(8, CONV_ROWS, 8):
                                part = part + prod[s8:s8 + 8]
                            dwacc_ref[pl.ds(8 * k, 8), lanes] += part
            return carry

        lax.fori_loop(0, tm // CONV_ROWS, chunk, 0)

        @pl.when(pl.program_id(0) == pl.num_programs(0) - 1)
        def _():
            dw_ref[...] = jnp.sum(dwacc_ref[...].reshape(CONV_HALO, 8, d), axis=1)

    row = pl.BlockSpec((tm, d), lambda i: (i, 0))
    vec = _full((1, d))
    return _pcall(
        body, name=name, grid=(t // tm,),
        in_specs=main + halo + [row, row, vec, vec, vec],
        out_specs=[row, _full((CONV_HALO, d)), vec, vec, vec],
        out_shape=[jax.ShapeDtypeStruct((t, d), F32), jax.ShapeDtypeStruct((CONV_HALO, d), F32)]
        + [jax.ShapeDtypeStruct((1, d), F32)] * 3,
        scratch_shapes=[pltpu.VMEM((tm + CONV_HALO, d), F32), pltpu.VMEM((8 * CONV_HALO, d), F32)],
        operands=(proj, proj, proj, proj, dcs, cv, bdw, gln, bln), comm=comm)


def _conv_bwd_taps(proj, dcv, wdw, name, comm=None):
    t = proj.shape[0]
    d = D_MODEL
    tm = _tile(t, ROW_TILE)
    hb = tm // CONV_HALO
    last_halo = t // CONV_HALO - 1

    def body(a_ref, g_ref, dcv_ref, dnext_ref, w_ref, dz_ref, scr_ref, dh_ref):
        scr_ref[0:tm, :] = dcv_ref[...]
        is_last = pl.program_id(0) == pl.num_programs(0) - 1
        scr_ref[tm:tm + CONV_HALO, :] = jnp.where(is_last, 0.0, dnext_ref[...])
        _conv_into(scr_ref, dh_ref, w_ref, tm, 0, True)
        dglu = dh_ref[...]
        a = a_ref[...].astype(F32)
        s = _sigmoid(g_ref[...].astype(F32))
        dz_ref[:, 0:d] = (dglu * s).astype(BF16)
        dz_ref[:, d:2 * d] = (dglu * a * s * (1.0 - s)).astype(BF16)

    return _pcall(
        body, name=name, grid=(t // tm,),
        in_specs=[pl.BlockSpec((tm, d), lambda i: (i, 2)), pl.BlockSpec((tm, d), lambda i: (i, 3)),
                  pl.BlockSpec((tm, d), lambda i: (i, 0)),
                  pl.BlockSpec((CONV_HALO, d), lambda i: (jnp.minimum((i + 1) * hb, last_halo), 0)),
                  _full(wdw.shape)],
        out_specs=pl.BlockSpec((tm, 2 * d), lambda i: (i, 0)),
        out_shape=jax.ShapeDtypeStruct((t, 2 * d), BF16),
        scratch_shapes=[pltpu.VMEM((tm + CONV_HALO, d), F32), pltpu.VMEM((tm, d), F32)],
        operands=(proj, proj, dcv, dcv, wdw), comm=comm)


def _pool_bwd(proj, dps, wpool, spool, name):
    t = proj.shape[0]
    d = D_MODEL
    tm = _tile(t, ROW_TILE)
    hb = tm // POOL_BLOCK
    last_halo = t // POOL_BLOCK - 1
    ext = tm + POOL_BLOCK

    def body(z_ref, zh_ref, d_ref, dnext_ref, w_ref, s_ref, dz_ref, dw_ref, ds_ref, scr_ref, dext_ref, hi_ref, lo_ref):
        @pl.when(pl.program_id(0) == 0)
        def _():
            dw_ref[...] = jnp.zeros_like(dw_ref)
            ds_ref[...] = jnp.zeros_like(ds_ref)

        _pool_fill(scr_ref, z_ref, zh_ref, tm)
        t0 = pl.program_id(0) * tm
        is_last = pl.program_id(0) == pl.num_programs(0) - 1
        dext_ref[0:tm, :] = d_ref[...].astype(F32)
        dext_ref[tm:ext, :] = jnp.where(is_last, 0.0, dnext_ref[...].astype(F32))
        for gi, w in enumerate(POOL_WINDOWS):
            cols = slice(gi * POOL_GROUP, (gi + 1) * POOL_GROUP)
            dps_ext = dext_ref[:, cols]
            dpm_ext = (dps_ext * s_ref[:, cols]).astype(BF16)
            dpooled_ext = _dot_nt(dpm_ext, w_ref[gi])
            dq = dpooled_ext / _pool_count(t0, ext, w)
            hi = dq.astype(BF16)
            hi_ref[...] = hi
            lo_ref[...] = (dq - hi.astype(F32)).astype(BF16)
            sums = _window_sums([hi_ref, lo_ref], _band(w, True), tm, slice(None))
            dz_ref[:, cols] = (sums - dpooled_ext[0:tm]).astype(BF16)
            pooled = _pooled_group(scr_ref, gi, w, tm, t0).astype(BF16)
            pm = jnp.dot(pooled, w_ref[gi], preferred_element_type=F32)
            ds_ref[:, cols] += _rowsum(dps_ext[0:tm] * pm)
            dw_ref[gi] += _dot_tn(pooled, dpm_ext[0:tm])

    return pl.pallas_call(
        body, name=name, grid=(t // tm,),
        in_specs=_pool_specs(tm, d) + [pl.BlockSpec((tm, d), lambda i: (i, 0)),
                                       pl.BlockSpec((POOL_BLOCK, d), lambda i: (jnp.minimum((i + 1) * hb, last_halo), 0)),
                                       _full(wpool.shape), _full(spool.shape)],
        out_specs=[pl.BlockSpec((tm, d), lambda i: (i, 0)), _full(wpool.shape), _full((1, d))],
        out_shape=[jax.ShapeDtypeStruct((t, d), BF16), jax.ShapeDtypeStruct(wpool.shape, F32),
                   jax.ShapeDtypeStruct((1, d), F32)],
        scratch_shapes=[pltpu.VMEM((tm + POOL_BLOCK, d), BF16), pltpu.VMEM((ext, d), F32),
                        pltpu.VMEM((ext, POOL_GROUP), BF16), pltpu.VMEM((ext, POOL_GROUP), BF16)],
        compiler_params=_params(1))(proj, proj, dps, dps, wpool, spool)


ANY = pl.BlockSpec(memory_space=pl.ANY)


def _mesh_pos():
    x, y, c = lax.axis_index("x"), lax.axis_index("y"), lax.axis_index("c")
    chips = [(1 - x, y), (x, 1 - y), (1 - x, 1 - y)]
    return x, y, c, chips


def _chip_of(xy):
    return 2 * xy[0] + xy[1]


def _half_view(a):
    return a.reshape(a.shape[:-2] + (2, a.shape[-2] // 2, a.shape[-1]))


def _same(arrs):
    return [jax.ShapeDtypeStruct(a.shape, a.dtype) for a in arrs]


def _in_place(n):
    return {g: g for g in range(n)}


def _sems(count):
    return [pltpu.SemaphoreType.DMA((count,)), pltpu.SemaphoreType.DMA((count,))]


def _gather_ici(bufs):
    n = len(bufs)

    def copy(buf, sems, g, j, chip):
        x, y, c, chips = _mesh_pos()
        slab = buf[g].at[chip, :, c]
        return pltpu.make_async_remote_copy(
            src_ref=slab, dst_ref=slab, send_sem=sems[0].at[3 * g + j], recv_sem=sems[1].at[3 * g + j],
            device_id=(*chips[j], c), device_id_type=MESH)

    def start(ins, buf, sems):
        x, y, c, chips = _mesh_pos()
        for g in range(n):
            for j in range(3):
                copy(buf, sems, g, j, 2 * x + y).start()

    def finish(ins, buf, sems):
        x, y, c, chips = _mesh_pos()
        for g in range(n):
            for j in range(3):
                copy(buf, sems, g, j, _chip_of(chips[j])).wait_recv()
        for g in range(n):
            for j in range(3):
                copy(buf, sems, g, j, 2 * x + y).wait_send()

    return _Payload(bufs, _same(bufs), _in_place(n), _sems(3 * n), start, finish)


def _gather_d2d(bufs):
    n = len(bufs)

    def copy(buf, sems, g, j, half):
        x, y, c, chips = _mesh_pos()
        slab = buf[g].at[_chip_of(chips[j]), :, half]
        return pltpu.make_async_remote_copy(
            src_ref=slab, dst_ref=slab, send_sem=sems[0].at[3 * g + j], recv_sem=sems[1].at[3 * g + j],
            device_id=(x, y, 1 - c), device_id_type=MESH)

    def start(ins, buf, sems):
        c = lax.axis_index("c")
        for g in range(n):
            for j in range(3):
                copy(buf, sems, g, j, c).start()

    def finish(ins, buf, sems):
        c = lax.axis_index("c")
        for g in range(n):
            for j in range(3):
                copy(buf, sems, g, j, 1 - c).wait_recv()
        for g in range(n):
            for j in range(3):
                copy(buf, sems, g, j, c).wait_send()

    return _Payload(bufs, _same(bufs), _in_place(n), _sems(3 * n), start, finish)


def _pair_exchange(grads):
    n = len(grads)

    def copy(src, dst, sems, g):
        x, y, c, _ = _mesh_pos()
        return pltpu.make_async_remote_copy(
            src_ref=src[g].at[:, :, 1 - c], dst_ref=dst[g], send_sem=sems[0].at[g], recv_sem=sems[1].at[g],
            device_id=(x, y, 1 - c), device_id_type=MESH)

    def start(src, dst, sems):
        for g in range(n):
            copy(src, dst, sems, g).start()

    def finish(src, dst, sems):
        for g in range(n):
            copy(src, dst, sems, g).wait()

    out_shape = [jax.ShapeDtypeStruct(g.shape[:2] + g.shape[3:], g.dtype) for g in grads]
    return _Payload(grads, out_shape, {}, _sems(n), start, finish)


def _chip_exchange(parts):
    n = len(parts)

    def copy(src, dst, sems, g, j, slot):
        x, y, c, chips = _mesh_pos()
        return pltpu.make_async_remote_copy(
            src_ref=src[g].at[_chip_of(chips[j])], dst_ref=dst[g].at[slot], send_sem=sems[0].at[3 * g + j],
            recv_sem=sems[1].at[3 * g + j], device_id=(*chips[j], c), device_id_type=MESH)

    def start(src, dst, sems):
        x, y, c, chips = _mesh_pos()
        for g in range(n):
            for j in range(3):
                copy(src, dst, sems, g, j, 2 * x + y).start()

    def finish(src, dst, sems):
        x, y, c, chips = _mesh_pos()
        for g in range(n):
            for j in range(3):
                copy(src, dst, sems, g, j, _chip_of(chips[j])).wait_recv()
        for g in range(n):
            for j in range(3):
                copy(src, dst, sems, g, j, 2 * x + y).wait_send()

    return _Payload(parts, _same(parts), {}, _sems(3 * n), start, finish)


def _pair_share(bufs):
    n = len(bufs)

    def copy(buf, sems, g, half):
        x, y, c, _ = _mesh_pos()
        slab = buf[g].at[:, :, half]
        return pltpu.make_async_remote_copy(
            src_ref=slab, dst_ref=slab, send_sem=sems[0].at[g], recv_sem=sems[1].at[g],
            device_id=(x, y, 1 - c), device_id_type=MESH)

    def start(ins, buf, sems):
        c = lax.axis_index("c")
        for g in range(n):
            copy(buf, sems, g, c).start()

    def finish(ins, buf, sems):
        c = lax.axis_index("c")
        for g in range(n):
            copy(buf, sems, g, 1 - c).wait_recv()
        for g in range(n):
            copy(buf, sems, g, c).wait_send()

    return _Payload(bufs, _same(bufs), _in_place(n), _sems(n), start, finish)


def _join(a, b):
    if a is None or b is None:
        return a or b
    na, ma = len(a.operands), len(a.out_shape)
    aliases = dict(a.aliases)
    aliases.update({na + i: ma + o for i, o in b.aliases.items()})
    ka = len(a.scratch)

    def start(ins, outs, sems):
        a.start(ins[:na], outs[:ma], sems[:ka])
        b.start(ins[na:], outs[ma:], sems[ka:])

    def finish(ins, outs, sems):
        a.finish(ins[:na], outs[:ma], sems[:ka])
        b.finish(ins[na:], outs[ma:], sems[ka:])

    joined = _Payload(a.operands + b.operands, a.out_shape + b.out_shape, aliases, list(a.scratch) + list(b.scratch),
                      start, finish)
    joined.parts = (a, b, ma)
    return joined


def _small_exchange(vec):
    def copy(src, dst, sems, k, slot):
        x, y, c, _ = _mesh_pos()
        peer = (x ^ (k >> 2), y ^ ((k >> 1) & 1), c ^ (k & 1))
        return pltpu.make_async_remote_copy(
            src_ref=src[0], dst_ref=dst[0].at[slot], send_sem=sems[0].at[k - 1], recv_sem=sems[1].at[k - 1],
            device_id=peer, device_id_type=MESH)

    def me():
        x, y, c, _ = _mesh_pos()
        return 4 * x + 2 * y + c

    def start(src, dst, sems):
        for k in range(1, 8):
            copy(src, dst, sems, k, me()).start()

    def finish(src, dst, sems):
        for k in range(1, 8):
            copy(src, dst, sems, k, me() ^ k).wait_recv()
        for k in range(1, 8):
            copy(src, dst, sems, k, me()).wait_send()

    return _Payload([vec], [jax.ShapeDtypeStruct((8,) + vec.shape, vec.dtype)], {}, _sems(7), start, finish)


def _small_sum(vec, landed, pos, name):
    r = vec.shape[0]

    def body(pos_ref, v_ref, l_ref, o_ref):
        k = pl.program_id(0)

        @pl.when(k == 0)
        def _():
            o_ref[...] = jnp.zeros_like(o_ref)

        @pl.when(k == pos_ref[POS_DEVICE])
        def _():
            o_ref[...] += v_ref[...]

        @pl.when(k != pos_ref[POS_DEVICE])
        def _():
            o_ref[...] += l_ref[...]

    def landed_index(k, pos_ref):
        me = pos_ref[POS_DEVICE]
        return jnp.where(k == me, (me + 1) % 8, k), 0, 0

    return pl.pallas_call(
        body, name=name,
        grid_spec=pltpu.PrefetchScalarGridSpec(
            num_scalar_prefetch=1, grid=(8,),
            in_specs=[pl.BlockSpec((r, 128), lambda k, pos_ref: (0, 0)), pl.BlockSpec((None, r, 128), landed_index)],
            out_specs=pl.BlockSpec((r, 128), lambda k, pos_ref: (0, 0))),
        out_shape=jax.ShapeDtypeStruct(vec.shape, F32),
        compiler_params=_params(1))(pos, vec, landed)


def _all_reduce_small(vec, name):
    r = vec.shape[0]

    def body(v_ref, o_ref, gath_ref, send_sem, recv_sem):
        x, y, c, _ = _mesh_pos()
        me = 4 * x + 2 * y + c
        gath_ref[me] = v_ref[...]
        copies = []
        for k in range(1, 8):
            peer = (x ^ (k >> 2), y ^ ((k >> 1) & 1), c ^ (k & 1))
            cp = pltpu.make_async_remote_copy(
                src_ref=v_ref, dst_ref=gath_ref.at[me], send_sem=send_sem.at[k - 1], recv_sem=recv_sem.at[k - 1],
                device_id=peer, device_id_type=MESH)
            cp.start()
            copies.append(cp)
        for k in range(1, 8):
            src_id = me ^ k
            pltpu.make_async_remote_copy(
                src_ref=v_ref, dst_ref=gath_ref.at[src_id], send_sem=send_sem.at[k - 1], recv_sem=recv_sem.at[k - 1],
                device_id=(x, y, c), device_id_type=MESH).wait_recv()
        for cp in copies:
            cp.wait_send()
        acc = gath_ref[0]
        for k in range(1, 8):
            acc = acc + gath_ref[k]
        o_ref[...] = acc

    return pl.pallas_call(
        body, name=name,
        in_specs=[pl.BlockSpec(memory_space=pltpu.VMEM)], out_specs=pl.BlockSpec(memory_space=pltpu.VMEM),
        out_shape=jax.ShapeDtypeStruct(vec.shape, F32),
        scratch_shapes=[pltpu.VMEM((8, r, 128), F32), pltpu.SemaphoreType.DMA((7,)), pltpu.SemaphoreType.DMA((7,))],
        compiler_params=pltpu.CompilerParams(has_side_effects=True, vmem_limit_bytes=VMEM_LIMIT))(vec)


def _row_block(rows, cols, mult=16):
    best = None
    for cand in range(mult, rows + 1, mult):
        if rows % cand == 0 and cand * cols * 4 <= EW_BLOCK_BYTES:
            best = cand
    return best or rows


POS_ME, POS_CORE, POS_DEVICE = 0, 4, 5


def _place(arrs, li, pos, dtype, name):
    s = len(arrs)
    _, rows, cols = arrs[0].shape
    rh = rows // 2
    tr = _row_block(rh, cols)
    nb = rh // tr

    def body(pos_ref, *refs):
        o_ref = refs[s]
        for j in range(s):
            @pl.when(pl.program_id(0) == j)
            def _(j=j):
                o_ref[...] = refs[j][...].astype(dtype)

    def in_spec(j):
        return pl.BlockSpec((None, tr, cols), lambda b, hf, i, pos_ref: (li, jnp.where(b == j, hf * nb + i, 0), 0))

    return pl.pallas_call(
        body, name=name,
        grid_spec=pltpu.PrefetchScalarGridSpec(
            num_scalar_prefetch=1, grid=(s, 2, nb), in_specs=[in_spec(j) for j in range(s)],
            out_specs=pl.BlockSpec((None, None, None, tr, cols),
                                   lambda b, hf, i, pos_ref: (pos_ref[POS_ME], b, hf, i, 0))),
        out_shape=jax.ShapeDtypeStruct((N_CHIPS, s, 2, rh, cols), dtype),
        compiler_params=_params(3))(pos, *arrs)


def _pair_sum(grad, recv, pos, out_dtype, name):
    _, s, rh, cols = recv.shape
    tr = _row_block(rh, cols)

    def body(pos_ref, g_ref, r_ref, o_ref):
        o_ref[...] = (g_ref[...] + r_ref[...]).astype(out_dtype)

    blk = (None, None, tr, cols)
    return pl.pallas_call(
        body, name=name,
        grid_spec=pltpu.PrefetchScalarGridSpec(
            num_scalar_prefetch=1, grid=(N_CHIPS, s, rh // tr),
            in_specs=[pl.BlockSpec((None, None, None, tr, cols),
                                   lambda a, b, i, pos_ref: (a, b, pos_ref[POS_CORE], i, 0)),
                      pl.BlockSpec(blk, lambda a, b, i, pos_ref: (a, b, i, 0))],
            out_specs=pl.BlockSpec(blk, lambda a, b, i, pos_ref: (a, b, i, 0))),
        out_shape=jax.ShapeDtypeStruct(recv.shape, out_dtype),
        compiler_params=_params(3))(pos, grad, recv)


def _chip_sum(part, landed, gbuf, li, n_layers, pos, name):
    _, s, rh, cols = part.shape
    tr = _row_block(rh, cols)

    def body(pos_ref, p_ref, a_ref, b_ref, c_ref, *rest):
        o_ref = rest[-1]
        o_ref[...] = ((p_ref[...].astype(F32) + a_ref[...].astype(F32)) + b_ref[...].astype(F32)) \
            + c_ref[...].astype(F32)

    def slot(k):
        return pl.BlockSpec((None, None, tr, cols), lambda b, i, pos_ref: (pos_ref[k], b, i, 0))

    in_specs = [slot(0), slot(1), slot(2), slot(3)]
    operands = [pos, part, landed, landed, landed]
    aliases = {}
    if gbuf is not None:
        in_specs.append(ANY)
        operands.append(gbuf)
        aliases = {len(operands) - 1: 0}
    return pl.pallas_call(
        body, name=name,
        grid_spec=pltpu.PrefetchScalarGridSpec(
            num_scalar_prefetch=1, grid=(s, rh // tr), in_specs=in_specs,
            out_specs=pl.BlockSpec((None, None, None, tr, cols),
                                   lambda b, i, pos_ref: (li, b, pos_ref[POS_CORE], i, 0))),
        out_shape=jax.ShapeDtypeStruct((n_layers, s, 2, rh, cols), F32),
        input_output_aliases=aliases,
        compiler_params=_params(2))(*operands)


def _adamw_math(w, g, m, v):
    m = ADAM_B1 * m + (1.0 - ADAM_B1) * g
    v = ADAM_B2 * v + (1.0 - ADAM_B2) * (g * g)
    m_hat = m / (1.0 - ADAM_B1 ** ADAM_STEP)
    v_hat = v / (1.0 - ADAM_B2 ** ADAM_STEP)
    delta = -ADAM_LR * (m_hat / (jnp.sqrt(v_hat) + ADAM_EPS) + ADAM_WD * w)
    return delta, m, v


def _adamw(w, g, slot, m, v, name):
    l, rows, cols = w.shape
    tr = _row_block(rows, cols, 8)

    def body(w_ref, g_ref, m_ref, v_ref, go_ref, d_ref, mo_ref, vo_ref):
        g_ = g_ref[...]
        delta, m_, v_ = _adamw_math(w_ref[...], g_, m_ref[...], v_ref[...])
        go_ref[...] = g_
        d_ref[...] = delta
        mo_ref[...] = m_
        vo_ref[...] = v_

    blk = pl.BlockSpec((None, tr, cols), lambda a, i: (a, i, 0))
    gblk = pl.BlockSpec((None, None, tr, cols), lambda a, i: (a, slot, i, 0))
    return pl.pallas_call(
        body, name=name, grid=(l, rows // tr), in_specs=[blk, gblk, blk, blk], out_specs=[blk] * 4,
        out_shape=[jax.ShapeDtypeStruct(w.shape, F32)] * 4,
        compiler_params=_params(2))(w, g, m, v)


SQ = ("w_sgu_out", "w_conv_out", "w_pool_out", "w_out", "w_ple_gate")
SMALL = ("g_mix_pre", "w_sgu_s", "b_sgu_s", "g_sgu_v", "b_sgu_v", "b_dw", "g_conv_ln", "b_conv_ln", "s_pool",
         "g_mix_post", "g_ffn_pre", "g_ffn_post")


WHERE = {"w_in": ("in", 0), "w_ffn_in": ("ffn_in", 0), "w_ffn_out": ("ffn_out", 0), "w_ple": ("mix", 0),
         "w_pool": ("mix", 1), "w_dw": ("dw", 0)}
WHERE.update({nm: ("sq", slot) for slot, nm in enumerate(SQ)})


class _LayerWeights:
    def __init__(self, fetch, small, li):
        self.fetch, self.small, self.li, self.cache = fetch, small, li, {}

    def __getitem__(self, nm):
        if nm not in self.cache:
            self.cache[nm] = self._big(nm) if nm in WHERE else self.small[nm][self.li]
        return self.cache[nm]

    def _big(self, nm):
        group, slot = WHERE[nm]
        g = self.fetch(group)
        g = g.reshape(g.shape[:2] + (-1, g.shape[-1]))
        if nm in ("w_in", "w_ffn_in"):
            return g.reshape(N_CHIPS, D_MODEL, -1)
        if nm == "w_ffn_out":
            return g.reshape(D_FF, D_MODEL)
        if nm in SQ:
            return g[:, slot].reshape(D_MODEL, D_MODEL)
        if nm == "w_ple":
            return g[:, slot].transpose(1, 0, 2).reshape(256, D_MODEL)
        if nm == "w_pool":
            return g[:, slot].reshape(N_CHIPS, 4, 64, 256).transpose(1, 0, 2, 3).reshape(4, 256, 256)
        return g.reshape(N_CHIPS, CONV_HALO, -1).transpose(1, 0, 2).reshape(CONV_HALO, D_MODEL)


def _vec(a):
    return a.reshape(1, -1)


def _layer_fwd(h, p, w, li, hosts=None):
    s = {}
    tag = "_l%d" % li
    s["h0"] = h
    proj, hn = _norm_mm(h, _vec(w["g_mix_pre"]), w["w_in"], "mix_in" + tag, _take(hosts, "mix_in"))
    s["proj"], s["hn"] = proj, hn
    bs3 = w["b_sgu_s"].reshape(SGU_HEADS, SGU_BLOCK, 1)
    s["sg"] = _sgu_fwd(proj, w["w_sgu_s"], bs3, _vec(w["g_sgu_v"]), _vec(w["b_sgu_v"]), "sgu_fwd" + tag,
                       _take(hosts, "sgu_fwd"))
    s["cs"], s["cv"] = _conv_fwd(proj, w["w_dw"], _vec(w["b_dw"]), _vec(w["g_conv_ln"]), _vec(w["b_conv_ln"]),
                                 "conv_fwd" + tag, _take(hosts, "conv_fwd"))
    s["ps"] = _pool_fwd(proj, w["w_pool"], _vec(w["s_pool"]), "pool_fwd" + tag, _take(hosts, "pool_fwd"))
    s["bra"], s["brb"], s["brc"], s["merged"] = _merge_fwd(
        proj, s["sg"], s["cs"], s["ps"], w["w_sgu_out"], w["w_conv_out"], w["w_pool_out"], "merge_fwd" + tag,
        _take(hosts, "merge_fwd"))
    s["mo"], h1 = _mm_norm_res(s["merged"], w["w_out"], _vec(w["g_mix_post"]), h, "mix_out" + tag,
                               _take(hosts, "mix_out"))
    s["h1"] = h1
    s["fg"], s["fu"], s["act"], s["hn2"] = _ffn_in(h1, _vec(w["g_ffn_pre"]), w["w_ffn_in"], "ffn_in" + tag,
                                                   _take(hosts, "ffn_in"))
    s["f"], h2 = _mm_norm_res(s["act"], w["w_ffn_out"], _vec(w["g_ffn_post"]), h1, "ffn_out" + tag,
                              _take(hosts, "ffn_out"))
    s["h2"] = h2
    h3, s["q"], s["e"] = _ple_fwd(h2, p, w["w_ple_gate"], w["w_ple"], "ple_fwd" + tag, _take(hosts, "ple_fwd"))
    return h3, s


def _layer_bwd(dh3, p, w, s, li, hosts=None, big=None, gs=None):
    tag = "_l%d" % li
    d = D_MODEL
    gs = {} if gs is None else gs
    big = {} if big is None else big
    dh2, sq, dw_ple = _ple_bwd(dh3, s["q"], s["e"], w["w_ple_gate"], s["h2"], p, SQ.index("w_ple_gate"), len(SQ),
                               "ple_bwd" + tag)
    dff, gs["g_ffn_post"], dw_ffn_out = _ffn_out_bwd(
        dh2, s["f"], _vec(w["g_ffn_post"]), s["fg"], s["fu"], s["act"], w["w_ffn_out"], "ffn_out_bwd" + tag,
        _take(hosts, "ffn_out_bwd"))
    big["ffn_out"] = dw_ffn_out.reshape(N_CHIPS, 1, D_FF // N_CHIPS, d)
    n_ff = w["w_ffn_in"].shape[2]
    dh1, gs["g_ffn_pre"] = _in_bwd([(dff, 2 * D_FF // n_ff)], w["w_ffn_in"], n_ff, s["h1"], _vec(w["g_ffn_pre"]),
                                   dh2, ROW_TILE, "ffn_in_bwd" + tag, _take(hosts, "ffn_in_bwd"))
    big["ffn_in"] = _dw_cols(s["hn2"], [(dff, 2 * D_FF // n_ff)], n_ff, 1, "dw_ffn_in" + tag)
    branch_w = ("w_sgu_out", "w_conv_out", "w_pool_out")
    dzg, dsg, dcs, dps, gs["g_mix_post"], big["sq"] = _merge_bwd(
        dh1, s["mo"], _vec(w["g_mix_post"]), s["proj"], (s["bra"], s["brb"], s["brc"]), (s["sg"], s["cs"], s["ps"]),
        s["merged"], w["w_out"], [w[nm] for nm in branch_w], sq, [SQ.index(nm) for nm in ("w_out",) + branch_w],
        "merge_bwd" + tag, _take(hosts, "merge_bwd"))
    bs3 = w["b_sgu_s"].reshape(SGU_HEADS, SGU_BLOCK, 1)
    dz_sgu, gs["w_sgu_s"], dbs3, gs["g_sgu_v"], gs["b_sgu_v"] = _sgu_bwd(
        s["proj"], dsg, w["w_sgu_s"], bs3, _vec(w["g_sgu_v"]), _vec(w["b_sgu_v"]), "sgu_bwd" + tag,
        _take(hosts, "sgu_bwd"))
    gs["b_sgu_s"] = dbs3
    dcv, dwdw, gs["b_dw"], gs["g_conv_ln"], gs["b_conv_ln"] = _conv_bwd_norm(
        s["proj"], dcs, s["cv"], _vec(w["b_dw"]), _vec(w["g_conv_ln"]), _vec(w["b_conv_ln"]), "conv_bwd_norm" + tag,
        _take(hosts, "conv_bwd_norm"))
    dz_conv = _conv_bwd_taps(s["proj"], dcv, w["w_dw"], "conv_bwd_taps" + tag, _take(hosts, "conv_bwd_taps"))
    dz_pool, dwpool, gs["s_pool"] = _pool_bwd(s["proj"], dps, w["w_pool"], _vec(w["s_pool"]), "pool_bwd" + tag)
    pieces = [(dz_sgu, 2), (dz_conv, 2), (dz_pool, 1), (dzg, 3)]
    big["in"] = _dw_cols(s["hn"], pieces, d, 2, "dw_in" + tag)
    gple = dw_ple.reshape(256, N_CHIPS, 256).transpose(1, 0, 2)
    gpool = dwpool.reshape(4, N_CHIPS, 64, 256).transpose(1, 0, 2, 3).reshape(N_CHIPS, 256, 256)
    big["mix"] = jnp.stack([gple, gpool], axis=1)
    big["dw"] = dwdw.reshape(CONV_HALO, N_CHIPS, 256).transpose(1, 0, 2)[:, None]
    dh0, gs["g_mix_pre"] = _in_bwd(pieces, w["w_in"], d, s["h0"], _vec(w["g_mix_pre"]), dh1, ROW_TILE_HEAVY,
                                   "mix_in_bwd" + tag, _take(hosts, "mix_in_bwd"))
    return dh0, big, gs


GROUPS = ("in", "sq", "ffn_in", "ffn_out", "mix", "dw")
WIRE_DTYPE = {"in": BF16, "sq": BF16, "ffn_in": BF16, "ffn_out": BF16, "mix": BF16, "dw": F32}
GATHER_FIRST = ("in", "mix", "dw")
GATHER_RIDES = (("mix_in", "sgu_fwd", ("sq", "ffn_in"), ()),
                ("conv_fwd", "pool_fwd", ("ffn_out",), ("in",)),
                ("merge_fwd", "mix_out", (), ("sq",)),
                ("ffn_in", "ffn_out", (), ("ffn_in", "ffn_out", "mix", "dw")))
REDUCE_UPPER = ("ffn_out_bwd", (("ffn_in_bwd", ("in", "ffn_out")), ("merge_bwd", ("sq", "ffn_in", "mix", "dw"))))
REDUCE_OWN = ("sgu_bwd", (("conv_bwd_norm", ("ffn_in", "ffn_out")), ("conv_bwd_taps", ("sq",))))
REDUCE_LAST = ("in", "mix", "dw")


def _group_members(wts):
    n_layers = wts["w_in"].shape[0]
    dw = wts["w_dw"].reshape(n_layers, CONV_WIDTH, -1)
    return {"in": [wts["w_in"]], "sq": [wts[nm] for nm in SQ], "ffn_in": [wts["w_ffn_in"]],
            "ffn_out": [wts["w_ffn_out"]],
            "mix": [wts["w_ple"], wts["w_pool"].reshape(n_layers, POOL_GROUP, POOL_GROUP)],
            "dw": [jnp.pad(dw, ((0, 0), (0, CONV_HALO - CONV_WIDTH), (0, 0)))]}


class _Gather:
    PLACED, OVER_ICI, FULL = 0, 1, 2

    def __init__(self):
        self.buf, self.stage, self.pending = {}, {}, []

    def put(self, key, buf):
        self.buf[key], self.stage[key] = buf, self.PLACED

    def _flush(self):
        for keys, pay, stage in self.pending:
            if pay.results is not None:
                for key, res in zip(keys, pay.results):
                    self.buf[key], self.stage[key] = res, stage
        self.pending = [entry for entry in self.pending if entry[1].results is None]

    def _factory(self, make, keys, before, after):
        def factory():
            if not keys:
                return None
            self._flush()
            assert all(self.stage[k] == before for k in keys), (keys, self.stage)
            pay = make([self.buf[k] for k in keys])
            self.pending.append((keys, pay, after))
            return pay
        return factory

    def ici(self, keys):
        return self._factory(_gather_ici, keys, self.PLACED, self.OVER_ICI)

    def d2d(self, keys):
        return self._factory(_gather_d2d, keys, self.OVER_ICI, self.FULL)

    def get(self, li, group):
        self._flush()
        assert self.stage[(li, group)] == self.FULL, (li, group)
        return self.buf[(li, group)]


class _Reduce:
    def __init__(self, pos, n_layers):
        self.pos, self.n_layers, self.exchanged, self.stages = pos, n_layers, [], []

    def exchange(self, li, groups, grads):
        def factory():
            pay = _pair_exchange([_half_view(grads[g]) for g in groups])
            self.exchanged.append((li, list(groups), pay))
            return pay
        return factory

    def _received(self, li, group):
        for lj, groups, pay in self.exchanged:
            if lj == li and group in groups:
                return pay.results[groups.index(group)]
        raise KeyError((li, group))

    def chips(self, li, groups, grads):
        def factory():
            parts = [_pair_sum(_half_view(grads[g]), self._received(li, g), self.pos, WIRE_DTYPE[g],
                               "pair_sum_%s_l%d" % (g, li)) for g in groups]
            pay = _chip_exchange(parts)
            self.stages.append((li, groups, parts, pay))
            return pay
        return factory

    def finish(self):
        reduced = {}
        for li, groups, parts, pay in self.stages:
            for g, part, landed in zip(groups, parts, pay.results):
                reduced[g] = _chip_sum(part, landed, reduced.get(g), li, self.n_layers, self.pos,
                                       "chip_sum_%s_l%d" % (g, li))
        return reduced


def _pack_small(tree):
    flat = jnp.concatenate([tree[nm].reshape(-1).astype(F32) for nm in SMALL])
    return flat.reshape(-1, 128)


def _unpack_small(packed, like):
    out, off = {}, 0
    flat = packed.reshape(-1)
    for nm in SMALL:
        n = like[nm].size
        out[nm] = flat[off:off + n].reshape(like[nm].shape)
        off += n
    return out


WEIGHTS = ("g_mix_pre", "w_in", "w_sgu_s", "b_sgu_s", "g_sgu_v", "b_sgu_v", "w_sgu_out", "w_dw", "b_dw", "g_conv_ln",
           "b_conv_ln", "w_conv_out", "w_pool", "s_pool", "w_pool_out", "w_out", "g_mix_post", "g_ffn_pre",
           "w_ffn_in", "w_ffn_out", "g_ffn_post", "w_ple", "w_ple_gate")


def kernel(x, p, g_mix_pre, w_in, w_sgu_s, b_sgu_s, g_sgu_v, b_sgu_v, w_sgu_out, w_dw, b_dw, g_conv_ln, b_conv_ln, w_conv_out, w_pool, s_pool, w_pool_out, w_out, g_mix_post, g_ffn_pre, w_ffn_in, w_ffn_out, g_ffn_post, w_ple, w_ple_gate, loss_target, m_g_mix_pre, m_w_in, m_w_sgu_s, m_b_sgu_s, m_g_sgu_v, m_b_sgu_v, m_w_sgu_out, m_w_dw, m_b_dw, m_g_conv_ln, m_b_conv_ln, m_w_conv_out, m_w_pool, m_s_pool, m_w_pool_out, m_w_out, m_g_mix_post, m_g_ffn_pre, m_w_ffn_in, m_w_ffn_out, m_g_ffn_post, m_w_ple, m_w_ple_gate, v_g_mix_pre, v_w_in, v_w_sgu_s, v_b_sgu_s, v_g_sgu_v, v_b_sgu_v, v_w_sgu_out, v_w_dw, v_b_dw, v_g_conv_ln, v_b_conv_ln, v_w_conv_out, v_w_pool, v_s_pool, v_w_pool_out, v_w_out, v_g_mix_post, v_g_ffn_pre, v_w_ffn_in, v_w_ffn_out, v_g_ffn_post, v_w_ple, v_w_ple_gate):
    args = dict(locals())
    wts = {nm: args[nm] for nm in WEIGHTS}
    mom = {nm: args["m_" + nm] for nm in WEIGHTS}
    var = {nm: args["v_" + nm] for nm in WEIGHTS}
    n_layers = w_in.shape[0]
    h = x.reshape(x.shape[1:])
    target = loss_target.reshape(loss_target.shape[1:])
    cx, cy, core = lax.axis_index("x"), lax.axis_index("y"), lax.axis_index("c")
    pos = jnp.stack([2 * cx + cy, 2 * (1 - cx) + cy, 2 * cx + (1 - cy), 2 * (1 - cx) + (1 - cy), core,
                     4 * cx + 2 * cy + core])
    pos = pos.astype(jnp.int32)

    members = _group_members(wts)
    gather = _Gather()
    for li in range(n_layers):
        for g in GROUPS:
            gather.put((li, g), _place(members[g], li, pos, WIRE_DTYPE[g], "place_%s_l%d" % (g, li)))
    first = [(0, g) for g in GATHER_FIRST]
    _run_payload(gather.ici(first)(), "gather_ici_first")
    _run_payload(gather.d2d(first)(), "gather_d2d_first")

    saved, layer_w = [], []
    for li in range(n_layers):
        hosts = {}
        for ici_host, d2d_host, own, nxt in GATHER_RIDES:
            keys = [(li, g) for g in own if li == 0] + [(li + 1, g) for g in nxt if li + 1 < n_layers]
            hosts[ici_host], hosts[d2d_host] = gather.ici(keys), gather.d2d(keys)
        w = _LayerWeights(functools.partial(gather.get, li), wts, li)
        layer_w.append(w)
        h, s = _layer_fwd(h, p[li, 0], w, li, hosts)
        saved.append(s)
    dh, sq_err = _loss_head(h, target, "loss_head")
    loss = lax.psum(sq_err[0, 0] * (0.5 / D_MODEL), ("x", "y", "c"))

    reduce = _Reduce(pos, n_layers)
    small_grads = [{} for _ in range(n_layers)]
    late = (0, SMALL[0])
    small = {}

    def small_vec():
        def leaf(li, nm):
            shape = wts[nm].shape[1:]
            return jnp.zeros(shape, F32) if (li, nm) == late else small_grads[li][nm].reshape(shape)
        return _pack_small({nm: jnp.stack([leaf(li, nm) for li in range(n_layers)], axis=0) for nm in SMALL})

    upper = None
    for li in reversed(range(n_layers)):
        own = {}
        hosts = {}
        plans = [(REDUCE_UPPER, li + 1, upper)] if upper is not None else []
        if li == 0:
            plans.append((REDUCE_OWN, 0, own))

            def last_rides(own=own):
                _run_payload(reduce.exchange(0, REDUCE_LAST, own)(), "pair_exchange_last")
                small["vec"] = small_vec()
                small["exchange"] = _small_exchange(small["vec"])
                return _join(reduce.chips(0, REDUCE_LAST, own)(), small["exchange"])
            hosts["mix_in_bwd"] = last_rides
        for (pair_host, chip_hosts), lj, grads in plans:
            groups = [g for _, gs_ in chip_hosts for g in gs_]
            hosts[pair_host] = reduce.exchange(lj, groups, grads)
            for chip_host, gs_ in chip_hosts:
                hosts[chip_host] = reduce.chips(lj, gs_, grads)
        dh, upper, _ = _layer_bwd(dh, p[li, 0], layer_w[li], saved[li], li, hosts, own, small_grads[li])
    grad_x = dh[None]
    reduced = reduce.finish()

    shared = _run_payload(_pair_share([reduced[g] for g in GROUPS]), "pair_share")
    red = {g: b.reshape(b.shape[:2] + (-1, b.shape[-1])) for g, b in zip(GROUPS, shared)}

    where = {"w_in": ("in", 0), "w_ffn_in": ("ffn_in", 0), "w_ffn_out": ("ffn_out", 0), "w_ple": ("mix", 0),
             "w_pool": ("mix", 1)}
    for slot, nm in enumerate(SQ):
        where[nm] = ("sq", slot)
    outs = {}
    for nm, (g, slot) in where.items():
        shape = wts[nm].shape
        to3 = lambda a: a.reshape((n_layers,) + red[g].shape[2:])
        res = _adamw(to3(wts[nm]), red[g], slot, to3(mom[nm]), to3(var[nm]), "adamw_" + nm)
        outs[nm] = [r.reshape(shape) for r in res]
    gdw = red["dw"][:, :, :CONV_WIDTH]
    to3 = lambda a: a.reshape(n_layers, CONV_WIDTH, -1)
    res = _adamw(to3(wts["w_dw"]), gdw, 0, to3(mom["w_dw"]), to3(var["w_dw"]), "adamw_w_dw")
    outs["w_dw"] = [r.reshape(wts["w_dw"].shape) for r in res]

    gmain = _small_sum(small["vec"], small["exchange"].results[0], pos, "small_sum")
    glate = _all_reduce_small(small_grads[late[0]][late[1]].reshape(-1, 128), "all_reduce_late")
    gsmall = jnp.concatenate([glate, gmain[glate.shape[0]:]], axis=0)
    pk = lambda tree: _pack_small({nm: tree[nm] for nm in SMALL})[None]
    res = _adamw(pk(wts), gsmall[None, None], 0, pk(mom), pk(var), "adamw_small")
    unpacked = [_unpack_small(r[0], wts) for r in res]
    for nm in SMALL:
        outs[nm] = [u[nm] for u in unpacked]

    result = [loss, grad_x]
    for k in range(4):
        result += [outs[nm][k] for nm in WEIGHTS]
    return tuple(result)
```

```python
import functools

import jax
import jax.numpy as jnp
from jax import lax
from jax.experimental import pallas as pl
from jax.experimental.pallas import tpu as pltpu

F32 = jnp.float32
BF16 = jnp.bfloat16
MESH = pl.DeviceIdType.MESH

EPS = 1e-6
D_MODEL = 1024
SGU_BLOCK = 128
SGU_HEADS = 8
CHUNK = 64
CONV_WIDTH = 31
CONV_HALO = 32
POOL_WINDOWS = (2, 4, 8, 16)
POOL_BLOCK = 128
POOL_GROUP = 256
D_FF = 2816
N_CHIPS = 4

ADAM_LR = 0.001
ADAM_B1 = 0.9
ADAM_B2 = 0.999
ADAM_EPS = 1e-08
ADAM_WD = 0.01
ADAM_STEP = 10

VMEM_LIMIT = 52 * 1024 * 1024
ROW_TILE = 512
ROW_TILE_HEAVY = 256
CONV_ROWS = 64
CONV_LANES = 128
EW_BLOCK_BYTES = 2 * 1024 * 1024
TOKEN_TILE = 2048
FF_CHUNK = 256


def _params(n_grid):
    return pltpu.CompilerParams(dimension_semantics=("arbitrary",) * n_grid, vmem_limit_bytes=VMEM_LIMIT)


def _dot(a, b):
    return jnp.dot(a.astype(BF16), b.astype(BF16), preferred_element_type=F32)


def _dot_nt(a, b):
    return lax.dot_general(a.astype(BF16), b.astype(BF16), (((1,), (1,)), ((), ())), preferred_element_type=F32)


def _dot_tn(a, b):
    return lax.dot_general(a.astype(BF16), b.astype(BF16), (((0,), (0,)), ((), ())), preferred_element_type=F32)


def _sigmoid(x):
    return 0.5 * jnp.tanh(0.5 * x) + 0.5


_GELU_C = 0.7978845608028654
_GELU_A = 0.044715


def _gelu(x):
    t = jnp.tanh(_GELU_C * (x + _GELU_A * x * x * x))
    return 0.5 * x * (1.0 + t)


def _gelu_and_grad(x):
    x2 = x * x
    t = jnp.tanh(_GELU_C * (x + _GELU_A * x2 * x))
    g = 0.5 * (1.0 + t) + 0.5 * x * (1.0 - t * t) * (_GELU_C * (1.0 + 3.0 * _GELU_A * x2))
    return 0.5 * x * (1.0 + t), g


def _rms_stats(x):
    r = lax.rsqrt(jnp.mean(x * x, axis=-1, keepdims=True) + EPS)
    return x * r, r


def _rms_bwd(xn, r, g, dy):
    gd = dy * g
    return r * (gd - xn * jnp.mean(gd * xn, axis=-1, keepdims=True)), dy * xn


def _ln_stats(x):
    mu = jnp.mean(x, axis=-1, keepdims=True)
    xc = x - mu
    rstd = lax.rsqrt(jnp.mean(xc * xc, axis=-1, keepdims=True) + EPS)
    return xc * rstd, rstd


def _ln_bwd(xhat, rstd, g, dy):
    dxh = dy * g
    return rstd * (dxh - jnp.mean(dxh, axis=-1, keepdims=True) - xhat * jnp.mean(dxh * xhat, axis=-1, keepdims=True))


def _rowsum(x):
    return jnp.sum(x, axis=0, keepdims=True)


def _tile(t, want):
    return min(t, want)


def _full(shape):
    n = len(shape)
    return pl.BlockSpec(shape, lambda *_: (0,) * n)


def _resident(shape):
    n = len(shape)
    return pl.BlockSpec(shape, lambda *_: (0,) * n, pipeline_mode=pl.Buffered(1))


class _Payload:
    def __init__(self, operands, out_shape, aliases, scratch, start, finish):
        self.operands, self.out_shape, self.aliases, self.scratch = list(operands), list(out_shape), aliases, scratch
        self.start, self.finish = start, finish
        self.results = None
        self.parts = None

    def deliver(self, results):
        self.results = list(results)
        if self.parts:
            a, b, ma = self.parts
            a.deliver(self.results[:ma])
            b.deliver(self.results[ma:])


def _pcall(body, *, name, grid, in_specs, out_specs, out_shape, operands, scratch_shapes=(), comm=None, aliases=None):
    single = not isinstance(out_shape, (list, tuple))
    out_specs = [out_specs] if single else list(out_specs)
    out_shape = [out_shape] if single else list(out_shape)
    aliases = dict(aliases or {})
    if comm is None:
        res = pl.pallas_call(
            body, name=name, grid=grid, in_specs=list(in_specs), out_specs=out_specs, out_shape=out_shape,
            scratch_shapes=list(scratch_shapes), input_output_aliases=aliases,
            compiler_params=_params(len(grid)))(*operands)
        return res[0] if single else res
    n_in, n_out, n_scr = len(in_specs), len(out_shape), len(scratch_shapes)
    ci, co = len(comm.operands), len(comm.out_shape)

    def hosted(*refs):
        bounds = [0, n_in, n_in + ci, n_in + ci + n_out, n_in + ci + n_out + co, n_in + ci + n_out + co + n_scr]
        a, b, c_, d_, s_ = [refs[lo:hi] for lo, hi in zip(bounds[:-1], bounds[1:])]
        t_ = refs[bounds[-1]:]
        ids = [pl.program_id(q) for q in range(len(grid))]
        first = functools.reduce(jnp.logical_and, [i == 0 for i in ids])
        last = functools.reduce(jnp.logical_and, [i == pl.num_programs(q) - 1 for q, i in enumerate(ids)])

        @pl.when(first)
        def _():
            comm.start(b, d_, t_)

        body(*a, *c_, *s_)

        @pl.when(last)
        def _():
            comm.finish(b, d_, t_)

    res = pl.pallas_call(
        hosted, name=name, grid=grid, in_specs=list(in_specs) + [ANY] * ci, out_specs=out_specs + [ANY] * co,
        out_shape=out_shape + comm.out_shape, scratch_shapes=list(scratch_shapes) + list(comm.scratch),
        input_output_aliases={**aliases, **{n_in + i: n_out + o for i, o in comm.aliases.items()}},
        compiler_params=pltpu.CompilerParams(dimension_semantics=("arbitrary",) * len(grid),
                                             vmem_limit_bytes=VMEM_LIMIT, has_side_effects=True),
    )(*operands, *comm.operands)
    comm.deliver(res[n_out:])
    res = res[:n_out]
    return res[0] if single else res


def _run_payload(comm, name):
    ci, co = len(comm.operands), len(comm.out_shape)

    def body(*refs):
        b, d_, t_ = refs[:ci], refs[ci:ci + co], refs[ci + co:]
        comm.start(b, d_, t_)
        comm.finish(b, d_, t_)

    res = pl.pallas_call(
        body, name=name, in_specs=[ANY] * ci, out_specs=[ANY] * co, out_shape=comm.out_shape,
        input_output_aliases=dict(comm.aliases), scratch_shapes=list(comm.scratch),
        compiler_params=pltpu.CompilerParams(has_side_effects=True))(*comm.operands)
    comm.deliver(res)
    return comm.results


def _take(hosts, key):
    return hosts[key]() if hosts and key in hosts else None


def _norm_mm(h, g, w4, name, comm=None):
    t, d = h.shape
    n = w4.shape[2]
    tm = _tile(t, ROW_TILE)

    step = _lane_block(n, 1024)

    def body(h_ref, g_ref, w_ref, o_ref, hn_ref):
        xn, _ = _rms_stats(h_ref[...])
        hn = (xn * g_ref[...]).astype(BF16)
        hn_ref[...] = hn
        for j in range(N_CHIPS):
            for c0 in range(0, n, step):
                o_ref[:, j * n + c0:j * n + c0 + step] = jnp.dot(
                    hn, w_ref[j, :, c0:c0 + step], preferred_element_type=F32).astype(BF16)

    return _pcall(
        body, name=name, grid=(t // tm,),
        in_specs=[pl.BlockSpec((tm, d), lambda i: (i, 0)), _full((1, d)), _resident(w4.shape)],
        out_specs=[pl.BlockSpec((tm, N_CHIPS * n), lambda i: (i, 0)), pl.BlockSpec((tm, d), lambda i: (i, 0))],
        out_shape=[jax.ShapeDtypeStruct((t, N_CHIPS * n), BF16), jax.ShapeDtypeStruct((t, d), BF16)],
        operands=(h, g, w4), comm=comm)


def _sgu_mask():
    ii = lax.broadcasted_iota(jnp.int32, (SGU_BLOCK, SGU_BLOCK), 0) // CHUNK
    jj = lax.broadcasted_iota(jnp.int32, (SGU_BLOCK, SGU_BLOCK), 1) // CHUNK
    return jj <= ii


def _sgu_fwd(proj, wm, bs3, gv, bv, name, comm=None):
    t = proj.shape[0]
    d = D_MODEL
    tm = _tile(t, ROW_TILE)
    hd = d // SGU_HEADS

    def body(zu_ref, zv_ref, wm_ref, bs_ref, gv_ref, bv_ref, o_ref):
        mask = _sgu_mask()
        for blk in range(tm // SGU_BLOCK):
            rows = pl.ds(blk * SGU_BLOCK, SGU_BLOCK)
            u = _gelu(zu_ref[rows, :].astype(F32))
            xhat, _ = _ln_stats(_gelu(zv_ref[rows, :].astype(F32)))
            vn = (xhat * gv_ref[...] + bv_ref[...]).astype(BF16)
            for hh in range(SGU_HEADS):
                cols = slice(hh * hd, (hh + 1) * hd)
                wmh = jnp.where(mask, wm_ref[hh], 0.0).astype(BF16)
                mixed = jnp.dot(wmh, vn[:, cols], preferred_element_type=F32) + bs_ref[hh]
                o_ref[rows, cols] = (u[:, cols] * mixed).astype(BF16)

    return _pcall(
        body, name=name, grid=(t // tm,),
        in_specs=[pl.BlockSpec((tm, d), lambda i: (i, 0)), pl.BlockSpec((tm, d), lambda i: (i, 1)),
                  _full(wm.shape), _full(bs3.shape), _full(gv.shape), _full(bv.shape)],
        out_specs=pl.BlockSpec((tm, d), lambda i: (i, 0)),
        out_shape=jax.ShapeDtypeStruct((t, d), BF16),
        operands=(proj, proj, wm, bs3, gv, bv), comm=comm)


def _conv_taps(scr_ref, r0, c0, base, weight):
    n = CONV_ROWS + CONV_HALO
    win = scr_ref[pl.ds(r0, n), pl.ds(c0, CONV_LANES)]
    acc = None
    for r in range(8):
        rolled = win if r == 0 else pltpu.roll(win, n - r, 0)
        for q in range((CONV_HALO + 7) // 8 + 1):
            k = 8 * q + r - base
            if 0 <= k < CONV_WIDTH and 8 * q + CONV_ROWS <= n:
                term = weight(k) * rolled[8 * q:8 * q + CONV_ROWS]
                acc = term if acc is None else acc + term
    return acc


def _glu_rows(a_ref, g_ref):
    return a_ref[...].astype(F32) * _sigmoid(g_ref[...].astype(F32))


def _conv_into(scr_ref, cv_ref, w_ref, tm, base, flip):
    def chunk(ci, carry):
        r0 = pl.multiple_of(ci * CONV_ROWS, CONV_ROWS)
        for c0 in range(0, D_MODEL, CONV_LANES):
            def weight(k, c0=c0):
                kk = CONV_WIDTH - 1 - k if flip else k
                return w_ref[kk:kk + 1, c0:c0 + CONV_LANES]
            cv_ref[pl.ds(r0, CONV_ROWS), pl.ds(c0, CONV_LANES)] = _conv_taps(scr_ref, r0, c0, base, weight)
        return carry

    lax.fori_loop(0, tm // CONV_ROWS, chunk, 0)


def _conv_specs(t, tm, d):
    hb = tm // CONV_HALO
    main = [pl.BlockSpec((tm, d), lambda i: (i, 2)), pl.BlockSpec((tm, d), lambda i: (i, 3))]
    halo = [pl.BlockSpec((CONV_HALO, d), lambda i: (jnp.maximum(i * hb - 1, 0), 2)),
            pl.BlockSpec((CONV_HALO, d), lambda i: (jnp.maximum(i * hb - 1, 0), 3))]
    return main, halo


def _fill_glu_history(scr_ref, a_ref, g_ref, ah_ref, gh_ref, tm):
    hist = _glu_rows(ah_ref, gh_ref)
    scr_ref[0:CONV_HALO, :] = jnp.where(pl.program_id(0) > 0, hist, 0.0)
    scr_ref[CONV_HALO:CONV_HALO + tm, :] = _glu_rows(a_ref, g_ref)


_CONV_BASE = CONV_HALO - (CONV_WIDTH - 1)


def _conv_fwd(proj, wdw, bdw, gln, bln, name, comm=None):
    t = proj.shape[0]
    d = D_MODEL
    tm = _tile(t, ROW_TILE)
    main, halo = _conv_specs(t, tm, d)

    def body(a_ref, g_ref, ah_ref, gh_ref, w_ref, b_ref, gl_ref, bl_ref, o_ref, cv_ref, scr_ref):
        _fill_glu_history(scr_ref, a_ref, g_ref, ah_ref, gh_ref, tm)
        _conv_into(scr_ref, cv_ref, w_ref, tm, _CONV_BASE, False)
        xhat, _ = _ln_stats(cv_ref[...] + b_ref[...])
        cn = xhat * gl_ref[...] + bl_ref[...]
        o_ref[...] = (cn * _sigmoid(cn)).astype(BF16)

    row = pl.BlockSpec((tm, d), lambda i: (i, 0))
    return _pcall(
        body, name=name, grid=(t // tm,),
        in_specs=main + halo + [_full(wdw.shape), _full(bdw.shape), _full(gln.shape), _full(bln.shape)],
        out_specs=[row, row],
        out_shape=[jax.ShapeDtypeStruct((t, d), BF16), jax.ShapeDtypeStruct((t, d), F32)],
        scratch_shapes=[pltpu.VMEM((tm + CONV_HALO, d), F32)],
        operands=(proj, proj, proj, proj, wdw, bdw, gln, bln), comm=comm)


def _pool_fill(scr_ref, z_ref, zh_ref, tm):
    scr_ref[0:POOL_BLOCK, :] = jnp.where(pl.program_id(0) > 0, zh_ref[...], jnp.zeros_like(zh_ref))
    scr_ref[POOL_BLOCK:POOL_BLOCK + tm, :] = z_ref[...]


def _pool_count(t0, rows, w):
    pos = (t0 + lax.broadcasted_iota(jnp.int32, (rows, 1), 0) + 1).astype(F32)
    return jnp.minimum(pos, float(w))


def _band(w, leading):
    i = lax.broadcasted_iota(jnp.int32, (POOL_BLOCK, 2 * POOL_BLOCK), 0)
    j = lax.broadcasted_iota(jnp.int32, (POOL_BLOCK, 2 * POOL_BLOCK), 1)
    off = j - i if leading else POOL_BLOCK + i - j
    return jnp.where((off >= 0) & (off < w), 1.0, 0.0).astype(BF16)


def _window_sums(refs, band, tm, cols):
    blocks = []
    for b in range(tm // POOL_BLOCK):
        rows = pl.ds(b * POOL_BLOCK, 2 * POOL_BLOCK)
        acc = None
        for ref in refs:
            term = jnp.dot(band, ref[rows, cols], preferred_element_type=F32)
            acc = term if acc is None else acc + term
        blocks.append(acc)
    return blocks[0] if len(blocks) == 1 else jnp.concatenate(blocks, axis=0)


def _pooled_group(scr_ref, gi, w, tm, t0):
    cols = pl.ds(gi * POOL_GROUP, POOL_GROUP)
    sums = _window_sums([scr_ref], _band(w, False), tm, cols)
    return sums / _pool_count(t0, tm, w) - scr_ref[pl.ds(POOL_BLOCK, tm), cols].astype(F32)


def _pool_specs(tm, d):
    hb = tm // POOL_BLOCK
    return [pl.BlockSpec((tm, d), lambda i: (i, 4)),
            pl.BlockSpec((POOL_BLOCK, d), lambda i: (jnp.maximum(i * hb - 1, 0), 4))]


def _pool_fwd(proj, wpool, spool, name, comm=None):
    t = proj.shape[0]
    d = D_MODEL
    tm = _tile(t, ROW_TILE)

    def body(z_ref, zh_ref, w_ref, s_ref, o_ref, scr_ref):
        _pool_fill(scr_ref, z_ref, zh_ref, tm)
        t0 = pl.program_id(0) * tm
        for gi, w in enumerate(POOL_WINDOWS):
            cols = slice(gi * POOL_GROUP, (gi + 1) * POOL_GROUP)
            pooled = _pooled_group(scr_ref, gi, w, tm, t0)
            o_ref[:, cols] = (_dot(pooled, w_ref[gi]) * s_ref[:, cols]).astype(BF16)

    return _pcall(
        body, name=name, grid=(t // tm,),
        in_specs=_pool_specs(tm, d) + [_full(wpool.shape), _full(spool.shape)],
        out_specs=pl.BlockSpec((tm, d), lambda i: (i, 0)),
        out_shape=jax.ShapeDtypeStruct((t, d), BF16),
        scratch_shapes=[pltpu.VMEM((tm + POOL_BLOCK, d), BF16)],
        operands=(proj, proj, wpool, spool), comm=comm)


def _merge_fwd(proj, sg, cs, ps, wa, wb, wc, name, comm=None):
    t = proj.shape[0]
    d = D_MODEL
    tm = _tile(t, ROW_TILE_HEAVY)

    def body(za_ref, zb_ref, zc_ref, sg_ref, cs_ref, ps_ref, wa_ref, wb_ref, wc_ref, ba_ref, bb_ref, bc_ref, m_ref):
        merged = None
        for z_ref, x_ref, w_ref, b_ref in ((za_ref, sg_ref, wa_ref, ba_ref), (zb_ref, cs_ref, wb_ref, bb_ref),
                                           (zc_ref, ps_ref, wc_ref, bc_ref)):
            br = jnp.dot(x_ref[...], w_ref[...], preferred_element_type=F32)
            b_ref[...] = br.astype(BF16)
            term = _sigmoid(z_ref[...].astype(F32)) * br
            merged = term if merged is None else merged + term
        m_ref[...] = merged.astype(BF16)

    row = pl.BlockSpec((tm, d), lambda i: (i, 0))
    wspec = _resident((d, d))
    return _pcall(
        body, name=name, grid=(t // tm,),
        in_specs=[pl.BlockSpec((tm, d), lambda i: (i, 5)), pl.BlockSpec((tm, d), lambda i: (i, 6)),
                  pl.BlockSpec((tm, d), lambda i: (i, 7)), row, row, row, wspec, wspec, wspec],
        out_specs=[row, row, row, row],
        out_shape=[jax.ShapeDtypeStruct((t, d), BF16)] * 4,
        operands=(proj, proj, proj, sg, cs, ps, wa, wb, wc), comm=comm)


def _mm_norm_res(a, w, g, hres, name, comm=None):
    t, k = a.shape
    d = w.shape[1]
    tm = _tile(t, ROW_TILE)

    def body(a_ref, w_ref, g_ref, h_ref, y_ref, o_ref):
        y = jnp.dot(a_ref[...], w_ref[...], preferred_element_type=F32)
        y_ref[...] = y
        yn, _ = _rms_stats(y)
        o_ref[...] = h_ref[...] + yn * g_ref[...]

    row = pl.BlockSpec((tm, d), lambda i: (i, 0))
    return _pcall(
        body, name=name, grid=(t // tm,),
        in_specs=[pl.BlockSpec((tm, k), lambda i: (i, 0)), _resident(w.shape), _full(g.shape), row],
        out_specs=[row, row],
        out_shape=[jax.ShapeDtypeStruct((t, d), F32)] * 2,
        operands=(a, w, g, hres), comm=comm)


def _ffn_in(h, g, w4, name, comm=None):
    t, d = h.shape
    n = w4.shape[2]
    tm = _tile(t, ROW_TILE)
    nj = D_FF // n

    def body(h_ref, g_ref, w_ref, fg_ref, fu_ref, act_ref, hn_ref):
        xn, _ = _rms_stats(h_ref[...])
        hn = (xn * g_ref[...]).astype(BF16)
        hn_ref[...] = hn
        for j in range(nj):
            cols = slice(j * n, (j + 1) * n)
            fg = jnp.dot(hn, w_ref[j], preferred_element_type=F32)
            fu = jnp.dot(hn, w_ref[j + nj], preferred_element_type=F32)
            fg_ref[:, cols] = fg.astype(BF16)
            fu_ref[:, cols] = fu.astype(BF16)
            act_ref[:, cols] = (fg * _sigmoid(fg) * fu).astype(BF16)

    wide = pl.BlockSpec((tm, D_FF), lambda i: (i, 0))
    return _pcall(
        body, name=name, grid=(t // tm,),
        in_specs=[pl.BlockSpec((tm, d), lambda i: (i, 0)), _full((1, d)), _resident(w4.shape)],
        out_specs=[wide, wide, wide, pl.BlockSpec((tm, d), lambda i: (i, 0))],
        out_shape=[jax.ShapeDtypeStruct((t, D_FF), BF16)] * 3 + [jax.ShapeDtypeStruct((t, d), BF16)],
        operands=(h, g, w4), comm=comm)


def _ple_fwd(h, p, wg, wp, name, comm=None):
    t, d = h.shape
    tm = _tile(t, ROW_TILE)

    def body(h_ref, p_ref, wg_ref, wp_ref, o_ref, q_ref, e_ref):
        hh = h_ref[...]
        q = _dot(hh, wg_ref[...])
        e = _dot(p_ref[...], wp_ref[...])
        q_ref[...] = q.astype(BF16)
        e_ref[...] = e.astype(BF16)
        o_ref[...] = hh + _sigmoid(q) * e

    row = pl.BlockSpec((tm, d), lambda i: (i, 0))
    return _pcall(
        body, name=name, grid=(t // tm,),
        in_specs=[row, pl.BlockSpec((tm, p.shape[1]), lambda i: (i, 0)), _resident(wg.shape), _resident(wp.shape)],
        out_specs=[row, row, row],
        out_shape=[jax.ShapeDtypeStruct((t, d), F32), jax.ShapeDtypeStruct((t, d), BF16),
                   jax.ShapeDtypeStruct((t, d), BF16)],
        operands=(h, p, wg, wp), comm=comm)


def _loss_head(y, target, name):
    t, d = y.shape
    tm = _tile(t, ROW_TILE)

    def body(y_ref, t_ref, dy_ref, l_ref):
        @pl.when(pl.program_id(0) == 0)
        def _():
            l_ref[...] = jnp.zeros_like(l_ref)

        err = y_ref[...] - t_ref[...]
        dy_ref[...] = err * (1.0 / d)
        l_ref[...] += jnp.sum(err * err, keepdims=True)[:, :1] * jnp.ones((1, 128), F32)

    row = pl.BlockSpec((tm, d), lambda i: (i, 0))
    return pl.pallas_call(
        body, name=name, grid=(t // tm,),
        in_specs=[row, row], out_specs=[row, _full((1, 128))],
        out_shape=[jax.ShapeDtypeStruct((t, d), F32), jax.ShapeDtypeStruct((1, 128), F32)],
        compiler_params=_params(1))(y, target)


def _paired_accumulate(acc, x_st, dy_st, x, dy):
    tm = x.shape[0]
    i, n = pl.program_id(0), pl.num_programs(0)
    odd = i % 2 == 1

    @pl.when(jnp.logical_not(odd))
    def _():
        x_st[0:tm, :] = x
        dy_st[0:tm, :] = dy

    @pl.when(odd)
    def _():
        x_st[tm:2 * tm, :] = x
        dy_st[tm:2 * tm, :] = dy
        acc[...] += _dot_tn(x_st[...], dy_st[...])

    @pl.when(jnp.logical_not(odd) & (i == n - 1))
    def _():
        acc[...] += _dot_tn(x, dy)


def _flush_slots(acc_ref, out_ref, slots, sem_ref):
    rows = out_ref.shape[2]

    @pl.when(pl.program_id(0) == pl.num_programs(0) - 1)
    def _():
        copies = [pltpu.make_async_copy(acc_ref.at[k, pl.ds(j * rows, rows)], out_ref.at[j, slot],
                                        sem_ref.at[N_CHIPS * k + j])
                  for k, slot in enumerate(slots) for j in range(N_CHIPS)]
        for cp in copies:
            cp.start()
        for cp in copies:
            cp.wait()


def _ple_bwd(dh, q, e, wg, h_in, p, slot, n_slots, name):
    t, d = dh.shape
    tm = _tile(t, ROW_TILE)
    rows = d // N_CHIPS

    def body(dh_ref, q_ref, e_ref, wg_ref, h_ref, p_ref, o_ref, sq_ref, dwp_ref, acc_ref, sem_ref, xst_ref, dyst_ref):
        @pl.when(pl.program_id(0) == 0)
        def _():
            acc_ref[...] = jnp.zeros_like(acc_ref)
            dwp_ref[...] = jnp.zeros_like(dwp_ref)

        dh_ = dh_ref[...]
        s = _sigmoid(q_ref[...].astype(F32))
        dq = (dh_ * e_ref[...].astype(F32) * s * (1.0 - s)).astype(BF16)
        o_ref[...] = dh_ + _dot_nt(dq, wg_ref[...])
        _paired_accumulate(acc_ref.at[0], xst_ref, dyst_ref, h_ref[...].astype(BF16), dq)
        dwp_ref[...] += _dot_tn(p_ref[...], dh_ * s)
        _flush_slots(acc_ref, sq_ref, (slot,), sem_ref)

    row = pl.BlockSpec((tm, d), lambda i: (i, 0))
    return _pcall(
        body, name=name, grid=(t // tm,),
        in_specs=[row, row, row, _resident(wg.shape), row, pl.BlockSpec((tm, p.shape[1]), lambda i: (i, 0))],
        out_specs=[row, ANY, _full((p.shape[1], d))],
        out_shape=[jax.ShapeDtypeStruct((t, d), F32), jax.ShapeDtypeStruct((N_CHIPS, n_slots, rows, d), F32),
                   jax.ShapeDtypeStruct((p.shape[1], d), F32)],
        scratch_shapes=[pltpu.VMEM((1, d, d), F32), pltpu.SemaphoreType.DMA((N_CHIPS,)),
                        pltpu.VMEM((2 * tm, d), BF16), pltpu.VMEM((2 * tm, d), BF16)],
        operands=(dh, q, e, wg, h_in, p))


def _ffn_out_bwd(dh, f, g, fg, fu, act, w, name, comm=None):
    t, d = dh.shape
    tm = _tile(t, ROW_TILE_HEAVY)

    def body(dh_ref, f_ref, g_ref, fg_ref, fu_ref, act_ref, w_ref, dff_ref, dg_ref, dw_ref, acc_ref, sem_ref,
             xst_ref, dyst_ref):
        @pl.when(pl.program_id(0) == 0)
        def _():
            dg_ref[...] = jnp.zeros_like(dg_ref)
            acc_ref[...] = jnp.zeros_like(acc_ref)

        fn, r = _rms_stats(f_ref[...])
        df, dgt = _rms_bwd(fn, r, g_ref[...], dh_ref[...])
        dg_ref[...] += _rowsum(dgt)
        df = df.astype(BF16)
        _paired_accumulate(acc_ref, xst_ref, dyst_ref, act_ref[...], df)
        for c0 in range(0, D_FF, FF_CHUNK):
            cols = slice(c0, c0 + FF_CHUNK)
            dact = _dot_nt(df, w_ref[cols, :])
            fg_ = fg_ref[:, cols].astype(F32)
            s = _sigmoid(fg_)
            gs = fg_ * s
            dff_ref[:, cols] = (dact * fu_ref[:, cols].astype(F32) * (s + gs - gs * s)).astype(BF16)
            dff_ref[:, D_FF + c0:D_FF + c0 + FF_CHUNK] = (dact * gs).astype(BF16)

        @pl.when(pl.program_id(0) == pl.num_programs(0) - 1)
        def _():
            cp = pltpu.make_async_copy(acc_ref, dw_ref, sem_ref.at[0])
            cp.start()
            cp.wait()

    row = pl.BlockSpec((tm, d), lambda i: (i, 0))
    wide = pl.BlockSpec((tm, D_FF), lambda i: (i, 0))
    return _pcall(
        body, name=name, grid=(t // tm,),
        in_specs=[row, row, _full(g.shape), wide, wide, wide, _resident(w.shape)],
        out_specs=[pl.BlockSpec((tm, 2 * D_FF), lambda i: (i, 0)), _full((1, d)), ANY],
        out_shape=[jax.ShapeDtypeStruct((t, 2 * D_FF), BF16), jax.ShapeDtypeStruct((1, d), F32),
                   jax.ShapeDtypeStruct((D_FF, d), F32)],
        scratch_shapes=[pltpu.VMEM((D_FF, d), F32), pltpu.SemaphoreType.DMA((1,)),
                        pltpu.VMEM((2 * tm, D_FF), BF16), pltpu.VMEM((2 * tm, d), BF16)],
        operands=(dh, f, g, fg, fu, act, w), comm=comm)


def _in_bwd(pieces, w4, unit, h, g, dres, tm, name, comm=None):
    t, d = h.shape
    tm = _tile(t, tm)
    per_chunk = w4.shape[2] // unit
    n_p = len(pieces)

    def body(*refs):
        p_refs = refs[:n_p]
        w_ref, h_ref, g_ref, r_ref, o_ref, dg_ref = refs[n_p:]

        @pl.when(pl.program_id(0) == 0)
        def _():
            dg_ref[...] = jnp.zeros_like(dg_ref)

        acc = None
        u = 0
        for p_ref, (_, nu) in zip(p_refs, pieces):
            for k in range(nu):
                lanes = slice((u % per_chunk) * unit, (u % per_chunk + 1) * unit)
                term = _dot_nt(p_ref[:, k * unit:(k + 1) * unit], w_ref[u // per_chunk, :, lanes])
                acc = term if acc is None else acc + term
                u += 1
        xn, r = _rms_stats(h_ref[...])
        dx, dgt = _rms_bwd(xn, r, g_ref[...], acc)
        dg_ref[...] += _rowsum(dgt)
        o_ref[...] = r_ref[...] + dx

    row = pl.BlockSpec((tm, d), lambda i: (i, 0))
    return _pcall(
        body, name=name, grid=(t // tm,),
        in_specs=[pl.BlockSpec((tm, a.shape[1]), lambda i: (i, 0)) for a, _ in pieces]
        + [_resident(w4.shape), row, _full((1, d)), row],
        out_specs=[row, _full((1, d))],
        out_shape=[jax.ShapeDtypeStruct((t, d), F32), jax.ShapeDtypeStruct((1, d), F32)],
        operands=(*[a for a, _ in pieces], w4, h, g, dres), comm=comm)


def _lane_block(n, cap):
    return max(b for b in range(128, min(n, cap) + 1, 128) if n % b == 0)


def _dw_cols(x, pieces, unit, per_chunk, name):
    t, m = x.shape
    tk = _tile(t, TOKEN_TILE)
    offs, total = [], 0
    for _, nu in pieces:
        offs.append(total)
        total += nu

    def body(x_ref, *refs):
        o_ref = refs[-1]
        u = pl.program_id(0)

        @pl.when(pl.program_id(1) == 0)
        def _():
            o_ref[...] = jnp.zeros_like(o_ref)

        for p_ref, off, (_, nu) in zip(refs[:-1], offs, pieces):
            @pl.when((u >= off) & (u < off + nu))
            def _(p_ref=p_ref):
                o_ref[...] += _dot_tn(x_ref[...], p_ref[...])

    def piece_spec(off, nu):
        def index(u, k):
            mine = (u >= off) & (u < off + nu)
            return jnp.where(mine, k, 0), jnp.clip(u - off, 0, nu - 1)
        return pl.BlockSpec((tk, unit), index)

    return pl.pallas_call(
        body, name=name, grid=(total, t // tk),
        in_specs=[pl.BlockSpec((tk, m), lambda u, k: (k, 0))] + [piece_spec(o, nu) for o, (_, nu) in zip(offs, pieces)],
        out_specs=pl.BlockSpec((None, None, m, unit), lambda u, k: (u // per_chunk, 0, 0, u % per_chunk)),
        out_shape=jax.ShapeDtypeStruct((N_CHIPS, 1, m, per_chunk * unit), F32),
        compiler_params=_params(2))(x, *[a for a, _ in pieces])


def _merge_bwd(dh, mo, g, proj, br, xs, merged, w_out, ws, sq, slots, name, comm=None):
    t, d = dh.shape
    tm = _tile(t, ROW_TILE_HEAVY)
    rows = d // N_CHIPS

    def body(dh_ref, mo_ref, g_ref, za_ref, zb_ref, zc_ref, ba_ref, bb_ref, bc_ref, xa_ref, xb_ref, xc_ref, m_ref,
             wo_ref, wa_ref, wb_ref, wc_ref, sq_in_ref, dz_ref, dsg_ref, dcs_ref, dps_ref, dg_ref, sq_ref,
             acc_ref, sem_ref):
        @pl.when(pl.program_id(0) == 0)
        def _():
            dg_ref[...] = jnp.zeros_like(dg_ref)
            acc_ref[...] = jnp.zeros_like(acc_ref)

        mon, r = _rms_stats(mo_ref[...])
        dmo, dgt = _rms_bwd(mon, r, g_ref[...], dh_ref[...])
        dg_ref[...] += _rowsum(dgt)
        dmo = dmo.astype(BF16)
        acc_ref[0] += _dot_tn(m_ref[...], dmo)
        dmerged = _dot_nt(dmo, wo_ref[...])
        branches = ((za_ref, ba_ref, xa_ref, wa_ref, dsg_ref), (zb_ref, bb_ref, xb_ref, wb_ref, dcs_ref),
                    (zc_ref, bc_ref, xc_ref, wc_ref, dps_ref))
        for j, (z_ref, b_ref, x_ref, w_ref, dx_ref) in enumerate(branches):
            gate = _sigmoid(z_ref[...].astype(F32))
            dbr = (dmerged * gate).astype(BF16)
            dz_ref[:, j * d:(j + 1) * d] = (dmerged * b_ref[...].astype(F32) * gate * (1.0 - gate)).astype(BF16)
            dx_ref[...] = _dot_nt(dbr, w_ref[...]).astype(BF16)
            acc_ref[1 + j] += _dot_tn(x_ref[...], dbr)
        _flush_slots(acc_ref, sq_ref, slots, sem_ref)

    row = pl.BlockSpec((tm, d), lambda i: (i, 0))
    wspec = _resident((d, d))
    bf = jax.ShapeDtypeStruct((t, d), BF16)
    n_in = 18
    return _pcall(
        body, name=name, grid=(t // tm,),
        in_specs=[row, row, _full(g.shape), pl.BlockSpec((tm, d), lambda i: (i, 5)),
                  pl.BlockSpec((tm, d), lambda i: (i, 6)), pl.BlockSpec((tm, d), lambda i: (i, 7)),
                  row, row, row, row, row, row, row, wspec, wspec, wspec, wspec, ANY],
        out_specs=[pl.BlockSpec((tm, 3 * d), lambda i: (i, 0)), row, row, row, _full((1, d)), ANY],
        out_shape=[jax.ShapeDtypeStruct((t, 3 * d), BF16), bf, bf, bf, jax.ShapeDtypeStruct((1, d), F32),
                   jax.ShapeDtypeStruct(sq.shape, sq.dtype)],
        scratch_shapes=[pltpu.VMEM((4, d, d), F32), pltpu.SemaphoreType.DMA((4 * N_CHIPS,))],
        operands=(dh, mo, g, proj, proj, proj, *br, *xs, merged, w_out, *ws, sq), comm=comm,
        aliases={n_in - 1: 5})


def _sgu_bwd(proj, dsg, wm, bs3, gv, bv, name, comm=None):
    t = proj.shape[0]
    d = D_MODEL
    tm = _tile(t, ROW_TILE_HEAVY)
    hd = d // SGU_HEADS

    def body(zu_ref, zv_ref, d_ref, wm_ref, bs_ref, gv_ref, bv_ref, dz_ref, dwm_ref, dbs_ref, dgv_ref, dbv_ref,
             dvn_ref):
        @pl.when(pl.program_id(0) == 0)
        def _():
            dwm_ref[...] = jnp.zeros_like(dwm_ref)
            dbs_ref[...] = jnp.zeros_like(dbs_ref)
            dgv_ref[...] = jnp.zeros_like(dgv_ref)
            dbv_ref[...] = jnp.zeros_like(dbv_ref)

        mask = _sgu_mask()
        for blk in range(tm // SGU_BLOCK):
            rows = pl.ds(blk * SGU_BLOCK, SGU_BLOCK)
            u, du_dz = _gelu_and_grad(zu_ref[rows, :].astype(F32))
            v0, dv_dz = _gelu_and_grad(zv_ref[rows, :].astype(F32))
            xhat, rstd = _ln_stats(v0)
            vn = (xhat * gv_ref[...] + bv_ref[...]).astype(BF16)
            dsg = d_ref[rows, :].astype(F32)
            dmix = (dsg * u).astype(BF16)
            for hh in range(SGU_HEADS):
                cols = slice(hh * hd, (hh + 1) * hd)
                wmh = jnp.where(mask, wm_ref[hh], 0.0).astype(BF16)
                vb = vn[:, cols]
                mixed = jnp.dot(wmh, vb, preferred_element_type=F32) + bs_ref[hh]
                dz_ref[rows, cols] = (dsg[:, cols] * mixed * du_dz[:, cols]).astype(BF16)
                dmh = dmix[:, cols]
                dwm_ref[hh] += jnp.where(mask, _dot_nt(dmh, vb), 0.0)
                dbs_ref[hh] += jnp.sum(dmh.astype(F32), axis=1, keepdims=True)
                dvn_ref[:, cols] = _dot_tn(wmh, dmh)
            dvn = dvn_ref[...]
            dgv_ref[...] += _rowsum(dvn * xhat)
            dbv_ref[...] += _rowsum(dvn)
            dz_ref[rows, d:2 * d] = (_ln_bwd(xhat, rstd, gv_ref[...], dvn) * dv_dz).astype(BF16)

    return _pcall(
        body, name=name, grid=(t // tm,),
        in_specs=[pl.BlockSpec((tm, d), lambda i: (i, 0)), pl.BlockSpec((tm, d), lambda i: (i, 1)),
                  pl.BlockSpec((tm, d), lambda i: (i, 0)), _full(wm.shape), _full(bs3.shape), _full(gv.shape),
                  _full(bv.shape)],
        out_specs=[pl.BlockSpec((tm, 2 * d), lambda i: (i, 0)), _full(wm.shape), _full(bs3.shape), _full((1, d)),
                   _full((1, d))],
        out_shape=[jax.ShapeDtypeStruct((t, 2 * d), BF16), jax.ShapeDtypeStruct(wm.shape, F32),
                   jax.ShapeDtypeStruct(bs3.shape, F32), jax.ShapeDtypeStruct((1, d), F32),
                   jax.ShapeDtypeStruct((1, d), F32)],
        scratch_shapes=[pltpu.VMEM((SGU_BLOCK, d), F32)],
        operands=(proj, proj, dsg, wm, bs3, gv, bv), comm=comm)


def _conv_bwd_norm(proj, dcs, cv, bdw, gln, bln, name, comm=None):
    t = proj.shape[0]
    d = D_MODEL
    tm = _tile(t, ROW_TILE)
    main, halo = _conv_specs(t, tm, d)
    n_win = CONV_ROWS + CONV_HALO

    def body(a_ref, g_ref, ah_ref, gh_ref, dcs_ref, cv_ref, b_ref, gl_ref, bl_ref,
             dcv_ref, dw_ref, db_ref, dgl_ref, dbl_ref, scr_ref, dwacc_ref):
        @pl.when(pl.program_id(0) == 0)
        def _():
            dwacc_ref[...] = jnp.zeros_like(dwacc_ref)
            db_ref[...] = jnp.zeros_like(db_ref)
            dgl_ref[...] = jnp.zeros_like(dgl_ref)
            dbl_ref[...] = jnp.zeros_like(dbl_ref)

        _fill_glu_history(scr_ref, a_ref, g_ref, ah_ref, gh_ref, tm)
        xhat, rstd = _ln_stats(cv_ref[...] + b_ref[...])
        cn = xhat * gl_ref[...] + bl_ref[...]
        s = _sigmoid(cn)
        dcn = dcs_ref[...].astype(F32) * (s * (1.0 + cn * (1.0 - s)))
        dgl_ref[...] += _rowsum(dcn * xhat)
        dbl_ref[...] += _rowsum(dcn)
        dcv = _ln_bwd(xhat, rstd, gl_ref[...], dcn)
        db_ref[...] += _rowsum(dcv)
        dcv_ref[...] = dcv

        def chunk(ci, carry):
            r0 = pl.multiple_of(ci * CONV_ROWS, CONV_ROWS)
            for c0 in range(0, d, CONV_LANES):
                lanes = pl.ds(c0, CONV_LANES)
                win = scr_ref[pl.ds(r0, n_win), lanes]
                dchunk = dcv_ref[pl.ds(r0, CONV_ROWS), lanes]
                for r in range(8):
                    rolled = win if r == 0 else pltpu.roll(win, n_win - r, 0)
                    for q in range(n_win // 8):
                        k = 8 * q + r - _CONV_BASE
                        if 0 <= k < CONV_WIDTH and 8 * q + CONV_ROWS <= n_win:
                            prod = dchunk * rolled[8 * q:8 * q + CONV_ROWS]
                            part = prod[0:8]
                            for s8 in range(8, CONV_ROWS, 8):
                                part = part + prod[s8:s8 + 8]
                            dwacc_ref[pl.ds(8 * k, 8), lanes] += part
            return carry

        lax.fori_loop(0, tm // CONV_ROWS, chunk, 0)

        @pl.when(pl.program_id(0) == pl.num_programs(0) - 1)
        def _():
            dw_ref[...] = jnp.sum(dwacc_ref[...].reshape(CONV_HALO, 8, d), axis=1)

    row = pl.BlockSpec((tm, d), lambda i: (i, 0))
    vec = _full((1, d))
    return _pcall(
        body, name=name, grid=(t // tm,),
        in_specs=main + halo + [row, row, vec, vec, vec],
        out_specs=[row, _full((CONV_HALO, d)), vec, vec, vec],
        out_shape=[jax.ShapeDtypeStruct((t, d), F32), jax.ShapeDtypeStruct((CONV_HALO, d), F32)]
        + [jax.ShapeDtypeStruct((1, d), F32)] * 3,
        scratch_shapes=[pltpu.VMEM((tm + CONV_HALO, d), F32), pltpu.VMEM((8 * CONV_HALO, d), F32)],
        operands=(proj, proj, proj, proj, dcs, cv, bdw, gln, bln), comm=comm)


def _conv_bwd_taps(proj, dcv, wdw, name, comm=None):
    t = proj.shape[0]
    d = D_MODEL
    tm = _tile(t, ROW_TILE)
    hb = tm // CONV_HALO
    last_halo = t // CONV_HALO - 1

    def body(a_ref, g_ref, dcv_ref, dnext_ref, w_ref, dz_ref, scr_ref, dh_ref):
        scr_ref[0:tm, :] = dcv_ref[...]
        is_last = pl.program_id(0) == pl.num_programs(0) - 1
        scr_ref[tm:tm + CONV_HALO, :] = jnp.where(is_last, 0.0, dnext_ref[...])
        _conv_into(scr_ref, dh_ref, w_ref, tm, 0, True)
        dglu = dh_ref[...]
        a = a_ref[...].astype(F32)
        s = _sigmoid(g_ref[...].astype(F32))
        dz_ref[:, 0:d] = (dglu * s).astype(BF16)
        dz_ref[:, d:2 * d] = (dglu * a * s * (1.0 - s)).astype(BF16)

    return _pcall(
        body, name=name, grid=(t // tm,),
        in_specs=[pl.BlockSpec((tm, d), lambda i: (i, 2)), pl.BlockSpec((tm, d), lambda i: (i, 3)),
                  pl.BlockSpec((tm, d), lambda i: (i, 0)),
                  pl.BlockSpec((CONV_HALO, d), lambda i: (jnp.minimum((i + 1) * hb, last_halo), 0)),
                  _full(wdw.shape)],
        out_specs=pl.BlockSpec((tm, 2 * d), lambda i: (i, 0)),
        out_shape=jax.ShapeDtypeStruct((t, 2 * d), BF16),
        scratch_shapes=[pltpu.VMEM((tm + CONV_HALO, d), F32), pltpu.VMEM((tm, d), F32)],
        operands=(proj, proj, dcv, dcv, wdw), comm=comm)


def _pool_bwd(proj, dps, wpool, spool, name):
    t = proj.shape[0]
    d = D_MODEL
    tm = _tile(t, ROW_TILE)
    hb = tm // POOL_BLOCK
    last_halo = t // POOL_BLOCK - 1
    ext = tm + POOL_BLOCK

    def body(z_ref, zh_ref, d_ref, dnext_ref, w_ref, s_ref, dz_ref, dw_ref, ds_ref, scr_ref, dext_ref, hi_ref, lo_ref):
        @pl.when(pl.program_id(0) == 0)
        def _():
            dw_ref[...] = jnp.zeros_like(dw_ref)
            ds_ref[...] = jnp.zeros_like(ds_ref)

        _pool_fill(scr_ref, z_ref, zh_ref, tm)
        t0 = pl.program_id(0) * tm
        is_last = pl.program_id(0) == pl.num_programs(0) - 1
        dext_ref[0:tm, :] = d_ref[...].astype(F32)
        dext_ref[tm:ext, :] = jnp.where(is_last, 0.0, dnext_ref[...].astype(F32))
        for gi, w in enumerate(POOL_WINDOWS):
            cols = slice(gi * POOL_GROUP, (gi + 1) * POOL_GROUP)
            dps_ext = dext_ref[:, cols]
            dpm_ext = (dps_ext * s_ref[:, cols]).astype(BF16)
            dpooled_ext = _dot_nt(dpm_ext, w_ref[gi])
            dq = dpooled_ext / _pool_count(t0, ext, w)
            hi = dq.astype(BF16)
            hi_ref[...] = hi
            lo_ref[...] = (dq - hi.astype(F32)).astype(BF16)
            sums = _window_sums([hi_ref, lo_ref], _band(w, True), tm, slice(None))
            dz_ref[:, cols] = (sums - dpooled_ext[0:tm]).astype(BF16)
            pooled = _pooled_group(scr_ref, gi, w, tm, t0).astype(BF16)
            pm = jnp.dot(pooled, w_ref[gi], preferred_element_type=F32)
            ds_ref[:, cols] += _rowsum(dps_ext[0:tm] * pm)
            dw_ref[gi] += _dot_tn(pooled, dpm_ext[0:tm])

    return pl.pallas_call(
        body, name=name, grid=(t // tm,),
        in_specs=_pool_specs(tm, d) + [pl.BlockSpec((tm, d), lambda i: (i, 0)),
                                       pl.BlockSpec((POOL_BLOCK, d), lambda i: (jnp.minimum((i + 1) * hb, last_halo), 0)),
                                       _full(wpool.shape), _full(spool.shape)],
        out_specs=[pl.BlockSpec((tm, d), lambda i: (i, 0)), _full(wpool.shape), _full((1, d))],
        out_shape=[jax.ShapeDtypeStruct((t, d), BF16), jax.ShapeDtypeStruct(wpool.shape, F32),
                   jax.ShapeDtypeStruct((1, d), F32)],
        scratch_shapes=[pltpu.VMEM((tm + POOL_BLOCK, d), BF16), pltpu.VMEM((ext, d), F32),
                        pltpu.VMEM((ext, POOL_GROUP), BF16), pltpu.VMEM((ext, POOL_GROUP), BF16)],
        compiler_params=_params(1))(proj, proj, dps, dps, wpool, spool)


ANY = pl.BlockSpec(memory_space=pl.ANY)


def _mesh_pos():
    x, y, c = lax.axis_index("x"), lax.axis_index("y"), lax.axis_index("c")
    chips = [(1 - x, y), (x, 1 - y), (1 - x, 1 - y)]
    return x, y, c, chips


def _chip_of(xy):
    return 2 * xy[0] + xy[1]


def _half_view(a):
    return a.reshape(a.shape[:-2] + (2, a.shape[-2] // 2, a.shape[-1]))


def _same(arrs):
    return [jax.ShapeDtypeStruct(a.shape, a.dtype) for a in arrs]


def _in_place(n):
    return {g: g for g in range(n)}


def _sems(count):
    return [pltpu.SemaphoreType.DMA((count,)), pltpu.SemaphoreType.DMA((count,))]


def _gather_ici(bufs):
    n = len(bufs)

    def copy(buf, sems, g, j, chip):
        x, y, c, chips = _mesh_pos()
        slab = buf[g].at[chip, :, c]
        return pltpu.make_async_remote_copy(
            src_ref=slab, dst_ref=slab, send_sem=sems[0].at[3 * g + j], recv_sem=sems[1].at[3 * g + j],
            device_id=(*chips[j], c), device_id_type=MESH)

    def start(ins, buf, sems):
        x, y, c, chips = _mesh_pos()
        for g in range(n):
            for j in range(3):
                copy(buf, sems, g, j, 2 * x + y).start()

    def finish(ins, buf, sems):
        x, y, c, chips = _mesh_pos()
        for g in range(n):
            for j in range(3):
                copy(buf, sems, g, j, _chip_of(chips[j])).wait_recv()
        for g in range(n):
            for j in range(3):
                copy(buf, sems, g, j, 2 * x + y).wait_send()

    return _Payload(bufs, _same(bufs), _in_place(n), _sems(3 * n), start, finish)


def _gather_d2d(bufs):
    n = len(bufs)

    def copy(buf, sems, g, j, half):
        x, y, c, chips = _mesh_pos()
        slab = buf[g].at[_chip_of(chips[j]), :, half]
        return pltpu.make_async_remote_copy(
            src_ref=slab, dst_ref=slab, send_sem=sems[0].at[3 * g + j], recv_sem=sems[1].at[3 * g + j],
            device_id=(x, y, 1 - c), device_id_type=MESH)

    def start(ins, buf, sems):
        c = lax.axis_index("c")
        for g in range(n):
            for j in range(3):
                copy(buf, sems, g, j, c).start()

    def finish(ins, buf, sems):
        c = lax.axis_index("c")
        for g in range(n):
            for j in range(3):
                copy(buf, sems, g, j, 1 - c).wait_recv()
        for g in range(n):
            for j in range(3):
                copy(buf, sems, g, j, c).wait_send()

    return _Payload(bufs, _same(bufs), _in_place(n), _sems(3 * n), start, finish)


def _pair_exchange(grads):
    n = len(grads)

    def copy(src, dst, sems, g):
        x, y, c, _ = _mesh_pos()
        return pltpu.make_async_remote_copy(
            src_ref=src[g].at[:, :, 1 - c], dst_ref=dst[g], send_sem=sems[0].at[g], recv_sem=sems[1].at[g],
            device_id=(x, y, 1 - c), device_id_type=MESH)

    def start(src, dst, sems):
        for g in range(n):
            copy(src, dst, sems, g).start()

    def finish(src, dst, sems):
        for g in range(n):
            copy(src, dst, sems, g).wait()

    out_shape = [jax.ShapeDtypeStruct(g.shape[:2] + g.shape[3:], g.dtype) for g in grads]
    return _Payload(grads, out_shape, {}, _sems(n), start, finish)


def _chip_exchange(parts):
    n = len(parts)

    def copy(src, dst, sems, g, j, slot):
        x, y, c, chips = _mesh_pos()
        return pltpu.make_async_remote_copy(
            src_ref=src[g].at[_chip_of(chips[j])], dst_ref=dst[g].at[slot], send_sem=sems[0].at[3 * g + j],
            recv_sem=sems[1].at[3 * g + j], device_id=(*chips[j], c), device_id_type=MESH)

    def start(src, dst, sems):
        x, y, c, chips = _mesh_pos()
        for g in range(n):
            for j in range(3):
                copy(src, dst, sems, g, j, 2 * x + y).start()

    def finish(src, dst, sems):
        x, y, c, chips = _mesh_pos()
        for g in range(n):
            for j in range(3):
                copy(src, dst, sems, g, j, _chip_of(chips[j])).wait_recv()
        for g in range(n):
            for j in range(3):
                copy(src, dst, sems, g, j, 2 * x + y).wait_send()

    return _Payload(parts, _same(parts), {}, _sems(3 * n), start, finish)


def _pair_share(bufs):
    n = len(bufs)

    def copy(buf, sems, g, half):
        x, y, c, _ = _mesh_pos()
        slab = buf[g].at[:, :, half]
        return pltpu.make_async_remote_copy(
            src_ref=slab, dst_ref=slab, send_sem=sems[0].at[g], recv_sem=sems[1].at[g],
            device_id=(x, y, 1 - c), device_id_type=MESH)

    def start(ins, buf, sems):
        c = lax.axis_index("c")
        for g in range(n):
            copy(buf, sems, g, c).start()

    def finish(ins, buf, sems):
        c = lax.axis_index("c")
        for g in range(n):
            copy(buf, sems, g, 1 - c).wait_recv()
        for g in range(n):
            copy(buf, sems, g, c).wait_send()

    return _Payload(bufs, _same(bufs), _in_place(n), _sems(n), start, finish)


def _join(a, b):
    if a is None or b is None:
        return a or b
    na, ma = len(a.operands), len(a.out_shape)
    aliases = dict(a.aliases)
    aliases.update({na + i: ma + o for i, o in b.aliases.items()})
    ka = len(a.scratch)

    def start(ins, outs, sems):
        a.start(ins[:na], outs[:ma], sems[:ka])
        b.start(ins[na:], outs[ma:], sems[ka:])

    def finish(ins, outs, sems):
        a.finish(ins[:na], outs[:ma], sems[:ka])
        b.finish(ins[na:], outs[ma:], sems[ka:])

    joined = _Payload(a.operands + b.operands, a.out_shape + b.out_shape, aliases, list(a.scratch) + list(b.scratch),
                      start, finish)
    joined.parts = (a, b, ma)
    return joined


def _small_exchange(vec):
    def copy(src, dst, sems, k, slot):
        x, y, c, _ = _mesh_pos()
        peer = (x ^ (k >> 2), y ^ ((k >> 1) & 1), c ^ (k & 1))
        return pltpu.make_async_remote_copy(
            src_ref=src[0], dst_ref=dst[0].at[slot], send_sem=sems[0].at[k - 1], recv_sem=sems[1].at[k - 1],
            device_id=peer, device_id_type=MESH)

    def me():
        x, y, c, _ = _mesh_pos()
        return 4 * x + 2 * y + c

    def start(src, dst, sems):
        for k in range(1, 8):
            copy(src, dst, sems, k, me()).start()

    def finish(src, dst, sems):
        for k in range(1, 8):
            copy(src, dst, sems, k, me() ^ k).wait_recv()
        for k in range(1, 8):
            copy(src, dst, sems, k, me()).wait_send()

    return _Payload([vec], [jax.ShapeDtypeStruct((8,) + vec.shape, vec.dtype)], {}, _sems(7), start, finish)


def _small_sum(vec, landed, pos, name):
    r = vec.shape[0]

    def body(pos_ref, v_ref, l_ref, o_ref):
        k = pl.program_id(0)

        @pl.when(k == 0)
        def _():
            o_ref[...] = jnp.zeros_like(o_ref)

        @pl.when(k == pos_ref[POS_DEVICE])
        def _():
            o_ref[...] += v_ref[...]

        @pl.when(k != pos_ref[POS_DEVICE])
        def _():
            o_ref[...] += l_ref[...]

    def landed_index(k, pos_ref):
        me = pos_ref[POS_DEVICE]
        return jnp.where(k == me, (me + 1) % 8, k), 0, 0

    return pl.pallas_call(
        body, name=name,
        grid_spec=pltpu.PrefetchScalarGridSpec(
            num_scalar_prefetch=1, grid=(8,),
            in_specs=[pl.BlockSpec((r, 128), lambda k, pos_ref: (0, 0)), pl.BlockSpec((None, r, 128), landed_index)],
            out_specs=pl.BlockSpec((r, 128), lambda k, pos_ref: (0, 0))),
        out_shape=jax.ShapeDtypeStruct(vec.shape, F32),
        compiler_params=_params(1))(pos, vec, landed)


def _all_reduce_small(vec, name):
    r = vec.shape[0]

    def body(v_ref, o_ref, gath_ref, send_sem, recv_sem):
        x, y, c, _ = _mesh_pos()
        me = 4 * x + 2 * y + c
        gath_ref[me] = v_ref[...]
        copies = []
        for k in range(1, 8):
            peer = (x ^ (k >> 2), y ^ ((k >> 1) & 1), c ^ (k & 1))
            cp = pltpu.make_async_remote_copy(
                src_ref=v_ref, dst_ref=gath_ref.at[me], send_sem=send_sem.at[k - 1], recv_sem=recv_sem.at[k - 1],
                device_id=peer, device_id_type=MESH)
            cp.start()
            copies.append(cp)
        for k in range(1, 8):
            src_id = me ^ k
            pltpu.make_async_remote_copy(
                src_ref=v_ref, dst_ref=gath_ref.at[src_id], send_sem=send_sem.at[k - 1], recv_sem=recv_sem.at[k - 1],
                device_id=(x, y, c), device_id_type=MESH).wait_recv()
        for cp in copies:
            cp.wait_send()
        acc = gath_ref[0]
        for k in range(1, 8):
            acc = acc + gath_ref[k]
        o_ref[...] = acc

    return pl.pallas_call(
        body, name=name,
        in_specs=[pl.BlockSpec(memory_space=pltpu.VMEM)], out_specs=pl.BlockSpec(memory_space=pltpu.VMEM),
        out_shape=jax.ShapeDtypeStruct(vec.shape, F32),
        scratch_shapes=[pltpu.VMEM((8, r, 128), F32), pltpu.SemaphoreType.DMA((7,)), pltpu.SemaphoreType.DMA((7,))],
        compiler_params=pltpu.CompilerParams(has_side_effects=True, vmem_limit_bytes=VMEM_LIMIT))(vec)


def _row_block(rows, cols, mult=16):
    best = None
    for cand in range(mult, rows + 1, mult):
        if rows % cand == 0 and cand * cols * 4 <= EW_BLOCK_BYTES:
            best = cand
    return best or rows


POS_ME, POS_CORE, POS_DEVICE = 0, 4, 5


def _place(arrs, li, pos, dtype, name):
    s = len(arrs)
    _, rows, cols = arrs[0].shape
    rh = rows // 2
    tr = _row_block(rh, cols)
    nb = rh // tr

    def body(pos_ref, *refs):
        o_ref = refs[s]
        for j in range(s):
            @pl.when(pl.program_id(0) == j)
            def _(j=j):
                o_ref[...] = refs[j][...].astype(dtype)

    def in_spec(j):
        return pl.BlockSpec((None, tr, cols), lambda b, hf, i, pos_ref: (li, jnp.where(b == j, hf * nb + i, 0), 0))

    return pl.pallas_call(
        body, name=name,
        grid_spec=pltpu.PrefetchScalarGridSpec(
            num_scalar_prefetch=1, grid=(s, 2, nb), in_specs=[in_spec(j) for j in range(s)],
            out_specs=pl.BlockSpec((None, None, None, tr, cols),
                                   lambda b, hf, i, pos_ref: (pos_ref[POS_ME], b, hf, i, 0))),
        out_shape=jax.ShapeDtypeStruct((N_CHIPS, s, 2, rh, cols), dtype),
        compiler_params=_params(3))(pos, *arrs)


def _pair_sum(grad, recv, pos, out_dtype, name):
    _, s, rh, cols = recv.shape
    tr = _row_block(rh, cols)

    def body(pos_ref, g_ref, r_ref, o_ref):
        o_ref[...] = (g_ref[...] + r_ref[...]).astype(out_dtype)

    blk = (None, None, tr, cols)
    return pl.pallas_call(
        body, name=name,
        grid_spec=pltpu.PrefetchScalarGridSpec(
            num_scalar_prefetch=1, grid=(N_CHIPS, s, rh // tr),
            in_specs=[pl.BlockSpec((None, None, None, tr, cols),
                                   lambda a, b, i, pos_ref: (a, b, pos_ref[POS_CORE], i, 0)),
                      pl.BlockSpec(blk, lambda a, b, i, pos_ref: (a, b, i, 0))],
            out_specs=pl.BlockSpec(blk, lambda a, b, i, pos_ref: (a, b, i, 0))),
        out_shape=jax.ShapeDtypeStruct(recv.shape, out_dtype),
        compiler_params=_params(3))(pos, grad, recv)


def _chip_sum(part, landed, gbuf, li, n_layers, pos, name):
    _, s, rh, cols = part.shape
    tr = _row_block(rh, cols)

    def body(pos_ref, p_ref, a_ref, b_ref, c_ref, *rest):
        o_ref = rest[-1]
        o_ref[...] = ((p_ref[...].astype(F32) + a_ref[...].astype(F32)) + b_ref[...].astype(F32)) \
            + c_ref[...].astype(F32)

    def slot(k):
        return pl.BlockSpec((None, None, tr, cols), lambda b, i, pos_ref: (pos_ref[k], b, i, 0))

    in_specs = [slot(0), slot(1), slot(2), slot(3)]
    operands = [pos, part, landed, landed, landed]
    aliases = {}
    if gbuf is not None:
        in_specs.append(ANY)
        operands.append(gbuf)
        aliases = {len(operands) - 1: 0}
    return pl.pallas_call(
        body, name=name,
        grid_spec=pltpu.PrefetchScalarGridSpec(
            num_scalar_prefetch=1, grid=(s, rh // tr), in_specs=in_specs,
            out_specs=pl.BlockSpec((None, None, None, tr, cols),
                                   lambda b, i, pos_ref: (li, b, pos_ref[POS_CORE], i, 0))),
        out_shape=jax.ShapeDtypeStruct((n_layers, s, 2, rh, cols), F32),
        input_output_aliases=aliases,
        compiler_params=_params(2))(*operands)


def _adamw_math(w, g, m, v):
    m = ADAM_B1 * m + (1.0 - ADAM_B1) * g
    v = ADAM_B2 * v + (1.0 - ADAM_B2) * (g * g)
    m_hat = m / (1.0 - ADAM_B1 ** ADAM_STEP)
    v_hat = v / (1.0 - ADAM_B2 ** ADAM_STEP)
    delta = -ADAM_LR * (m_hat / (jnp.sqrt(v_hat) + ADAM_EPS) + ADAM_WD * w)
    return delta, m, v


def _adamw(w, g, slot, m, v, name):
    l, rows, cols = w.shape
    tr = _row_block(rows, cols, 8)

    def body(w_ref, g_ref, m_ref, v_ref, go_ref, d_ref, mo_ref, vo_ref):
        g_ = g_ref[...]
        delta, m_, v_ = _adamw_math(w_ref[...], g_, m_ref[...], v_ref[...])
        go_ref[...] = g_
        d_ref[...] = delta
        mo_ref[...] = m_
        vo_ref[...] = v_

    blk = pl.BlockSpec((None, tr, cols), lambda a, i: (a, i, 0))
    gblk = pl.BlockSpec((None, None, tr, cols), lambda a, i: (a, slot, i, 0))
    return pl.pallas_call(
        body, name=name, grid=(l, rows // tr), in_specs=[blk, gblk, blk, blk], out_specs=[blk] * 4,
        out_shape=[jax.ShapeDtypeStruct(w.shape, F32)] * 4,
        compiler_params=_params(2))(w, g, m, v)


SQ = ("w_sgu_out", "w_conv_out", "w_pool_out", "w_out", "w_ple_gate")
SMALL = ("g_mix_pre", "w_sgu_s", "b_sgu_s", "g_sgu_v", "b_sgu_v", "b_dw", "g_conv_ln", "b_conv_ln", "s_pool",
         "g_mix_post", "g_ffn_pre", "g_ffn_post")


WHERE = {"w_in": ("in", 0), "w_ffn_in": ("ffn_in", 0), "w_ffn_out": ("ffn_out", 0), "w_ple": ("mix", 0),
         "w_pool": ("mix", 1), "w_dw": ("dw", 0)}
WHERE.update({nm: ("sq", slot) for slot, nm in enumerate(SQ)})


class _LayerWeights:
    def __init__(self, fetch, small, li):
        self.fetch, self.small, self.li, self.cache = fetch, small, li, {}

    def __getitem__(self, nm):
        if nm not in self.cache:
            self.cache[nm] = self._big(nm) if nm in WHERE else self.small[nm][self.li]
        return self.cache[nm]

    def _big(self, nm):
        group, slot = WHERE[nm]
        g = self.fetch(group)
        g = g.reshape(g.shape[:2] + (-1, g.shape[-1]))
        if nm in ("w_in", "w_ffn_in"):
            return g.reshape(N_CHIPS, D_MODEL, -1)
        if nm == "w_ffn_out":
            return g.reshape(D_FF, D_MODEL)
        if nm in SQ:
            return g[:, slot].reshape(D_MODEL, D_MODEL)
        if nm == "w_ple":
            return g[:, slot].transpose(1, 0, 2).reshape(256, D_MODEL)
        if nm == "w_pool":
            return g[:, slot].reshape(N_CHIPS, 4, 64, 256).transpose(1, 0, 2, 3).reshape(4, 256, 256)
        return g.reshape(N_CHIPS, CONV_HALO, -1).transpose(1, 0, 2).reshape(CONV_HALO, D_MODEL)


def _vec(a):
    return a.reshape(1, -1)


def _layer_fwd(h, p, w, li, hosts=None):
    s = {}
    tag = "_l%d" % li
    s["h0"] = h
    proj, hn = _norm_mm(h, _vec(w["g_mix_pre"]), w["w_in"], "mix_in" + tag, _take(hosts, "mix_in"))
    s["proj"], s["hn"] = proj, hn
    bs3 = w["b_sgu_s"].reshape(SGU_HEADS, SGU_BLOCK, 1)
    s["sg"] = _sgu_fwd(proj, w["w_sgu_s"], bs3, _vec(w["g_sgu_v"]), _vec(w["b_sgu_v"]), "sgu_fwd" + tag,
                       _take(hosts, "sgu_fwd"))
    s["cs"], s["cv"] = _conv_fwd(proj, w["w_dw"], _vec(w["b_dw"]), _vec(w["g_conv_ln"]), _vec(w["b_conv_ln"]),
                                 "conv_fwd" + tag, _take(hosts, "conv_fwd"))
    s["ps"] = _pool_fwd(proj, w["w_pool"], _vec(w["s_pool"]), "pool_fwd" + tag, _take(hosts, "pool_fwd"))
    s["bra"], s["brb"], s["brc"], s["merged"] = _merge_fwd(
        proj, s["sg"], s["cs"], s["ps"], w["w_sgu_out"], w["w_conv_out"], w["w_pool_out"], "merge_fwd" + tag,
        _take(hosts, "merge_fwd"))
    s["mo"], h1 = _mm_norm_res(s["merged"], w["w_out"], _vec(w["g_mix_post"]), h, "mix_out" + tag,
                               _take(hosts, "mix_out"))
    s["h1"] = h1
    s["fg"], s["fu"], s["act"], s["hn2"] = _ffn_in(h1, _vec(w["g_ffn_pre"]), w["w_ffn_in"], "ffn_in" + tag,
                                                   _take(hosts, "ffn_in"))
    s["f"], h2 = _mm_norm_res(s["act"], w["w_ffn_out"], _vec(w["g_ffn_post"]), h1, "ffn_out" + tag,
                              _take(hosts, "ffn_out"))
    s["h2"] = h2
    h3, s["q"], s["e"] = _ple_fwd(h2, p, w["w_ple_gate"], w["w_ple"], "ple_fwd" + tag, _take(hosts, "ple_fwd"))
    return h3, s


def _layer_bwd(dh3, p, w, s, li, hosts=None, big=None, gs=None):
    tag = "_l%d" % li
    d = D_MODEL
    gs = {} if gs is None else gs
    big = {} if big is None else big
    dh2, sq, dw_ple = _ple_bwd(dh3, s["q"], s["e"], w["w_ple_gate"], s["h2"], p, SQ.index("w_ple_gate"), len(SQ),
                               "ple_bwd" + tag)
    dff, gs["g_ffn_post"], dw_ffn_out = _ffn_out_bwd(
        dh2, s["f"], _vec(w["g_ffn_post"]), s["fg"], s["fu"], s["act"], w["w_ffn_out"], "ffn_out_bwd" + tag,
        _take(hosts, "ffn_out_bwd"))
    big["ffn_out"] = dw_ffn_out.reshape(N_CHIPS, 1, D_FF // N_CHIPS, d)
    n_ff = w["w_ffn_in"].shape[2]
    dh1, gs["g_ffn_pre"] = _in_bwd([(dff, 2 * D_FF // n_ff)], w["w_ffn_in"], n_ff, s["h1"], _vec(w["g_ffn_pre"]),
                                   dh2, ROW_TILE, "ffn_in_bwd" + tag, _take(hosts, "ffn_in_bwd"))
    big["ffn_in"] = _dw_cols(s["hn2"], [(dff, 2 * D_FF // n_ff)], n_ff, 1, "dw_ffn_in" + tag)
    branch_w = ("w_sgu_out", "w_conv_out", "w_pool_out")
    dzg, dsg, dcs, dps, gs["g_mix_post"], big["sq"] = _merge_bwd(
        dh1, s["mo"], _vec(w["g_mix_post"]), s["proj"], (s["bra"], s["brb"], s["brc"]), (s["sg"], s["cs"], s["ps"]),
        s["merged"], w["w_out"], [w[nm] for nm in branch_w], sq, [SQ.index(nm) for nm in ("w_out",) + branch_w],
        "merge_bwd" + tag, _take(hosts, "merge_bwd"))
    bs3 = w["b_sgu_s"].reshape(SGU_HEADS, SGU_BLOCK, 1)
    dz_sgu, gs["w_sgu_s"], dbs3, gs["g_sgu_v"], gs["b_sgu_v"] = _sgu_bwd(
        s["proj"], dsg, w["w_sgu_s"], bs3, _vec(w["g_sgu_v"]), _vec(w["b_sgu_v"]), "sgu_bwd" + tag,
        _take(hosts, "sgu_bwd"))
    gs["b_sgu_s"] = dbs3
    dcv, dwdw, gs["b_dw"], gs["g_conv_ln"], gs["b_conv_ln"] = _conv_bwd_norm(
        s["proj"], dcs, s["cv"], _vec(w["b_dw"]), _vec(w["g_conv_ln"]), _vec(w["b_conv_ln"]), "conv_bwd_norm" + tag,
        _take(hosts, "conv_bwd_norm"))
    dz_conv = _conv_bwd_taps(s["proj"], dcv, w["w_dw"], "conv_bwd_taps" + tag, _take(hosts, "conv_bwd_taps"))
    dz_pool, dwpool, gs["s_pool"] = _pool_bwd(s["proj"], dps, w["w_pool"], _vec(w["s_pool"]), "pool_bwd" + tag)
    pieces = [(dz_sgu, 2), (dz_conv, 2), (dz_pool, 1), (dzg, 3)]
    big["in"] = _dw_cols(s["hn"], pieces, d, 2, "dw_in" + tag)
    gple = dw_ple.reshape(256, N_CHIPS, 256).transpose(1, 0, 2)
    gpool = dwpool.reshape(4, N_CHIPS, 64, 256).transpose(1, 0, 2, 3).reshape(N_CHIPS, 256, 256)
    big["mix"] = jnp.stack([gple, gpool], axis=1)
    big["dw"] = dwdw.reshape(CONV_HALO, N_CHIPS, 256).transpose(1, 0, 2)[:, None]
    dh0, gs["g_mix_pre"] = _in_bwd(pieces, w["w_in"], d, s["h0"], _vec(w["g_mix_pre"]), dh1, ROW_TILE_HEAVY,
                                   "mix_in_bwd" + tag, _take(hosts, "mix_in_bwd"))
    return dh0, big, gs


GROUPS = ("in", "sq", "ffn_in", "ffn_out", "mix", "dw")
WIRE_DTYPE = {"in": BF16, "sq": BF16, "ffn_in": BF16, "ffn_out": BF16, "mix": BF16, "dw": F32}
GATHER_FIRST = ("in", "mix", "dw")
GATHER_RIDES = (("mix_in", "sgu_fwd", ("sq", "ffn_in"), ()),
                ("conv_fwd", "pool_fwd", ("ffn_out",), ("in",)),
                ("merge_fwd", "mix_out", (), ("sq",)),
                ("ffn_in", "ffn_out", (), ("ffn_in", "ffn_out", "mix", "dw")))
REDUCE_UPPER = ("ffn_out_bwd", (("ffn_in_bwd", ("in", "ffn_out")), ("merge_bwd", ("sq", "ffn_in", "mix", "dw"))))
REDUCE_OWN = ("sgu_bwd", (("conv_bwd_norm", ("ffn_in", "ffn_out")), ("conv_bwd_taps", ("sq",))))
REDUCE_LAST = ("in", "mix", "dw")


def _group_members(wts):
    n_layers = wts["w_in"].shape[0]
    dw = wts["w_dw"].reshape(n_layers, CONV_WIDTH, -1)
    return {"in": [wts["w_in"]], "sq": [wts[nm] for nm in SQ], "ffn_in": [wts["w_ffn_in"]],
            "ffn_out": [wts["w_ffn_out"]],
            "mix": [wts["w_ple"], wts["w_pool"].reshape(n_layers, POOL_GROUP, POOL_GROUP)],
            "dw": [jnp.pad(dw, ((0, 0), (0, CONV_HALO - CONV_WIDTH), (0, 0)))]}


class _Gather:
    PLACED, OVER_ICI, FULL = 0, 1, 2

    def __init__(self):
        self.buf, self.stage, self.pending = {}, {}, []

    def put(self, key, buf):
        self.buf[key], self.stage[key] = buf, self.PLACED

    def _flush(self):
        for keys, pay, stage in self.pending:
            if pay.results is not None:
                for key, res in zip(keys, pay.results):
                    self.buf[key], self.stage[key] = res, stage
        self.pending = [entry for entry in self.pending if entry[1].results is None]

    def _factory(self, make, keys, before, after):
        def factory():
            if not keys:
                return None
            self._flush()
            assert all(self.stage[k] == before for k in keys), (keys, self.stage)
            pay = make([self.buf[k] for k in keys])
            self.pending.append((keys, pay, after))
            return pay
        return factory

    def ici(self, keys):
        return self._factory(_gather_ici, keys, self.PLACED, self.OVER_ICI)

    def d2d(self, keys):
        return self._factory(_gather_d2d, keys, self.OVER_ICI, self.FULL)

    def get(self, li, group):
        self._flush()
        assert self.stage[(li, group)] == self.FULL, (li, group)
        return self.buf[(li, group)]


class _Reduce:
    def __init__(self, pos, n_layers):
        self.pos, self.n_layers, self.exchanged, self.stages = pos, n_layers, [], []

    def exchange(self, li, groups, grads):
        def factory():
            pay = _pair_exchange([_half_view(grads[g]) for g in groups])
            self.exchanged.append((li, list(groups), pay))
            return pay
        return factory

    def _received(self, li, group):
        for lj, groups, pay in self.exchanged:
            if lj == li and group in groups:
                return pay.results[groups.index(group)]
        raise KeyError((li, group))

    def chips(self, li, groups, grads):
        def factory():
            parts = [_pair_sum(_half_view(grads[g]), self._received(li, g), self.pos, WIRE_DTYPE[g],
                               "pair_sum_%s_l%d" % (g, li)) for g in groups]
            pay = _chip_exchange(parts)
            self.stages.append((li, groups, parts, pay))
            return pay
        return factory

    def finish(self):
        reduced = {}
        for li, groups, parts, pay in self.stages:
            for g, part, landed in zip(groups, parts, pay.results):
                reduced[g] = _chip_sum(part, landed, reduced.get(g), li, self.n_layers, self.pos,
                                       "chip_sum_%s_l%d" % (g, li))
        return reduced


def _pack_small(tree):
    flat = jnp.concatenate([tree[nm].reshape(-1).astype(F32) for nm in SMALL])
    return flat.reshape(-1, 128)


def _unpack_small(packed, like):
    out, off = {}, 0
    flat = packed.reshape(-1)
    for nm in SMALL:
        n = like[nm].size
        out[nm] = flat[off:off + n].reshape(like[nm].shape)
        off += n
    return out


WEIGHTS = ("g_mix_pre", "w_in", "w_sgu_s", "b_sgu_s", "g_sgu_v", "b_sgu_v", "w_sgu_out", "w_dw", "b_dw", "g_conv_ln",
           "b_conv_ln", "w_conv_out", "w_pool", "s_pool", "w_pool_out", "w_out", "g_mix_post", "g_ffn_pre",
           "w_ffn_in", "w_ffn_out", "g_ffn_post", "w_ple", "w_ple_gate")


def kernel(x, p, g_mix_pre, w_in, w_sgu_s, b_sgu_s, g_sgu_v, b_sgu_v, w_sgu_out, w_dw, b_dw, g_conv_ln, b_conv_ln, w_conv_out, w_pool, s_pool, w_pool_out, w_out, g_mix_post, g_ffn_pre, w_ffn_in, w_ffn_out, g_ffn_post, w_ple, w_ple_gate, loss_target, m_g_mix_pre, m_w_in, m_w_sgu_s, m_b_sgu_s, m_g_sgu_v, m_b_sgu_v, m_w_sgu_out, m_w_dw, m_b_dw, m_g_conv_ln, m_b_conv_ln, m_w_conv_out, m_w_pool, m_s_pool, m_w_pool_out, m_w_out, m_g_mix_post, m_g_ffn_pre, m_w_ffn_in, m_w_ffn_out, m_g_ffn_post, m_w_ple, m_w_ple_gate, v_g_mix_pre, v_w_in, v_w_sgu_s, v_b_sgu_s, v_g_sgu_v, v_b_sgu_v, v_w_sgu_out, v_w_dw, v_b_dw, v_g_conv_ln, v_b_conv_ln, v_w_conv_out, v_w_pool, v_s_pool, v_w_pool_out, v_w_out, v_g_mix_post, v_g_ffn_pre, v_w_ffn_in, v_w_ffn_out, v_g_ffn_post, v_w_ple, v_w_ple_gate):
    args = dict(locals())
    wts = {nm: args[nm] for nm in WEIGHTS}
    mom = {nm: args["m_" + nm] for nm in WEIGHTS}
    var = {nm: args["v_" + nm] for nm in WEIGHTS}
    n_layers = w_in.shape[0]
    h = x.reshape(x.shape[1:])
    target = loss_target.reshape(loss_target.shape[1:])
    cx, cy, core = lax.axis_index("x"), lax.axis_index("y"), lax.axis_index("c")
    pos = jnp.stack([2 * cx + cy, 2 * (1 - cx) + cy, 2 * cx + (1 - cy), 2 * (1 - cx) + (1 - cy), core,
                     4 * cx + 2 * cy + core])
    pos = pos.astype(jnp.int32)

    members = _group_members(wts)
    gather = _Gather()
    for li in range(n_layers):
        for g in GROUPS:
            gather.put((li, g), _place(members[g], li, pos, WIRE_DTYPE[g], "place_%s_l%d" % (g, li)))
    first = [(0, g) for g in GATHER_FIRST]
    _run_payload(gather.ici(first)(), "gather_ici_first")
    _run_payload(gather.d2d(first)(), "gather_d2d_first")

    saved, layer_w = [], []
    for li in range(n_layers):
        hosts = {}
        for ici_host, d2d_host, own, nxt in GATHER_RIDES:
            keys = [(li, g) for g in own if li == 0] + [(li + 1, g) for g in nxt if li + 1 < n_layers]
            hosts[ici_host], hosts[d2d_host] = gather.ici(keys), gather.d2d(keys)
        w = _LayerWeights(functools.partial(gather.get, li), wts, li)
        layer_w.append(w)
        h, s = _layer_fwd(h, p[li, 0], w, li, hosts)
        saved.append(s)
    dh, sq_err = _loss_head(h, target, "loss_head")
    loss = lax.psum(sq_err[0, 0] * (0.5 / D_MODEL), ("x", "y", "c"))

    reduce = _Reduce(pos, n_layers)
    small_grads = [{} for _ in range(n_layers)]
    late = (0, SMALL[0])
    small = {}

    def small_vec():
        def leaf(li, nm):
            shape = wts[nm].shape[1:]
            return jnp.zeros(shape, F32) if (li, nm) == late else small_grads[li][nm].reshape(shape)
        return _pack_small({nm: jnp.stack([leaf(li, nm) for li in range(n_layers)], axis=0) for nm in SMALL})

    upper = None
    for li in reversed(range(n_layers)):
        own = {}
        hosts = {}
        plans = [(REDUCE_UPPER, li + 1, upper)] if upper is not None else []
        if li == 0:
            plans.append((REDUCE_OWN, 0, own))

            def last_rides(own=own):
                _run_payload(reduce.exchange(0, REDUCE_LAST, own)(), "pair_exchange_last")
                small["vec"] = small_vec()
                small["exchange"] = _small_exchange(small["vec"])
                return _join(reduce.chips(0, REDUCE_LAST, own)(), small["exchange"])
            hosts["mix_in_bwd"] = last_rides
        for (pair_host, chip_hosts), lj, grads in plans:
            groups = [g for _, gs_ in chip_hosts for g in gs_]
            hosts[pair_host] = reduce.exchange(lj, groups, grads)
            for chip_host, gs_ in chip_hosts:
                hosts[chip_host] = reduce.chips(lj, gs_, grads)
        dh, upper, _ = _layer_bwd(dh, p[li, 0], layer_w[li], saved[li], li, hosts, own, small_grads[li])
    grad_x = dh[None]
    reduced = reduce.finish()

    shared = _run_payload(_pair_share([reduced[g] for g in GROUPS]), "pair_share")
    red = {g: b.reshape(b.shape[:2] + (-1, b.shape[-1])) for g, b in zip(GROUPS, shared)}

    where = {"w_in": ("in", 0), "w_ffn_in": ("ffn_in", 0), "w_ffn_out": ("ffn_out", 0), "w_ple": ("mix", 0),
             "w_pool": ("mix", 1)}
    for slot, nm in enumerate(SQ):
        where[nm] = ("sq", slot)
    outs = {}
    for nm, (g, slot) in where.items():
        shape = wts[nm].shape
        to3 = lambda a: a.reshape((n_layers,) + red[g].shape[2:])
        res = _adamw(to3(wts[nm]), red[g], slot, to3(mom[nm]), to3(var[nm]), "adamw_" + nm)
        outs[nm] = [r.reshape(shape) for r in res]
    gdw = red["dw"][:, :, :CONV_WIDTH]
    to3 = lambda a: a.reshape(n_layers, CONV_WIDTH, -1)
    res = _adamw(to3(wts["w_dw"]), gdw, 0, to3(mom["w_dw"]), to3(var["w_dw"]), "adamw_w_dw")
    outs["w_dw"] = [r.reshape(wts["w_dw"].shape) for r in res]

    gmain = _small_sum(small["vec"], small["exchange"].results[0], pos, "small_sum")
    glate = _all_reduce_small(small_grads[late[0]][late[1]].reshape(-1, 128), "all_reduce_late")
    gsmall = jnp.concatenate([glate, gmain[glate.shape[0]:]], axis=0)
    pk = lambda tree: _pack_small({nm: tree[nm] for nm in SMALL})[None]
    res = _adamw(pk(wts), gsmall[None, None], 0, pk(mom), pk(var), "adamw_small")
    unpacked = [_unpack_small(r[0], wts) for r in res]
    for nm in SMALL:
        outs[nm] = [u[nm] for u in unpacked]

    result = [loss, grad_x]
    for k in range(4):
        result += [outs[nm][k] for nm in WEIGHTS]
    return tuple(result)
```

```python
import functools

import jax
import jax.numpy as jnp
from jax import lax
from jax.experimental import pallas as pl
from jax.experimental.pallas import tpu as pltpu

F32 = jnp.float32
BF16 = jnp.bfloat16
MESH = pl.DeviceIdType.MESH

EPS = 1e-6
D_MODEL = 1024
SGU_BLOCK = 128
SGU_HEADS = 8
CHUNK = 64
CONV_WIDTH = 31
CONV_HALO = 32
POOL_WINDOWS = (2, 4, 8, 16)
POOL_BLOCK = 128
POOL_GROUP = 256
D_FF = 2816
N_CHIPS = 4

ADAM_LR = 0.001
ADAM_B1 = 0.9
ADAM_B2 = 0.999
ADAM_EPS = 1e-08
ADAM_WD = 0.01
ADAM_STEP = 10

VMEM_LIMIT = 52 * 1024 * 1024
ROW_TILE_LIGHT = 1024
ROW_TILE = 512
ROW_TILE_HEAVY = 256
CONV_ROWS = 64
CONV_LANES = 128
EW_BLOCK_BYTES = 2 * 1024 * 1024
TOKEN_TILE = 2048
FF_CHUNK = 256


def _params(n_grid):
    return pltpu.CompilerParams(dimension_semantics=("arbitrary",) * n_grid, vmem_limit_bytes=VMEM_LIMIT)


def _dot(a, b):
    return jnp.dot(a.astype(BF16), b.astype(BF16), preferred_element_type=F32)


def _dot_nt(a, b):
    return lax.dot_general(a.astype(BF16), b.astype(BF16), (((1,), (1,)), ((), ())), preferred_element_type=F32)


def _dot_tn(a, b):
    return lax.dot_general(a.astype(BF16), b.astype(BF16), (((0,), (0,)), ((), ())), preferred_element_type=F32)


def _sigmoid(x):
    return 0.5 * jnp.tanh(0.5 * x) + 0.5


_GELU_C = 0.7978845608028654
_GELU_A = 0.044715


def _gelu(x):
    t = jnp.tanh(_GELU_C * (x + _GELU_A * x * x * x))
    return 0.5 * x * (1.0 + t)


def _gelu_and_grad(x):
    x2 = x * x
    t = jnp.tanh(_GELU_C * (x + _GELU_A * x2 * x))
    g = 0.5 * (1.0 + t) + 0.5 * x * (1.0 - t * t) * (_GELU_C * (1.0 + 3.0 * _GELU_A * x2))
    return 0.5 * x * (1.0 + t), g


def _rms_stats(x):
    r = lax.rsqrt(jnp.mean(x * x, axis=-1, keepdims=True) + EPS)
    return x * r, r


def _rms_bwd(xn, r, g, dy):
    gd = dy * g
    return r * (gd - xn * jnp.mean(gd * xn, axis=-1, keepdims=True)), dy * xn


def _ln_stats(x):
    mu = jnp.mean(x, axis=-1, keepdims=True)
    xc = x - mu
    rstd = lax.rsqrt(jnp.mean(xc * xc, axis=-1, keepdims=True) + EPS)
    return xc * rstd, rstd


def _ln_bwd(xhat, rstd, g, dy):
    dxh = dy * g
    return rstd * (dxh - jnp.mean(dxh, axis=-1, keepdims=True) - xhat * jnp.mean(dxh * xhat, axis=-1, keepdims=True))


def _rowsum(x):
    return jnp.sum(x, axis=0, keepdims=True)


def _tile(t, want):
    return min(t, want)


def _full(shape):
    n = len(shape)
    return pl.BlockSpec(shape, lambda *_: (0,) * n)


def _resident(shape):
    n = len(shape)
    return pl.BlockSpec(shape, lambda *_: (0,) * n, pipeline_mode=pl.Buffered(1))


class _Payload:
    def __init__(self, operands, out_shape, aliases, scratch, start, finish):
        self.operands, self.out_shape, self.aliases, self.scratch = list(operands), list(out_shape), aliases, scratch
        self.start, self.finish = start, finish
        self.results = None
        self.parts = None

    def deliver(self, results):
        self.results = list(results)
        if self.parts:
            a, b, ma = self.parts
            a.deliver(self.results[:ma])
            b.deliver(self.results[ma:])


def _pcall(body, *, name, grid, in_specs, out_specs, out_shape, operands, scratch_shapes=(), comm=None, aliases=None):
    single = not isinstance(out_shape, (list, tuple))
    out_specs = [out_specs] if single else list(out_specs)
    out_shape = [out_shape] if single else list(out_shape)
    aliases = dict(aliases or {})
    if comm is None:
        res = pl.pallas_call(
            body, name=name, grid=grid, in_specs=list(in_specs), out_specs=out_specs, out_shape=out_shape,
            scratch_shapes=list(scratch_shapes), input_output_aliases=aliases,
            compiler_params=_params(len(grid)))(*operands)
        return res[0] if single else res
    n_in, n_out, n_scr = len(in_specs), len(out_shape), len(scratch_shapes)
    ci, co = len(comm.operands), len(comm.out_shape)

    def hosted(*refs):
        bounds = [0, n_in, n_in + ci, n_in + ci + n_out, n_in + ci + n_out + co, n_in + ci + n_out + co + n_scr]
        a, b, c_, d_, s_ = [refs[lo:hi] for lo, hi in zip(bounds[:-1], bounds[1:])]
        t_ = refs[bounds[-1]:]
        ids = [pl.program_id(q) for q in range(len(grid))]
        first = functools.reduce(jnp.logical_and, [i == 0 for i in ids])
        last = functools.reduce(jnp.logical_and, [i == pl.num_programs(q) - 1 for q, i in enumerate(ids)])

        @pl.when(first)
        def _():
            comm.start(b, d_, t_)

        body(*a, *c_, *s_)

        @pl.when(last)
        def _():
            comm.finish(b, d_, t_)

    res = pl.pallas_call(
        hosted, name=name, grid=grid, in_specs=list(in_specs) + [ANY] * ci, out_specs=out_specs + [ANY] * co,
        out_shape=out_shape + comm.out_shape, scratch_shapes=list(scratch_shapes) + list(comm.scratch),
        input_output_aliases={**aliases, **{n_in + i: n_out + o for i, o in comm.aliases.items()}},
        compiler_params=pltpu.CompilerParams(dimension_semantics=("arbitrary",) * len(grid),
                                             vmem_limit_bytes=VMEM_LIMIT, has_side_effects=True),
    )(*operands, *comm.operands)
    comm.deliver(res[n_out:])
    res = res[:n_out]
    return res[0] if single else res


def _run_payload(comm, name):
    ci, co = len(comm.operands), len(comm.out_shape)

    def body(*refs):
        b, d_, t_ = refs[:ci], refs[ci:ci + co], refs[ci + co:]
        comm.start(b, d_, t_)
        comm.finish(b, d_, t_)

    res = pl.pallas_call(
        body, name=name, in_specs=[ANY] * ci, out_specs=[ANY] * co, out_shape=comm.out_shape,
        input_output_aliases=dict(comm.aliases), scratch_shapes=list(comm.scratch),
        compiler_params=pltpu.CompilerParams(has_side_effects=True))(*comm.operands)
    comm.deliver(res)
    return comm.results


def _take(hosts, key):
    return hosts[key]() if hosts and key in hosts else None


def _norm_mm(h, g, w4, name, comm=None):
    t, d = h.shape
    n = w4.shape[2]
    tm = _tile(t, ROW_TILE)

    step = _lane_block(n, 1024)

    def body(h_ref, g_ref, w_ref, o_ref, hn_ref):
        xn, _ = _rms_stats(h_ref[...])
        hn = (xn * g_ref[...]).astype(BF16)
        hn_ref[...] = hn
        for j in range(N_CHIPS):
            for c0 in range(0, n, step):
                o_ref[:, j * n + c0:j * n + c0 + step] = jnp.dot(
                    hn, w_ref[j, :, c0:c0 + step], preferred_element_type=F32).astype(BF16)

    return _pcall(
        body, name=name, grid=(t // tm,),
        in_specs=[pl.BlockSpec((tm, d), lambda i: (i, 0)), _full((1, d)), _resident(w4.shape)],
        out_specs=[pl.BlockSpec((tm, N_CHIPS * n), lambda i: (i, 0)), pl.BlockSpec((tm, d), lambda i: (i, 0))],
        out_shape=[jax.ShapeDtypeStruct((t, N_CHIPS * n), BF16), jax.ShapeDtypeStruct((t, d), BF16)],
        operands=(h, g, w4), comm=comm)


def _sgu_mask():
    ii = lax.broadcasted_iota(jnp.int32, (SGU_BLOCK, SGU_BLOCK), 0) // CHUNK
    jj = lax.broadcasted_iota(jnp.int32, (SGU_BLOCK, SGU_BLOCK), 1) // CHUNK
    return jj <= ii


def _sgu_fwd(proj, wm, bs3, gv, bv, name, comm=None):
    t = proj.shape[0]
    d = D_MODEL
    tm = _tile(t, ROW_TILE)
    hd = d // SGU_HEADS

    def body(zu_ref, zv_ref, wm_ref, bs_ref, gv_ref, bv_ref, o_ref):
        mask = _sgu_mask()
        for blk in range(tm // SGU_BLOCK):
            rows = pl.ds(blk * SGU_BLOCK, SGU_BLOCK)
            u = _gelu(zu_ref[rows, :].astype(F32))
            xhat, _ = _ln_stats(_gelu(zv_ref[rows, :].astype(F32)))
            vn = (xhat * gv_ref[...] + bv_ref[...]).astype(BF16)
            for hh in range(SGU_HEADS):
                cols = slice(hh * hd, (hh + 1) * hd)
                wmh = jnp.where(mask, wm_ref[hh], 0.0).astype(BF16)
                mixed = jnp.dot(wmh, vn[:, cols], preferred_element_type=F32) + bs_ref[hh]
                o_ref[rows, cols] = (u[:, cols] * mixed).astype(BF16)

    return _pcall(
        body, name=name, grid=(t // tm,),
        in_specs=[pl.BlockSpec((tm, d), lambda i: (i, 0)), pl.BlockSpec((tm, d), lambda i: (i, 1)),
                  _full(wm.shape), _full(bs3.shape), _full(gv.shape), _full(bv.shape)],
        out_specs=pl.BlockSpec((tm, d), lambda i: (i, 0)),
        out_shape=jax.ShapeDtypeStruct((t, d), BF16),
        operands=(proj, proj, wm, bs3, gv, bv), comm=comm)


def _conv_taps(scr_ref, r0, c0, base, weight):
    n = CONV_ROWS + CONV_HALO
    win = scr_ref[pl.ds(r0, n), pl.ds(c0, CONV_LANES)]
    acc = None
    for r in range(8):
        rolled = win if r == 0 else pltpu.roll(win, n - r, 0)
        for q in range((CONV_HALO + 7) // 8 + 1):
            k = 8 * q + r - base
            if 0 <= k < CONV_WIDTH and 8 * q + CONV_ROWS <= n:
                term = weight(k) * rolled[8 * q:8 * q + CONV_ROWS]
                acc = term if acc is None else acc + term
    return acc


def _glu_rows(a_ref, g_ref):
    return a_ref[...].astype(F32) * _sigmoid(g_ref[...].astype(F32))


def _conv_into(scr_ref, cv_ref, w_ref, tm, base, flip):
    def chunk(ci, carry):
        r0 = pl.multiple_of(ci * CONV_ROWS, CONV_ROWS)
        for c0 in range(0, D_MODEL, CONV_LANES):
            def weight(k, c0=c0):
                kk = CONV_WIDTH - 1 - k if flip else k
                return w_ref[kk:kk + 1, c0:c0 + CONV_LANES]
            cv_ref[pl.ds(r0, CONV_ROWS), pl.ds(c0, CONV_LANES)] = _conv_taps(scr_ref, r0, c0, base, weight)
        return carry

    lax.fori_loop(0, tm // CONV_ROWS, chunk, 0)


def _conv_specs(t, tm, d):
    hb = tm // CONV_HALO
    main = [pl.BlockSpec((tm, d), lambda i: (i, 2)), pl.BlockSpec((tm, d), lambda i: (i, 3))]
    halo = [pl.BlockSpec((CONV_HALO, d), lambda i: (jnp.maximum(i * hb - 1, 0), 2)),
            pl.BlockSpec((CONV_HALO, d), lambda i: (jnp.maximum(i * hb - 1, 0), 3))]
    return main, halo


def _fill_glu_history(scr_ref, a_ref, g_ref, ah_ref, gh_ref, tm):
    hist = _glu_rows(ah_ref, gh_ref)
    scr_ref[0:CONV_HALO, :] = jnp.where(pl.program_id(0) > 0, hist, 0.0)
    scr_ref[CONV_HALO:CONV_HALO + tm, :] = _glu_rows(a_ref, g_ref)


_CONV_BASE = CONV_HALO - (CONV_WIDTH - 1)


def _conv_fwd(proj, wdw, bdw, gln, bln, name, comm=None):
    t = proj.shape[0]
    d = D_MODEL
    tm = _tile(t, ROW_TILE)
    main, halo = _conv_specs(t, tm, d)

    def body(a_ref, g_ref, ah_ref, gh_ref, w_ref, b_ref, gl_ref, bl_ref, o_ref, cv_ref, scr_ref):
        _fill_glu_history(scr_ref, a_ref, g_ref, ah_ref, gh_ref, tm)
        _conv_into(scr_ref, cv_ref, w_ref, tm, _CONV_BASE, False)
        xhat, _ = _ln_stats(cv_ref[...] + b_ref[...])
        cn = xhat * gl_ref[...] + bl_ref[...]
        o_ref[...] = (cn * _sigmoid(cn)).astype(BF16)

    row = pl.BlockSpec((tm, d), lambda i: (i, 0))
    return _pcall(
        body, name=name, grid=(t // tm,),
        in_specs=main + halo + [_full(wdw.shape), _full(bdw.shape), _full(gln.shape), _full(bln.shape)],
        out_specs=[row, row],
        out_shape=[jax.ShapeDtypeStruct((t, d), BF16), jax.ShapeDtypeStruct((t, d), F32)],
        scratch_shapes=[pltpu.VMEM((tm + CONV_HALO, d), F32)],
        operands=(proj, proj, proj, proj, wdw, bdw, gln, bln), comm=comm)


def _pool_fill(scr_ref, z_ref, zh_ref, tm):
    scr_ref[0:POOL_BLOCK, :] = jnp.where(pl.program_id(0) > 0, zh_ref[...], jnp.zeros_like(zh_ref))
    scr_ref[POOL_BLOCK:POOL_BLOCK + tm, :] = z_ref[...]


def _pool_count(t0, rows, w):
    pos = (t0 + lax.broadcasted_iota(jnp.int32, (rows, 1), 0) + 1).astype(F32)
    return jnp.minimum(pos, float(w))


def _band(w, leading):
    i = lax.broadcasted_iota(jnp.int32, (POOL_BLOCK, 2 * POOL_BLOCK), 0)
    j = lax.broadcasted_iota(jnp.int32, (POOL_BLOCK, 2 * POOL_BLOCK), 1)
    off = j - i if leading else POOL_BLOCK + i - j
    return jnp.where((off >= 0) & (off < w), 1.0, 0.0).astype(BF16)


def _window_sums(refs, band, tm, cols):
    blocks = []
    for b in range(tm // POOL_BLOCK):
        rows = pl.ds(b * POOL_BLOCK, 2 * POOL_BLOCK)
        acc = None
        for ref in refs:
            term = jnp.dot(band, ref[rows, cols], preferred_element_type=F32)
            acc = term if acc is None else acc + term
        blocks.append(acc)
    return blocks[0] if len(blocks) == 1 else jnp.concatenate(blocks, axis=0)


def _pooled_group(scr_ref, gi, w, tm, t0):
    cols = pl.ds(gi * POOL_GROUP, POOL_GROUP)
    sums = _window_sums([scr_ref], _band(w, False), tm, cols)
    return sums / _pool_count(t0, tm, w) - scr_ref[pl.ds(POOL_BLOCK, tm), cols].astype(F32)


def _pool_specs(tm, d):
    hb = tm // POOL_BLOCK
    return [pl.BlockSpec((tm, d), lambda i: (i, 4)),
            pl.BlockSpec((POOL_BLOCK, d), lambda i: (jnp.maximum(i * hb - 1, 0), 4))]


def _pool_fwd(proj, wpool, spool, name, comm=None):
    t = proj.shape[0]
    d = D_MODEL
    tm = _tile(t, ROW_TILE)

    def body(z_ref, zh_ref, w_ref, s_ref, o_ref, scr_ref):
        _pool_fill(scr_ref, z_ref, zh_ref, tm)
        t0 = pl.program_id(0) * tm
        for gi, w in enumerate(POOL_WINDOWS):
            cols = slice(gi * POOL_GROUP, (gi + 1) * POOL_GROUP)
            pooled = _pooled_group(scr_ref, gi, w, tm, t0)
            o_ref[:, cols] = (_dot(pooled, w_ref[gi]) * s_ref[:, cols]).astype(BF16)

    return _pcall(
        body, name=name, grid=(t // tm,),
        in_specs=_pool_specs(tm, d) + [_full(wpool.shape), _full(spool.shape)],
        out_specs=pl.BlockSpec((tm, d), lambda i: (i, 0)),
        out_shape=jax.ShapeDtypeStruct((t, d), BF16),
        scratch_shapes=[pltpu.VMEM((tm + POOL_BLOCK, d), BF16)],
        operands=(proj, proj, wpool, spool), comm=comm)


def _merge_fwd(proj, sg, cs, ps, wa, wb, wc, name, comm=None):
    t = proj.shape[0]
    d = D_MODEL
    tm = _tile(t, ROW_TILE_HEAVY)

    def body(za_ref, zb_ref, zc_ref, sg_ref, cs_ref, ps_ref, wa_ref, wb_ref, wc_ref, ba_ref, bb_ref, bc_ref, m_ref):
        merged = None
        for z_ref, x_ref, w_ref, b_ref in ((za_ref, sg_ref, wa_ref, ba_ref), (zb_ref, cs_ref, wb_ref, bb_ref),
                                           (zc_ref, ps_ref, wc_ref, bc_ref)):
            br = jnp.dot(x_ref[...], w_ref[...], preferred_element_type=F32)
            b_ref[...] = br.astype(BF16)
            term = _sigmoid(z_ref[...].astype(F32)) * br
            merged = term if merged is None else merged + term
        m_ref[...] = merged.astype(BF16)

    row = pl.BlockSpec((tm, d), lambda i: (i, 0))
    wspec = _resident((d, d))
    return _pcall(
        body, name=name, grid=(t // tm,),
        in_specs=[pl.BlockSpec((tm, d), lambda i: (i, 5)), pl.BlockSpec((tm, d), lambda i: (i, 6)),
                  pl.BlockSpec((tm, d), lambda i: (i, 7)), row, row, row, wspec, wspec, wspec],
        out_specs=[row, row, row, row],
        out_shape=[jax.ShapeDtypeStruct((t, d), BF16)] * 4,
        operands=(proj, proj, proj, sg, cs, ps, wa, wb, wc), comm=comm)


def _mm_norm_res(a, w, g, hres, name, comm=None):
    t, k = a.shape
    d = w.shape[1]
    tm = _tile(t, ROW_TILE_LIGHT)

    def body(a_ref, w_ref, g_ref, h_ref, y_ref, o_ref):
        y = jnp.dot(a_ref[...], w_ref[...], preferred_element_type=F32)
        y_ref[...] = y
        yn, _ = _rms_stats(y)
        o_ref[...] = h_ref[...] + yn * g_ref[...]

    row = pl.BlockSpec((tm, d), lambda i: (i, 0))
    return _pcall(
        body, name=name, grid=(t // tm,),
        in_specs=[pl.BlockSpec((tm, k), lambda i: (i, 0)), _resident(w.shape), _full(g.shape), row],
        out_specs=[row, row],
        out_shape=[jax.ShapeDtypeStruct((t, d), F32)] * 2,
        operands=(a, w, g, hres), comm=comm)


def _ffn_in(h, g, w4, name, comm=None):
    t, d = h.shape
    n = w4.shape[2]
    tm = _tile(t, ROW_TILE)
    nj = D_FF // n

    def body(h_ref, g_ref, w_ref, fg_ref, fu_ref, act_ref, hn_ref):
        xn, _ = _rms_stats(h_ref[...])
        hn = (xn * g_ref[...]).astype(BF16)
        hn_ref[...] = hn
        for j in range(nj):
            cols = slice(j * n, (j + 1) * n)
            fg = jnp.dot(hn, w_ref[j], preferred_element_type=F32)
            fu = jnp.dot(hn, w_ref[j + nj], preferred_element_type=F32)
            fg_ref[:, cols] = fg.astype(BF16)
            fu_ref[:, cols] = fu.astype(BF16)
            act_ref[:, cols] = (fg * _sigmoid(fg) * fu).astype(BF16)

    wide = pl.BlockSpec((tm, D_FF), lambda i: (i, 0))
    return _pcall(
        body, name=name, grid=(t // tm,),
        in_specs=[pl.BlockSpec((tm, d), lambda i: (i, 0)), _full((1, d)), _resident(w4.shape)],
        out_specs=[wide, wide, wide, pl.BlockSpec((tm, d), lambda i: (i, 0))],
        out_shape=[jax.ShapeDtypeStruct((t, D_FF), BF16)] * 3 + [jax.ShapeDtypeStruct((t, d), BF16)],
        operands=(h, g, w4), comm=comm)


def _ple_fwd(h, p, wg, wp, name, comm=None):
    t, d = h.shape
    tm = _tile(t, ROW_TILE_LIGHT)

    def body(h_ref, p_ref, wg_ref, wp_ref, o_ref, q_ref, e_ref):
        hh = h_ref[...]
        q = _dot(hh, wg_ref[...])
        e = _dot(p_ref[...], wp_ref[...])
        q_ref[...] = q.astype(BF16)
        e_ref[...] = e.astype(BF16)
        o_ref[...] = hh + _sigmoid(q) * e

    row = pl.BlockSpec((tm, d), lambda i: (i, 0))
    return _pcall(
        body, name=name, grid=(t // tm,),
        in_specs=[row, pl.BlockSpec((tm, p.shape[1]), lambda i: (i, 0)), _resident(wg.shape), _resident(wp.shape)],
        out_specs=[row, row, row],
        out_shape=[jax.ShapeDtypeStruct((t, d), F32), jax.ShapeDtypeStruct((t, d), BF16),
                   jax.ShapeDtypeStruct((t, d), BF16)],
        operands=(h, p, wg, wp), comm=comm)


def _loss_head(y, target, name):
    t, d = y.shape
    tm = _tile(t, ROW_TILE_LIGHT)

    def body(y_ref, t_ref, dy_ref, l_ref):
        @pl.when(pl.program_id(0) == 0)
        def _():
            l_ref[...] = jnp.zeros_like(l_ref)

        err = y_ref[...] - t_ref[...]
        dy_ref[...] = err * (1.0 / d)
        l_ref[...] += jnp.sum(err * err, keepdims=True)[:, :1] * jnp.ones((1, 128), F32)

    row = pl.BlockSpec((tm, d), lambda i: (i, 0))
    return pl.pallas_call(
        body, name=name, grid=(t // tm,),
        in_specs=[row, row], out_specs=[row, _full((1, 128))],
        out_shape=[jax.ShapeDtypeStruct((t, d), F32), jax.ShapeDtypeStruct((1, 128), F32)],
        compiler_params=_params(1))(y, target)


def _flush_slots(acc_ref, out_ref, slots, sem_ref):
    rows = out_ref.shape[2]

    @pl.when(pl.program_id(0) == pl.num_programs(0) - 1)
    def _():
        copies = [pltpu.make_async_copy(acc_ref.at[k, pl.ds(j * rows, rows)], out_ref.at[j, slot],
                                        sem_ref.at[N_CHIPS * k + j])
                  for k, slot in enumerate(slots) for j in range(N_CHIPS)]
        for cp in copies:
            cp.start()
        for cp in copies:
            cp.wait()


def _ple_bwd(dh, q, e, wg, h_in, p, slot, n_slots, name):
    t, d = dh.shape
    tm = _tile(t, ROW_TILE)
    rows = d // N_CHIPS

    def body(dh_ref, q_ref, e_ref, wg_ref, h_ref, p_ref, o_ref, sq_ref, dwp_ref, acc_ref, sem_ref):
        @pl.when(pl.program_id(0) == 0)
        def _():
            acc_ref[...] = jnp.zeros_like(acc_ref)
            dwp_ref[...] = jnp.zeros_like(dwp_ref)

        dh_ = dh_ref[...]
        s = _sigmoid(q_ref[...].astype(F32))
        dq = (dh_ * e_ref[...].astype(F32) * s * (1.0 - s)).astype(BF16)
        o_ref[...] = dh_ + _dot_nt(dq, wg_ref[...])
        acc_ref[0] += _dot_tn(h_ref[...], dq)
        dwp_ref[...] += _dot_tn(p_ref[...], dh_ * s)
        _flush_slots(acc_ref, sq_ref, (slot,), sem_ref)

    row = pl.BlockSpec((tm, d), lambda i: (i, 0))
    return _pcall(
        body, name=name, grid=(t // tm,),
        in_specs=[row, row, row, _resident(wg.shape), row, pl.BlockSpec((tm, p.shape[1]), lambda i: (i, 0))],
        out_specs=[row, ANY, _full((p.shape[1], d))],
        out_shape=[jax.ShapeDtypeStruct((t, d), F32), jax.ShapeDtypeStruct((N_CHIPS, n_slots, rows, d), F32),
                   jax.ShapeDtypeStruct((p.shape[1], d), F32)],
        scratch_shapes=[pltpu.VMEM((1, d, d), F32), pltpu.SemaphoreType.DMA((N_CHIPS,))],
        operands=(dh, q, e, wg, h_in, p))


def _ffn_out_bwd(dh, f, g, fg, fu, act, w, name, comm=None):
    t, d = dh.shape
    tm = _tile(t, ROW_TILE_HEAVY)

    def body(dh_ref, f_ref, g_ref, fg_ref, fu_ref, act_ref, w_ref, dff_ref, dg_ref, dw_ref, acc_ref, sem_ref):
        @pl.when(pl.program_id(0) == 0)
        def _():
            dg_ref[...] = jnp.zeros_like(dg_ref)
            acc_ref[...] = jnp.zeros_like(acc_ref)

        fn, r = _rms_stats(f_ref[...])
        df, dgt = _rms_bwd(fn, r, g_ref[...], dh_ref[...])
        dg_ref[...] += _rowsum(dgt)
        df = df.astype(BF16)
        acc_ref[...] += _dot_tn(act_ref[...], df)
        for c0 in range(0, D_FF, FF_CHUNK):
            cols = slice(c0, c0 + FF_CHUNK)
            dact = _dot_nt(df, w_ref[cols, :])
            fg_ = fg_ref[:, cols].astype(F32)
            s = _sigmoid(fg_)
            gs = fg_ * s
            dff_ref[:, cols] = (dact * fu_ref[:, cols].astype(F32) * (s + gs - gs * s)).astype(BF16)
            dff_ref[:, D_FF + c0:D_FF + c0 + FF_CHUNK] = (dact * gs).astype(BF16)

        @pl.when(pl.program_id(0) == pl.num_programs(0) - 1)
        def _():
            cp = pltpu.make_async_copy(acc_ref, dw_ref, sem_ref.at[0])
            cp.start()
            cp.wait()

    row = pl.BlockSpec((tm, d), lambda i: (i, 0))
    wide = pl.BlockSpec((tm, D_FF), lambda i: (i, 0))
    return _pcall(
        body, name=name, grid=(t // tm,),
        in_specs=[row, row, _full(g.shape), wide, wide, wide, _resident(w.shape)],
        out_specs=[pl.BlockSpec((tm, 2 * D_FF), lambda i: (i, 0)), _full((1, d)), ANY],
        out_shape=[jax.ShapeDtypeStruct((t, 2 * D_FF), BF16), jax.ShapeDtypeStruct((1, d), F32),
                   jax.ShapeDtypeStruct((D_FF, d), F32)],
        scratch_shapes=[pltpu.VMEM((D_FF, d), F32), pltpu.SemaphoreType.DMA((1,))],
        operands=(dh, f, g, fg, fu, act, w), comm=comm)


def _in_bwd(pieces, w4, unit, h, g, dres, tm, name, comm=None):
    t, d = h.shape
    tm = _tile(t, tm)
    per_chunk = w4.shape[2] // unit
    n_p = len(pieces)

    def body(*refs):
        p_refs = refs[:n_p]
        w_ref, h_ref, g_ref, r_ref, o_ref, dg_ref = refs[n_p:]

        @pl.when(pl.program_id(0) == 0)
        def _():
            dg_ref[...] = jnp.zeros_like(dg_ref)

        acc = None
        u = 0
        for p_ref, (_, nu) in zip(p_refs, pieces):
            for k in range(nu):
                lanes = slice((u % per_chunk) * unit, (u % per_chunk + 1) * unit)
                term = _dot_nt(p_ref[:, k * unit:(k + 1) * unit], w_ref[u // per_chunk, :, lanes])
                acc = term if acc is None else acc + term
                u += 1
        xn, r = _rms_stats(h_ref[...])
        dx, dgt = _rms_bwd(xn, r, g_ref[...], acc)
        dg_ref[...] += _rowsum(dgt)
        o_ref[...] = r_ref[...] + dx

    row = pl.BlockSpec((tm, d), lambda i: (i, 0))
    return _pcall(
        body, name=name, grid=(t // tm,),
        in_specs=[pl.BlockSpec((tm, a.shape[1]), lambda i: (i, 0)) for a, _ in pieces]
        + [_resident(w4.shape), row, _full((1, d)), row],
        out_specs=[row, _full((1, d))],
        out_shape=[jax.ShapeDtypeStruct((t, d), F32), jax.ShapeDtypeStruct((1, d), F32)],
        operands=(*[a for a, _ in pieces], w4, h, g, dres), comm=comm)


def _lane_block(n, cap):
    return max(b for b in range(128, min(n, cap) + 1, 128) if n % b == 0)


def _dw_cols(x, pieces, unit, per_chunk, name):
    t, m = x.shape
    tk = _tile(t, TOKEN_TILE)
    offs, total = [], 0
    for _, nu in pieces:
        offs.append(total)
        total += nu

    def body(x_ref, *refs):
        o_ref = refs[-1]
        u = pl.program_id(0)

        @pl.when(pl.program_id(1) == 0)
        def _():
            o_ref[...] = jnp.zeros_like(o_ref)

        for p_ref, off, (_, nu) in zip(refs[:-1], offs, pieces):
            @pl.when((u >= off) & (u < off + nu))
            def _(p_ref=p_ref):
                o_ref[...] += _dot_tn(x_ref[...], p_ref[...])

    def piece_spec(off, nu):
        def index(u, k):
            mine = (u >= off) & (u < off + nu)
            return jnp.where(mine, k, 0), jnp.clip(u - off, 0, nu - 1)
        return pl.BlockSpec((tk, unit), index)

    return pl.pallas_call(
        body, name=name, grid=(total, t // tk),
        in_specs=[pl.BlockSpec((tk, m), lambda u, k: (k, 0))] + [piece_spec(o, nu) for o, (_, nu) in zip(offs, pieces)],
        out_specs=pl.BlockSpec((None, None, m, unit), lambda u, k: (u // per_chunk, 0, 0, u % per_chunk)),
        out_shape=jax.ShapeDtypeStruct((N_CHIPS, 1, m, per_chunk * unit), F32),
        compiler_params=_params(2))(x, *[a for a, _ in pieces])


def _merge_bwd(dh, mo, g, proj, br, xs, merged, w_out, ws, sq, slots, name, comm=None):
    t, d = dh.shape
    tm = _tile(t, ROW_TILE_HEAVY)
    rows = d // N_CHIPS

    def body(dh_ref, mo_ref, g_ref, za_ref, zb_ref, zc_ref, ba_ref, bb_ref, bc_ref, xa_ref, xb_ref, xc_ref, m_ref,
             wo_ref, wa_ref, wb_ref, wc_ref, sq_in_ref, dz_ref, dsg_ref, dcs_ref, dps_ref, dg_ref, sq_ref,
             acc_ref, sem_ref):
        @pl.when(pl.program_id(0) == 0)
        def _():
            dg_ref[...] = jnp.zeros_like(dg_ref)
            acc_ref[...] = jnp.zeros_like(acc_ref)

        mon, r = _rms_stats(mo_ref[...])
        dmo, dgt = _rms_bwd(mon, r, g_ref[...], dh_ref[...])
        dg_ref[...] += _rowsum(dgt)
        dmo = dmo.astype(BF16)
        acc_ref[0] += _dot_tn(m_ref[...], dmo)
        dmerged = _dot_nt(dmo, wo_ref[...])
        branches = ((za_ref, ba_ref, xa_ref, wa_ref, dsg_ref), (zb_ref, bb_ref, xb_ref, wb_ref, dcs_ref),
                    (zc_ref, bc_ref, xc_ref, wc_ref, dps_ref))
        for j, (z_ref, b_ref, x_ref, w_ref, dx_ref) in enumerate(branches):
            gate = _sigmoid(z_ref[...].astype(F32))
            dbr = (dmerged * gate).astype(BF16)
            dz_ref[:, j * d:(j + 1) * d] = (dmerged * b_ref[...].astype(F32) * gate * (1.0 - gate)).astype(BF16)
            dx_ref[...] = _dot_nt(dbr, w_ref[...]).astype(BF16)
            acc_ref[1 + j] += _dot_tn(x_ref[...], dbr)
        _flush_slots(acc_ref, sq_ref, slots, sem_ref)

    row = pl.BlockSpec((tm, d), lambda i: (i, 0))
    wspec = _resident((d, d))
    bf = jax.ShapeDtypeStruct((t, d), BF16)
    n_in = 18
    return _pcall(
        body, name=name, grid=(t // tm,),
        in_specs=[row, row, _full(g.shape), pl.BlockSpec((tm, d), lambda i: (i, 5)),
                  pl.BlockSpec((tm, d), lambda i: (i, 6)), pl.BlockSpec((tm, d), lambda i: (i, 7)),
                  row, row, row, row, row, row, row, wspec, wspec, wspec, wspec, ANY],
        out_specs=[pl.BlockSpec((tm, 3 * d), lambda i: (i, 0)), row, row, row, _full((1, d)), ANY],
        out_shape=[jax.ShapeDtypeStruct((t, 3 * d), BF16), bf, bf, bf, jax.ShapeDtypeStruct((1, d), F32),
                   jax.ShapeDtypeStruct(sq.shape, sq.dtype)],
        scratch_shapes=[pltpu.VMEM((4, d, d), F32), pltpu.SemaphoreType.DMA((4 * N_CHIPS,))],
        operands=(dh, mo, g, proj, proj, proj, *br, *xs, merged, w_out, *ws, sq), comm=comm,
        aliases={n_in - 1: 5})


def _sgu_bwd(proj, dsg, wm, bs3, gv, bv, name, comm=None):
    t = proj.shape[0]
    d = D_MODEL
    tm = _tile(t, ROW_TILE_HEAVY)
    hd = d // SGU_HEADS

    def body(zu_ref, zv_ref, d_ref, wm_ref, bs_ref, gv_ref, bv_ref, dz_ref, dwm_ref, dbs_ref, dgv_ref, dbv_ref,
             dvn_ref):
        @pl.when(pl.program_id(0) == 0)
        def _():
            dwm_ref[...] = jnp.zeros_like(dwm_ref)
            dbs_ref[...] = jnp.zeros_like(dbs_ref)
            dgv_ref[...] = jnp.zeros_like(dgv_ref)
            dbv_ref[...] = jnp.zeros_like(dbv_ref)

        mask = _sgu_mask()
        for blk in range(tm // SGU_BLOCK):
            rows = pl.ds(blk * SGU_BLOCK, SGU_BLOCK)
            u, du_dz = _gelu_and_grad(zu_ref[rows, :].astype(F32))
            v0, dv_dz = _gelu_and_grad(zv_ref[rows, :].astype(F32))
            xhat, rstd = _ln_stats(v0)
            vn = (xhat * gv_ref[...] + bv_ref[...]).astype(BF16)
            dsg = d_ref[rows, :].astype(F32)
            dmix = (dsg * u).astype(BF16)
            for hh in range(SGU_HEADS):
                cols = slice(hh * hd, (hh + 1) * hd)
                wmh = jnp.where(mask, wm_ref[hh], 0.0).astype(BF16)
                vb = vn[:, cols]
                mixed = jnp.dot(wmh, vb, preferred_element_type=F32) + bs_ref[hh]
                dz_ref[rows, cols] = (dsg[:, cols] * mixed * du_dz[:, cols]).astype(BF16)
                dmh = dmix[:, cols]
                dwm_ref[hh] += jnp.where(mask, _dot_nt(dmh, vb), 0.0)
                dbs_ref[hh] += jnp.sum(dmh.astype(F32), axis=1, keepdims=True)
                dvn_ref[:, cols] = _dot_tn(wmh, dmh)
            dvn = dvn_ref[...]
            dgv_ref[...] += _rowsum(dvn * xhat)
            dbv_ref[...] += _rowsum(dvn)
            dz_ref[rows, d:2 * d] = (_ln_bwd(xhat, rstd, gv_ref[...], dvn) * dv_dz).astype(BF16)

    return _pcall(
        body, name=name, grid=(t // tm,),
        in_specs=[pl.BlockSpec((tm, d), lambda i: (i, 0)), pl.BlockSpec((tm, d), lambda i: (i, 1)),
                  pl.BlockSpec((tm, d), lambda i: (i, 0)), _full(wm.shape), _full(bs3.shape), _full(gv.shape),
                  _full(bv.shape)],
        out_specs=[pl.BlockSpec((tm, 2 * d), lambda i: (i, 0)), _full(wm.shape), _full(bs3.shape), _full((1, d)),
                   _full((1, d))],
        out_shape=[jax.ShapeDtypeStruct((t, 2 * d), BF16), jax.ShapeDtypeStruct(wm.shape, F32),
                   jax.ShapeDtypeStruct(bs3.shape, F32), jax.ShapeDtypeStruct((1, d), F32),
                   jax.ShapeDtypeStruct((1, d), F32)],
        scratch_shapes=[pltpu.VMEM((SGU_BLOCK, d), F32)],
        operands=(proj, proj, dsg, wm, bs3, gv, bv), comm=comm)


def _conv_bwd_norm(proj, dcs, cv, bdw, gln, bln, name, comm=None):
    t = proj.shape[0]
    d = D_MODEL
    tm = _tile(t, ROW_TILE)
    main, halo = _conv_specs(t, tm, d)
    n_win = CONV_ROWS + CONV_HALO

    def body(a_ref, g_ref, ah_ref, gh_ref, dcs_ref, cv_ref, b_ref, gl_ref, bl_ref,
             dcv_ref, dw_ref, db_ref, dgl_ref, dbl_ref, scr_ref, dwacc_ref):
        @pl.when(pl.program_id(0) == 0)
        def _():
            dwacc_ref[...] = jnp.zeros_like(dwacc_ref)
            db_ref[...] = jnp.zeros_like(db_ref)
            dgl_ref[...] = jnp.zeros_like(dgl_ref)
            dbl_ref[...] = jnp.zeros_like(dbl_ref)

        _fill_glu_history(scr_ref, a_ref, g_ref, ah_ref, gh_ref, tm)
        xhat, rstd = _ln_stats(cv_ref[...] + b_ref[...])
        cn = xhat * gl_ref[...] + bl_ref[...]
        s = _sigmoid(cn)
        dcn = dcs_ref[...].astype(F32) * (s * (1.0 + cn * (1.0 - s)))
        dgl_ref[...] += _rowsum(dcn * xhat)
        dbl_ref[...] += _rowsum(dcn)
        dcv = _ln_bwd(xhat, rstd, gl_ref[...], dcn)
        db_ref[...] += _rowsum(dcv)
        dcv_ref[...] = dcv

        def chunk(ci, carry):
            r0 = pl.multiple_of(ci * CONV_ROWS, CONV_ROWS)
            for c0 in range(0, d, CONV_LANES):
                lanes = pl.ds(c0, CONV_LANES)
                win = scr_ref[pl.ds(r0, n_win), lanes]
                dchunk = dcv_ref[pl.ds(r0, CONV_ROWS), lanes]
                for r in range(8):
                    rolled = win if r == 0 else pltpu.roll(win, n_win - r, 0)
                    for q in range(n_win // 8):
                        k = 8 * q + r - _CONV_BASE
                        if 0 <= k < CONV_WIDTH and 8 * q + CONV_ROWS <= n_win:
                            prod = dchunk * rolled[8 * q:8 * q + CONV_ROWS]
                            part = prod[0:8]
                            for s8 in range(8, CONV_ROWS, 8):
                                part = part + prod[s8:s8 + 8]
                            dwacc_ref[pl.ds(8 * k, 8), lanes] += part
            return carry

        lax.fori_loop(0, tm // CONV_ROWS, chunk, 0)

        @pl.when(pl.program_id(0) == pl.num_programs(0) - 1)
        def _():
            dw_ref[...] = jnp.sum(dwacc_ref[...].reshape(CONV_HALO, 8, d), axis=1)

    row = pl.BlockSpec((tm, d), lambda i: (i, 0))
    vec = _full((1, d))
    return _pcall(
        body, name=name, grid=(t // tm,),
        in_specs=main + halo + [row, row, vec, vec, vec],
        out_specs=[row, _full((CONV_HALO, d)), vec, vec, vec],
        out_shape=[jax.ShapeDtypeStruct((t, d), F32), jax.ShapeDtypeStruct((CONV_HALO, d), F32)]
        + [jax.ShapeDtypeStruct((1, d), F32)] * 3,
        scratch_shapes=[pltpu.VMEM((tm + CONV_HALO, d), F32), pltpu.VMEM((8 * CONV_HALO, d), F32)],
        operands=(proj, proj, proj, proj, dcs, cv, bdw, gln, bln), comm=comm)


def _conv_bwd_taps(proj, dcv, wdw, name, comm=None):
    t = proj.shape[0]
    d = D_MODEL
    tm = _tile(t, ROW_TILE)
    hb = tm // CONV_HALO
    last_halo = t // CONV_HALO - 1

    def body(a_ref, g_ref, dcv_ref, dnext_ref, w_ref, dz_ref, scr_ref, dh_ref):
        scr_ref[0:tm, :] = dcv_ref[...]
        is_last = pl.program_id(0) == pl.num_programs(0) - 1
        scr_ref[tm:tm + CONV_HALO, :] = jnp.where(is_last, 0.0, dnext_ref[...])
        _conv_into(scr_ref, dh_ref, w_ref, tm, 0, True)
        dglu = dh_ref[...]
        a = a_ref[...].astype(F32)
        s = _sigmoid(g_ref[...].astype(F32))
        dz_ref[:, 0:d] = (dglu * s).astype(BF16)
        dz_ref[:, d:2 * d] = (dglu * a * s * (1.0 - s)).astype(BF16)

    return _pcall(
        body, name=name, grid=(t // tm,),
        in_specs=[pl.BlockSpec((tm, d), lambda i: (i, 2)), pl.BlockSpec((tm, d), lambda i: (i, 3)),
                  pl.BlockSpec((tm, d), lambda i: (i, 0)),
                  pl.BlockSpec((CONV_HALO, d), lambda i: (jnp.minimum((i + 1) * hb, last_halo), 0)),
                  _full(wdw.shape)],
        out_specs=pl.BlockSpec((tm, 2 * d), lambda i: (i, 0)),
        out_shape=jax.ShapeDtypeStruct((t, 2 * d), BF16),
        scratch_shapes=[pltpu.VMEM((tm + CONV_HALO, d), F32), pltpu.VMEM((tm, d), F32)],
        operands=(proj, proj, dcv, dcv, wdw), comm=comm)


def _pool_bwd(proj, dps, wpool, spool, name):
    t = proj.shape[0]
    d = D_MODEL
    tm = _tile(t, ROW_TILE)
    hb = tm // POOL_BLOCK
    last_halo = t // POOL_BLOCK - 1
    ext = tm + POOL_BLOCK

    def body(z_ref, zh_ref, d_ref, dnext_ref, w_ref, s_ref, dz_ref, dw_ref, ds_ref, scr_ref, dext_ref, hi_ref, lo_ref):
        @pl.when(pl.program_id(0) == 0)
        def _():
            dw_ref[...] = jnp.zeros_like(dw_ref)
            ds_ref[...] = jnp.zeros_like(ds_ref)

        _pool_fill(scr_ref, z_ref, zh_ref, tm)
        t0 = pl.program_id(0) * tm
        is_last = pl.program_id(0) == pl.num_programs(0) - 1
        dext_ref[0:tm, :] = d_ref[...].astype(F32)
        dext_ref[tm:ext, :] = jnp.where(is_last, 0.0, dnext_ref[...].astype(F32))
        for gi, w in enumerate(POOL_WINDOWS):
            cols = slice(gi * POOL_GROUP, (gi + 1) * POOL_GROUP)
            dps_ext = dext_ref[:, cols]
            dpm_ext = (dps_ext * s_ref[:, cols]).astype(BF16)
            dpooled_ext = _dot_nt(dpm_ext, w_ref[gi])
            dq = dpooled_ext / _pool_count(t0, ext, w)
            hi = dq.astype(BF16)
            hi_ref[...] = hi
            lo_ref[...] = (dq - hi.astype(F32)).astype(BF16)
            sums = _window_sums([hi_ref, lo_ref], _band(w, True), tm, slice(None))
            dz_ref[:, cols] = (sums - dpooled_ext[0:tm]).astype(BF16)
            pooled = _pooled_group(scr_ref, gi, w, tm, t0).astype(BF16)
            pm = jnp.dot(pooled, w_ref[gi], preferred_element_type=F32)
            ds_ref[:, cols] += _rowsum(dps_ext[0:tm] * pm)
            dw_ref[gi] += _dot_tn(pooled, dpm_ext[0:tm])

    return pl.pallas_call(
        body, name=name, grid=(t // tm,),
        in_specs=_pool_specs(tm, d) + [pl.BlockSpec((tm, d), lambda i: (i, 0)),
                                       pl.BlockSpec((POOL_BLOCK, d), lambda i: (jnp.minimum((i + 1) * hb, last_halo), 0)),
                                       _full(wpool.shape), _full(spool.shape)],
        out_specs=[pl.BlockSpec((tm, d), lambda i: (i, 0)), _full(wpool.shape), _full((1, d))],
        out_shape=[jax.ShapeDtypeStruct((t, d), BF16), jax.ShapeDtypeStruct(wpool.shape, F32),
                   jax.ShapeDtypeStruct((1, d), F32)],
        scratch_shapes=[pltpu.VMEM((tm + POOL_BLOCK, d), BF16), pltpu.VMEM((ext, d), F32),
                        pltpu.VMEM((ext, POOL_GROUP), BF16), pltpu.VMEM((ext, POOL_GROUP), BF16)],
        compiler_params=_params(1))(proj, proj, dps, dps, wpool, spool)


ANY = pl.BlockSpec(memory_space=pl.ANY)


def _mesh_pos():
    x, y, c = lax.axis_index("x"), lax.axis_index("y"), lax.axis_index("c")
    chips = [(1 - x, y), (x, 1 - y), (1 - x, 1 - y)]
    return x, y, c, chips


def _chip_of(xy):
    return 2 * xy[0] + xy[1]


def _half_view(a):
    return a.reshape(a.shape[:-2] + (2, a.shape[-2] // 2, a.shape[-1]))


def _same(arrs):
    return [jax.ShapeDtypeStruct(a.shape, a.dtype) for a in arrs]


def _in_place(n):
    return {g: g for g in range(n)}


def _sems(count):
    return [pltpu.SemaphoreType.DMA((count,)), pltpu.SemaphoreType.DMA((count,))]


def _gather_ici(bufs):
    n = len(bufs)

    def copy(buf, sems, g, j, chip):
        x, y, c, chips = _mesh_pos()
        slab = buf[g].at[chip, :, c]
        return pltpu.make_async_remote_copy(
            src_ref=slab, dst_ref=slab, send_sem=sems[0].at[3 * g + j], recv_sem=sems[1].at[3 * g + j],
            device_id=(*chips[j], c), device_id_type=MESH)

    def start(ins, buf, sems):
        x, y, c, chips = _mesh_pos()
        for g in range(n):
            for j in range(3):
                copy(buf, sems, g, j, 2 * x + y).start()

    def finish(ins, buf, sems):
        x, y, c, chips = _mesh_pos()
        for g in range(n):
            for j in range(3):
                copy(buf, sems, g, j, _chip_of(chips[j])).wait_recv()
        for g in range(n):
            for j in range(3):
                copy(buf, sems, g, j, 2 * x + y).wait_send()

    return _Payload(bufs, _same(bufs), _in_place(n), _sems(3 * n), start, finish)


def _gather_d2d(bufs):
    n = len(bufs)

    def copy(buf, sems, g, j, half):
        x, y, c, chips = _mesh_pos()
        slab = buf[g].at[_chip_of(chips[j]), :, half]
        return pltpu.make_async_remote_copy(
            src_ref=slab, dst_ref=slab, send_sem=sems[0].at[3 * g + j], recv_sem=sems[1].at[3 * g + j],
            device_id=(x, y, 1 - c), device_id_type=MESH)

    def start(ins, buf, sems):
        c = lax.axis_index("c")
        for g in range(n):
            for j in range(3):
                copy(buf, sems, g, j, c).start()

    def finish(ins, buf, sems):
        c = lax.axis_index("c")
        for g in range(n):
            for j in range(3):
                copy(buf, sems, g, j, 1 - c).wait_recv()
        for g in range(n):
            for j in range(3):
                copy(buf, sems, g, j, c).wait_send()

    return _Payload(bufs, _same(bufs), _in_place(n), _sems(3 * n), start, finish)


def _pair_exchange(grads):
    n = len(grads)

    def copy(src, dst, sems, g):
        x, y, c, _ = _mesh_pos()
        return pltpu.make_async_remote_copy(
            src_ref=src[g].at[:, :, 1 - c], dst_ref=dst[g], send_sem=sems[0].at[g], recv_sem=sems[1].at[g],
            device_id=(x, y, 1 - c), device_id_type=MESH)

    def start(src, dst, sems):
        for g in range(n):
            copy(src, dst, sems, g).start()

    def finish(src, dst, sems):
        for g in range(n):
            copy(src, dst, sems, g).wait()

    out_shape = [jax.ShapeDtypeStruct(g.shape[:2] + g.shape[3:], g.dtype) for g in grads]
    return _Payload(grads, out_shape, {}, _sems(n), start, finish)


def _chip_exchange(parts):
    n = len(parts)

    def copy(src, dst, sems, g, j, slot):
        x, y, c, chips = _mesh_pos()
        return pltpu.make_async_remote_copy(
            src_ref=src[g].at[_chip_of(chips[j])], dst_ref=dst[g].at[slot], send_sem=sems[0].at[3 * g + j],
            recv_sem=sems[1].at[3 * g + j], device_id=(*chips[j], c), device_id_type=MESH)

    def start(src, dst, sems):
        x, y, c, chips = _mesh_pos()
        for g in range(n):
            for j in range(3):
                copy(src, dst, sems, g, j, 2 * x + y).start()

    def finish(src, dst, sems):
        x, y, c, chips = _mesh_pos()
        for g in range(n):
            for j in range(3):
                copy(src, dst, sems, g, j, _chip_of(chips[j])).wait_recv()
        for g in range(n):
            for j in range(3):
                copy(src, dst, sems, g, j, 2 * x + y).wait_send()

    return _Payload(parts, _same(parts), {}, _sems(3 * n), start, finish)


def _pair_share(bufs):
    n = len(bufs)

    def copy(buf, sems, g, half):
        x, y, c, _ = _mesh_pos()
        slab = buf[g].at[:, :, half]
        return pltpu.make_async_remote_copy(
            src_ref=slab, dst_ref=slab, send_sem=sems[0].at[g], recv_sem=sems[1].at[g],
            device_id=(x, y, 1 - c), device_id_type=MESH)

    def start(ins, buf, sems):
        c = lax.axis_index("c")
        for g in range(n):
            copy(buf, sems, g, c).start()

    def finish(ins, buf, sems):
        c = lax.axis_index("c")
        for g in range(n):
            copy(buf, sems, g, 1 - c).wait_recv()
        for g in range(n):
            copy(buf, sems, g, c).wait_send()

    return _Payload(bufs, _same(bufs), _in_place(n), _sems(n), start, finish)


def _join(a, b):
    if a is None or b is None:
        return a or b
    na, ma = len(a.operands), len(a.out_shape)
    aliases = dict(a.aliases)
    aliases.update({na + i: ma + o for i, o in b.aliases.items()})
    ka = len(a.scratch)

    def start(ins, outs, sems):
        a.start(ins[:na], outs[:ma], sems[:ka])
        b.start(ins[na:], outs[ma:], sems[ka:])

    def finish(ins, outs, sems):
        a.finish(ins[:na], outs[:ma], sems[:ka])
        b.finish(ins[na:], outs[ma:], sems[ka:])

    joined = _Payload(a.operands + b.operands, a.out_shape + b.out_shape, aliases, list(a.scratch) + list(b.scratch),
                      start, finish)
    joined.parts = (a, b, ma)
    return joined


def _small_exchange(vec):
    def copy(src, dst, sems, k, slot):
        x, y, c, _ = _mesh_pos()
        peer = (x ^ (k >> 2), y ^ ((k >> 1) & 1), c ^ (k & 1))
        return pltpu.make_async_remote_copy(
            src_ref=src[0], dst_ref=dst[0].at[slot], send_sem=sems[0].at[k - 1], recv_sem=sems[1].at[k - 1],
            device_id=peer, device_id_type=MESH)

    def me():
        x, y, c, _ = _mesh_pos()
        return 4 * x + 2 * y + c

    def start(src, dst, sems):
        for k in range(1, 8):
            copy(src, dst, sems, k, me()).start()

    def finish(src, dst, sems):
        for k in range(1, 8):
            copy(src, dst, sems, k, me() ^ k).wait_recv()
        for k in range(1, 8):
            copy(src, dst, sems, k, me()).wait_send()

    return _Payload([vec], [jax.ShapeDtypeStruct((8,) + vec.shape, vec.dtype)], {}, _sems(7), start, finish)


def _small_sum(vec, landed, pos, name):
    r = vec.shape[0]

    def body(pos_ref, v_ref, l_ref, o_ref):
        k = pl.program_id(0)

        @pl.when(k == 0)
        def _():
            o_ref[...] = jnp.zeros_like(o_ref)

        @pl.when(k == pos_ref[POS_DEVICE])
        def _():
            o_ref[...] += v_ref[...]

        @pl.when(k != pos_ref[POS_DEVICE])
        def _():
            o_ref[...] += l_ref[...]

    def landed_index(k, pos_ref):
        me = pos_ref[POS_DEVICE]
        return jnp.where(k == me, (me + 1) % 8, k), 0, 0

    return pl.pallas_call(
        body, name=name,
        grid_spec=pltpu.PrefetchScalarGridSpec(
            num_scalar_prefetch=1, grid=(8,),
            in_specs=[pl.BlockSpec((r, 128), lambda k, pos_ref: (0, 0)), pl.BlockSpec((None, r, 128), landed_index)],
            out_specs=pl.BlockSpec((r, 128), lambda k, pos_ref: (0, 0))),
        out_shape=jax.ShapeDtypeStruct(vec.shape, F32),
        compiler_params=_params(1))(pos, vec, landed)


def _all_reduce_small(vec, name):
    r = vec.shape[0]

    def body(v_ref, o_ref, gath_ref, send_sem, recv_sem):
        x, y, c, _ = _mesh_pos()
        me = 4 * x + 2 * y + c
        gath_ref[me] = v_ref[...]
        copies = []
        for k in range(1, 8):
            peer = (x ^ (k >> 2), y ^ ((k >> 1) & 1), c ^ (k & 1))
            cp = pltpu.make_async_remote_copy(
                src_ref=v_ref, dst_ref=gath_ref.at[me], send_sem=send_sem.at[k - 1], recv_sem=recv_sem.at[k - 1],
                device_id=peer, device_id_type=MESH)
            cp.start()
            copies.append(cp)
        for k in range(1, 8):
            src_id = me ^ k
            pltpu.make_async_remote_copy(
                src_ref=v_ref, dst_ref=gath_ref.at[src_id], send_sem=send_sem.at[k - 1], recv_sem=recv_sem.at[k - 1],
                device_id=(x, y, c), device_id_type=MESH).wait_recv()
        for cp in copies:
            cp.wait_send()
        acc = gath_ref[0]
        for k in range(1, 8):
            acc = acc + gath_ref[k]
        o_ref[...] = acc

    return pl.pallas_call(
        body, name=name,
        in_specs=[pl.BlockSpec(memory_space=pltpu.VMEM)], out_specs=pl.BlockSpec(memory_space=pltpu.VMEM),
        out_shape=jax.ShapeDtypeStruct(vec.shape, F32),
        scratch_shapes=[pltpu.VMEM((8, r, 128), F32), pltpu.SemaphoreType.DMA((7,)), pltpu.SemaphoreType.DMA((7,))],
        compiler_params=pltpu.CompilerParams(has_side_effects=True, vmem_limit_bytes=VMEM_LIMIT))(vec)


def _row_block(rows, cols, mult=16):
    best = None
    for cand in range(mult, rows + 1, mult):
        if rows % cand == 0 and cand * cols * 4 <= EW_BLOCK_BYTES:
            best = cand
    return best or rows


POS_ME, POS_CORE, POS_DEVICE = 0, 4, 5


def _place(arrs, li, pos, dtype, name):
    s = len(arrs)
    _, rows, cols = arrs[0].shape
    rh = rows // 2
    tr = _row_block(rh, cols)
    nb = rh // tr

    def body(pos_ref, *refs):
        o_ref = refs[s]
        for j in range(s):
            @pl.when(pl.program_id(0) == j)
            def _(j=j):
                o_ref[...] = refs[j][...].astype(dtype)

    def in_spec(j):
        return pl.BlockSpec((None, tr, cols), lambda b, hf, i, pos_ref: (li, jnp.where(b == j, hf * nb + i, 0), 0))

    return pl.pallas_call(
        body, name=name,
        grid_spec=pltpu.PrefetchScalarGridSpec(
            num_scalar_prefetch=1, grid=(s, 2, nb), in_specs=[in_spec(j) for j in range(s)],
            out_specs=pl.BlockSpec((None, None, None, tr, cols),
                                   lambda b, hf, i, pos_ref: (pos_ref[POS_ME], b, hf, i, 0))),
        out_shape=jax.ShapeDtypeStruct((N_CHIPS, s, 2, rh, cols), dtype),
        compiler_params=_params(3))(pos, *arrs)


def _pair_sum(grad, recv, pos, out_dtype, name):
    _, s, rh, cols = recv.shape
    tr = _row_block(rh, cols)

    def body(pos_ref, g_ref, r_ref, o_ref):
        o_ref[...] = (g_ref[...] + r_ref[...]).astype(out_dtype)

    blk = (None, None, tr, cols)
    return pl.pallas_call(
        body, name=name,
        grid_spec=pltpu.PrefetchScalarGridSpec(
            num_scalar_prefetch=1, grid=(N_CHIPS, s, rh // tr),
            in_specs=[pl.BlockSpec((None, None, None, tr, cols),
                                   lambda a, b, i, pos_ref: (a, b, pos_ref[POS_CORE], i, 0)),
                      pl.BlockSpec(blk, lambda a, b, i, pos_ref: (a, b, i, 0))],
            out_specs=pl.BlockSpec(blk, lambda a, b, i, pos_ref: (a, b, i, 0))),
        out_shape=jax.ShapeDtypeStruct(recv.shape, out_dtype),
        compiler_params=_params(3))(pos, grad, recv)


def _chip_sum(part, landed, gbuf, li, n_layers, pos, name):
    _, s, rh, cols = part.shape
    tr = _row_block(rh, cols)

    def body(pos_ref, p_ref, a_ref, b_ref, c_ref, *rest):
        o_ref = rest[-1]
        o_ref[...] = ((p_ref[...].astype(F32) + a_ref[...].astype(F32)) + b_ref[...].astype(F32)) \
            + c_ref[...].astype(F32)

    def slot(k):
        return pl.BlockSpec((None, None, tr, cols), lambda b, i, pos_ref: (pos_ref[k], b, i, 0))

    in_specs = [slot(0), slot(1), slot(2), slot(3)]
    operands = [pos, part, landed, landed, landed]
    aliases = {}
    if gbuf is not None:
        in_specs.append(ANY)
        operands.append(gbuf)
        aliases = {len(operands) - 1: 0}
    return pl.pallas_call(
        body, name=name,
        grid_spec=pltpu.PrefetchScalarGridSpec(
            num_scalar_prefetch=1, grid=(s, rh // tr), in_specs=in_specs,
            out_specs=pl.BlockSpec((None, None, None, tr, cols),
                                   lambda b, i, pos_ref: (li, b, pos_ref[POS_CORE], i, 0))),
        out_shape=jax.ShapeDtypeStruct((n_layers, s, 2, rh, cols), F32),
        input_output_aliases=aliases,
        compiler_params=_params(2))(*operands)


def _adamw_math(w, g, m, v):
    m = ADAM_B1 * m + (1.0 - ADAM_B1) * g
    v = ADAM_B2 * v + (1.0 - ADAM_B2) * (g * g)
    m_hat = m / (1.0 - ADAM_B1 ** ADAM_STEP)
    v_hat = v / (1.0 - ADAM_B2 ** ADAM_STEP)
    delta = -ADAM_LR * (m_hat / (jnp.sqrt(v_hat) + ADAM_EPS) + ADAM_WD * w)
    return delta, m, v


def _adamw(w, g, slot, m, v, name):
    l, rows, cols = w.shape
    tr = _row_block(rows, cols, 8)

    def body(w_ref, g_ref, m_ref, v_ref, go_ref, d_ref, mo_ref, vo_ref):
        g_ = g_ref[...]
        delta, m_, v_ = _adamw_math(w_ref[...], g_, m_ref[...], v_ref[...])
        go_ref[...] = g_
        d_ref[...] = delta
        mo_ref[...] = m_
        vo_ref[...] = v_

    blk = pl.BlockSpec((None, tr, cols), lambda a, i: (a, i, 0))
    gblk = pl.BlockSpec((None, None, tr, cols), lambda a, i: (a, slot, i, 0))
    return pl.pallas_call(
        body, name=name, grid=(l, rows // tr), in_specs=[blk, gblk, blk, blk], out_specs=[blk] * 4,
        out_shape=[jax.ShapeDtypeStruct(w.shape, F32)] * 4,
        compiler_params=_params(2))(w, g, m, v)


SQ = ("w_sgu_out", "w_conv_out", "w_pool_out", "w_out", "w_ple_gate")
SMALL = ("g_mix_pre", "w_sgu_s", "b_sgu_s", "g_sgu_v", "b_sgu_v", "b_dw", "g_conv_ln", "b_conv_ln", "s_pool",
         "g_mix_post", "g_ffn_pre", "g_ffn_post")


WHERE = {"w_in": ("in", 0), "w_ffn_in": ("ffn_in", 0), "w_ffn_out": ("ffn_out", 0), "w_ple": ("mix", 0),
         "w_pool": ("mix", 1), "w_dw": ("dw", 0)}
WHERE.update({nm: ("sq", slot) for slot, nm in enumerate(SQ)})


class _LayerWeights:
    def __init__(self, fetch, small, li):
        self.fetch, self.small, self.li, self.cache = fetch, small, li, {}

    def __getitem__(self, nm):
        if nm not in self.cache:
            self.cache[nm] = self._big(nm) if nm in WHERE else self.small[nm][self.li]
        return self.cache[nm]

    def _big(self, nm):
        group, slot = WHERE[nm]
        g = self.fetch(group)
        g = g.reshape(g.shape[:2] + (-1, g.shape[-1]))
        if nm in ("w_in", "w_ffn_in"):
            return g.reshape(N_CHIPS, D_MODEL, -1)
        if nm == "w_ffn_out":
            return g.reshape(D_FF, D_MODEL)
        if nm in SQ:
            return g[:, slot].reshape(D_MODEL, D_MODEL)
        if nm == "w_ple":
            return g[:, slot].transpose(1, 0, 2).reshape(256, D_MODEL)
        if nm == "w_pool":
            return g[:, slot].reshape(N_CHIPS, 4, 64, 256).transpose(1, 0, 2, 3).reshape(4, 256, 256)
        return g.reshape(N_CHIPS, CONV_HALO, -1).transpose(1, 0, 2).reshape(CONV_HALO, D_MODEL)


def _vec(a):
    return a.reshape(1, -1)


def _layer_fwd(h, p, w, li, hosts=None):
    s = {}
    tag = "_l%d" % li
    s["h0"] = h
    proj, hn = _norm_mm(h, _vec(w["g_mix_pre"]), w["w_in"], "mix_in" + tag, _take(hosts, "mix_in"))
    s["proj"], s["hn"] = proj, hn
    bs3 = w["b_sgu_s"].reshape(SGU_HEADS, SGU_BLOCK, 1)
    s["sg"] = _sgu_fwd(proj, w["w_sgu_s"], bs3, _vec(w["g_sgu_v"]), _vec(w["b_sgu_v"]), "sgu_fwd" + tag,
                       _take(hosts, "sgu_fwd"))
    s["cs"], s["cv"] = _conv_fwd(proj, w["w_dw"], _vec(w["b_dw"]), _vec(w["g_conv_ln"]), _vec(w["b_conv_ln"]),
                                 "conv_fwd" + tag, _take(hosts, "conv_fwd"))
    s["ps"] = _pool_fwd(proj, w["w_pool"], _vec(w["s_pool"]), "pool_fwd" + tag, _take(hosts, "pool_fwd"))
    s["bra"], s["brb"], s["brc"], s["merged"] = _merge_fwd(
        proj, s["sg"], s["cs"], s["ps"], w["w_sgu_out"], w["w_conv_out"], w["w_pool_out"], "merge_fwd" + tag,
        _take(hosts, "merge_fwd"))
    s["mo"], h1 = _mm_norm_res(s["merged"], w["w_out"], _vec(w["g_mix_post"]), h, "mix_out" + tag,
                               _take(hosts, "mix_out"))
    s["h1"] = h1
    s["fg"], s["fu"], s["act"], s["hn2"] = _ffn_in(h1, _vec(w["g_ffn_pre"]), w["w_ffn_in"], "ffn_in" + tag,
                                                   _take(hosts, "ffn_in"))
    s["f"], h2 = _mm_norm_res(s["act"], w["w_ffn_out"], _vec(w["g_ffn_post"]), h1, "ffn_out" + tag,
                              _take(hosts, "ffn_out"))
    s["h2"] = h2
    h3, s["q"], s["e"] = _ple_fwd(h2, p, w["w_ple_gate"], w["w_ple"], "ple_fwd" + tag, _take(hosts, "ple_fwd"))
    return h3, s


def _layer_bwd(dh3, p, w, s, li, hosts=None, big=None, gs=None):
    tag = "_l%d" % li
    d = D_MODEL
    gs = {} if gs is None else gs
    big = {} if big is None else big
    dh2, sq, dw_ple = _ple_bwd(dh3, s["q"], s["e"], w["w_ple_gate"], s["h2"], p, SQ.index("w_ple_gate"), len(SQ),
                               "ple_bwd" + tag)
    dff, gs["g_ffn_post"], dw_ffn_out = _ffn_out_bwd(
        dh2, s["f"], _vec(w["g_ffn_post"]), s["fg"], s["fu"], s["act"], w["w_ffn_out"], "ffn_out_bwd" + tag,
        _take(hosts, "ffn_out_bwd"))
    big["ffn_out"] = dw_ffn_out.reshape(N_CHIPS, 1, D_FF // N_CHIPS, d)
    n_ff = w["w_ffn_in"].shape[2]
    dh1, gs["g_ffn_pre"] = _in_bwd([(dff, 2 * D_FF // n_ff)], w["w_ffn_in"], n_ff, s["h1"], _vec(w["g_ffn_pre"]),
                                   dh2, ROW_TILE, "ffn_in_bwd" + tag, _take(hosts, "ffn_in_bwd"))
    big["ffn_in"] = _dw_cols(s["hn2"], [(dff, 2 * D_FF // n_ff)], n_ff, 1, "dw_ffn_in" + tag)
    branch_w = ("w_sgu_out", "w_conv_out", "w_pool_out")
    dzg, dsg, dcs, dps, gs["g_mix_post"], big["sq"] = _merge_bwd(
        dh1, s["mo"], _vec(w["g_mix_post"]), s["proj"], (s["bra"], s["brb"], s["brc"]), (s["sg"], s["cs"], s["ps"]),
        s["merged"], w["w_out"], [w[nm] for nm in branch_w], sq, [SQ.index(nm) for nm in ("w_out",) + branch_w],
        "merge_bwd" + tag, _take(hosts, "merge_bwd"))
    bs3 = w["b_sgu_s"].reshape(SGU_HEADS, SGU_BLOCK, 1)
    dz_sgu, gs["w_sgu_s"], dbs3, gs["g_sgu_v"], gs["b_sgu_v"] = _sgu_bwd(
        s["proj"], dsg, w["w_sgu_s"], bs3, _vec(w["g_sgu_v"]), _vec(w["b_sgu_v"]), "sgu_bwd" + tag,
        _take(hosts, "sgu_bwd"))
    gs["b_sgu_s"] = dbs3
    dcv, dwdw, gs["b_dw"], gs["g_conv_ln"], gs["b_conv_ln"] = _conv_bwd_norm(
        s["proj"], dcs, s["cv"], _vec(w["b_dw"]), _vec(w["g_conv_ln"]), _vec(w["b_conv_ln"]), "conv_bwd_norm" + tag,
        _take(hosts, "conv_bwd_norm"))
    dz_conv = _conv_bwd_taps(s["proj"], dcv, w["w_dw"], "conv_bwd_taps" + tag, _take(hosts, "conv_bwd_taps"))
    dz_pool, dwpool, gs["s_pool"] = _pool_bwd(s["proj"], dps, w["w_pool"], _vec(w["s_pool"]), "pool_bwd" + tag)
    pieces = [(dz_sgu, 2), (dz_conv, 2), (dz_pool, 1), (dzg, 3)]
    big["in"] = _dw_cols(s["hn"], pieces, d, 2, "dw_in" + tag)
    gple = dw_ple.reshape(256, N_CHIPS, 256).transpose(1, 0, 2)
    gpool = dwpool.reshape(4, N_CHIPS, 64, 256).transpose(1, 0, 2, 3).reshape(N_CHIPS, 256, 256)
    big["mix"] = jnp.stack([gple, gpool], axis=1)
    big["dw"] = dwdw.reshape(CONV_HALO, N_CHIPS, 256).transpose(1, 0, 2)[:, None]
    dh0, gs["g_mix_pre"] = _in_bwd(pieces, w["w_in"], d, s["h0"], _vec(w["g_mix_pre"]), dh1, ROW_TILE,
                                   "mix_in_bwd" + tag, _take(hosts, "mix_in_bwd"))
    return dh0, big, gs


GROUPS = ("in", "sq", "ffn_in", "ffn_out", "mix", "dw")
WIRE_DTYPE = {"in": BF16, "sq": BF16, "ffn_in": BF16, "ffn_out": BF16, "mix": BF16, "dw": F32}
GATHER_FIRST = ("in", "mix", "dw")
GATHER_RIDES = (("mix_in", "sgu_fwd", ("sq", "ffn_in"), ()),
                ("conv_fwd", "pool_fwd", ("ffn_out",), ("in",)),
                ("merge_fwd", "mix_out", (), ("sq",)),
                ("ffn_in", "ffn_out", (), ("ffn_in", "ffn_out", "mix", "dw")))
REDUCE_UPPER = ("ffn_out_bwd", (("ffn_in_bwd", ("in", "ffn_out")), ("merge_bwd", ("sq", "ffn_in", "mix", "dw"))))
REDUCE_OWN = ("sgu_bwd", (("conv_bwd_norm", ("ffn_in", "ffn_out")), ("conv_bwd_taps", ("sq",))))
REDUCE_LAST = ("in", "mix", "dw")


def _group_members(wts):
    n_layers = wts["w_in"].shape[0]
    dw = wts["w_dw"].reshape(n_layers, CONV_WIDTH, -1)
    return {"in": [wts["w_in"]], "sq": [wts[nm] for nm in SQ], "ffn_in": [wts["w_ffn_in"]],
            "ffn_out": [wts["w_ffn_out"]],
            "mix": [wts["w_ple"], wts["w_pool"].reshape(n_layers, POOL_GROUP, POOL_GROUP)],
            "dw": [jnp.pad(dw, ((0, 0), (0, CONV_HALO - CONV_WIDTH), (0, 0)))]}


class _Gather:
    PLACED, OVER_ICI, FULL = 0, 1, 2

    def __init__(self):
        self.buf, self.stage, self.pending = {}, {}, []

    def put(self, key, buf):
        self.buf[key], self.stage[key] = buf, self.PLACED

    def _flush(self):
        for keys, pay, stage in self.pending:
            if pay.results is not None:
                for key, res in zip(keys, pay.results):
                    self.buf[key], self.stage[key] = res, stage
        self.pending = [entry for entry in self.pending if entry[1].results is None]

    def _factory(self, make, keys, before, after):
        def factory():
            if not keys:
                return None
            self._flush()
            assert all(self.stage[k] == before for k in keys), (keys, self.stage)
            pay = make([self.buf[k] for k in keys])
            self.pending.append((keys, pay, after))
            return pay
        return factory

    def ici(self, keys):
        return self._factory(_gather_ici, keys, self.PLACED, self.OVER_ICI)

    def d2d(self, keys):
        return self._factory(_gather_d2d, keys, self.OVER_ICI, self.FULL)

    def get(self, li, group):
        self._flush()
        assert self.stage[(li, group)] == self.FULL, (li, group)
        return self.buf[(li, group)]


class _Reduce:
    def __init__(self, pos, n_layers):
        self.pos, self.n_layers, self.exchanged, self.stages = pos, n_layers, [], []

    def exchange(self, li, groups, grads):
        def factory():
            pay = _pair_exchange([_half_view(grads[g]) for g in groups])
            self.exchanged.append((li, list(groups), pay))
            return pay
        return factory

    def _received(self, li, group):
        for lj, groups, pay in self.exchanged:
            if lj == li and group in groups:
                return pay.results[groups.index(group)]
        raise KeyError((li, group))

    def chips(self, li, groups, grads):
        def factory():
            parts = [_pair_sum(_half_view(grads[g]), self._received(li, g), self.pos, WIRE_DTYPE[g],
                               "pair_sum_%s_l%d" % (g, li)) for g in groups]
            pay = _chip_exchange(parts)
            self.stages.append((li, groups, parts, pay))
            return pay
        return factory

    def finish(self):
        reduced = {}
        for li, groups, parts, pay in self.stages:
            for g, part, landed in zip(groups, parts, pay.results):
                reduced[g] = _chip_sum(part, landed, reduced.get(g), li, self.n_layers, self.pos,
                                       "chip_sum_%s_l%d" % (g, li))
        return reduced


def _pack_small(tree):
    flat = jnp.concatenate([tree[nm].reshape(-1).astype(F32) for nm in SMALL])
    return flat.reshape(-1, 128)


def _unpack_small(packed, like):
    out, off = {}, 0
    flat = packed.reshape(-1)
    for nm in SMALL:
        n = like[nm].size
        out[nm] = flat[off:off + n].reshape(like[nm].shape)
        off += n
    return out


WEIGHTS = ("g_mix_pre", "w_in", "w_sgu_s", "b_sgu_s", "g_sgu_v", "b_sgu_v", "w_sgu_out", "w_dw", "b_dw", "g_conv_ln",
           "b_conv_ln", "w_conv_out", "w_pool", "s_pool", "w_pool_out", "w_out", "g_mix_post", "g_ffn_pre",
           "w_ffn_in", "w_ffn_out", "g_ffn_post", "w_ple", "w_ple_gate")


def kernel(x, p, g_mix_pre, w_in, w_sgu_s, b_sgu_s, g_sgu_v, b_sgu_v, w_sgu_out, w_dw, b_dw, g_conv_ln, b_conv_ln, w_conv_out, w_pool, s_pool, w_pool_out, w_out, g_mix_post, g_ffn_pre, w_ffn_in, w_ffn_out, g_ffn_post, w_ple, w_ple_gate, loss_target, m_g_mix_pre, m_w_in, m_w_sgu_s, m_b_sgu_s, m_g_sgu_v, m_b_sgu_v, m_w_sgu_out, m_w_dw, m_b_dw, m_g_conv_ln, m_b_conv_ln, m_w_conv_out, m_w_pool, m_s_pool, m_w_pool_out, m_w_out, m_g_mix_post, m_g_ffn_pre, m_w_ffn_in, m_w_ffn_out, m_g_ffn_post, m_w_ple, m_w_ple_gate, v_g_mix_pre, v_w_in, v_w_sgu_s, v_b_sgu_s, v_g_sgu_v, v_b_sgu_v, v_w_sgu_out, v_w_dw, v_b_dw, v_g_conv_ln, v_b_conv_ln, v_w_conv_out, v_w_pool, v_s_pool, v_w_pool_out, v_w_out, v_g_mix_post, v_g_ffn_pre, v_w_ffn_in, v_w_ffn_out, v_g_ffn_post, v_w_ple, v_w_ple_gate):
    args = dict(locals())
    wts = {nm: args[nm] for nm in WEIGHTS}
    mom = {nm: args["m_" + nm] for nm in WEIGHTS}
    var = {nm: args["v_" + nm] for nm in WEIGHTS}
    n_layers = w_in.shape[0]
    h = x.reshape(x.shape[1:])
    target = loss_target.reshape(loss_target.shape[1:])
    cx, cy, core = lax.axis_index("x"), lax.axis_index("y"), lax.axis_index("c")
    pos = jnp.stack([2 * cx + cy, 2 * (1 - cx) + cy, 2 * cx + (1 - cy), 2 * (1 - cx) + (1 - cy), core,
                     4 * cx + 2 * cy + core])
    pos = pos.astype(jnp.int32)

    members = _group_members(wts)
    gather = _Gather()
    for li in range(n_layers):
        for g in GROUPS:
            gather.put((li, g), _place(members[g], li, pos, WIRE_DTYPE[g], "place_%s_l%d" % (g, li)))
    first = [(0, g) for g in GATHER_FIRST]
    _run_payload(gather.ici(first)(), "gather_ici_first")
    _run_payload(gather.d2d(first)(), "gather_d2d_first")

    saved, layer_w = [], []
    for li in range(n_layers):
        hosts = {}
        for ici_host, d2d_host, own, nxt in GATHER_RIDES:
            keys = [(li, g) for g in own if li == 0] + [(li + 1, g) for g in nxt if li + 1 < n_layers]
            hosts[ici_host], hosts[d2d_host] = gather.ici(keys), gather.d2d(keys)
        w = _LayerWeights(functools.partial(gather.get, li), wts, li)
        layer_w.append(w)
        h, s = _layer_fwd(h, p[li, 0], w, li, hosts)
        saved.append(s)
    dh, sq_err = _loss_head(h, target, "loss_head")
    loss = lax.psum(sq_err[0, 0] * (0.5 / D_MODEL), ("x", "y", "c"))

    reduce = _Reduce(pos, n_layers)
    small_grads = [{} for _ in range(n_layers)]
    late = (0, SMALL[0])
    small = {}

    def small_vec():
        def leaf(li, nm):
            shape = wts[nm].shape[1:]
            return jnp.zeros(shape, F32) if (li, nm) == late else small_grads[li][nm].reshape(shape)
        return _pack_small({nm: jnp.stack([leaf(li, nm) for li in range(n_layers)], axis=0) for nm in SMALL})

    upper = None
    for li in reversed(range(n_layers)):
        own = {}
        hosts = {}
        plans = [(REDUCE_UPPER, li + 1, upper)] if upper is not None else []
        if li == 0:
            plans.append((REDUCE_OWN, 0, own))

            def last_rides(own=own):
                _run_payload(reduce.exchange(0, REDUCE_LAST, own)(), "pair_exchange_last")
                small["vec"] = small_vec()
                small["exchange"] = _small_exchange(small["vec"])
                return _join(reduce.chips(0, REDUCE_LAST, own)(), small["exchange"])
            hosts["mix_in_bwd"] = last_rides
        for (pair_host, chip_hosts), lj, grads in plans:
            groups = [g for _, gs_ in chip_hosts for g in gs_]
            hosts[pair_host] = reduce.exchange(lj, groups, grads)
            for chip_host, gs_ in chip_hosts:
                hosts[chip_host] = reduce.chips(lj, gs_, grads)
        dh, upper, _ = _layer_bwd(dh, p[li, 0], layer_w[li], saved[li], li, hosts, own, small_grads[li])
    grad_x = dh[None]
    reduced = reduce.finish()

    shared = _run_payload(_pair_share([reduced[g] for g in GROUPS]), "pair_share")
    red = {g: b.reshape(b.shape[:2] + (-1, b.shape[-1])) for g, b in zip(GROUPS, shared)}

    where = {"w_in": ("in", 0), "w_ffn_in": ("ffn_in", 0), "w_ffn_out": ("ffn_out", 0), "w_ple": ("mix", 0),
             "w_pool": ("mix", 1)}
    for slot, nm in enumerate(SQ):
        where[nm] = ("sq", slot)
    outs = {}
    for nm, (g, slot) in where.items():
        shape = wts[nm].shape
        to3 = lambda a: a.reshape((n_layers,) + red[g].shape[2:])
        res = _adamw(to3(wts[nm]), red[g], slot, to3(mom[nm]), to3(var[nm]), "adamw_" + nm)
        outs[nm] = [r.reshape(shape) for r in res]
    gdw = red["dw"][:, :, :CONV_WIDTH]
    to3 = lambda a: a.reshape(n_layers, CONV_WIDTH, -1)
    res = _adamw(to3(wts["w_dw"]), gdw, 0, to3(mom["w_dw"]), to3(var["w_dw"]), "adamw_w_dw")
    outs["w_dw"] = [r.reshape(wts["w_dw"].shape) for r in res]

    gmain = _small_sum(small["vec"], small["exchange"].results[0], pos, "small_sum")
    glate = _all_reduce_small(small_grads[late[0]][late[1]].reshape(-1, 128), "all_reduce_late")
    gsmall = jnp.concatenate([glate, gmain[glate.shape[0]:]], axis=0)
    pk = lambda tree: _pack_small({nm: tree[nm] for nm in SMALL})[None]
    res = _adamw(pk(wts), gsmall[None, None], 0, pk(mom), pk(var), "adamw_small")
    unpacked = [_unpack_small(r[0], wts) for r in res]
    for nm in SMALL:
        outs[nm] = [u[nm] for u in unpacked]

    result = [loss, grad_x]
    for k in range(4):
        result += [outs[nm][k] for nm in WEIGHTS]
    return tuple(result)
```

```python
import functools

import jax
import jax.numpy as jnp
from jax import lax
from jax.experimental import pallas as pl
from jax.experimental.pallas import tpu as pltpu

F32 = jnp.float32
BF16 = jnp.bfloat16
MESH = pl.DeviceIdType.MESH

EPS = 1e-6
D_MODEL = 1024
SGU_BLOCK = 128
SGU_HEADS = 8
CHUNK = 64
CONV_WIDTH = 31
CONV_HALO = 32
POOL_WINDOWS = (2, 4, 8, 16)
POOL_BLOCK = 128
POOL_GROUP = 256
D_FF = 2816
N_CHIPS = 4

ADAM_LR = 0.001
ADAM_B1 = 0.9
ADAM_B2 = 0.999
ADAM_EPS = 1e-08
ADAM_WD = 0.01
ADAM_STEP = 10

VMEM_LIMIT = 52 * 1024 * 1024
ROW_TILE_LIGHT = 1024
ROW_TILE = 512
ROW_TILE_HEAVY = 256
CONV_ROWS = 128
CONV_LANES = 128
EW_BLOCK_BYTES = 2 * 1024 * 1024
TOKEN_TILE = 2048
FF_CHUNK = 256


def _params(n_grid):
    return pltpu.CompilerParams(dimension_semantics=("arbitrary",) * n_grid, vmem_limit_bytes=VMEM_LIMIT)


def _dot(a, b):
    return jnp.dot(a.astype(BF16), b.astype(BF16), preferred_element_type=F32)


def _dot_nt(a, b):
    return lax.dot_general(a.astype(BF16), b.astype(BF16), (((1,), (1,)), ((), ())), preferred_element_type=F32)


def _dot_tn(a, b):
    return lax.dot_general(a.astype(BF16), b.astype(BF16), (((0,), (0,)), ((), ())), preferred_element_type=F32)


def _sigmoid(x):
    return 0.5 * jnp.tanh(0.5 * x) + 0.5


_GELU_C = 0.7978845608028654
_GELU_A = 0.044715


def _gelu(x):
    t = jnp.tanh(_GELU_C * (x + _GELU_A * x * x * x))
    return 0.5 * x * (1.0 + t)


def _gelu_and_grad(x):
    x2 = x * x
    t = jnp.tanh(_GELU_C * (x + _GELU_A * x2 * x))
    g = 0.5 * (1.0 + t) + 0.5 * x * (1.0 - t * t) * (_GELU_C * (1.0 + 3.0 * _GELU_A * x2))
    return 0.5 * x * (1.0 + t), g


def _rms_stats(x):
    r = lax.rsqrt(jnp.mean(x * x, axis=-1, keepdims=True) + EPS)
    return x * r, r


def _rms_bwd(xn, r, g, dy):
    gd = dy * g
    return r * (gd - xn * jnp.mean(gd * xn, axis=-1, keepdims=True)), dy * xn


def _ln_stats(x):
    mu = jnp.mean(x, axis=-1, keepdims=True)
    xc = x - mu
    rstd = lax.rsqrt(jnp.mean(xc * xc, axis=-1, keepdims=True) + EPS)
    return xc * rstd, rstd


def _ln_bwd(xhat, rstd, g, dy):
    dxh = dy * g
    return rstd * (dxh - jnp.mean(dxh, axis=-1, keepdims=True) - xhat * jnp.mean(dxh * xhat, axis=-1, keepdims=True))


def _rowsum(x):
    return jnp.sum(x, axis=0, keepdims=True)


def _tile(t, want):
    return min(t, want)


def _full(shape):
    n = len(shape)
    return pl.BlockSpec(shape, lambda *_: (0,) * n)


def _resident(shape):
    n = len(shape)
    return pl.BlockSpec(shape, lambda *_: (0,) * n, pipeline_mode=pl.Buffered(1))


class _Payload:
    def __init__(self, operands, out_shape, aliases, scratch, start, finish):
        self.operands, self.out_shape, self.aliases, self.scratch = list(operands), list(out_shape), aliases, scratch
        self.start, self.finish = start, finish
        self.results = None
        self.parts = None

    def deliver(self, results):
        self.results = list(results)
        if self.parts:
            a, b, ma = self.parts
            a.deliver(self.results[:ma])
            b.deliver(self.results[ma:])


def _pcall(body, *, name, grid, in_specs, out_specs, out_shape, operands, scratch_shapes=(), comm=None, aliases=None):
    single = not isinstance(out_shape, (list, tuple))
    out_specs = [out_specs] if single else list(out_specs)
    out_shape = [out_shape] if single else list(out_shape)
    aliases = dict(aliases or {})
    if comm is None:
        res = pl.pallas_call(
            body, name=name, grid=grid, in_specs=list(in_specs), out_specs=out_specs, out_shape=out_shape,
            scratch_shapes=list(scratch_shapes), input_output_aliases=aliases,
            compiler_params=_params(len(grid)))(*operands)
        return res[0] if single else res
    n_in, n_out, n_scr = len(in_specs), len(out_shape), len(scratch_shapes)
    ci, co = len(comm.operands), len(comm.out_shape)

    def hosted(*refs):
        bounds = [0, n_in, n_in + ci, n_in + ci + n_out, n_in + ci + n_out + co, n_in + ci + n_out + co + n_scr]
        a, b, c_, d_, s_ = [refs[lo:hi] for lo, hi in zip(bounds[:-1], bounds[1:])]
        t_ = refs[bounds[-1]:]
        ids = [pl.program_id(q) for q in range(len(grid))]
        first = functools.reduce(jnp.logical_and, [i == 0 for i in ids])
        last = functools.reduce(jnp.logical_and, [i == pl.num_programs(q) - 1 for q, i in enumerate(ids)])

        @pl.when(first)
        def _():
            comm.start(b, d_, t_)

        body(*a, *c_, *s_)

        @pl.when(last)
        def _():
            comm.finish(b, d_, t_)

    res = pl.pallas_call(
        hosted, name=name, grid=grid, in_specs=list(in_specs) + [ANY] * ci, out_specs=out_specs + [ANY] * co,
        out_shape=out_shape + comm.out_shape, scratch_shapes=list(scratch_shapes) + list(comm.scratch),
        input_output_aliases={**aliases, **{n_in + i: n_out + o for i, o in comm.aliases.items()}},
        compiler_params=pltpu.CompilerParams(dimension_semantics=("arbitrary",) * len(grid),
                                             vmem_limit_bytes=VMEM_LIMIT, has_side_effects=True),
    )(*operands, *comm.operands)
    comm.deliver(res[n_out:])
    res = res[:n_out]
    return res[0] if single else res


def _run_payload(comm, name):
    ci, co = len(comm.operands), len(comm.out_shape)

    def body(*refs):
        b, d_, t_ = refs[:ci], refs[ci:ci + co], refs[ci + co:]
        comm.start(b, d_, t_)
        comm.finish(b, d_, t_)

    res = pl.pallas_call(
        body, name=name, in_specs=[ANY] * ci, out_specs=[ANY] * co, out_shape=comm.out_shape,
        input_output_aliases=dict(comm.aliases), scratch_shapes=list(comm.scratch),
        compiler_params=pltpu.CompilerParams(has_side_effects=True))(*comm.operands)
    comm.deliver(res)
    return comm.results


def _take(hosts, key):
    return hosts[key]() if hosts and key in hosts else None


def _norm_mm(h, g, w4, name, comm=None):
    t, d = h.shape
    n = w4.shape[2]
    tm = _tile(t, ROW_TILE)

    step = _lane_block(n, 1024)

    def body(h_ref, g_ref, w_ref, o_ref, hn_ref):
        xn, _ = _rms_stats(h_ref[...])
        hn = (xn * g_ref[...]).astype(BF16)
        hn_ref[...] = hn
        for j in range(N_CHIPS):
            for c0 in range(0, n, step):
                o_ref[:, j * n + c0:j * n + c0 + step] = jnp.dot(
                    hn, w_ref[j, :, c0:c0 + step], preferred_element_type=F32).astype(BF16)

    return _pcall(
        body, name=name, grid=(t // tm,),
        in_specs=[pl.BlockSpec((tm, d), lambda i: (i, 0)), _full((1, d)), _resident(w4.shape)],
        out_specs=[pl.BlockSpec((tm, N_CHIPS * n), lambda i: (i, 0)), pl.BlockSpec((tm, d), lambda i: (i, 0))],
        out_shape=[jax.ShapeDtypeStruct((t, N_CHIPS * n), BF16), jax.ShapeDtypeStruct((t, d), BF16)],
        operands=(h, g, w4), comm=comm)


def _sgu_mask():
    ii = lax.broadcasted_iota(jnp.int32, (SGU_BLOCK, SGU_BLOCK), 0) // CHUNK
    jj = lax.broadcasted_iota(jnp.int32, (SGU_BLOCK, SGU_BLOCK), 1) // CHUNK
    return jj <= ii


def _sgu_fwd(proj, wm, bs3, gv, bv, name, comm=None):
    t = proj.shape[0]
    d = D_MODEL
    tm = _tile(t, ROW_TILE)
    hd = d // SGU_HEADS

    def body(zu_ref, zv_ref, wm_ref, bs_ref, gv_ref, bv_ref, o_ref):
        mask = _sgu_mask()
        for blk in range(tm // SGU_BLOCK):
            rows = pl.ds(blk * SGU_BLOCK, SGU_BLOCK)
            u = _gelu(zu_ref[rows, :].astype(F32))
            xhat, _ = _ln_stats(_gelu(zv_ref[rows, :].astype(F32)))
            vn = (xhat * gv_ref[...] + bv_ref[...]).astype(BF16)
            for hh in range(SGU_HEADS):
                cols = slice(hh * hd, (hh + 1) * hd)
                wmh = jnp.where(mask, wm_ref[hh], 0.0).astype(BF16)
                mixed = jnp.dot(wmh, vn[:, cols], preferred_element_type=F32) + bs_ref[hh]
                o_ref[rows, cols] = (u[:, cols] * mixed).astype(BF16)

    return _pcall(
        body, name=name, grid=(t // tm,),
        in_specs=[pl.BlockSpec((tm, d), lambda i: (i, 0)), pl.BlockSpec((tm, d), lambda i: (i, 1)),
                  _full(wm.shape), _full(bs3.shape), _full(gv.shape), _full(bv.shape)],
        out_specs=pl.BlockSpec((tm, d), lambda i: (i, 0)),
        out_shape=jax.ShapeDtypeStruct((t, d), BF16),
        operands=(proj, proj, wm, bs3, gv, bv), comm=comm)


def _conv_taps(scr_ref, r0, c0, base, weight):
    n = CONV_ROWS + CONV_HALO
    win = scr_ref[pl.ds(r0, n), pl.ds(c0, CONV_LANES)]
    acc = None
    for r in range(8):
        rolled = win if r == 0 else pltpu.roll(win, n - r, 0)
        for q in range((CONV_HALO + 7) // 8 + 1):
            k = 8 * q + r - base
            if 0 <= k < CONV_WIDTH and 8 * q + CONV_ROWS <= n:
                term = weight(k) * rolled[8 * q:8 * q + CONV_ROWS]
                acc = term if acc is None else acc + term
    return acc


def _glu_rows(a_ref, g_ref):
    return a_ref[...].astype(F32) * _sigmoid(g_ref[...].astype(F32))


def _conv_into(scr_ref, cv_ref, w_ref, tm, base, flip):
    def chunk(ci, carry):
        r0 = pl.multiple_of(ci * CONV_ROWS, CONV_ROWS)
        for c0 in range(0, D_MODEL, CONV_LANES):
            def weight(k, c0=c0):
                kk = CONV_WIDTH - 1 - k if flip else k
                return w_ref[kk:kk + 1, c0:c0 + CONV_LANES]
            cv_ref[pl.ds(r0, CONV_ROWS), pl.ds(c0, CONV_LANES)] = _conv_taps(scr_ref, r0, c0, base, weight)
        return carry

    lax.fori_loop(0, tm // CONV_ROWS, chunk, 0)


def _conv_specs(t, tm, d):
    hb = tm // CONV_HALO
    main = [pl.BlockSpec((tm, d), lambda i: (i, 2)), pl.BlockSpec((tm, d), lambda i: (i, 3))]
    halo = [pl.BlockSpec((CONV_HALO, d), lambda i: (jnp.maximum(i * hb - 1, 0), 2)),
            pl.BlockSpec((CONV_HALO, d), lambda i: (jnp.maximum(i * hb - 1, 0), 3))]
    return main, halo


def _fill_glu_history(scr_ref, a_ref, g_ref, ah_ref, gh_ref, tm):
    hist = _glu_rows(ah_ref, gh_ref)
    scr_ref[0:CONV_HALO, :] = jnp.where(pl.program_id(0) > 0, hist, 0.0)
    scr_ref[CONV_HALO:CONV_HALO + tm, :] = _glu_rows(a_ref, g_ref)


_CONV_BASE = CONV_HALO - (CONV_WIDTH - 1)


def _conv_fwd(proj, wdw, bdw, gln, bln, name, comm=None):
    t = proj.shape[0]
    d = D_MODEL
    tm = _tile(t, ROW_TILE)
    main, halo = _conv_specs(t, tm, d)

    def body(a_ref, g_ref, ah_ref, gh_ref, w_ref, b_ref, gl_ref, bl_ref, o_ref, cv_ref, scr_ref):
        _fill_glu_history(scr_ref, a_ref, g_ref, ah_ref, gh_ref, tm)
        _conv_into(scr_ref, cv_ref, w_ref, tm, _CONV_BASE, False)
        xhat, _ = _ln_stats(cv_ref[...] + b_ref[...])
        cn = xhat * gl_ref[...] + bl_ref[...]
        o_ref[...] = (cn * _sigmoid(cn)).astype(BF16)

    row = pl.BlockSpec((tm, d), lambda i: (i, 0))
    return _pcall(
        body, name=name, grid=(t // tm,),
        in_specs=main + halo + [_full(wdw.shape), _full(bdw.shape), _full(gln.shape), _full(bln.shape)],
        out_specs=[row, row],
        out_shape=[jax.ShapeDtypeStruct((t, d), BF16), jax.ShapeDtypeStruct((t, d), F32)],
        scratch_shapes=[pltpu.VMEM((tm + CONV_HALO, d), F32)],
        operands=(proj, proj, proj, proj, wdw, bdw, gln, bln), comm=comm)


def _pool_fill(scr_ref, z_ref, zh_ref, tm):
    scr_ref[0:POOL_BLOCK, :] = jnp.where(pl.program_id(0) > 0, zh_ref[...], jnp.zeros_like(zh_ref))
    scr_ref[POOL_BLOCK:POOL_BLOCK + tm, :] = z_ref[...]


def _pool_count(t0, rows, w):
    pos = (t0 + lax.broadcasted_iota(jnp.int32, (rows, 1), 0) + 1).astype(F32)
    return jnp.minimum(pos, float(w))


def _band(w, leading):
    i = lax.broadcasted_iota(jnp.int32, (POOL_BLOCK, 2 * POOL_BLOCK), 0)
    j = lax.broadcasted_iota(jnp.int32, (POOL_BLOCK, 2 * POOL_BLOCK), 1)
    off = j - i if leading else POOL_BLOCK + i - j
    return jnp.where((off >= 0) & (off < w), 1.0, 0.0).astype(BF16)


def _window_sums(refs, band, tm, cols):
    blocks = []
    for b in range(tm // POOL_BLOCK):
        rows = pl.ds(b * POOL_BLOCK, 2 * POOL_BLOCK)
        acc = None
        for ref in refs:
            term = jnp.dot(band, ref[rows, cols], preferred_element_type=F32)
            acc = term if acc is None else acc + term
        blocks.append(acc)
    return blocks[0] if len(blocks) == 1 else jnp.concatenate(blocks, axis=0)


def _pooled_group(scr_ref, gi, w, tm, t0):
    cols = pl.ds(gi * POOL_GROUP, POOL_GROUP)
    sums = _window_sums([scr_ref], _band(w, False), tm, cols)
    return sums / _pool_count(t0, tm, w) - scr_ref[pl.ds(POOL_BLOCK, tm), cols].astype(F32)


def _pool_specs(tm, d):
    hb = tm // POOL_BLOCK
    return [pl.BlockSpec((tm, d), lambda i: (i, 4)),
            pl.BlockSpec((POOL_BLOCK, d), lambda i: (jnp.maximum(i * hb - 1, 0), 4))]


def _pool_fwd(proj, wpool, spool, name, comm=None):
    t = proj.shape[0]
    d = D_MODEL
    tm = _tile(t, ROW_TILE)

    def body(z_ref, zh_ref, w_ref, s_ref, o_ref, scr_ref):
        _pool_fill(scr_ref, z_ref, zh_ref, tm)
        t0 = pl.program_id(0) * tm
        for gi, w in enumerate(POOL_WINDOWS):
            cols = slice(gi * POOL_GROUP, (gi + 1) * POOL_GROUP)
            pooled = _pooled_group(scr_ref, gi, w, tm, t0)
            o_ref[:, cols] = (_dot(pooled, w_ref[gi]) * s_ref[:, cols]).astype(BF16)

    return _pcall(
        body, name=name, grid=(t // tm,),
        in_specs=_pool_specs(tm, d) + [_full(wpool.shape), _full(spool.shape)],
        out_specs=pl.BlockSpec((tm, d), lambda i: (i, 0)),
        out_shape=jax.ShapeDtypeStruct((t, d), BF16),
        scratch_shapes=[pltpu.VMEM((tm + POOL_BLOCK, d), BF16)],
        operands=(proj, proj, wpool, spool), comm=comm)


def _merge_fwd(proj, sg, cs, ps, wa, wb, wc, name, comm=None):
    t = proj.shape[0]
    d = D_MODEL
    tm = _tile(t, ROW_TILE_HEAVY)

    def body(za_ref, zb_ref, zc_ref, sg_ref, cs_ref, ps_ref, wa_ref, wb_ref, wc_ref, ba_ref, bb_ref, bc_ref, m_ref):
        merged = None
        for z_ref, x_ref, w_ref, b_ref in ((za_ref, sg_ref, wa_ref, ba_ref), (zb_ref, cs_ref, wb_ref, bb_ref),
                                           (zc_ref, ps_ref, wc_ref, bc_ref)):
            br = jnp.dot(x_ref[...], w_ref[...], preferred_element_type=F32)
            b_ref[...] = br.astype(BF16)
            term = _sigmoid(z_ref[...].astype(F32)) * br
            merged = term if merged is None else merged + term
        m_ref[...] = merged.astype(BF16)

    row = pl.BlockSpec((tm, d), lambda i: (i, 0))
    wspec = _resident((d, d))
    return _pcall(
        body, name=name, grid=(t // tm,),
        in_specs=[pl.BlockSpec((tm, d), lambda i: (i, 5)), pl.BlockSpec((tm, d), lambda i: (i, 6)),
                  pl.BlockSpec((tm, d), lambda i: (i, 7)), row, row, row, wspec, wspec, wspec],
        out_specs=[row, row, row, row],
        out_shape=[jax.ShapeDtypeStruct((t, d), BF16)] * 4,
        operands=(proj, proj, proj, sg, cs, ps, wa, wb, wc), comm=comm)


def _mm_norm_res(a, w, g, hres, name, comm=None):
    t, k = a.shape
    d = w.shape[1]
    tm = _tile(t, ROW_TILE_LIGHT)

    def body(a_ref, w_ref, g_ref, h_ref, y_ref, o_ref):
        y = jnp.dot(a_ref[...], w_ref[...], preferred_element_type=F32)
        y_ref[...] = y
        yn, _ = _rms_stats(y)
        o_ref[...] = h_ref[...] + yn * g_ref[...]

    row = pl.BlockSpec((tm, d), lambda i: (i, 0))
    return _pcall(
        body, name=name, grid=(t // tm,),
        in_specs=[pl.BlockSpec((tm, k), lambda i: (i, 0)), _resident(w.shape), _full(g.shape), row],
        out_specs=[row, row],
        out_shape=[jax.ShapeDtypeStruct((t, d), F32)] * 2,
        operands=(a, w, g, hres), comm=comm)


def _ffn_in(h, g, w4, name, comm=None):
    t, d = h.shape
    n = w4.shape[2]
    tm = _tile(t, ROW_TILE)
    nj = D_FF // n

    def body(h_ref, g_ref, w_ref, fg_ref, fu_ref, act_ref, hn_ref):
        xn, _ = _rms_stats(h_ref[...])
        hn = (xn * g_ref[...]).astype(BF16)
        hn_ref[...] = hn
        for j in range(nj):
            cols = slice(j * n, (j + 1) * n)
            fg = jnp.dot(hn, w_ref[j], preferred_element_type=F32)
            fu = jnp.dot(hn, w_ref[j + nj], preferred_element_type=F32)
            fg_ref[:, cols] = fg.astype(BF16)
            fu_ref[:, cols] = fu.astype(BF16)
            act_ref[:, cols] = (fg * _sigmoid(fg) * fu).astype(BF16)

    wide = pl.BlockSpec((tm, D_FF), lambda i: (i, 0))
    return _pcall(
        body, name=name, grid=(t // tm,),
        in_specs=[pl.BlockSpec((tm, d), lambda i: (i, 0)), _full((1, d)), _resident(w4.shape)],
        out_specs=[wide, wide, wide, pl.BlockSpec((tm, d), lambda i: (i, 0))],
        out_shape=[jax.ShapeDtypeStruct((t, D_FF), BF16)] * 3 + [jax.ShapeDtypeStruct((t, d), BF16)],
        operands=(h, g, w4), comm=comm)


def _ple_fwd(h, p, wg, wp, name, comm=None):
    t, d = h.shape
    tm = _tile(t, ROW_TILE_LIGHT)

    def body(h_ref, p_ref, wg_ref, wp_ref, o_ref, q_ref, e_ref):
        hh = h_ref[...]
        q = _dot(hh, wg_ref[...])
        e = _dot(p_ref[...], wp_ref[...])
        q_ref[...] = q.astype(BF16)
        e_ref[...] = e.astype(BF16)
        o_ref[...] = hh + _sigmoid(q) * e

    row = pl.BlockSpec((tm, d), lambda i: (i, 0))
    return _pcall(
        body, name=name, grid=(t // tm,),
        in_specs=[row, pl.BlockSpec((tm, p.shape[1]), lambda i: (i, 0)), _resident(wg.shape), _resident(wp.shape)],
        out_specs=[row, row, row],
        out_shape=[jax.ShapeDtypeStruct((t, d), F32), jax.ShapeDtypeStruct((t, d), BF16),
                   jax.ShapeDtypeStruct((t, d), BF16)],
        operands=(h, p, wg, wp), comm=comm)


def _loss_head(y, target, name):
    t, d = y.shape
    tm = _tile(t, ROW_TILE_LIGHT)

    def body(y_ref, t_ref, dy_ref, l_ref):
        @pl.when(pl.program_id(0) == 0)
        def _():
            l_ref[...] = jnp.zeros_like(l_ref)

        err = y_ref[...] - t_ref[...]
        dy_ref[...] = err * (1.0 / d)
        l_ref[...] += jnp.sum(err * err, keepdims=True)[:, :1] * jnp.ones((1, 128), F32)

    row = pl.BlockSpec((tm, d), lambda i: (i, 0))
    return pl.pallas_call(
        body, name=name, grid=(t // tm,),
        in_specs=[row, row], out_specs=[row, _full((1, 128))],
        out_shape=[jax.ShapeDtypeStruct((t, d), F32), jax.ShapeDtypeStruct((1, 128), F32)],
        compiler_params=_params(1))(y, target)


def _flush_slots(acc_ref, out_ref, slots, sem_ref):
    rows = out_ref.shape[2]

    @pl.when(pl.program_id(0) == pl.num_programs(0) - 1)
    def _():
        copies = [pltpu.make_async_copy(acc_ref.at[k, pl.ds(j * rows, rows)], out_ref.at[j, slot],
                                        sem_ref.at[N_CHIPS * k + j])
                  for k, slot in enumerate(slots) for j in range(N_CHIPS)]
        for cp in copies:
            cp.start()
        for cp in copies:
            cp.wait()


def _ple_bwd(dh, q, e, wg, h_in, p, slot, n_slots, name):
    t, d = dh.shape
    tm = _tile(t, ROW_TILE)
    rows = d // N_CHIPS

    def body(dh_ref, q_ref, e_ref, wg_ref, h_ref, p_ref, o_ref, sq_ref, dwp_ref, acc_ref, sem_ref):
        @pl.when(pl.program_id(0) == 0)
        def _():
            acc_ref[...] = jnp.zeros_like(acc_ref)
            dwp_ref[...] = jnp.zeros_like(dwp_ref)

        dh_ = dh_ref[...]
        s = _sigmoid(q_ref[...].astype(F32))
        dq = (dh_ * e_ref[...].astype(F32) * s * (1.0 - s)).astype(BF16)
        o_ref[...] = dh_ + _dot_nt(dq, wg_ref[...])
        acc_ref[0] += _dot_tn(h_ref[...], dq)
        dwp_ref[...] += _dot_tn(p_ref[...], dh_ * s)
        _flush_slots(acc_ref, sq_ref, (slot,), sem_ref)

    row = pl.BlockSpec((tm, d), lambda i: (i, 0))
    return _pcall(
        body, name=name, grid=(t // tm,),
        in_specs=[row, row, row, _resident(wg.shape), row, pl.BlockSpec((tm, p.shape[1]), lambda i: (i, 0))],
        out_specs=[row, ANY, _full((p.shape[1], d))],
        out_shape=[jax.ShapeDtypeStruct((t, d), F32), jax.ShapeDtypeStruct((N_CHIPS, n_slots, rows, d), F32),
                   jax.ShapeDtypeStruct((p.shape[1], d), F32)],
        scratch_shapes=[pltpu.VMEM((1, d, d), F32), pltpu.SemaphoreType.DMA((N_CHIPS,))],
        operands=(dh, q, e, wg, h_in, p))


def _ffn_out_bwd(dh, f, g, fg, fu, act, w, name, comm=None):
    t, d = dh.shape
    tm = _tile(t, ROW_TILE_HEAVY)

    def body(dh_ref, f_ref, g_ref, fg_ref, fu_ref, act_ref, w_ref, dff_ref, dg_ref, dw_ref, acc_ref, sem_ref):
        @pl.when(pl.program_id(0) == 0)
        def _():
            dg_ref[...] = jnp.zeros_like(dg_ref)
            acc_ref[...] = jnp.zeros_like(acc_ref)

        fn, r = _rms_stats(f_ref[...])
        df, dgt = _rms_bwd(fn, r, g_ref[...], dh_ref[...])
        dg_ref[...] += _rowsum(dgt)
        df = df.astype(BF16)
        acc_ref[...] += _dot_tn(act_ref[...], df)
        for c0 in range(0, D_FF, FF_CHUNK):
            cols = slice(c0, c0 + FF_CHUNK)
            dact = _dot_nt(df, w_ref[cols, :])
            fg_ = fg_ref[:, cols].astype(F32)
            s = _sigmoid(fg_)
            gs = fg_ * s
            dff_ref[:, cols] = (dact * fu_ref[:, cols].astype(F32) * (s + gs - gs * s)).astype(BF16)
            dff_ref[:, D_FF + c0:D_FF + c0 + FF_CHUNK] = (dact * gs).astype(BF16)

        @pl.when(pl.program_id(0) == pl.num_programs(0) - 1)
        def _():
            cp = pltpu.make_async_copy(acc_ref, dw_ref, sem_ref.at[0])
            cp.start()
            cp.wait()

    row = pl.BlockSpec((tm, d), lambda i: (i, 0))
    wide = pl.BlockSpec((tm, D_FF), lambda i: (i, 0))
    return _pcall(
        body, name=name, grid=(t // tm,),
        in_specs=[row, row, _full(g.shape), wide, wide, wide, _resident(w.shape)],
        out_specs=[pl.BlockSpec((tm, 2 * D_FF), lambda i: (i, 0)), _full((1, d)), ANY],
        out_shape=[jax.ShapeDtypeStruct((t, 2 * D_FF), BF16), jax.ShapeDtypeStruct((1, d), F32),
                   jax.ShapeDtypeStruct((D_FF, d), F32)],
        scratch_shapes=[pltpu.VMEM((D_FF, d), F32), pltpu.SemaphoreType.DMA((1,))],
        operands=(dh, f, g, fg, fu, act, w), comm=comm)


def _in_bwd(pieces, w4, unit, h, g, dres, tm, name, comm=None):
    t, d = h.shape
    tm = _tile(t, tm)
    per_chunk = w4.shape[2] // unit
    n_p = len(pieces)

    def body(*refs):
        p_refs = refs[:n_p]
        w_ref, h_ref, g_ref, r_ref, o_ref, dg_ref = refs[n_p:]

        @pl.when(pl.program_id(0) == 0)
        def _():
            dg_ref[...] = jnp.zeros_like(dg_ref)

        acc = None
        u = 0
        for p_ref, (_, nu) in zip(p_refs, pieces):
            for k in range(nu):
                lanes = slice((u % per_chunk) * unit, (u % per_chunk + 1) * unit)
                term = _dot_nt(p_ref[:, k * unit:(k + 1) * unit], w_ref[u // per_chunk, :, lanes])
                acc = term if acc is None else acc + term
                u += 1
        xn, r = _rms_stats(h_ref[...])
        dx, dgt = _rms_bwd(xn, r, g_ref[...], acc)
        dg_ref[...] += _rowsum(dgt)
        o_ref[...] = r_ref[...] + dx

    row = pl.BlockSpec((tm, d), lambda i: (i, 0))
    return _pcall(
        body, name=name, grid=(t // tm,),
        in_specs=[pl.BlockSpec((tm, a.shape[1]), lambda i: (i, 0)) for a, _ in pieces]
        + [_resident(w4.shape), row, _full((1, d)), row],
        out_specs=[row, _full((1, d))],
        out_shape=[jax.ShapeDtypeStruct((t, d), F32), jax.ShapeDtypeStruct((1, d), F32)],
        operands=(*[a for a, _ in pieces], w4, h, g, dres), comm=comm)


def _lane_block(n, cap):
    return max(b for b in range(128, min(n, cap) + 1, 128) if n % b == 0)


def _dw_cols(x, pieces, unit, per_chunk, name):
    t, m = x.shape
    tk = _tile(t, TOKEN_TILE)
    offs, total = [], 0
    for _, nu in pieces:
        offs.append(total)
        total += nu

    def body(x_ref, *refs):
        o_ref = refs[-1]
        u = pl.program_id(0)

        @pl.when(pl.program_id(1) == 0)
        def _():
            o_ref[...] = jnp.zeros_like(o_ref)

        for p_ref, off, (_, nu) in zip(refs[:-1], offs, pieces):
            @pl.when((u >= off) & (u < off + nu))
            def _(p_ref=p_ref):
                o_ref[...] += _dot_tn(x_ref[...], p_ref[...])

    def piece_spec(off, nu):
        def index(u, k):
            mine = (u >= off) & (u < off + nu)
            return jnp.where(mine, k, 0), jnp.clip(u - off, 0, nu - 1)
        return pl.BlockSpec((tk, unit), index)

    return pl.pallas_call(
        body, name=name, grid=(total, t // tk),
        in_specs=[pl.BlockSpec((tk, m), lambda u, k: (k, 0))] + [piece_spec(o, nu) for o, (_, nu) in zip(offs, pieces)],
        out_specs=pl.BlockSpec((None, None, m, unit), lambda u, k: (u // per_chunk, 0, 0, u % per_chunk)),
        out_shape=jax.ShapeDtypeStruct((N_CHIPS, 1, m, per_chunk * unit), F32),
        compiler_params=_params(2))(x, *[a for a, _ in pieces])


def _merge_bwd(dh, mo, g, proj, br, xs, merged, w_out, ws, sq, slots, name, comm=None):
    t, d = dh.shape
    tm = _tile(t, ROW_TILE_HEAVY)
    rows = d // N_CHIPS

    def body(dh_ref, mo_ref, g_ref, za_ref, zb_ref, zc_ref, ba_ref, bb_ref, bc_ref, xa_ref, xb_ref, xc_ref, m_ref,
             wo_ref, wa_ref, wb_ref, wc_ref, sq_in_ref, dz_ref, dsg_ref, dcs_ref, dps_ref, dg_ref, sq_ref,
             acc_ref, sem_ref):
        @pl.when(pl.program_id(0) == 0)
        def _():
            dg_ref[...] = jnp.zeros_like(dg_ref)
            acc_ref[...] = jnp.zeros_like(acc_ref)

        mon, r = _rms_stats(mo_ref[...])
        dmo, dgt = _rms_bwd(mon, r, g_ref[...], dh_ref[...])
        dg_ref[...] += _rowsum(dgt)
        dmo = dmo.astype(BF16)
        acc_ref[0] += _dot_tn(m_ref[...], dmo)
        dmerged = _dot_nt(dmo, wo_ref[...])
        branches = ((za_ref, ba_ref, xa_ref, wa_ref, dsg_ref), (zb_ref, bb_ref, xb_ref, wb_ref, dcs_ref),
                    (zc_ref, bc_ref, xc_ref, wc_ref, dps_ref))
        for j, (z_ref, b_ref, x_ref, w_ref, dx_ref) in enumerate(branches):
            gate = _sigmoid(z_ref[...].astype(F32))
            dbr = (dmerged * gate).astype(BF16)
            dz_ref[:, j * d:(j + 1) * d] = (dmerged * b_ref[...].astype(F32) * gate * (1.0 - gate)).astype(BF16)
            dx_ref[...] = _dot_nt(dbr, w_ref[...]).astype(BF16)
            acc_ref[1 + j] += _dot_tn(x_ref[...], dbr)
        _flush_slots(acc_ref, sq_ref, slots, sem_ref)

    row = pl.BlockSpec((tm, d), lambda i: (i, 0))
    wspec = _resident((d, d))
    bf = jax.ShapeDtypeStruct((t, d), BF16)
    n_in = 18
    return _pcall(
        body, name=name, grid=(t // tm,),
        in_specs=[row, row, _full(g.shape), pl.BlockSpec((tm, d), lambda i: (i, 5)),
                  pl.BlockSpec((tm, d), lambda i: (i, 6)), pl.BlockSpec((tm, d), lambda i: (i, 7)),
                  row, row, row, row, row, row, row, wspec, wspec, wspec, wspec, ANY],
        out_specs=[pl.BlockSpec((tm, 3 * d), lambda i: (i, 0)), row, row, row, _full((1, d)), ANY],
        out_shape=[jax.ShapeDtypeStruct((t, 3 * d), BF16), bf, bf, bf, jax.ShapeDtypeStruct((1, d), F32),
                   jax.ShapeDtypeStruct(sq.shape, sq.dtype)],
        scratch_shapes=[pltpu.VMEM((4, d, d), F32), pltpu.SemaphoreType.DMA((4 * N_CHIPS,))],
        operands=(dh, mo, g, proj, proj, proj, *br, *xs, merged, w_out, *ws, sq), comm=comm,
        aliases={n_in - 1: 5})


def _sgu_bwd(proj, dsg, wm, bs3, gv, bv, name, comm=None):
    t = proj.shape[0]
    d = D_MODEL
    tm = _tile(t, ROW_TILE_HEAVY)
    hd = d // SGU_HEADS

    def body(zu_ref, zv_ref, d_ref, wm_ref, bs_ref, gv_ref, bv_ref, dz_ref, dwm_ref, dbs_ref, dgv_ref, dbv_ref,
             dvn_ref):
        @pl.when(pl.program_id(0) == 0)
        def _():
            dwm_ref[...] = jnp.zeros_like(dwm_ref)
            dbs_ref[...] = jnp.zeros_like(dbs_ref)
            dgv_ref[...] = jnp.zeros_like(dgv_ref)
            dbv_ref[...] = jnp.zeros_like(dbv_ref)

        mask = _sgu_mask()
        for blk in range(tm // SGU_BLOCK):
            rows = pl.ds(blk * SGU_BLOCK, SGU_BLOCK)
            u, du_dz = _gelu_and_grad(zu_ref[rows, :].astype(F32))
            v0, dv_dz = _gelu_and_grad(zv_ref[rows, :].astype(F32))
            xhat, rstd = _ln_stats(v0)
            vn = (xhat * gv_ref[...] + bv_ref[...]).astype(BF16)
            dsg = d_ref[rows, :].astype(F32)
            dmix = (dsg * u).astype(BF16)
            for hh in range(SGU_HEADS):
                cols = slice(hh * hd, (hh + 1) * hd)
                wmh = jnp.where(mask, wm_ref[hh], 0.0).astype(BF16)
                vb = vn[:, cols]
                mixed = jnp.dot(wmh, vb, preferred_element_type=F32) + bs_ref[hh]
                dz_ref[rows, cols] = (dsg[:, cols] * mixed * du_dz[:, cols]).astype(BF16)
                dmh = dmix[:, cols]
                dwm_ref[hh] += jnp.where(mask, _dot_nt(dmh, vb), 0.0)
                dbs_ref[hh] += jnp.sum(dmh.astype(F32), axis=1, keepdims=True)
                dvn_ref[:, cols] = _dot_tn(wmh, dmh)
            dvn = dvn_ref[...]
            dgv_ref[...] += _rowsum(dvn * xhat)
            dbv_ref[...] += _rowsum(dvn)
            dz_ref[rows, d:2 * d] = (_ln_bwd(xhat, rstd, gv_ref[...], dvn) * dv_dz).astype(BF16)

    return _pcall(
        body, name=name, grid=(t // tm,),
        in_specs=[pl.BlockSpec((tm, d), lambda i: (i, 0)), pl.BlockSpec((tm, d), lambda i: (i, 1)),
                  pl.BlockSpec((tm, d), lambda i: (i, 0)), _full(wm.shape), _full(bs3.shape), _full(gv.shape),
                  _full(bv.shape)],
        out_specs=[pl.BlockSpec((tm, 2 * d), lambda i: (i, 0)), _full(wm.shape), _full(bs3.shape), _full((1, d)),
                   _full((1, d))],
        out_shape=[jax.ShapeDtypeStruct((t, 2 * d), BF16), jax.ShapeDtypeStruct(wm.shape, F32),
                   jax.ShapeDtypeStruct(bs3.shape, F32), jax.ShapeDtypeStruct((1, d), F32),
                   jax.ShapeDtypeStruct((1, d), F32)],
        scratch_shapes=[pltpu.VMEM((SGU_BLOCK, d), F32)],
        operands=(proj, proj, dsg, wm, bs3, gv, bv), comm=comm)


def _conv_bwd_norm(proj, dcs, cv, bdw, gln, bln, name, comm=None):
    t = proj.shape[0]
    d = D_MODEL
    tm = _tile(t, ROW_TILE)
    main, halo = _conv_specs(t, tm, d)
    n_win = CONV_ROWS + CONV_HALO

    def body(a_ref, g_ref, ah_ref, gh_ref, dcs_ref, cv_ref, b_ref, gl_ref, bl_ref,
             dcv_ref, dw_ref, db_ref, dgl_ref, dbl_ref, scr_ref, dwacc_ref):
        @pl.when(pl.program_id(0) == 0)
        def _():
            dwacc_ref[...] = jnp.zeros_like(dwacc_ref)
            db_ref[...] = jnp.zeros_like(db_ref)
            dgl_ref[...] = jnp.zeros_like(dgl_ref)
            dbl_ref[...] = jnp.zeros_like(dbl_ref)

        _fill_glu_history(scr_ref, a_ref, g_ref, ah_ref, gh_ref, tm)
        xhat, rstd = _ln_stats(cv_ref[...] + b_ref[...])
        cn = xhat * gl_ref[...] + bl_ref[...]
        s = _sigmoid(cn)
        dcn = dcs_ref[...].astype(F32) * (s * (1.0 + cn * (1.0 - s)))
        dgl_ref[...] += _rowsum(dcn * xhat)
        dbl_ref[...] += _rowsum(dcn)
        dcv = _ln_bwd(xhat, rstd, gl_ref[...], dcn)
        db_ref[...] += _rowsum(dcv)
        dcv_ref[...] = dcv

        def chunk(ci, carry):
            r0 = pl.multiple_of(ci * CONV_ROWS, CONV_ROWS)
            for c0 in range(0, d, CONV_LANES):
                lanes = pl.ds(c0, CONV_LANES)
                win = scr_ref[pl.ds(r0, n_win), lanes]
                dchunk = dcv_ref[pl.ds(r0, CONV_ROWS), lanes]
                for r in range(8):
                    rolled = win if r == 0 else pltpu.roll(win, n_win - r, 0)
                    for q in range(n_win // 8):
                        k = 8 * q + r - _CONV_BASE
                        if 0 <= k < CONV_WIDTH and 8 * q + CONV_ROWS <= n_win:
                            prod = dchunk * rolled[8 * q:8 * q + CONV_ROWS]
                            part = prod[0:8]
                            for s8 in range(8, CONV_ROWS, 8):
                                part = part + prod[s8:s8 + 8]
                            dwacc_ref[pl.ds(8 * k, 8), lanes] += part
            return carry

        lax.fori_loop(0, tm // CONV_ROWS, chunk, 0)

        @pl.when(pl.program_id(0) == pl.num_programs(0) - 1)
        def _():
            dw_ref[...] = jnp.sum(dwacc_ref[...].reshape(CONV_HALO, 8, d), axis=1)

    row = pl.BlockSpec((tm, d), lambda i: (i, 0))
    vec = _full((1, d))
    return _pcall(
        body, name=name, grid=(t // tm,),
        in_specs=main + halo + [row, row, vec, vec, vec],
        out_specs=[row, _full((CONV_HALO, d)), vec, vec, vec],
        out_shape=[jax.ShapeDtypeStruct((t, d), F32), jax.ShapeDtypeStruct((CONV_HALO, d), F32)]
        + [jax.ShapeDtypeStruct((1, d), F32)] * 3,
        scratch_shapes=[pltpu.VMEM((tm + CONV_HALO, d), F32), pltpu.VMEM((8 * CONV_HALO, d), F32)],
        operands=(proj, proj, proj, proj, dcs, cv, bdw, gln, bln), comm=comm)


def _conv_bwd_taps(proj, dcv, wdw, name, comm=None):
    t = proj.shape[0]
    d = D_MODEL
    tm = _tile(t, ROW_TILE)
    hb = tm // CONV_HALO
    last_halo = t // CONV_HALO - 1

    def body(a_ref, g_ref, dcv_ref, dnext_ref, w_ref, dz_ref, scr_ref, dh_ref):
        scr_ref[0:tm, :] = dcv_ref[...]
        is_last = pl.program_id(0) == pl.num_programs(0) - 1
        scr_ref[tm:tm + CONV_HALO, :] = jnp.where(is_last, 0.0, dnext_ref[...])
        _conv_into(scr_ref, dh_ref, w_ref, tm, 0, True)
        dglu = dh_ref[...]
        a = a_ref[...].astype(F32)
        s = _sigmoid(g_ref[...].astype(F32))
        dz_ref[:, 0:d] = (dglu * s).astype(BF16)
        dz_ref[:, d:2 * d] = (dglu * a * s * (1.0 - s)).astype(BF16)

    return _pcall(
        body, name=name, grid=(t // tm,),
        in_specs=[pl.BlockSpec((tm, d), lambda i: (i, 2)), pl.BlockSpec((tm, d), lambda i: (i, 3)),
                  pl.BlockSpec((tm, d), lambda i: (i, 0)),
                  pl.BlockSpec((CONV_HALO, d), lambda i: (jnp.minimum((i + 1) * hb, last_halo), 0)),
                  _full(wdw.shape)],
        out_specs=pl.BlockSpec((tm, 2 * d), lambda i: (i, 0)),
        out_shape=jax.ShapeDtypeStruct((t, 2 * d), BF16),
        scratch_shapes=[pltpu.VMEM((tm + CONV_HALO, d), F32), pltpu.VMEM((tm, d), F32)],
        operands=(proj, proj, dcv, dcv, wdw), comm=comm)


def _pool_bwd(proj, dps, wpool, spool, name):
    t = proj.shape[0]
    d = D_MODEL
    tm = _tile(t, ROW_TILE)
    hb = tm // POOL_BLOCK
    last_halo = t // POOL_BLOCK - 1
    ext = tm + POOL_BLOCK

    def body(z_ref, zh_ref, d_ref, dnext_ref, w_ref, s_ref, dz_ref, dw_ref, ds_ref, scr_ref, dext_ref, hi_ref, lo_ref):
        @pl.when(pl.program_id(0) == 0)
        def _():
            dw_ref[...] = jnp.zeros_like(dw_ref)
            ds_ref[...] = jnp.zeros_like(ds_ref)

        _pool_fill(scr_ref, z_ref, zh_ref, tm)
        t0 = pl.program_id(0) * tm
        is_last = pl.program_id(0) == pl.num_programs(0) - 1
        dext_ref[0:tm, :] = d_ref[...].astype(F32)
        dext_ref[tm:ext, :] = jnp.where(is_last, 0.0, dnext_ref[...].astype(F32))
        for gi, w in enumerate(POOL_WINDOWS):
            cols = slice(gi * POOL_GROUP, (gi + 1) * POOL_GROUP)
            dps_ext = dext_ref[:, cols]
            dpm_ext = (dps_ext * s_ref[:, cols]).astype(BF16)
            dpooled_ext = _dot_nt(dpm_ext, w_ref[gi])
            dq = dpooled_ext / _pool_count(t0, ext, w)
            hi = dq.astype(BF16)
            hi_ref[...] = hi
            lo_ref[...] = (dq - hi.astype(F32)).astype(BF16)
            sums = _window_sums([hi_ref, lo_ref], _band(w, True), tm, slice(None))
            dz_ref[:, cols] = (sums - dpooled_ext[0:tm]).astype(BF16)
            pooled = _pooled_group(scr_ref, gi, w, tm, t0).astype(BF16)
            pm = jnp.dot(pooled, w_ref[gi], preferred_element_type=F32)
            ds_ref[:, cols] += _rowsum(dps_ext[0:tm] * pm)
            dw_ref[gi] += _dot_tn(pooled, dpm_ext[0:tm])

    return pl.pallas_call(
        body, name=name, grid=(t // tm,),
        in_specs=_pool_specs(tm, d) + [pl.BlockSpec((tm, d), lambda i: (i, 0)),
                                       pl.BlockSpec((POOL_BLOCK, d), lambda i: (jnp.minimum((i + 1) * hb, last_halo), 0)),
                                       _full(wpool.shape), _full(spool.shape)],
        out_specs=[pl.BlockSpec((tm, d), lambda i: (i, 0)), _full(wpool.shape), _full((1, d))],
        out_shape=[jax.ShapeDtypeStruct((t, d), BF16), jax.ShapeDtypeStruct(wpool.shape, F32),
                   jax.ShapeDtypeStruct((1, d), F32)],
        scratch_shapes=[pltpu.VMEM((tm + POOL_BLOCK, d), BF16), pltpu.VMEM((ext, d), F32),
                        pltpu.VMEM((ext, POOL_GROUP), BF16), pltpu.VMEM((ext, POOL_GROUP), BF16)],
        compiler_params=_params(1))(proj, proj, dps, dps, wpool, spool)


ANY = pl.BlockSpec(memory_space=pl.ANY)


def _mesh_pos():
    x, y, c = lax.axis_index("x"), lax.axis_index("y"), lax.axis_index("c")
    chips = [(1 - x, y), (x, 1 - y), (1 - x, 1 - y)]
    return x, y, c, chips


def _chip_of(xy):
    return 2 * xy[0] + xy[1]


def _half_view(a):
    return a.reshape(a.shape[:-2] + (2, a.shape[-2] // 2, a.shape[-1]))


def _same(arrs):
    return [jax.ShapeDtypeStruct(a.shape, a.dtype) for a in arrs]


def _in_place(n):
    return {g: g for g in range(n)}


def _sems(count):
    return [pltpu.SemaphoreType.DMA((count,)), pltpu.SemaphoreType.DMA((count,))]


def _gather_ici(bufs):
    n = len(bufs)

    def copy(buf, sems, g, j, chip):
        x, y, c, chips = _mesh_pos()
        slab = buf[g].at[chip, :, c]
        return pltpu.make_async_remote_copy(
            src_ref=slab, dst_ref=slab, send_sem=sems[0].at[3 * g + j], recv_sem=sems[1].at[3 * g + j],
            device_id=(*chips[j], c), device_id_type=MESH)

    def start(ins, buf, sems):
        x, y, c, chips = _mesh_pos()
        for g in range(n):
            for j in range(3):
                copy(buf, sems, g, j, 2 * x + y).start()

    def finish(ins, buf, sems):
        x, y, c, chips = _mesh_pos()
        for g in range(n):
            for j in range(3):
                copy(buf, sems, g, j, _chip_of(chips[j])).wait_recv()
        for g in range(n):
            for j in range(3):
                copy(buf, sems, g, j, 2 * x + y).wait_send()

    return _Payload(bufs, _same(bufs), _in_place(n), _sems(3 * n), start, finish)


def _gather_d2d(bufs):
    n = len(bufs)

    def copy(buf, sems, g, j, half):
        x, y, c, chips = _mesh_pos()
        slab = buf[g].at[_chip_of(chips[j]), :, half]
        return pltpu.make_async_remote_copy(
            src_ref=slab, dst_ref=slab, send_sem=sems[0].at[3 * g + j], recv_sem=sems[1].at[3 * g + j],
            device_id=(x, y, 1 - c), device_id_type=MESH)

    def start(ins, buf, sems):
        c = lax.axis_index("c")
        for g in range(n):
            for j in range(3):
                copy(buf, sems, g, j, c).start()

    def finish(ins, buf, sems):
        c = lax.axis_index("c")
        for g in range(n):
            for j in range(3):
                copy(buf, sems, g, j, 1 - c).wait_recv()
        for g in range(n):
            for j in range(3):
                copy(buf, sems, g, j, c).wait_send()

    return _Payload(bufs, _same(bufs), _in_place(n), _sems(3 * n), start, finish)


def _pair_exchange(grads):
    n = len(grads)

    def copy(src, dst, sems, g):
        x, y, c, _ = _mesh_pos()
        return pltpu.make_async_remote_copy(
            src_ref=src[g].at[:, :, 1 - c], dst_ref=dst[g], send_sem=sems[0].at[g], recv_sem=sems[1].at[g],
            device_id=(x, y, 1 - c), device_id_type=MESH)

    def start(src, dst, sems):
        for g in range(n):
            copy(src, dst, sems, g).start()

    def finish(src, dst, sems):
        for g in range(n):
            copy(src, dst, sems, g).wait()

    out_shape = [jax.ShapeDtypeStruct(g.shape[:2] + g.shape[3:], g.dtype) for g in grads]
    return _Payload(grads, out_shape, {}, _sems(n), start, finish)


def _chip_exchange(parts):
    n = len(parts)

    def copy(src, dst, sems, g, j, slot):
        x, y, c, chips = _mesh_pos()
        return pltpu.make_async_remote_copy(
            src_ref=src[g].at[_chip_of(chips[j])], dst_ref=dst[g].at[slot], send_sem=sems[0].at[3 * g + j],
            recv_sem=sems[1].at[3 * g + j], device_id=(*chips[j], c), device_id_type=MESH)

    def start(src, dst, sems):
        x, y, c, chips = _mesh_pos()
        for g in range(n):
            for j in range(3):
                copy(src, dst, sems, g, j, 2 * x + y).start()

    def finish(src, dst, sems):
        x, y, c, chips = _mesh_pos()
        for g in range(n):
            for j in range(3):
                copy(src, dst, sems, g, j, _chip_of(chips[j])).wait_recv()
        for g in range(n):
            for j in range(3):
                copy(src, dst, sems, g, j, 2 * x + y).wait_send()

    return _Payload(parts, _same(parts), {}, _sems(3 * n), start, finish)


def _pair_share(bufs):
    n = len(bufs)

    def copy(buf, sems, g, half):
        x, y, c, _ = _mesh_pos()
        slab = buf[g].at[:, :, half]
        return pltpu.make_async_remote_copy(
            src_ref=slab, dst_ref=slab, send_sem=sems[0].at[g], recv_sem=sems[1].at[g],
            device_id=(x, y, 1 - c), device_id_type=MESH)

    def start(ins, buf, sems):
        c = lax.axis_index("c")
        for g in range(n):
            copy(buf, sems, g, c).start()

    def finish(ins, buf, sems):
        c = lax.axis_index("c")
        for g in range(n):
            copy(buf, sems, g, 1 - c).wait_recv()
        for g in range(n):
            copy(buf, sems, g, c).wait_send()

    return _Payload(bufs, _same(bufs), _in_place(n), _sems(n), start, finish)


def _join(a, b):
    if a is None or b is None:
        return a or b
    na, ma = len(a.operands), len(a.out_shape)
    aliases = dict(a.aliases)
    aliases.update({na + i: ma + o for i, o in b.aliases.items()})
    ka = len(a.scratch)

    def start(ins, outs, sems):
        a.start(ins[:na], outs[:ma], sems[:ka])
        b.start(ins[na:], outs[ma:], sems[ka:])

    def finish(ins, outs, sems):
        a.finish(ins[:na], outs[:ma], sems[:ka])
        b.finish(ins[na:], outs[ma:], sems[ka:])

    joined = _Payload(a.operands + b.operands, a.out_shape + b.out_shape, aliases, list(a.scratch) + list(b.scratch),
                      start, finish)
    joined.parts = (a, b, ma)
    return joined


def _small_exchange(vec):
    def copy(src, dst, sems, k, slot):
        x, y, c, _ = _mesh_pos()
        peer = (x ^ (k >> 2), y ^ ((k >> 1) & 1), c ^ (k & 1))
        return pltpu.make_async_remote_copy(
            src_ref=src[0], dst_ref=dst[0].at[slot], send_sem=sems[0].at[k - 1], recv_sem=sems[1].at[k - 1],
            device_id=peer, device_id_type=MESH)

    def me():
        x, y, c, _ = _mesh_pos()
        return 4 * x + 2 * y + c

    def start(src, dst, sems):
        for k in range(1, 8):
            copy(src, dst, sems, k, me()).start()

    def finish(src, dst, sems):
        for k in range(1, 8):
            copy(src, dst, sems, k, me() ^ k).wait_recv()
        for k in range(1, 8):
            copy(src, dst, sems, k, me()).wait_send()

    return _Payload([vec], [jax.ShapeDtypeStruct((8,) + vec.shape, vec.dtype)], {}, _sems(7), start, finish)


def _small_sum(vec, landed, pos, name):
    r = vec.shape[0]

    def body(pos_ref, v_ref, l_ref, o_ref):
        k = pl.program_id(0)

        @pl.when(k == 0)
        def _():
            o_ref[...] = jnp.zeros_like(o_ref)

        @pl.when(k == pos_ref[POS_DEVICE])
        def _():
            o_ref[...] += v_ref[...]

        @pl.when(k != pos_ref[POS_DEVICE])
        def _():
            o_ref[...] += l_ref[...]

    def landed_index(k, pos_ref):
        me = pos_ref[POS_DEVICE]
        return jnp.where(k == me, (me + 1) % 8, k), 0, 0

    return pl.pallas_call(
        body, name=name,
        grid_spec=pltpu.PrefetchScalarGridSpec(
            num_scalar_prefetch=1, grid=(8,),
            in_specs=[pl.BlockSpec((r, 128), lambda k, pos_ref: (0, 0)), pl.BlockSpec((None, r, 128), landed_index)],
            out_specs=pl.BlockSpec((r, 128), lambda k, pos_ref: (0, 0))),
        out_shape=jax.ShapeDtypeStruct(vec.shape, F32),
        compiler_params=_params(1))(pos, vec, landed)


def _all_reduce_small(vec, name):
    r = vec.shape[0]

    def body(v_ref, o_ref, gath_ref, send_sem, recv_sem):
        x, y, c, _ = _mesh_pos()
        me = 4 * x + 2 * y + c
        gath_ref[me] = v_ref[...]
        copies = []
        for k in range(1, 8):
            peer = (x ^ (k >> 2), y ^ ((k >> 1) & 1), c ^ (k & 1))
            cp = pltpu.make_async_remote_copy(
                src_ref=v_ref, dst_ref=gath_ref.at[me], send_sem=send_sem.at[k - 1], recv_sem=recv_sem.at[k - 1],
                device_id=peer, device_id_type=MESH)
            cp.start()
            copies.append(cp)
        for k in range(1, 8):
            src_id = me ^ k
            pltpu.make_async_remote_copy(
                src_ref=v_ref, dst_ref=gath_ref.at[src_id], send_sem=send_sem.at[k - 1], recv_sem=recv_sem.at[k - 1],
                device_id=(x, y, c), device_id_type=MESH).wait_recv()
        for cp in copies:
            cp.wait_send()
        acc = gath_ref[0]
        for k in range(1, 8):
            acc = acc + gath_ref[k]
        o_ref[...] = acc

    return pl.pallas_call(
        body, name=name,
        in_specs=[pl.BlockSpec(memory_space=pltpu.VMEM)], out_specs=pl.BlockSpec(memory_space=pltpu.VMEM),
        out_shape=jax.ShapeDtypeStruct(vec.shape, F32),
        scratch_shapes=[pltpu.VMEM((8, r, 128), F32), pltpu.SemaphoreType.DMA((7,)), pltpu.SemaphoreType.DMA((7,))],
        compiler_params=pltpu.CompilerParams(has_side_effects=True, vmem_limit_bytes=VMEM_LIMIT))(vec)


def _row_block(rows, cols, mult=16):
    best = None
    for cand in range(mult, rows + 1, mult):
        if rows % cand == 0 and cand * cols * 4 <= EW_BLOCK_BYTES:
            best = cand
    return best or rows


POS_ME, POS_CORE, POS_DEVICE = 0, 4, 5


def _place(arrs, li, pos, dtype, name):
    s = len(arrs)
    _, rows, cols = arrs[0].shape
    rh = rows // 2
    tr = _row_block(rh, cols)
    nb = rh // tr

    def body(pos_ref, *refs):
        o_ref = refs[s]
        for j in range(s):
            @pl.when(pl.program_id(0) == j)
            def _(j=j):
                o_ref[...] = refs[j][...].astype(dtype)

    def in_spec(j):
        return pl.BlockSpec((None, tr, cols), lambda b, hf, i, pos_ref: (li, jnp.where(b == j, hf * nb + i, 0), 0))

    return pl.pallas_call(
        body, name=name,
        grid_spec=pltpu.PrefetchScalarGridSpec(
            num_scalar_prefetch=1, grid=(s, 2, nb), in_specs=[in_spec(j) for j in range(s)],
            out_specs=pl.BlockSpec((None, None, None, tr, cols),
                                   lambda b, hf, i, pos_ref: (pos_ref[POS_ME], b, hf, i, 0))),
        out_shape=jax.ShapeDtypeStruct((N_CHIPS, s, 2, rh, cols), dtype),
        compiler_params=_params(3))(pos, *arrs)


def _pair_sum(grad, recv, pos, out_dtype, name):
    _, s, rh, cols = recv.shape
    tr = _row_block(rh, cols)

    def body(pos_ref, g_ref, r_ref, o_ref):
        o_ref[...] = (g_ref[...] + r_ref[...]).astype(out_dtype)

    blk = (None, None, tr, cols)
    return pl.pallas_call(
        body, name=name,
        grid_spec=pltpu.PrefetchScalarGridSpec(
            num_scalar_prefetch=1, grid=(N_CHIPS, s, rh // tr),
            in_specs=[pl.BlockSpec((None, None, None, tr, cols),
                                   lambda a, b, i, pos_ref: (a, b, pos_ref[POS_CORE], i, 0)),
                      pl.BlockSpec(blk, lambda a, b, i, pos_ref: (a, b, i, 0))],
            out_specs=pl.BlockSpec(blk, lambda a, b, i, pos_ref: (a, b, i, 0))),
        out_shape=jax.ShapeDtypeStruct(recv.shape, out_dtype),
        compiler_params=_params(3))(pos, grad, recv)


def _chip_sum(part, landed, gbuf, li, n_layers, pos, name):
    _, s, rh, cols = part.shape
    tr = _row_block(rh, cols)

    def body(pos_ref, p_ref, a_ref, b_ref, c_ref, *rest):
        o_ref = rest[-1]
        o_ref[...] = ((p_ref[...].astype(F32) + a_ref[...].astype(F32)) + b_ref[...].astype(F32)) \
            + c_ref[...].astype(F32)

    def slot(k):
        return pl.BlockSpec((None, None, tr, cols), lambda b, i, pos_ref: (pos_ref[k], b, i, 0))

    in_specs = [slot(0), slot(1), slot(2), slot(3)]
    operands = [pos, part, landed, landed, landed]
    aliases = {}
    if gbuf is not None:
        in_specs.append(ANY)
        operands.append(gbuf)
        aliases = {len(operands) - 1: 0}
    return pl.pallas_call(
        body, name=name,
        grid_spec=pltpu.PrefetchScalarGridSpec(
            num_scalar_prefetch=1, grid=(s, rh // tr), in_specs=in_specs,
            out_specs=pl.BlockSpec((None, None, None, tr, cols),
                                   lambda b, i, pos_ref: (li, b, pos_ref[POS_CORE], i, 0))),
        out_shape=jax.ShapeDtypeStruct((n_layers, s, 2, rh, cols), F32),
        input_output_aliases=aliases,
        compiler_params=_params(2))(*operands)


def _adamw_math(w, g, m, v):
    m = ADAM_B1 * m + (1.0 - ADAM_B1) * g
    v = ADAM_B2 * v + (1.0 - ADAM_B2) * (g * g)
    m_hat = m / (1.0 - ADAM_B1 ** ADAM_STEP)
    v_hat = v / (1.0 - ADAM_B2 ** ADAM_STEP)
    delta = -ADAM_LR * (m_hat / (jnp.sqrt(v_hat) + ADAM_EPS) + ADAM_WD * w)
    return delta, m, v


def _adamw(w, g, slot, m, v, name):
    l, rows, cols = w.shape
    tr = _row_block(rows, cols, 8)

    def body(w_ref, g_ref, m_ref, v_ref, go_ref, d_ref, mo_ref, vo_ref):
        g_ = g_ref[...]
        delta, m_, v_ = _adamw_math(w_ref[...], g_, m_ref[...], v_ref[...])
        go_ref[...] = g_
        d_ref[...] = delta
        mo_ref[...] = m_
        vo_ref[...] = v_

    blk = pl.BlockSpec((None, tr, cols), lambda a, i: (a, i, 0))
    gblk = pl.BlockSpec((None, None, tr, cols), lambda a, i: (a, slot, i, 0))
    return pl.pallas_call(
        body, name=name, grid=(l, rows // tr), in_specs=[blk, gblk, blk, blk], out_specs=[blk] * 4,
        out_shape=[jax.ShapeDtypeStruct(w.shape, F32)] * 4,
        compiler_params=_params(2))(w, g, m, v)


SQ = ("w_sgu_out", "w_conv_out", "w_pool_out", "w_out", "w_ple_gate")
SMALL = ("g_mix_pre", "w_sgu_s", "b_sgu_s", "g_sgu_v", "b_sgu_v", "b_dw", "g_conv_ln", "b_conv_ln", "s_pool",
         "g_mix_post", "g_ffn_pre", "g_ffn_post")


WHERE = {"w_in": ("in", 0), "w_ffn_in": ("ffn_in", 0), "w_ffn_out": ("ffn_out", 0), "w_ple": ("mix", 0),
         "w_pool": ("mix", 1), "w_dw": ("dw", 0)}
WHERE.update({nm: ("sq", slot) for slot, nm in enumerate(SQ)})


class _LayerWeights:
    def __init__(self, fetch, small, li):
        self.fetch, self.small, self.li, self.cache = fetch, small, li, {}

    def __getitem__(self, nm):
        if nm not in self.cache:
            self.cache[nm] = self._big(nm) if nm in WHERE else self.small[nm][self.li]
        return self.cache[nm]

    def _big(self, nm):
        group, slot = WHERE[nm]
        g = self.fetch(group)
        g = g.reshape(g.shape[:2] + (-1, g.shape[-1]))
        if nm in ("w_in", "w_ffn_in"):
            return g.reshape(N_CHIPS, D_MODEL, -1)
        if nm == "w_ffn_out":
            return g.reshape(D_FF, D_MODEL)
        if nm in SQ:
            return g[:, slot].reshape(D_MODEL, D_MODEL)
        if nm == "w_ple":
            return g[:, slot].transpose(1, 0, 2).reshape(256, D_MODEL)
        if nm == "w_pool":
            return g[:, slot].reshape(N_CHIPS, 4, 64, 256).transpose(1, 0, 2, 3).reshape(4, 256, 256)
        return g.reshape(N_CHIPS, CONV_HALO, -1).transpose(1, 0, 2).reshape(CONV_HALO, D_MODEL)


def _vec(a):
    return a.reshape(1, -1)


def _layer_fwd(h, p, w, li, hosts=None):
    s = {}
    tag = "_l%d" % li
    s["h0"] = h
    proj, hn = _norm_mm(h, _vec(w["g_mix_pre"]), w["w_in"], "mix_in" + tag, _take(hosts, "mix_in"))
    s["proj"], s["hn"] = proj, hn
    bs3 = w["b_sgu_s"].reshape(SGU_HEADS, SGU_BLOCK, 1)
    s["sg"] = _sgu_fwd(proj, w["w_sgu_s"], bs3, _vec(w["g_sgu_v"]), _vec(w["b_sgu_v"]), "sgu_fwd" + tag,
                       _take(hosts, "sgu_fwd"))
    s["cs"], s["cv"] = _conv_fwd(proj, w["w_dw"], _vec(w["b_dw"]), _vec(w["g_conv_ln"]), _vec(w["b_conv_ln"]),
                                 "conv_fwd" + tag, _take(hosts, "conv_fwd"))
    s["ps"] = _pool_fwd(proj, w["w_pool"], _vec(w["s_pool"]), "pool_fwd" + tag, _take(hosts, "pool_fwd"))
    s["bra"], s["brb"], s["brc"], s["merged"] = _merge_fwd(
        proj, s["sg"], s["cs"], s["ps"], w["w_sgu_out"], w["w_conv_out"], w["w_pool_out"], "merge_fwd" + tag,
        _take(hosts, "merge_fwd"))
    s["mo"], h1 = _mm_norm_res(s["merged"], w["w_out"], _vec(w["g_mix_post"]), h, "mix_out" + tag,
                               _take(hosts, "mix_out"))
    s["h1"] = h1
    s["fg"], s["fu"], s["act"], s["hn2"] = _ffn_in(h1, _vec(w["g_ffn_pre"]), w["w_ffn_in"], "ffn_in" + tag,
                                                   _take(hosts, "ffn_in"))
    s["f"], h2 = _mm_norm_res(s["act"], w["w_ffn_out"], _vec(w["g_ffn_post"]), h1, "ffn_out" + tag,
                              _take(hosts, "ffn_out"))
    s["h2"] = h2
    h3, s["q"], s["e"] = _ple_fwd(h2, p, w["w_ple_gate"], w["w_ple"], "ple_fwd" + tag, _take(hosts, "ple_fwd"))
    return h3, s


def _layer_bwd(dh3, p, w, s, li, hosts=None, big=None, gs=None):
    tag = "_l%d" % li
    d = D_MODEL
    gs = {} if gs is None else gs
    big = {} if big is None else big
    dh2, sq, dw_ple = _ple_bwd(dh3, s["q"], s["e"], w["w_ple_gate"], s["h2"], p, SQ.index("w_ple_gate"), len(SQ),
                               "ple_bwd" + tag)
    dff, gs["g_ffn_post"], dw_ffn_out = _ffn_out_bwd(
        dh2, s["f"], _vec(w["g_ffn_post"]), s["fg"], s["fu"], s["act"], w["w_ffn_out"], "ffn_out_bwd" + tag,
        _take(hosts, "ffn_out_bwd"))
    big["ffn_out"] = dw_ffn_out.reshape(N_CHIPS, 1, D_FF // N_CHIPS, d)
    n_ff = w["w_ffn_in"].shape[2]
    dh1, gs["g_ffn_pre"] = _in_bwd([(dff, 2 * D_FF // n_ff)], w["w_ffn_in"], n_ff, s["h1"], _vec(w["g_ffn_pre"]),
                                   dh2, ROW_TILE, "ffn_in_bwd" + tag, _take(hosts, "ffn_in_bwd"))
    big["ffn_in"] = _dw_cols(s["hn2"], [(dff, 2 * D_FF // n_ff)], n_ff, 1, "dw_ffn_in" + tag)
    branch_w = ("w_sgu_out", "w_conv_out", "w_pool_out")
    dzg, dsg, dcs, dps, gs["g_mix_post"], big["sq"] = _merge_bwd(
        dh1, s["mo"], _vec(w["g_mix_post"]), s["proj"], (s["bra"], s["brb"], s["brc"]), (s["sg"], s["cs"], s["ps"]),
        s["merged"], w["w_out"], [w[nm] for nm in branch_w], sq, [SQ.index(nm) for nm in ("w_out",) + branch_w],
        "merge_bwd" + tag, _take(hosts, "merge_bwd"))
    bs3 = w["b_sgu_s"].reshape(SGU_HEADS, SGU_BLOCK, 1)
    dz_sgu, gs["w_sgu_s"], dbs3, gs["g_sgu_v"], gs["b_sgu_v"] = _sgu_bwd(
        s["proj"], dsg, w["w_sgu_s"], bs3, _vec(w["g_sgu_v"]), _vec(w["b_sgu_v"]), "sgu_bwd" + tag,
        _take(hosts, "sgu_bwd"))
    gs["b_sgu_s"] = dbs3
    dcv, dwdw, gs["b_dw"], gs["g_conv_ln"], gs["b_conv_ln"] = _conv_bwd_norm(
        s["proj"], dcs, s["cv"], _vec(w["b_dw"]), _vec(w["g_conv_ln"]), _vec(w["b_conv_ln"]), "conv_bwd_norm" + tag,
        _take(hosts, "conv_bwd_norm"))
    dz_conv = _conv_bwd_taps(s["proj"], dcv, w["w_dw"], "conv_bwd_taps" + tag, _take(hosts, "conv_bwd_taps"))
    dz_pool, dwpool, gs["s_pool"] = _pool_bwd(s["proj"], dps, w["w_pool"], _vec(w["s_pool"]), "pool_bwd" + tag)
    pieces = [(dz_sgu, 2), (dz_conv, 2), (dz_pool, 1), (dzg, 3)]
    big["in"] = _dw_cols(s["hn"], pieces, d, 2, "dw_in" + tag)
    gple = dw_ple.reshape(256, N_CHIPS, 256).transpose(1, 0, 2)
    gpool = dwpool.reshape(4, N_CHIPS, 64, 256).transpose(1, 0, 2, 3).reshape(N_CHIPS, 256, 256)
    big["mix"] = jnp.stack([gple, gpool], axis=1)
    big["dw"] = dwdw.reshape(CONV_HALO, N_CHIPS, 256).transpose(1, 0, 2)[:, None]
    dh0, gs["g_mix_pre"] = _in_bwd(pieces, w["w_in"], d, s["h0"], _vec(w["g_mix_pre"]), dh1, ROW_TILE,
                                   "mix_in_bwd" + tag, _take(hosts, "mix_in_bwd"))
    return dh0, big, gs


GROUPS = ("in", "sq", "ffn_in", "ffn_out", "mix", "dw")
WIRE_DTYPE = {"in": BF16, "sq": BF16, "ffn_in": BF16, "ffn_out": BF16, "mix": BF16, "dw": F32}
GATHER_FIRST = ("in", "mix", "dw")
GATHER_RIDES = (("mix_in", "sgu_fwd", ("sq", "ffn_in"), ()),
                ("conv_fwd", "pool_fwd", ("ffn_out",), ("in",)),
                ("merge_fwd", "mix_out", (), ("sq",)),
                ("ffn_in", "ffn_out", (), ("ffn_in", "ffn_out", "mix", "dw")))
REDUCE_UPPER = ("ffn_out_bwd", (("ffn_in_bwd", ("in", "ffn_out")), ("merge_bwd", ("sq", "ffn_in", "mix", "dw"))))
REDUCE_OWN = ("sgu_bwd", (("conv_bwd_norm", ("ffn_in", "ffn_out")), ("conv_bwd_taps", ("sq",))))
REDUCE_LAST = ("in", "mix", "dw")


def _group_members(wts):
    n_layers = wts["w_in"].shape[0]
    dw = wts["w_dw"].reshape(n_layers, CONV_WIDTH, -1)
    return {"in": [wts["w_in"]], "sq": [wts[nm] for nm in SQ], "ffn_in": [wts["w_ffn_in"]],
            "ffn_out": [wts["w_ffn_out"]],
            "mix": [wts["w_ple"], wts["w_pool"].reshape(n_layers, POOL_GROUP, POOL_GROUP)],
            "dw": [jnp.pad(dw, ((0, 0), (0, CONV_HALO - CONV_WIDTH), (0, 0)))]}


class _Gather:
    PLACED, OVER_ICI, FULL = 0, 1, 2

    def __init__(self):
        self.buf, self.stage, self.pending = {}, {}, []

    def put(self, key, buf):
        self.buf[key], self.stage[key] = buf, self.PLACED

    def _flush(self):
        for keys, pay, stage in self.pending:
            if pay.results is not None:
                for key, res in zip(keys, pay.results):
                    self.buf[key], self.stage[key] = res, stage
        self.pending = [entry for entry in self.pending if entry[1].results is None]

    def _factory(self, make, keys, before, after):
        def factory():
            if not keys:
                return None
            self._flush()
            assert all(self.stage[k] == before for k in keys), (keys, self.stage)
            pay = make([self.buf[k] for k in keys])
            self.pending.append((keys, pay, after))
            return pay
        return factory

    def ici(self, keys):
        return self._factory(_gather_ici, keys, self.PLACED, self.OVER_ICI)

    def d2d(self, keys):
        return self._factory(_gather_d2d, keys, self.OVER_ICI, self.FULL)

    def get(self, li, group):
        self._flush()
        assert self.stage[(li, group)] == self.FULL, (li, group)
        return self.buf[(li, group)]


class _Reduce:
    def __init__(self, pos, n_layers):
        self.pos, self.n_layers, self.exchanged, self.stages = pos, n_layers, [], []

    def exchange(self, li, groups, grads):
        def factory():
            pay = _pair_exchange([_half_view(grads[g]) for g in groups])
            self.exchanged.append((li, list(groups), pay))
            return pay
        return factory

    def _received(self, li, group):
        for lj, groups, pay in self.exchanged:
            if lj == li and group in groups:
                return pay.results[groups.index(group)]
        raise KeyError((li, group))

    def chips(self, li, groups, grads):
        def factory():
            parts = [_pair_sum(_half_view(grads[g]), self._received(li, g), self.pos, WIRE_DTYPE[g],
                               "pair_sum_%s_l%d" % (g, li)) for g in groups]
            pay = _chip_exchange(parts)
            self.stages.append((li, groups, parts, pay))
            return pay
        return factory

    def finish(self):
        reduced = {}
        for li, groups, parts, pay in self.stages:
            for g, part, landed in zip(groups, parts, pay.results):
                reduced[g] = _chip_sum(part, landed, reduced.get(g), li, self.n_layers, self.pos,
                                       "chip_sum_%s_l%d" % (g, li))
        return reduced


def _pack_small(tree):
    flat = jnp.concatenate([tree[nm].reshape(-1).astype(F32) for nm in SMALL])
    return flat.reshape(-1, 128)


def _unpack_small(packed, like):
    out, off = {}, 0
    flat = packed.reshape(-1)
    for nm in SMALL:
        n = like[nm].size
        out[nm] = flat[off:off + n].reshape(like[nm].shape)
        off += n
    return out


WEIGHTS = ("g_mix_pre", "w_in", "w_sgu_s", "b_sgu_s", "g_sgu_v", "b_sgu_v", "w_sgu_out", "w_dw", "b_dw", "g_conv_ln",
           "b_conv_ln", "w_conv_out", "w_pool", "s_pool", "w_pool_out", "w_out", "g_mix_post", "g_ffn_pre",
           "w_ffn_in", "w_ffn_out", "g_ffn_post", "w_ple", "w_ple_gate")


def kernel(x, p, g_mix_pre, w_in, w_sgu_s, b_sgu_s, g_sgu_v, b_sgu_v, w_sgu_out, w_dw, b_dw, g_conv_ln, b_conv_ln, w_conv_out, w_pool, s_pool, w_pool_out, w_out, g_mix_post, g_ffn_pre, w_ffn_in, w_ffn_out, g_ffn_post, w_ple, w_ple_gate, loss_target, m_g_mix_pre, m_w_in, m_w_sgu_s, m_b_sgu_s, m_g_sgu_v, m_b_sgu_v, m_w_sgu_out, m_w_dw, m_b_dw, m_g_conv_ln, m_b_conv_ln, m_w_conv_out, m_w_pool, m_s_pool, m_w_pool_out, m_w_out, m_g_mix_post, m_g_ffn_pre, m_w_ffn_in, m_w_ffn_out, m_g_ffn_post, m_w_ple, m_w_ple_gate, v_g_mix_pre, v_w_in, v_w_sgu_s, v_b_sgu_s, v_g_sgu_v, v_b_sgu_v, v_w_sgu_out, v_w_dw, v_b_dw, v_g_conv_ln, v_b_conv_ln, v_w_conv_out, v_w_pool, v_s_pool, v_w_pool_out, v_w_out, v_g_mix_post, v_g_ffn_pre, v_w_ffn_in, v_w_ffn_out, v_g_ffn_post, v_w_ple, v_w_ple_gate):
    args = dict(locals())
    wts = {nm: args[nm] for nm in WEIGHTS}
    mom = {nm: args["m_" + nm] for nm in WEIGHTS}
    var = {nm: args["v_" + nm] for nm in WEIGHTS}
    n_layers = w_in.shape[0]
    h = x.reshape(x.shape[1:])
    target = loss_target.reshape(loss_target.shape[1:])
    cx, cy, core = lax.axis_index("x"), lax.axis_index("y"), lax.axis_index("c")
    pos = jnp.stack([2 * cx + cy, 2 * (1 - cx) + cy, 2 * cx + (1 - cy), 2 * (1 - cx) + (1 - cy), core,
                     4 * cx + 2 * cy + core])
    pos = pos.astype(jnp.int32)

    members = _group_members(wts)
    gather = _Gather()
    for li in range(n_layers):
        for g in GROUPS:
            gather.put((li, g), _place(members[g], li, pos, WIRE_DTYPE[g], "place_%s_l%d" % (g, li)))
    first = [(0, g) for g in GATHER_FIRST]
    _run_payload(gather.ici(first)(), "gather_ici_first")
    _run_payload(gather.d2d(first)(), "gather_d2d_first")

    saved, layer_w = [], []
    for li in range(n_layers):
        hosts = {}
        for ici_host, d2d_host, own, nxt in GATHER_RIDES:
            keys = [(li, g) for g in own if li == 0] + [(li + 1, g) for g in nxt if li + 1 < n_layers]
            hosts[ici_host], hosts[d2d_host] = gather.ici(keys), gather.d2d(keys)
        w = _LayerWeights(functools.partial(gather.get, li), wts, li)
        layer_w.append(w)
        h, s = _layer_fwd(h, p[li, 0], w, li, hosts)
        saved.append(s)
    dh, sq_err = _loss_head(h, target, "loss_head")
    loss = lax.psum(sq_err[0, 0] * (0.5 / D_MODEL), ("x", "y", "c"))

    reduce = _Reduce(pos, n_layers)
    small_grads = [{} for _ in range(n_layers)]
    late = (0, SMALL[0])
    small = {}

    def small_vec():
        def leaf(li, nm):
            shape = wts[nm].shape[1:]
            return jnp.zeros(shape, F32) if (li, nm) == late else small_grads[li][nm].reshape(shape)
        return _pack_small({nm: jnp.stack([leaf(li, nm) for li in range(n_layers)], axis=0) for nm in SMALL})

    upper = None
    for li in reversed(range(n_layers)):
        own = {}
        hosts = {}
        plans = [(REDUCE_UPPER, li + 1, upper)] if upper is not None else []
        if li == 0:
            plans.append((REDUCE_OWN, 0, own))

            def last_rides(own=own):
                _run_payload(reduce.exchange(0, REDUCE_LAST, own)(), "pair_exchange_last")
                small["vec"] = small_vec()
                small["exchange"] = _small_exchange(small["vec"])
                return _join(reduce.chips(0, REDUCE_LAST, own)(), small["exchange"])
            hosts["mix_in_bwd"] = last_rides
        for (pair_host, chip_hosts), lj, grads in plans:
            groups = [g for _, gs_ in chip_hosts for g in gs_]
            hosts[pair_host] = reduce.exchange(lj, groups, grads)
            for chip_host, gs_ in chip_hosts:
                hosts[chip_host] = reduce.chips(lj, gs_, grads)
        dh, upper, _ = _layer_bwd(dh, p[li, 0], layer_w[li], saved[li], li, hosts, own, small_grads[li])
    grad_x = dh[None]
    reduced = reduce.finish()

    shared = _run_payload(_pair_share([reduced[g] for g in GROUPS]), "pair_share")
    red = {g: b.reshape(b.shape[:2] + (-1, b.shape[-1])) for g, b in zip(GROUPS, shared)}

    where = {"w_in": ("in", 0), "w_ffn_in": ("ffn_in", 0), "w_ffn_out": ("ffn_out", 0), "w_ple": ("mix", 0),
             "w_pool": ("mix", 1)}
    for slot, nm in enumerate(SQ):
        where[nm] = ("sq", slot)
    outs = {}
    for nm, (g, slot) in where.items():
        shape = wts[nm].shape
        to3 = lambda a: a.reshape((n_layers,) + red[g].shape[2:])
        res = _adamw(to3(wts[nm]), red[g], slot, to3(mom[nm]), to3(var[nm]), "adamw_" + nm)
        outs[nm] = [r.reshape(shape) for r in res]
    gdw = red["dw"][:, :, :CONV_WIDTH]
    to3 = lambda a: a.reshape(n_layers, CONV_WIDTH, -1)
    res = _adamw(to3(wts["w_dw"]), gdw, 0, to3(mom["w_dw"]), to3(var["w_dw"]), "adamw_w_dw")
    outs["w_dw"] = [r.reshape(wts["w_dw"].shape) for r in res]

    gmain = _small_sum(small["vec"], small["exchange"].results[0], pos, "small_sum")
    glate = _all_reduce_small(small_grads[late[0]][late[1]].reshape(-1, 128), "all_reduce_late")
    gsmall = jnp.concatenate([glate, gmain[glate.shape[0]:]], axis=0)
    pk = lambda tree: _pack_small({nm: tree[nm] for nm in SMALL})[None]
    res = _adamw(pk(wts), gsmall[None, None], 0, pk(mom), pk(var), "adamw_small")
    unpacked = [_unpack_small(r[0], wts) for r in res]
    for nm in SMALL:
        outs[nm] = [u[nm] for u in unpacked]

    result = [loss, grad_x]
    for k in range(4):
        result += [outs[nm][k] for nm in WEIGHTS]
    return tuple(result)
```

```python
import functools

import jax
import jax.numpy as jnp
from jax import lax
from jax.experimental import pallas as pl
from jax.experimental.pallas import tpu as pltpu

F32 = jnp.float32
BF16 = jnp.bfloat16
MESH = pl.DeviceIdType.MESH

EPS = 1e-6
D_MODEL = 1024
SGU_BLOCK = 128
SGU_HEADS = 8
CHUNK = 64
CONV_WIDTH = 31
CONV_HALO = 32
POOL_WINDOWS = (2, 4, 8, 16)
POOL_BLOCK = 128
POOL_GROUP = 256
D_FF = 2816
N_CHIPS = 4

ADAM_LR = 0.001
ADAM_B1 = 0.9
ADAM_B2 = 0.999
ADAM_EPS = 1e-08
ADAM_WD = 0.01
ADAM_STEP = 10

VMEM_LIMIT = 52 * 1024 * 1024
ROW_TILE_LIGHT = 1024
ROW_TILE = 512
ROW_TILE_HEAVY = 256
CONV_ROWS = 128
CONV_LANES = 128
EW_BLOCK_BYTES = 2 * 1024 * 1024
TOKEN_TILE = 2048
FF_CHUNK = 256


def _params(n_grid):
    return pltpu.CompilerParams(dimension_semantics=("arbitrary",) * n_grid, vmem_limit_bytes=VMEM_LIMIT)


def _dot(a, b):
    return jnp.dot(a.astype(BF16), b.astype(BF16), preferred_element_type=F32)


def _dot_nt(a, b):
    return lax.dot_general(a.astype(BF16), b.astype(BF16), (((1,), (1,)), ((), ())), preferred_element_type=F32)


def _dot_tn(a, b):
    return lax.dot_general(a.astype(BF16), b.astype(BF16), (((0,), (0,)), ((), ())), preferred_element_type=F32)


def _sigmoid(x):
    return 0.5 * jnp.tanh(0.5 * x) + 0.5


_GELU_C = 0.7978845608028654
_GELU_A = 0.044715


def _gelu(x):
    t = jnp.tanh(_GELU_C * (x + _GELU_A * x * x * x))
    return 0.5 * x * (1.0 + t)


def _gelu_and_grad(x):
    x2 = x * x
    t = jnp.tanh(_GELU_C * (x + _GELU_A * x2 * x))
    g = 0.5 * (1.0 + t) + 0.5 * x * (1.0 - t * t) * (_GELU_C * (1.0 + 3.0 * _GELU_A * x2))
    return 0.5 * x * (1.0 + t), g


def _rms_stats(x):
    r = lax.rsqrt(jnp.mean(x * x, axis=-1, keepdims=True) + EPS)
    return x * r, r


def _rms_bwd(xn, r, g, dy):
    gd = dy * g
    return r * (gd - xn * jnp.mean(gd * xn, axis=-1, keepdims=True)), dy * xn


def _ln_stats(x):
    mu = jnp.mean(x, axis=-1, keepdims=True)
    xc = x - mu
    rstd = lax.rsqrt(jnp.mean(xc * xc, axis=-1, keepdims=True) + EPS)
    return xc * rstd, rstd


def _ln_bwd(xhat, rstd, g, dy):
    dxh = dy * g
    return rstd * (dxh - jnp.mean(dxh, axis=-1, keepdims=True) - xhat * jnp.mean(dxh * xhat, axis=-1, keepdims=True))


def _rowsum(x):
    return jnp.sum(x, axis=0, keepdims=True)


def _tile(t, want):
    return min(t, want)


def _full(shape):
    n = len(shape)
    return pl.BlockSpec(shape, lambda *_: (0,) * n)


def _resident(shape):
    n = len(shape)
    return pl.BlockSpec(shape, lambda *_: (0,) * n, pipeline_mode=pl.Buffered(1))


class _Payload:
    def __init__(self, operands, out_shape, aliases, scratch, start, finish):
        self.operands, self.out_shape, self.aliases, self.scratch = list(operands), list(out_shape), aliases, scratch
        self.start, self.finish = start, finish
        self.results = None
        self.parts = None

    def deliver(self, results):
        self.results = list(results)
        if self.parts:
            a, b, ma = self.parts
            a.deliver(self.results[:ma])
            b.deliver(self.results[ma:])


def _pcall(body, *, name, grid, in_specs, out_specs, out_shape, operands, scratch_shapes=(), comm=None, aliases=None):
    single = not isinstance(out_shape, (list, tuple))
    out_specs = [out_specs] if single else list(out_specs)
    out_shape = [out_shape] if single else list(out_shape)
    aliases = dict(aliases or {})
    if comm is None:
        res = pl.pallas_call(
            body, name=name, grid=grid, in_specs=list(in_specs), out_specs=out_specs, out_shape=out_shape,
            scratch_shapes=list(scratch_shapes), input_output_aliases=aliases,
            compiler_params=_params(len(grid)))(*operands)
        return res[0] if single else res
    n_in, n_out, n_scr = len(in_specs), len(out_shape), len(scratch_shapes)
    ci, co = len(comm.operands), len(comm.out_shape)

    def hosted(*refs):
        bounds = [0, n_in, n_in + ci, n_in + ci + n_out, n_in + ci + n_out + co, n_in + ci + n_out + co + n_scr]
        a, b, c_, d_, s_ = [refs[lo:hi] for lo, hi in zip(bounds[:-1], bounds[1:])]
        t_ = refs[bounds[-1]:]
        ids = [pl.program_id(q) for q in range(len(grid))]
        first = functools.reduce(jnp.logical_and, [i == 0 for i in ids])
        last = functools.reduce(jnp.logical_and, [i == pl.num_programs(q) - 1 for q, i in enumerate(ids)])

        @pl.when(first)
        def _():
            comm.start(b, d_, t_)

        body(*a, *c_, *s_)

        @pl.when(last)
        def _():
            comm.finish(b, d_, t_)

    res = pl.pallas_call(
        hosted, name=name, grid=grid, in_specs=list(in_specs) + [ANY] * ci, out_specs=out_specs + [ANY] * co,
        out_shape=out_shape + comm.out_shape, scratch_shapes=list(scratch_shapes) + list(comm.scratch),
        input_output_aliases={**aliases, **{n_in + i: n_out + o for i, o in comm.aliases.items()}},
        compiler_params=pltpu.CompilerParams(dimension_semantics=("arbitrary",) * len(grid),
                                             vmem_limit_bytes=VMEM_LIMIT, has_side_effects=True),
    )(*operands, *comm.operands)
    comm.deliver(res[n_out:])
    res = res[:n_out]
    return res[0] if single else res


def _run_payload(comm, name):
    ci, co = len(comm.operands), len(comm.out_shape)

    def body(*refs):
        b, d_, t_ = refs[:ci], refs[ci:ci + co], refs[ci + co:]
        comm.start(b, d_, t_)
        comm.finish(b, d_, t_)

    res = pl.pallas_call(
        body, name=name, in_specs=[ANY] * ci, out_specs=[ANY] * co, out_shape=comm.out_shape,
        input_output_aliases=dict(comm.aliases), scratch_shapes=list(comm.scratch),
        compiler_params=pltpu.CompilerParams(has_side_effects=True))(*comm.operands)
    comm.deliver(res)
    return comm.results


def _take(hosts, key):
    return hosts[key]() if hosts and key in hosts else None


def _norm_mm(h, g, w4, name, comm=None):
    t, d = h.shape
    n = w4.shape[2]
    tm = _tile(t, ROW_TILE)

    step = _lane_block(n, 1024)

    def body(h_ref, g_ref, w_ref, o_ref, hn_ref):
        xn, _ = _rms_stats(h_ref[...])
        hn = (xn * g_ref[...]).astype(BF16)
        hn_ref[...] = hn
        for j in range(N_CHIPS):
            for c0 in range(0, n, step):
                o_ref[:, j * n + c0:j * n + c0 + step] = jnp.dot(
                    hn, w_ref[j, :, c0:c0 + step], preferred_element_type=F32).astype(BF16)

    return _pcall(
        body, name=name, grid=(t // tm,),
        in_specs=[pl.BlockSpec((tm, d), lambda i: (i, 0)), _full((1, d)), _resident(w4.shape)],
        out_specs=[pl.BlockSpec((tm, N_CHIPS * n), lambda i: (i, 0)), pl.BlockSpec((tm, d), lambda i: (i, 0))],
        out_shape=[jax.ShapeDtypeStruct((t, N_CHIPS * n), BF16), jax.ShapeDtypeStruct((t, d), BF16)],
        operands=(h, g, w4), comm=comm)


def _sgu_mask():
    ii = lax.broadcasted_iota(jnp.int32, (SGU_BLOCK, SGU_BLOCK), 0) // CHUNK
    jj = lax.broadcasted_iota(jnp.int32, (SGU_BLOCK, SGU_BLOCK), 1) // CHUNK
    return jj <= ii


def _sgu_fwd(proj, wm, bs3, gv, bv, name, comm=None):
    t = proj.shape[0]
    d = D_MODEL
    tm = _tile(t, ROW_TILE)
    hd = d // SGU_HEADS

    def body(zu_ref, zv_ref, wm_ref, bs_ref, gv_ref, bv_ref, o_ref):
        mask = _sgu_mask()
        for blk in range(tm // SGU_BLOCK):
            rows = pl.ds(blk * SGU_BLOCK, SGU_BLOCK)
            u = _gelu(zu_ref[rows, :].astype(F32))
            xhat, _ = _ln_stats(_gelu(zv_ref[rows, :].astype(F32)))
            vn = (xhat * gv_ref[...] + bv_ref[...]).astype(BF16)
            for hh in range(SGU_HEADS):
                cols = slice(hh * hd, (hh + 1) * hd)
                wmh = jnp.where(mask, wm_ref[hh], 0.0).astype(BF16)
                mixed = jnp.dot(wmh, vn[:, cols], preferred_element_type=F32) + bs_ref[hh]
                o_ref[rows, cols] = (u[:, cols] * mixed).astype(BF16)

    return _pcall(
        body, name=name, grid=(t // tm,),
        in_specs=[pl.BlockSpec((tm, d), lambda i: (i, 0)), pl.BlockSpec((tm, d), lambda i: (i, 1)),
                  _full(wm.shape), _full(bs3.shape), _full(gv.shape), _full(bv.shape)],
        out_specs=pl.BlockSpec((tm, d), lambda i: (i, 0)),
        out_shape=jax.ShapeDtypeStruct((t, d), BF16),
        operands=(proj, proj, wm, bs3, gv, bv), comm=comm)


def _conv_taps(scr_ref, r0, c0, base, weight):
    n = CONV_ROWS + CONV_HALO
    win = scr_ref[pl.ds(r0, n), pl.ds(c0, CONV_LANES)]
    acc = None
    for r in range(8):
        rolled = win if r == 0 else pltpu.roll(win, n - r, 0)
        for q in range((CONV_HALO + 7) // 8 + 1):
            k = 8 * q + r - base
            if 0 <= k < CONV_WIDTH and 8 * q + CONV_ROWS <= n:
                term = weight(k) * rolled[8 * q:8 * q + CONV_ROWS]
                acc = term if acc is None else acc + term
    return acc


def _glu_rows(a_ref, g_ref):
    return a_ref[...].astype(F32) * _sigmoid(g_ref[...].astype(F32))


def _conv_into(scr_ref, cv_ref, w_ref, tm, base, flip):
    def chunk(ci, carry):
        r0 = pl.multiple_of(ci * CONV_ROWS, CONV_ROWS)
        for c0 in range(0, D_MODEL, CONV_LANES):
            def weight(k, c0=c0):
                kk = CONV_WIDTH - 1 - k if flip else k
                return w_ref[kk:kk + 1, c0:c0 + CONV_LANES]
            cv_ref[pl.ds(r0, CONV_ROWS), pl.ds(c0, CONV_LANES)] = _conv_taps(scr_ref, r0, c0, base, weight)
        return carry

    lax.fori_loop(0, tm // CONV_ROWS, chunk, 0)


def _conv_specs(t, tm, d):
    hb = tm // CONV_HALO
    main = [pl.BlockSpec((tm, d), lambda i: (i, 2)), pl.BlockSpec((tm, d), lambda i: (i, 3))]
    halo = [pl.BlockSpec((CONV_HALO, d), lambda i: (jnp.maximum(i * hb - 1, 0), 2)),
            pl.BlockSpec((CONV_HALO, d), lambda i: (jnp.maximum(i * hb - 1, 0), 3))]
    return main, halo


def _fill_glu_history(scr_ref, a_ref, g_ref, ah_ref, gh_ref, tm):
    hist = _glu_rows(ah_ref, gh_ref)
    scr_ref[0:CONV_HALO, :] = jnp.where(pl.program_id(0) > 0, hist, 0.0)
    scr_ref[CONV_HALO:CONV_HALO + tm, :] = _glu_rows(a_ref, g_ref)


_CONV_BASE = CONV_HALO - (CONV_WIDTH - 1)


def _conv_fwd(proj, wdw, bdw, gln, bln, name, comm=None):
    t = proj.shape[0]
    d = D_MODEL
    tm = _tile(t, ROW_TILE)
    main, halo = _conv_specs(t, tm, d)

    def body(a_ref, g_ref, ah_ref, gh_ref, w_ref, b_ref, gl_ref, bl_ref, o_ref, cv_ref, scr_ref):
        _fill_glu_history(scr_ref, a_ref, g_ref, ah_ref, gh_ref, tm)
        _conv_into(scr_ref, cv_ref, w_ref, tm, _CONV_BASE, False)
        xhat, _ = _ln_stats(cv_ref[...] + b_ref[...])
        cn = xhat * gl_ref[...] + bl_ref[...]
        o_ref[...] = (cn * _sigmoid(cn)).astype(BF16)

    row = pl.BlockSpec((tm, d), lambda i: (i, 0))
    return _pcall(
        body, name=name, grid=(t // tm,),
        in_specs=main + halo + [_full(wdw.shape), _full(bdw.shape), _full(gln.shape), _full(bln.shape)],
        out_specs=[row, row],
        out_shape=[jax.ShapeDtypeStruct((t, d), BF16), jax.ShapeDtypeStruct((t, d), F32)],
        scratch_shapes=[pltpu.VMEM((tm + CONV_HALO, d), F32)],
        operands=(proj, proj, proj, proj, wdw, bdw, gln, bln), comm=comm)


def _pool_fill(scr_ref, z_ref, zh_ref, tm):
    scr_ref[0:POOL_BLOCK, :] = jnp.where(pl.program_id(0) > 0, zh_ref[...], jnp.zeros_like(zh_ref))
    scr_ref[POOL_BLOCK:POOL_BLOCK + tm, :] = z_ref[...]


def _pool_count(t0, rows, w):
    pos = (t0 + lax.broadcasted_iota(jnp.int32, (rows, 1), 0) + 1).astype(F32)
    return jnp.minimum(pos, float(w))


def _band(w, leading):
    i = lax.broadcasted_iota(jnp.int32, (POOL_BLOCK, 2 * POOL_BLOCK), 0)
    j = lax.broadcasted_iota(jnp.int32, (POOL_BLOCK, 2 * POOL_BLOCK), 1)
    off = j - i if leading else POOL_BLOCK + i - j
    return jnp.where((off >= 0) & (off < w), 1.0, 0.0).astype(BF16)


def _window_sums(refs, band, tm, cols):
    blocks = []
    for b in range(tm // POOL_BLOCK):
        rows = pl.ds(b * POOL_BLOCK, 2 * POOL_BLOCK)
        acc = None
        for ref in refs:
            term = jnp.dot(band, ref[rows, cols], preferred_element_type=F32)
            acc = term if acc is None else acc + term
        blocks.append(acc)
    return blocks[0] if len(blocks) == 1 else jnp.concatenate(blocks, axis=0)


def _pooled_group(scr_ref, gi, w, tm, t0):
    cols = pl.ds(gi * POOL_GROUP, POOL_GROUP)
    sums = _window_sums([scr_ref], _band(w, False), tm, cols)
    return sums / _pool_count(t0, tm, w) - scr_ref[pl.ds(POOL_BLOCK, tm), cols].astype(F32)


def _pool_specs(tm, d):
    hb = tm // POOL_BLOCK
    return [pl.BlockSpec((tm, d), lambda i: (i, 4)),
            pl.BlockSpec((POOL_BLOCK, d), lambda i: (jnp.maximum(i * hb - 1, 0), 4))]


def _pool_fwd(proj, wpool, spool, name, comm=None):
    t = proj.shape[0]
    d = D_MODEL
    tm = _tile(t, ROW_TILE)

    def body(z_ref, zh_ref, w_ref, s_ref, o_ref, scr_ref):
        _pool_fill(scr_ref, z_ref, zh_ref, tm)
        t0 = pl.program_id(0) * tm
        for gi, w in enumerate(POOL_WINDOWS):
            cols = slice(gi * POOL_GROUP, (gi + 1) * POOL_GROUP)
            pooled = _pooled_group(scr_ref, gi, w, tm, t0)
            o_ref[:, cols] = (_dot(pooled, w_ref[gi]) * s_ref[:, cols]).astype(BF16)

    return _pcall(
        body, name=name, grid=(t // tm,),
        in_specs=_pool_specs(tm, d) + [_full(wpool.shape), _full(spool.shape)],
        out_specs=pl.BlockSpec((tm, d), lambda i: (i, 0)),
        out_shape=jax.ShapeDtypeStruct((t, d), BF16),
        scratch_shapes=[pltpu.VMEM((tm + POOL_BLOCK, d), BF16)],
        operands=(proj, proj, wpool, spool), comm=comm)


def _merge_fwd(proj, sg, cs, ps, wa, wb, wc, name, comm=None):
    t = proj.shape[0]
    d = D_MODEL
    tm = _tile(t, ROW_TILE)

    def body(za_ref, zb_ref, zc_ref, sg_ref, cs_ref, ps_ref, wa_ref, wb_ref, wc_ref, ba_ref, bb_ref, bc_ref, m_ref):
        merged = None
        for z_ref, x_ref, w_ref, b_ref in ((za_ref, sg_ref, wa_ref, ba_ref), (zb_ref, cs_ref, wb_ref, bb_ref),
                                           (zc_ref, ps_ref, wc_ref, bc_ref)):
            br = jnp.dot(x_ref[...], w_ref[...], preferred_element_type=F32)
            b_ref[...] = br.astype(BF16)
            term = _sigmoid(z_ref[...].astype(F32)) * br
            merged = term if merged is None else merged + term
        m_ref[...] = merged.astype(BF16)

    row = pl.BlockSpec((tm, d), lambda i: (i, 0))
    wspec = _resident((d, d))
    return _pcall(
        body, name=name, grid=(t // tm,),
        in_specs=[pl.BlockSpec((tm, d), lambda i: (i, 5)), pl.BlockSpec((tm, d), lambda i: (i, 6)),
                  pl.BlockSpec((tm, d), lambda i: (i, 7)), row, row, row, wspec, wspec, wspec],
        out_specs=[row, row, row, row],
        out_shape=[jax.ShapeDtypeStruct((t, d), BF16)] * 4,
        operands=(proj, proj, proj, sg, cs, ps, wa, wb, wc), comm=comm)


def _mm_norm_res(a, w, g, hres, name, comm=None):
    t, k = a.shape
    d = w.shape[1]
    tm = _tile(t, ROW_TILE_LIGHT)

    def body(a_ref, w_ref, g_ref, h_ref, y_ref, o_ref):
        y = jnp.dot(a_ref[...], w_ref[...], preferred_element_type=F32)
        y_ref[...] = y
        yn, _ = _rms_stats(y)
        o_ref[...] = h_ref[...] + yn * g_ref[...]

    row = pl.BlockSpec((tm, d), lambda i: (i, 0))
    return _pcall(
        body, name=name, grid=(t // tm,),
        in_specs=[pl.BlockSpec((tm, k), lambda i: (i, 0)), _resident(w.shape), _full(g.shape), row],
        out_specs=[row, row],
        out_shape=[jax.ShapeDtypeStruct((t, d), F32)] * 2,
        operands=(a, w, g, hres), comm=comm)


def _ffn_in(h, g, w4, name, comm=None):
    t, d = h.shape
    n = w4.shape[2]
    tm = _tile(t, ROW_TILE)
    nj = D_FF // n

    def body(h_ref, g_ref, w_ref, fg_ref, fu_ref, act_ref, hn_ref):
        xn, _ = _rms_stats(h_ref[...])
        hn = (xn * g_ref[...]).astype(BF16)
        hn_ref[...] = hn
        for j in range(nj):
            cols = slice(j * n, (j + 1) * n)
            fg = jnp.dot(hn, w_ref[j], preferred_element_type=F32)
            fu = jnp.dot(hn, w_ref[j + nj], preferred_element_type=F32)
            fg_ref[:, cols] = fg.astype(BF16)
            fu_ref[:, cols] = fu.astype(BF16)
            act_ref[:, cols] = (fg * _sigmoid(fg) * fu).astype(BF16)

    wide = pl.BlockSpec((tm, D_FF), lambda i: (i, 0))
    return _pcall(
        body, name=name, grid=(t // tm,),
        in_specs=[pl.BlockSpec((tm, d), lambda i: (i, 0)), _full((1, d)), _resident(w4.shape)],
        out_specs=[wide, wide, wide, pl.BlockSpec((tm, d), lambda i: (i, 0))],
        out_shape=[jax.ShapeDtypeStruct((t, D_FF), BF16)] * 3 + [jax.ShapeDtypeStruct((t, d), BF16)],
        operands=(h, g, w4), comm=comm)


def _ple_fwd(h, p, wg, wp, name, comm=None):
    t, d = h.shape
    tm = _tile(t, ROW_TILE_LIGHT)

    def body(h_ref, p_ref, wg_ref, wp_ref, o_ref, q_ref, e_ref):
        hh = h_ref[...]
        q = _dot(hh, wg_ref[...])
        e = _dot(p_ref[...], wp_ref[...])
        q_ref[...] = q.astype(BF16)
        e_ref[...] = e.astype(BF16)
        o_ref[...] = hh + _sigmoid(q) * e

    row = pl.BlockSpec((tm, d), lambda i: (i, 0))
    return _pcall(
        body, name=name, grid=(t // tm,),
        in_specs=[row, pl.BlockSpec((tm, p.shape[1]), lambda i: (i, 0)), _resident(wg.shape), _resident(wp.shape)],
        out_specs=[row, row, row],
        out_shape=[jax.ShapeDtypeStruct((t, d), F32), jax.ShapeDtypeStruct((t, d), BF16),
                   jax.ShapeDtypeStruct((t, d), BF16)],
        operands=(h, p, wg, wp), comm=comm)


def _loss_head(y, target, name):
    t, d = y.shape
    tm = _tile(t, ROW_TILE_LIGHT)

    def body(y_ref, t_ref, dy_ref, l_ref):
        @pl.when(pl.program_id(0) == 0)
        def _():
            l_ref[...] = jnp.zeros_like(l_ref)

        err = y_ref[...] - t_ref[...]
        dy_ref[...] = err * (1.0 / d)
        l_ref[...] += jnp.sum(err * err, keepdims=True)[:, :1] * jnp.ones((1, 128), F32)

    row = pl.BlockSpec((tm, d), lambda i: (i, 0))
    return pl.pallas_call(
        body, name=name, grid=(t // tm,),
        in_specs=[row, row], out_specs=[row, _full((1, 128))],
        out_shape=[jax.ShapeDtypeStruct((t, d), F32), jax.ShapeDtypeStruct((1, 128), F32)],
        compiler_params=_params(1))(y, target)


def _flush_slots(acc_ref, out_ref, slots, sem_ref):
    rows = out_ref.shape[2]

    @pl.when(pl.program_id(0) == pl.num_programs(0) - 1)
    def _():
        copies = [pltpu.make_async_copy(acc_ref.at[k, pl.ds(j * rows, rows)], out_ref.at[j, slot],
                                        sem_ref.at[N_CHIPS * k + j])
                  for k, slot in enumerate(slots) for j in range(N_CHIPS)]
        for cp in copies:
            cp.start()
        for cp in copies:
            cp.wait()


def _ple_bwd(dh, q, e, wg, h_in, p, slot, n_slots, name):
    t, d = dh.shape
    tm = _tile(t, ROW_TILE)
    rows = d // N_CHIPS

    def body(dh_ref, q_ref, e_ref, wg_ref, h_ref, p_ref, o_ref, sq_ref, dwp_ref, acc_ref, sem_ref):
        @pl.when(pl.program_id(0) == 0)
        def _():
            acc_ref[...] = jnp.zeros_like(acc_ref)
            dwp_ref[...] = jnp.zeros_like(dwp_ref)

        dh_ = dh_ref[...]
        s = _sigmoid(q_ref[...].astype(F32))
        dq = (dh_ * e_ref[...].astype(F32) * s * (1.0 - s)).astype(BF16)
        o_ref[...] = dh_ + _dot_nt(dq, wg_ref[...])
        acc_ref[0] += _dot_tn(h_ref[...], dq)
        dwp_ref[...] += _dot_tn(p_ref[...], dh_ * s)
        _flush_slots(acc_ref, sq_ref, (slot,), sem_ref)

    row = pl.BlockSpec((tm, d), lambda i: (i, 0))
    return _pcall(
        body, name=name, grid=(t // tm,),
        in_specs=[row, row, row, _resident(wg.shape), row, pl.BlockSpec((tm, p.shape[1]), lambda i: (i, 0))],
        out_specs=[row, ANY, _full((p.shape[1], d))],
        out_shape=[jax.ShapeDtypeStruct((t, d), F32), jax.ShapeDtypeStruct((N_CHIPS, n_slots, rows, d), F32),
                   jax.ShapeDtypeStruct((p.shape[1], d), F32)],
        scratch_shapes=[pltpu.VMEM((1, d, d), F32), pltpu.SemaphoreType.DMA((N_CHIPS,))],
        operands=(dh, q, e, wg, h_in, p))


def _ffn_out_bwd(dh, f, g, fg, fu, act, w, name, comm=None):
    t, d = dh.shape
    tm = _tile(t, ROW_TILE_HEAVY)

    def body(dh_ref, f_ref, g_ref, fg_ref, fu_ref, act_ref, w_ref, dff_ref, dg_ref, dw_ref, acc_ref, sem_ref):
        @pl.when(pl.program_id(0) == 0)
        def _():
            dg_ref[...] = jnp.zeros_like(dg_ref)
            acc_ref[...] = jnp.zeros_like(acc_ref)

        fn, r = _rms_stats(f_ref[...])
        df, dgt = _rms_bwd(fn, r, g_ref[...], dh_ref[...])
        dg_ref[...] += _rowsum(dgt)
        df = df.astype(BF16)
        acc_ref[...] += _dot_tn(act_ref[...], df)
        for c0 in range(0, D_FF, FF_CHUNK):
            cols = slice(c0, c0 + FF_CHUNK)
            dact = _dot_nt(df, w_ref[cols, :])
            fg_ = fg_ref[:, cols].astype(F32)
            s = _sigmoid(fg_)
            gs = fg_ * s
            dff_ref[:, cols] = (dact * fu_ref[:, cols].astype(F32) * (s + gs - gs * s)).astype(BF16)
            dff_ref[:, D_FF + c0:D_FF + c0 + FF_CHUNK] = (dact * gs).astype(BF16)

        @pl.when(pl.program_id(0) == pl.num_programs(0) - 1)
        def _():
            cp = pltpu.make_async_copy(acc_ref, dw_ref, sem_ref.at[0])
            cp.start()
            cp.wait()

    row = pl.BlockSpec((tm, d), lambda i: (i, 0))
    wide = pl.BlockSpec((tm, D_FF), lambda i: (i, 0))
    return _pcall(
        body, name=name, grid=(t // tm,),
        in_specs=[row, row, _full(g.shape), wide, wide, wide, _resident(w.shape)],
        out_specs=[pl.BlockSpec((tm, 2 * D_FF), lambda i: (i, 0)), _full((1, d)), ANY],
        out_shape=[jax.ShapeDtypeStruct((t, 2 * D_FF), BF16), jax.ShapeDtypeStruct((1, d), F32),
                   jax.ShapeDtypeStruct((D_FF, d), F32)],
        scratch_shapes=[pltpu.VMEM((D_FF, d), F32), pltpu.SemaphoreType.DMA((1,))],
        operands=(dh, f, g, fg, fu, act, w), comm=comm)


def _in_bwd(pieces, w4, unit, h, g, dres, tm, name, comm=None):
    t, d = h.shape
    tm = _tile(t, tm)
    per_chunk = w4.shape[2] // unit
    n_p = len(pieces)

    def body(*refs):
        p_refs = refs[:n_p]
        w_ref, h_ref, g_ref, r_ref, o_ref, dg_ref = refs[n_p:]

        @pl.when(pl.program_id(0) == 0)
        def _():
            dg_ref[...] = jnp.zeros_like(dg_ref)

        acc = None
        u = 0
        for p_ref, (_, nu) in zip(p_refs, pieces):
            for k in range(nu):
                lanes = slice((u % per_chunk) * unit, (u % per_chunk + 1) * unit)
                term = _dot_nt(p_ref[:, k * unit:(k + 1) * unit], w_ref[u // per_chunk, :, lanes])
                acc = term if acc is None else acc + term
                u += 1
        xn, r = _rms_stats(h_ref[...])
        dx, dgt = _rms_bwd(xn, r, g_ref[...], acc)
        dg_ref[...] += _rowsum(dgt)
        o_ref[...] = r_ref[...] + dx

    row = pl.BlockSpec((tm, d), lambda i: (i, 0))
    return _pcall(
        body, name=name, grid=(t // tm,),
        in_specs=[pl.BlockSpec((tm, a.shape[1]), lambda i: (i, 0)) for a, _ in pieces]
        + [_resident(w4.shape), row, _full((1, d)), row],
        out_specs=[row, _full((1, d))],
        out_shape=[jax.ShapeDtypeStruct((t, d), F32), jax.ShapeDtypeStruct((1, d), F32)],
        operands=(*[a for a, _ in pieces], w4, h, g, dres), comm=comm)


def _lane_block(n, cap):
    return max(b for b in range(128, min(n, cap) + 1, 128) if n % b == 0)


def _dw_cols(x, pieces, unit, per_chunk, name):
    t, m = x.shape
    tk = _tile(t, TOKEN_TILE)
    offs, total = [], 0
    for _, nu in pieces:
        offs.append(total)
        total += nu

    def body(x_ref, *refs):
        o_ref = refs[-1]
        u = pl.program_id(0)

        @pl.when(pl.program_id(1) == 0)
        def _():
            o_ref[...] = jnp.zeros_like(o_ref)

        for p_ref, off, (_, nu) in zip(refs[:-1], offs, pieces):
            @pl.when((u >= off) & (u < off + nu))
            def _(p_ref=p_ref):
                o_ref[...] += _dot_tn(x_ref[...], p_ref[...])

    def piece_spec(off, nu):
        def index(u, k):
            mine = (u >= off) & (u < off + nu)
            return jnp.where(mine, k, 0), jnp.clip(u - off, 0, nu - 1)
        return pl.BlockSpec((tk, unit), index)

    return pl.pallas_call(
        body, name=name, grid=(total, t // tk),
        in_specs=[pl.BlockSpec((tk, m), lambda u, k: (k, 0))] + [piece_spec(o, nu) for o, (_, nu) in zip(offs, pieces)],
        out_specs=pl.BlockSpec((None, None, m, unit), lambda u, k: (u // per_chunk, 0, 0, u % per_chunk)),
        out_shape=jax.ShapeDtypeStruct((N_CHIPS, 1, m, per_chunk * unit), F32),
        compiler_params=_params(2))(x, *[a for a, _ in pieces])


def _merge_bwd(dh, mo, g, proj, br, xs, merged, w_out, ws, sq, slots, name, comm=None):
    t, d = dh.shape
    tm = _tile(t, ROW_TILE_HEAVY)
    rows = d // N_CHIPS

    def body(dh_ref, mo_ref, g_ref, za_ref, zb_ref, zc_ref, ba_ref, bb_ref, bc_ref, xa_ref, xb_ref, xc_ref, m_ref,
             wo_ref, wa_ref, wb_ref, wc_ref, sq_in_ref, dz_ref, dsg_ref, dcs_ref, dps_ref, dg_ref, sq_ref,
             acc_ref, sem_ref):
        @pl.when(pl.program_id(0) == 0)
        def _():
            dg_ref[...] = jnp.zeros_like(dg_ref)
            acc_ref[...] = jnp.zeros_like(acc_ref)

        mon, r = _rms_stats(mo_ref[...])
        dmo, dgt = _rms_bwd(mon, r, g_ref[...], dh_ref[...])
        dg_ref[...] += _rowsum(dgt)
        dmo = dmo.astype(BF16)
        acc_ref[0] += _dot_tn(m_ref[...], dmo)
        dmerged = _dot_nt(dmo, wo_ref[...])
        branches = ((za_ref, ba_ref, xa_ref, wa_ref, dsg_ref), (zb_ref, bb_ref, xb_ref, wb_ref, dcs_ref),
                    (zc_ref, bc_ref, xc_ref, wc_ref, dps_ref))
        for j, (z_ref, b_ref, x_ref, w_ref, dx_ref) in enumerate(branches):
            gate = _sigmoid(z_ref[...].astype(F32))
            dbr = (dmerged * gate).astype(BF16)
            dz_ref[:, j * d:(j + 1) * d] = (dmerged * b_ref[...].astype(F32) * gate * (1.0 - gate)).astype(BF16)
            dx_ref[...] = _dot_nt(dbr, w_ref[...]).astype(BF16)
            acc_ref[1 + j] += _dot_tn(x_ref[...], dbr)
        _flush_slots(acc_ref, sq_ref, slots, sem_ref)

    row = pl.BlockSpec((tm, d), lambda i: (i, 0))
    wspec = _resident((d, d))
    bf = jax.ShapeDtypeStruct((t, d), BF16)
    n_in = 18
    return _pcall(
        body, name=name, grid=(t // tm,),
        in_specs=[row, row, _full(g.shape), pl.BlockSpec((tm, d), lambda i: (i, 5)),
                  pl.BlockSpec((tm, d), lambda i: (i, 6)), pl.BlockSpec((tm, d), lambda i: (i, 7)),
                  row, row, row, row, row, row, row, wspec, wspec, wspec, wspec, ANY],
        out_specs=[pl.BlockSpec((tm, 3 * d), lambda i: (i, 0)), row, row, row, _full((1, d)), ANY],
        out_shape=[jax.ShapeDtypeStruct((t, 3 * d), BF16), bf, bf, bf, jax.ShapeDtypeStruct((1, d), F32),
                   jax.ShapeDtypeStruct(sq.shape, sq.dtype)],
        scratch_shapes=[pltpu.VMEM((4, d, d), F32), pltpu.SemaphoreType.DMA((4 * N_CHIPS,))],
        operands=(dh, mo, g, proj, proj, proj, *br, *xs, merged, w_out, *ws, sq), comm=comm,
        aliases={n_in - 1: 5})


def _sgu_bwd(proj, dsg, wm, bs3, gv, bv, name, comm=None):
    t = proj.shape[0]
    d = D_MODEL
    tm = _tile(t, ROW_TILE_HEAVY)
    hd = d // SGU_HEADS

    def body(zu_ref, zv_ref, d_ref, wm_ref, bs_ref, gv_ref, bv_ref, dz_ref, dwm_ref, dbs_ref, dgv_ref, dbv_ref,
             dvn_ref):
        @pl.when(pl.program_id(0) == 0)
        def _():
            dwm_ref[...] = jnp.zeros_like(dwm_ref)
            dbs_ref[...] = jnp.zeros_like(dbs_ref)
            dgv_ref[...] = jnp.zeros_like(dgv_ref)
            dbv_ref[...] = jnp.zeros_like(dbv_ref)

        mask = _sgu_mask()
        for blk in range(tm // SGU_BLOCK):
            rows = pl.ds(blk * SGU_BLOCK, SGU_BLOCK)
            u, du_dz = _gelu_and_grad(zu_ref[rows, :].astype(F32))
            v0, dv_dz = _gelu_and_grad(zv_ref[rows, :].astype(F32))
            xhat, rstd = _ln_stats(v0)
            vn = (xhat * gv_ref[...] + bv_ref[...]).astype(BF16)
            dsg = d_ref[rows, :].astype(F32)
            dmix = (dsg * u).astype(BF16)
            for hh in range(SGU_HEADS):
                cols = slice(hh * hd, (hh + 1) * hd)
                wmh = jnp.where(mask, wm_ref[hh], 0.0).astype(BF16)
                vb = vn[:, cols]
                mixed = jnp.dot(wmh, vb, preferred_element_type=F32) + bs_ref[hh]
                dz_ref[rows, cols] = (dsg[:, cols] * mixed * du_dz[:, cols]).astype(BF16)
                dmh = dmix[:, cols]
                dwm_ref[hh] += jnp.where(mask, _dot_nt(dmh, vb), 0.0)
                dbs_ref[hh] += jnp.sum(dmh.astype(F32), axis=1, keepdims=True)
                dvn_ref[:, cols] = _dot_tn(wmh, dmh)
            dvn = dvn_ref[...]
            dgv_ref[...] += _rowsum(dvn * xhat)
            dbv_ref[...] += _rowsum(dvn)
            dz_ref[rows, d:2 * d] = (_ln_bwd(xhat, rstd, gv_ref[...], dvn) * dv_dz).astype(BF16)

    return _pcall(
        body, name=name, grid=(t // tm,),
        in_specs=[pl.BlockSpec((tm, d), lambda i: (i, 0)), pl.BlockSpec((tm, d), lambda i: (i, 1)),
                  pl.BlockSpec((tm, d), lambda i: (i, 0)), _full(wm.shape), _full(bs3.shape), _full(gv.shape),
                  _full(bv.shape)],
        out_specs=[pl.BlockSpec((tm, 2 * d), lambda i: (i, 0)), _full(wm.shape), _full(bs3.shape), _full((1, d)),
                   _full((1, d))],
        out_shape=[jax.ShapeDtypeStruct((t, 2 * d), BF16), jax.ShapeDtypeStruct(wm.shape, F32),
                   jax.ShapeDtypeStruct(bs3.shape, F32), jax.ShapeDtypeStruct((1, d), F32),
                   jax.ShapeDtypeStruct((1, d), F32)],
        scratch_shapes=[pltpu.VMEM((SGU_BLOCK, d), F32)],
        operands=(proj, proj, dsg, wm, bs3, gv, bv), comm=comm)


def _conv_bwd_norm(proj, dcs, cv, bdw, gln, bln, name, comm=None):
    t = proj.shape[0]
    d = D_MODEL
    tm = _tile(t, ROW_TILE)
    main, halo = _conv_specs(t, tm, d)
    n_win = CONV_ROWS + CONV_HALO

    def body(a_ref, g_ref, ah_ref, gh_ref, dcs_ref, cv_ref, b_ref, gl_ref, bl_ref,
             dcv_ref, dw_ref, db_ref, dgl_ref, dbl_ref, scr_ref, dwacc_ref):
        @pl.when(pl.program_id(0) == 0)
        def _():
            dwacc_ref[...] = jnp.zeros_like(dwacc_ref)
            db_ref[...] = jnp.zeros_like(db_ref)
            dgl_ref[...] = jnp.zeros_like(dgl_ref)
            dbl_ref[...] = jnp.zeros_like(dbl_ref)

        _fill_glu_history(scr_ref, a_ref, g_ref, ah_ref, gh_ref, tm)
        xhat, rstd = _ln_stats(cv_ref[...] + b_ref[...])
        cn = xhat * gl_ref[...] + bl_ref[...]
        s = _sigmoid(cn)
        dcn = dcs_ref[...].astype(F32) * (s * (1.0 + cn * (1.0 - s)))
        dgl_ref[...] += _rowsum(dcn * xhat)
        dbl_ref[...] += _rowsum(dcn)
        dcv = _ln_bwd(xhat, rstd, gl_ref[...], dcn)
        db_ref[...] += _rowsum(dcv)
        dcv_ref[...] = dcv

        def chunk(ci, carry):
            r0 = pl.multiple_of(ci * CONV_ROWS, CONV_ROWS)
            for c0 in range(0, d, CONV_LANES):
                lanes = pl.ds(c0, CONV_LANES)
                win = scr_ref[pl.ds(r0, n_win), lanes]
                dchunk = dcv_ref[pl.ds(r0, CONV_ROWS), lanes]
                for r in range(8):
                    rolled = win if r == 0 else pltpu.roll(win, n_win - r, 0)
                    for q in range(n_win // 8):
                        k = 8 * q + r - _CONV_BASE
                        if 0 <= k < CONV_WIDTH and 8 * q + CONV_ROWS <= n_win:
                            prod = dchunk * rolled[8 * q:8 * q + CONV_ROWS]
                            part = prod[0:8]
                            for s8 in range(8, CONV_ROWS, 8):
                                part = part + prod[s8:s8 + 8]
                            dwacc_ref[pl.ds(8 * k, 8), lanes] += part
            return carry

        lax.fori_loop(0, tm // CONV_ROWS, chunk, 0)

        @pl.when(pl.program_id(0) == pl.num_programs(0) - 1)
        def _():
            dw_ref[...] = jnp.sum(dwacc_ref[...].reshape(CONV_HALO, 8, d), axis=1)

    row = pl.BlockSpec((tm, d), lambda i: (i, 0))
    vec = _full((1, d))
    return _pcall(
        body, name=name, grid=(t // tm,),
        in_specs=main + halo + [row, row, vec, vec, vec],
        out_specs=[row, _full((CONV_HALO, d)), vec, vec, vec],
        out_shape=[jax.ShapeDtypeStruct((t, d), F32), jax.ShapeDtypeStruct((CONV_HALO, d), F32)]
        + [jax.ShapeDtypeStruct((1, d), F32)] * 3,
        scratch_shapes=[pltpu.VMEM((tm + CONV_HALO, d), F32), pltpu.VMEM((8 * CONV_HALO, d), F32)],
        operands=(proj, proj, proj, proj, dcs, cv, bdw, gln, bln), comm=comm)


def _conv_bwd_taps(proj, dcv, wdw, name, comm=None):
    t = proj.shape[0]
    d = D_MODEL
    tm = _tile(t, ROW_TILE)
    hb = tm // CONV_HALO
    last_halo = t // CONV_HALO - 1

    def body(a_ref, g_ref, dcv_ref, dnext_ref, w_ref, dz_ref, scr_ref, dh_ref):
        scr_ref[0:tm, :] = dcv_ref[...]
        is_last = pl.program_id(0) == pl.num_programs(0) - 1
        scr_ref[tm:tm + CONV_HALO, :] = jnp.where(is_last, 0.0, dnext_ref[...])
        _conv_into(scr_ref, dh_ref, w_ref, tm, 0, True)
        dglu = dh_ref[...]
        a = a_ref[...].astype(F32)
        s = _sigmoid(g_ref[...].astype(F32))
        dz_ref[:, 0:d] = (dglu * s).astype(BF16)
        dz_ref[:, d:2 * d] = (dglu * a * s * (1.0 - s)).astype(BF16)

    return _pcall(
        body, name=name, grid=(t // tm,),
        in_specs=[pl.BlockSpec((tm, d), lambda i: (i, 2)), pl.BlockSpec((tm, d), lambda i: (i, 3)),
                  pl.BlockSpec((tm, d), lambda i: (i, 0)),
                  pl.BlockSpec((CONV_HALO, d), lambda i: (jnp.minimum((i + 1) * hb, last_halo), 0)),
                  _full(wdw.shape)],
        out_specs=pl.BlockSpec((tm, 2 * d), lambda i: (i, 0)),
        out_shape=jax.ShapeDtypeStruct((t, 2 * d), BF16),
        scratch_shapes=[pltpu.VMEM((tm + CONV_HALO, d), F32), pltpu.VMEM((tm, d), F32)],
        operands=(proj, proj, dcv, dcv, wdw), comm=comm)


def _pool_bwd(proj, dps, wpool, spool, name):
    t = proj.shape[0]
    d = D_MODEL
    tm = _tile(t, ROW_TILE)
    hb = tm // POOL_BLOCK
    last_halo = t // POOL_BLOCK - 1
    ext = tm + POOL_BLOCK

    def body(z_ref, zh_ref, d_ref, dnext_ref, w_ref, s_ref, dz_ref, dw_ref, ds_ref, scr_ref, dext_ref, hi_ref, lo_ref):
        @pl.when(pl.program_id(0) == 0)
        def _():
            dw_ref[...] = jnp.zeros_like(dw_ref)
            ds_ref[...] = jnp.zeros_like(ds_ref)

        _pool_fill(scr_ref, z_ref, zh_ref, tm)
        t0 = pl.program_id(0) * tm
        is_last = pl.program_id(0) == pl.num_programs(0) - 1
        dext_ref[0:tm, :] = d_ref[...].astype(F32)
        dext_ref[tm:ext, :] = jnp.where(is_last, 0.0, dnext_ref[...].astype(F32))
        for gi, w in enumerate(POOL_WINDOWS):
            cols = slice(gi * POOL_GROUP, (gi + 1) * POOL_GROUP)
            dps_ext = dext_ref[:, cols]
            dpm_ext = (dps_ext * s_ref[:, cols]).astype(BF16)
            dpooled_ext = _dot_nt(dpm_ext, w_ref[gi])
            dq = dpooled_ext / _pool_count(t0, ext, w)
            hi = dq.astype(BF16)
            hi_ref[...] = hi
            lo_ref[...] = (dq - hi.astype(F32)).astype(BF16)
            sums = _window_sums([hi_ref, lo_ref], _band(w, True), tm, slice(None))
            dz_ref[:, cols] = (sums - dpooled_ext[0:tm]).astype(BF16)
            pooled = _pooled_group(scr_ref, gi, w, tm, t0).astype(BF16)
            pm = jnp.dot(pooled, w_ref[gi], preferred_element_type=F32)
            ds_ref[:, cols] += _rowsum(dps_ext[0:tm] * pm)
            dw_ref[gi] += _dot_tn(pooled, dpm_ext[0:tm])

    return pl.pallas_call(
        body, name=name, grid=(t // tm,),
        in_specs=_pool_specs(tm, d) + [pl.BlockSpec((tm, d), lambda i: (i, 0)),
                                       pl.BlockSpec((POOL_BLOCK, d), lambda i: (jnp.minimum((i + 1) * hb, last_halo), 0)),
                                       _full(wpool.shape), _full(spool.shape)],
        out_specs=[pl.BlockSpec((tm, d), lambda i: (i, 0)), _full(wpool.shape), _full((1, d))],
        out_shape=[jax.ShapeDtypeStruct((t, d), BF16), jax.ShapeDtypeStruct(wpool.shape, F32),
                   jax.ShapeDtypeStruct((1, d), F32)],
        scratch_shapes=[pltpu.VMEM((tm + POOL_BLOCK, d), BF16), pltpu.VMEM((ext, d), F32),
                        pltpu.VMEM((ext, POOL_GROUP), BF16), pltpu.VMEM((ext, POOL_GROUP), BF16)],
        compiler_params=_params(1))(proj, proj, dps, dps, wpool, spool)


ANY = pl.BlockSpec(memory_space=pl.ANY)


def _mesh_pos():
    x, y, c = lax.axis_index("x"), lax.axis_index("y"), lax.axis_index("c")
    chips = [(1 - x, y), (x, 1 - y), (1 - x, 1 - y)]
    return x, y, c, chips


def _chip_of(xy):
    return 2 * xy[0] + xy[1]


def _half_view(a):
    return a.reshape(a.shape[:-2] + (2, a.shape[-2] // 2, a.shape[-1]))


def _same(arrs):
    return [jax.ShapeDtypeStruct(a.shape, a.dtype) for a in arrs]


def _in_place(n):
    return {g: g for g in range(n)}


def _sems(count):
    return [pltpu.SemaphoreType.DMA((count,)), pltpu.SemaphoreType.DMA((count,))]


def _gather_ici(bufs):
    n = len(bufs)

    def copy(buf, sems, g, j, chip):
        x, y, c, chips = _mesh_pos()
        slab = buf[g].at[chip, :, c]
        return pltpu.make_async_remote_copy(
            src_ref=slab, dst_ref=slab, send_sem=sems[0].at[3 * g + j], recv_sem=sems[1].at[3 * g + j],
            device_id=(*chips[j], c), device_id_type=MESH)

    def start(ins, buf, sems):
        x, y, c, chips = _mesh_pos()
        for g in range(n):
            for j in range(3):
                copy(buf, sems, g, j, 2 * x + y).start()

    def finish(ins, buf, sems):
        x, y, c, chips = _mesh_pos()
        for g in range(n):
            for j in range(3):
                copy(buf, sems, g, j, _chip_of(chips[j])).wait_recv()
        for g in range(n):
            for j in range(3):
                copy(buf, sems, g, j, 2 * x + y).wait_send()

    return _Payload(bufs, _same(bufs), _in_place(n), _sems(3 * n), start, finish)


def _gather_d2d(bufs):
    n = len(bufs)

    def copy(buf, sems, g, j, half):
        x, y, c, chips = _mesh_pos()
        slab = buf[g].at[_chip_of(chips[j]), :, half]
        return pltpu.make_async_remote_copy(
            src_ref=slab, dst_ref=slab, send_sem=sems[0].at[3 * g + j], recv_sem=sems[1].at[3 * g + j],
            device_id=(x, y, 1 - c), device_id_type=MESH)

    def start(ins, buf, sems):
        c = lax.axis_index("c")
        for g in range(n):
            for j in range(3):
                copy(buf, sems, g, j, c).start()

    def finish(ins, buf, sems):
        c = lax.axis_index("c")
        for g in range(n):
            for j in range(3):
                copy(buf, sems, g, j, 1 - c).wait_recv()
        for g in range(n):
            for j in range(3):
                copy(buf, sems, g, j, c).wait_send()

    return _Payload(bufs, _same(bufs), _in_place(n), _sems(3 * n), start, finish)


def _pair_exchange(grads):
    n = len(grads)

    def copy(src, dst, sems, g):
        x, y, c, _ = _mesh_pos()
        return pltpu.make_async_remote_copy(
            src_ref=src[g].at[:, :, 1 - c], dst_ref=dst[g], send_sem=sems[0].at[g], recv_sem=sems[1].at[g],
            device_id=(x, y, 1 - c), device_id_type=MESH)

    def start(src, dst, sems):
        for g in range(n):
            copy(src, dst, sems, g).start()

    def finish(src, dst, sems):
        for g in range(n):
            copy(src, dst, sems, g).wait()

    out_shape = [jax.ShapeDtypeStruct(g.shape[:2] + g.shape[3:], g.dtype) for g in grads]
    return _Payload(grads, out_shape, {}, _sems(n), start, finish)


def _chip_exchange(parts):
    n = len(parts)

    def copy(src, dst, sems, g, j, slot):
        x, y, c, chips = _mesh_pos()
        return pltpu.make_async_remote_copy(
            src_ref=src[g].at[_chip_of(chips[j])], dst_ref=dst[g].at[slot], send_sem=sems[0].at[3 * g + j],
            recv_sem=sems[1].at[3 * g + j], device_id=(*chips[j], c), device_id_type=MESH)

    def start(src, dst, sems):
        x, y, c, chips = _mesh_pos()
        for g in range(n):
            for j in range(3):
                copy(src, dst, sems, g, j, 2 * x + y).start()

    def finish(src, dst, sems):
        x, y, c, chips = _mesh_pos()
        for g in range(n):
            for j in range(3):
                copy(src, dst, sems, g, j, _chip_of(chips[j])).wait_recv()
        for g in range(n):
            for j in range(3):
                copy(src, dst, sems, g, j, 2 * x + y).wait_send()

    return _Payload(parts, _same(parts), {}, _sems(3 * n), start, finish)


def _pair_share(bufs):
    n = len(bufs)

    def copy(buf, sems, g, half):
        x, y, c, _ = _mesh_pos()
        slab = buf[g].at[:, :, half]
        return pltpu.make_async_remote_copy(
            src_ref=slab, dst_ref=slab, send_sem=sems[0].at[g], recv_sem=sems[1].at[g],
            device_id=(x, y, 1 - c), device_id_type=MESH)

    def start(ins, buf, sems):
        c = lax.axis_index("c")
        for g in range(n):
            copy(buf, sems, g, c).start()

    def finish(ins, buf, sems):
        c = lax.axis_index("c")
        for g in range(n):
            copy(buf, sems, g, 1 - c).wait_recv()
        for g in range(n):
            copy(buf, sems, g, c).wait_send()

    return _Payload(bufs, _same(bufs), _in_place(n), _sems(n), start, finish)


def _join(a, b):
    if a is None or b is None:
        return a or b
    na, ma = len(a.operands), len(a.out_shape)
    aliases = dict(a.aliases)
    aliases.update({na + i: ma + o for i, o in b.aliases.items()})
    ka = len(a.scratch)

    def start(ins, outs, sems):
        a.start(ins[:na], outs[:ma], sems[:ka])
        b.start(ins[na:], outs[ma:], sems[ka:])

    def finish(ins, outs, sems):
        a.finish(ins[:na], outs[:ma], sems[:ka])
        b.finish(ins[na:], outs[ma:], sems[ka:])

    joined = _Payload(a.operands + b.operands, a.out_shape + b.out_shape, aliases, list(a.scratch) + list(b.scratch),
                      start, finish)
    joined.parts = (a, b, ma)
    return joined


def _small_exchange(vec):
    def copy(src, dst, sems, k, slot):
        x, y, c, _ = _mesh_pos()
        peer = (x ^ (k >> 2), y ^ ((k >> 1) & 1), c ^ (k & 1))
        return pltpu.make_async_remote_copy(
            src_ref=src[0], dst_ref=dst[0].at[slot], send_sem=sems[0].at[k - 1], recv_sem=sems[1].at[k - 1],
            device_id=peer, device_id_type=MESH)

    def me():
        x, y, c, _ = _mesh_pos()
        return 4 * x + 2 * y + c

    def start(src, dst, sems):
        for k in range(1, 8):
            copy(src, dst, sems, k, me()).start()

    def finish(src, dst, sems):
        for k in range(1, 8):
            copy(src, dst, sems, k, me() ^ k).wait_recv()
        for k in range(1, 8):
            copy(src, dst, sems, k, me()).wait_send()

    return _Payload([vec], [jax.ShapeDtypeStruct((8,) + vec.shape, vec.dtype)], {}, _sems(7), start, finish)


def _small_sum(vec, landed, pos, name):
    r = vec.shape[0]

    def body(pos_ref, v_ref, l_ref, o_ref):
        k = pl.program_id(0)

        @pl.when(k == 0)
        def _():
            o_ref[...] = jnp.zeros_like(o_ref)

        @pl.when(k == pos_ref[POS_DEVICE])
        def _():
            o_ref[...] += v_ref[...]

        @pl.when(k != pos_ref[POS_DEVICE])
        def _():
            o_ref[...] += l_ref[...]

    def landed_index(k, pos_ref):
        me = pos_ref[POS_DEVICE]
        return jnp.where(k == me, (me + 1) % 8, k), 0, 0

    return pl.pallas_call(
        body, name=name,
        grid_spec=pltpu.PrefetchScalarGridSpec(
            num_scalar_prefetch=1, grid=(8,),
            in_specs=[pl.BlockSpec((r, 128), lambda k, pos_ref: (0, 0)), pl.BlockSpec((None, r, 128), landed_index)],
            out_specs=pl.BlockSpec((r, 128), lambda k, pos_ref: (0, 0))),
        out_shape=jax.ShapeDtypeStruct(vec.shape, F32),
        compiler_params=_params(1))(pos, vec, landed)


def _all_reduce_small(vec, name):
    r = vec.shape[0]

    def body(v_ref, o_ref, gath_ref, send_sem, recv_sem):
        x, y, c, _ = _mesh_pos()
        me = 4 * x + 2 * y + c
        gath_ref[me] = v_ref[...]
        copies = []
        for k in range(1, 8):
            peer = (x ^ (k >> 2), y ^ ((k >> 1) & 1), c ^ (k & 1))
            cp = pltpu.make_async_remote_copy(
                src_ref=v_ref, dst_ref=gath_ref.at[me], send_sem=send_sem.at[k - 1], recv_sem=recv_sem.at[k - 1],
                device_id=peer, device_id_type=MESH)
            cp.start()
            copies.append(cp)
        for k in range(1, 8):
            src_id = me ^ k
            pltpu.make_async_remote_copy(
                src_ref=v_ref, dst_ref=gath_ref.at[src_id], send_sem=send_sem.at[k - 1], recv_sem=recv_sem.at[k - 1],
                device_id=(x, y, c), device_id_type=MESH).wait_recv()
        for cp in copies:
            cp.wait_send()
        acc = gath_ref[0]
        for k in range(1, 8):
            acc = acc + gath_ref[k]
        o_ref[...] = acc

    return pl.pallas_call(
        body, name=name,
        in_specs=[pl.BlockSpec(memory_space=pltpu.VMEM)], out_specs=pl.BlockSpec(memory_space=pltpu.VMEM),
        out_shape=jax.ShapeDtypeStruct(vec.shape, F32),
        scratch_shapes=[pltpu.VMEM((8, r, 128), F32), pltpu.SemaphoreType.DMA((7,)), pltpu.SemaphoreType.DMA((7,))],
        compiler_params=pltpu.CompilerParams(has_side_effects=True, vmem_limit_bytes=VMEM_LIMIT))(vec)


def _row_block(rows, cols, mult=16):
    best = None
    for cand in range(mult, rows + 1, mult):
        if rows % cand == 0 and cand * cols * 4 <= EW_BLOCK_BYTES:
            best = cand
    return best or rows


POS_ME, POS_CORE, POS_DEVICE = 0, 4, 5


def _place(arrs, li, pos, dtype, name):
    s = len(arrs)
    _, rows, cols = arrs[0].shape
    rh = rows // 2
    tr = _row_block(rh, cols)
    nb = rh // tr

    def body(pos_ref, *refs):
        o_ref = refs[s]
        for j in range(s):
            @pl.when(pl.program_id(0) == j)
            def _(j=j):
                o_ref[...] = refs[j][...].astype(dtype)

    def in_spec(j):
        return pl.BlockSpec((None, tr, cols), lambda b, hf, i, pos_ref: (li, jnp.where(b == j, hf * nb + i, 0), 0))

    return pl.pallas_call(
        body, name=name,
        grid_spec=pltpu.PrefetchScalarGridSpec(
            num_scalar_prefetch=1, grid=(s, 2, nb), in_specs=[in_spec(j) for j in range(s)],
            out_specs=pl.BlockSpec((None, None, None, tr, cols),
                                   lambda b, hf, i, pos_ref: (pos_ref[POS_ME], b, hf, i, 0))),
        out_shape=jax.ShapeDtypeStruct((N_CHIPS, s, 2, rh, cols), dtype),
        compiler_params=_params(3))(pos, *arrs)


def _pair_sum(grad, recv, pos, out_dtype, name):
    _, s, rh, cols = recv.shape
    tr = _row_block(rh, cols)

    def body(pos_ref, g_ref, r_ref, o_ref):
        o_ref[...] = (g_ref[...] + r_ref[...]).astype(out_dtype)

    blk = (None, None, tr, cols)
    return pl.pallas_call(
        body, name=name,
        grid_spec=pltpu.PrefetchScalarGridSpec(
            num_scalar_prefetch=1, grid=(N_CHIPS, s, rh // tr),
            in_specs=[pl.BlockSpec((None, None, None, tr, cols),
                                   lambda a, b, i, pos_ref: (a, b, pos_ref[POS_CORE], i, 0)),
                      pl.BlockSpec(blk, lambda a, b, i, pos_ref: (a, b, i, 0))],
            out_specs=pl.BlockSpec(blk, lambda a, b, i, pos_ref: (a, b, i, 0))),
        out_shape=jax.ShapeDtypeStruct(recv.shape, out_dtype),
        compiler_params=_params(3))(pos, grad, recv)


def _chip_sum(part, landed, gbuf, li, n_layers, pos, name):
    _, s, rh, cols = part.shape
    tr = _row_block(rh, cols)

    def body(pos_ref, p_ref, a_ref, b_ref, c_ref, *rest):
        o_ref = rest[-1]
        o_ref[...] = ((p_ref[...].astype(F32) + a_ref[...].astype(F32)) + b_ref[...].astype(F32)) \
            + c_ref[...].astype(F32)

    def slot(k):
        return pl.BlockSpec((None, None, tr, cols), lambda b, i, pos_ref: (pos_ref[k], b, i, 0))

    in_specs = [slot(0), slot(1), slot(2), slot(3)]
    operands = [pos, part, landed, landed, landed]
    aliases = {}
    if gbuf is not None:
        in_specs.append(ANY)
        operands.append(gbuf)
        aliases = {len(operands) - 1: 0}
    return pl.pallas_call(
        body, name=name,
        grid_spec=pltpu.PrefetchScalarGridSpec(
            num_scalar_prefetch=1, grid=(s, rh // tr), in_specs=in_specs,
            out_specs=pl.BlockSpec((None, None, None, tr, cols),
                                   lambda b, i, pos_ref: (li, b, pos_ref[POS_CORE], i, 0))),
        out_shape=jax.ShapeDtypeStruct((n_layers, s, 2, rh, cols), F32),
        input_output_aliases=aliases,
        compiler_params=_params(2))(*operands)


def _adamw_math(w, g, m, v):
    m = ADAM_B1 * m + (1.0 - ADAM_B1) * g
    v = ADAM_B2 * v + (1.0 - ADAM_B2) * (g * g)
    m_hat = m / (1.0 - ADAM_B1 ** ADAM_STEP)
    v_hat = v / (1.0 - ADAM_B2 ** ADAM_STEP)
    delta = -ADAM_LR * (m_hat / (jnp.sqrt(v_hat) + ADAM_EPS) + ADAM_WD * w)
    return delta, m, v


def _adamw(w, g, slot, m, v, name):
    l, rows, cols = w.shape
    tr = _row_block(rows, cols, 8)

    def body(w_ref, g_ref, m_ref, v_ref, go_ref, d_ref, mo_ref, vo_ref):
        g_ = g_ref[...]
        delta, m_, v_ = _adamw_math(w_ref[...], g_, m_ref[...], v_ref[...])
        go_ref[...] = g_
        d_ref[...] = delta
        mo_ref[...] = m_
        vo_ref[...] = v_

    blk = pl.BlockSpec((None, tr, cols), lambda a, i: (a, i, 0))
    gblk = pl.BlockSpec((None, None, tr, cols), lambda a, i: (a, slot, i, 0))
    return pl.pallas_call(
        body, name=name, grid=(l, rows // tr), in_specs=[blk, gblk, blk, blk], out_specs=[blk] * 4,
        out_shape=[jax.ShapeDtypeStruct(w.shape, F32)] * 4,
        compiler_params=_params(2))(w, g, m, v)


SQ = ("w_sgu_out", "w_conv_out", "w_pool_out", "w_out", "w_ple_gate")
SMALL = ("g_mix_pre", "w_sgu_s", "b_sgu_s", "g_sgu_v", "b_sgu_v", "b_dw", "g_conv_ln", "b_conv_ln", "s_pool",
         "g_mix_post", "g_ffn_pre", "g_ffn_post")


WHERE = {"w_in": ("in", 0), "w_ffn_in": ("ffn_in", 0), "w_ffn_out": ("ffn_out", 0), "w_ple": ("mix", 0),
         "w_pool": ("mix", 1), "w_dw": ("dw", 0)}
WHERE.update({nm: ("sq", slot) for slot, nm in enumerate(SQ)})


class _LayerWeights:
    def __init__(self, fetch, small, li):
        self.fetch, self.small, self.li, self.cache = fetch, small, li, {}

    def __getitem__(self, nm):
        if nm not in self.cache:
            self.cache[nm] = self._big(nm) if nm in WHERE else self.small[nm][self.li]
        return self.cache[nm]

    def _big(self, nm):
        group, slot = WHERE[nm]
        g = self.fetch(group)
        g = g.reshape(g.shape[:2] + (-1, g.shape[-1]))
        if nm in ("w_in", "w_ffn_in"):
            return g.reshape(N_CHIPS, D_MODEL, -1)
        if nm == "w_ffn_out":
            return g.reshape(D_FF, D_MODEL)
        if nm in SQ:
            return g[:, slot].reshape(D_MODEL, D_MODEL)
        if nm == "w_ple":
            return g[:, slot].transpose(1, 0, 2).reshape(256, D_MODEL)
        if nm == "w_pool":
            return g[:, slot].reshape(N_CHIPS, 4, 64, 256).transpose(1, 0, 2, 3).reshape(4, 256, 256)
        return g.reshape(N_CHIPS, CONV_HALO, -1).transpose(1, 0, 2).reshape(CONV_HALO, D_MODEL)


def _vec(a):
    return a.reshape(1, -1)


def _layer_fwd(h, p, w, li, hosts=None):
    s = {}
    tag = "_l%d" % li
    s["h0"] = h
    proj, hn = _norm_mm(h, _vec(w["g_mix_pre"]), w["w_in"], "mix_in" + tag, _take(hosts, "mix_in"))
    s["proj"], s["hn"] = proj, hn
    bs3 = w["b_sgu_s"].reshape(SGU_HEADS, SGU_BLOCK, 1)
    s["sg"] = _sgu_fwd(proj, w["w_sgu_s"], bs3, _vec(w["g_sgu_v"]), _vec(w["b_sgu_v"]), "sgu_fwd" + tag,
                       _take(hosts, "sgu_fwd"))
    s["cs"], s["cv"] = _conv_fwd(proj, w["w_dw"], _vec(w["b_dw"]), _vec(w["g_conv_ln"]), _vec(w["b_conv_ln"]),
                                 "conv_fwd" + tag, _take(hosts, "conv_fwd"))
    s["ps"] = _pool_fwd(proj, w["w_pool"], _vec(w["s_pool"]), "pool_fwd" + tag, _take(hosts, "pool_fwd"))
    s["bra"], s["brb"], s["brc"], s["merged"] = _merge_fwd(
        proj, s["sg"], s["cs"], s["ps"], w["w_sgu_out"], w["w_conv_out"], w["w_pool_out"], "merge_fwd" + tag,
        _take(hosts, "merge_fwd"))
    s["mo"], h1 = _mm_norm_res(s["merged"], w["w_out"], _vec(w["g_mix_post"]), h, "mix_out" + tag,
                               _take(hosts, "mix_out"))
    s["h1"] = h1
    s["fg"], s["fu"], s["act"], s["hn2"] = _ffn_in(h1, _vec(w["g_ffn_pre"]), w["w_ffn_in"], "ffn_in" + tag,
                                                   _take(hosts, "ffn_in"))
    s["f"], h2 = _mm_norm_res(s["act"], w["w_ffn_out"], _vec(w["g_ffn_post"]), h1, "ffn_out" + tag,
                              _take(hosts, "ffn_out"))
    s["h2"] = h2
    h3, s["q"], s["e"] = _ple_fwd(h2, p, w["w_ple_gate"], w["w_ple"], "ple_fwd" + tag, _take(hosts, "ple_fwd"))
    return h3, s


def _layer_bwd(dh3, p, w, s, li, hosts=None, big=None, gs=None):
    tag = "_l%d" % li
    d = D_MODEL
    gs = {} if gs is None else gs
    big = {} if big is None else big
    dh2, sq, dw_ple = _ple_bwd(dh3, s["q"], s["e"], w["w_ple_gate"], s["h2"], p, SQ.index("w_ple_gate"), len(SQ),
                               "ple_bwd" + tag)
    dff, gs["g_ffn_post"], dw_ffn_out = _ffn_out_bwd(
        dh2, s["f"], _vec(w["g_ffn_post"]), s["fg"], s["fu"], s["act"], w["w_ffn_out"], "ffn_out_bwd" + tag,
        _take(hosts, "ffn_out_bwd"))
    big["ffn_out"] = dw_ffn_out.reshape(N_CHIPS, 1, D_FF // N_CHIPS, d)
    n_ff = w["w_ffn_in"].shape[2]
    dh1, gs["g_ffn_pre"] = _in_bwd([(dff, 2 * D_FF // n_ff)], w["w_ffn_in"], n_ff, s["h1"], _vec(w["g_ffn_pre"]),
                                   dh2, ROW_TILE, "ffn_in_bwd" + tag, _take(hosts, "ffn_in_bwd"))
    big["ffn_in"] = _dw_cols(s["hn2"], [(dff, 2 * D_FF // n_ff)], n_ff, 1, "dw_ffn_in" + tag)
    branch_w = ("w_sgu_out", "w_conv_out", "w_pool_out")
    dzg, dsg, dcs, dps, gs["g_mix_post"], big["sq"] = _merge_bwd(
        dh1, s["mo"], _vec(w["g_mix_post"]), s["proj"], (s["bra"], s["brb"], s["brc"]), (s["sg"], s["cs"], s["ps"]),
        s["merged"], w["w_out"], [w[nm] for nm in branch_w], sq, [SQ.index(nm) for nm in ("w_out",) + branch_w],
        "merge_bwd" + tag, _take(hosts, "merge_bwd"))
    bs3 = w["b_sgu_s"].reshape(SGU_HEADS, SGU_BLOCK, 1)
    dz_sgu, gs["w_sgu_s"], dbs3, gs["g_sgu_v"], gs["b_sgu_v"] = _sgu_bwd(
        s["proj"], dsg, w["w_sgu_s"], bs3, _vec(w["g_sgu_v"]), _vec(w["b_sgu_v"]), "sgu_bwd" + tag,
        _take(hosts, "sgu_bwd"))
    gs["b_sgu_s"] = dbs3
    dcv, dwdw, gs["b_dw"], gs["g_conv_ln"], gs["b_conv_ln"] = _conv_bwd_norm(
        s["proj"], dcs, s["cv"], _vec(w["b_dw"]), _vec(w["g_conv_ln"]), _vec(w["b_conv_ln"]), "conv_bwd_norm" + tag,
        _take(hosts, "conv_bwd_norm"))
    dz_conv = _conv_bwd_taps(s["proj"], dcv, w["w_dw"], "conv_bwd_taps" + tag, _take(hosts, "conv_bwd_taps"))
    dz_pool, dwpool, gs["s_pool"] = _pool_bwd(s["proj"], dps, w["w_pool"], _vec(w["s_pool"]), "pool_bwd" + tag)
    pieces = [(dz_sgu, 2), (dz_conv, 2), (dz_pool, 1), (dzg, 3)]
    big["in"] = _dw_cols(s["hn"], pieces, d, 2, "dw_in" + tag)
    gple = dw_ple.reshape(256, N_CHIPS, 256).transpose(1, 0, 2)
    gpool = dwpool.reshape(4, N_CHIPS, 64, 256).transpose(1, 0, 2, 3).reshape(N_CHIPS, 256, 256)
    big["mix"] = jnp.stack([gple, gpool], axis=1)
    big["dw"] = dwdw.reshape(CONV_HALO, N_CHIPS, 256).transpose(1, 0, 2)[:, None]
    dh0, gs["g_mix_pre"] = _in_bwd(pieces, w["w_in"], d, s["h0"], _vec(w["g_mix_pre"]), dh1, ROW_TILE,
                                   "mix_in_bwd" + tag, _take(hosts, "mix_in_bwd"))
    return dh0, big, gs


GROUPS = ("in", "sq", "ffn_in", "ffn_out", "mix", "dw")
WIRE_DTYPE = {"in": BF16, "sq": BF16, "ffn_in": BF16, "ffn_out": BF16, "mix": BF16, "dw": F32}
GATHER_FIRST = ("in", "mix", "dw")
GATHER_RIDES = (("mix_in", "sgu_fwd", ("sq", "ffn_in"), ()),
                ("conv_fwd", "pool_fwd", ("ffn_out",), ("in",)),
                ("merge_fwd", "mix_out", (), ("sq",)),
                ("ffn_in", "ffn_out", (), ("ffn_in", "ffn_out", "mix", "dw")))
REDUCE_UPPER = ("ffn_out_bwd", (("ffn_in_bwd", ("in", "ffn_out")), ("merge_bwd", ("sq", "ffn_in", "mix", "dw"))))
REDUCE_OWN = ("sgu_bwd", (("conv_bwd_norm", ("ffn_in", "ffn_out")), ("conv_bwd_taps", ("sq",))))
REDUCE_LAST = ("in", "mix", "dw")


def _group_members(wts):
    n_layers = wts["w_in"].shape[0]
    dw = wts["w_dw"].reshape(n_layers, CONV_WIDTH, -1)
    return {"in": [wts["w_in"]], "sq": [wts[nm] for nm in SQ], "ffn_in": [wts["w_ffn_in"]],
            "ffn_out": [wts["w_ffn_out"]],
            "mix": [wts["w_ple"], wts["w_pool"].reshape(n_layers, POOL_GROUP, POOL_GROUP)],
            "dw": [jnp.pad(dw, ((0, 0), (0, CONV_HALO - CONV_WIDTH), (0, 0)))]}


class _Gather:
    PLACED, OVER_ICI, FULL = 0, 1, 2

    def __init__(self):
        self.buf, self.stage, self.pending = {}, {}, []

    def put(self, key, buf):
        self.buf[key], self.stage[key] = buf, self.PLACED

    def _flush(self):
        for keys, pay, stage in self.pending:
            if pay.results is not None:
                for key, res in zip(keys, pay.results):
                    self.buf[key], self.stage[key] = res, stage
        self.pending = [entry for entry in self.pending if entry[1].results is None]

    def _factory(self, make, keys, before, after):
        def factory():
            if not keys:
                return None
            self._flush()
            assert all(self.stage[k] == before for k in keys), (keys, self.stage)
            pay = make([self.buf[k] for k in keys])
            self.pending.append((keys, pay, after))
            return pay
        return factory

    def ici(self, keys):
        return self._factory(_gather_ici, keys, self.PLACED, self.OVER_ICI)

    def d2d(self, keys):
        return self._factory(_gather_d2d, keys, self.OVER_ICI, self.FULL)

    def get(self, li, group):
        self._flush()
        assert self.stage[(li, group)] == self.FULL, (li, group)
        return self.buf[(li, group)]


class _Reduce:
    def __init__(self, pos, n_layers):
        self.pos, self.n_layers, self.exchanged, self.stages = pos, n_layers, [], []

    def exchange(self, li, groups, grads):
        def factory():
            pay = _pair_exchange([_half_view(grads[g]) for g in groups])
            self.exchanged.append((li, list(groups), pay))
            return pay
        return factory

    def _received(self, li, group):
        for lj, groups, pay in self.exchanged:
            if lj == li and group in groups:
                return pay.results[groups.index(group)]
        raise KeyError((li, group))

    def chips(self, li, groups, grads):
        def factory():
            parts = [_pair_sum(_half_view(grads[g]), self._received(li, g), self.pos, WIRE_DTYPE[g],
                               "pair_sum_%s_l%d" % (g, li)) for g in groups]
            pay = _chip_exchange(parts)
            self.stages.append((li, groups, parts, pay))
            return pay
        return factory

    def finish(self):
        reduced = {}
        for li, groups, parts, pay in self.stages:
            for g, part, landed in zip(groups, parts, pay.results):
                reduced[g] = _chip_sum(part, landed, reduced.get(g), li, self.n_layers, self.pos,
                                       "chip_sum_%s_l%d" % (g, li))
        return reduced


def _pack_small(tree):
    flat = jnp.concatenate([tree[nm].reshape(-1).astype(F32) for nm in SMALL])
    return flat.reshape(-1, 128)


def _unpack_small(packed, like):
    out, off = {}, 0
    flat = packed.reshape(-1)
    for nm in SMALL:
        n = like[nm].size
        out[nm] = flat[off:off + n].reshape(like[nm].shape)
        off += n
    return out


WEIGHTS = ("g_mix_pre", "w_in", "w_sgu_s", "b_sgu_s", "g_sgu_v", "b_sgu_v", "w_sgu_out", "w_dw", "b_dw", "g_conv_ln",
           "b_conv_ln", "w_conv_out", "w_pool", "s_pool", "w_pool_out", "w_out", "g_mix_post", "g_ffn_pre",
           "w_ffn_in", "w_ffn_out", "g_ffn_post", "w_ple", "w_ple_gate")


def kernel(x, p, g_mix_pre, w_in, w_sgu_s, b_sgu_s, g_sgu_v, b_sgu_v, w_sgu_out, w_dw, b_dw, g_conv_ln, b_conv_ln, w_conv_out, w_pool, s_pool, w_pool_out, w_out, g_mix_post, g_ffn_pre, w_ffn_in, w_ffn_out, g_ffn_post, w_ple, w_ple_gate, loss_target, m_g_mix_pre, m_w_in, m_w_sgu_s, m_b_sgu_s, m_g_sgu_v, m_b_sgu_v, m_w_sgu_out, m_w_dw, m_b_dw, m_g_conv_ln, m_b_conv_ln, m_w_conv_out, m_w_pool, m_s_pool, m_w_pool_out, m_w_out, m_g_mix_post, m_g_ffn_pre, m_w_ffn_in, m_w_ffn_out, m_g_ffn_post, m_w_ple, m_w_ple_gate, v_g_mix_pre, v_w_in, v_w_sgu_s, v_b_sgu_s, v_g_sgu_v, v_b_sgu_v, v_w_sgu_out, v_w_dw, v_b_dw, v_g_conv_ln, v_b_conv_ln, v_w_conv_out, v_w_pool, v_s_pool, v_w_pool_out, v_w_out, v_g_mix_post, v_g_ffn_pre, v_w_ffn_in, v_w_ffn_out, v_g_ffn_post, v_w_ple, v_w_ple_gate):
    args = dict(locals())
    wts = {nm: args[nm] for nm in WEIGHTS}
    mom = {nm: args["m_" + nm] for nm in WEIGHTS}
    var = {nm: args["v_" + nm] for nm in WEIGHTS}
    n_layers = w_in.shape[0]
    h = x.reshape(x.shape[1:])
    target = loss_target.reshape(loss_target.shape[1:])
    cx, cy, core = lax.axis_index("x"), lax.axis_index("y"), lax.axis_index("c")
    pos = jnp.stack([2 * cx + cy, 2 * (1 - cx) + cy, 2 * cx + (1 - cy), 2 * (1 - cx) + (1 - cy), core,
                     4 * cx + 2 * cy + core])
    pos = pos.astype(jnp.int32)

    members = _group_members(wts)
    gather = _Gather()
    for li in range(n_layers):
        for g in GROUPS:
            gather.put((li, g), _place(members[g], li, pos, WIRE_DTYPE[g], "place_%s_l%d" % (g, li)))
    first = [(0, g) for g in GATHER_FIRST]
    _run_payload(gather.ici(first)(), "gather_ici_first")
    _run_payload(gather.d2d(first)(), "gather_d2d_first")

    saved, layer_w = [], []
    for li in range(n_layers):
        hosts = {}
        for ici_host, d2d_host, own, nxt in GATHER_RIDES:
            keys = [(li, g) for g in own if li == 0] + [(li + 1, g) for g in nxt if li + 1 < n_layers]
            hosts[ici_host], hosts[d2d_host] = gather.ici(keys), gather.d2d(keys)
        w = _LayerWeights(functools.partial(gather.get, li), wts, li)
        layer_w.append(w)
        h, s = _layer_fwd(h, p[li, 0], w, li, hosts)
        saved.append(s)
    dh, sq_err = _loss_head(h, target, "loss_head")
    loss = lax.psum(sq_err[0, 0] * (0.5 / D_MODEL), ("x", "y", "c"))

    reduce = _Reduce(pos, n_layers)
    small_grads = [{} for _ in range(n_layers)]
    late = (0, SMALL[0])
    small = {}

    def small_vec():
        def leaf(li, nm):
            shape = wts[nm].shape[1:]
            return jnp.zeros(shape, F32) if (li, nm) == late else small_grads[li][nm].reshape(shape)
        return _pack_small({nm: jnp.stack([leaf(li, nm) for li in range(n_layers)], axis=0) for nm in SMALL})

    upper = None
    for li in reversed(range(n_layers)):
        own = {}
        hosts = {}
        plans = [(REDUCE_UPPER, li + 1, upper)] if upper is not None else []
        if li == 0:
            plans.append((REDUCE_OWN, 0, own))

            def last_rides(own=own):
                _run_payload(reduce.exchange(0, REDUCE_LAST, own)(), "pair_exchange_last")
                small["vec"] = small_vec()
                small["exchange"] = _small_exchange(small["vec"])
                return _join(reduce.chips(0, REDUCE_LAST, own)(), small["exchange"])
            hosts["mix_in_bwd"] = last_rides
        for (pair_host, chip_hosts), lj, grads in plans:
            groups = [g for _, gs_ in chip_hosts for g in gs_]
            hosts[pair_host] = reduce.exchange(lj, groups, grads)
            for chip_host, gs_ in chip_hosts:
                hosts[chip_host] = reduce.chips(lj, gs_, grads)
        dh, upper, _ = _layer_bwd(dh, p[li, 0], layer_w[li], saved[li], li, hosts, own, small_grads[li])
    grad_x = dh[None]
    reduced = reduce.finish()

    shared = _run_payload(_pair_share([reduced[g] for g in GROUPS]), "pair_share")
    red = {g: b.reshape(b.shape[:2] + (-1, b.shape[-1])) for g, b in zip(GROUPS, shared)}

    where = {"w_in": ("in", 0), "w_ffn_in": ("ffn_in", 0), "w_ffn_out": ("ffn_out", 0), "w_ple": ("mix", 0),
             "w_pool": ("mix", 1)}
    for slot, nm in enumerate(SQ):
        where[nm] = ("sq", slot)
    outs = {}
    for nm, (g, slot) in where.items():
        shape = wts[nm].shape
        to3 = lambda a: a.reshape((n_layers,) + red[g].shape[2:])
        res = _adamw(to3(wts[nm]), red[g], slot, to3(mom[nm]), to3(var[nm]), "adamw_" + nm)
        outs[nm] = [r.reshape(shape) for r in res]
    gdw = red["dw"][:, :, :CONV_WIDTH]
    to3 = lambda a: a.reshape(n_layers, CONV_WIDTH, -1)
    res = _adamw(to3(wts["w_dw"]), gdw, 0, to3(mom["w_dw"]), to3(var["w_dw"]), "adamw_w_dw")
    outs["w_dw"] = [r.reshape(wts["w_dw"].shape) for r in res]

    gmain = _small_sum(small["vec"], small["exchange"].results[0], pos, "small_sum")
    glate = _all_reduce_small(small_grads[late[0]][late[1]].reshape(-1, 128), "all_reduce_late")
    gsmall = jnp.concatenate([glate, gmain[glate.shape[0]:]], axis=0)
    pk = lambda tree: _pack_small({nm: tree[nm] for nm in SMALL})[None]
    res = _adamw(pk(wts), gsmall[None, None], 0, pk(mom), pk(var), "adamw_small")
    unpacked = [_unpack_small(r[0], wts) for r in res]
    for nm in SMALL:
        outs[nm] = [u[nm] for u in unpacked]

    result = [loss, grad_x]
    for k in range(4):
        result += [outs[nm][k] for nm in WEIGHTS]
    return tuple(result)
```
